```python
import jax, jax.numpy as jnp
from jax import lax
import numpy as np

D_MODEL = 2048
BATCH = 8
SEQ = 2048
DEPTH = 2

HEAD_DIM = 128
ATTN_WIDTH = D_MODEL // 2
N_Q_HEADS = ATTN_WIDTH // HEAD_DIM
N_KV_HEADS = max(1, N_Q_HEADS // 4)
GQA_GROUP = N_Q_HEADS // N_KV_HEADS
KV_WIDTH = N_KV_HEADS * HEAD_DIM
WINDOW = 128
BLOCK = 128
GMLP_WIDTH = D_MODEL - ATTN_WIDTH
GMLP_HEAD_DIM = 128
N_GMLP_HEADS = GMLP_WIDTH // GMLP_HEAD_DIM
CHUNK = 128
IN_WIDTH = ATTN_WIDTH + 2 * KV_WIDTH + 2 * GMLP_WIDTH
D_FF = 5632
CONV_WIDTH = 3
ROPE_THETA = 10000.0
EPS = 1e-6
MASK_VALUE = -1e30

kernel_name = "hybrid_window_gqa_sgu_convffn_encoder"


def rms_norm(x, g):
    xf = x.astype(jnp.float32)
    y = xf * lax.rsqrt(jnp.mean(xf * xf, axis=-1, keepdims=True) + EPS)
    return (y * g.astype(jnp.float32)).astype(x.dtype)


def layer_norm(x, g, b):
    xf = x.astype(jnp.float32)
    mu = jnp.mean(xf, axis=-1, keepdims=True)
    xc = xf - mu
    y = xc * lax.rsqrt(jnp.mean(xc * xc, axis=-1, keepdims=True) + EPS)
    return (y * g.astype(jnp.float32) + b.astype(jnp.float32)).astype(x.dtype)


def rope_tables(seq):
    inv_freq = ROPE_THETA ** (-jnp.arange(0, HEAD_DIM, 2, dtype=jnp.float32) / HEAD_DIM)
    ang = jnp.arange(seq, dtype=jnp.float32)[:, None] * inv_freq[None, :]
    return jnp.cos(ang), jnp.sin(ang)


def apply_rope(x, cos, sin):
    xf = x.astype(jnp.float32)
    x1, x2 = jnp.split(xf, 2, axis=-1)
    c = cos[None, :, None, :]
    s = sin[None, :, None, :]
    return jnp.concatenate([x1 * c - x2 * s, x2 * c + x1 * s], axis=-1).astype(x.dtype)


def banded_window_attention(q, k, v, sink):
    B, S, _, D = q.shape
    nb = S // BLOCK
    qb = q.reshape(B, nb, BLOCK, N_KV_HEADS, GQA_GROUP, D)

    def band(t):
        tp = jnp.pad(t, ((0, 0), (BLOCK, BLOCK), (0, 0), (0, 0)))
        tp = tp.reshape(B, nb + 2, BLOCK, N_KV_HEADS, D)
        return jnp.concatenate([tp[:, :-2], tp[:, 1:-1], tp[:, 2:]], axis=2)

    kb, vb = band(k), band(v)
    s = jnp.einsum('bnqhgd,bnkhd->bnhgqk', qb, kb).astype(jnp.float32) * (D ** -0.5)
    blk = jnp.arange(nb)[:, None, None]
    q_pos = blk * BLOCK + jnp.arange(BLOCK)[None, :, None]
    k_pos = blk * BLOCK - BLOCK + jnp.arange(3 * BLOCK)[None, None, :]
    valid = (jnp.abs(k_pos - q_pos) <= WINDOW) & (k_pos >= 0) & (k_pos < S)
    s = jnp.where(valid[None, :, None, None], s, MASK_VALUE)
    sk = sink.astype(jnp.float32).reshape(N_KV_HEADS, GQA_GROUP)[None, None, :, :, None, None]
    m = jnp.maximum(jnp.max(s, axis=-1, keepdims=True), sk)
    p = jnp.exp(s - m)
    probs = p / (jnp.sum(p, axis=-1, keepdims=True) + jnp.exp(sk - m))
    out = jnp.einsum('bnhgqk,bnkhd->bnqhgd', probs.astype(v.dtype), vb)
    return out.reshape(B, S, N_Q_HEADS * D)


def chunked_spatial_gating(u, v, ln_g, ln_b, w_s, b_s):
    B, S, _ = u.shape
    nc = S // CHUNK
    vn = layer_norm(v, ln_g, ln_b).reshape(B, nc, CHUNK, N_GMLP_HEADS, GMLP_HEAD_DIM)
    f = jnp.einsum('hpq,bcqhd->bcphd', w_s, vn) + b_s.T[None, None, :, :, None]
    return u * f.reshape(B, S, GMLP_WIDTH)


def depthwise_conv_centred(h, w, b):
    S = h.shape[1]
    half = CONV_WIDTH // 2
    hp = jnp.pad(h, ((0, 0), (half, half), (0, 0)))
    out = b
    for t in range(CONV_WIDTH):
        out = out + hp[:, t:t + S] * w[t]
    return out


def conv_gated_ffn(h, w_up, conv_w, conv_b, w_down):
    a = depthwise_conv_centred(h @ w_up, conv_w, conv_b)
    g, u = jnp.split(a, 2, axis=-1)
    return (jax.nn.silu(g) * u) @ w_down


def _fwd_setup_inputs(seed: int = 0) -> dict:
    key = jax.random.key(seed)
    ks = jax.random.split(key, 20)
    f32 = jnp.float32
    nrm = lambda k, shape, scale: jax.random.normal(k, shape, f32) * scale
    res_scale = (2.0 * DEPTH) ** -0.5
    return {
        "x": jax.random.normal(ks[0], (BATCH, SEQ, D_MODEL), f32),
        "norm1_g": 1.0 + nrm(ks[1], (DEPTH, D_MODEL), 0.02),
        "w_in": nrm(ks[2], (DEPTH, D_MODEL, IN_WIDTH), D_MODEL ** -0.5),
        "q_norm_g": 1.0 + nrm(ks[3], (DEPTH, HEAD_DIM), 0.02),
        "k_norm_g": 1.0 + nrm(ks[4], (DEPTH, HEAD_DIM), 0.02),
        "sink": nrm(ks[5], (DEPTH, N_Q_HEADS), 0.5),
        "sgu_ln_g": 1.0 + nrm(ks[6], (DEPTH, GMLP_WIDTH), 0.02),
        "sgu_ln_b": nrm(ks[7], (DEPTH, GMLP_WIDTH), 0.02),
        "w_s": nrm(ks[8], (DEPTH, N_GMLP_HEADS, CHUNK, CHUNK), 0.5 * CHUNK ** -0.5),
        "b_s": 1.0 + nrm(ks[9], (DEPTH, N_GMLP_HEADS, CHUNK), 0.02),
        "attn_out_g": 1.0 + nrm(ks[10], (DEPTH, ATTN_WIDTH), 0.02),
        "sgu_out_g": 1.0 + nrm(ks[11], (DEPTH, GMLP_WIDTH), 0.02),
        "w_o": nrm(ks[12], (DEPTH, D_MODEL, D_MODEL), D_MODEL ** -0.5 * res_scale),
        "norm2_g": 1.0 + nrm(ks[13], (DEPTH, D_MODEL), 0.02),
        "w_up": nrm(ks[14], (DEPTH, D_MODEL, 2 * D_FF), D_MODEL ** -0.5),
        "conv_w": nrm(ks[15], (DEPTH, CONV_WIDTH, 2 * D_FF), CONV_WIDTH ** -0.5),
        "conv_b": nrm(ks[16], (DEPTH, 2 * D_FF), 0.01),
        "w_down": nrm(ks[17], (DEPTH, D_FF, D_MODEL), D_FF ** -0.5 * res_scale),
    }


def _fwd_reference(x, norm1_g, w_in, q_norm_g, k_norm_g, sink, sgu_ln_g, sgu_ln_b, w_s, b_s,
              attn_out_g, sgu_out_g, w_o, norm2_g, w_up, conv_w, conv_b, w_down):
    B, S, _ = x.shape
    cos, sin = rope_tables(S)
    splits = [ATTN_WIDTH, ATTN_WIDTH + KV_WIDTH, ATTN_WIDTH + 2 * KV_WIDTH,
              ATTN_WIDTH + 2 * KV_WIDTH + GMLP_WIDTH]
    for l in range(DEPTH):
        h = rms_norm(x, norm1_g[l])
        q, k, v, gu, gv = jnp.split(h @ w_in[l], splits, axis=-1)
        q = apply_rope(rms_norm(q.reshape(B, S, N_Q_HEADS, HEAD_DIM), q_norm_g[l]), cos, sin)
        k = apply_rope(rms_norm(k.reshape(B, S, N_KV_HEADS, HEAD_DIM), k_norm_g[l]), cos, sin)
        v = v.reshape(B, S, N_KV_HEADS, HEAD_DIM)
        attn = banded_window_attention(q, k, v, sink[l])
        sgu = chunked_spatial_gating(jax.nn.gelu(gu), jax.nn.gelu(gv),
                                     sgu_ln_g[l], sgu_ln_b[l], w_s[l], b_s[l])
        mixed = jnp.concatenate([rms_norm(attn, attn_out_g[l]), rms_norm(sgu, sgu_out_g[l])], axis=-1)
        x = x + mixed @ w_o[l]
        x = x + conv_gated_ffn(rms_norm(x, norm2_g[l]), w_up[l], conv_w[l], conv_b[l], w_down[l])
    return x


import jax as _jax
import jax.numpy as _jnp

TWIN_FORMAT = 'train_step'
FWD_PARAMS = ['x', 'norm1_g', 'w_in', 'q_norm_g', 'k_norm_g', 'sink', 'sgu_ln_g', 'sgu_ln_b', 'w_s', 'b_s', 'attn_out_g', 'sgu_out_g', 'w_o', 'norm2_g', 'w_up', 'conv_w', 'conv_b', 'w_down']
TWIN_WEIGHTS = ['norm1_g', 'w_in', 'q_norm_g', 'k_norm_g', 'sink', 'sgu_ln_g', 'sgu_ln_b', 'w_s', 'b_s', 'attn_out_g', 'sgu_out_g', 'w_o', 'norm2_g', 'w_up', 'conv_w', 'conv_b', 'w_down']
TWIN_DIFF_INPUT = 'x'
TWIN_INPUTS = ['x', 'norm1_g', 'w_in', 'q_norm_g', 'k_norm_g', 'sink', 'sgu_ln_g', 'sgu_ln_b', 'w_s', 'b_s', 'attn_out_g', 'sgu_out_g', 'w_o', 'norm2_g', 'w_up', 'conv_w', 'conv_b', 'w_down', 'loss_target', 'm_norm1_g', 'm_w_in', 'm_q_norm_g', 'm_k_norm_g', 'm_sink', 'm_sgu_ln_g', 'm_sgu_ln_b', 'm_w_s', 'm_b_s', 'm_attn_out_g', 'm_sgu_out_g', 'm_w_o', 'm_norm2_g', 'm_w_up', 'm_conv_w', 'm_conv_b', 'm_w_down', 'v_norm1_g', 'v_w_in', 'v_q_norm_g', 'v_k_norm_g', 'v_sink', 'v_sgu_ln_g', 'v_sgu_ln_b', 'v_w_s', 'v_b_s', 'v_attn_out_g', 'v_sgu_out_g', 'v_w_o', 'v_norm2_g', 'v_w_up', 'v_conv_w', 'v_conv_b', 'v_w_down']
TWIN_OUTPUTS = ['loss', 'grad_x', 'grad_norm1_g', 'grad_w_in', 'grad_q_norm_g', 'grad_k_norm_g', 'grad_sink', 'grad_sgu_ln_g', 'grad_sgu_ln_b', 'grad_w_s', 'grad_b_s', 'grad_attn_out_g', 'grad_sgu_out_g', 'grad_w_o', 'grad_norm2_g', 'grad_w_up', 'grad_conv_w', 'grad_conv_b', 'grad_w_down', 'delta_norm1_g', 'delta_w_in', 'delta_q_norm_g', 'delta_k_norm_g', 'delta_sink', 'delta_sgu_ln_g', 'delta_sgu_ln_b', 'delta_w_s', 'delta_b_s', 'delta_attn_out_g', 'delta_sgu_out_g', 'delta_w_o', 'delta_norm2_g', 'delta_w_up', 'delta_conv_w', 'delta_conv_b', 'delta_w_down', 'new_m_norm1_g', 'new_m_w_in', 'new_m_q_norm_g', 'new_m_k_norm_g', 'new_m_sink', 'new_m_sgu_ln_g', 'new_m_sgu_ln_b', 'new_m_w_s', 'new_m_b_s', 'new_m_attn_out_g', 'new_m_sgu_out_g', 'new_m_w_o', 'new_m_norm2_g', 'new_m_w_up', 'new_m_conv_w', 'new_m_conv_b', 'new_m_w_down', 'new_v_norm1_g', 'new_v_w_in', 'new_v_q_norm_g', 'new_v_k_norm_g', 'new_v_sink', 'new_v_sgu_ln_g', 'new_v_sgu_ln_b', 'new_v_w_s', 'new_v_b_s', 'new_v_attn_out_g', 'new_v_sgu_out_g', 'new_v_w_o', 'new_v_norm2_g', 'new_v_w_up', 'new_v_conv_w', 'new_v_conv_b', 'new_v_w_down']
TWIN_LEAF_KINDS = {'loss': 'loss', 'grad_x': 'grad_x', 'grad_norm1_g': 'grad_w', 'grad_w_in': 'grad_w', 'grad_q_norm_g': 'grad_w', 'grad_k_norm_g': 'grad_w', 'grad_sink': 'grad_w', 'grad_sgu_ln_g': 'grad_w', 'grad_sgu_ln_b': 'grad_w', 'grad_w_s': 'grad_w', 'grad_b_s': 'grad_w', 'grad_attn_out_g': 'grad_w', 'grad_sgu_out_g': 'grad_w', 'grad_w_o': 'grad_w', 'grad_norm2_g': 'grad_w', 'grad_w_up': 'grad_w', 'grad_conv_w': 'grad_w', 'grad_conv_b': 'grad_w', 'grad_w_down': 'grad_w', 'delta_norm1_g': 'delta_w', 'delta_w_in': 'delta_w', 'delta_q_norm_g': 'delta_w', 'delta_k_norm_g': 'delta_w', 'delta_sink': 'delta_w', 'delta_sgu_ln_g': 'delta_w', 'delta_sgu_ln_b': 'delta_w', 'delta_w_s': 'delta_w', 'delta_b_s': 'delta_w', 'delta_attn_out_g': 'delta_w', 'delta_sgu_out_g': 'delta_w', 'delta_w_o': 'delta_w', 'delta_norm2_g': 'delta_w', 'delta_w_up': 'delta_w', 'delta_conv_w': 'delta_w', 'delta_conv_b': 'delta_w', 'delta_w_down': 'delta_w', 'new_m_norm1_g': 'new_m', 'new_m_w_in': 'new_m', 'new_m_q_norm_g': 'new_m', 'new_m_k_norm_g': 'new_m', 'new_m_sink': 'new_m', 'new_m_sgu_ln_g': 'new_m', 'new_m_sgu_ln_b': 'new_m', 'new_m_w_s': 'new_m', 'new_m_b_s': 'new_m', 'new_m_attn_out_g': 'new_m', 'new_m_sgu_out_g': 'new_m', 'new_m_w_o': 'new_m', 'new_m_norm2_g': 'new_m', 'new_m_w_up': 'new_m', 'new_m_conv_w': 'new_m', 'new_m_conv_b': 'new_m', 'new_m_w_down': 'new_m', 'new_v_norm1_g': 'new_v', 'new_v_w_in': 'new_v', 'new_v_q_norm_g': 'new_v', 'new_v_k_norm_g': 'new_v', 'new_v_sink': 'new_v', 'new_v_sgu_ln_g': 'new_v', 'new_v_sgu_ln_b': 'new_v', 'new_v_w_s': 'new_v', 'new_v_b_s': 'new_v', 'new_v_attn_out_g': 'new_v', 'new_v_sgu_out_g': 'new_v', 'new_v_w_o': 'new_v', 'new_v_norm2_g': 'new_v', 'new_v_w_up': 'new_v', 'new_v_conv_w': 'new_v', 'new_v_conv_b': 'new_v', 'new_v_w_down': 'new_v'}


def _forward(args):
    return _fwd_reference(*[args[k] for k in FWD_PARAMS])


def _output_shape():
    out = _jax.eval_shape(lambda: _forward(_fwd_setup_inputs(0)))
    return out.shape, out.dtype

N_MICROBATCH = 1
ADAM_LR = 0.001
ADAM_B1 = 0.9
ADAM_B2 = 0.999
ADAM_EPS = 1e-08
ADAM_WD = 0.01
ADAM_STEP = 10
PER_EXAMPLE_BATCH_AXIS = {'x': 0, 'loss_target': 0}
SHARED_INPUTS = []
_WEIGHT_DTYPES = {'norm1_g': _jnp.float32, 'w_in': _jnp.float32, 'q_norm_g': _jnp.float32, 'k_norm_g': _jnp.float32, 'sink': _jnp.float32, 'sgu_ln_g': _jnp.float32, 'sgu_ln_b': _jnp.float32, 'w_s': _jnp.float32, 'b_s': _jnp.float32, 'attn_out_g': _jnp.float32, 'sgu_out_g': _jnp.float32, 'w_o': _jnp.float32, 'norm2_g': _jnp.float32, 'w_up': _jnp.float32, 'conv_w': _jnp.float32, 'conv_b': _jnp.float32, 'w_down': _jnp.float32}
MOMENT_SCALE = {'norm1_g': 3.285000e-01, 'w_in': 2.514534e-01, 'q_norm_g': 1.658793e-01, 'k_norm_g': 1.693184e-01, 'sink': 1.195201e-02, 'sgu_ln_g': 3.076417e-02, 'sgu_ln_b': 3.248407e-02, 'w_s': 5.607222e-02, 'b_s': 6.064867e-02, 'attn_out_g': 2.000592e+00, 'sgu_out_g': 2.104192e+00, 'w_o': 1.093452e+00, 'norm2_g': 1.621325e+00, 'w_up': 6.250870e-02, 'conv_w': 2.401901e-01, 'conv_b': 2.674085e-01, 'w_down': 1.332605e-01}


def _to_microbatches(a, axis):
    t = _jnp.moveaxis(a, axis, 0)
    t = t.reshape((N_MICROBATCH, t.shape[0] // N_MICROBATCH) + t.shape[1:])
    return _jnp.moveaxis(t, 1, axis + 1)


def setup_inputs(seed: int = 0) -> dict:
    inp = _fwd_setup_inputs(seed)
    key = _jax.random.fold_in(_jax.random.key(seed), 7919)
    shape, _ = _output_shape()
    out = dict(inp)
    out["loss_target"] = _jax.random.normal(_jax.random.fold_in(key, 0), shape, _jnp.float32)
    for i, name in enumerate(TWIN_WEIGHTS):
        w = inp[name].astype(_jnp.float32)
        if MOMENT_SCALE is None:
            s = _jnp.sqrt(_jnp.mean(_jnp.square(w)) + 1e-30)
        else:
            s = MOMENT_SCALE[name]
        km, kv = _jax.random.split(_jax.random.fold_in(key, i + 1))
        out[name] = w
        out["m_" + name] = s * _jax.random.normal(km, w.shape, _jnp.float32)
        out["v_" + name] = (s * s) * _jax.random.uniform(kv, w.shape, _jnp.float32, 0.5, 1.5)
    if N_MICROBATCH > 1:
        for name, axis in PER_EXAMPLE_BATCH_AXIS.items():
            out[name] = _to_microbatches(out[name], axis)
    return {'x': out['x'], 'norm1_g': out['norm1_g'], 'w_in': out['w_in'], 'q_norm_g': out['q_norm_g'], 'k_norm_g': out['k_norm_g'], 'sink': out['sink'], 'sgu_ln_g': out['sgu_ln_g'], 'sgu_ln_b': out['sgu_ln_b'], 'w_s': out['w_s'], 'b_s': out['b_s'], 'attn_out_g': out['attn_out_g'], 'sgu_out_g': out['sgu_out_g'], 'w_o': out['w_o'], 'norm2_g': out['norm2_g'], 'w_up': out['w_up'], 'conv_w': out['conv_w'], 'conv_b': out['conv_b'], 'w_down': out['w_down'], 'loss_target': out['loss_target'], 'm_norm1_g': out['m_norm1_g'], 'm_w_in': out['m_w_in'], 'm_q_norm_g': out['m_q_norm_g'], 'm_k_norm_g': out['m_k_norm_g'], 'm_sink': out['m_sink'], 'm_sgu_ln_g': out['m_sgu_ln_g'], 'm_sgu_ln_b': out['m_sgu_ln_b'], 'm_w_s': out['m_w_s'], 'm_b_s': out['m_b_s'], 'm_attn_out_g': out['m_attn_out_g'], 'm_sgu_out_g': out['m_sgu_out_g'], 'm_w_o': out['m_w_o'], 'm_norm2_g': out['m_norm2_g'], 'm_w_up': out['m_w_up'], 'm_conv_w': out['m_conv_w'], 'm_conv_b': out['m_conv_b'], 'm_w_down': out['m_w_down'], 'v_norm1_g': out['v_norm1_g'], 'v_w_in': out['v_w_in'], 'v_q_norm_g': out['v_q_norm_g'], 'v_k_norm_g': out['v_k_norm_g'], 'v_sink': out['v_sink'], 'v_sgu_ln_g': out['v_sgu_ln_g'], 'v_sgu_ln_b': out['v_sgu_ln_b'], 'v_w_s': out['v_w_s'], 'v_b_s': out['v_b_s'], 'v_attn_out_g': out['v_attn_out_g'], 'v_sgu_out_g': out['v_sgu_out_g'], 'v_w_o': out['v_w_o'], 'v_norm2_g': out['v_norm2_g'], 'v_w_up': out['v_w_up'], 'v_conv_w': out['v_conv_w'], 'v_conv_b': out['v_conv_b'], 'v_w_down': out['v_w_down']}


def _loss(weights, diff, rest, loss_target):
    with _jax.named_scope("forward"):
        args = {**rest, TWIN_DIFF_INPUT: diff, **{k: w.astype(_WEIGHT_DTYPES[k]) for k, w in weights.items()}}
        y = _forward(args)
    with _jax.named_scope("loss_head"):
        err = _jnp.square(y.astype(_jnp.float32) - loss_target)
        return 0.5 * _jnp.sum(_jnp.mean(err, axis=-1)) if err.ndim else 0.5 * err


def _adamw(w, g, m, v):
    m = ADAM_B1 * m + (1.0 - ADAM_B1) * g
    v = ADAM_B2 * v + (1.0 - ADAM_B2) * _jnp.square(g)
    m_hat = m / (1.0 - ADAM_B1 ** ADAM_STEP)
    v_hat = v / (1.0 - ADAM_B2 ** ADAM_STEP)
    delta = -ADAM_LR * (m_hat / (_jnp.sqrt(v_hat) + ADAM_EPS) + ADAM_WD * w)
    return delta, m, v


def reference(x, norm1_g, w_in, q_norm_g, k_norm_g, sink, sgu_ln_g, sgu_ln_b, w_s, b_s, attn_out_g, sgu_out_g, w_o, norm2_g, w_up, conv_w, conv_b, w_down, loss_target, m_norm1_g, m_w_in, m_q_norm_g, m_k_norm_g, m_sink, m_sgu_ln_g, m_sgu_ln_b, m_w_s, m_b_s, m_attn_out_g, m_sgu_out_g, m_w_o, m_norm2_g, m_w_up, m_conv_w, m_conv_b, m_w_down, v_norm1_g, v_w_in, v_q_norm_g, v_k_norm_g, v_sink, v_sgu_ln_g, v_sgu_ln_b, v_w_s, v_b_s, v_attn_out_g, v_sgu_out_g, v_w_o, v_norm2_g, v_w_up, v_conv_w, v_conv_b, v_w_down):
    given = dict(x=x, norm1_g=norm1_g, w_in=w_in, q_norm_g=q_norm_g, k_norm_g=k_norm_g, sink=sink, sgu_ln_g=sgu_ln_g, sgu_ln_b=sgu_ln_b, w_s=w_s, b_s=b_s, attn_out_g=attn_out_g, sgu_out_g=sgu_out_g, w_o=w_o, norm2_g=norm2_g, w_up=w_up, conv_w=conv_w, conv_b=conv_b, w_down=w_down, loss_target=loss_target, m_norm1_g=m_norm1_g, m_w_in=m_w_in, m_q_norm_g=m_q_norm_g, m_k_norm_g=m_k_norm_g, m_sink=m_sink, m_sgu_ln_g=m_sgu_ln_g, m_sgu_ln_b=m_sgu_ln_b, m_w_s=m_w_s, m_b_s=m_b_s, m_attn_out_g=m_attn_out_g, m_sgu_out_g=m_sgu_out_g, m_w_o=m_w_o, m_norm2_g=m_norm2_g, m_w_up=m_w_up, m_conv_w=m_conv_w, m_conv_b=m_conv_b, m_w_down=m_w_down, v_norm1_g=v_norm1_g, v_w_in=v_w_in, v_q_norm_g=v_q_norm_g, v_k_norm_g=v_k_norm_g, v_sink=v_sink, v_sgu_ln_g=v_sgu_ln_g, v_sgu_ln_b=v_sgu_ln_b, v_w_s=v_w_s, v_b_s=v_b_s, v_attn_out_g=v_attn_out_g, v_sgu_out_g=v_sgu_out_g, v_w_o=v_w_o, v_norm2_g=v_norm2_g, v_w_up=v_w_up, v_conv_w=v_conv_w, v_conv_b=v_conv_b, v_w_down=v_w_down)
    weights = {n: given[n] for n in TWIN_WEIGHTS}
    shared = {n: given[n] for n in SHARED_INPUTS}
    per_example = {n: given[n] for n in ['x']}
    grad_fn = _jax.value_and_grad(_loss, argnums=(0, 1))

    def one_microbatch(ex, loss_target):
        ex = dict(ex)
        diff = ex.pop(TWIN_DIFF_INPUT)
        return grad_fn(weights, diff, {**shared, **ex}, loss_target)

    if N_MICROBATCH == 1:
        loss, (grad_w, grad_x) = one_microbatch(per_example, given["loss_target"])
    else:
        def body(carry, xs):
            loss_sum, grad_sum = carry
            l_k, (gw_k, gx_k) = one_microbatch(xs[0], xs[1])
            with _jax.named_scope("update"):
                return (loss_sum + l_k, _jax.tree.map(_jnp.add, grad_sum, gw_k)), gx_k

        init = (_jnp.zeros((), _jnp.float32), _jax.tree.map(_jnp.zeros_like, weights))
        (loss, grad_w), grad_x = _jax.lax.scan(body, init, (per_example, given["loss_target"]))
    with _jax.named_scope("update"):
        delta_w, new_m, new_v = {}, {}, {}
        for n in TWIN_WEIGHTS:
            delta_w[n], new_m[n], new_v[n] = _adamw(weights[n], grad_w[n], given["m_" + n], given["v_" + n])
    return (loss, grad_x, *[grad_w[n] for n in TWIN_WEIGHTS], *[delta_w[n] for n in TWIN_WEIGHTS],
            *[new_m[n] for n in TWIN_WEIGHTS], *[new_v[n] for n in TWIN_WEIGHTS])
```

```python
import jax
import jax.numpy as jnp
from jax import lax
from jax.experimental import pallas as pl
from jax.experimental.pallas import tpu as pltpu

F32 = jnp.float32
BF16 = jnp.bfloat16
MESH = pl.DeviceIdType.MESH

N_DEV = 8
HEAD = 128
EPS = 1e-6
MASK_VALUE = -1e30
ROPE_THETA = 10000.0
GELU_C = 0.7978845608028654
GELU_A = 0.044715

ADAM_LR = 0.001
ADAM_B1 = 0.9
ADAM_B2 = 0.999
ADAM_EPS = 1e-08
ADAM_WD = 0.01
ADAM_STEP = 10

VMEM_LIMIT = 56 * 1024 * 1024


def _tile(n, pref, mult):
    best = None
    for t in range(mult, min(n, pref) + 1, mult):
        if n % t == 0:
            best = t
    return n if best is None else best


def _params(sem=None):
    kw = dict(vmem_limit_bytes=VMEM_LIMIT)
    if sem is not None:
        kw["dimension_semantics"] = sem
    return pltpu.CompilerParams(**kw)


def _gelu(x):
    return x * (0.5 * (1.0 + jnp.tanh(GELU_C * (x + GELU_A * (x * x * x)))))


def _gelu_grad(x):
    t = jnp.tanh(GELU_C * (x + GELU_A * (x * x * x)))
    return 0.5 * (1.0 + t) + 0.5 * x * (1.0 - t * t) * (GELU_C * (1.0 + 3.0 * GELU_A * (x * x)))


def _sigmoid(x):
    return 1.0 / (1.0 + jnp.exp(-x))


def _vec_spec(l, n):
    return pl.BlockSpec((None, 1, n), lambda *_: (l, 0, 0))


def _mm(name, a, b, *, M, N, K, tm, tn, tk, a_spec, b_spec, ta=False, tb=False, out_dtype=F32, res=None):
    nm, nn, nk = M // tm, N // tn, K // tk
    assert nm * tm == M and nn * tn == N and nk * tk == K
    assert not (ta and nk > 1)
    dims = (((1,), (1,)), ((), ())) if tb else (((1,), (0,)), ((), ()))

    def body(*refs):
        refs = list(refs)
        a_ref = refs.pop(0)
        b_ref = refs.pop(0)
        r_ref = refs.pop(0) if res is not None else None
        o_ref = refs.pop(0)
        acc = refs.pop(0) if nk > 1 else None
        at = refs.pop(0) if ta else None
        k = pl.program_id(2)
        if ta:
            @pl.when(pl.program_id(1) == 0)
            def _():
                at[...] = a_ref[...].T
            lhs = at[...]
        else:
            lhs = a_ref[...]
        p = lax.dot_general(lhs, b_ref[...], dims, preferred_element_type=F32)

        def finish(r):
            if r_ref is not None:
                r = r_ref[...] + r
            o_ref[...] = r.astype(out_dtype)

        if nk == 1:
            finish(p)
        else:
            @pl.when(k == 0)
            def _():
                acc[...] = p

            @pl.when(k > 0)
            def _():
                acc[...] += p

            @pl.when(k == nk - 1)
            def _():
                finish(acc[...])

    in_specs = [a_spec, b_spec]
    args = [a, b]
    if res is not None:
        in_specs.append(pl.BlockSpec((tm, tn), lambda i, j, k: (i, j)))
        args.append(res)
    scratch = []
    if nk > 1:
        scratch.append(pltpu.VMEM((tm, tn), F32))
    if ta:
        scratch.append(pltpu.VMEM((tm, tk), BF16))
    return pl.pallas_call(
        body, name=name, grid=(nm, nn, nk),
        in_specs=in_specs,
        out_specs=pl.BlockSpec((tm, tn), lambda i, j, k: (i, j)),
        out_shape=jax.ShapeDtypeStruct((M, N), out_dtype),
        scratch_shapes=scratch,
        compiler_params=_params(("arbitrary", "arbitrary", "arbitrary")),
    )(*args)


def _rms_fwd(name, x, g3, l):
    S, D = x.shape
    tr = _tile(S, 256, 16)

    def body(x_ref, g_ref, h_ref):
        xv = x_ref[...]
        r = lax.rsqrt(jnp.mean(xv * xv, axis=-1, keepdims=True) + EPS)
        h_ref[...] = ((xv * r) * g_ref[...]).astype(BF16)

    return pl.pallas_call(
        body, name=name, grid=(S // tr,),
        in_specs=[pl.BlockSpec((tr, D), lambda i: (i, 0)), _vec_spec(l, D)],
        out_specs=pl.BlockSpec((tr, D), lambda i: (i, 0)),
        out_shape=jax.ShapeDtypeStruct((S, D), BF16),
        compiler_params=_params(("arbitrary",)),
    )(x, g3)


def _rope(t, cos2, sin2):
    return t * cos2 + pltpu.roll(t, HEAD // 2, axis=1) * sin2


def _qkv_prep(name, z, qg3, kg3, cos2, sin2, l, cfg):
    S, AW, KVW, NQ, NKV = cfg["S"], cfg["AW"], cfg["KVW"], cfg["NQ"], cfg["NKV"]
    tr = _tile(S, 256, 16)
    kv_blk = (3 * AW) // (2 * KVW)

    def body(zq_ref, zkv_ref, qg_ref, kg_ref, c_ref, s_ref, q_ref, k_ref, v_ref):
        cosv, sinv = c_ref[...], s_ref[...]

        def norm_rope(t, g):
            r = lax.rsqrt(jnp.mean(t * t, axis=-1, keepdims=True) + EPS)
            return _rope((t * r) * g, cosv, sinv)

        for h in range(NQ):
            sl = slice(h * HEAD, (h + 1) * HEAD)
            q_ref[:, sl] = norm_rope(zq_ref[:, sl], qg_ref[...]).astype(BF16)
        for h in range(NKV):
            sl = slice(h * HEAD, (h + 1) * HEAD)
            k_ref[:, sl] = norm_rope(zkv_ref[:, sl], kg_ref[...]).astype(BF16)
        v_ref[...] = zkv_ref[:, KVW:].astype(BF16)

    return pl.pallas_call(
        body, name=name, grid=(S // tr,),
        in_specs=[pl.BlockSpec((tr, AW), lambda i: (i, 0)),
                  pl.BlockSpec((tr, 2 * KVW), lambda i: (i, kv_blk)),
                  _vec_spec(l, HEAD), _vec_spec(l, HEAD),
                  pl.BlockSpec((tr, HEAD), lambda i: (i, 0)),
                  pl.BlockSpec((tr, HEAD), lambda i: (i, 0))],
        out_specs=[pl.BlockSpec((tr, AW), lambda i: (i, 0)),
                   pl.BlockSpec((tr, KVW), lambda i: (i, 0)),
                   pl.BlockSpec((tr, KVW), lambda i: (i, 0))],
        out_shape=[jax.ShapeDtypeStruct((S, AW), BF16),
                   jax.ShapeDtypeStruct((S, KVW), BF16),
                   jax.ShapeDtypeStruct((S, KVW), BF16)],
        compiler_params=_params(("arbitrary",)),
    )(z, z, qg3, kg3, cos2, sin2)


def _band_specs(width, nb):
    return [pl.BlockSpec((HEAD, width), lambda n: (jnp.maximum(n - 1, 0), 0)),
            pl.BlockSpec((HEAD, width), lambda n: (n, 0)),
            pl.BlockSpec((HEAD, width), lambda n: (jnp.minimum(n + 1, nb - 1), 0))]


def _attn_probs(qs, kj, n, sink_of_row, S, G):
    s = lax.dot_general(qs, kj, (((1,), (1,)), ((), ())), preferred_element_type=F32) * (HEAD ** -0.5)
    rows = lax.broadcasted_iota(jnp.int32, (G * HEAD, 3 * HEAD), 0)
    cols = lax.broadcasted_iota(jnp.int32, (G * HEAD, 3 * HEAD), 1)
    qi = rows & (HEAD - 1)
    kpos = n * HEAD - HEAD + cols
    valid = (cols >= qi) & (cols <= qi + 2 * HEAD) & (kpos >= 0) & (kpos < S)
    s = jnp.where(valid, s, MASK_VALUE)
    m = jnp.maximum(jnp.max(s, axis=-1, keepdims=True), sink_of_row)
    p = jnp.exp(s - m)
    e_sink = jnp.exp(sink_of_row - m)
    inv = 1.0 / (jnp.sum(p, axis=-1, keepdims=True) + e_sink)
    return p * inv, e_sink * inv


def _sink_rows(sink_ref, l, j, G):
    hidx = lax.broadcasted_iota(jnp.int32, (G * HEAD, 1), 0) // HEAD
    col = jnp.full((G * HEAD, 1), sink_ref[l, j * G], F32)
    for g in range(1, G):
        col = jnp.where(hidx == g, sink_ref[l, j * G + g], col)
    return col


def _attn_fwd(name, q, k, v, sink, ag3, l, cfg):
    S, AW, KVW, NKV, G = cfg["S"], cfg["AW"], cfg["KVW"], cfg["NKV"], cfg["G"]
    nb = S // HEAD

    def body(q_ref, kp, kc, kn, vp, vc, vn, sink_ref, ag_ref, a_ref, mix_ref):
        n = pl.program_id(0)
        kb = jnp.concatenate([kp[...], kc[...], kn[...]], axis=0)
        vb = jnp.concatenate([vp[...], vc[...], vn[...]], axis=0)
        for j in range(NKV):
            sl = slice(j * HEAD, (j + 1) * HEAD)
            qs = jnp.concatenate([q_ref[:, (j * G + g) * HEAD:(j * G + g + 1) * HEAD] for g in range(G)], axis=0)
            probs, _ = _attn_probs(qs, kb[:, sl], n, _sink_rows(sink_ref, l, j, G), S, G)
            o = jnp.dot(probs.astype(BF16), vb[:, sl], preferred_element_type=F32)
            for g in range(G):
                a_ref[:, (j * G + g) * HEAD:(j * G + g + 1) * HEAD] = o[g * HEAD:(g + 1) * HEAD]
        a = a_ref[...]
        r = lax.rsqrt(jnp.mean(a * a, axis=-1, keepdims=True) + EPS)
        mix_ref[...] = ((a * r) * ag_ref[...]).astype(BF16)

    return pl.pallas_call(
        body, name=name, grid=(nb,),
        in_specs=[pl.BlockSpec((HEAD, AW), lambda n: (n, 0))] + _band_specs(KVW, nb) + _band_specs(KVW, nb)
                 + [pl.BlockSpec(memory_space=pltpu.SMEM), _vec_spec(l, AW)],
        out_specs=[pl.BlockSpec((HEAD, AW), lambda n: (n, 0)), pl.BlockSpec((HEAD, AW), lambda n: (n, 0))],
        out_shape=[jax.ShapeDtypeStruct((S, AW), F32), jax.ShapeDtypeStruct((S, AW), BF16)],
        compiler_params=_params(("arbitrary",)),
    )(q, k, k, k, v, v, v, sink, ag3)


def _sgu_forward_math(gu, gv, lng, lnb):
    u = _gelu(gu)
    vv = _gelu(gv)
    mu = jnp.mean(vv, axis=-1, keepdims=True)
    xc = vv - mu
    rstd = lax.rsqrt(jnp.mean(xc * xc, axis=-1, keepdims=True) + EPS)
    xhat = xc * rstd
    vn = xhat * lng + lnb
    return u, xhat, rstd, vn


def _sgu_fwd(name, z, lng3, lnb3, ws_b, bs_b, og3, l, cfg):
    S, GW, NG = cfg["S"], cfg["GW"], cfg["NG"]

    def body(gu_ref, gv_ref, lng_ref, lnb_ref, ws_ref, bs_ref, og_ref, mix_ref, sg_ref):
        u, _, _, vn = _sgu_forward_math(gu_ref[...], gv_ref[...], lng_ref[...], lnb_ref[...])
        vnb = vn.astype(BF16)
        for h in range(NG):
            sl = slice(h * HEAD, (h + 1) * HEAD)
            f = jnp.dot(ws_ref[h], vnb[:, sl], preferred_element_type=F32) + bs_ref[h]
            sg_ref[:, sl] = u[:, sl] * f
        sg = sg_ref[...]
        r = lax.rsqrt(jnp.mean(sg * sg, axis=-1, keepdims=True) + EPS)
        mix_ref[...] = ((sg * r) * og_ref[...]).astype(BF16)

    return pl.pallas_call(
        body, name=name, grid=(S // HEAD,),
        in_specs=[pl.BlockSpec((HEAD, GW), lambda c: (c, 1)),
                  pl.BlockSpec((HEAD, GW), lambda c: (c, 2)),
                  _vec_spec(l, GW), _vec_spec(l, GW),
                  pl.BlockSpec((None, NG, HEAD, HEAD), lambda c: (l, 0, 0, 0)),
                  pl.BlockSpec((None, NG, HEAD, HEAD), lambda c: (l, 0, 0, 0)),
                  _vec_spec(l, GW)],
        out_specs=pl.BlockSpec((HEAD, GW), lambda c: (c, 0)),
        out_shape=jax.ShapeDtypeStruct((S, GW), BF16),
        scratch_shapes=[pltpu.VMEM((HEAD, GW), F32)],
        compiler_params=_params(("arbitrary",)),
    )(z, z, lng3, lnb3, ws_b, bs_b, og3)


def _conv3(t, w, b, S):
    row = lax.broadcasted_iota(jnp.int32, t.shape, 0)
    dn = jnp.where(row == 0, 0.0, pltpu.roll(t, 1, axis=0))
    up = jnp.where(row == S - 1, 0.0, pltpu.roll(t, S - 1, axis=0))
    return ((b + dn * w[0:1]) + t * w[1:2]) + up * w[2:3], dn, up


def _conv_glu_fwd(name, ap, cw, cb3, l, cfg):
    S, F = cfg["S"], cfg["F"]
    tc = _tile(F, 256, 128)
    nf = F // tc

    def body(g_ref, u_ref, wg_ref, wu_ref, bg_ref, bu_ref, y_ref):
        ag, _, _ = _conv3(g_ref[...], wg_ref[...], bg_ref[...], S)
        au, _, _ = _conv3(u_ref[...], wu_ref[...], bu_ref[...], S)
        y_ref[...] = ((ag * _sigmoid(ag)) * au).astype(BF16)

    return pl.pallas_call(
        body, name=name, grid=(nf,),
        in_specs=[pl.BlockSpec((S, tc), lambda j: (0, j)),
                  pl.BlockSpec((S, tc), lambda j: (0, j + nf)),
                  pl.BlockSpec((None, 3, tc), lambda j: (l, 0, j)),
                  pl.BlockSpec((None, 3, tc), lambda j: (l, 0, j + nf)),
                  pl.BlockSpec((None, 1, tc), lambda j: (l, 0, j)),
                  pl.BlockSpec((None, 1, tc), lambda j: (l, 0, j + nf))],
        out_specs=pl.BlockSpec((S, tc), lambda j: (0, j)),
        out_shape=jax.ShapeDtypeStruct((S, F), BF16),
        compiler_params=_params(("arbitrary",)),
    )(ap, ap, cw, cw, cb3, cb3)


def _loss_bwd(name, y, target):
    S, D = y.shape
    tr = _tile(S, 256, 16)

    def body(y_ref, t_ref, loss_ref, d_ref, db_ref):
        @pl.when(pl.program_id(0) == 0)
        def _():
            loss_ref[...] = jnp.zeros_like(loss_ref)

        err = y_ref[...] - t_ref[...]
        part = 0.5 * jnp.sum(jnp.mean(err * err, axis=-1, keepdims=True), axis=0, keepdims=True)
        loss_ref[...] += jnp.broadcast_to(part, loss_ref.shape)
        d = err * (1.0 / D)
        d_ref[...] = d
        db_ref[...] = d.astype(BF16)

    return pl.pallas_call(
        body, name=name, grid=(S // tr,),
        in_specs=[pl.BlockSpec((tr, D), lambda i: (i, 0)), pl.BlockSpec((tr, D), lambda i: (i, 0))],
        out_specs=[pl.BlockSpec((8, 128), lambda i: (0, 0)),
                   pl.BlockSpec((tr, D), lambda i: (i, 0)),
                   pl.BlockSpec((tr, D), lambda i: (i, 0))],
        out_shape=[jax.ShapeDtypeStruct((8, 128), F32),
                   jax.ShapeDtypeStruct((S, D), F32),
                   jax.ShapeDtypeStruct((S, D), BF16)],
        compiler_params=_params(("arbitrary",)),
    )(y, target)


def _rms_bwd(name, x, dh, dres, g3, l):
    S, D = x.shape
    tr = _tile(S, 256, 16)

    def body(x_ref, dh_ref, dr_ref, g_ref, dx_ref, dxb_ref, dg_ref):
        @pl.when(pl.program_id(0) == 0)
        def _():
            dg_ref[...] = jnp.zeros_like(dg_ref)

        xv = x_ref[...]
        dhv = dh_ref[...]
        r = lax.rsqrt(jnp.mean(xv * xv, axis=-1, keepdims=True) + EPS)
        xhat = xv * r
        dhg = dhv * g_ref[...]
        dx = dr_ref[...] + r * (dhg - xhat * jnp.mean(dhg * xhat, axis=-1, keepdims=True))
        dx_ref[...] = dx
        dxb_ref[...] = dx.astype(BF16)
        dg_ref[...] += jnp.sum(dhv * xhat, axis=0, keepdims=True)

    return pl.pallas_call(
        body, name=name, grid=(S // tr,),
        in_specs=[pl.BlockSpec((tr, D), lambda i: (i, 0)), pl.BlockSpec((tr, D), lambda i: (i, 0)),
                  pl.BlockSpec((tr, D), lambda i: (i, 0)), _vec_spec(l, D)],
        out_specs=[pl.BlockSpec((tr, D), lambda i: (i, 0)), pl.BlockSpec((tr, D), lambda i: (i, 0)),
                   pl.BlockSpec((1, D), lambda i: (0, 0))],
        out_shape=[jax.ShapeDtypeStruct((S, D), F32), jax.ShapeDtypeStruct((S, D), BF16),
                   jax.ShapeDtypeStruct((1, D), F32)],
        compiler_params=_params(("arbitrary",)),
    )(x, dh, dres, g3)


def _glu_conv_bwd(name, dy, ap, cw, cb3, l, cfg):
    S, F = cfg["S"], cfg["F"]
    tc = _tile(F, 256, 128)
    nf = F // tc

    def body(dy_ref, g_ref, u_ref, wg_ref, wu_ref, bg_ref, bu_ref, dap_ref, dw_ref, db_ref):
        apg, apu = g_ref[...], u_ref[...]
        wg, wu = wg_ref[...], wu_ref[...]
        ag, g_dn, g_up = _conv3(apg, wg, bg_ref[...], S)
        au, u_dn, u_up = _conv3(apu, wu, bu_ref[...], S)
        sig = _sigmoid(ag)
        dyv = dy_ref[...]
        da_u = dyv * (ag * sig)
        da_g = (dyv * au) * (sig * (1.0 + ag * (1.0 - sig)))
        row = lax.broadcasted_iota(jnp.int32, (S, tc), 0)

        def back(da, w):
            nxt = jnp.where(row == S - 1, 0.0, pltpu.roll(da, S - 1, axis=0))
            prv = jnp.where(row == 0, 0.0, pltpu.roll(da, 1, axis=0))
            return (nxt * w[0:1] + da * w[1:2]) + prv * w[2:3]

        dap_ref[0] = back(da_g, wg).astype(BF16)
        dap_ref[1] = back(da_u, wu).astype(BF16)

        def wgrad(da, dn, t, up):
            return jnp.concatenate([jnp.sum(da * dn, axis=0, keepdims=True),
                                    jnp.sum(da * t, axis=0, keepdims=True),
                                    jnp.sum(da * up, axis=0, keepdims=True)], axis=0)

        dw_ref[0] = wgrad(da_g, g_dn, apg, g_up)
        dw_ref[1] = wgrad(da_u, u_dn, apu, u_up)
        db_ref[0] = jnp.sum(da_g, axis=0, keepdims=True)
        db_ref[1] = jnp.sum(da_u, axis=0, keepdims=True)

    return pl.pallas_call(
        body, name=name, grid=(nf,),
        in_specs=[pl.BlockSpec((S, tc), lambda j: (0, j)),
                  pl.BlockSpec((S, tc), lambda j: (0, j)),
                  pl.BlockSpec((S, tc), lambda j: (0, j + nf)),
                  pl.BlockSpec((None, 3, tc), lambda j: (l, 0, j)),
                  pl.BlockSpec((None, 3, tc), lambda j: (l, 0, j + nf)),
                  pl.BlockSpec((None, 1, tc), lambda j: (l, 0, j)),
                  pl.BlockSpec((None, 1, tc), lambda j: (l, 0, j + nf))],
        out_specs=[pl.BlockSpec((2, S, tc), lambda j: (0, 0, j)),
                   pl.BlockSpec((2, 3, tc), lambda j: (0, 0, j)),
                   pl.BlockSpec((2, 1, tc), lambda j: (0, 0, j))],
        out_shape=[jax.ShapeDtypeStruct((2, S, F), BF16),
                   jax.ShapeDtypeStruct((2, 3, F), F32),
                   jax.ShapeDtypeStruct((2, 1, F), F32)],
        compiler_params=_params(("arbitrary",)),
    )(dy, ap, ap, cw, cw, cb3, cb3)


def _attn_bwd(name, q, k, v, attn, dmix, sink, ag3, l, cfg):
    S, AW, KVW, NQ, NKV, G = cfg["S"], cfg["AW"], cfg["KVW"], cfg["NQ"], cfg["NKV"], cfg["G"]
    nb = S // HEAD
    scale = HEAD ** -0.5

    def body(q_ref, kp, kc, kn, vp, vc, vn, a_ref, dm_ref, sink_ref, ag_ref,
             dq_ref, dk_ref, dv_ref, dsink_ref, dag_ref, da_scr):
        n = pl.program_id(0)

        @pl.when(n == 0)
        def _():
            dk_ref[...] = jnp.zeros_like(dk_ref)
            dv_ref[...] = jnp.zeros_like(dv_ref)
            dsink_ref[...] = jnp.zeros_like(dsink_ref)
            dag_ref[...] = jnp.zeros_like(dag_ref)

        a = a_ref[...]
        dm = dm_ref[...]
        r = lax.rsqrt(jnp.mean(a * a, axis=-1, keepdims=True) + EPS)
        xhat = a * r
        dmg = dm * ag_ref[...]
        da_scr[...] = r * (dmg - xhat * jnp.mean(dmg * xhat, axis=-1, keepdims=True))
        dag_ref[...] += jnp.sum(dm * xhat, axis=0, keepdims=True)

        kb = jnp.concatenate([kp[...], kc[...], kn[...]], axis=0)
        vb = jnp.concatenate([vp[...], vc[...], vn[...]], axis=0)
        band = pl.ds(pl.multiple_of(n * HEAD, HEAD), 3 * HEAD)
        for j in range(NKV):
            sl = slice(j * HEAD, (j + 1) * HEAD)
            heads = [slice((j * G + g) * HEAD, (j * G + g + 1) * HEAD) for g in range(G)]
            qs = jnp.concatenate([q_ref[:, hs] for hs in heads], axis=0)
            do = jnp.concatenate([da_scr[:, hs] for hs in heads], axis=0)
            kj, vj = kb[:, sl], vb[:, sl]
            probs, p_sink = _attn_probs(qs, kj, n, _sink_rows(sink_ref, l, j, G), S, G)
            dob = do.astype(BF16)
            dprobs = lax.dot_general(dob, vj, (((1,), (1,)), ((), ())), preferred_element_type=F32)
            delta = jnp.sum(dprobs * probs, axis=-1, keepdims=True)
            ds = (probs * (dprobs - delta)) * scale
            dsb = ds.astype(BF16)
            dsk = -(p_sink * delta)
            dq = jnp.dot(dsb, kj, preferred_element_type=F32)
            for g in range(G):
                dq_ref[:, heads[g]] = dq[g * HEAD:(g + 1) * HEAD]
                part = jnp.sum(dsk[g * HEAD:(g + 1) * HEAD], axis=0, keepdims=True)
                dsink_ref[j * G + g:j * G + g + 1, :] += jnp.broadcast_to(part, (1, HEAD))
            dk_ref[band, sl] += lax.dot_general(dsb, qs, (((0,), (0,)), ((), ())), preferred_element_type=F32)
            dv_ref[band, sl] += lax.dot_general(probs.astype(BF16), dob, (((0,), (0,)), ((), ())),
                                                preferred_element_type=F32)

    return pl.pallas_call(
        body, name=name, grid=(nb,),
        in_specs=[pl.BlockSpec((HEAD, AW), lambda n: (n, 0))] + _band_specs(KVW, nb) + _band_specs(KVW, nb)
                 + [pl.BlockSpec((HEAD, AW), lambda n: (n, 0)),
                    pl.BlockSpec((HEAD, AW), lambda n: (n, 0)),
                    pl.BlockSpec(memory_space=pltpu.SMEM), _vec_spec(l, AW)],
        out_specs=[pl.BlockSpec((HEAD, AW), lambda n: (n, 0)),
                   pl.BlockSpec((S + 2 * HEAD, KVW), lambda n: (0, 0)),
                   pl.BlockSpec((S + 2 * HEAD, KVW), lambda n: (0, 0)),
                   pl.BlockSpec((NQ, HEAD), lambda n: (0, 0)),
                   pl.BlockSpec((1, AW), lambda n: (0, 0))],
        out_shape=[jax.ShapeDtypeStruct((S, AW), F32),
                   jax.ShapeDtypeStruct((S + 2 * HEAD, KVW), F32),
                   jax.ShapeDtypeStruct((S + 2 * HEAD, KVW), F32),
                   jax.ShapeDtypeStruct((NQ, HEAD), F32),
                   jax.ShapeDtypeStruct((1, AW), F32)],
        scratch_shapes=[pltpu.VMEM((HEAD, AW), F32)],
        compiler_params=_params(("arbitrary",)),
    )(q, k, k, k, v, v, v, attn, dmix, sink, ag3)


def _qkv_prep_bwd(name, z, dq, dk_pad, dv_pad, qg3, kg3, cos2, sin2, l, cfg):
    S, AW, KVW, NQ, NKV = cfg["S"], cfg["AW"], cfg["KVW"], cfg["NQ"], cfg["NKV"]
    kv_blk = (3 * AW) // (2 * KVW)

    def body(zq_ref, zkv_ref, dq_ref, dk_ref, dv_ref, qg_ref, kg_ref, c_ref, s_ref,
             dzq_ref, dzkv_ref, dqg_ref, dkg_ref):
        @pl.when(pl.program_id(0) == 0)
        def _():
            dqg_ref[...] = jnp.zeros_like(dqg_ref)
            dkg_ref[...] = jnp.zeros_like(dkg_ref)

        cosv, sinv = c_ref[...], s_ref[...]

        def back(t, dr, g):
            r = lax.rsqrt(jnp.mean(t * t, axis=-1, keepdims=True) + EPS)
            xhat = t * r
            dn = dr * cosv + pltpu.roll(dr * sinv, HEAD // 2, axis=1)
            dxh = dn * g
            dt = r * (dxh - xhat * jnp.mean(dxh * xhat, axis=-1, keepdims=True))
            return dt, jnp.sum(dn * xhat, axis=0, keepdims=True)

        gq = jnp.zeros((1, HEAD), F32)
        for h in range(NQ):
            sl = slice(h * HEAD, (h + 1) * HEAD)
            dt, gpart = back(zq_ref[:, sl], dq_ref[:, sl], qg_ref[...])
            dzq_ref[:, sl] = dt.astype(BF16)
            gq = gq + gpart
        dqg_ref[...] += gq
        gk = jnp.zeros((1, HEAD), F32)
        for h in range(NKV):
            sl = slice(h * HEAD, (h + 1) * HEAD)
            dt, gpart = back(zkv_ref[:, sl], dk_ref[:, sl], kg_ref[...])
            dzkv_ref[:, sl] = dt.astype(BF16)
            gk = gk + gpart
        dkg_ref[...] += gk
        dzkv_ref[:, KVW:] = dv_ref[...].astype(BF16)

    return pl.pallas_call(
        body, name=name, grid=(S // HEAD,),
        in_specs=[pl.BlockSpec((HEAD, AW), lambda i: (i, 0)),
                  pl.BlockSpec((HEAD, 2 * KVW), lambda i: (i, kv_blk)),
                  pl.BlockSpec((HEAD, AW), lambda i: (i, 0)),
                  pl.BlockSpec((HEAD, KVW), lambda i: (i + 1, 0)),
                  pl.BlockSpec((HEAD, KVW), lambda i: (i + 1, 0)),
                  _vec_spec(l, HEAD), _vec_spec(l, HEAD),
                  pl.BlockSpec((HEAD, HEAD), lambda i: (i, 0)),
                  pl.BlockSpec((HEAD, HEAD), lambda i: (i, 0))],
        out_specs=[pl.BlockSpec((HEAD, AW), lambda i: (i, 0)),
                   pl.BlockSpec((HEAD, 2 * KVW), lambda i: (i, 0)),
                   pl.BlockSpec((1, HEAD), lambda i: (0, 0)),
                   pl.BlockSpec((1, HEAD), lambda i: (0, 0))],
        out_shape=[jax.ShapeDtypeStruct((S, AW), BF16),
                   jax.ShapeDtypeStruct((S, 2 * KVW), BF16),
                   jax.ShapeDtypeStruct((1, HEAD), F32),
                   jax.ShapeDtypeStruct((1, HEAD), F32)],
        compiler_params=_params(("arbitrary",)),
    )(z, z, dq, dk_pad, dv_pad, qg3, kg3, cos2, sin2)


def _sgu_bwd(name, z, dmix, lng3, lnb3, ws_b, wst_b, bs_b, og3, l, cfg):
    S, GW, NG = cfg["S"], cfg["GW"], cfg["NG"]

    def body(gu_ref, gv_ref, dm_ref, lng_ref, lnb_ref, ws_ref, wst_ref, bs_ref, og_ref,
             dgu_ref, dgv_ref, dws_ref, dbs_ref, dlng_ref, dlnb_ref, dog_ref, sg_scr, f_scr, dvn_scr):
        @pl.when(pl.program_id(0) == 0)
        def _():
            dws_ref[...] = jnp.zeros_like(dws_ref)
            dbs_ref[...] = jnp.zeros_like(dbs_ref)
            dlng_ref[...] = jnp.zeros_like(dlng_ref)
            dlnb_ref[...] = jnp.zeros_like(dlnb_ref)
            dog_ref[...] = jnp.zeros_like(dog_ref)

        gu, gv = gu_ref[...], gv_ref[...]
        lng = lng_ref[...]
        u, xhat, rstd, vn = _sgu_forward_math(gu, gv, lng, lnb_ref[...])
        vnb = vn.astype(BF16)
        for h in range(NG):
            sl = slice(h * HEAD, (h + 1) * HEAD)
            f = jnp.dot(ws_ref[h], vnb[:, sl], preferred_element_type=F32) + bs_ref[h]
            f_scr[:, sl] = f
            sg_scr[:, sl] = u[:, sl] * f
        sg = sg_scr[...]
        dm = dm_ref[...]
        r = lax.rsqrt(jnp.mean(sg * sg, axis=-1, keepdims=True) + EPS)
        sghat = sg * r
        dmg = dm * og_ref[...]
        dsg = r * (dmg - sghat * jnp.mean(dmg * sghat, axis=-1, keepdims=True))
        dog_ref[...] += jnp.sum(dm * sghat, axis=0, keepdims=True)
        du = dsg * f_scr[...]
        df = dsg * u
        dfb = df.astype(BF16)
        for h in range(NG):
            sl = slice(h * HEAD, (h + 1) * HEAD)
            dvn_scr[:, sl] = jnp.dot(wst_ref[h], dfb[:, sl], preferred_element_type=F32)
            dws_ref[h] += lax.dot_general(dfb[:, sl], vnb[:, sl], (((1,), (1,)), ((), ())),
                                          preferred_element_type=F32)
            dbs_ref[h] += jnp.broadcast_to(jnp.sum(df[:, sl], axis=-1, keepdims=True), (HEAD, HEAD))
        dvn = dvn_scr[...]
        dlng_ref[...] += jnp.sum(dvn * xhat, axis=0, keepdims=True)
        dlnb_ref[...] += jnp.sum(dvn, axis=0, keepdims=True)
        dxh = dvn * lng
        dvv = rstd * ((dxh - jnp.mean(dxh, axis=-1, keepdims=True))
                      - xhat * jnp.mean(dxh * xhat, axis=-1, keepdims=True))
        dgu_ref[...] = (du * _gelu_grad(gu)).astype(BF16)
        dgv_ref[...] = (dvv * _gelu_grad(gv)).astype(BF16)

    vec = pl.BlockSpec((1, GW), lambda c: (0, 0))
    mat = pl.BlockSpec((NG, HEAD, HEAD), lambda c: (0, 0, 0))
    wsp = pl.BlockSpec((None, NG, HEAD, HEAD), lambda c: (l, 0, 0, 0))
    return pl.pallas_call(
        body, name=name, grid=(S // HEAD,),
        in_specs=[pl.BlockSpec((HEAD, GW), lambda c: (c, 1)),
                  pl.BlockSpec((HEAD, GW), lambda c: (c, 2)),
                  pl.BlockSpec((HEAD, GW), lambda c: (c, 1)),
                  _vec_spec(l, GW), _vec_spec(l, GW), wsp, wsp, wsp, _vec_spec(l, GW)],
        out_specs=[pl.BlockSpec((HEAD, GW), lambda c: (c, 0)), pl.BlockSpec((HEAD, GW), lambda c: (c, 0)),
                   mat, mat, vec, vec, vec],
        out_shape=[jax.ShapeDtypeStruct((S, GW), BF16), jax.ShapeDtypeStruct((S, GW), BF16),
                   jax.ShapeDtypeStruct((NG, HEAD, HEAD), F32), jax.ShapeDtypeStruct((NG, HEAD, HEAD), F32),
                   jax.ShapeDtypeStruct((1, GW), F32), jax.ShapeDtypeStruct((1, GW), F32),
                   jax.ShapeDtypeStruct((1, GW), F32)],
        scratch_shapes=[pltpu.VMEM((HEAD, GW), F32), pltpu.VMEM((HEAD, GW), F32), pltpu.VMEM((HEAD, GW), F32)],
        compiler_params=_params(("arbitrary",)),
    )(z, z, dmix, lng3, lnb3, ws_b, wst_b, bs_b, og3)


def _mesh_pos():
    x, y, c = lax.axis_index("x"), lax.axis_index("y"), lax.axis_index("c")
    return x, y, c


def _dev_index(p):
    return 4 * p[0] + 2 * p[1] + p[2]


def _all_gather(shards, out_shapes, slabs):
    na = len(shards)

    def body(*refs):
        ins, outs = refs[:na], refs[na:2 * na]
        send_sems, recv_sems, local_sems = refs[2 * na:]
        x, y, c = _mesh_pos()
        me, sib = (x, y, c), (x, y, 1 - c)
        chips = [(1 - x, y), (x, 1 - y), (1 - x, 1 - y)]

        def copy(a, k, block, to, src=None):
            dst = slabs[a](outs[a], _dev_index(block))
            return pltpu.make_async_remote_copy(
                src_ref=dst if src is None else src, dst_ref=dst,
                send_sem=send_sems.at[a, k], recv_sem=recv_sems.at[a, k],
                device_id=to, device_id_type=MESH)

        local = [pltpu.make_async_copy(ins[a], slabs[a](outs[a], _dev_index(me)), local_sems.at[a])
                 for a in range(na)]
        for cp in local:
            cp.start()
        first = []
        for a in range(na):
            first.append(copy(a, 0, me, sib, src=ins[a]))
            first += [copy(a, 1 + j, me, (*chip, c), src=ins[a]) for j, chip in enumerate(chips)]
        for cp in first:
            cp.start()
        passed = []
        for j, chip in enumerate(chips):
            for a in range(na):
                copy(a, 1 + j, (*chip, c), me).wait_recv()
                fwd = copy(a, 4 + j, (*chip, c), sib)
                fwd.start()
                passed.append(fwd)
        for a in range(na):
            copy(a, 0, sib, me).wait_recv()
            for j, chip in enumerate(chips):
                copy(a, 4 + j, (*chip, 1 - c), me).wait_recv()
        for cp in first + passed:
            cp.wait_send()
        for cp in local:
            cp.wait()

    any_spec = pl.BlockSpec(memory_space=pl.ANY)
    return pl.pallas_call(
        body, name="ag",
        in_specs=[any_spec] * na, out_specs=[any_spec] * na,
        out_shape=out_shapes,
        scratch_shapes=[pltpu.SemaphoreType.DMA((na, 7)), pltpu.SemaphoreType.DMA((na, 7)),
                        pltpu.SemaphoreType.DMA((na,))],
    )(*shards)


def _reduce_scatter_exchange(sends, jobs, out_shapes):
    ns, nj = len(sends), len(jobs)

    def body(*refs):
        ins, outs = refs[:ns], refs[ns:ns + len(out_shapes)]
        send_sems, recv_sems, local_sems = refs[ns + len(out_shapes):]
        x, y, c = _mesh_pos()
        me = (x, y, c)
        me_i = _dev_index(me)
        copies = []
        for t, (si, slab, ri, slot) in enumerate(jobs):
            loc = pltpu.make_async_copy(slab(ins[si], me_i), slot(outs[ri], me_i), local_sems.at[t])
            loc.start()
            copies.append(loc)
            for k in range(1, N_DEV):
                peer = (x ^ ((k >> 2) & 1), y ^ ((k >> 1) & 1), c ^ (k & 1))
                cp = pltpu.make_async_remote_copy(
                    src_ref=slab(ins[si], _dev_index(peer)), dst_ref=slot(outs[ri], me_i),
                    send_sem=send_sems.at[t, k - 1], recv_sem=recv_sems.at[t, k - 1],
                    device_id=peer, device_id_type=MESH)
                cp.start()
                copies.append(cp)
        for cp in copies:
            cp.wait()

    any_spec = pl.BlockSpec(memory_space=pl.ANY)
    return pl.pallas_call(
        body, name="rs",
        in_specs=[any_spec] * ns, out_specs=[any_spec] * len(out_shapes),
        out_shape=out_shapes,
        scratch_shapes=[pltpu.SemaphoreType.DMA((nj, N_DEV - 1)), pltpu.SemaphoreType.DMA((nj, N_DEV - 1)),
                        pltpu.SemaphoreType.DMA((nj,))],
    )(*sends)


def _adamw_math(w, g, m, v):
    m2 = ADAM_B1 * m + (1.0 - ADAM_B1) * g
    v2 = ADAM_B2 * v + (1.0 - ADAM_B2) * (g * g)
    m_hat = m2 / (1.0 - ADAM_B1 ** ADAM_STEP)
    v_hat = v2 / (1.0 - ADAM_B2 ** ADAM_STEP)
    delta = -ADAM_LR * (m_hat / (jnp.sqrt(v_hat) + ADAM_EPS) + ADAM_WD * w)
    return delta, m2, v2


def _adamw_sum(name, recv, w, m, v, row_mult):
    L, _, R, C = recv.shape
    tr = _tile(R, 128, row_mult)

    def body(r_ref, w_ref, m_ref, v_ref, g_ref, d_ref, nm_ref, nv_ref):
        g = r_ref[0].astype(F32)
        for s in range(1, N_DEV):
            g = g + r_ref[s].astype(F32)
        d, m2, v2 = _adamw_math(w_ref[...], g, m_ref[...], v_ref[...])
        g_ref[...] = g
        d_ref[...] = d
        nm_ref[...] = m2
        nv_ref[...] = v2

    blk = pl.BlockSpec((None, tr, C), lambda l, i: (l, i, 0))
    shp = jax.ShapeDtypeStruct((L, R, C), F32)
    return pl.pallas_call(
        body, name=name, grid=(L, R // tr),
        in_specs=[pl.BlockSpec((None, N_DEV, tr, C), lambda l, i: (l, 0, i, 0)), blk, blk, blk],
        out_specs=[blk, blk, blk, blk],
        out_shape=[shp, shp, shp, shp],
        compiler_params=_params(("arbitrary", "arbitrary")),
    )(recv, w, m, v)


def _adamw_plain(name, g, w, m, v):
    def body(g_ref, w_ref, m_ref, v_ref, d_ref, nm_ref, nv_ref):
        d, m2, v2 = _adamw_math(w_ref[...], g_ref[...], m_ref[...], v_ref[...])
        d_ref[...] = d
        nm_ref[...] = m2
        nv_ref[...] = v2

    shp = jax.ShapeDtypeStruct(g.shape, F32)
    return pl.pallas_call(body, name=name, out_shape=[shp, shp, shp], compiler_params=_params())(g, w, m, v)


SMALL = ["norm1_g", "q_norm_g", "k_norm_g", "sink", "sgu_ln_g", "sgu_ln_b", "w_s", "b_s",
         "attn_out_g", "sgu_out_g", "norm2_g", "conv_b"]
PACK_ALIGN = 1024


def _pack(pieces):
    flat = []
    for p in pieces:
        f = p.reshape(-1).astype(F32)
        pad = (-f.shape[0]) % PACK_ALIGN
        flat.append(jnp.pad(f, (0, pad)) if pad else f)
    return jnp.concatenate(flat).reshape(-1, 128)


def _unpack(packed, shapes):
    flat = packed.reshape(-1)
    out, off = [], 0
    for shp in shapes:
        n = 1
        for d in shp:
            n *= d
        out.append(flat[off:off + n].reshape(shp))
        off += n + ((-n) % PACK_ALIGN)
    return out


def kernel(x, norm1_g, w_in, q_norm_g, k_norm_g, sink, sgu_ln_g, sgu_ln_b, w_s, b_s, attn_out_g, sgu_out_g, w_o, norm2_g, w_up, conv_w, conv_b, w_down, loss_target, m_norm1_g, m_w_in, m_q_norm_g, m_k_norm_g, m_sink, m_sgu_ln_g, m_sgu_ln_b, m_w_s, m_b_s, m_attn_out_g, m_sgu_out_g, m_w_o, m_norm2_g, m_w_up, m_conv_w, m_conv_b, m_w_down, v_norm1_g, v_w_in, v_q_norm_g, v_k_norm_g, v_sink, v_sgu_ln_g, v_sgu_ln_b, v_w_s, v_b_s, v_attn_out_g, v_sgu_out_g, v_w_o, v_norm2_g, v_w_up, v_conv_w, v_conv_b, v_w_down):
    weights = dict(norm1_g=norm1_g, w_in=w_in, q_norm_g=q_norm_g, k_norm_g=k_norm_g, sink=sink, sgu_ln_g=sgu_ln_g,
                   sgu_ln_b=sgu_ln_b, w_s=w_s, b_s=b_s, attn_out_g=attn_out_g, sgu_out_g=sgu_out_g, w_o=w_o,
                   norm2_g=norm2_g, w_up=w_up, conv_w=conv_w, conv_b=conv_b, w_down=w_down)
    mom_m = dict(norm1_g=m_norm1_g, w_in=m_w_in, q_norm_g=m_q_norm_g, k_norm_g=m_k_norm_g, sink=m_sink,
                 sgu_ln_g=m_sgu_ln_g, sgu_ln_b=m_sgu_ln_b, w_s=m_w_s, b_s=m_b_s, attn_out_g=m_attn_out_g,
                 sgu_out_g=m_sgu_out_g, w_o=m_w_o, norm2_g=m_norm2_g, w_up=m_w_up, conv_w=m_conv_w,
                 conv_b=m_conv_b, w_down=m_w_down)
    mom_v = dict(norm1_g=v_norm1_g, w_in=v_w_in, q_norm_g=v_q_norm_g, k_norm_g=v_k_norm_g, sink=v_sink,
                 sgu_ln_g=v_sgu_ln_g, sgu_ln_b=v_sgu_ln_b, w_s=v_w_s, b_s=v_b_s, attn_out_g=v_attn_out_g,
                 sgu_out_g=v_sgu_out_g, w_o=v_w_o, norm2_g=v_norm2_g, w_up=v_w_up, conv_w=v_conv_w,
                 conv_b=v_conv_b, w_down=v_w_down)
    order = ["norm1_g", "w_in", "q_norm_g", "k_norm_g", "sink", "sgu_ln_g", "sgu_ln_b", "w_s", "b_s",
             "attn_out_g", "sgu_out_g", "w_o", "norm2_g", "w_up", "conv_w", "conv_b", "w_down"]

    _, S, D = x.shape
    L = w_in.shape[0]
    AW = D // 2
    NQ = AW // HEAD
    NKV = max(1, NQ // 4)
    G = NQ // NKV
    KVW = NKV * HEAD
    GW = D - AW
    NG = GW // HEAD
    IN = AW + 2 * KVW + 2 * GW
    INS = w_in.shape[2]
    OS = w_o.shape[1]
    US = w_up.shape[2]
    DS = w_down.shape[1]
    F2 = US * N_DEV
    F = F2 // 2
    assert INS * N_DEV == IN and OS * N_DEV == D and DS * N_DEV == F and AW == GW and (3 * AW) % (2 * KVW) == 0
    cfg = dict(S=S, D=D, AW=AW, NQ=NQ, NKV=NKV, G=G, KVW=KVW, GW=GW, NG=NG, F=F, F2=F2)

    def rows_slab(n):
        return lambda ref, idx: ref.at[:, pl.ds(pl.multiple_of(idx * n, n), n), :]

    def cols_slab(n):
        return lambda ref, idx: ref.at[:, :, pl.ds(pl.multiple_of(idx * n, n), n)]

    def lead_slab(ref, idx):
        return ref.at[idx]

    gathered = _all_gather(
        [w_in.astype(BF16), w_o.astype(BF16), w_up.astype(BF16), w_down.astype(BF16), conv_w],
        [jax.ShapeDtypeStruct((N_DEV, L, D, INS), BF16), jax.ShapeDtypeStruct((L, D, D), BF16),
         jax.ShapeDtypeStruct((L, D, F2), BF16), jax.ShapeDtypeStruct((L, F, D), BF16),
         jax.ShapeDtypeStruct((L, 3, F2), F32)],
        [lead_slab, rows_slab(OS), cols_slab(US), rows_slab(DS), cols_slab(US)])
    w_in_sh, w_o_f, w_up_f, w_down_f, conv_w_f = gathered
    w_in_f = jnp.transpose(w_in_sh, (1, 2, 0, 3)).reshape(L, D, IN)
    w_in_p = jnp.concatenate([w_in_f[:, :, :AW], w_in_f[:, :, AW + 2 * KVW:], w_in_f[:, :, AW:AW + 2 * KVW]], axis=2)

    n1g3, n2g3 = norm1_g.reshape(L, 1, D), norm2_g.reshape(L, 1, D)
    qg3, kg3 = q_norm_g.reshape(L, 1, HEAD), k_norm_g.reshape(L, 1, HEAD)
    lng3, lnb3 = sgu_ln_g.reshape(L, 1, GW), sgu_ln_b.reshape(L, 1, GW)
    ag3, og3 = attn_out_g.reshape(L, 1, AW), sgu_out_g.reshape(L, 1, GW)
    cb3 = conv_b.reshape(L, 1, F2)
    ws_b = w_s.astype(BF16)
    wst_b = jnp.swapaxes(w_s, 2, 3).astype(BF16)
    bs_b = jnp.broadcast_to(b_s[..., None], (L, NG, HEAD, HEAD))
    inv_freq = ROPE_THETA ** (-jnp.arange(0, HEAD, 2, dtype=F32) / HEAD)
    ang = jnp.arange(S, dtype=F32)[:, None] * inv_freq[None, :]
    cos2 = jnp.concatenate([jnp.cos(ang), jnp.cos(ang)], axis=1)
    sin2 = jnp.concatenate([-jnp.sin(ang), jnp.sin(ang)], axis=1)

    tn = 512
    t_in, t_d, t_f, t_f2 = _tile(IN, tn, 128), _tile(D, tn, 128), _tile(F, tn, 128), _tile(F2, tn, 128)
    tk_f = _tile(F, 1408, 128)
    tk_in = _tile(IN, 1792, 128)
    tm_f = _tile(F, 512, 128)

    def w_spec(l, tk, tn_):
        return pl.BlockSpec((None, tk, tn_), lambda i, j, k: (l, k, j))

    def wt_spec(l, tn_, tk):
        return pl.BlockSpec((None, tn_, tk), lambda i, j, k: (l, j, k))

    def a_spec(tm, tk):
        return pl.BlockSpec((tm, tk), lambda i, j, k: (i, k))

    def at_spec(tk, tm):
        return pl.BlockSpec((tk, tm), lambda i, j, k: (k, i))

    def b_spec(tk, tn_):
        return pl.BlockSpec((tk, tn_), lambda i, j, k: (k, j))

    xs = x.reshape(S, D)
    saved = []
    cur = xs
    for l in range(L):
        h = _rms_fwd("rms1_fwd", cur, n1g3, l)
        z = _mm("mm_in", h, w_in_p, M=S, N=IN, K=D, tm=S, tn=t_in, tk=D,
                a_spec=a_spec(S, D), b_spec=w_spec(l, D, t_in))
        q_r, k_r, v_b = _qkv_prep("qkv_prep", z, qg3, kg3, cos2, sin2, l, cfg)
        attn, mix_l = _attn_fwd("attn_fwd", q_r, k_r, v_b, sink, ag3, l, cfg)
        mix_r = _sgu_fwd("sgu_fwd", z, lng3, lnb3, ws_b, bs_b, og3, l, cfg)
        mixed = jnp.concatenate([mix_l, mix_r], axis=1)
        x1 = _mm("mm_o", mixed, w_o_f, M=S, N=D, K=D, tm=S, tn=t_d, tk=D,
                 a_spec=a_spec(S, D), b_spec=w_spec(l, D, t_d), res=cur)
        h2 = _rms_fwd("rms2_fwd", x1, n2g3, l)
        ap = _mm("mm_up", h2, w_up_f, M=S, N=F2, K=D, tm=S, tn=t_f2, tk=D,
                 a_spec=a_spec(S, D), b_spec=w_spec(l, D, t_f2))
        y_b = _conv_glu_fwd("conv_glu_fwd", ap, conv_w_f, cb3, l, cfg)
        x2 = _mm("mm_down", y_b, w_down_f, M=S, N=D, K=F, tm=S, tn=t_d, tk=tk_f,
                 a_spec=a_spec(S, tk_f), b_spec=w_spec(l, tk_f, t_d), res=x1)
        saved.append(dict(x=cur, h=h, z=z, q=q_r, k=k_r, v=v_b, attn=attn, mixed=mixed, x1=x1, h2=h2, ap=ap, y=y_b))
        cur = x2

    loss_tile, dx, dxb = _loss_bwd("loss", cur, loss_target.reshape(S, D))
    loss = lax.psum(loss_tile[0, 0], ("x", "y", "c"))

    gW = [dict() for _ in range(L)]
    gS = [dict() for _ in range(L)]
    nkf = F // tk_f
    njf = F // t_f
    for l in reversed(range(L)):
        sv = saved[l]
        dy = _mm("mm_dy", dxb, w_down_f, M=S, N=F, K=D, tm=S, tn=t_f, tk=D, tb=True,
                 a_spec=a_spec(S, D), b_spec=wt_spec(l, t_f, D))
        gW[l]["w_down"] = _mm("mm_gdown", sv["y"], dxb, M=F, N=D, K=S, tm=tm_f, tn=D, tk=S, ta=True,
                              a_spec=at_spec(S, tm_f), b_spec=b_spec(S, D), out_dtype=BF16)
        dap3, dcw, dcb = _glu_conv_bwd("glu_conv_bwd", dy, sv["ap"], conv_w_f, cb3, l, cfg)
        gS[l]["conv_w"] = jnp.concatenate([dcw[0], dcw[1]], axis=1)
        gS[l]["conv_b"] = jnp.concatenate([dcb[0], dcb[1]], axis=1).reshape(F2)
        dh2 = _mm("mm_dh2", dap3, w_up_f, M=S, N=D, K=F2, tm=S, tn=t_d, tk=tk_f, tb=True,
                  a_spec=pl.BlockSpec((None, S, tk_f), lambda i, j, k: (k // nkf, 0, k % nkf)),
                  b_spec=wt_spec(l, t_d, tk_f))
        gW[l]["w_up"] = _mm("mm_gup", sv["h2"], dap3, M=D, N=F2, K=S, tm=D, tn=t_f, tk=S, ta=True,
                            a_spec=at_spec(S, D),
                            b_spec=pl.BlockSpec((None, S, t_f), lambda i, j, k: (j // njf, 0, j % njf)),
                            out_dtype=BF16)
        dx1, dx1b, dg2 = _rms_bwd("rms2_bwd", sv["x1"], dh2, dx, n2g3, l)
        gS[l]["norm2_g"] = dg2.reshape(D)
        dmix = _mm("mm_dmix", dx1b, w_o_f, M=S, N=D, K=D, tm=S, tn=t_d, tk=D, tb=True,
                   a_spec=a_spec(S, D), b_spec=wt_spec(l, t_d, D))
        gW[l]["w_o"] = _mm("mm_go", sv["mixed"], dx1b, M=D, N=D, K=S, tm=D, tn=t_d, tk=S, ta=True,
                           a_spec=at_spec(S, D), b_spec=b_spec(S, t_d), out_dtype=BF16)
        dq_r, dk_pad, dv_pad, dsink, dag = _attn_bwd("attn_bwd", sv["q"], sv["k"], sv["v"], sv["attn"], dmix,
                                                     sink, ag3, l, cfg)
        gS[l]["sink"] = dsink[:, 0]
        gS[l]["attn_out_g"] = dag.reshape(AW)
        dzgu, dzgv, dws, dbs, dlng, dlnb, dog = _sgu_bwd("sgu_bwd", sv["z"], dmix, lng3, lnb3, ws_b, wst_b, bs_b,
                                                        og3, l, cfg)
        gS[l]["w_s"] = dws
        gS[l]["b_s"] = dbs[:, :, 0]
        gS[l]["sgu_ln_g"] = dlng.reshape(GW)
        gS[l]["sgu_ln_b"] = dlnb.reshape(GW)
        gS[l]["sgu_out_g"] = dog.reshape(GW)
        dzq, dzkv, dqg, dkg = _qkv_prep_bwd("qkv_prep_bwd", sv["z"], dq_r, dk_pad, dv_pad, qg3, kg3, cos2, sin2,
                                            l, cfg)
        gS[l]["q_norm_g"] = dqg.reshape(HEAD)
        gS[l]["k_norm_g"] = dkg.reshape(HEAD)
        dz = jnp.concatenate([dzq, dzgu, dzgv, dzkv], axis=1)
        dh = _mm("mm_dh", dz, w_in_p, M=S, N=D, K=IN, tm=S, tn=t_d, tk=tk_in, tb=True,
                 a_spec=a_spec(S, tk_in), b_spec=wt_spec(l, t_d, tk_in))
        gin_p = _mm("mm_gin", sv["h"], dz, M=D, N=IN, K=S, tm=D, tn=t_in, tk=S, ta=True,
                    a_spec=at_spec(S, D), b_spec=b_spec(S, t_in), out_dtype=BF16)
        gin = jnp.concatenate([gin_p[:, :AW], gin_p[:, AW + 2 * GW:], gin_p[:, AW:AW + 2 * GW]], axis=1)
        gW[l]["w_in"] = jnp.transpose(gin.reshape(D, N_DEV, INS), (1, 0, 2))
        dx, dxb, dg1 = _rms_bwd("rms1_bwd", sv["x"], dh, dx1, n1g3, l)
        gS[l]["norm1_g"] = dg1.reshape(D)
    grad_x = dx.reshape(1, S, D)

    small_pieces = []
    for l in range(L):
        small_pieces += [gS[l][n] for n in SMALL] + [gS[l]["conv_w"]]
    small_send = _pack(small_pieces)
    SR = small_send.shape[0]

    def rs_rows(n):
        return lambda ref, idx: ref.at[pl.ds(pl.multiple_of(idx * n, n), n), :]

    def rs_cols(n):
        return lambda ref, idx: ref.at[:, pl.ds(pl.multiple_of(idx * n, n), n)]

    def slot_of(l):
        return lambda ref, idx: ref.at[l, idx]

    sends, jobs = [], []
    for l in range(L):
        for ri, (name, slab) in enumerate([("w_in", lead_slab), ("w_o", rs_rows(OS)), ("w_up", rs_cols(US)),
                                           ("w_down", rs_rows(DS))]):
            jobs.append((len(sends), slab, ri, slot_of(l)))
            sends.append(gW[l][name])
    jobs.append((len(sends), lambda ref, idx: ref, 4, lead_slab))
    sends.append(small_send)
    recv_in, recv_o, recv_up, recv_down, recv_small = _reduce_scatter_exchange(
        sends, jobs,
        [jax.ShapeDtypeStruct((L, N_DEV, D, INS), BF16), jax.ShapeDtypeStruct((L, N_DEV, OS, D), BF16),
         jax.ShapeDtypeStruct((L, N_DEV, D, US), BF16), jax.ShapeDtypeStruct((L, N_DEV, DS, D), BF16),
         jax.ShapeDtypeStruct((N_DEV, SR, 128), F32)])

    grads, deltas, new_m, new_v = {}, {}, {}, {}
    for name, recv in [("w_in", recv_in), ("w_o", recv_o), ("w_up", recv_up), ("w_down", recv_down)]:
        grads[name], deltas[name], new_m[name], new_v[name] = _adamw_sum(
            "adamw_" + name, recv, weights[name], mom_m[name], mom_v[name], 16)

    small_shapes = []
    for l in range(L):
        small_shapes += [weights[n].shape[1:] for n in SMALL] + [(3, F2)]

    def pack_small(src):
        pieces = []
        for l in range(L):
            pieces += [src[n][l] for n in SMALL] + [jnp.zeros((3, F2), F32)]
        return _pack(pieces)

    tr_s = _tile(SR, 512, 8)

    def small_body(r_ref, w_ref, m_ref, v_ref, g_ref, d_ref, nm_ref, nv_ref):
        g = r_ref[0]
        for s in range(1, N_DEV):
            g = g + r_ref[s]
        d, m2, v2 = _adamw_math(w_ref[...], g, m_ref[...], v_ref[...])
        g_ref[...] = g
        d_ref[...] = d
        nm_ref[...] = m2
        nv_ref[...] = v2

    sblk = pl.BlockSpec((tr_s, 128), lambda i: (i, 0))
    sshp = jax.ShapeDtypeStruct((SR, 128), F32)
    sm_g, sm_d, sm_m, sm_v = pl.pallas_call(
        small_body, name="adamw_small", grid=(SR // tr_s,),
        in_specs=[pl.BlockSpec((N_DEV, tr_s, 128), lambda i: (0, i, 0)), sblk, sblk, sblk],
        out_specs=[sblk, sblk, sblk, sblk], out_shape=[sshp, sshp, sshp, sshp],
        compiler_params=_params(("arbitrary",)),
    )(recv_small, pack_small(weights), pack_small(mom_m), pack_small(mom_v))

    per = len(SMALL) + 1
    for store, packed in [(grads, sm_g), (deltas, sm_d), (new_m, sm_m), (new_v, sm_v)]:
        parts = _unpack(packed, small_shapes)
        for i, n in enumerate(SMALL):
            store[n] = jnp.stack([parts[l * per + i] for l in range(L)])
        if store is grads:
            conv_full = jnp.stack([parts[l * per + len(SMALL)] for l in range(L)])
    me_i = _dev_index(_mesh_pos())
    g_cw = lax.dynamic_slice_in_dim(conv_full, me_i * US, US, axis=2)
    grads["conv_w"] = g_cw
    d_cw, m_cw, v_cw = _adamw_plain("adamw_conv_w", g_cw.reshape(L * 3, US), conv_w.reshape(L * 3, US),
                                    m_conv_w.reshape(L * 3, US), v_conv_w.reshape(L * 3, US))
    deltas["conv_w"] = d_cw.reshape(L, 3, US)
    new_m["conv_w"] = m_cw.reshape(L, 3, US)
    new_v["conv_w"] = v_cw.reshape(L, 3, US)

    return (loss, grad_x, *[grads[n] for n in order], *[deltas[n] for n in order],
            *[new_m[n] for n in order], *[new_v[n] for n in order])
```

```python
import jax
import jax.numpy as jnp
from jax import lax
from jax.experimental import pallas as pl
from jax.experimental.pallas import tpu as pltpu

F32 = jnp.float32
BF16 = jnp.bfloat16
MESH = pl.DeviceIdType.MESH

N_DEV = 8
HEAD = 128
EPS = 1e-6
MASK_VALUE = -1e30
ROPE_THETA = 10000.0
GELU_C = 0.7978845608028654
GELU_A = 0.044715

ADAM_LR = 0.001
ADAM_B1 = 0.9
ADAM_B2 = 0.999
ADAM_EPS = 1e-08
ADAM_WD = 0.01
ADAM_STEP = 10

VMEM_LIMIT = 56 * 1024 * 1024


def _tile(n, pref, mult):
    best = None
    for t in range(mult, min(n, pref) + 1, mult):
        if n % t == 0:
            best = t
    return n if best is None else best


def _params(sem=None):
    kw = dict(vmem_limit_bytes=VMEM_LIMIT)
    if sem is not None:
        kw["dimension_semantics"] = sem
    return pltpu.CompilerParams(**kw)


def _gelu(x):
    return x * (0.5 * (1.0 + jnp.tanh(GELU_C * (x + GELU_A * (x * x * x)))))


def _gelu_grad(x):
    t = jnp.tanh(GELU_C * (x + GELU_A * (x * x * x)))
    return 0.5 * (1.0 + t) + 0.5 * x * (1.0 - t * t) * (GELU_C * (1.0 + 3.0 * GELU_A * (x * x)))


def _sigmoid(x):
    return 1.0 / (1.0 + jnp.exp(-x))


def _vec_spec(l, n):
    return pl.BlockSpec((None, 1, n), lambda *_: (l, 0, 0))


def _mm(name, a, b, *, M, N, K, tm, tn, tk, a_spec, b_spec, ta=False, tb=False, out_dtype=F32, res=None,
        out_shape=None, out_spec=None):
    nm, nn, nk = M // tm, N // tn, K // tk
    assert nm * tm == M and nn * tn == N and nk * tk == K
    assert not (ta and nk > 1)
    dims = (((1,), (1,)), ((), ())) if tb else (((1,), (0,)), ((), ()))

    def body(*refs):
        refs = list(refs)
        a_ref = refs.pop(0)
        b_ref = refs.pop(0)
        r_ref = refs.pop(0) if res is not None else None
        o_ref = refs.pop(0)
        acc = refs.pop(0) if nk > 1 else None
        at = refs.pop(0) if ta else None
        k = pl.program_id(2)
        if ta:
            @pl.when(pl.program_id(1) == 0)
            def _():
                at[...] = a_ref[...].T
            lhs = at[...]
        else:
            lhs = a_ref[...]
        p = lax.dot_general(lhs, b_ref[...], dims, preferred_element_type=F32)

        def finish(r):
            if r_ref is not None:
                r = r_ref[...] + r
            o_ref[...] = r.astype(out_dtype)

        if nk == 1:
            finish(p)
        else:
            @pl.when(k == 0)
            def _():
                acc[...] = p

            @pl.when(k > 0)
            def _():
                acc[...] += p

            @pl.when(k == nk - 1)
            def _():
                finish(acc[...])

    in_specs = [a_spec, b_spec]
    args = [a, b]
    if res is not None:
        in_specs.append(pl.BlockSpec((tm, tn), lambda i, j, k: (i, j)))
        args.append(res)
    scratch = []
    if nk > 1:
        scratch.append(pltpu.VMEM((tm, tn), F32))
    if ta:
        scratch.append(pltpu.VMEM((tm, tk), BF16))
    return pl.pallas_call(
        body, name=name, grid=(nm, nn, nk),
        in_specs=in_specs,
        out_specs=pl.BlockSpec((tm, tn), lambda i, j, k: (i, j)) if out_spec is None else out_spec,
        out_shape=jax.ShapeDtypeStruct((M, N) if out_shape is None else out_shape, out_dtype),
        scratch_shapes=scratch,
        compiler_params=_params(("arbitrary", "arbitrary", "arbitrary")),
    )(*args)


def _rms_fwd(name, x, g3, l):
    S, D = x.shape
    tr = _tile(S, 256, 16)

    def body(x_ref, g_ref, h_ref):
        xv = x_ref[...]
        r = lax.rsqrt(jnp.mean(xv * xv, axis=-1, keepdims=True) + EPS)
        h_ref[...] = ((xv * r) * g_ref[...]).astype(BF16)

    return pl.pallas_call(
        body, name=name, grid=(S // tr,),
        in_specs=[pl.BlockSpec((tr, D), lambda i: (i, 0)), _vec_spec(l, D)],
        out_specs=pl.BlockSpec((tr, D), lambda i: (i, 0)),
        out_shape=jax.ShapeDtypeStruct((S, D), BF16),
        compiler_params=_params(("arbitrary",)),
    )(x, g3)


def _rope(t, cos2, sin2):
    return t * cos2 + pltpu.roll(t, HEAD // 2, axis=1) * sin2


def _qkv_prep(name, z, qg3, kg3, cos2, sin2, l, cfg):
    S, AW, KVW, NQ, NKV = cfg["S"], cfg["AW"], cfg["KVW"], cfg["NQ"], cfg["NKV"]
    tr = _tile(S, 256, 16)
    kv_blk = (3 * AW) // (2 * KVW)

    def body(zq_ref, zkv_ref, qg_ref, kg_ref, c_ref, s_ref, q_ref, k_ref, v_ref):
        cosv, sinv = c_ref[...], s_ref[...]

        def norm_rope(t, g):
            r = lax.rsqrt(jnp.mean(t * t, axis=-1, keepdims=True) + EPS)
            return _rope((t * r) * g, cosv, sinv)

        for h in range(NQ):
            sl = slice(h * HEAD, (h + 1) * HEAD)
            q_ref[:, sl] = norm_rope(zq_ref[:, sl], qg_ref[...]).astype(BF16)
        for h in range(NKV):
            sl = slice(h * HEAD, (h + 1) * HEAD)
            k_ref[:, sl] = norm_rope(zkv_ref[:, sl], kg_ref[...]).astype(BF16)
        v_ref[...] = zkv_ref[:, KVW:].astype(BF16)

    return pl.pallas_call(
        body, name=name, grid=(S // tr,),
        in_specs=[pl.BlockSpec((tr, AW), lambda i: (i, 0)),
                  pl.BlockSpec((tr, 2 * KVW), lambda i: (i, kv_blk)),
                  _vec_spec(l, HEAD), _vec_spec(l, HEAD),
                  pl.BlockSpec((tr, HEAD), lambda i: (i, 0)),
                  pl.BlockSpec((tr, HEAD), lambda i: (i, 0))],
        out_specs=[pl.BlockSpec((tr, AW), lambda i: (i, 0)),
                   pl.BlockSpec((tr, KVW), lambda i: (i, 0)),
                   pl.BlockSpec((tr, KVW), lambda i: (i, 0))],
        out_shape=[jax.ShapeDtypeStruct((S, AW), BF16),
                   jax.ShapeDtypeStruct((S, KVW), BF16),
                   jax.ShapeDtypeStruct((S, KVW), BF16)],
        compiler_params=_params(("arbitrary",)),
    )(z, z, qg3, kg3, cos2, sin2)


def _band_specs(width, nb):
    return [pl.BlockSpec((HEAD, width), lambda n: (jnp.maximum(n - 1, 0), 0)),
            pl.BlockSpec((HEAD, width), lambda n: (n, 0)),
            pl.BlockSpec((HEAD, width), lambda n: (jnp.minimum(n + 1, nb - 1), 0))]


def _attn_probs(qs, kj, n, sink_of_row, S, G):
    s = lax.dot_general(qs, kj, (((1,), (1,)), ((), ())), preferred_element_type=F32) * (HEAD ** -0.5)
    rows = lax.broadcasted_iota(jnp.int32, (G * HEAD, 3 * HEAD), 0)
    cols = lax.broadcasted_iota(jnp.int32, (G * HEAD, 3 * HEAD), 1)
    qi = rows & (HEAD - 1)
    kpos = n * HEAD - HEAD + cols
    valid = (cols >= qi) & (cols <= qi + 2 * HEAD) & (kpos >= 0) & (kpos < S)
    s = jnp.where(valid, s, MASK_VALUE)
    m = jnp.maximum(jnp.max(s, axis=-1, keepdims=True), sink_of_row)
    p = jnp.exp(s - m)
    e_sink = jnp.exp(sink_of_row - m)
    inv = 1.0 / (jnp.sum(p, axis=-1, keepdims=True) + e_sink)
    return p * inv, e_sink * inv


def _sink_rows(sink_ref, l, j, G):
    hidx = lax.broadcasted_iota(jnp.int32, (G * HEAD, 1), 0) // HEAD
    col = jnp.full((G * HEAD, 1), sink_ref[l, j * G], F32)
    for g in range(1, G):
        col = jnp.where(hidx == g, sink_ref[l, j * G + g], col)
    return col


def _attn_fwd(name, q, k, v, sink, ag3, l, cfg):
    S, AW, KVW, NKV, G = cfg["S"], cfg["AW"], cfg["KVW"], cfg["NKV"], cfg["G"]
    nb = S // HEAD

    def body(q_ref, kp, kc, kn, vp, vc, vn, sink_ref, ag_ref, a_ref, mix_ref):
        n = pl.program_id(0)
        kb = jnp.concatenate([kp[...], kc[...], kn[...]], axis=0)
        vb = jnp.concatenate([vp[...], vc[...], vn[...]], axis=0)
        for j in range(NKV):
            sl = slice(j * HEAD, (j + 1) * HEAD)
            qs = jnp.concatenate([q_ref[:, (j * G + g) * HEAD:(j * G + g + 1) * HEAD] for g in range(G)], axis=0)
            probs, _ = _attn_probs(qs, kb[:, sl], n, _sink_rows(sink_ref, l, j, G), S, G)
            o = jnp.dot(probs.astype(BF16), vb[:, sl], preferred_element_type=F32)
            for g in range(G):
                a_ref[:, (j * G + g) * HEAD:(j * G + g + 1) * HEAD] = o[g * HEAD:(g + 1) * HEAD]
        a = a_ref[...]
        r = lax.rsqrt(jnp.mean(a * a, axis=-1, keepdims=True) + EPS)
        mix_ref[...] = ((a * r) * ag_ref[...]).astype(BF16)

    return pl.pallas_call(
        body, name=name, grid=(nb,),
        in_specs=[pl.BlockSpec((HEAD, AW), lambda n: (n, 0))] + _band_specs(KVW, nb) + _band_specs(KVW, nb)
                 + [pl.BlockSpec(memory_space=pltpu.SMEM), _vec_spec(l, AW)],
        out_specs=[pl.BlockSpec((HEAD, AW), lambda n: (n, 0)), pl.BlockSpec((HEAD, AW), lambda n: (n, 0))],
        out_shape=[jax.ShapeDtypeStruct((S, AW), F32), jax.ShapeDtypeStruct((S, AW), BF16)],
        compiler_params=_params(("arbitrary",)),
    )(q, k, k, k, v, v, v, sink, ag3)


def _sgu_forward_math(gu, gv, lng, lnb):
    u = _gelu(gu)
    vv = _gelu(gv)
    mu = jnp.mean(vv, axis=-1, keepdims=True)
    xc = vv - mu
    rstd = lax.rsqrt(jnp.mean(xc * xc, axis=-1, keepdims=True) + EPS)
    xhat = xc * rstd
    vn = xhat * lng + lnb
    return u, xhat, rstd, vn


def _sgu_fwd(name, z, lng3, lnb3, ws_b, bs_b, og3, l, cfg):
    S, GW, NG = cfg["S"], cfg["GW"], cfg["NG"]

    def body(gu_ref, gv_ref, lng_ref, lnb_ref, ws_ref, bs_ref, og_ref, mix_ref, sg_ref):
        u, _, _, vn = _sgu_forward_math(gu_ref[...], gv_ref[...], lng_ref[...], lnb_ref[...])
        vnb = vn.astype(BF16)
        for h in range(NG):
            sl = slice(h * HEAD, (h + 1) * HEAD)
            f = jnp.dot(ws_ref[h], vnb[:, sl], preferred_element_type=F32) + bs_ref[h]
            sg_ref[:, sl] = u[:, sl] * f
        sg = sg_ref[...]
        r = lax.rsqrt(jnp.mean(sg * sg, axis=-1, keepdims=True) + EPS)
        mix_ref[...] = ((sg * r) * og_ref[...]).astype(BF16)

    return pl.pallas_call(
        body, name=name, grid=(S // HEAD,),
        in_specs=[pl.BlockSpec((HEAD, GW), lambda c: (c, 1)),
                  pl.BlockSpec((HEAD, GW), lambda c: (c, 2)),
                  _vec_spec(l, GW), _vec_spec(l, GW),
                  pl.BlockSpec((None, NG, HEAD, HEAD), lambda c: (l, 0, 0, 0)),
                  pl.BlockSpec((None, NG, HEAD, HEAD), lambda c: (l, 0, 0, 0)),
                  _vec_spec(l, GW)],
        out_specs=pl.BlockSpec((HEAD, GW), lambda c: (c, 0)),
        out_shape=jax.ShapeDtypeStruct((S, GW), BF16),
        scratch_shapes=[pltpu.VMEM((HEAD, GW), F32)],
        compiler_params=_params(("arbitrary",)),
    )(z, z, lng3, lnb3, ws_b, bs_b, og3)


def _conv3(t, w, b, S):
    row = lax.broadcasted_iota(jnp.int32, t.shape, 0)
    dn = jnp.where(row == 0, 0.0, pltpu.roll(t, 1, axis=0))
    up = jnp.where(row == S - 1, 0.0, pltpu.roll(t, S - 1, axis=0))
    return ((b + dn * w[0:1]) + t * w[1:2]) + up * w[2:3], dn, up


def _conv_glu_fwd(name, ap, cw, cb3, l, cfg):
    S, F = cfg["S"], cfg["F"]
    tc = _tile(F, 256, 128)
    nf = F // tc

    def body(g_ref, u_ref, wg_ref, wu_ref, bg_ref, bu_ref, y_ref):
        ag, _, _ = _conv3(g_ref[...], wg_ref[...], bg_ref[...], S)
        au, _, _ = _conv3(u_ref[...], wu_ref[...], bu_ref[...], S)
        y_ref[...] = ((ag * _sigmoid(ag)) * au).astype(BF16)

    return pl.pallas_call(
        body, name=name, grid=(nf,),
        in_specs=[pl.BlockSpec((S, tc), lambda j: (0, j)),
                  pl.BlockSpec((S, tc), lambda j: (0, j + nf)),
                  pl.BlockSpec((None, 3, tc), lambda j: (l, 0, j)),
                  pl.BlockSpec((None, 3, tc), lambda j: (l, 0, j + nf)),
                  pl.BlockSpec((None, 1, tc), lambda j: (l, 0, j)),
                  pl.BlockSpec((None, 1, tc), lambda j: (l, 0, j + nf))],
        out_specs=pl.BlockSpec((S, tc), lambda j: (0, j)),
        out_shape=jax.ShapeDtypeStruct((S, F), BF16),
        compiler_params=_params(("arbitrary",)),
    )(ap, ap, cw, cw, cb3, cb3)


def _loss_bwd(name, y, target):
    S, D = y.shape
    tr = _tile(S, 256, 16)

    def body(y_ref, t_ref, loss_ref, d_ref, db_ref):
        @pl.when(pl.program_id(0) == 0)
        def _():
            loss_ref[...] = jnp.zeros_like(loss_ref)

        err = y_ref[...] - t_ref[...]
        part = 0.5 * jnp.sum(jnp.mean(err * err, axis=-1, keepdims=True), axis=0, keepdims=True)
        loss_ref[...] += jnp.broadcast_to(part, loss_ref.shape)
        d = err * (1.0 / D)
        d_ref[...] = d
        db_ref[...] = d.astype(BF16)

    return pl.pallas_call(
        body, name=name, grid=(S // tr,),
        in_specs=[pl.BlockSpec((tr, D), lambda i: (i, 0)), pl.BlockSpec((tr, D), lambda i: (i, 0))],
        out_specs=[pl.BlockSpec((8, 128), lambda i: (0, 0)),
                   pl.BlockSpec((tr, D), lambda i: (i, 0)),
                   pl.BlockSpec((tr, D), lambda i: (i, 0))],
        out_shape=[jax.ShapeDtypeStruct((8, 128), F32),
                   jax.ShapeDtypeStruct((S, D), F32),
                   jax.ShapeDtypeStruct((S, D), BF16)],
        compiler_params=_params(("arbitrary",)),
    )(y, target)


def _rms_bwd(name, x, dh, dres, g3, l):
    S, D = x.shape
    tr = _tile(S, 256, 16)

    def body(x_ref, dh_ref, dr_ref, g_ref, dx_ref, dxb_ref, dg_ref):
        @pl.when(pl.program_id(0) == 0)
        def _():
            dg_ref[...] = jnp.zeros_like(dg_ref)

        xv = x_ref[...]
        dhv = dh_ref[...]
        r = lax.rsqrt(jnp.mean(xv * xv, axis=-1, keepdims=True) + EPS)
        xhat = xv * r
        dhg = dhv * g_ref[...]
        dx = dr_ref[...] + r * (dhg - xhat * jnp.mean(dhg * xhat, axis=-1, keepdims=True))
        dx_ref[...] = dx
        dxb_ref[...] = dx.astype(BF16)
        dg_ref[...] += jnp.sum(dhv * xhat, axis=0, keepdims=True)

    return pl.pallas_call(
        body, name=name, grid=(S // tr,),
        in_specs=[pl.BlockSpec((tr, D), lambda i: (i, 0)), pl.BlockSpec((tr, D), lambda i: (i, 0)),
                  pl.BlockSpec((tr, D), lambda i: (i, 0)), _vec_spec(l, D)],
        out_specs=[pl.BlockSpec((tr, D), lambda i: (i, 0)), pl.BlockSpec((tr, D), lambda i: (i, 0)),
                   pl.BlockSpec((1, D), lambda i: (0, 0))],
        out_shape=[jax.ShapeDtypeStruct((S, D), F32), jax.ShapeDtypeStruct((S, D), BF16),
                   jax.ShapeDtypeStruct((1, D), F32)],
        compiler_params=_params(("arbitrary",)),
    )(x, dh, dres, g3)


def _glu_conv_bwd(name, dy, ap, cw, cb3, l, cfg):
    S, F = cfg["S"], cfg["F"]
    tc = _tile(F, 256, 128)
    nf = F // tc

    def body(dy_ref, g_ref, u_ref, wg_ref, wu_ref, bg_ref, bu_ref, dap_ref, dw_ref, db_ref):
        apg, apu = g_ref[...], u_ref[...]
        wg, wu = wg_ref[...], wu_ref[...]
        ag, g_dn, g_up = _conv3(apg, wg, bg_ref[...], S)
        au, u_dn, u_up = _conv3(apu, wu, bu_ref[...], S)
        sig = _sigmoid(ag)
        dyv = dy_ref[...]
        da_u = dyv * (ag * sig)
        da_g = (dyv * au) * (sig * (1.0 + ag * (1.0 - sig)))
        row = lax.broadcasted_iota(jnp.int32, (S, tc), 0)

        def back(da, w):
            nxt = jnp.where(row == S - 1, 0.0, pltpu.roll(da, S - 1, axis=0))
            prv = jnp.where(row == 0, 0.0, pltpu.roll(da, 1, axis=0))
            return (nxt * w[0:1] + da * w[1:2]) + prv * w[2:3]

        dap_ref[0] = back(da_g, wg).astype(BF16)
        dap_ref[1] = back(da_u, wu).astype(BF16)

        def wgrad(da, dn, t, up):
            return jnp.concatenate([jnp.sum(da * dn, axis=0, keepdims=True),
                                    jnp.sum(da * t, axis=0, keepdims=True),
                                    jnp.sum(da * up, axis=0, keepdims=True)], axis=0)

        dw_ref[0] = wgrad(da_g, g_dn, apg, g_up)
        dw_ref[1] = wgrad(da_u, u_dn, apu, u_up)
        db_ref[0] = jnp.sum(da_g, axis=0, keepdims=True)
        db_ref[1] = jnp.sum(da_u, axis=0, keepdims=True)

    return pl.pallas_call(
        body, name=name, grid=(nf,),
        in_specs=[pl.BlockSpec((S, tc), lambda j: (0, j)),
                  pl.BlockSpec((S, tc), lambda j: (0, j)),
                  pl.BlockSpec((S, tc), lambda j: (0, j + nf)),
                  pl.BlockSpec((None, 3, tc), lambda j: (l, 0, j)),
                  pl.BlockSpec((None, 3, tc), lambda j: (l, 0, j + nf)),
                  pl.BlockSpec((None, 1, tc), lambda j: (l, 0, j)),
                  pl.BlockSpec((None, 1, tc), lambda j: (l, 0, j + nf))],
        out_specs=[pl.BlockSpec((2, S, tc), lambda j: (0, 0, j)),
                   pl.BlockSpec((2, 3, tc), lambda j: (0, 0, j)),
                   pl.BlockSpec((2, 1, tc), lambda j: (0, 0, j))],
        out_shape=[jax.ShapeDtypeStruct((2, S, F), BF16),
                   jax.ShapeDtypeStruct((2, 3, F), F32),
                   jax.ShapeDtypeStruct((2, 1, F), F32)],
        compiler_params=_params(("arbitrary",)),
    )(dy, ap, ap, cw, cw, cb3, cb3)


def _attn_bwd(name, q, k, v, attn, dmix, sink, ag3, l, cfg):
    S, AW, KVW, NQ, NKV, G = cfg["S"], cfg["AW"], cfg["KVW"], cfg["NQ"], cfg["NKV"], cfg["G"]
    nb = S // HEAD
    scale = HEAD ** -0.5

    def body(q_ref, kp, kc, kn, vp, vc, vn, a_ref, dm_ref, sink_ref, ag_ref,
             dq_ref, dk_ref, dv_ref, dsink_ref, dag_ref, da_scr):
        n = pl.program_id(0)

        @pl.when(n == 0)
        def _():
            dk_ref[...] = jnp.zeros_like(dk_ref)
            dv_ref[...] = jnp.zeros_like(dv_ref)
            dsink_ref[...] = jnp.zeros_like(dsink_ref)
            dag_ref[...] = jnp.zeros_like(dag_ref)

        a = a_ref[...]
        dm = dm_ref[...]
        r = lax.rsqrt(jnp.mean(a * a, axis=-1, keepdims=True) + EPS)
        xhat = a * r
        dmg = dm * ag_ref[...]
        da_scr[...] = r * (dmg - xhat * jnp.mean(dmg * xhat, axis=-1, keepdims=True))
        dag_ref[...] += jnp.sum(dm * xhat, axis=0, keepdims=True)

        kb = jnp.concatenate([kp[...], kc[...], kn[...]], axis=0)
        vb = jnp.concatenate([vp[...], vc[...], vn[...]], axis=0)
        band = pl.ds(pl.multiple_of(n * HEAD, HEAD), 3 * HEAD)
        for j in range(NKV):
            sl = slice(j * HEAD, (j + 1) * HEAD)
            heads = [slice((j * G + g) * HEAD, (j * G + g + 1) * HEAD) for g in range(G)]
            qs = jnp.concatenate([q_ref[:, hs] for hs in heads], axis=0)
            do = jnp.concatenate([da_scr[:, hs] for hs in heads], axis=0)
            kj, vj = kb[:, sl], vb[:, sl]
            probs, p_sink = _attn_probs(qs, kj, n, _sink_rows(sink_ref, l, j, G), S, G)
            dob = do.astype(BF16)
            dprobs = lax.dot_general(dob, vj, (((1,), (1,)), ((), ())), preferred_element_type=F32)
            delta = jnp.sum(dprobs * probs, axis=-1, keepdims=True)
            ds = (probs * (dprobs - delta)) * scale
            dsb = ds.astype(BF16)
            dsk = -(p_sink * delta)
            dq = jnp.dot(dsb, kj, preferred_element_type=F32)
            for g in range(G):
                dq_ref[:, heads[g]] = dq[g * HEAD:(g + 1) * HEAD]
                part = jnp.sum(dsk[g * HEAD:(g + 1) * HEAD], axis=0, keepdims=True)
                dsink_ref[j * G + g:j * G + g + 1, :] += jnp.broadcast_to(part, (1, HEAD))
            dk_ref[band, sl] += lax.dot_general(dsb, qs, (((0,), (0,)), ((), ())), preferred_element_type=F32)
            dv_ref[band, sl] += lax.dot_general(probs.astype(BF16), dob, (((0,), (0,)), ((), ())),
                                                preferred_element_type=F32)

    return pl.pallas_call(
        body, name=name, grid=(nb,),
        in_specs=[pl.BlockSpec((HEAD, AW), lambda n: (n, 0))] + _band_specs(KVW, nb) + _band_specs(KVW, nb)
                 + [pl.BlockSpec((HEAD, AW), lambda n: (n, 0)),
                    pl.BlockSpec((HEAD, AW), lambda n: (n, 0)),
                    pl.BlockSpec(memory_space=pltpu.SMEM), _vec_spec(l, AW)],
        out_specs=[pl.BlockSpec((HEAD, AW), lambda n: (n, 0)),
                   pl.BlockSpec((S + 2 * HEAD, KVW), lambda n: (0, 0)),
                   pl.BlockSpec((S + 2 * HEAD, KVW), lambda n: (0, 0)),
                   pl.BlockSpec((NQ, HEAD), lambda n: (0, 0)),
                   pl.BlockSpec((1, AW), lambda n: (0, 0))],
        out_shape=[jax.ShapeDtypeStruct((S, AW), F32),
                   jax.ShapeDtypeStruct((S + 2 * HEAD, KVW), F32),
                   jax.ShapeDtypeStruct((S + 2 * HEAD, KVW), F32),
                   jax.ShapeDtypeStruct((NQ, HEAD), F32),
                   jax.ShapeDtypeStruct((1, AW), F32)],
        scratch_shapes=[pltpu.VMEM((HEAD, AW), F32)],
        compiler_params=_params(("arbitrary",)),
    )(q, k, k, k, v, v, v, attn, dmix, sink, ag3)


def _qkv_prep_bwd(name, z, dq, dk_pad, dv_pad, qg3, kg3, cos2, sin2, l, cfg):
    S, AW, KVW, NQ, NKV = cfg["S"], cfg["AW"], cfg["KVW"], cfg["NQ"], cfg["NKV"]
    kv_blk = (3 * AW) // (2 * KVW)

    def body(zq_ref, zkv_ref, dq_ref, dk_ref, dv_ref, qg_ref, kg_ref, c_ref, s_ref,
             dzq_ref, dzkv_ref, dqg_ref, dkg_ref):
        @pl.when(pl.program_id(0) == 0)
        def _():
            dqg_ref[...] = jnp.zeros_like(dqg_ref)
            dkg_ref[...] = jnp.zeros_like(dkg_ref)

        cosv, sinv = c_ref[...], s_ref[...]

        def back(t, dr, g):
            r = lax.rsqrt(jnp.mean(t * t, axis=-1, keepdims=True) + EPS)
            xhat = t * r
            dn = dr * cosv + pltpu.roll(dr * sinv, HEAD // 2, axis=1)
            dxh = dn * g
            dt = r * (dxh - xhat * jnp.mean(dxh * xhat, axis=-1, keepdims=True))
            return dt, jnp.sum(dn * xhat, axis=0, keepdims=True)

        gq = jnp.zeros((1, HEAD), F32)
        for h in range(NQ):
            sl = slice(h * HEAD, (h + 1) * HEAD)
            dt, gpart = back(zq_ref[:, sl], dq_ref[:, sl], qg_ref[...])
            dzq_ref[:, sl] = dt.astype(BF16)
            gq = gq + gpart
        dqg_ref[...] += gq
        gk = jnp.zeros((1, HEAD), F32)
        for h in range(NKV):
            sl = slice(h * HEAD, (h + 1) * HEAD)
            dt, gpart = back(zkv_ref[:, sl], dk_ref[:, sl], kg_ref[...])
            dzkv_ref[:, sl] = dt.astype(BF16)
            gk = gk + gpart
        dkg_ref[...] += gk
        dzkv_ref[:, KVW:] = dv_ref[...].astype(BF16)

    return pl.pallas_call(
        body, name=name, grid=(S // HEAD,),
        in_specs=[pl.BlockSpec((HEAD, AW), lambda i: (i, 0)),
                  pl.BlockSpec((HEAD, 2 * KVW), lambda i: (i, kv_blk)),
                  pl.BlockSpec((HEAD, AW), lambda i: (i, 0)),
                  pl.BlockSpec((HEAD, KVW), lambda i: (i + 1, 0)),
                  pl.BlockSpec((HEAD, KVW), lambda i: (i + 1, 0)),
                  _vec_spec(l, HEAD), _vec_spec(l, HEAD),
                  pl.BlockSpec((HEAD, HEAD), lambda i: (i, 0)),
                  pl.BlockSpec((HEAD, HEAD), lambda i: (i, 0))],
        out_specs=[pl.BlockSpec((HEAD, AW), lambda i: (i, 0)),
                   pl.BlockSpec((HEAD, 2 * KVW), lambda i: (i, 0)),
                   pl.BlockSpec((1, HEAD), lambda i: (0, 0)),
                   pl.BlockSpec((1, HEAD), lambda i: (0, 0))],
        out_shape=[jax.ShapeDtypeStruct((S, AW), BF16),
                   jax.ShapeDtypeStruct((S, 2 * KVW), BF16),
                   jax.ShapeDtypeStruct((1, HEAD), F32),
                   jax.ShapeDtypeStruct((1, HEAD), F32)],
        compiler_params=_params(("arbitrary",)),
    )(z, z, dq, dk_pad, dv_pad, qg3, kg3, cos2, sin2)


def _sgu_bwd(name, z, dmix, lng3, lnb3, ws_b, wst_b, bs_b, og3, l, cfg):
    S, GW, NG = cfg["S"], cfg["GW"], cfg["NG"]

    def body(gu_ref, gv_ref, dm_ref, lng_ref, lnb_ref, ws_ref, wst_ref, bs_ref, og_ref,
             dgu_ref, dgv_ref, dws_ref, dbs_ref, dlng_ref, dlnb_ref, dog_ref, sg_scr, f_scr, dvn_scr):
        @pl.when(pl.program_id(0) == 0)
        def _():
            dws_ref[...] = jnp.zeros_like(dws_ref)
            dbs_ref[...] = jnp.zeros_like(dbs_ref)
            dlng_ref[...] = jnp.zeros_like(dlng_ref)
            dlnb_ref[...] = jnp.zeros_like(dlnb_ref)
            dog_ref[...] = jnp.zeros_like(dog_ref)

        gu, gv = gu_ref[...], gv_ref[...]
        lng = lng_ref[...]
        u, xhat, rstd, vn = _sgu_forward_math(gu, gv, lng, lnb_ref[...])
        vnb = vn.astype(BF16)
        for h in range(NG):
            sl = slice(h * HEAD, (h + 1) * HEAD)
            f = jnp.dot(ws_ref[h], vnb[:, sl], preferred_element_type=F32) + bs_ref[h]
            f_scr[:, sl] = f
            sg_scr[:, sl] = u[:, sl] * f
        sg = sg_scr[...]
        dm = dm_ref[...]
        r = lax.rsqrt(jnp.mean(sg * sg, axis=-1, keepdims=True) + EPS)
        sghat = sg * r
        dmg = dm * og_ref[...]
        dsg = r * (dmg - sghat * jnp.mean(dmg * sghat, axis=-1, keepdims=True))
        dog_ref[...] += jnp.sum(dm * sghat, axis=0, keepdims=True)
        du = dsg * f_scr[...]
        df = dsg * u
        dfb = df.astype(BF16)
        for h in range(NG):
            sl = slice(h * HEAD, (h + 1) * HEAD)
            dvn_scr[:, sl] = jnp.dot(wst_ref[h], dfb[:, sl], preferred_element_type=F32)
            dws_ref[h] += lax.dot_general(dfb[:, sl], vnb[:, sl], (((1,), (1,)), ((), ())),
                                          preferred_element_type=F32)
            dbs_ref[h] += jnp.broadcast_to(jnp.sum(df[:, sl], axis=-1, keepdims=True), (HEAD, HEAD))
        dvn = dvn_scr[...]
        dlng_ref[...] += jnp.sum(dvn * xhat, axis=0, keepdims=True)
        dlnb_ref[...] += jnp.sum(dvn, axis=0, keepdims=True)
        dxh = dvn * lng
        dvv = rstd * ((dxh - jnp.mean(dxh, axis=-1, keepdims=True))
                      - xhat * jnp.mean(dxh * xhat, axis=-1, keepdims=True))
        dgu_ref[...] = (du * _gelu_grad(gu)).astype(BF16)
        dgv_ref[...] = (dvv * _gelu_grad(gv)).astype(BF16)

    vec = pl.BlockSpec((1, GW), lambda c: (0, 0))
    mat = pl.BlockSpec((NG, HEAD, HEAD), lambda c: (0, 0, 0))
    wsp = pl.BlockSpec((None, NG, HEAD, HEAD), lambda c: (l, 0, 0, 0))
    return pl.pallas_call(
        body, name=name, grid=(S // HEAD,),
        in_specs=[pl.BlockSpec((HEAD, GW), lambda c: (c, 1)),
                  pl.BlockSpec((HEAD, GW), lambda c: (c, 2)),
                  pl.BlockSpec((HEAD, GW), lambda c: (c, 1)),
                  _vec_spec(l, GW), _vec_spec(l, GW), wsp, wsp, wsp, _vec_spec(l, GW)],
        out_specs=[pl.BlockSpec((HEAD, GW), lambda c: (c, 0)), pl.BlockSpec((HEAD, GW), lambda c: (c, 0)),
                   mat, mat, vec, vec, vec],
        out_shape=[jax.ShapeDtypeStruct((S, GW), BF16), jax.ShapeDtypeStruct((S, GW), BF16),
                   jax.ShapeDtypeStruct((NG, HEAD, HEAD), F32), jax.ShapeDtypeStruct((NG, HEAD, HEAD), F32),
                   jax.ShapeDtypeStruct((1, GW), F32), jax.ShapeDtypeStruct((1, GW), F32),
                   jax.ShapeDtypeStruct((1, GW), F32)],
        scratch_shapes=[pltpu.VMEM((HEAD, GW), F32), pltpu.VMEM((HEAD, GW), F32), pltpu.VMEM((HEAD, GW), F32)],
        compiler_params=_params(("arbitrary",)),
    )(z, z, dmix, lng3, lnb3, ws_b, wst_b, bs_b, og3)


def _mesh_pos():
    x, y, c = lax.axis_index("x"), lax.axis_index("y"), lax.axis_index("c")
    return x, y, c


def _dev_index(p):
    return 4 * p[0] + 2 * p[1] + p[2]


def _all_gather(shards, out_shapes, slabs):
    na = len(shards)

    def body(*refs):
        ins, outs = refs[:na], refs[na:2 * na]
        send_sems, recv_sems, local_sems = refs[2 * na:]
        x, y, c = _mesh_pos()
        me, sib = (x, y, c), (x, y, 1 - c)
        chips = [(1 - x, y), (x, 1 - y), (1 - x, 1 - y)]

        def copy(a, k, block, to, src=None):
            dst = slabs[a](outs[a], _dev_index(block))
            return pltpu.make_async_remote_copy(
                src_ref=dst if src is None else src, dst_ref=dst,
                send_sem=send_sems.at[a, k], recv_sem=recv_sems.at[a, k],
                device_id=to, device_id_type=MESH)

        local = [pltpu.make_async_copy(ins[a], slabs[a](outs[a], _dev_index(me)), local_sems.at[a])
                 for a in range(na)]
        for cp in local:
            cp.start()
        first = []
        for a in range(na):
            first.append(copy(a, 0, me, sib, src=ins[a]))
            first += [copy(a, 1 + j, me, (*chip, c), src=ins[a]) for j, chip in enumerate(chips)]
        for cp in first:
            cp.start()
        passed = []
        for j, chip in enumerate(chips):
            for a in range(na):
                copy(a, 1 + j, (*chip, c), me).wait_recv()
                fwd = copy(a, 4 + j, (*chip, c), sib)
                fwd.start()
                passed.append(fwd)
        for a in range(na):
            copy(a, 0, sib, me).wait_recv()
            for j, chip in enumerate(chips):
                copy(a, 4 + j, (*chip, 1 - c), me).wait_recv()
        for cp in first + passed:
            cp.wait_send()
        for cp in local:
            cp.wait()

    any_spec = pl.BlockSpec(memory_space=pl.ANY)
    return pl.pallas_call(
        body, name="ag",
        in_specs=[any_spec] * na, out_specs=[any_spec] * na,
        out_shape=out_shapes,
        scratch_shapes=[pltpu.SemaphoreType.DMA((na, 7)), pltpu.SemaphoreType.DMA((na, 7)),
                        pltpu.SemaphoreType.DMA((na,))],
    )(*shards)


def _rs_d2d(sends, small):
    nj = len(sends)

    def body(*refs):
        ins, outs = refs[:nj + 1], refs[nj + 1:2 * nj + 2]
        send_sems, recv_sems = refs[2 * nj + 2:]
        x, y, c = _mesh_pos()
        copies = []
        for t in range(nj + 1):
            cp = pltpu.make_async_remote_copy(
                src_ref=ins[t].at[:, 1 - c] if t < nj else ins[t], dst_ref=outs[t],
                send_sem=send_sems.at[t], recv_sem=recv_sems.at[t],
                device_id=(x, y, 1 - c), device_id_type=MESH)
            cp.start()
            copies.append(cp)
        for cp in copies:
            cp.wait()

    any_spec = pl.BlockSpec(memory_space=pl.ANY)
    shapes = [jax.ShapeDtypeStruct((g.shape[0],) + g.shape[2:], g.dtype) for g in sends]
    shapes.append(jax.ShapeDtypeStruct(small.shape, small.dtype))
    return pl.pallas_call(
        body, name="rs_d2d",
        in_specs=[any_spec] * (nj + 1), out_specs=[any_spec] * (nj + 1), out_shape=shapes,
        scratch_shapes=[pltpu.SemaphoreType.DMA((nj + 1,)), pltpu.SemaphoreType.DMA((nj + 1,))],
    )(*sends, small)


def _pair_sum(name, g4, recv, cvec, row_mult):
    Q, _, R, C = g4.shape
    tr = _tile(R, 256, row_mult)

    def body(c_ref, a_ref, b_ref, o_ref):
        o_ref[...] = (a_ref[...].astype(F32) + b_ref[...].astype(F32)).astype(BF16)

    grid_spec = pltpu.PrefetchScalarGridSpec(
        num_scalar_prefetch=1, grid=(Q, R // tr),
        in_specs=[pl.BlockSpec((None, None, tr, C), lambda q, i, c_ref: (q, c_ref[0], i, 0)),
                  pl.BlockSpec((None, tr, C), lambda q, i, c_ref: (q, i, 0))],
        out_specs=pl.BlockSpec((None, tr, C), lambda q, i, c_ref: (q, i, 0)))
    return pl.pallas_call(
        body, name=name, grid_spec=grid_spec, out_shape=jax.ShapeDtypeStruct((Q, R, C), BF16),
        compiler_params=_params(("arbitrary", "arbitrary")),
    )(cvec, g4, recv)


def _add2(name, a, b):
    R, C = a.shape
    tr = _tile(R, 512, 8)

    def body(a_ref, b_ref, o_ref):
        o_ref[...] = a_ref[...] + b_ref[...]

    blk = pl.BlockSpec((tr, C), lambda i: (i, 0))
    return pl.pallas_call(body, name=name, grid=(R // tr,), in_specs=[blk, blk], out_specs=blk,
                          out_shape=jax.ShapeDtypeStruct((R, C), a.dtype),
                          compiler_params=_params(("arbitrary",)))(a, b)


def _rs_ici(chipsums, jobs, small, out_shapes):
    nj = len(chipsums)

    def body(*refs):
        ins, outs = refs[:nj + 1], refs[nj + 1:nj + 1 + len(out_shapes)]
        send_sems, recv_sems, local_sems = refs[nj + 1 + len(out_shapes):]
        x, y, c = _mesh_pos()
        q_me = 2 * x + y
        copies = []
        for t in range(nj + 1):
            if t < nj:
                ri, l = jobs[t]
                src_of = lambda q, t=t: ins[t].at[q]
                dst = outs[ri].at[l, q_me]
            else:
                src_of = lambda q, t=t: ins[t]
                dst = outs[len(out_shapes) - 1].at[q_me]
            loc = pltpu.make_async_copy(src_of(q_me), dst, local_sems.at[t])
            loc.start()
            copies.append(loc)
            for k in range(1, 4):
                px, py = x ^ (k >> 1), y ^ (k & 1)
                cp = pltpu.make_async_remote_copy(
                    src_ref=src_of(2 * px + py), dst_ref=dst,
                    send_sem=send_sems.at[t, k - 1], recv_sem=recv_sems.at[t, k - 1],
                    device_id=(px, py, c), device_id_type=MESH)
                cp.start()
                copies.append(cp)
        for cp in copies:
            cp.wait()

    any_spec = pl.BlockSpec(memory_space=pl.ANY)
    return pl.pallas_call(
        body, name="rs_ici",
        in_specs=[any_spec] * (nj + 1), out_specs=[any_spec] * len(out_shapes), out_shape=out_shapes,
        scratch_shapes=[pltpu.SemaphoreType.DMA((nj + 1, 3)), pltpu.SemaphoreType.DMA((nj + 1, 3)),
                        pltpu.SemaphoreType.DMA((nj + 1,))],
    )(*chipsums, small)


def _adamw_math(w, g, m, v):
    m2 = ADAM_B1 * m + (1.0 - ADAM_B1) * g
    v2 = ADAM_B2 * v + (1.0 - ADAM_B2) * (g * g)
    m_hat = m2 / (1.0 - ADAM_B1 ** ADAM_STEP)
    v_hat = v2 / (1.0 - ADAM_B2 ** ADAM_STEP)
    delta = -ADAM_LR * (m_hat / (jnp.sqrt(v_hat) + ADAM_EPS) + ADAM_WD * w)
    return delta, m2, v2


def _adamw_sum(name, recv, w, m, v, row_mult):
    L, NS, R, C = recv.shape
    tr = _tile(R, 128, row_mult)

    def body(r_ref, w_ref, m_ref, v_ref, g_ref, d_ref, nm_ref, nv_ref):
        g = r_ref[0].astype(F32)
        for s in range(1, NS):
            g = g + r_ref[s].astype(F32)
        d, m2, v2 = _adamw_math(w_ref[...], g, m_ref[...], v_ref[...])
        g_ref[...] = g
        d_ref[...] = d
        nm_ref[...] = m2
        nv_ref[...] = v2

    blk = pl.BlockSpec((None, tr, C), lambda l, i: (l, i, 0))
    shp = jax.ShapeDtypeStruct((L, R, C), F32)
    return pl.pallas_call(
        body, name=name, grid=(L, R // tr),
        in_specs=[pl.BlockSpec((None, NS, tr, C), lambda l, i: (l, 0, i, 0)), blk, blk, blk],
        out_specs=[blk, blk, blk, blk],
        out_shape=[shp, shp, shp, shp],
        compiler_params=_params(("arbitrary", "arbitrary")),
    )(recv, w, m, v)


def _adamw_plain(name, g, w, m, v):
    def body(g_ref, w_ref, m_ref, v_ref, d_ref, nm_ref, nv_ref):
        d, m2, v2 = _adamw_math(w_ref[...], g_ref[...], m_ref[...], v_ref[...])
        d_ref[...] = d
        nm_ref[...] = m2
        nv_ref[...] = v2

    shp = jax.ShapeDtypeStruct(g.shape, F32)
    return pl.pallas_call(body, name=name, out_shape=[shp, shp, shp], compiler_params=_params())(g, w, m, v)


SMALL = ["norm1_g", "q_norm_g", "k_norm_g", "sink", "sgu_ln_g", "sgu_ln_b", "w_s", "b_s",
         "attn_out_g", "sgu_out_g", "norm2_g", "conv_b"]
PACK_ALIGN = 1024


def _pack(pieces):
    flat = []
    for p in pieces:
        f = p.reshape(-1).astype(F32)
        pad = (-f.shape[0]) % PACK_ALIGN
        flat.append(jnp.pad(f, (0, pad)) if pad else f)
    return jnp.concatenate(flat).reshape(-1, 128)


def _unpack(packed, shapes):
    flat = packed.reshape(-1)
    out, off = [], 0
    for shp in shapes:
        n = 1
        for d in shp:
            n *= d
        out.append(flat[off:off + n].reshape(shp))
        off += n + ((-n) % PACK_ALIGN)
    return out


def kernel(x, norm1_g, w_in, q_norm_g, k_norm_g, sink, sgu_ln_g, sgu_ln_b, w_s, b_s, attn_out_g, sgu_out_g, w_o, norm2_g, w_up, conv_w, conv_b, w_down, loss_target, m_norm1_g, m_w_in, m_q_norm_g, m_k_norm_g, m_sink, m_sgu_ln_g, m_sgu_ln_b, m_w_s, m_b_s, m_attn_out_g, m_sgu_out_g, m_w_o, m_norm2_g, m_w_up, m_conv_w, m_conv_b, m_w_down, v_norm1_g, v_w_in, v_q_norm_g, v_k_norm_g, v_sink, v_sgu_ln_g, v_sgu_ln_b, v_w_s, v_b_s, v_attn_out_g, v_sgu_out_g, v_w_o, v_norm2_g, v_w_up, v_conv_w, v_conv_b, v_w_down):
    weights = dict(norm1_g=norm1_g, w_in=w_in, q_norm_g=q_norm_g, k_norm_g=k_norm_g, sink=sink, sgu_ln_g=sgu_ln_g,
                   sgu_ln_b=sgu_ln_b, w_s=w_s, b_s=b_s, attn_out_g=attn_out_g, sgu_out_g=sgu_out_g, w_o=w_o,
                   norm2_g=norm2_g, w_up=w_up, conv_w=conv_w, conv_b=conv_b, w_down=w_down)
    mom_m = dict(norm1_g=m_norm1_g, w_in=m_w_in, q_norm_g=m_q_norm_g, k_norm_g=m_k_norm_g, sink=m_sink,
                 sgu_ln_g=m_sgu_ln_g, sgu_ln_b=m_sgu_ln_b, w_s=m_w_s, b_s=m_b_s, attn_out_g=m_attn_out_g,
                 sgu_out_g=m_sgu_out_g, w_o=m_w_o, norm2_g=m_norm2_g, w_up=m_w_up, conv_w=m_conv_w,
                 conv_b=m_conv_b, w_down=m_w_down)
    mom_v = dict(norm1_g=v_norm1_g, w_in=v_w_in, q_norm_g=v_q_norm_g, k_norm_g=v_k_norm_g, sink=v_sink,
                 sgu_ln_g=v_sgu_ln_g, sgu_ln_b=v_sgu_ln_b, w_s=v_w_s, b_s=v_b_s, attn_out_g=v_attn_out_g,
                 sgu_out_g=v_sgu_out_g, w_o=v_w_o, norm2_g=v_norm2_g, w_up=v_w_up, conv_w=v_conv_w,
                 conv_b=v_conv_b, w_down=v_w_down)
    order = ["norm1_g", "w_in", "q_norm_g", "k_norm_g", "sink", "sgu_ln_g", "sgu_ln_b", "w_s", "b_s",
             "attn_out_g", "sgu_out_g", "w_o", "norm2_g", "w_up", "conv_w", "conv_b", "w_down"]

    _, S, D = x.shape
    L = w_in.shape[0]
    AW = D // 2
    NQ = AW // HEAD
    NKV = max(1, NQ // 4)
    G = NQ // NKV
    KVW = NKV * HEAD
    GW = D - AW
    NG = GW // HEAD
    IN = AW + 2 * KVW + 2 * GW
    INS = w_in.shape[2]
    OS = w_o.shape[1]
    US = w_up.shape[2]
    DS = w_down.shape[1]
    F2 = US * N_DEV
    F = F2 // 2
    assert INS * N_DEV == IN and OS * N_DEV == D and DS * N_DEV == F and AW == GW and (3 * AW) % (2 * KVW) == 0
    cfg = dict(S=S, D=D, AW=AW, NQ=NQ, NKV=NKV, G=G, KVW=KVW, GW=GW, NG=NG, F=F, F2=F2)

    def rows_slab(n):
        return lambda ref, idx: ref.at[:, pl.ds(pl.multiple_of(idx * n, n), n), :]

    def cols_slab(n):
        return lambda ref, idx: ref.at[:, :, pl.ds(pl.multiple_of(idx * n, n), n)]

    def lead_slab(ref, idx):
        return ref.at[idx]

    gathered = _all_gather(
        [w_in.astype(BF16), w_o.astype(BF16), w_up.astype(BF16), w_down.astype(BF16), conv_w],
        [jax.ShapeDtypeStruct((N_DEV, L, D, INS), BF16), jax.ShapeDtypeStruct((L, D, D), BF16),
         jax.ShapeDtypeStruct((L, D, F2), BF16), jax.ShapeDtypeStruct((L, F, D), BF16),
         jax.ShapeDtypeStruct((L, 3, F2), F32)],
        [lead_slab, rows_slab(OS), cols_slab(US), rows_slab(DS), cols_slab(US)])
    w_in_sh, w_o_f, w_up_f, w_down_f, conv_w_f = gathered
    w_in_f = jnp.transpose(w_in_sh, (1, 2, 0, 3)).reshape(L, D, IN)
    w_in_p = jnp.concatenate([w_in_f[:, :, :AW], w_in_f[:, :, AW + 2 * KVW:], w_in_f[:, :, AW:AW + 2 * KVW]], axis=2)

    n1g3, n2g3 = norm1_g.reshape(L, 1, D), norm2_g.reshape(L, 1, D)
    qg3, kg3 = q_norm_g.reshape(L, 1, HEAD), k_norm_g.reshape(L, 1, HEAD)
    lng3, lnb3 = sgu_ln_g.reshape(L, 1, GW), sgu_ln_b.reshape(L, 1, GW)
    ag3, og3 = attn_out_g.reshape(L, 1, AW), sgu_out_g.reshape(L, 1, GW)
    cb3 = conv_b.reshape(L, 1, F2)
    ws_b = w_s.astype(BF16)
    wst_b = jnp.swapaxes(w_s, 2, 3).astype(BF16)
    bs_b = jnp.broadcast_to(b_s[..., None], (L, NG, HEAD, HEAD))
    inv_freq = ROPE_THETA ** (-jnp.arange(0, HEAD, 2, dtype=F32) / HEAD)
    ang = jnp.arange(S, dtype=F32)[:, None] * inv_freq[None, :]
    cos2 = jnp.concatenate([jnp.cos(ang), jnp.cos(ang)], axis=1)
    sin2 = jnp.concatenate([-jnp.sin(ang), jnp.sin(ang)], axis=1)

    tn = 512
    t_in, t_d, t_f, t_f2 = _tile(IN, tn, 128), _tile(D, tn, 128), _tile(F, tn, 128), _tile(F2, tn, 128)
    tk_f = _tile(F, 1408, 128)
    tk_in = _tile(IN, 1792, 128)
    tm_f = _tile(F, 512, 128)

    def w_spec(l, tk, tn_):
        return pl.BlockSpec((None, tk, tn_), lambda i, j, k: (l, k, j))

    def wt_spec(l, tn_, tk):
        return pl.BlockSpec((None, tn_, tk), lambda i, j, k: (l, j, k))

    def a_spec(tm, tk):
        return pl.BlockSpec((tm, tk), lambda i, j, k: (i, k))

    def at_spec(tk, tm):
        return pl.BlockSpec((tk, tm), lambda i, j, k: (k, i))

    def b_spec(tk, tn_):
        return pl.BlockSpec((tk, tn_), lambda i, j, k: (k, j))

    xs = x.reshape(S, D)
    saved = []
    cur = xs
    for l in range(L):
        h = _rms_fwd("rms1_fwd", cur, n1g3, l)
        z = _mm("mm_in", h, w_in_p, M=S, N=IN, K=D, tm=S, tn=t_in, tk=D,
                a_spec=a_spec(S, D), b_spec=w_spec(l, D, t_in))
        q_r, k_r, v_b = _qkv_prep("qkv_prep", z, qg3, kg3, cos2, sin2, l, cfg)
        attn, mix_l = _attn_fwd("attn_fwd", q_r, k_r, v_b, sink, ag3, l, cfg)
        mix_r = _sgu_fwd("sgu_fwd", z, lng3, lnb3, ws_b, bs_b, og3, l, cfg)
        mixed = jnp.concatenate([mix_l, mix_r], axis=1)
        x1 = _mm("mm_o", mixed, w_o_f, M=S, N=D, K=D, tm=S, tn=t_d, tk=D,
                 a_spec=a_spec(S, D), b_spec=w_spec(l, D, t_d), res=cur)
        h2 = _rms_fwd("rms2_fwd", x1, n2g3, l)
        ap = _mm("mm_up", h2, w_up_f, M=S, N=F2, K=D, tm=S, tn=t_f2, tk=D,
                 a_spec=a_spec(S, D), b_spec=w_spec(l, D, t_f2))
        y_b = _conv_glu_fwd("conv_glu_fwd", ap, conv_w_f, cb3, l, cfg)
        x2 = _mm("mm_down", y_b, w_down_f, M=S, N=D, K=F, tm=S, tn=t_d, tk=tk_f,
                 a_spec=a_spec(S, tk_f), b_spec=w_spec(l, tk_f, t_d), res=x1)
        saved.append(dict(x=cur, h=h, z=z, q=q_r, k=k_r, v=v_b, attn=attn, mixed=mixed, x1=x1, h2=h2, ap=ap, y=y_b))
        cur = x2

    loss_tile, dx, dxb = _loss_bwd("loss", cur, loss_target.reshape(S, D))
    loss = lax.psum(loss_tile[0, 0], ("x", "y", "c"))

    gW = [dict() for _ in range(L)]
    gS = [dict() for _ in range(L)]
    nkf = F // tk_f
    for l in reversed(range(L)):
        sv = saved[l]
        dy = _mm("mm_dy", dxb, w_down_f, M=S, N=F, K=D, tm=S, tn=t_f, tk=D, tb=True,
                 a_spec=a_spec(S, D), b_spec=wt_spec(l, t_f, D))
        gW[l]["w_down"] = _mm("mm_gdown", sv["y"], dxb, M=F, N=D, K=S, tm=tm_f, tn=D, tk=S, ta=True,
                              a_spec=at_spec(S, tm_f), b_spec=b_spec(S, D), out_dtype=BF16)
        dap3, dcw, dcb = _glu_conv_bwd("glu_conv_bwd", dy, sv["ap"], conv_w_f, cb3, l, cfg)
        gS[l]["conv_w"] = jnp.concatenate([dcw[0], dcw[1]], axis=1)
        gS[l]["conv_b"] = jnp.concatenate([dcb[0], dcb[1]], axis=1).reshape(F2)
        dh2 = _mm("mm_dh2", dap3, w_up_f, M=S, N=D, K=F2, tm=S, tn=t_d, tk=tk_f, tb=True,
                  a_spec=pl.BlockSpec((None, S, tk_f), lambda i, j, k: (k // nkf, 0, k % nkf)),
                  b_spec=wt_spec(l, t_d, tk_f))
        gW[l]["w_up"] = _mm("mm_gup", sv["h2"], dap3, M=D, N=F2, K=S, tm=D, tn=US, tk=S, ta=True,
                            a_spec=at_spec(S, D),
                            b_spec=pl.BlockSpec((None, S, US), lambda i, j, k: (j // (N_DEV // 2), 0, j % (N_DEV // 2))),
                            out_dtype=BF16, out_shape=(N_DEV, D, US),
                            out_spec=pl.BlockSpec((None, D, US), lambda i, j, k: (j, 0, 0)))
        dx1, dx1b, dg2 = _rms_bwd("rms2_bwd", sv["x1"], dh2, dx, n2g3, l)
        gS[l]["norm2_g"] = dg2.reshape(D)
        dmix = _mm("mm_dmix", dx1b, w_o_f, M=S, N=D, K=D, tm=S, tn=t_d, tk=D, tb=True,
                   a_spec=a_spec(S, D), b_spec=wt_spec(l, t_d, D))
        gW[l]["w_o"] = _mm("mm_go", sv["mixed"], dx1b, M=D, N=D, K=S, tm=D, tn=t_d, tk=S, ta=True,
                           a_spec=at_spec(S, D), b_spec=b_spec(S, t_d), out_dtype=BF16)
        dq_r, dk_pad, dv_pad, dsink, dag = _attn_bwd("attn_bwd", sv["q"], sv["k"], sv["v"], sv["attn"], dmix,
                                                     sink, ag3, l, cfg)
        gS[l]["sink"] = dsink[:, 0]
        gS[l]["attn_out_g"] = dag.reshape(AW)
        dzgu, dzgv, dws, dbs, dlng, dlnb, dog = _sgu_bwd("sgu_bwd", sv["z"], dmix, lng3, lnb3, ws_b, wst_b, bs_b,
                                                        og3, l, cfg)
        gS[l]["w_s"] = dws
        gS[l]["b_s"] = dbs[:, :, 0]
        gS[l]["sgu_ln_g"] = dlng.reshape(GW)
        gS[l]["sgu_ln_b"] = dlnb.reshape(GW)
        gS[l]["sgu_out_g"] = dog.reshape(GW)
        dzq, dzkv, dqg, dkg = _qkv_prep_bwd("qkv_prep_bwd", sv["z"], dq_r, dk_pad, dv_pad, qg3, kg3, cos2, sin2,
                                            l, cfg)
        gS[l]["q_norm_g"] = dqg.reshape(HEAD)
        gS[l]["k_norm_g"] = dkg.reshape(HEAD)
        dz = jnp.concatenate([dzq, dzgu, dzgv, dzkv], axis=1)
        dh = _mm("mm_dh", dz, w_in_p, M=S, N=D, K=IN, tm=S, tn=t_d, tk=tk_in, tb=True,
                 a_spec=a_spec(S, tk_in), b_spec=wt_spec(l, t_d, tk_in))
        gin_p = _mm("mm_gin", sv["h"], dz, M=D, N=IN, K=S, tm=D, tn=t_in, tk=S, ta=True,
                    a_spec=at_spec(S, D), b_spec=b_spec(S, t_in), out_dtype=BF16)
        gin = jnp.concatenate([gin_p[:, :AW], gin_p[:, AW + 2 * GW:], gin_p[:, AW:AW + 2 * GW]], axis=1)
        gW[l]["w_in"] = jnp.transpose(gin.reshape(D, N_DEV, INS), (1, 0, 2))
        dx, dxb, dg1 = _rms_bwd("rms1_bwd", sv["x"], dh, dx1, n1g3, l)
        gS[l]["norm1_g"] = dg1.reshape(D)
    grad_x = dx.reshape(1, S, D)

    small_pieces = []
    for l in range(L):
        small_pieces += [gS[l][n] for n in SMALL] + [gS[l]["conv_w"]]
    small_send = _pack(small_pieces)
    SR = small_send.shape[0]

    big = [("w_in", D, INS), ("w_o", OS, D), ("w_up", D, US), ("w_down", DS, D)]
    sends, jobs = [], []
    for l in range(L):
        for ri, (name, R, C) in enumerate(big):
            jobs.append((ri, l))
            sends.append(gW[l][name].reshape(N_DEV // 2, 2, R, C))
    halves = _rs_d2d(sends, small_send)
    cvec = jnp.reshape(lax.axis_index("c"), (1,)).astype(jnp.int32)
    chipsums = [_pair_sum("pair_sum", g4, got, cvec, 16) for g4, got in zip(sends, halves[:-1])]
    small_chip = _add2("pair_sum_small", small_send, halves[-1])
    recv_in, recv_o, recv_up, recv_down, recv_small = _rs_ici(
        chipsums, jobs, small_chip,
        [jax.ShapeDtypeStruct((L, N_DEV // 2, R, C), BF16) for _, R, C in big]
        + [jax.ShapeDtypeStruct((N_DEV // 2, SR, 128), F32)])

    grads, deltas, new_m, new_v = {}, {}, {}, {}
    for name, recv in [("w_in", recv_in), ("w_o", recv_o), ("w_up", recv_up), ("w_down", recv_down)]:
        grads[name], deltas[name], new_m[name], new_v[name] = _adamw_sum(
            "adamw_" + name, recv, weights[name], mom_m[name], mom_v[name], 16)

    small_shapes = []
    for l in range(L):
        small_shapes += [weights[n].shape[1:] for n in SMALL] + [(3, F2)]

    def pack_small(src):
        pieces = []
        for l in range(L):
            pieces += [src[n][l] for n in SMALL] + [jnp.zeros((3, F2), F32)]
        return _pack(pieces)

    tr_s = _tile(SR, 512, 8)

    def small_body(r_ref, w_ref, m_ref, v_ref, g_ref, d_ref, nm_ref, nv_ref):
        g = r_ref[0]
        for s in range(1, N_DEV // 2):
            g = g + r_ref[s]
        d, m2, v2 = _adamw_math(w_ref[...], g, m_ref[...], v_ref[...])
        g_ref[...] = g
        d_ref[...] = d
        nm_ref[...] = m2
        nv_ref[...] = v2

    sblk = pl.BlockSpec((tr_s, 128), lambda i: (i, 0))
    sshp = jax.ShapeDtypeStruct((SR, 128), F32)
    sm_g, sm_d, sm_m, sm_v = pl.pallas_call(
        small_body, name="adamw_small", grid=(SR // tr_s,),
        in_specs=[pl.BlockSpec((N_DEV // 2, tr_s, 128), lambda i: (0, i, 0)), sblk, sblk, sblk],
        out_specs=[sblk, sblk, sblk, sblk], out_shape=[sshp, sshp, sshp, sshp],
        compiler_params=_params(("arbitrary",)),
    )(recv_small, pack_small(weights), pack_small(mom_m), pack_small(mom_v))

    per = len(SMALL) + 1
    for store, packed in [(grads, sm_g), (deltas, sm_d), (new_m, sm_m), (new_v, sm_v)]:
        parts = _unpack(packed, small_shapes)
        for i, n in enumerate(SMALL):
            store[n] = jnp.stack([parts[l * per + i] for l in range(L)])
        if store is grads:
            conv_full = jnp.stack([parts[l * per + len(SMALL)] for l in range(L)])
    me_i = _dev_index(_mesh_pos())
    g_cw = lax.dynamic_slice_in_dim(conv_full, me_i * US, US, axis=2)
    grads["conv_w"] = g_cw
    d_cw, m_cw, v_cw = _adamw_plain("adamw_conv_w", g_cw.reshape(L * 3, US), conv_w.reshape(L * 3, US),
                                    m_conv_w.reshape(L * 3, US), v_conv_w.reshape(L * 3, US))
    deltas["conv_w"] = d_cw.reshape(L, 3, US)
    new_m["conv_w"] = m_cw.reshape(L, 3, US)
    new_v["conv_w"] = v_cw.reshape(L, 3, US)

    return (loss, grad_x, *[grads[n] for n in order], *[deltas[n] for n in order],
            *[new_m[n] for n in order], *[new_v[n] for n in order])
```

```python
import jax
import jax.numpy as jnp
from jax import lax
from jax.experimental import pallas as pl
from jax.experimental.pallas import tpu as pltpu
from jax.experimental.pallas import tpu_sc as plsc

F32 = jnp.float32
BF16 = jnp.bfloat16
MESH = pl.DeviceIdType.MESH

N_DEV = 8
HEAD = 128
EPS = 1e-6
MASK_VALUE = -1e30
ROPE_THETA = 10000.0
GELU_C = 0.7978845608028654
GELU_A = 0.044715

ADAM_LR = 0.001
ADAM_B1 = 0.9
ADAM_B2 = 0.999
ADAM_EPS = 1e-08
ADAM_WD = 0.01
ADAM_STEP = 10

VMEM_LIMIT = 56 * 1024 * 1024
AG_COLLECTIVE_ID = 0


def _tile(n, pref, mult):
    best = None
    for t in range(mult, min(n, pref) + 1, mult):
        if n % t == 0:
            best = t
    return n if best is None else best


def _params(sem=None):
    kw = dict(vmem_limit_bytes=VMEM_LIMIT)
    if sem is not None:
        kw["dimension_semantics"] = sem
    return pltpu.CompilerParams(**kw)


def _gelu(x):
    return x * (0.5 * (1.0 + jnp.tanh(GELU_C * (x + GELU_A * (x * x * x)))))


def _gelu_grad(x):
    t = jnp.tanh(GELU_C * (x + GELU_A * (x * x * x)))
    return 0.5 * (1.0 + t) + 0.5 * x * (1.0 - t * t) * (GELU_C * (1.0 + 3.0 * GELU_A * (x * x)))


def _sigmoid(x):
    return 1.0 / (1.0 + jnp.exp(-x))


def _vec_spec(l, n):
    return pl.BlockSpec((None, 1, n), lambda *_: (l, 0, 0))


def _mm(name, a, b, *, M, N, K, tm, tn, tk, a_spec, b_spec, ta=False, tb=False, out_dtype=F32, res=None,
        out_shape=None, out_spec=None):
    nm, nn, nk = M // tm, N // tn, K // tk
    assert nm * tm == M and nn * tn == N and nk * tk == K
    assert not (ta and nk > 1)
    dims = (((1,), (1,)), ((), ())) if tb else (((1,), (0,)), ((), ()))

    def body(*refs):
        refs = list(refs)
        a_ref = refs.pop(0)
        b_ref = refs.pop(0)
        r_ref = refs.pop(0) if res is not None else None
        o_ref = refs.pop(0)
        acc = refs.pop(0) if nk > 1 else None
        at = refs.pop(0) if ta else None
        k = pl.program_id(2)
        if ta:
            @pl.when(pl.program_id(1) == 0)
            def _():
                at[...] = a_ref[...].T
            lhs = at[...]
        else:
            lhs = a_ref[...]
        p = lax.dot_general(lhs, b_ref[...], dims, preferred_element_type=F32)

        def finish(r):
            if r_ref is not None:
                r = r_ref[...] + r
            o_ref[...] = r.astype(out_dtype)

        if nk == 1:
            finish(p)
        else:
            @pl.when(k == 0)
            def _():
                acc[...] = p

            @pl.when(k > 0)
            def _():
                acc[...] += p

            @pl.when(k == nk - 1)
            def _():
                finish(acc[...])

    in_specs = [a_spec, b_spec]
    args = [a, b]
    if res is not None:
        in_specs.append(pl.BlockSpec((tm, tn), lambda i, j, k: (i, j)))
        args.append(res)
    scratch = []
    if nk > 1:
        scratch.append(pltpu.VMEM((tm, tn), F32))
    if ta:
        scratch.append(pltpu.VMEM((tm, tk), BF16))
    return pl.pallas_call(
        body, name=name, grid=(nm, nn, nk),
        in_specs=in_specs,
        out_specs=pl.BlockSpec((tm, tn), lambda i, j, k: (i, j)) if out_spec is None else out_spec,
        out_shape=jax.ShapeDtypeStruct((M, N) if out_shape is None else out_shape, out_dtype),
        scratch_shapes=scratch,
        compiler_params=_params(("arbitrary", "arbitrary", "arbitrary")),
    )(*args)


def _rms_fwd(name, x, g3, l):
    S, D = x.shape
    tr = _tile(S, 256, 16)

    def body(x_ref, g_ref, h_ref):
        xv = x_ref[...]
        r = lax.rsqrt(jnp.mean(xv * xv, axis=-1, keepdims=True) + EPS)
        h_ref[...] = ((xv * r) * g_ref[...]).astype(BF16)

    return pl.pallas_call(
        body, name=name, grid=(S // tr,),
        in_specs=[pl.BlockSpec((tr, D), lambda i: (i, 0)), _vec_spec(l, D)],
        out_specs=pl.BlockSpec((tr, D), lambda i: (i, 0)),
        out_shape=jax.ShapeDtypeStruct((S, D), BF16),
        compiler_params=_params(("arbitrary",)),
    )(x, g3)


def _rope(t, cos2, sin2):
    return t * cos2 + pltpu.roll(t, HEAD // 2, axis=1) * sin2


def _qkv_prep(name, z, qg3, kg3, cos2, sin2, l, cfg):
    S, AW, KVW, NQ, NKV = cfg["S"], cfg["AW"], cfg["KVW"], cfg["NQ"], cfg["NKV"]
    tr = _tile(S, 256, 16)
    kv_blk = (3 * AW) // (2 * KVW)

    def body(zq_ref, zkv_ref, qg_ref, kg_ref, c_ref, s_ref, q_ref, k_ref, v_ref):
        cosv, sinv = c_ref[...], s_ref[...]

        def norm_rope(t, g):
            r = lax.rsqrt(jnp.mean(t * t, axis=-1, keepdims=True) + EPS)
            return _rope((t * r) * g, cosv, sinv)

        for h in range(NQ):
            sl = slice(h * HEAD, (h + 1) * HEAD)
            q_ref[:, sl] = norm_rope(zq_ref[:, sl], qg_ref[...]).astype(BF16)
        for h in range(NKV):
            sl = slice(h * HEAD, (h + 1) * HEAD)
            k_ref[:, sl] = norm_rope(zkv_ref[:, sl], kg_ref[...]).astype(BF16)
        v_ref[...] = zkv_ref[:, KVW:].astype(BF16)

    return pl.pallas_call(
        body, name=name, grid=(S // tr,),
        in_specs=[pl.BlockSpec((tr, AW), lambda i: (i, 0)),
                  pl.BlockSpec((tr, 2 * KVW), lambda i: (i, kv_blk)),
                  _vec_spec(l, HEAD), _vec_spec(l, HEAD),
                  pl.BlockSpec((tr, HEAD), lambda i: (i, 0)),
                  pl.BlockSpec((tr, HEAD), lambda i: (i, 0))],
        out_specs=[pl.BlockSpec((tr, AW), lambda i: (i, 0)),
                   pl.BlockSpec((tr, KVW), lambda i: (i, 0)),
                   pl.BlockSpec((tr, KVW), lambda i: (i, 0))],
        out_shape=[jax.ShapeDtypeStruct((S, AW), BF16),
                   jax.ShapeDtypeStruct((S, KVW), BF16),
                   jax.ShapeDtypeStruct((S, KVW), BF16)],
        compiler_params=_params(("arbitrary",)),
    )(z, z, qg3, kg3, cos2, sin2)


def _band_specs(width, nb):
    return [pl.BlockSpec((HEAD, width), lambda n: (jnp.maximum(n - 1, 0), 0)),
            pl.BlockSpec((HEAD, width), lambda n: (n, 0)),
            pl.BlockSpec((HEAD, width), lambda n: (jnp.minimum(n + 1, nb - 1), 0))]


def _attn_probs(qs, kj, n, sink_of_row, S, G):
    s = lax.dot_general(qs, kj, (((1,), (1,)), ((), ())), preferred_element_type=F32) * (HEAD ** -0.5)
    rows = lax.broadcasted_iota(jnp.int32, (G * HEAD, 3 * HEAD), 0)
    cols = lax.broadcasted_iota(jnp.int32, (G * HEAD, 3 * HEAD), 1)
    qi = rows & (HEAD - 1)
    kpos = n * HEAD - HEAD + cols
    valid = (cols >= qi) & (cols <= qi + 2 * HEAD) & (kpos >= 0) & (kpos < S)
    s = jnp.where(valid, s, MASK_VALUE)
    m = jnp.maximum(jnp.max(s, axis=-1, keepdims=True), sink_of_row)
    p = jnp.exp(s - m)
    e_sink = jnp.exp(sink_of_row - m)
    inv = 1.0 / (jnp.sum(p, axis=-1, keepdims=True) + e_sink)
    return p * inv, e_sink * inv


def _sink_rows(sink_ref, l, j, G):
    hidx = lax.broadcasted_iota(jnp.int32, (G * HEAD, 1), 0) // HEAD
    col = jnp.full((G * HEAD, 1), sink_ref[l, j * G], F32)
    for g in range(1, G):
        col = jnp.where(hidx == g, sink_ref[l, j * G + g], col)
    return col


def _attn_fwd(name, q, k, v, sink, ag3, l, cfg):
    S, AW, KVW, NKV, G = cfg["S"], cfg["AW"], cfg["KVW"], cfg["NKV"], cfg["G"]
    nb = S // HEAD

    def body(q_ref, kp, kc, kn, vp, vc, vn, sink_ref, ag_ref, a_ref, mix_ref):
        n = pl.program_id(0)
        kb = jnp.concatenate([kp[...], kc[...], kn[...]], axis=0)
        vb = jnp.concatenate([vp[...], vc[...], vn[...]], axis=0)
        for j in range(NKV):
            sl = slice(j * HEAD, (j + 1) * HEAD)
            qs = jnp.concatenate([q_ref[:, (j * G + g) * HEAD:(j * G + g + 1) * HEAD] for g in range(G)], axis=0)
            probs, _ = _attn_probs(qs, kb[:, sl], n, _sink_rows(sink_ref, l, j, G), S, G)
            o = jnp.dot(probs.astype(BF16), vb[:, sl], preferred_element_type=F32)
            for g in range(G):
                a_ref[:, (j * G + g) * HEAD:(j * G + g + 1) * HEAD] = o[g * HEAD:(g + 1) * HEAD]
        a = a_ref[...]
        r = lax.rsqrt(jnp.mean(a * a, axis=-1, keepdims=True) + EPS)
        mix_ref[...] = ((a * r) * ag_ref[...]).astype(BF16)

    return pl.pallas_call(
        body, name=name, grid=(nb,),
        in_specs=[pl.BlockSpec((HEAD, AW), lambda n: (n, 0))] + _band_specs(KVW, nb) + _band_specs(KVW, nb)
                 + [pl.BlockSpec(memory_space=pltpu.SMEM), _vec_spec(l, AW)],
        out_specs=[pl.BlockSpec((HEAD, AW), lambda n: (n, 0)), pl.BlockSpec((HEAD, AW), lambda n: (n, 0))],
        out_shape=[jax.ShapeDtypeStruct((S, AW), F32), jax.ShapeDtypeStruct((S, AW), BF16)],
        compiler_params=_params(("arbitrary",)),
    )(q, k, k, k, v, v, v, sink, ag3)


def _sgu_forward_math(gu, gv, lng, lnb):
    u = _gelu(gu)
    vv = _gelu(gv)
    mu = jnp.mean(vv, axis=-1, keepdims=True)
    xc = vv - mu
    rstd = lax.rsqrt(jnp.mean(xc * xc, axis=-1, keepdims=True) + EPS)
    xhat = xc * rstd
    vn = xhat * lng + lnb
    return u, xhat, rstd, vn


def _sgu_fwd(name, z, lng3, lnb3, ws_b, bs_b, og3, l, cfg):
    S, GW, NG = cfg["S"], cfg["GW"], cfg["NG"]

    def body(gu_ref, gv_ref, lng_ref, lnb_ref, ws_ref, bs_ref, og_ref, mix_ref, sg_ref):
        u, _, _, vn = _sgu_forward_math(gu_ref[...], gv_ref[...], lng_ref[...], lnb_ref[...])
        vnb = vn.astype(BF16)
        for h in range(NG):
            sl = slice(h * HEAD, (h + 1) * HEAD)
            f = jnp.dot(ws_ref[h], vnb[:, sl], preferred_element_type=F32) + bs_ref[h]
            sg_ref[:, sl] = u[:, sl] * f
        sg = sg_ref[...]
        r = lax.rsqrt(jnp.mean(sg * sg, axis=-1, keepdims=True) + EPS)
        mix_ref[...] = ((sg * r) * og_ref[...]).astype(BF16)

    return pl.pallas_call(
        body, name=name, grid=(S // HEAD,),
        in_specs=[pl.BlockSpec((HEAD, GW), lambda c: (c, 1)),
                  pl.BlockSpec((HEAD, GW), lambda c: (c, 2)),
                  _vec_spec(l, GW), _vec_spec(l, GW),
                  pl.BlockSpec((None, NG, HEAD, HEAD), lambda c: (l, 0, 0, 0)),
                  pl.BlockSpec((None, NG, HEAD, HEAD), lambda c: (l, 0, 0, 0)),
                  _vec_spec(l, GW)],
        out_specs=pl.BlockSpec((HEAD, GW), lambda c: (c, 0)),
        out_shape=jax.ShapeDtypeStruct((S, GW), BF16),
        scratch_shapes=[pltpu.VMEM((HEAD, GW), F32)],
        compiler_params=_params(("arbitrary",)),
    )(z, z, lng3, lnb3, ws_b, bs_b, og3)


def _conv3(t, w, b, S):
    row = lax.broadcasted_iota(jnp.int32, t.shape, 0)
    dn = jnp.where(row == 0, 0.0, pltpu.roll(t, 1, axis=0))
    up = jnp.where(row == S - 1, 0.0, pltpu.roll(t, S - 1, axis=0))
    return ((b + dn * w[0:1]) + t * w[1:2]) + up * w[2:3], dn, up


def _conv_glu_fwd(name, ap, cw, cb3, l, cfg):
    S, F = cfg["S"], cfg["F"]
    tc = _tile(F, 256, 128)
    nf = F // tc

    def body(g_ref, u_ref, wg_ref, wu_ref, bg_ref, bu_ref, y_ref):
        ag, _, _ = _conv3(g_ref[...], wg_ref[...], bg_ref[...], S)
        au, _, _ = _conv3(u_ref[...], wu_ref[...], bu_ref[...], S)
        y_ref[...] = ((ag * _sigmoid(ag)) * au).astype(BF16)

    return pl.pallas_call(
        body, name=name, grid=(nf,),
        in_specs=[pl.BlockSpec((S, tc), lambda j: (0, j)),
                  pl.BlockSpec((S, tc), lambda j: (0, j + nf)),
                  pl.BlockSpec((None, 3, tc), lambda j: (l, 0, j)),
                  pl.BlockSpec((None, 3, tc), lambda j: (l, 0, j + nf)),
                  pl.BlockSpec((None, 1, tc), lambda j: (l, 0, j)),
                  pl.BlockSpec((None, 1, tc), lambda j: (l, 0, j + nf))],
        out_specs=pl.BlockSpec((S, tc), lambda j: (0, j)),
        out_shape=jax.ShapeDtypeStruct((S, F), BF16),
        compiler_params=_params(("arbitrary",)),
    )(ap, ap, cw, cw, cb3, cb3)


def _loss_bwd(name, y, target):
    S, D = y.shape
    tr = _tile(S, 256, 16)

    def body(y_ref, t_ref, loss_ref, d_ref, db_ref):
        @pl.when(pl.program_id(0) == 0)
        def _():
            loss_ref[...] = jnp.zeros_like(loss_ref)

        err = y_ref[...] - t_ref[...]
        part = 0.5 * jnp.sum(jnp.mean(err * err, axis=-1, keepdims=True), axis=0, keepdims=True)
        loss_ref[...] += jnp.broadcast_to(part, loss_ref.shape)
        d = err * (1.0 / D)
        d_ref[...] = d
        db_ref[...] = d.astype(BF16)

    return pl.pallas_call(
        body, name=name, grid=(S // tr,),
        in_specs=[pl.BlockSpec((tr, D), lambda i: (i, 0)), pl.BlockSpec((tr, D), lambda i: (i, 0))],
        out_specs=[pl.BlockSpec((8, 128), lambda i: (0, 0)),
                   pl.BlockSpec((tr, D), lambda i: (i, 0)),
                   pl.BlockSpec((tr, D), lambda i: (i, 0))],
        out_shape=[jax.ShapeDtypeStruct((8, 128), F32),
                   jax.ShapeDtypeStruct((S, D), F32),
                   jax.ShapeDtypeStruct((S, D), BF16)],
        compiler_params=_params(("arbitrary",)),
    )(y, target)


def _rms_bwd(name, x, dh, dres, g3, l):
    S, D = x.shape
    tr = _tile(S, 256, 16)

    def body(x_ref, dh_ref, dr_ref, g_ref, dx_ref, dxb_ref, dg_ref):
        @pl.when(pl.program_id(0) == 0)
        def _():
            dg_ref[...] = jnp.zeros_like(dg_ref)

        xv = x_ref[...]
        dhv = dh_ref[...]
        r = lax.rsqrt(jnp.mean(xv * xv, axis=-1, keepdims=True) + EPS)
        xhat = xv * r
        dhg = dhv * g_ref[...]
        dx = dr_ref[...] + r * (dhg - xhat * jnp.mean(dhg * xhat, axis=-1, keepdims=True))
        dx_ref[...] = dx
        dxb_ref[...] = dx.astype(BF16)
        dg_ref[...] += jnp.sum(dhv * xhat, axis=0, keepdims=True)

    return pl.pallas_call(
        body, name=name, grid=(S // tr,),
        in_specs=[pl.BlockSpec((tr, D), lambda i: (i, 0)), pl.BlockSpec((tr, D), lambda i: (i, 0)),
                  pl.BlockSpec((tr, D), lambda i: (i, 0)), _vec_spec(l, D)],
        out_specs=[pl.BlockSpec((tr, D), lambda i: (i, 0)), pl.BlockSpec((tr, D), lambda i: (i, 0)),
                   pl.BlockSpec((1, D), lambda i: (0, 0))],
        out_shape=[jax.ShapeDtypeStruct((S, D), F32), jax.ShapeDtypeStruct((S, D), BF16),
                   jax.ShapeDtypeStruct((1, D), F32)],
        compiler_params=_params(("arbitrary",)),
    )(x, dh, dres, g3)


def _glu_conv_bwd(name, dy, ap, cw, cb3, l, cfg):
    S, F = cfg["S"], cfg["F"]
    tc = _tile(F, 256, 128)
    nf = F // tc

    def body(dy_ref, g_ref, u_ref, wg_ref, wu_ref, bg_ref, bu_ref, dap_ref, dw_ref, db_ref):
        apg, apu = g_ref[...], u_ref[...]
        wg, wu = wg_ref[...], wu_ref[...]
        ag, g_dn, g_up = _conv3(apg, wg, bg_ref[...], S)
        au, u_dn, u_up = _conv3(apu, wu, bu_ref[...], S)
        sig = _sigmoid(ag)
        dyv = dy_ref[...]
        da_u = dyv * (ag * sig)
        da_g = (dyv * au) * (sig * (1.0 + ag * (1.0 - sig)))
        row = lax.broadcasted_iota(jnp.int32, (S, tc), 0)

        def back(da, w):
            nxt = jnp.where(row == S - 1, 0.0, pltpu.roll(da, S - 1, axis=0))
            prv = jnp.where(row == 0, 0.0, pltpu.roll(da, 1, axis=0))
            return (nxt * w[0:1] + da * w[1:2]) + prv * w[2:3]

        dap_ref[0] = back(da_g, wg).astype(BF16)
        dap_ref[1] = back(da_u, wu).astype(BF16)

        def wgrad(da, dn, t, up):
            return jnp.concatenate([jnp.sum(da * dn, axis=0, keepdims=True),
                                    jnp.sum(da * t, axis=0, keepdims=True),
                                    jnp.sum(da * up, axis=0, keepdims=True)], axis=0)

        dw_ref[0] = wgrad(da_g, g_dn, apg, g_up)
        dw_ref[1] = wgrad(da_u, u_dn, apu, u_up)
        db_ref[0] = jnp.sum(da_g, axis=0, keepdims=True)
        db_ref[1] = jnp.sum(da_u, axis=0, keepdims=True)

    return pl.pallas_call(
        body, name=name, grid=(nf,),
        in_specs=[pl.BlockSpec((S, tc), lambda j: (0, j)),
                  pl.BlockSpec((S, tc), lambda j: (0, j)),
                  pl.BlockSpec((S, tc), lambda j: (0, j + nf)),
                  pl.BlockSpec((None, 3, tc), lambda j: (l, 0, j)),
                  pl.BlockSpec((None, 3, tc), lambda j: (l, 0, j + nf)),
                  pl.BlockSpec((None, 1, tc), lambda j: (l, 0, j)),
                  pl.BlockSpec((None, 1, tc), lambda j: (l, 0, j + nf))],
        out_specs=[pl.BlockSpec((2, S, tc), lambda j: (0, 0, j)),
                   pl.BlockSpec((2, 3, tc), lambda j: (0, 0, j)),
                   pl.BlockSpec((2, 1, tc), lambda j: (0, 0, j))],
        out_shape=[jax.ShapeDtypeStruct((2, S, F), BF16),
                   jax.ShapeDtypeStruct((2, 3, F), F32),
                   jax.ShapeDtypeStruct((2, 1, F), F32)],
        compiler_params=_params(("arbitrary",)),
    )(dy, ap, ap, cw, cw, cb3, cb3)


def _attn_bwd(name, q, k, v, attn, dmix, sink, ag3, l, cfg):
    S, AW, KVW, NQ, NKV, G = cfg["S"], cfg["AW"], cfg["KVW"], cfg["NQ"], cfg["NKV"], cfg["G"]
    nb = S // HEAD
    scale = HEAD ** -0.5

    def body(q_ref, kp, kc, kn, vp, vc, vn, a_ref, dm_ref, sink_ref, ag_ref,
             dq_ref, dk_ref, dv_ref, dsink_ref, dag_ref, da_scr):
        n = pl.program_id(0)

        @pl.when(n == 0)
        def _():
            dk_ref[...] = jnp.zeros_like(dk_ref)
            dv_ref[...] = jnp.zeros_like(dv_ref)
            dsink_ref[...] = jnp.zeros_like(dsink_ref)
            dag_ref[...] = jnp.zeros_like(dag_ref)

        a = a_ref[...]
        dm = dm_ref[...]
        r = lax.rsqrt(jnp.mean(a * a, axis=-1, keepdims=True) + EPS)
        xhat = a * r
        dmg = dm * ag_ref[...]
        da_scr[...] = r * (dmg - xhat * jnp.mean(dmg * xhat, axis=-1, keepdims=True))
        dag_ref[...] += jnp.sum(dm * xhat, axis=0, keepdims=True)

        kb = jnp.concatenate([kp[...], kc[...], kn[...]], axis=0)
        vb = jnp.concatenate([vp[...], vc[...], vn[...]], axis=0)
        band = pl.ds(pl.multiple_of(n * HEAD, HEAD), 3 * HEAD)
        for j in range(NKV):
            sl = slice(j * HEAD, (j + 1) * HEAD)
            heads = [slice((j * G + g) * HEAD, (j * G + g + 1) * HEAD) for g in range(G)]
            qs = jnp.concatenate([q_ref[:, hs] for hs in heads], axis=0)
            do = jnp.concatenate([da_scr[:, hs] for hs in heads], axis=0)
            kj, vj = kb[:, sl], vb[:, sl]
            probs, p_sink = _attn_probs(qs, kj, n, _sink_rows(sink_ref, l, j, G), S, G)
            dob = do.astype(BF16)
            dprobs = lax.dot_general(dob, vj, (((1,), (1,)), ((), ())), preferred_element_type=F32)
            delta = jnp.sum(dprobs * probs, axis=-1, keepdims=True)
            ds = (probs * (dprobs - delta)) * scale
            dsb = ds.astype(BF16)
            dsk = -(p_sink * delta)
            dq = jnp.dot(dsb, kj, preferred_element_type=F32)
            for g in range(G):
                dq_ref[:, heads[g]] = dq[g * HEAD:(g + 1) * HEAD]
                part = jnp.sum(dsk[g * HEAD:(g + 1) * HEAD], axis=0, keepdims=True)
                dsink_ref[j * G + g:j * G + g + 1, :] += jnp.broadcast_to(part, (1, HEAD))
            dk_ref[band, sl] += lax.dot_general(dsb, qs, (((0,), (0,)), ((), ())), preferred_element_type=F32)
            dv_ref[band, sl] += lax.dot_general(probs.astype(BF16), dob, (((0,), (0,)), ((), ())),
                                                preferred_element_type=F32)

    return pl.pallas_call(
        body, name=name, grid=(nb,),
        in_specs=[pl.BlockSpec((HEAD, AW), lambda n: (n, 0))] + _band_specs(KVW, nb) + _band_specs(KVW, nb)
                 + [pl.BlockSpec((HEAD, AW), lambda n: (n, 0)),
                    pl.BlockSpec((HEAD, AW), lambda n: (n, 0)),
                    pl.BlockSpec(memory_space=pltpu.SMEM), _vec_spec(l, AW)],
        out_specs=[pl.BlockSpec((HEAD, AW), lambda n: (n, 0)),
                   pl.BlockSpec((S + 2 * HEAD, KVW), lambda n: (0, 0)),
                   pl.BlockSpec((S + 2 * HEAD, KVW), lambda n: (0, 0)),
                   pl.BlockSpec((NQ, HEAD), lambda n: (0, 0)),
                   pl.BlockSpec((1, AW), lambda n: (0, 0))],
        out_shape=[jax.ShapeDtypeStruct((S, AW), F32),
                   jax.ShapeDtypeStruct((S + 2 * HEAD, KVW), F32),
                   jax.ShapeDtypeStruct((S + 2 * HEAD, KVW), F32),
                   jax.ShapeDtypeStruct((NQ, HEAD), F32),
                   jax.ShapeDtypeStruct((1, AW), F32)],
        scratch_shapes=[pltpu.VMEM((HEAD, AW), F32)],
        compiler_params=_params(("arbitrary",)),
    )(q, k, k, k, v, v, v, attn, dmix, sink, ag3)


def _qkv_prep_bwd(name, z, dq, dk_pad, dv_pad, qg3, kg3, cos2, sin2, l, cfg):
    S, AW, KVW, NQ, NKV = cfg["S"], cfg["AW"], cfg["KVW"], cfg["NQ"], cfg["NKV"]
    kv_blk = (3 * AW) // (2 * KVW)

    def body(zq_ref, zkv_ref, dq_ref, dk_ref, dv_ref, qg_ref, kg_ref, c_ref, s_ref,
             dzq_ref, dzkv_ref, dqg_ref, dkg_ref):
        @pl.when(pl.program_id(0) == 0)
        def _():
            dqg_ref[...] = jnp.zeros_like(dqg_ref)
            dkg_ref[...] = jnp.zeros_like(dkg_ref)

        cosv, sinv = c_ref[...], s_ref[...]

        def back(t, dr, g):
            r = lax.rsqrt(jnp.mean(t * t, axis=-1, keepdims=True) + EPS)
            xhat = t * r
            dn = dr * cosv + pltpu.roll(dr * sinv, HEAD // 2, axis=1)
            dxh = dn * g
            dt = r * (dxh - xhat * jnp.mean(dxh * xhat, axis=-1, keepdims=True))
            return dt, jnp.sum(dn * xhat, axis=0, keepdims=True)

        gq = jnp.zeros((1, HEAD), F32)
        for h in range(NQ):
            sl = slice(h * HEAD, (h + 1) * HEAD)
            dt, gpart = back(zq_ref[:, sl], dq_ref[:, sl], qg_ref[...])
            dzq_ref[:, sl] = dt.astype(BF16)
            gq = gq + gpart
        dqg_ref[...] += gq
        gk = jnp.zeros((1, HEAD), F32)
        for h in range(NKV):
            sl = slice(h * HEAD, (h + 1) * HEAD)
            dt, gpart = back(zkv_ref[:, sl], dk_ref[:, sl], kg_ref[...])
            dzkv_ref[:, sl] = dt.astype(BF16)
            gk = gk + gpart
        dkg_ref[...] += gk
        dzkv_ref[:, KVW:] = dv_ref[...].astype(BF16)

    return pl.pallas_call(
        body, name=name, grid=(S // HEAD,),
        in_specs=[pl.BlockSpec((HEAD, AW), lambda i: (i, 0)),
                  pl.BlockSpec((HEAD, 2 * KVW), lambda i: (i, kv_blk)),
                  pl.BlockSpec((HEAD, AW), lambda i: (i, 0)),
                  pl.BlockSpec((HEAD, KVW), lambda i: (i + 1, 0)),
                  pl.BlockSpec((HEAD, KVW), lambda i: (i + 1, 0)),
                  _vec_spec(l, HEAD), _vec_spec(l, HEAD),
                  pl.BlockSpec((HEAD, HEAD), lambda i: (i, 0)),
                  pl.BlockSpec((HEAD, HEAD), lambda i: (i, 0))],
        out_specs=[pl.BlockSpec((HEAD, AW), lambda i: (i, 0)),
                   pl.BlockSpec((HEAD, 2 * KVW), lambda i: (i, 0)),
                   pl.BlockSpec((1, HEAD), lambda i: (0, 0)),
                   pl.BlockSpec((1, HEAD), lambda i: (0, 0))],
        out_shape=[jax.ShapeDtypeStruct((S, AW), BF16),
                   jax.ShapeDtypeStruct((S, 2 * KVW), BF16),
                   jax.ShapeDtypeStruct((1, HEAD), F32),
                   jax.ShapeDtypeStruct((1, HEAD), F32)],
        compiler_params=_params(("arbitrary",)),
    )(z, z, dq, dk_pad, dv_pad, qg3, kg3, cos2, sin2)


def _sgu_bwd(name, z, dmix, lng3, lnb3, ws_b, wst_b, bs_b, og3, l, cfg):
    S, GW, NG = cfg["S"], cfg["GW"], cfg["NG"]

    def body(gu_ref, gv_ref, dm_ref, lng_ref, lnb_ref, ws_ref, wst_ref, bs_ref, og_ref,
             dgu_ref, dgv_ref, dws_ref, dbs_ref, dlng_ref, dlnb_ref, dog_ref, sg_scr, f_scr, dvn_scr):
        @pl.when(pl.program_id(0) == 0)
        def _():
            dws_ref[...] = jnp.zeros_like(dws_ref)
            dbs_ref[...] = jnp.zeros_like(dbs_ref)
            dlng_ref[...] = jnp.zeros_like(dlng_ref)
            dlnb_ref[...] = jnp.zeros_like(dlnb_ref)
            dog_ref[...] = jnp.zeros_like(dog_ref)

        gu, gv = gu_ref[...], gv_ref[...]
        lng = lng_ref[...]
        u, xhat, rstd, vn = _sgu_forward_math(gu, gv, lng, lnb_ref[...])
        vnb = vn.astype(BF16)
        for h in range(NG):
            sl = slice(h * HEAD, (h + 1) * HEAD)
            f = jnp.dot(ws_ref[h], vnb[:, sl], preferred_element_type=F32) + bs_ref[h]
            f_scr[:, sl] = f
            sg_scr[:, sl] = u[:, sl] * f
        sg = sg_scr[...]
        dm = dm_ref[...]
        r = lax.rsqrt(jnp.mean(sg * sg, axis=-1, keepdims=True) + EPS)
        sghat = sg * r
        dmg = dm * og_ref[...]
        dsg = r * (dmg - sghat * jnp.mean(dmg * sghat, axis=-1, keepdims=True))
        dog_ref[...] += jnp.sum(dm * sghat, axis=0, keepdims=True)
        du = dsg * f_scr[...]
        df = dsg * u
        dfb = df.astype(BF16)
        for h in range(NG):
            sl = slice(h * HEAD, (h + 1) * HEAD)
            dvn_scr[:, sl] = jnp.dot(wst_ref[h], dfb[:, sl], preferred_element_type=F32)
            dws_ref[h] += lax.dot_general(dfb[:, sl], vnb[:, sl], (((1,), (1,)), ((), ())),
                                          preferred_element_type=F32)
            dbs_ref[h] += jnp.broadcast_to(jnp.sum(df[:, sl], axis=-1, keepdims=True), (HEAD, HEAD))
        dvn = dvn_scr[...]
        dlng_ref[...] += jnp.sum(dvn * xhat, axis=0, keepdims=True)
        dlnb_ref[...] += jnp.sum(dvn, axis=0, keepdims=True)
        dxh = dvn * lng
        dvv = rstd * ((dxh - jnp.mean(dxh, axis=-1, keepdims=True))
                      - xhat * jnp.mean(dxh * xhat, axis=-1, keepdims=True))
        dgu_ref[...] = (du * _gelu_grad(gu)).astype(BF16)
        dgv_ref[...] = (dvv * _gelu_grad(gv)).astype(BF16)

    vec = pl.BlockSpec((1, GW), lambda c: (0, 0))
    mat = pl.BlockSpec((NG, HEAD, HEAD), lambda c: (0, 0, 0))
    wsp = pl.BlockSpec((None, NG, HEAD, HEAD), lambda c: (l, 0, 0, 0))
    return pl.pallas_call(
        body, name=name, grid=(S // HEAD,),
        in_specs=[pl.BlockSpec((HEAD, GW), lambda c: (c, 1)),
                  pl.BlockSpec((HEAD, GW), lambda c: (c, 2)),
                  pl.BlockSpec((HEAD, GW), lambda c: (c, 1)),
                  _vec_spec(l, GW), _vec_spec(l, GW), wsp, wsp, wsp, _vec_spec(l, GW)],
        out_specs=[pl.BlockSpec((HEAD, GW), lambda c: (c, 0)), pl.BlockSpec((HEAD, GW), lambda c: (c, 0)),
                   mat, mat, vec, vec, vec],
        out_shape=[jax.ShapeDtypeStruct((S, GW), BF16), jax.ShapeDtypeStruct((S, GW), BF16),
                   jax.ShapeDtypeStruct((NG, HEAD, HEAD), F32), jax.ShapeDtypeStruct((NG, HEAD, HEAD), F32),
                   jax.ShapeDtypeStruct((1, GW), F32), jax.ShapeDtypeStruct((1, GW), F32),
                   jax.ShapeDtypeStruct((1, GW), F32)],
        scratch_shapes=[pltpu.VMEM((HEAD, GW), F32), pltpu.VMEM((HEAD, GW), F32), pltpu.VMEM((HEAD, GW), F32)],
        compiler_params=_params(("arbitrary",)),
    )(z, z, dmix, lng3, lnb3, ws_b, wst_b, bs_b, og3)


def _mesh_pos():
    x, y, c = lax.axis_index("x"), lax.axis_index("y"), lax.axis_index("c")
    return x, y, c


def _dev_index(p):
    return 4 * p[0] + 2 * p[1] + p[2]


def _handshake(peers):
    barrier = pltpu.get_barrier_semaphore()
    for p in peers:
        pl.semaphore_signal(barrier, inc=1, device_id=p, device_id_type=MESH)
    pl.semaphore_wait(barrier, len(peers))


def _gather_body(na, slabs):
    def body(*refs):
        ins, outs = refs[:na], refs[na:2 * na]
        send_sems, recv_sems, local_sems = refs[2 * na:]
        x, y, c = _mesh_pos()
        me, sib = (x, y, c), (x, y, 1 - c)
        chips = [(1 - x, y), (x, 1 - y), (1 - x, 1 - y)]
        _handshake([sib] + [(*chip, c) for chip in chips])

        def copy(a, k, block, to, src=None):
            dst = slabs[a](outs[a], _dev_index(block))
            return pltpu.make_async_remote_copy(
                src_ref=dst if src is None else src, dst_ref=dst,
                send_sem=send_sems.at[7 * a + k], recv_sem=recv_sems.at[7 * a + k],
                device_id=to, device_id_type=MESH)

        local = [pltpu.make_async_copy(ins[a], slabs[a](outs[a], _dev_index(me)), local_sems.at[a])
                 for a in range(na)]
        for cp in local:
            cp.start()
        first = []
        for a in range(na):
            first.append(copy(a, 0, me, sib, src=ins[a]))
            first += [copy(a, 1 + j, me, (*chip, c), src=ins[a]) for j, chip in enumerate(chips)]
        for cp in first:
            cp.start()
        passed = []
        for j, chip in enumerate(chips):
            for a in range(na):
                copy(a, 1 + j, (*chip, c), me).wait_recv()
                fwd = copy(a, 4 + j, (*chip, c), sib)
                fwd.start()
                passed.append(fwd)
        for a in range(na):
            copy(a, 0, sib, me).wait_recv()
            for j, chip in enumerate(chips):
                copy(a, 4 + j, (*chip, 1 - c), me).wait_recv()
        for cp in first + passed:
            cp.wait_send()
        for cp in local:
            cp.wait()

    return body


def _all_gather(name, shards, out_shapes, slabs):
    na = len(shards)
    return pl.kernel(
        _gather_body(na, slabs), out_type=out_shapes,
        mesh=plsc.ScalarSubcoreMesh(axis_name="seq", num_cores=1), name=name,
        scratch_types=[pltpu.SemaphoreType.DMA((7 * na,)), pltpu.SemaphoreType.DMA((7 * na,)),
                       pltpu.SemaphoreType.DMA((na,))],
        compiler_params=pltpu.CompilerParams(collective_id=AG_COLLECTIVE_ID),
    )(*shards)


def _rs_d2d(sends, small):
    nj = len(sends)

    def body(*refs):
        ins, outs = refs[:nj + 1], refs[nj + 1:2 * nj + 2]
        send_sems, recv_sems = refs[2 * nj + 2:]
        x, y, c = _mesh_pos()
        copies = []
        for t in range(nj + 1):
            cp = pltpu.make_async_remote_copy(
                src_ref=ins[t].at[:, 1 - c] if t < nj else ins[t], dst_ref=outs[t],
                send_sem=send_sems.at[t], recv_sem=recv_sems.at[t],
                device_id=(x, y, 1 - c), device_id_type=MESH)
            cp.start()
            copies.append(cp)
        for cp in copies:
            cp.wait()

    any_spec = pl.BlockSpec(memory_space=pl.ANY)
    shapes = [jax.ShapeDtypeStruct((g.shape[0],) + g.shape[2:], g.dtype) for g in sends]
    shapes.append(jax.ShapeDtypeStruct(small.shape, small.dtype))
    return pl.pallas_call(
        body, name="rs_d2d",
        in_specs=[any_spec] * (nj + 1), out_specs=[any_spec] * (nj + 1), out_shape=shapes,
        scratch_shapes=[pltpu.SemaphoreType.DMA((nj + 1,)), pltpu.SemaphoreType.DMA((nj + 1,))],
    )(*sends, small)


def _pair_sum(name, g4, recv, cvec, row_mult):
    Q, _, R, C = g4.shape
    tr = _tile(R, 256, row_mult)

    def body(c_ref, a_ref, b_ref, o_ref):
        o_ref[...] = (a_ref[...].astype(F32) + b_ref[...].astype(F32)).astype(BF16)

    grid_spec = pltpu.PrefetchScalarGridSpec(
        num_scalar_prefetch=1, grid=(Q, R // tr),
        in_specs=[pl.BlockSpec((None, None, tr, C), lambda q, i, c_ref: (q, c_ref[0], i, 0)),
                  pl.BlockSpec((None, tr, C), lambda q, i, c_ref: (q, i, 0))],
        out_specs=pl.BlockSpec((None, tr, C), lambda q, i, c_ref: (q, i, 0)))
    return pl.pallas_call(
        body, name=name, grid_spec=grid_spec, out_shape=jax.ShapeDtypeStruct((Q, R, C), BF16),
        compiler_params=_params(("arbitrary", "arbitrary")),
    )(cvec, g4, recv)


def _add2(name, a, b):
    R, C = a.shape
    tr = _tile(R, 512, 8)

    def body(a_ref, b_ref, o_ref):
        o_ref[...] = a_ref[...] + b_ref[...]

    blk = pl.BlockSpec((tr, C), lambda i: (i, 0))
    return pl.pallas_call(body, name=name, grid=(R // tr,), in_specs=[blk, blk], out_specs=blk,
                          out_shape=jax.ShapeDtypeStruct((R, C), a.dtype),
                          compiler_params=_params(("arbitrary",)))(a, b)


def _rs_ici(chipsums, jobs, small, out_shapes):
    nj = len(chipsums)

    def body(*refs):
        ins, outs = refs[:nj + 1], refs[nj + 1:nj + 1 + len(out_shapes)]
        send_sems, recv_sems, local_sems = refs[nj + 1 + len(out_shapes):]
        x, y, c = _mesh_pos()
        q_me = 2 * x + y
        copies = []
        for t in range(nj + 1):
            if t < nj:
                ri, l = jobs[t]
                src_of = lambda q, t=t: ins[t].at[q]
                dst = outs[ri].at[l, q_me]
            else:
                src_of = lambda q, t=t: ins[t]
                dst = outs[len(out_shapes) - 1].at[q_me]
            loc = pltpu.make_async_copy(src_of(q_me), dst, local_sems.at[t])
            loc.start()
            copies.append(loc)
            for k in range(1, 4):
                px, py = x ^ (k >> 1), y ^ (k & 1)
                cp = pltpu.make_async_remote_copy(
                    src_ref=src_of(2 * px + py), dst_ref=dst,
                    send_sem=send_sems.at[t, k - 1], recv_sem=recv_sems.at[t, k - 1],
                    device_id=(px, py, c), device_id_type=MESH)
                cp.start()
                copies.append(cp)
        for cp in copies:
            cp.wait()

    any_spec = pl.BlockSpec(memory_space=pl.ANY)
    return pl.pallas_call(
        body, name="rs_ici",
        in_specs=[any_spec] * (nj + 1), out_specs=[any_spec] * len(out_shapes), out_shape=out_shapes,
        scratch_shapes=[pltpu.SemaphoreType.DMA((nj + 1, 3)), pltpu.SemaphoreType.DMA((nj + 1, 3)),
                        pltpu.SemaphoreType.DMA((nj + 1,))],
    )(*chipsums, small)


def _adamw_math(w, g, m, v):
    m2 = ADAM_B1 * m + (1.0 - ADAM_B1) * g
    v2 = ADAM_B2 * v + (1.0 - ADAM_B2) * (g * g)
    m_hat = m2 / (1.0 - ADAM_B1 ** ADAM_STEP)
    v_hat = v2 / (1.0 - ADAM_B2 ** ADAM_STEP)
    delta = -ADAM_LR * (m_hat / (jnp.sqrt(v_hat) + ADAM_EPS) + ADAM_WD * w)
    return delta, m2, v2


def _adamw_sum(name, recv, w, m, v, row_mult):
    L, NS, R, C = recv.shape
    tr = _tile(R, 128, row_mult)

    def body(r_ref, w_ref, m_ref, v_ref, g_ref, d_ref, nm_ref, nv_ref):
        g = r_ref[0].astype(F32)
        for s in range(1, NS):
            g = g + r_ref[s].astype(F32)
        d, m2, v2 = _adamw_math(w_ref[...], g, m_ref[...], v_ref[...])
        g_ref[...] = g
        d_ref[...] = d
        nm_ref[...] = m2
        nv_ref[...] = v2

    blk = pl.BlockSpec((None, tr, C), lambda l, i: (l, i, 0))
    shp = jax.ShapeDtypeStruct((L, R, C), F32)
    return pl.pallas_call(
        body, name=name, grid=(L, R // tr),
        in_specs=[pl.BlockSpec((None, NS, tr, C), lambda l, i: (l, 0, i, 0)), blk, blk, blk],
        out_specs=[blk, blk, blk, blk],
        out_shape=[shp, shp, shp, shp],
        compiler_params=_params(("arbitrary", "arbitrary")),
    )(recv, w, m, v)


def _adamw_plain(name, g, w, m, v):
    def body(g_ref, w_ref, m_ref, v_ref, d_ref, nm_ref, nv_ref):
        d, m2, v2 = _adamw_math(w_ref[...], g_ref[...], m_ref[...], v_ref[...])
        d_ref[...] = d
        nm_ref[...] = m2
        nv_ref[...] = v2

    shp = jax.ShapeDtypeStruct(g.shape, F32)
    return pl.pallas_call(body, name=name, out_shape=[shp, shp, shp], compiler_params=_params())(g, w, m, v)


SMALL = ["norm1_g", "q_norm_g", "k_norm_g", "sink", "sgu_ln_g", "sgu_ln_b", "w_s", "b_s",
         "attn_out_g", "sgu_out_g", "norm2_g", "conv_b"]
PACK_ALIGN = 1024


def _pack(pieces):
    flat = []
    for p in pieces:
        f = p.reshape(-1).astype(F32)
        pad = (-f.shape[0]) % PACK_ALIGN
        flat.append(jnp.pad(f, (0, pad)) if pad else f)
    return jnp.concatenate(flat).reshape(-1, 128)


def _unpack(packed, shapes):
    flat = packed.reshape(-1)
    out, off = [], 0
    for shp in shapes:
        n = 1
        for d in shp:
            n *= d
        out.append(flat[off:off + n].reshape(shp))
        off += n + ((-n) % PACK_ALIGN)
    return out


def kernel(x, norm1_g, w_in, q_norm_g, k_norm_g, sink, sgu_ln_g, sgu_ln_b, w_s, b_s, attn_out_g, sgu_out_g, w_o, norm2_g, w_up, conv_w, conv_b, w_down, loss_target, m_norm1_g, m_w_in, m_q_norm_g, m_k_norm_g, m_sink, m_sgu_ln_g, m_sgu_ln_b, m_w_s, m_b_s, m_attn_out_g, m_sgu_out_g, m_w_o, m_norm2_g, m_w_up, m_conv_w, m_conv_b, m_w_down, v_norm1_g, v_w_in, v_q_norm_g, v_k_norm_g, v_sink, v_sgu_ln_g, v_sgu_ln_b, v_w_s, v_b_s, v_attn_out_g, v_sgu_out_g, v_w_o, v_norm2_g, v_w_up, v_conv_w, v_conv_b, v_w_down):
    weights = dict(norm1_g=norm1_g, w_in=w_in, q_norm_g=q_norm_g, k_norm_g=k_norm_g, sink=sink, sgu_ln_g=sgu_ln_g,
                   sgu_ln_b=sgu_ln_b, w_s=w_s, b_s=b_s, attn_out_g=attn_out_g, sgu_out_g=sgu_out_g, w_o=w_o,
                   norm2_g=norm2_g, w_up=w_up, conv_w=conv_w, conv_b=conv_b, w_down=w_down)
    mom_m = dict(norm1_g=m_norm1_g, w_in=m_w_in, q_norm_g=m_q_norm_g, k_norm_g=m_k_norm_g, sink=m_sink,
                 sgu_ln_g=m_sgu_ln_g, sgu_ln_b=m_sgu_ln_b, w_s=m_w_s, b_s=m_b_s, attn_out_g=m_attn_out_g,
                 sgu_out_g=m_sgu_out_g, w_o=m_w_o, norm2_g=m_norm2_g, w_up=m_w_up, conv_w=m_conv_w,
                 conv_b=m_conv_b, w_down=m_w_down)
    mom_v = dict(norm1_g=v_norm1_g, w_in=v_w_in, q_norm_g=v_q_norm_g, k_norm_g=v_k_norm_g, sink=v_sink,
                 sgu_ln_g=v_sgu_ln_g, sgu_ln_b=v_sgu_ln_b, w_s=v_w_s, b_s=v_b_s, attn_out_g=v_attn_out_g,
                 sgu_out_g=v_sgu_out_g, w_o=v_w_o, norm2_g=v_norm2_g, w_up=v_w_up, conv_w=v_conv_w,
                 conv_b=v_conv_b, w_down=v_w_down)
    order = ["norm1_g", "w_in", "q_norm_g", "k_norm_g", "sink", "sgu_ln_g", "sgu_ln_b", "w_s", "b_s",
             "attn_out_g", "sgu_out_g", "w_o", "norm2_g", "w_up", "conv_w", "conv_b", "w_down"]

    _, S, D = x.shape
    L = w_in.shape[0]
    AW = D // 2
    NQ = AW // HEAD
    NKV = max(1, NQ // 4)
    G = NQ // NKV
    KVW = NKV * HEAD
    GW = D - AW
    NG = GW // HEAD
    IN = AW + 2 * KVW + 2 * GW
    INS = w_in.shape[2]
    OS = w_o.shape[1]
    US = w_up.shape[2]
    DS = w_down.shape[1]
    F2 = US * N_DEV
    F = F2 // 2
    assert INS * N_DEV == IN and OS * N_DEV == D and DS * N_DEV == F and AW == GW and (3 * AW) % (2 * KVW) == 0
    cfg = dict(S=S, D=D, AW=AW, NQ=NQ, NKV=NKV, G=G, KVW=KVW, GW=GW, NG=NG, F=F, F2=F2)

    def rows_slab(n):
        return lambda ref, idx: ref.at[:, pl.ds(pl.multiple_of(idx * n, n), n), :]

    def cols_slab(n):
        return lambda ref, idx: ref.at[:, :, pl.ds(pl.multiple_of(idx * n, n), n)]

    def lead_slab(ref, idx):
        return ref.at[idx]

    w_in_p, w_o_f, w_up_f, w_down_f = [], [], [], []
    for l in range(L):
        (w_in_sh,) = _all_gather("ag_in", [w_in[l:l + 1].astype(BF16)],
                                 [jax.ShapeDtypeStruct((N_DEV, 1, D, INS), BF16)], [lead_slab])
        if l == 0:
            (conv_w_f,) = _all_gather("ag_conv_w", [conv_w], [jax.ShapeDtypeStruct((L, 3, F2), F32)], [cols_slab(US)])
        w_in_f = jnp.transpose(w_in_sh, (1, 2, 0, 3)).reshape(1, D, IN)
        w_in_p.append(jnp.concatenate([w_in_f[:, :, :AW], w_in_f[:, :, AW + 2 * KVW:], w_in_f[:, :, AW:AW + 2 * KVW]],
                                      axis=2))
        w_o_f += _all_gather("ag_o", [w_o[l:l + 1].astype(BF16)], [jax.ShapeDtypeStruct((1, D, D), BF16)],
                             [rows_slab(OS)])
        w_up_f += _all_gather("ag_up", [w_up[l:l + 1].astype(BF16)], [jax.ShapeDtypeStruct((1, D, F2), BF16)],
                              [cols_slab(US)])
        w_down_f += _all_gather("ag_down", [w_down[l:l + 1].astype(BF16)], [jax.ShapeDtypeStruct((1, F, D), BF16)],
                                [rows_slab(DS)])

    n1g3, n2g3 = norm1_g.reshape(L, 1, D), norm2_g.reshape(L, 1, D)
    qg3, kg3 = q_norm_g.reshape(L, 1, HEAD), k_norm_g.reshape(L, 1, HEAD)
    lng3, lnb3 = sgu_ln_g.reshape(L, 1, GW), sgu_ln_b.reshape(L, 1, GW)
    ag3, og3 = attn_out_g.reshape(L, 1, AW), sgu_out_g.reshape(L, 1, GW)
    cb3 = conv_b.reshape(L, 1, F2)
    ws_b = w_s.astype(BF16)
    wst_b = jnp.swapaxes(w_s, 2, 3).astype(BF16)
    bs_b = jnp.broadcast_to(b_s[..., None], (L, NG, HEAD, HEAD))
    inv_freq = ROPE_THETA ** (-jnp.arange(0, HEAD, 2, dtype=F32) / HEAD)
    ang = jnp.arange(S, dtype=F32)[:, None] * inv_freq[None, :]
    cos2 = jnp.concatenate([jnp.cos(ang), jnp.cos(ang)], axis=1)
    sin2 = jnp.concatenate([-jnp.sin(ang), jnp.sin(ang)], axis=1)

    tn = 512
    t_in, t_d, t_f, t_f2 = _tile(IN, tn, 128), _tile(D, tn, 128), _tile(F, tn, 128), _tile(F2, tn, 128)
    tk_f = _tile(F, 1408, 128)
    tk_in = _tile(IN, 1792, 128)
    tm_f = _tile(F, 512, 128)

    def w_spec(tk, tn_):
        return pl.BlockSpec((None, tk, tn_), lambda i, j, k: (0, k, j))

    def wt_spec(tn_, tk):
        return pl.BlockSpec((None, tn_, tk), lambda i, j, k: (0, j, k))

    def a_spec(tm, tk):
        return pl.BlockSpec((tm, tk), lambda i, j, k: (i, k))

    def at_spec(tk, tm):
        return pl.BlockSpec((tk, tm), lambda i, j, k: (k, i))

    def b_spec(tk, tn_):
        return pl.BlockSpec((tk, tn_), lambda i, j, k: (k, j))

    xs = x.reshape(S, D)
    saved = []
    cur = xs
    for l in range(L):
        h = _rms_fwd("rms1_fwd", cur, n1g3, l)
        z = _mm("mm_in", h, w_in_p[l], M=S, N=IN, K=D, tm=S, tn=t_in, tk=D,
                a_spec=a_spec(S, D), b_spec=w_spec(D, t_in))
        q_r, k_r, v_b = _qkv_prep("qkv_prep", z, qg3, kg3, cos2, sin2, l, cfg)
        attn, mix_l = _attn_fwd("attn_fwd", q_r, k_r, v_b, sink, ag3, l, cfg)
        mix_r = _sgu_fwd("sgu_fwd", z, lng3, lnb3, ws_b, bs_b, og3, l, cfg)
        mixed = jnp.concatenate([mix_l, mix_r], axis=1)
        x1 = _mm("mm_o", mixed, w_o_f[l], M=S, N=D, K=D, tm=S, tn=t_d, tk=D,
                 a_spec=a_spec(S, D), b_spec=w_spec(D, t_d), res=cur)
        h2 = _rms_fwd("rms2_fwd", x1, n2g3, l)
        ap = _mm("mm_up", h2, w_up_f[l], M=S, N=F2, K=D, tm=S, tn=t_f2, tk=D,
                 a_spec=a_spec(S, D), b_spec=w_spec(D, t_f2))
        y_b = _conv_glu_fwd("conv_glu_fwd", ap, conv_w_f, cb3, l, cfg)
        x2 = _mm("mm_down", y_b, w_down_f[l], M=S, N=D, K=F, tm=S, tn=t_d, tk=tk_f,
                 a_spec=a_spec(S, tk_f), b_spec=w_spec(tk_f, t_d), res=x1)
        saved.append(dict(x=cur, h=h, z=z, q=q_r, k=k_r, v=v_b, attn=attn, mixed=mixed, x1=x1, h2=h2, ap=ap, y=y_b))
        cur = x2

    loss_tile, dx, dxb = _loss_bwd("loss", cur, loss_target.reshape(S, D))
    loss = lax.psum(loss_tile[0, 0], ("x", "y", "c"))

    gW = [dict() for _ in range(L)]
    gS = [dict() for _ in range(L)]
    nkf = F // tk_f
    for l in reversed(range(L)):
        sv = saved[l]
        dy = _mm("mm_dy", dxb, w_down_f[l], M=S, N=F, K=D, tm=S, tn=t_f, tk=D, tb=True,
                 a_spec=a_spec(S, D), b_spec=wt_spec(t_f, D))
        gW[l]["w_down"] = _mm("mm_gdown", sv["y"], dxb, M=F, N=D, K=S, tm=tm_f, tn=D, tk=S, ta=True,
                              a_spec=at_spec(S, tm_f), b_spec=b_spec(S, D), out_dtype=BF16)
        dap3, dcw, dcb = _glu_conv_bwd("glu_conv_bwd", dy, sv["ap"], conv_w_f, cb3, l, cfg)
        gS[l]["conv_w"] = jnp.concatenate([dcw[0], dcw[1]], axis=1)
        gS[l]["conv_b"] = jnp.concatenate([dcb[0], dcb[1]], axis=1).reshape(F2)
        dh2 = _mm("mm_dh2", dap3, w_up_f[l], M=S, N=D, K=F2, tm=S, tn=t_d, tk=tk_f, tb=True,
                  a_spec=pl.BlockSpec((None, S, tk_f), lambda i, j, k: (k // nkf, 0, k % nkf)),
                  b_spec=wt_spec(t_d, tk_f))
        gW[l]["w_up"] = _mm("mm_gup", sv["h2"], dap3, M=D, N=F2, K=S, tm=D, tn=US, tk=S, ta=True,
                            a_spec=at_spec(S, D),
                            b_spec=pl.BlockSpec((None, S, US), lambda i, j, k: (j // (N_DEV // 2), 0, j % (N_DEV // 2))),
                            out_dtype=BF16, out_shape=(N_DEV, D, US),
                            out_spec=pl.BlockSpec((None, D, US), lambda i, j, k: (j, 0, 0)))
        dx1, dx1b, dg2 = _rms_bwd("rms2_bwd", sv["x1"], dh2, dx, n2g3, l)
        gS[l]["norm2_g"] = dg2.reshape(D)
        dmix = _mm("mm_dmix", dx1b, w_o_f[l], M=S, N=D, K=D, tm=S, tn=t_d, tk=D, tb=True,
                   a_spec=a_spec(S, D), b_spec=wt_spec(t_d, D))
        gW[l]["w_o"] = _mm("mm_go", sv["mixed"], dx1b, M=D, N=D, K=S, tm=D, tn=t_d, tk=S, ta=True,
                           a_spec=at_spec(S, D), b_spec=b_spec(S, t_d), out_dtype=BF16)
        dq_r, dk_pad, dv_pad, dsink, dag = _attn_bwd("attn_bwd", sv["q"], sv["k"], sv["v"], sv["attn"], dmix,
                                                     sink, ag3, l, cfg)
        gS[l]["sink"] = dsink[:, 0]
        gS[l]["attn_out_g"] = dag.reshape(AW)
        dzgu, dzgv, dws, dbs, dlng, dlnb, dog = _sgu_bwd("sgu_bwd", sv["z"], dmix, lng3, lnb3, ws_b, wst_b, bs_b,
                                                        og3, l, cfg)
        gS[l]["w_s"] = dws
        gS[l]["b_s"] = dbs[:, :, 0]
        gS[l]["sgu_ln_g"] = dlng.reshape(GW)
        gS[l]["sgu_ln_b"] = dlnb.reshape(GW)
        gS[l]["sgu_out_g"] = dog.reshape(GW)
        dzq, dzkv, dqg, dkg = _qkv_prep_bwd("qkv_prep_bwd", sv["z"], dq_r, dk_pad, dv_pad, qg3, kg3, cos2, sin2,
                                            l, cfg)
        gS[l]["q_norm_g"] = dqg.reshape(HEAD)
        gS[l]["k_norm_g"] = dkg.reshape(HEAD)
        dz = jnp.concatenate([dzq, dzgu, dzgv, dzkv], axis=1)
        dh = _mm("mm_dh", dz, w_in_p[l], M=S, N=D, K=IN, tm=S, tn=t_d, tk=tk_in, tb=True,
                 a_spec=a_spec(S, tk_in), b_spec=wt_spec(t_d, tk_in))
        gin_p = _mm("mm_gin", sv["h"], dz, M=D, N=IN, K=S, tm=D, tn=t_in, tk=S, ta=True,
                    a_spec=at_spec(S, D), b_spec=b_spec(S, t_in), out_dtype=BF16)
        gin = jnp.concatenate([gin_p[:, :AW], gin_p[:, AW + 2 * GW:], gin_p[:, AW:AW + 2 * GW]], axis=1)
        gW[l]["w_in"] = jnp.transpose(gin.reshape(D, N_DEV, INS), (1, 0, 2))
        dx, dxb, dg1 = _rms_bwd("rms1_bwd", sv["x"], dh, dx1, n1g3, l)
        gS[l]["norm1_g"] = dg1.reshape(D)
    grad_x = dx.reshape(1, S, D)

    small_pieces = []
    for l in range(L):
        small_pieces += [gS[l][n] for n in SMALL] + [gS[l]["conv_w"]]
    small_send = _pack(small_pieces)
    SR = small_send.shape[0]

    big = [("w_in", D, INS), ("w_o", OS, D), ("w_up", D, US), ("w_down", DS, D)]
    sends, jobs = [], []
    for l in range(L):
        for ri, (name, R, C) in enumerate(big):
            jobs.append((ri, l))
            sends.append(gW[l][name].reshape(N_DEV // 2, 2, R, C))
    halves = _rs_d2d(sends, small_send)
    cvec = jnp.reshape(lax.axis_index("c"), (1,)).astype(jnp.int32)
    chipsums = [_pair_sum("pair_sum", g4, got, cvec, 16) for g4, got in zip(sends, halves[:-1])]
    small_chip = _add2("pair_sum_small", small_send, halves[-1])
    recv_in, recv_o, recv_up, recv_down, recv_small = _rs_ici(
        chipsums, jobs, small_chip,
        [jax.ShapeDtypeStruct((L, N_DEV // 2, R, C), BF16) for _, R, C in big]
        + [jax.ShapeDtypeStruct((N_DEV // 2, SR, 128), F32)])

    grads, deltas, new_m, new_v = {}, {}, {}, {}
    for name, recv in [("w_in", recv_in), ("w_o", recv_o), ("w_up", recv_up), ("w_down", recv_down)]:
        grads[name], deltas[name], new_m[name], new_v[name] = _adamw_sum(
            "adamw_" + name, recv, weights[name], mom_m[name], mom_v[name], 16)

    small_shapes = []
    for l in range(L):
        small_shapes += [weights[n].shape[1:] for n in SMALL] + [(3, F2)]

    def pack_small(src):
        pieces = []
        for l in range(L):
            pieces += [src[n][l] for n in SMALL] + [jnp.zeros((3, F2), F32)]
        return _pack(pieces)

    tr_s = _tile(SR, 512, 8)

    def small_body(r_ref, w_ref, m_ref, v_ref, g_ref, d_ref, nm_ref, nv_ref):
        g = r_ref[0]
        for s in range(1, N_DEV // 2):
            g = g + r_ref[s]
        d, m2, v2 = _adamw_math(w_ref[...], g, m_ref[...], v_ref[...])
        g_ref[...] = g
        d_ref[...] = d
        nm_ref[...] = m2
        nv_ref[...] = v2

    sblk = pl.BlockSpec((tr_s, 128), lambda i: (i, 0))
    sshp = jax.ShapeDtypeStruct((SR, 128), F32)
    sm_g, sm_d, sm_m, sm_v = pl.pallas_call(
        small_body, name="adamw_small", grid=(SR // tr_s,),
        in_specs=[pl.BlockSpec((N_DEV // 2, tr_s, 128), lambda i: (0, i, 0)), sblk, sblk, sblk],
        out_specs=[sblk, sblk, sblk, sblk], out_shape=[sshp, sshp, sshp, sshp],
        compiler_params=_params(("arbitrary",)),
    )(recv_small, pack_small(weights), pack_small(mom_m), pack_small(mom_v))

    per = len(SMALL) + 1
    for store, packed in [(grads, sm_g), (deltas, sm_d), (new_m, sm_m), (new_v, sm_v)]:
        parts = _unpack(packed, small_shapes)
        for i, n in enumerate(SMALL):
            store[n] = jnp.stack([parts[l * per + i] for l in range(L)])
        if store is grads:
            conv_full = jnp.stack([parts[l * per + len(SMALL)] for l in range(L)])
    me_i = _dev_index(_mesh_pos())
    g_cw = lax.dynamic_slice_in_dim(conv_full, me_i * US, US, axis=2)
    grads["conv_w"] = g_cw
    d_cw, m_cw, v_cw = _adamw_plain("adamw_conv_w", g_cw.reshape(L * 3, US), conv_w.reshape(L * 3, US),
                                    m_conv_w.reshape(L * 3, US), v_conv_w.reshape(L * 3, US))
    deltas["conv_w"] = d_cw.reshape(L, 3, US)
    new_m["conv_w"] = m_cw.reshape(L, 3, US)
    new_v["conv_w"] = v_cw.reshape(L, 3, US)

    return (loss, grad_x, *[grads[n] for n in order], *[deltas[n] for n in order],
            *[new_m[n] for n in order], *[new_v[n] for n in order])
```

```python
import jax
import jax.numpy as jnp
from jax import lax
from jax.experimental import pallas as pl
from jax.experimental.pallas import tpu as pltpu
from jax.experimental.pallas import tpu_sc as plsc

F32 = jnp.float32
BF16 = jnp.bfloat16
MESH = pl.DeviceIdType.MESH

N_DEV = 8
HEAD = 128
EPS = 1e-6
MASK_VALUE = -1e30
ROPE_THETA = 10000.0
GELU_C = 0.7978845608028654
GELU_A = 0.044715

ADAM_LR = 0.001
ADAM_B1 = 0.9
ADAM_B2 = 0.999
ADAM_EPS = 1e-08
ADAM_WD = 0.01
ADAM_STEP = 10

VMEM_LIMIT = 56 * 1024 * 1024


def _tile(n, pref, mult):
    best = None
    for t in range(mult, min(n, pref) + 1, mult):
        if n % t == 0:
            best = t
    return n if best is None else best


def _params(sem=None):
    kw = dict(vmem_limit_bytes=VMEM_LIMIT)
    if sem is not None:
        kw["dimension_semantics"] = sem
    return pltpu.CompilerParams(**kw)


def _gelu(x):
    return x * (0.5 * (1.0 + jnp.tanh(GELU_C * (x + GELU_A * (x * x * x)))))


def _gelu_grad(x):
    t = jnp.tanh(GELU_C * (x + GELU_A * (x * x * x)))
    return 0.5 * (1.0 + t) + 0.5 * x * (1.0 - t * t) * (GELU_C * (1.0 + 3.0 * GELU_A * (x * x)))


def _sigmoid(x):
    return 1.0 / (1.0 + jnp.exp(-x))


def _vec_spec(l, n):
    return pl.BlockSpec((None, 1, n), lambda *_: (l, 0, 0))


def _mm(name, a, b, *, M, N, K, tm, tn, tk, a_spec, b_spec, ta=False, tb=False, out_dtype=F32, res=None,
        out_shape=None, out_spec=None):
    nm, nn, nk = M // tm, N // tn, K // tk
    assert nm * tm == M and nn * tn == N and nk * tk == K
    assert not (ta and nk > 1)
    dims = (((1,), (1,)), ((), ())) if tb else (((1,), (0,)), ((), ()))

    def body(*refs):
        refs = list(refs)
        a_ref = refs.pop(0)
        b_ref = refs.pop(0)
        r_ref = refs.pop(0) if res is not None else None
        o_ref = refs.pop(0)
        acc = refs.pop(0) if nk > 1 else None
        at = refs.pop(0) if ta else None
        k = pl.program_id(2)
        if ta:
            @pl.when(pl.program_id(1) == 0)
            def _():
                at[...] = a_ref[...].T
            lhs = at[...]
        else:
            lhs = a_ref[...]
        p = lax.dot_general(lhs, b_ref[...], dims, preferred_element_type=F32)

        def finish(r):
            if r_ref is not None:
                r = r_ref[...] + r
            o_ref[...] = r.astype(out_dtype)

        if nk == 1:
            finish(p)
        else:
            @pl.when(k == 0)
            def _():
                acc[...] = p

            @pl.when(k > 0)
            def _():
                acc[...] += p

            @pl.when(k == nk - 1)
            def _():
                finish(acc[...])

    in_specs = [a_spec, b_spec]
    args = [a, b]
    if res is not None:
        in_specs.append(pl.BlockSpec((tm, tn), lambda i, j, k: (i, j)))
        args.append(res)
    scratch = []
    if nk > 1:
        scratch.append(pltpu.VMEM((tm, tn), F32))
    if ta:
        scratch.append(pltpu.VMEM((tm, tk), BF16))
    return pl.pallas_call(
        body, name=name, grid=(nm, nn, nk),
        in_specs=in_specs,
        out_specs=pl.BlockSpec((tm, tn), lambda i, j, k: (i, j)) if out_spec is None else out_spec,
        out_shape=jax.ShapeDtypeStruct((M, N) if out_shape is None else out_shape, out_dtype),
        scratch_shapes=scratch,
        compiler_params=_params(("arbitrary", "arbitrary", "arbitrary")),
    )(*args)


def _rms_fwd(name, x, g3, l):
    S, D = x.shape
    tr = _tile(S, 256, 16)

    def body(x_ref, g_ref, h_ref):
        xv = x_ref[...]
        r = lax.rsqrt(jnp.mean(xv * xv, axis=-1, keepdims=True) + EPS)
        h_ref[...] = ((xv * r) * g_ref[...]).astype(BF16)

    return pl.pallas_call(
        body, name=name, grid=(S // tr,),
        in_specs=[pl.BlockSpec((tr, D), lambda i: (i, 0)), _vec_spec(l, D)],
        out_specs=pl.BlockSpec((tr, D), lambda i: (i, 0)),
        out_shape=jax.ShapeDtypeStruct((S, D), BF16),
        compiler_params=_params(("arbitrary",)),
    )(x, g3)


def _rope(t, cos2, sin2):
    return t * cos2 + pltpu.roll(t, HEAD // 2, axis=1) * sin2


def _qkv_prep(name, z, qg3, kg3, cos2, sin2, l, cfg):
    S, AW, KVW, NQ, NKV = cfg["S"], cfg["AW"], cfg["KVW"], cfg["NQ"], cfg["NKV"]
    tr = _tile(S, 256, 16)
    kv_blk = (3 * AW) // (2 * KVW)

    def body(zq_ref, zkv_ref, qg_ref, kg_ref, c_ref, s_ref, q_ref, k_ref, v_ref):
        cosv, sinv = c_ref[...], s_ref[...]

        def norm_rope(t, g):
            r = lax.rsqrt(jnp.mean(t * t, axis=-1, keepdims=True) + EPS)
            return _rope((t * r) * g, cosv, sinv)

        for h in range(NQ):
            sl = slice(h * HEAD, (h + 1) * HEAD)
            q_ref[:, sl] = norm_rope(zq_ref[:, sl], qg_ref[...]).astype(BF16)
        for h in range(NKV):
            sl = slice(h * HEAD, (h + 1) * HEAD)
            k_ref[:, sl] = norm_rope(zkv_ref[:, sl], kg_ref[...]).astype(BF16)
        v_ref[...] = zkv_ref[:, KVW:].astype(BF16)

    return pl.pallas_call(
        body, name=name, grid=(S // tr,),
        in_specs=[pl.BlockSpec((tr, AW), lambda i: (i, 0)),
                  pl.BlockSpec((tr, 2 * KVW), lambda i: (i, kv_blk)),
                  _vec_spec(l, HEAD), _vec_spec(l, HEAD),
                  pl.BlockSpec((tr, HEAD), lambda i: (i, 0)),
                  pl.BlockSpec((tr, HEAD), lambda i: (i, 0))],
        out_specs=[pl.BlockSpec((tr, AW), lambda i: (i, 0)),
                   pl.BlockSpec((tr, KVW), lambda i: (i, 0)),
                   pl.BlockSpec((tr, KVW), lambda i: (i, 0))],
        out_shape=[jax.ShapeDtypeStruct((S, AW), BF16),
                   jax.ShapeDtypeStruct((S, KVW), BF16),
                   jax.ShapeDtypeStruct((S, KVW), BF16)],
        compiler_params=_params(("arbitrary",)),
    )(z, z, qg3, kg3, cos2, sin2)


def _band_specs(width, nb):
    return [pl.BlockSpec((HEAD, width), lambda n: (jnp.maximum(n - 1, 0), 0)),
            pl.BlockSpec((HEAD, width), lambda n: (n, 0)),
            pl.BlockSpec((HEAD, width), lambda n: (jnp.minimum(n + 1, nb - 1), 0))]


def _attn_probs(qs, kj, n, sink_of_row, S, G):
    s = lax.dot_general(qs, kj, (((1,), (1,)), ((), ())), preferred_element_type=F32) * (HEAD ** -0.5)
    rows = lax.broadcasted_iota(jnp.int32, (G * HEAD, 3 * HEAD), 0)
    cols = lax.broadcasted_iota(jnp.int32, (G * HEAD, 3 * HEAD), 1)
    qi = rows & (HEAD - 1)
    kpos = n * HEAD - HEAD + cols
    valid = (cols >= qi) & (cols <= qi + 2 * HEAD) & (kpos >= 0) & (kpos < S)
    s = jnp.where(valid, s, MASK_VALUE)
    m = jnp.maximum(jnp.max(s, axis=-1, keepdims=True), sink_of_row)
    p = jnp.exp(s - m)
    e_sink = jnp.exp(sink_of_row - m)
    inv = 1.0 / (jnp.sum(p, axis=-1, keepdims=True) + e_sink)
    return p * inv, e_sink * inv


def _sink_rows(sink_ref, l, j, G):
    hidx = lax.broadcasted_iota(jnp.int32, (G * HEAD, 1), 0) // HEAD
    col = jnp.full((G * HEAD, 1), sink_ref[l, j * G], F32)
    for g in range(1, G):
        col = jnp.where(hidx == g, sink_ref[l, j * G + g], col)
    return col


def _attn_fwd(name, q, k, v, sink, ag3, l, cfg):
    S, AW, KVW, NKV, G = cfg["S"], cfg["AW"], cfg["KVW"], cfg["NKV"], cfg["G"]
    nb = S // HEAD

    def body(q_ref, kp, kc, kn, vp, vc, vn, sink_ref, ag_ref, a_ref, mix_ref):
        n = pl.program_id(0)
        kb = jnp.concatenate([kp[...], kc[...], kn[...]], axis=0)
        vb = jnp.concatenate([vp[...], vc[...], vn[...]], axis=0)
        for j in range(NKV):
            sl = slice(j * HEAD, (j + 1) * HEAD)
            qs = jnp.concatenate([q_ref[:, (j * G + g) * HEAD:(j * G + g + 1) * HEAD] for g in range(G)], axis=0)
            probs, _ = _attn_probs(qs, kb[:, sl], n, _sink_rows(sink_ref, l, j, G), S, G)
            o = jnp.dot(probs.astype(BF16), vb[:, sl], preferred_element_type=F32)
            for g in range(G):
                a_ref[:, (j * G + g) * HEAD:(j * G + g + 1) * HEAD] = o[g * HEAD:(g + 1) * HEAD]
        a = a_ref[...]
        r = lax.rsqrt(jnp.mean(a * a, axis=-1, keepdims=True) + EPS)
        mix_ref[...] = ((a * r) * ag_ref[...]).astype(BF16)

    return pl.pallas_call(
        body, name=name, grid=(nb,),
        in_specs=[pl.BlockSpec((HEAD, AW), lambda n: (n, 0))] + _band_specs(KVW, nb) + _band_specs(KVW, nb)
                 + [pl.BlockSpec(memory_space=pltpu.SMEM), _vec_spec(l, AW)],
        out_specs=[pl.BlockSpec((HEAD, AW), lambda n: (n, 0)), pl.BlockSpec((HEAD, AW), lambda n: (n, 0))],
        out_shape=[jax.ShapeDtypeStruct((S, AW), F32), jax.ShapeDtypeStruct((S, AW), BF16)],
        compiler_params=_params(("arbitrary",)),
    )(q, k, k, k, v, v, v, sink, ag3)


def _sgu_forward_math(gu, gv, lng, lnb):
    u = _gelu(gu)
    vv = _gelu(gv)
    mu = jnp.mean(vv, axis=-1, keepdims=True)
    xc = vv - mu
    rstd = lax.rsqrt(jnp.mean(xc * xc, axis=-1, keepdims=True) + EPS)
    xhat = xc * rstd
    vn = xhat * lng + lnb
    return u, xhat, rstd, vn


def _sgu_fwd(name, z, lng3, lnb3, ws_b, bs_b, og3, l, cfg):
    S, GW, NG = cfg["S"], cfg["GW"], cfg["NG"]

    def body(gu_ref, gv_ref, lng_ref, lnb_ref, ws_ref, bs_ref, og_ref, mix_ref, sg_ref):
        u, _, _, vn = _sgu_forward_math(gu_ref[...], gv_ref[...], lng_ref[...], lnb_ref[...])
        vnb = vn.astype(BF16)
        for h in range(NG):
            sl = slice(h * HEAD, (h + 1) * HEAD)
            f = jnp.dot(ws_ref[h], vnb[:, sl], preferred_element_type=F32) + bs_ref[h]
            sg_ref[:, sl] = u[:, sl] * f
        sg = sg_ref[...]
        r = lax.rsqrt(jnp.mean(sg * sg, axis=-1, keepdims=True) + EPS)
        mix_ref[...] = ((sg * r) * og_ref[...]).astype(BF16)

    return pl.pallas_call(
        body, name=name, grid=(S // HEAD,),
        in_specs=[pl.BlockSpec((HEAD, GW), lambda c: (c, 1)),
                  pl.BlockSpec((HEAD, GW), lambda c: (c, 2)),
                  _vec_spec(l, GW), _vec_spec(l, GW),
                  pl.BlockSpec((None, NG, HEAD, HEAD), lambda c: (l, 0, 0, 0)),
                  pl.BlockSpec((None, NG, HEAD, HEAD), lambda c: (l, 0, 0, 0)),
                  _vec_spec(l, GW)],
        out_specs=pl.BlockSpec((HEAD, GW), lambda c: (c, 0)),
        out_shape=jax.ShapeDtypeStruct((S, GW), BF16),
        scratch_shapes=[pltpu.VMEM((HEAD, GW), F32)],
        compiler_params=_params(("arbitrary",)),
    )(z, z, lng3, lnb3, ws_b, bs_b, og3)


def _conv3(t, w, b, S):
    row = lax.broadcasted_iota(jnp.int32, t.shape, 0)
    dn = jnp.where(row == 0, 0.0, pltpu.roll(t, 1, axis=0))
    up = jnp.where(row == S - 1, 0.0, pltpu.roll(t, S - 1, axis=0))
    return ((b + dn * w[0:1]) + t * w[1:2]) + up * w[2:3], dn, up


def _conv_glu_fwd(name, ap, cw, cb3, l, cfg):
    S, F = cfg["S"], cfg["F"]
    tc = _tile(F, 256, 128)
    nf = F // tc

    def body(g_ref, u_ref, wg_ref, wu_ref, bg_ref, bu_ref, y_ref):
        ag, _, _ = _conv3(g_ref[...], wg_ref[...], bg_ref[...], S)
        au, _, _ = _conv3(u_ref[...], wu_ref[...], bu_ref[...], S)
        y_ref[...] = ((ag * _sigmoid(ag)) * au).astype(BF16)

    return pl.pallas_call(
        body, name=name, grid=(nf,),
        in_specs=[pl.BlockSpec((S, tc), lambda j: (0, j)),
                  pl.BlockSpec((S, tc), lambda j: (0, j + nf)),
                  pl.BlockSpec((None, 3, tc), lambda j: (l, 0, j)),
                  pl.BlockSpec((None, 3, tc), lambda j: (l, 0, j + nf)),
                  pl.BlockSpec((None, 1, tc), lambda j: (l, 0, j)),
                  pl.BlockSpec((None, 1, tc), lambda j: (l, 0, j + nf))],
        out_specs=pl.BlockSpec((S, tc), lambda j: (0, j)),
        out_shape=jax.ShapeDtypeStruct((S, F), BF16),
        compiler_params=_params(("arbitrary",)),
    )(ap, ap, cw, cw, cb3, cb3)


def _loss_bwd(name, y, target):
    S, D = y.shape
    tr = _tile(S, 256, 16)

    def body(y_ref, t_ref, loss_ref, d_ref, db_ref):
        @pl.when(pl.program_id(0) == 0)
        def _():
            loss_ref[...] = jnp.zeros_like(loss_ref)

        err = y_ref[...] - t_ref[...]
        part = 0.5 * jnp.sum(jnp.mean(err * err, axis=-1, keepdims=True), axis=0, keepdims=True)
        loss_ref[...] += jnp.broadcast_to(part, loss_ref.shape)
        d = err * (1.0 / D)
        d_ref[...] = d
        db_ref[...] = d.astype(BF16)

    return pl.pallas_call(
        body, name=name, grid=(S // tr,),
        in_specs=[pl.BlockSpec((tr, D), lambda i: (i, 0)), pl.BlockSpec((tr, D), lambda i: (i, 0))],
        out_specs=[pl.BlockSpec((8, 128), lambda i: (0, 0)),
                   pl.BlockSpec((tr, D), lambda i: (i, 0)),
                   pl.BlockSpec((tr, D), lambda i: (i, 0))],
        out_shape=[jax.ShapeDtypeStruct((8, 128), F32),
                   jax.ShapeDtypeStruct((S, D), F32),
                   jax.ShapeDtypeStruct((S, D), BF16)],
        compiler_params=_params(("arbitrary",)),
    )(y, target)


def _rms_bwd(name, x, dh, dres, g3, l):
    S, D = x.shape
    tr = _tile(S, 256, 16)

    def body(x_ref, dh_ref, dr_ref, g_ref, dx_ref, dxb_ref, dg_ref):
        @pl.when(pl.program_id(0) == 0)
        def _():
            dg_ref[...] = jnp.zeros_like(dg_ref)

        xv = x_ref[...]
        dhv = dh_ref[...]
        r = lax.rsqrt(jnp.mean(xv * xv, axis=-1, keepdims=True) + EPS)
        xhat = xv * r
        dhg = dhv * g_ref[...]
        dx = dr_ref[...] + r * (dhg - xhat * jnp.mean(dhg * xhat, axis=-1, keepdims=True))
        dx_ref[...] = dx
        dxb_ref[...] = dx.astype(BF16)
        dg_ref[...] += jnp.sum(dhv * xhat, axis=0, keepdims=True)

    return pl.pallas_call(
        body, name=name, grid=(S // tr,),
        in_specs=[pl.BlockSpec((tr, D), lambda i: (i, 0)), pl.BlockSpec((tr, D), lambda i: (i, 0)),
                  pl.BlockSpec((tr, D), lambda i: (i, 0)), _vec_spec(l, D)],
        out_specs=[pl.BlockSpec((tr, D), lambda i: (i, 0)), pl.BlockSpec((tr, D), lambda i: (i, 0)),
                   pl.BlockSpec((1, D), lambda i: (0, 0))],
        out_shape=[jax.ShapeDtypeStruct((S, D), F32), jax.ShapeDtypeStruct((S, D), BF16),
                   jax.ShapeDtypeStruct((1, D), F32)],
        compiler_params=_params(("arbitrary",)),
    )(x, dh, dres, g3)


def _glu_conv_bwd(name, dy, ap, cw, cb3, l, cfg):
    S, F = cfg["S"], cfg["F"]
    tc = _tile(F, 256, 128)
    nf = F // tc

    def body(dy_ref, g_ref, u_ref, wg_ref, wu_ref, bg_ref, bu_ref, dap_ref, dw_ref, db_ref):
        apg, apu = g_ref[...], u_ref[...]
        wg, wu = wg_ref[...], wu_ref[...]
        ag, g_dn, g_up = _conv3(apg, wg, bg_ref[...], S)
        au, u_dn, u_up = _conv3(apu, wu, bu_ref[...], S)
        sig = _sigmoid(ag)
        dyv = dy_ref[...]
        da_u = dyv * (ag * sig)
        da_g = (dyv * au) * (sig * (1.0 + ag * (1.0 - sig)))
        row = lax.broadcasted_iota(jnp.int32, (S, tc), 0)

        def back(da, w):
            nxt = jnp.where(row == S - 1, 0.0, pltpu.roll(da, S - 1, axis=0))
            prv = jnp.where(row == 0, 0.0, pltpu.roll(da, 1, axis=0))
            return (nxt * w[0:1] + da * w[1:2]) + prv * w[2:3]

        dap_ref[0] = back(da_g, wg).astype(BF16)
        dap_ref[1] = back(da_u, wu).astype(BF16)

        def wgrad(da, dn, t, up):
            return jnp.concatenate([jnp.sum(da * dn, axis=0, keepdims=True),
                                    jnp.sum(da * t, axis=0, keepdims=True),
                                    jnp.sum(da * up, axis=0, keepdims=True)], axis=0)

        dw_ref[0] = wgrad(da_g, g_dn, apg, g_up)
        dw_ref[1] = wgrad(da_u, u_dn, apu, u_up)
        db_ref[0] = jnp.sum(da_g, axis=0, keepdims=True)
        db_ref[1] = jnp.sum(da_u, axis=0, keepdims=True)

    return pl.pallas_call(
        body, name=name, grid=(nf,),
        in_specs=[pl.BlockSpec((S, tc), lambda j: (0, j)),
                  pl.BlockSpec((S, tc), lambda j: (0, j)),
                  pl.BlockSpec((S, tc), lambda j: (0, j + nf)),
                  pl.BlockSpec((None, 3, tc), lambda j: (l, 0, j)),
                  pl.BlockSpec((None, 3, tc), lambda j: (l, 0, j + nf)),
                  pl.BlockSpec((None, 1, tc), lambda j: (l, 0, j)),
                  pl.BlockSpec((None, 1, tc), lambda j: (l, 0, j + nf))],
        out_specs=[pl.BlockSpec((2, S, tc), lambda j: (0, 0, j)),
                   pl.BlockSpec((2, 3, tc), lambda j: (0, 0, j)),
                   pl.BlockSpec((2, 1, tc), lambda j: (0, 0, j))],
        out_shape=[jax.ShapeDtypeStruct((2, S, F), BF16),
                   jax.ShapeDtypeStruct((2, 3, F), F32),
                   jax.ShapeDtypeStruct((2, 1, F), F32)],
        compiler_params=_params(("arbitrary",)),
    )(dy, ap, ap, cw, cw, cb3, cb3)


def _attn_bwd(name, q, k, v, attn, dmix, sink, ag3, l, cfg):
    S, AW, KVW, NQ, NKV, G = cfg["S"], cfg["AW"], cfg["KVW"], cfg["NQ"], cfg["NKV"], cfg["G"]
    nb = S // HEAD
    scale = HEAD ** -0.5

    def body(q_ref, kp, kc, kn, vp, vc, vn, a_ref, dm_ref, sink_ref, ag_ref,
             dq_ref, dk_ref, dv_ref, dsink_ref, dag_ref, da_scr):
        n = pl.program_id(0)

        @pl.when(n == 0)
        def _():
            dk_ref[...] = jnp.zeros_like(dk_ref)
            dv_ref[...] = jnp.zeros_like(dv_ref)
            dsink_ref[...] = jnp.zeros_like(dsink_ref)
            dag_ref[...] = jnp.zeros_like(dag_ref)

        a = a_ref[...]
        dm = dm_ref[...]
        r = lax.rsqrt(jnp.mean(a * a, axis=-1, keepdims=True) + EPS)
        xhat = a * r
        dmg = dm * ag_ref[...]
        da_scr[...] = r * (dmg - xhat * jnp.mean(dmg * xhat, axis=-1, keepdims=True))
        dag_ref[...] += jnp.sum(dm * xhat, axis=0, keepdims=True)

        kb = jnp.concatenate([kp[...], kc[...], kn[...]], axis=0)
        vb = jnp.concatenate([vp[...], vc[...], vn[...]], axis=0)
        band = pl.ds(pl.multiple_of(n * HEAD, HEAD), 3 * HEAD)
        for j in range(NKV):
            sl = slice(j * HEAD, (j + 1) * HEAD)
            heads = [slice((j * G + g) * HEAD, (j * G + g + 1) * HEAD) for g in range(G)]
            qs = jnp.concatenate([q_ref[:, hs] for hs in heads], axis=0)
            do = jnp.concatenate([da_scr[:, hs] for hs in heads], axis=0)
            kj, vj = kb[:, sl], vb[:, sl]
            probs, p_sink = _attn_probs(qs, kj, n, _sink_rows(sink_ref, l, j, G), S, G)
            dob = do.astype(BF16)
            dprobs = lax.dot_general(dob, vj, (((1,), (1,)), ((), ())), preferred_element_type=F32)
            delta = jnp.sum(dprobs * probs, axis=-1, keepdims=True)
            ds = (probs * (dprobs - delta)) * scale
            dsb = ds.astype(BF16)
            dsk = -(p_sink * delta)
            dq = jnp.dot(dsb, kj, preferred_element_type=F32)
            for g in range(G):
                dq_ref[:, heads[g]] = dq[g * HEAD:(g + 1) * HEAD]
                part = jnp.sum(dsk[g * HEAD:(g + 1) * HEAD], axis=0, keepdims=True)
                dsink_ref[j * G + g:j * G + g + 1, :] += jnp.broadcast_to(part, (1, HEAD))
            dk_ref[band, sl] += lax.dot_general(dsb, qs, (((0,), (0,)), ((), ())), preferred_element_type=F32)
            dv_ref[band, sl] += lax.dot_general(probs.astype(BF16), dob, (((0,), (0,)), ((), ())),
                                                preferred_element_type=F32)

    return pl.pallas_call(
        body, name=name, grid=(nb,),
        in_specs=[pl.BlockSpec((HEAD, AW), lambda n: (n, 0))] + _band_specs(KVW, nb) + _band_specs(KVW, nb)
                 + [pl.BlockSpec((HEAD, AW), lambda n: (n, 0)),
                    pl.BlockSpec((HEAD, AW), lambda n: (n, 0)),
                    pl.BlockSpec(memory_space=pltpu.SMEM), _vec_spec(l, AW)],
        out_specs=[pl.BlockSpec((HEAD, AW), lambda n: (n, 0)),
                   pl.BlockSpec((S + 2 * HEAD, KVW), lambda n: (0, 0)),
                   pl.BlockSpec((S + 2 * HEAD, KVW), lambda n: (0, 0)),
                   pl.BlockSpec((NQ, HEAD), lambda n: (0, 0)),
                   pl.BlockSpec((1, AW), lambda n: (0, 0))],
        out_shape=[jax.ShapeDtypeStruct((S, AW), F32),
                   jax.ShapeDtypeStruct((S + 2 * HEAD, KVW), F32),
                   jax.ShapeDtypeStruct((S + 2 * HEAD, KVW), F32),
                   jax.ShapeDtypeStruct((NQ, HEAD), F32),
                   jax.ShapeDtypeStruct((1, AW), F32)],
        scratch_shapes=[pltpu.VMEM((HEAD, AW), F32)],
        compiler_params=_params(("arbitrary",)),
    )(q, k, k, k, v, v, v, attn, dmix, sink, ag3)


def _qkv_prep_bwd(name, z, dq, dk_pad, dv_pad, qg3, kg3, cos2, sin2, l, cfg):
    S, AW, KVW, NQ, NKV = cfg["S"], cfg["AW"], cfg["KVW"], cfg["NQ"], cfg["NKV"]
    kv_blk = (3 * AW) // (2 * KVW)

    def body(zq_ref, zkv_ref, dq_ref, dk_ref, dv_ref, qg_ref, kg_ref, c_ref, s_ref,
             dzq_ref, dzkv_ref, dqg_ref, dkg_ref):
        @pl.when(pl.program_id(0) == 0)
        def _():
            dqg_ref[...] = jnp.zeros_like(dqg_ref)
            dkg_ref[...] = jnp.zeros_like(dkg_ref)

        cosv, sinv = c_ref[...], s_ref[...]

        def back(t, dr, g):
            r = lax.rsqrt(jnp.mean(t * t, axis=-1, keepdims=True) + EPS)
            xhat = t * r
            dn = dr * cosv + pltpu.roll(dr * sinv, HEAD // 2, axis=1)
            dxh = dn * g
            dt = r * (dxh - xhat * jnp.mean(dxh * xhat, axis=-1, keepdims=True))
            return dt, jnp.sum(dn * xhat, axis=0, keepdims=True)

        gq = jnp.zeros((1, HEAD), F32)
        for h in range(NQ):
            sl = slice(h * HEAD, (h + 1) * HEAD)
            dt, gpart = back(zq_ref[:, sl], dq_ref[:, sl], qg_ref[...])
            dzq_ref[:, sl] = dt.astype(BF16)
            gq = gq + gpart
        dqg_ref[...] += gq
        gk = jnp.zeros((1, HEAD), F32)
        for h in range(NKV):
            sl = slice(h * HEAD, (h + 1) * HEAD)
            dt, gpart = back(zkv_ref[:, sl], dk_ref[:, sl], kg_ref[...])
            dzkv_ref[:, sl] = dt.astype(BF16)
            gk = gk + gpart
        dkg_ref[...] += gk
        dzkv_ref[:, KVW:] = dv_ref[...].astype(BF16)

    return pl.pallas_call(
        body, name=name, grid=(S // HEAD,),
        in_specs=[pl.BlockSpec((HEAD, AW), lambda i: (i, 0)),
                  pl.BlockSpec((HEAD, 2 * KVW), lambda i: (i, kv_blk)),
                  pl.BlockSpec((HEAD, AW), lambda i: (i, 0)),
                  pl.BlockSpec((HEAD, KVW), lambda i: (i + 1, 0)),
                  pl.BlockSpec((HEAD, KVW), lambda i: (i + 1, 0)),
                  _vec_spec(l, HEAD), _vec_spec(l, HEAD),
                  pl.BlockSpec((HEAD, HEAD), lambda i: (i, 0)),
                  pl.BlockSpec((HEAD, HEAD), lambda i: (i, 0))],
        out_specs=[pl.BlockSpec((HEAD, AW), lambda i: (i, 0)),
                   pl.BlockSpec((HEAD, 2 * KVW), lambda i: (i, 0)),
                   pl.BlockSpec((1, HEAD), lambda i: (0, 0)),
                   pl.BlockSpec((1, HEAD), lambda i: (0, 0))],
        out_shape=[jax.ShapeDtypeStruct((S, AW), BF16),
                   jax.ShapeDtypeStruct((S, 2 * KVW), BF16),
                   jax.ShapeDtypeStruct((1, HEAD), F32),
                   jax.ShapeDtypeStruct((1, HEAD), F32)],
        compiler_params=_params(("arbitrary",)),
    )(z, z, dq, dk_pad, dv_pad, qg3, kg3, cos2, sin2)


def _sgu_bwd(name, z, dmix, lng3, lnb3, ws_b, wst_b, bs_b, og3, l, cfg):
    S, GW, NG = cfg["S"], cfg["GW"], cfg["NG"]

    def body(gu_ref, gv_ref, dm_ref, lng_ref, lnb_ref, ws_ref, wst_ref, bs_ref, og_ref,
             dgu_ref, dgv_ref, dws_ref, dbs_ref, dlng_ref, dlnb_ref, dog_ref, sg_scr, f_scr, dvn_scr):
        @pl.when(pl.program_id(0) == 0)
        def _():
            dws_ref[...] = jnp.zeros_like(dws_ref)
            dbs_ref[...] = jnp.zeros_like(dbs_ref)
            dlng_ref[...] = jnp.zeros_like(dlng_ref)
            dlnb_ref[...] = jnp.zeros_like(dlnb_ref)
            dog_ref[...] = jnp.zeros_like(dog_ref)

        gu, gv = gu_ref[...], gv_ref[...]
        lng = lng_ref[...]
        u, xhat, rstd, vn = _sgu_forward_math(gu, gv, lng, lnb_ref[...])
        vnb = vn.astype(BF16)
        for h in range(NG):
            sl = slice(h * HEAD, (h + 1) * HEAD)
            f = jnp.dot(ws_ref[h], vnb[:, sl], preferred_element_type=F32) + bs_ref[h]
            f_scr[:, sl] = f
            sg_scr[:, sl] = u[:, sl] * f
        sg = sg_scr[...]
        dm = dm_ref[...]
        r = lax.rsqrt(jnp.mean(sg * sg, axis=-1, keepdims=True) + EPS)
        sghat = sg * r
        dmg = dm * og_ref[...]
        dsg = r * (dmg - sghat * jnp.mean(dmg * sghat, axis=-1, keepdims=True))
        dog_ref[...] += jnp.sum(dm * sghat, axis=0, keepdims=True)
        du = dsg * f_scr[...]
        df = dsg * u
        dfb = df.astype(BF16)
        for h in range(NG):
            sl = slice(h * HEAD, (h + 1) * HEAD)
            dvn_scr[:, sl] = jnp.dot(wst_ref[h], dfb[:, sl], preferred_element_type=F32)
            dws_ref[h] += lax.dot_general(dfb[:, sl], vnb[:, sl], (((1,), (1,)), ((), ())),
                                          preferred_element_type=F32)
            dbs_ref[h] += jnp.broadcast_to(jnp.sum(df[:, sl], axis=-1, keepdims=True), (HEAD, HEAD))
        dvn = dvn_scr[...]
        dlng_ref[...] += jnp.sum(dvn * xhat, axis=0, keepdims=True)
        dlnb_ref[...] += jnp.sum(dvn, axis=0, keepdims=True)
        dxh = dvn * lng
        dvv = rstd * ((dxh - jnp.mean(dxh, axis=-1, keepdims=True))
                      - xhat * jnp.mean(dxh * xhat, axis=-1, keepdims=True))
        dgu_ref[...] = (du * _gelu_grad(gu)).astype(BF16)
        dgv_ref[...] = (dvv * _gelu_grad(gv)).astype(BF16)

    vec = pl.BlockSpec((1, GW), lambda c: (0, 0))
    mat = pl.BlockSpec((NG, HEAD, HEAD), lambda c: (0, 0, 0))
    wsp = pl.BlockSpec((None, NG, HEAD, HEAD), lambda c: (l, 0, 0, 0))
    return pl.pallas_call(
        body, name=name, grid=(S // HEAD,),
        in_specs=[pl.BlockSpec((HEAD, GW), lambda c: (c, 1)),
                  pl.BlockSpec((HEAD, GW), lambda c: (c, 2)),
                  pl.BlockSpec((HEAD, GW), lambda c: (c, 1)),
                  _vec_spec(l, GW), _vec_spec(l, GW), wsp, wsp, wsp, _vec_spec(l, GW)],
        out_specs=[pl.BlockSpec((HEAD, GW), lambda c: (c, 0)), pl.BlockSpec((HEAD, GW), lambda c: (c, 0)),
                   mat, mat, vec, vec, vec],
        out_shape=[jax.ShapeDtypeStruct((S, GW), BF16), jax.ShapeDtypeStruct((S, GW), BF16),
                   jax.ShapeDtypeStruct((NG, HEAD, HEAD), F32), jax.ShapeDtypeStruct((NG, HEAD, HEAD), F32),
                   jax.ShapeDtypeStruct((1, GW), F32), jax.ShapeDtypeStruct((1, GW), F32),
                   jax.ShapeDtypeStruct((1, GW), F32)],
        scratch_shapes=[pltpu.VMEM((HEAD, GW), F32), pltpu.VMEM((HEAD, GW), F32), pltpu.VMEM((HEAD, GW), F32)],
        compiler_params=_params(("arbitrary",)),
    )(z, z, dmix, lng3, lnb3, ws_b, wst_b, bs_b, og3)


def _mesh_pos():
    x, y, c = lax.axis_index("x"), lax.axis_index("y"), lax.axis_index("c")
    return x, y, c


def _dev_index(p):
    return 4 * p[0] + 2 * p[1] + p[2]


def _handshake(peers):
    barrier = pltpu.get_barrier_semaphore()
    for p in peers:
        pl.semaphore_signal(barrier, inc=1, device_id=p, device_id_type=MESH)
    pl.semaphore_wait(barrier, len(peers))


def _gather_body(na, slabs):
    def body(*refs):
        ins, outs = refs[:na], refs[na:2 * na]
        send_sems, recv_sems, local_sems = refs[2 * na:]
        x, y, c = _mesh_pos()
        me, sib = (x, y, c), (x, y, 1 - c)
        chips = [(1 - x, y), (x, 1 - y), (1 - x, 1 - y)]
        _handshake([sib] + [(*chip, c) for chip in chips])

        def copy(a, k, block, to, src=None):
            dst = slabs[a](outs[a], _dev_index(block))
            return pltpu.make_async_remote_copy(
                src_ref=dst if src is None else src, dst_ref=dst,
                send_sem=send_sems.at[7 * a + k], recv_sem=recv_sems.at[7 * a + k],
                device_id=to, device_id_type=MESH)

        local = [pltpu.make_async_copy(ins[a], slabs[a](outs[a], _dev_index(me)), local_sems.at[a])
                 for a in range(na)]
        for cp in local:
            cp.start()
        first = []
        for a in range(na):
            first.append(copy(a, 0, me, sib, src=ins[a]))
            first += [copy(a, 1 + j, me, (*chip, c), src=ins[a]) for j, chip in enumerate(chips)]
        for cp in first:
            cp.start()
        passed = []
        for j, chip in enumerate(chips):
            for a in range(na):
                copy(a, 1 + j, (*chip, c), me).wait_recv()
                fwd = copy(a, 4 + j, (*chip, c), sib)
                fwd.start()
                passed.append(fwd)
        for a in range(na):
            copy(a, 0, sib, me).wait_recv()
            for j, chip in enumerate(chips):
                copy(a, 4 + j, (*chip, 1 - c), me).wait_recv()
        for cp in first + passed:
            cp.wait_send()
        for cp in local:
            cp.wait()

    return body


def _all_gather(name, cid, shards, out_shapes, slabs):
    na = len(shards)
    return pl.kernel(
        _gather_body(na, slabs), out_type=out_shapes,
        mesh=plsc.ScalarSubcoreMesh(axis_name="seq", num_cores=1), name=name,
        scratch_types=[pltpu.SemaphoreType.DMA((7 * na,)), pltpu.SemaphoreType.DMA((7 * na,)),
                       pltpu.SemaphoreType.DMA((na,))],
        compiler_params=pltpu.CompilerParams(collective_id=cid),
    )(*shards)


def _d2d_body(nb, ns):
    n = nb + ns

    def body(*refs):
        ins, outs = refs[:n], refs[n:2 * n]
        send_sems, recv_sems = refs[2 * n:]
        x, y, c = _mesh_pos()
        sib = (x, y, 1 - c)
        _handshake([sib])
        copies = []
        for t in range(n):
            cp = pltpu.make_async_remote_copy(
                src_ref=ins[t].at[:, 1 - c] if t < nb else ins[t], dst_ref=outs[t],
                send_sem=send_sems.at[t], recv_sem=recv_sems.at[t],
                device_id=sib, device_id_type=MESH)
            cp.start()
            copies.append(cp)
        for cp in copies:
            cp.wait()

    return body


def _rs_d2d(name, cid, bigs, smalls):
    shapes = [jax.ShapeDtypeStruct((g.shape[0],) + g.shape[2:], g.dtype) for g in bigs]
    shapes += [jax.ShapeDtypeStruct(s.shape, s.dtype) for s in smalls]
    n = len(shapes)
    return pl.kernel(
        _d2d_body(len(bigs), len(smalls)), out_type=shapes,
        mesh=plsc.ScalarSubcoreMesh(axis_name="seq", num_cores=1), name=name,
        scratch_types=[pltpu.SemaphoreType.DMA((n,)), pltpu.SemaphoreType.DMA((n,))],
        compiler_params=pltpu.CompilerParams(collective_id=cid),
    )(*bigs, *smalls)


def _pair_sum(name, g4, recv, cvec, row_mult):
    Q, _, R, C = g4.shape
    tr = _tile(R, 256, row_mult)

    def body(c_ref, a_ref, b_ref, o_ref):
        o_ref[...] = (a_ref[...].astype(F32) + b_ref[...].astype(F32)).astype(BF16)

    grid_spec = pltpu.PrefetchScalarGridSpec(
        num_scalar_prefetch=1, grid=(Q, R // tr),
        in_specs=[pl.BlockSpec((None, None, tr, C), lambda q, i, c_ref: (q, c_ref[0], i, 0)),
                  pl.BlockSpec((None, tr, C), lambda q, i, c_ref: (q, i, 0))],
        out_specs=pl.BlockSpec((None, tr, C), lambda q, i, c_ref: (q, i, 0)))
    return pl.pallas_call(
        body, name=name, grid_spec=grid_spec, out_shape=jax.ShapeDtypeStruct((Q, R, C), BF16),
        compiler_params=_params(("arbitrary", "arbitrary")),
    )(cvec, g4, recv)


def _add2(name, a, b):
    R, C = a.shape
    tr = _tile(R, 512, 8)

    def body(a_ref, b_ref, o_ref):
        o_ref[...] = a_ref[...] + b_ref[...]

    blk = pl.BlockSpec((tr, C), lambda i: (i, 0))
    return pl.pallas_call(body, name=name, grid=(R // tr,), in_specs=[blk, blk], out_specs=blk,
                          out_shape=jax.ShapeDtypeStruct((R, C), a.dtype),
                          compiler_params=_params(("arbitrary",)))(a, b)


def _ici_body(nb, ns):
    n = nb + ns

    def body(*refs):
        ins, outs = refs[:n], refs[n:2 * n]
        send_sems, recv_sems, local_sems = refs[2 * n:]
        x, y, c = _mesh_pos()
        q_me = 2 * x + y
        chips = [(x ^ (k >> 1), y ^ (k & 1)) for k in range(1, 4)]
        _handshake([(px, py, c) for px, py in chips])
        copies = []
        for t in range(n):
            src_of = (lambda q, t=t: ins[t].at[q]) if t < nb else (lambda q, t=t: ins[t])
            dst = outs[t].at[q_me]
            loc = pltpu.make_async_copy(src_of(q_me), dst, local_sems.at[t])
            loc.start()
            copies.append(loc)
            for k, (px, py) in enumerate(chips):
                cp = pltpu.make_async_remote_copy(
                    src_ref=src_of(2 * px + py), dst_ref=dst,
                    send_sem=send_sems.at[3 * t + k], recv_sem=recv_sems.at[3 * t + k],
                    device_id=(px, py, c), device_id_type=MESH)
                cp.start()
                copies.append(cp)
        for cp in copies:
            cp.wait()

    return body


def _rs_ici(name, cid, bigs, smalls):
    shapes = [jax.ShapeDtypeStruct(g.shape, g.dtype) for g in bigs]
    shapes += [jax.ShapeDtypeStruct((N_DEV // 2,) + s.shape, s.dtype) for s in smalls]
    n = len(shapes)
    return pl.kernel(
        _ici_body(len(bigs), len(smalls)), out_type=shapes,
        mesh=plsc.ScalarSubcoreMesh(axis_name="seq", num_cores=1), name=name,
        scratch_types=[pltpu.SemaphoreType.DMA((3 * n,)), pltpu.SemaphoreType.DMA((3 * n,)),
                       pltpu.SemaphoreType.DMA((n,))],
        compiler_params=pltpu.CompilerParams(collective_id=cid),
    )(*bigs, *smalls)


def _adamw_math(w, g, m, v):
    m2 = ADAM_B1 * m + (1.0 - ADAM_B1) * g
    v2 = ADAM_B2 * v + (1.0 - ADAM_B2) * (g * g)
    m_hat = m2 / (1.0 - ADAM_B1 ** ADAM_STEP)
    v_hat = v2 / (1.0 - ADAM_B2 ** ADAM_STEP)
    delta = -ADAM_LR * (m_hat / (jnp.sqrt(v_hat) + ADAM_EPS) + ADAM_WD * w)
    return delta, m2, v2


def _adamw_sum(name, recv, w, m, v, l, prev, row_mult):
    NS, R, C = recv.shape
    L = w.shape[0]
    tr = _tile(R, 128, row_mult)

    def body(*refs):
        r_ref, w_ref, m_ref, v_ref = refs[:4]
        g_ref, d_ref, nm_ref, nv_ref = refs[-4:]
        g = r_ref[0].astype(F32)
        for s in range(1, NS):
            g = g + r_ref[s].astype(F32)
        d, m2, v2 = _adamw_math(w_ref[...], g, m_ref[...], v_ref[...])
        g_ref[...] = g
        d_ref[...] = d
        nm_ref[...] = m2
        nv_ref[...] = v2

    blk = pl.BlockSpec((None, tr, C), lambda i: (l, i, 0))
    shp = jax.ShapeDtypeStruct((L, R, C), F32)
    in_specs = [pl.BlockSpec((NS, tr, C), lambda i: (0, i, 0)), blk, blk, blk]
    args = [recv, w, m, v]
    aliases = {}
    if prev is not None:
        in_specs += [pl.BlockSpec(memory_space=pl.ANY)] * 4
        args += list(prev)
        aliases = {4 + i: i for i in range(4)}
    return pl.pallas_call(
        body, name=name, grid=(R // tr,),
        in_specs=in_specs, out_specs=[blk, blk, blk, blk], out_shape=[shp, shp, shp, shp],
        input_output_aliases=aliases,
        compiler_params=_params(("arbitrary",)),
    )(*args)


def _adamw_plain(name, g, w, m, v):
    def body(g_ref, w_ref, m_ref, v_ref, d_ref, nm_ref, nv_ref):
        d, m2, v2 = _adamw_math(w_ref[...], g_ref[...], m_ref[...], v_ref[...])
        d_ref[...] = d
        nm_ref[...] = m2
        nv_ref[...] = v2

    shp = jax.ShapeDtypeStruct(g.shape, F32)
    return pl.pallas_call(body, name=name, out_shape=[shp, shp, shp], compiler_params=_params())(g, w, m, v)


SMALL = ["norm1_g", "q_norm_g", "k_norm_g", "sink", "sgu_ln_g", "sgu_ln_b", "w_s", "b_s",
         "attn_out_g", "sgu_out_g", "norm2_g", "conv_b"]
PACK_ALIGN = 1024


def _pack(pieces):
    flat = []
    for p in pieces:
        f = p.reshape(-1).astype(F32)
        pad = (-f.shape[0]) % PACK_ALIGN
        flat.append(jnp.pad(f, (0, pad)) if pad else f)
    return jnp.concatenate(flat).reshape(-1, 128)


def _unpack(packed, shapes):
    flat = packed.reshape(-1)
    out, off = [], 0
    for shp in shapes:
        n = 1
        for d in shp:
            n *= d
        out.append(flat[off:off + n].reshape(shp))
        off += n + ((-n) % PACK_ALIGN)
    return out


def kernel(x, norm1_g, w_in, q_norm_g, k_norm_g, sink, sgu_ln_g, sgu_ln_b, w_s, b_s, attn_out_g, sgu_out_g, w_o, norm2_g, w_up, conv_w, conv_b, w_down, loss_target, m_norm1_g, m_w_in, m_q_norm_g, m_k_norm_g, m_sink, m_sgu_ln_g, m_sgu_ln_b, m_w_s, m_b_s, m_attn_out_g, m_sgu_out_g, m_w_o, m_norm2_g, m_w_up, m_conv_w, m_conv_b, m_w_down, v_norm1_g, v_w_in, v_q_norm_g, v_k_norm_g, v_sink, v_sgu_ln_g, v_sgu_ln_b, v_w_s, v_b_s, v_attn_out_g, v_sgu_out_g, v_w_o, v_norm2_g, v_w_up, v_conv_w, v_conv_b, v_w_down):
    weights = dict(norm1_g=norm1_g, w_in=w_in, q_norm_g=q_norm_g, k_norm_g=k_norm_g, sink=sink, sgu_ln_g=sgu_ln_g,
                   sgu_ln_b=sgu_ln_b, w_s=w_s, b_s=b_s, attn_out_g=attn_out_g, sgu_out_g=sgu_out_g, w_o=w_o,
                   norm2_g=norm2_g, w_up=w_up, conv_w=conv_w, conv_b=conv_b, w_down=w_down)
    mom_m = dict(norm1_g=m_norm1_g, w_in=m_w_in, q_norm_g=m_q_norm_g, k_norm_g=m_k_norm_g, sink=m_sink,
                 sgu_ln_g=m_sgu_ln_g, sgu_ln_b=m_sgu_ln_b, w_s=m_w_s, b_s=m_b_s, attn_out_g=m_attn_out_g,
                 sgu_out_g=m_sgu_out_g, w_o=m_w_o, norm2_g=m_norm2_g, w_up=m_w_up, conv_w=m_conv_w,
                 conv_b=m_conv_b, w_down=m_w_down)
    mom_v = dict(norm1_g=v_norm1_g, w_in=v_w_in, q_norm_g=v_q_norm_g, k_norm_g=v_k_norm_g, sink=v_sink,
                 sgu_ln_g=v_sgu_ln_g, sgu_ln_b=v_sgu_ln_b, w_s=v_w_s, b_s=v_b_s, attn_out_g=v_attn_out_g,
                 sgu_out_g=v_sgu_out_g, w_o=v_w_o, norm2_g=v_norm2_g, w_up=v_w_up, conv_w=v_conv_w,
                 conv_b=v_conv_b, w_down=v_w_down)
    order = ["norm1_g", "w_in", "q_norm_g", "k_norm_g", "sink", "sgu_ln_g", "sgu_ln_b", "w_s", "b_s",
             "attn_out_g", "sgu_out_g", "w_o", "norm2_g", "w_up", "conv_w", "conv_b", "w_down"]

    _, S, D = x.shape
    L = w_in.shape[0]
    AW = D // 2
    NQ = AW // HEAD
    NKV = max(1, NQ // 4)
    G = NQ // NKV
    KVW = NKV * HEAD
    GW = D - AW
    NG = GW // HEAD
    IN = AW + 2 * KVW + 2 * GW
    INS = w_in.shape[2]
    OS = w_o.shape[1]
    US = w_up.shape[2]
    DS = w_down.shape[1]
    F2 = US * N_DEV
    F = F2 // 2
    assert INS * N_DEV == IN and OS * N_DEV == D and DS * N_DEV == F and AW == GW and (3 * AW) % (2 * KVW) == 0
    cfg = dict(S=S, D=D, AW=AW, NQ=NQ, NKV=NKV, G=G, KVW=KVW, GW=GW, NG=NG, F=F, F2=F2)

    def rows_slab(n):
        return lambda ref, idx: ref.at[:, pl.ds(pl.multiple_of(idx * n, n), n), :]

    def cols_slab(n):
        return lambda ref, idx: ref.at[:, :, pl.ds(pl.multiple_of(idx * n, n), n)]

    def lead_slab(ref, idx):
        return ref.at[idx]

    ids = iter(range(64))
    w_in_p, w_o_f, w_up_f, w_down_f = [], [], [], []
    for l in range(L):
        (w_in_sh,) = _all_gather("ag_in", next(ids), [w_in[l:l + 1].astype(BF16)],
                                 [jax.ShapeDtypeStruct((N_DEV, 1, D, INS), BF16)], [lead_slab])
        if l == 0:
            (conv_w_f,) = _all_gather("ag_conv_w", next(ids), [conv_w], [jax.ShapeDtypeStruct((L, 3, F2), F32)],
                                      [cols_slab(US)])
        w_in_f = jnp.transpose(w_in_sh, (1, 2, 0, 3)).reshape(1, D, IN)
        w_in_p.append(jnp.concatenate([w_in_f[:, :, :AW], w_in_f[:, :, AW + 2 * KVW:], w_in_f[:, :, AW:AW + 2 * KVW]],
                                      axis=2))
        w_o_f += _all_gather("ag_o", next(ids), [w_o[l:l + 1].astype(BF16)],
                             [jax.ShapeDtypeStruct((1, D, D), BF16)], [rows_slab(OS)])
        w_up_f += _all_gather("ag_up", next(ids), [w_up[l:l + 1].astype(BF16)],
                              [jax.ShapeDtypeStruct((1, D, F2), BF16)], [cols_slab(US)])
        w_down_f += _all_gather("ag_down", next(ids), [w_down[l:l + 1].astype(BF16)],
                                [jax.ShapeDtypeStruct((1, F, D), BF16)], [rows_slab(DS)])

    n1g3, n2g3 = norm1_g.reshape(L, 1, D), norm2_g.reshape(L, 1, D)
    qg3, kg3 = q_norm_g.reshape(L, 1, HEAD), k_norm_g.reshape(L, 1, HEAD)
    lng3, lnb3 = sgu_ln_g.reshape(L, 1, GW), sgu_ln_b.reshape(L, 1, GW)
    ag3, og3 = attn_out_g.reshape(L, 1, AW), sgu_out_g.reshape(L, 1, GW)
    cb3 = conv_b.reshape(L, 1, F2)
    ws_b = w_s.astype(BF16)
    wst_b = jnp.swapaxes(w_s, 2, 3).astype(BF16)
    bs_b = jnp.broadcast_to(b_s[..., None], (L, NG, HEAD, HEAD))
    inv_freq = ROPE_THETA ** (-jnp.arange(0, HEAD, 2, dtype=F32) / HEAD)
    ang = jnp.arange(S, dtype=F32)[:, None] * inv_freq[None, :]
    cos2 = jnp.concatenate([jnp.cos(ang), jnp.cos(ang)], axis=1)
    sin2 = jnp.concatenate([-jnp.sin(ang), jnp.sin(ang)], axis=1)

    tn = 512
    t_in, t_d, t_f, t_f2 = _tile(IN, tn, 128), _tile(D, tn, 128), _tile(F, tn, 128), _tile(F2, tn, 128)
    tk_f = _tile(F, 1408, 128)
    tk_in = _tile(IN, 1792, 128)
    tm_f = _tile(F, 512, 128)

    def w_spec(tk, tn_):
        return pl.BlockSpec((None, tk, tn_), lambda i, j, k: (0, k, j))

    def wt_spec(tn_, tk):
        return pl.BlockSpec((None, tn_, tk), lambda i, j, k: (0, j, k))

    def a_spec(tm, tk):
        return pl.BlockSpec((tm, tk), lambda i, j, k: (i, k))

    def at_spec(tk, tm):
        return pl.BlockSpec((tk, tm), lambda i, j, k: (k, i))

    def b_spec(tk, tn_):
        return pl.BlockSpec((tk, tn_), lambda i, j, k: (k, j))

    xs = x.reshape(S, D)
    saved = []
    cur = xs
    for l in range(L):
        h = _rms_fwd("rms1_fwd", cur, n1g3, l)
        z = _mm("mm_in", h, w_in_p[l], M=S, N=IN, K=D, tm=S, tn=t_in, tk=D,
                a_spec=a_spec(S, D), b_spec=w_spec(D, t_in))
        q_r, k_r, v_b = _qkv_prep("qkv_prep", z, qg3, kg3, cos2, sin2, l, cfg)
        attn, mix_l = _attn_fwd("attn_fwd", q_r, k_r, v_b, sink, ag3, l, cfg)
        mix_r = _sgu_fwd("sgu_fwd", z, lng3, lnb3, ws_b, bs_b, og3, l, cfg)
        mixed = jnp.concatenate([mix_l, mix_r], axis=1)
        x1 = _mm("mm_o", mixed, w_o_f[l], M=S, N=D, K=D, tm=S, tn=t_d, tk=D,
                 a_spec=a_spec(S, D), b_spec=w_spec(D, t_d), res=cur)
        h2 = _rms_fwd("rms2_fwd", x1, n2g3, l)
        ap = _mm("mm_up", h2, w_up_f[l], M=S, N=F2, K=D, tm=S, tn=t_f2, tk=D,
                 a_spec=a_spec(S, D), b_spec=w_spec(D, t_f2))
        y_b = _conv_glu_fwd("conv_glu_fwd", ap, conv_w_f, cb3, l, cfg)
        x2 = _mm("mm_down", y_b, w_down_f[l], M=S, N=D, K=F, tm=S, tn=t_d, tk=tk_f,
                 a_spec=a_spec(S, tk_f), b_spec=w_spec(tk_f, t_d), res=x1)
        saved.append(dict(x=cur, h=h, z=z, q=q_r, k=k_r, v=v_b, attn=attn, mixed=mixed, x1=x1, h2=h2, ap=ap, y=y_b))
        cur = x2

    loss_tile, dx, dxb = _loss_bwd("loss", cur, loss_target.reshape(S, D))
    loss = lax.psum(loss_tile[0, 0], ("x", "y", "c"))

    gS = [dict() for _ in range(L)]
    recv = [dict() for _ in range(L)]
    cvec = jnp.reshape(lax.axis_index("c"), (1,)).astype(jnp.int32)

    def reduce_scatter(l, name, g, R, C, small=None):
        g4 = g.reshape(N_DEV // 2, 2, R, C)
        got = _rs_d2d("rs_d2d_" + name, next(ids), [g4], [] if small is None else [small])
        chip = [_pair_sum("pair_sum_" + name, g4, got[0], cvec, 16)]
        chip_small = [] if small is None else [_add2("pair_sum_small", small, got[1])]
        out = _rs_ici("rs_ici_" + name, next(ids), chip, chip_small)
        recv[l][name] = out[0]
        if small is not None:
            recv[l]["small"] = out[1]

    nkf = F // tk_f
    for l in reversed(range(L)):
        sv = saved[l]
        dy = _mm("mm_dy", dxb, w_down_f[l], M=S, N=F, K=D, tm=S, tn=t_f, tk=D, tb=True,
                 a_spec=a_spec(S, D), b_spec=wt_spec(t_f, D))
        g_down = _mm("mm_gdown", sv["y"], dxb, M=F, N=D, K=S, tm=tm_f, tn=D, tk=S, ta=True,
                     a_spec=at_spec(S, tm_f), b_spec=b_spec(S, D), out_dtype=BF16)
        reduce_scatter(l, "w_down", g_down, DS, D)
        dap3, dcw, dcb = _glu_conv_bwd("glu_conv_bwd", dy, sv["ap"], conv_w_f, cb3, l, cfg)
        gS[l]["conv_w"] = jnp.concatenate([dcw[0], dcw[1]], axis=1)
        gS[l]["conv_b"] = jnp.concatenate([dcb[0], dcb[1]], axis=1).reshape(F2)
        dh2 = _mm("mm_dh2", dap3, w_up_f[l], M=S, N=D, K=F2, tm=S, tn=t_d, tk=tk_f, tb=True,
                  a_spec=pl.BlockSpec((None, S, tk_f), lambda i, j, k: (k // nkf, 0, k % nkf)),
                  b_spec=wt_spec(t_d, tk_f))
        g_up = _mm("mm_gup", sv["h2"], dap3, M=D, N=F2, K=S, tm=D, tn=US, tk=S, ta=True,
                   a_spec=at_spec(S, D),
                   b_spec=pl.BlockSpec((None, S, US), lambda i, j, k: (j // (N_DEV // 2), 0, j % (N_DEV // 2))),
                   out_dtype=BF16, out_shape=(N_DEV, D, US),
                   out_spec=pl.BlockSpec((None, D, US), lambda i, j, k: (j, 0, 0)))
        reduce_scatter(l, "w_up", g_up, D, US)
        dx1, dx1b, dg2 = _rms_bwd("rms2_bwd", sv["x1"], dh2, dx, n2g3, l)
        gS[l]["norm2_g"] = dg2.reshape(D)
        dmix = _mm("mm_dmix", dx1b, w_o_f[l], M=S, N=D, K=D, tm=S, tn=t_d, tk=D, tb=True,
                   a_spec=a_spec(S, D), b_spec=wt_spec(t_d, D))
        g_o = _mm("mm_go", sv["mixed"], dx1b, M=D, N=D, K=S, tm=D, tn=t_d, tk=S, ta=True,
                  a_spec=at_spec(S, D), b_spec=b_spec(S, t_d), out_dtype=BF16)
        reduce_scatter(l, "w_o", g_o, OS, D)
        dq_r, dk_pad, dv_pad, dsink, dag = _attn_bwd("attn_bwd", sv["q"], sv["k"], sv["v"], sv["attn"], dmix,
                                                     sink, ag3, l, cfg)
        gS[l]["sink"] = dsink[:, 0]
        gS[l]["attn_out_g"] = dag.reshape(AW)
        dzgu, dzgv, dws, dbs, dlng, dlnb, dog = _sgu_bwd("sgu_bwd", sv["z"], dmix, lng3, lnb3, ws_b, wst_b, bs_b,
                                                        og3, l, cfg)
        gS[l]["w_s"] = dws
        gS[l]["b_s"] = dbs[:, :, 0]
        gS[l]["sgu_ln_g"] = dlng.reshape(GW)
        gS[l]["sgu_ln_b"] = dlnb.reshape(GW)
        gS[l]["sgu_out_g"] = dog.reshape(GW)
        dzq, dzkv, dqg, dkg = _qkv_prep_bwd("qkv_prep_bwd", sv["z"], dq_r, dk_pad, dv_pad, qg3, kg3, cos2, sin2,
                                            l, cfg)
        gS[l]["q_norm_g"] = dqg.reshape(HEAD)
        gS[l]["k_norm_g"] = dkg.reshape(HEAD)
        dz = jnp.concatenate([dzq, dzgu, dzgv, dzkv], axis=1)
        dh = _mm("mm_dh", dz, w_in_p[l], M=S, N=D, K=IN, tm=S, tn=t_d, tk=tk_in, tb=True,
                 a_spec=a_spec(S, tk_in), b_spec=wt_spec(t_d, tk_in))
        gin_p = _mm("mm_gin", sv["h"], dz, M=D, N=IN, K=S, tm=D, tn=t_in, tk=S, ta=True,
                    a_spec=at_spec(S, D), b_spec=b_spec(S, t_in), out_dtype=BF16)
        gin = jnp.concatenate([gin_p[:, :AW], gin_p[:, AW + 2 * GW:], gin_p[:, AW:AW + 2 * GW]], axis=1)
        g_in = jnp.transpose(gin.reshape(D, N_DEV, INS), (1, 0, 2))
        dx, dxb, dg1 = _rms_bwd("rms1_bwd", sv["x"], dh, dx1, n1g3, l)
        gS[l]["norm1_g"] = dg1.reshape(D)
        reduce_scatter(l, "w_in", g_in, D, INS, small=_pack([gS[l][n] for n in SMALL] + [gS[l]["conv_w"]]))
    grad_x = dx.reshape(1, S, D)

    grads, deltas, new_m, new_v = {}, {}, {}, {}
    for name in ["w_down", "w_up", "w_o", "w_in"]:
        res = None
        for l in reversed(range(L)):
            res = _adamw_sum("adamw_" + name, recv[l][name], weights[name], mom_m[name], mom_v[name], l, res, 16)
        grads[name], deltas[name], new_m[name], new_v[name] = res

    small_shapes = [weights[n].shape[1:] for n in SMALL] + [(3, F2)]
    SR = recv[0]["small"].shape[1]
    tr_s = _tile(SR, 512, 8)
    sblk = pl.BlockSpec((tr_s, 128), lambda i: (i, 0))
    sshp = jax.ShapeDtypeStruct((SR, 128), F32)
    small_parts = [[None] * L for _ in range(4)]
    for l in reversed(range(L)):
        def small_body(r_ref, w_ref, m_ref, v_ref, g_ref, d_ref, nm_ref, nv_ref):
            g = r_ref[0]
            for s in range(1, N_DEV // 2):
                g = g + r_ref[s]
            d, m2, v2 = _adamw_math(w_ref[...], g, m_ref[...], v_ref[...])
            g_ref[...] = g
            d_ref[...] = d
            nm_ref[...] = m2
            nv_ref[...] = v2

        def pack_small(src):
            return _pack([src[n][l] for n in SMALL] + [jnp.zeros((3, F2), F32)])

        packed = pl.pallas_call(
            small_body, name="adamw_small", grid=(SR // tr_s,),
            in_specs=[pl.BlockSpec((N_DEV // 2, tr_s, 128), lambda i: (0, i, 0)), sblk, sblk, sblk],
            out_specs=[sblk, sblk, sblk, sblk], out_shape=[sshp, sshp, sshp, sshp],
            compiler_params=_params(("arbitrary",)),
        )(recv[l]["small"], pack_small(weights), pack_small(mom_m), pack_small(mom_v))
        for t in range(4):
            small_parts[t][l] = _unpack(packed[t], small_shapes)
    for t, store in enumerate([grads, deltas, new_m, new_v]):
        for i, n in enumerate(SMALL):
            store[n] = jnp.stack([small_parts[t][l][i] for l in range(L)])
    conv_full = jnp.stack([small_parts[0][l][len(SMALL)] for l in range(L)])
    me_i = _dev_index(_mesh_pos())
    g_cw = lax.dynamic_slice_in_dim(conv_full, me_i * US, US, axis=2)
    grads["conv_w"] = g_cw
    d_cw, m_cw, v_cw = _adamw_plain("adamw_conv_w", g_cw.reshape(L * 3, US), conv_w.reshape(L * 3, US),
                                    m_conv_w.reshape(L * 3, US), v_conv_w.reshape(L * 3, US))
    deltas["conv_w"] = d_cw.reshape(L, 3, US)
    new_m["conv_w"] = m_cw.reshape(L, 3, US)
    new_v["conv_w"] = v_cw.reshape(L, 3, US)

    return (loss, grad_x, *[grads[n] for n in order], *[deltas[n] for n in order],
            *[new_m[n] for n in order], *[new_v[n] for n in order])
```

```python
import jax
import jax.numpy as jnp
from jax import lax
from jax.experimental import pallas as pl
from jax.experimental.pallas import tpu as pltpu
from jax.experimental.pallas import tpu_sc as plsc

F32 = jnp.float32
BF16 = jnp.bfloat16
MESH = pl.DeviceIdType.MESH

N_DEV = 8
HEAD = 128
EPS = 1e-6
MASK_VALUE = -1e30
ROPE_THETA = 10000.0
GELU_C = 0.7978845608028654
GELU_A = 0.044715

ADAM_LR = 0.001
ADAM_B1 = 0.9
ADAM_B2 = 0.999
ADAM_EPS = 1e-08
ADAM_WD = 0.01
ADAM_STEP = 10

VMEM_LIMIT = 56 * 1024 * 1024


def _tile(n, pref, mult):
    best = None
    for t in range(mult, min(n, pref) + 1, mult):
        if n % t == 0:
            best = t
    return n if best is None else best


def _params(sem=None):
    kw = dict(vmem_limit_bytes=VMEM_LIMIT)
    if sem is not None:
        kw["dimension_semantics"] = sem
    return pltpu.CompilerParams(**kw)


def _gelu(x):
    return x * (0.5 * (1.0 + jnp.tanh(GELU_C * (x + GELU_A * (x * x * x)))))


def _gelu_grad(x):
    t = jnp.tanh(GELU_C * (x + GELU_A * (x * x * x)))
    return 0.5 * (1.0 + t) + 0.5 * x * (1.0 - t * t) * (GELU_C * (1.0 + 3.0 * GELU_A * (x * x)))


def _sigmoid(x):
    return 1.0 / (1.0 + jnp.exp(-x))


def _vec_spec(l, n):
    return pl.BlockSpec((None, 1, n), lambda *_: (l, 0, 0))


def _mm(name, a, b, *, M, N, K, tm, tn, tk, a_spec, b_spec, ta=False, tb=False, out_dtype=F32, res=None,
        out_shape=None, out_spec=None):
    nm, nn, nk = M // tm, N // tn, K // tk
    assert nm * tm == M and nn * tn == N and nk * tk == K
    assert not (ta and nk > 1)
    dims = (((1,), (1,)), ((), ())) if tb else (((1,), (0,)), ((), ()))

    def body(*refs):
        refs = list(refs)
        a_ref = refs.pop(0)
        b_ref = refs.pop(0)
        r_ref = refs.pop(0) if res is not None else None
        o_ref = refs.pop(0)
        acc = refs.pop(0) if nk > 1 else None
        at = refs.pop(0) if ta else None
        k = pl.program_id(2)
        if ta:
            @pl.when(pl.program_id(1) == 0)
            def _():
                at[...] = a_ref[...].T
            lhs = at[...]
        else:
            lhs = a_ref[...]
        p = lax.dot_general(lhs, b_ref[...], dims, preferred_element_type=F32)

        def finish(r):
            if r_ref is not None:
                r = r_ref[...] + r
            o_ref[...] = r.astype(out_dtype)

        if nk == 1:
            finish(p)
        else:
            @pl.when(k == 0)
            def _():
                acc[...] = p

            @pl.when(k > 0)
            def _():
                acc[...] += p

            @pl.when(k == nk - 1)
            def _():
                finish(acc[...])

    in_specs = [a_spec, b_spec]
    args = [a, b]
    if res is not None:
        in_specs.append(pl.BlockSpec((tm, tn), lambda i, j, k: (i, j)))
        args.append(res)
    scratch = []
    if nk > 1:
        scratch.append(pltpu.VMEM((tm, tn), F32))
    if ta:
        scratch.append(pltpu.VMEM((tm, tk), BF16))
    return pl.pallas_call(
        body, name=name, grid=(nm, nn, nk),
        in_specs=in_specs,
        out_specs=pl.BlockSpec((tm, tn), lambda i, j, k: (i, j)) if out_spec is None else out_spec,
        out_shape=jax.ShapeDtypeStruct((M, N) if out_shape is None else out_shape, out_dtype),
        scratch_shapes=scratch,
        compiler_params=_params(("arbitrary", "arbitrary", "arbitrary")),
    )(*args)


def _rms_fwd(name, x, g3, l):
    S, D = x.shape
    tr = _tile(S, 256, 16)

    def body(x_ref, g_ref, h_ref):
        xv = x_ref[...]
        r = lax.rsqrt(jnp.mean(xv * xv, axis=-1, keepdims=True) + EPS)
        h_ref[...] = ((xv * r) * g_ref[...]).astype(BF16)

    return pl.pallas_call(
        body, name=name, grid=(S // tr,),
        in_specs=[pl.BlockSpec((tr, D), lambda i: (i, 0)), _vec_spec(l, D)],
        out_specs=pl.BlockSpec((tr, D), lambda i: (i, 0)),
        out_shape=jax.ShapeDtypeStruct((S, D), BF16),
        compiler_params=_params(("arbitrary",)),
    )(x, g3)


def _rope(t, cos2, sin2):
    return t * cos2 + pltpu.roll(t, HEAD // 2, axis=1) * sin2


def _qkv_prep(name, z, qg3, kg3, cos2, sin2, l, cfg):
    S, AW, KVW, NQ, NKV = cfg["S"], cfg["AW"], cfg["KVW"], cfg["NQ"], cfg["NKV"]
    tr = _tile(S, 256, 16)
    kv_blk = (3 * AW) // (2 * KVW)

    def body(zq_ref, zkv_ref, qg_ref, kg_ref, c_ref, s_ref, q_ref, k_ref, v_ref):
        cosv, sinv = c_ref[...], s_ref[...]

        def norm_rope(t, g):
            r = lax.rsqrt(jnp.mean(t * t, axis=-1, keepdims=True) + EPS)
            return _rope((t * r) * g, cosv, sinv)

        for h in range(NQ):
            sl = slice(h * HEAD, (h + 1) * HEAD)
            q_ref[:, sl] = norm_rope(zq_ref[:, sl], qg_ref[...]).astype(BF16)
        for h in range(NKV):
            sl = slice(h * HEAD, (h + 1) * HEAD)
            k_ref[:, sl] = norm_rope(zkv_ref[:, sl], kg_ref[...]).astype(BF16)
        v_ref[...] = zkv_ref[:, KVW:].astype(BF16)

    return pl.pallas_call(
        body, name=name, grid=(S // tr,),
        in_specs=[pl.BlockSpec((tr, AW), lambda i: (i, 0)),
                  pl.BlockSpec((tr, 2 * KVW), lambda i: (i, kv_blk)),
                  _vec_spec(l, HEAD), _vec_spec(l, HEAD),
                  pl.BlockSpec((tr, HEAD), lambda i: (i, 0)),
                  pl.BlockSpec((tr, HEAD), lambda i: (i, 0))],
        out_specs=[pl.BlockSpec((tr, AW), lambda i: (i, 0)),
                   pl.BlockSpec((tr, KVW), lambda i: (i, 0)),
                   pl.BlockSpec((tr, KVW), lambda i: (i, 0))],
        out_shape=[jax.ShapeDtypeStruct((S, AW), BF16),
                   jax.ShapeDtypeStruct((S, KVW), BF16),
                   jax.ShapeDtypeStruct((S, KVW), BF16)],
        compiler_params=_params(("arbitrary",)),
    )(z, z, qg3, kg3, cos2, sin2)


def _band_specs(width, nb):
    return [pl.BlockSpec((HEAD, width), lambda n: (jnp.maximum(n - 1, 0), 0)),
            pl.BlockSpec((HEAD, width), lambda n: (n, 0)),
            pl.BlockSpec((HEAD, width), lambda n: (jnp.minimum(n + 1, nb - 1), 0))]


def _attn_probs(qs, kj, n, sink_of_row, S, G):
    s = lax.dot_general(qs, kj, (((1,), (1,)), ((), ())), preferred_element_type=F32) * (HEAD ** -0.5)
    rows = lax.broadcasted_iota(jnp.int32, (G * HEAD, 3 * HEAD), 0)
    cols = lax.broadcasted_iota(jnp.int32, (G * HEAD, 3 * HEAD), 1)
    qi = rows & (HEAD - 1)
    kpos = n * HEAD - HEAD + cols
    valid = (cols >= qi) & (cols <= qi + 2 * HEAD) & (kpos >= 0) & (kpos < S)
    s = jnp.where(valid, s, MASK_VALUE)
    m = jnp.maximum(jnp.max(s, axis=-1, keepdims=True), sink_of_row)
    p = jnp.exp(s - m)
    e_sink = jnp.exp(sink_of_row - m)
    inv = 1.0 / (jnp.sum(p, axis=-1, keepdims=True) + e_sink)
    return p * inv, e_sink * inv


def _sink_rows(sink_ref, l, j, G):
    hidx = lax.broadcasted_iota(jnp.int32, (G * HEAD, 1), 0) // HEAD
    col = jnp.full((G * HEAD, 1), sink_ref[l, j * G], F32)
    for g in range(1, G):
        col = jnp.where(hidx == g, sink_ref[l, j * G + g], col)
    return col


def _attn_fwd(name, q, k, v, sink, ag3, l, cfg):
    S, AW, KVW, NKV, G = cfg["S"], cfg["AW"], cfg["KVW"], cfg["NKV"], cfg["G"]
    nb = S // HEAD

    def body(q_ref, kp, kc, kn, vp, vc, vn, sink_ref, ag_ref, a_ref, mix_ref):
        n = pl.program_id(0)
        kb = jnp.concatenate([kp[...], kc[...], kn[...]], axis=0)
        vb = jnp.concatenate([vp[...], vc[...], vn[...]], axis=0)
        for j in range(NKV):
            sl = slice(j * HEAD, (j + 1) * HEAD)
            qs = jnp.concatenate([q_ref[:, (j * G + g) * HEAD:(j * G + g + 1) * HEAD] for g in range(G)], axis=0)
            probs, _ = _attn_probs(qs, kb[:, sl], n, _sink_rows(sink_ref, l, j, G), S, G)
            o = jnp.dot(probs.astype(BF16), vb[:, sl], preferred_element_type=F32)
            for g in range(G):
                a_ref[:, (j * G + g) * HEAD:(j * G + g + 1) * HEAD] = o[g * HEAD:(g + 1) * HEAD]
        a = a_ref[...]
        r = lax.rsqrt(jnp.mean(a * a, axis=-1, keepdims=True) + EPS)
        mix_ref[...] = ((a * r) * ag_ref[...]).astype(BF16)

    return pl.pallas_call(
        body, name=name, grid=(nb,),
        in_specs=[pl.BlockSpec((HEAD, AW), lambda n: (n, 0))] + _band_specs(KVW, nb) + _band_specs(KVW, nb)
                 + [pl.BlockSpec(memory_space=pltpu.SMEM), _vec_spec(l, AW)],
        out_specs=[pl.BlockSpec((HEAD, AW), lambda n: (n, 0)), pl.BlockSpec((HEAD, AW), lambda n: (n, 0))],
        out_shape=[jax.ShapeDtypeStruct((S, AW), F32), jax.ShapeDtypeStruct((S, AW), BF16)],
        compiler_params=_params(("arbitrary",)),
    )(q, k, k, k, v, v, v, sink, ag3)


def _sgu_forward_math(gu, gv, lng, lnb):
    u = _gelu(gu)
    vv = _gelu(gv)
    mu = jnp.mean(vv, axis=-1, keepdims=True)
    xc = vv - mu
    rstd = lax.rsqrt(jnp.mean(xc * xc, axis=-1, keepdims=True) + EPS)
    xhat = xc * rstd
    vn = xhat * lng + lnb
    return u, xhat, rstd, vn


def _sgu_fwd(name, z, lng3, lnb3, ws_b, bs_b, og3, l, cfg):
    S, GW, NG = cfg["S"], cfg["GW"], cfg["NG"]

    def body(gu_ref, gv_ref, lng_ref, lnb_ref, ws_ref, bs_ref, og_ref, mix_ref, sg_ref):
        u, _, _, vn = _sgu_forward_math(gu_ref[...], gv_ref[...], lng_ref[...], lnb_ref[...])
        vnb = vn.astype(BF16)
        for h in range(NG):
            sl = slice(h * HEAD, (h + 1) * HEAD)
            f = jnp.dot(ws_ref[h], vnb[:, sl], preferred_element_type=F32) + bs_ref[h]
            sg_ref[:, sl] = u[:, sl] * f
        sg = sg_ref[...]
        r = lax.rsqrt(jnp.mean(sg * sg, axis=-1, keepdims=True) + EPS)
        mix_ref[...] = ((sg * r) * og_ref[...]).astype(BF16)

    return pl.pallas_call(
        body, name=name, grid=(S // HEAD,),
        in_specs=[pl.BlockSpec((HEAD, GW), lambda c: (c, 1)),
                  pl.BlockSpec((HEAD, GW), lambda c: (c, 2)),
                  _vec_spec(l, GW), _vec_spec(l, GW),
                  pl.BlockSpec((None, NG, HEAD, HEAD), lambda c: (l, 0, 0, 0)),
                  pl.BlockSpec((None, NG, HEAD, HEAD), lambda c: (l, 0, 0, 0)),
                  _vec_spec(l, GW)],
        out_specs=pl.BlockSpec((HEAD, GW), lambda c: (c, 0)),
        out_shape=jax.ShapeDtypeStruct((S, GW), BF16),
        scratch_shapes=[pltpu.VMEM((HEAD, GW), F32)],
        compiler_params=_params(("arbitrary",)),
    )(z, z, lng3, lnb3, ws_b, bs_b, og3)


def _conv3(t, w, b, S):
    row = lax.broadcasted_iota(jnp.int32, t.shape, 0)
    dn = jnp.where(row == 0, 0.0, pltpu.roll(t, 1, axis=0))
    up = jnp.where(row == S - 1, 0.0, pltpu.roll(t, S - 1, axis=0))
    return ((b + dn * w[0:1]) + t * w[1:2]) + up * w[2:3], dn, up


def _conv_glu_fwd(name, ap, cw, cb3, l, cfg):
    S, F = cfg["S"], cfg["F"]
    tc = _tile(F, 256, 128)
    nf = F // tc

    def body(g_ref, u_ref, wg_ref, wu_ref, bg_ref, bu_ref, y_ref):
        ag, _, _ = _conv3(g_ref[...], wg_ref[...], bg_ref[...], S)
        au, _, _ = _conv3(u_ref[...], wu_ref[...], bu_ref[...], S)
        y_ref[...] = ((ag * _sigmoid(ag)) * au).astype(BF16)

    return pl.pallas_call(
        body, name=name, grid=(nf,),
        in_specs=[pl.BlockSpec((S, tc), lambda j: (0, j)),
                  pl.BlockSpec((S, tc), lambda j: (0, j + nf)),
                  pl.BlockSpec((None, 3, tc), lambda j: (l, 0, j)),
                  pl.BlockSpec((None, 3, tc), lambda j: (l, 0, j + nf)),
                  pl.BlockSpec((None, 1, tc), lambda j: (l, 0, j)),
                  pl.BlockSpec((None, 1, tc), lambda j: (l, 0, j + nf))],
        out_specs=pl.BlockSpec((S, tc), lambda j: (0, j)),
        out_shape=jax.ShapeDtypeStruct((S, F), BF16),
        compiler_params=_params(("arbitrary",)),
    )(ap, ap, cw, cw, cb3, cb3)


def _loss_bwd(name, y, target):
    S, D = y.shape
    tr = _tile(S, 256, 16)

    def body(y_ref, t_ref, loss_ref, d_ref, db_ref):
        @pl.when(pl.program_id(0) == 0)
        def _():
            loss_ref[...] = jnp.zeros_like(loss_ref)

        err = y_ref[...] - t_ref[...]
        part = 0.5 * jnp.sum(jnp.mean(err * err, axis=-1, keepdims=True), axis=0, keepdims=True)
        loss_ref[...] += jnp.broadcast_to(part, loss_ref.shape)
        d = err * (1.0 / D)
        d_ref[...] = d
        db_ref[...] = d.astype(BF16)

    return pl.pallas_call(
        body, name=name, grid=(S // tr,),
        in_specs=[pl.BlockSpec((tr, D), lambda i: (i, 0)), pl.BlockSpec((tr, D), lambda i: (i, 0))],
        out_specs=[pl.BlockSpec((8, 128), lambda i: (0, 0)),
                   pl.BlockSpec((tr, D), lambda i: (i, 0)),
                   pl.BlockSpec((tr, D), lambda i: (i, 0))],
        out_shape=[jax.ShapeDtypeStruct((8, 128), F32),
                   jax.ShapeDtypeStruct((S, D), F32),
                   jax.ShapeDtypeStruct((S, D), BF16)],
        compiler_params=_params(("arbitrary",)),
    )(y, target)


def _rms_bwd(name, x, dh, dres, g3, l):
    S, D = x.shape
    tr = _tile(S, 256, 16)

    def body(x_ref, dh_ref, dr_ref, g_ref, dx_ref, dxb_ref, dg_ref):
        @pl.when(pl.program_id(0) == 0)
        def _():
            dg_ref[...] = jnp.zeros_like(dg_ref)

        xv = x_ref[...]
        dhv = dh_ref[...]
        r = lax.rsqrt(jnp.mean(xv * xv, axis=-1, keepdims=True) + EPS)
        xhat = xv * r
        dhg = dhv * g_ref[...]
        dx = dr_ref[...] + r * (dhg - xhat * jnp.mean(dhg * xhat, axis=-1, keepdims=True))
        dx_ref[...] = dx
        dxb_ref[...] = dx.astype(BF16)
        dg_ref[...] += jnp.sum(dhv * xhat, axis=0, keepdims=True)

    return pl.pallas_call(
        body, name=name, grid=(S // tr,),
        in_specs=[pl.BlockSpec((tr, D), lambda i: (i, 0)), pl.BlockSpec((tr, D), lambda i: (i, 0)),
                  pl.BlockSpec((tr, D), lambda i: (i, 0)), _vec_spec(l, D)],
        out_specs=[pl.BlockSpec((tr, D), lambda i: (i, 0)), pl.BlockSpec((tr, D), lambda i: (i, 0)),
                   pl.BlockSpec((1, D), lambda i: (0, 0))],
        out_shape=[jax.ShapeDtypeStruct((S, D), F32), jax.ShapeDtypeStruct((S, D), BF16),
                   jax.ShapeDtypeStruct((1, D), F32)],
        compiler_params=_params(("arbitrary",)),
    )(x, dh, dres, g3)


def _glu_conv_bwd(name, dy, ap, cw, cb3, l, cfg):
    S, F = cfg["S"], cfg["F"]
    tc = _tile(F, 256, 128)
    nf = F // tc

    def body(dy_ref, g_ref, u_ref, wg_ref, wu_ref, bg_ref, bu_ref, dap_ref, dw_ref, db_ref):
        apg, apu = g_ref[...], u_ref[...]
        wg, wu = wg_ref[...], wu_ref[...]
        ag, g_dn, g_up = _conv3(apg, wg, bg_ref[...], S)
        au, u_dn, u_up = _conv3(apu, wu, bu_ref[...], S)
        sig = _sigmoid(ag)
        dyv = dy_ref[...]
        da_u = dyv * (ag * sig)
        da_g = (dyv * au) * (sig * (1.0 + ag * (1.0 - sig)))
        row = lax.broadcasted_iota(jnp.int32, (S, tc), 0)

        def back(da, w):
            nxt = jnp.where(row == S - 1, 0.0, pltpu.roll(da, S - 1, axis=0))
            prv = jnp.where(row == 0, 0.0, pltpu.roll(da, 1, axis=0))
            return (nxt * w[0:1] + da * w[1:2]) + prv * w[2:3]

        dap_ref[0] = back(da_g, wg).astype(BF16)
        dap_ref[1] = back(da_u, wu).astype(BF16)

        def wgrad(da, dn, t, up):
            return jnp.concatenate([jnp.sum(da * dn, axis=0, keepdims=True),
                                    jnp.sum(da * t, axis=0, keepdims=True),
                                    jnp.sum(da * up, axis=0, keepdims=True)], axis=0)

        dw_ref[0] = wgrad(da_g, g_dn, apg, g_up)
        dw_ref[1] = wgrad(da_u, u_dn, apu, u_up)
        db_ref[0] = jnp.sum(da_g, axis=0, keepdims=True)
        db_ref[1] = jnp.sum(da_u, axis=0, keepdims=True)

    return pl.pallas_call(
        body, name=name, grid=(nf,),
        in_specs=[pl.BlockSpec((S, tc), lambda j: (0, j)),
                  pl.BlockSpec((S, tc), lambda j: (0, j)),
                  pl.BlockSpec((S, tc), lambda j: (0, j + nf)),
                  pl.BlockSpec((None, 3, tc), lambda j: (l, 0, j)),
                  pl.BlockSpec((None, 3, tc), lambda j: (l, 0, j + nf)),
                  pl.BlockSpec((None, 1, tc), lambda j: (l, 0, j)),
                  pl.BlockSpec((None, 1, tc), lambda j: (l, 0, j + nf))],
        out_specs=[pl.BlockSpec((2, S, tc), lambda j: (0, 0, j)),
                   pl.BlockSpec((2, 3, tc), lambda j: (0, 0, j)),
                   pl.BlockSpec((2, 1, tc), lambda j: (0, 0, j))],
        out_shape=[jax.ShapeDtypeStruct((2, S, F), BF16),
                   jax.ShapeDtypeStruct((2, 3, F), F32),
                   jax.ShapeDtypeStruct((2, 1, F), F32)],
        compiler_params=_params(("arbitrary",)),
    )(dy, ap, ap, cw, cw, cb3, cb3)


def _attn_bwd(name, q, k, v, attn, dmix, sink, ag3, l, cfg):
    S, AW, KVW, NQ, NKV, G = cfg["S"], cfg["AW"], cfg["KVW"], cfg["NQ"], cfg["NKV"], cfg["G"]
    nb = S // HEAD
    scale = HEAD ** -0.5

    def body(q_ref, kp, kc, kn, vp, vc, vn, a_ref, dm_ref, sink_ref, ag_ref,
             dq_ref, dk_ref, dv_ref, dsink_ref, dag_ref, da_scr):
        n = pl.program_id(0)

        @pl.when(n == 0)
        def _():
            dk_ref[...] = jnp.zeros_like(dk_ref)
            dv_ref[...] = jnp.zeros_like(dv_ref)
            dsink_ref[...] = jnp.zeros_like(dsink_ref)
            dag_ref[...] = jnp.zeros_like(dag_ref)

        a = a_ref[...]
        dm = dm_ref[...]
        r = lax.rsqrt(jnp.mean(a * a, axis=-1, keepdims=True) + EPS)
        xhat = a * r
        dmg = dm * ag_ref[...]
        da_scr[...] = r * (dmg - xhat * jnp.mean(dmg * xhat, axis=-1, keepdims=True))
        dag_ref[...] += jnp.sum(dm * xhat, axis=0, keepdims=True)

        kb = jnp.concatenate([kp[...], kc[...], kn[...]], axis=0)
        vb = jnp.concatenate([vp[...], vc[...], vn[...]], axis=0)
        band = pl.ds(pl.multiple_of(n * HEAD, HEAD), 3 * HEAD)
        for j in range(NKV):
            sl = slice(j * HEAD, (j + 1) * HEAD)
            heads = [slice((j * G + g) * HEAD, (j * G + g + 1) * HEAD) for g in range(G)]
            qs = jnp.concatenate([q_ref[:, hs] for hs in heads], axis=0)
            do = jnp.concatenate([da_scr[:, hs] for hs in heads], axis=0)
            kj, vj = kb[:, sl], vb[:, sl]
            probs, p_sink = _attn_probs(qs, kj, n, _sink_rows(sink_ref, l, j, G), S, G)
            dob = do.astype(BF16)
            dprobs = lax.dot_general(dob, vj, (((1,), (1,)), ((), ())), preferred_element_type=F32)
            delta = jnp.sum(dprobs * probs, axis=-1, keepdims=True)
            ds = (probs * (dprobs - delta)) * scale
            dsb = ds.astype(BF16)
            dsk = -(p_sink * delta)
            dq = jnp.dot(dsb, kj, preferred_element_type=F32)
            for g in range(G):
                dq_ref[:, heads[g]] = dq[g * HEAD:(g + 1) * HEAD]
                part = jnp.sum(dsk[g * HEAD:(g + 1) * HEAD], axis=0, keepdims=True)
                dsink_ref[j * G + g:j * G + g + 1, :] += jnp.broadcast_to(part, (1, HEAD))
            dk_ref[band, sl] += lax.dot_general(dsb, qs, (((0,), (0,)), ((), ())), preferred_element_type=F32)
            dv_ref[band, sl] += lax.dot_general(probs.astype(BF16), dob, (((0,), (0,)), ((), ())),
                                                preferred_element_type=F32)

    return pl.pallas_call(
        body, name=name, grid=(nb,),
        in_specs=[pl.BlockSpec((HEAD, AW), lambda n: (n, 0))] + _band_specs(KVW, nb) + _band_specs(KVW, nb)
                 + [pl.BlockSpec((HEAD, AW), lambda n: (n, 0)),
                    pl.BlockSpec((HEAD, AW), lambda n: (n, 0)),
                    pl.BlockSpec(memory_space=pltpu.SMEM), _vec_spec(l, AW)],
        out_specs=[pl.BlockSpec((HEAD, AW), lambda n: (n, 0)),
                   pl.BlockSpec((S + 2 * HEAD, KVW), lambda n: (0, 0)),
                   pl.BlockSpec((S + 2 * HEAD, KVW), lambda n: (0, 0)),
                   pl.BlockSpec((NQ, HEAD), lambda n: (0, 0)),
                   pl.BlockSpec((1, AW), lambda n: (0, 0))],
        out_shape=[jax.ShapeDtypeStruct((S, AW), F32),
                   jax.ShapeDtypeStruct((S + 2 * HEAD, KVW), F32),
                   jax.ShapeDtypeStruct((S + 2 * HEAD, KVW), F32),
                   jax.ShapeDtypeStruct((NQ, HEAD), F32),
                   jax.ShapeDtypeStruct((1, AW), F32)],
        scratch_shapes=[pltpu.VMEM((HEAD, AW), F32)],
        compiler_params=_params(("arbitrary",)),
    )(q, k, k, k, v, v, v, attn, dmix, sink, ag3)


def _qkv_prep_bwd(name, z, dq, dk_pad, dv_pad, qg3, kg3, cos2, sin2, l, cfg):
    S, AW, KVW, NQ, NKV = cfg["S"], cfg["AW"], cfg["KVW"], cfg["NQ"], cfg["NKV"]
    kv_blk = (3 * AW) // (2 * KVW)

    def body(zq_ref, zkv_ref, dq_ref, dk_ref, dv_ref, qg_ref, kg_ref, c_ref, s_ref,
             dzq_ref, dzkv_ref, dqg_ref, dkg_ref):
        @pl.when(pl.program_id(0) == 0)
        def _():
            dqg_ref[...] = jnp.zeros_like(dqg_ref)
            dkg_ref[...] = jnp.zeros_like(dkg_ref)

        cosv, sinv = c_ref[...], s_ref[...]

        def back(t, dr, g):
            r = lax.rsqrt(jnp.mean(t * t, axis=-1, keepdims=True) + EPS)
            xhat = t * r
            dn = dr * cosv + pltpu.roll(dr * sinv, HEAD // 2, axis=1)
            dxh = dn * g
            dt = r * (dxh - xhat * jnp.mean(dxh * xhat, axis=-1, keepdims=True))
            return dt, jnp.sum(dn * xhat, axis=0, keepdims=True)

        gq = jnp.zeros((1, HEAD), F32)
        for h in range(NQ):
            sl = slice(h * HEAD, (h + 1) * HEAD)
            dt, gpart = back(zq_ref[:, sl], dq_ref[:, sl], qg_ref[...])
            dzq_ref[:, sl] = dt.astype(BF16)
            gq = gq + gpart
        dqg_ref[...] += gq
        gk = jnp.zeros((1, HEAD), F32)
        for h in range(NKV):
            sl = slice(h * HEAD, (h + 1) * HEAD)
            dt, gpart = back(zkv_ref[:, sl], dk_ref[:, sl], kg_ref[...])
            dzkv_ref[:, sl] = dt.astype(BF16)
            gk = gk + gpart
        dkg_ref[...] += gk
        dzkv_ref[:, KVW:] = dv_ref[...].astype(BF16)

    return pl.pallas_call(
        body, name=name, grid=(S // HEAD,),
        in_specs=[pl.BlockSpec((HEAD, AW), lambda i: (i, 0)),
                  pl.BlockSpec((HEAD, 2 * KVW), lambda i: (i, kv_blk)),
                  pl.BlockSpec((HEAD, AW), lambda i: (i, 0)),
                  pl.BlockSpec((HEAD, KVW), lambda i: (i + 1, 0)),
                  pl.BlockSpec((HEAD, KVW), lambda i: (i + 1, 0)),
                  _vec_spec(l, HEAD), _vec_spec(l, HEAD),
                  pl.BlockSpec((HEAD, HEAD), lambda i: (i, 0)),
                  pl.BlockSpec((HEAD, HEAD), lambda i: (i, 0))],
        out_specs=[pl.BlockSpec((HEAD, AW), lambda i: (i, 0)),
                   pl.BlockSpec((HEAD, 2 * KVW), lambda i: (i, 0)),
                   pl.BlockSpec((1, HEAD), lambda i: (0, 0)),
                   pl.BlockSpec((1, HEAD), lambda i: (0, 0))],
        out_shape=[jax.ShapeDtypeStruct((S, AW), BF16),
                   jax.ShapeDtypeStruct((S, 2 * KVW), BF16),
                   jax.ShapeDtypeStruct((1, HEAD), F32),
                   jax.ShapeDtypeStruct((1, HEAD), F32)],
        compiler_params=_params(("arbitrary",)),
    )(z, z, dq, dk_pad, dv_pad, qg3, kg3, cos2, sin2)


def _sgu_bwd(name, z, dmix, lng3, lnb3, ws_b, wst_b, bs_b, og3, l, cfg):
    S, GW, NG = cfg["S"], cfg["GW"], cfg["NG"]

    def body(gu_ref, gv_ref, dm_ref, lng_ref, lnb_ref, ws_ref, wst_ref, bs_ref, og_ref,
             dgu_ref, dgv_ref, dws_ref, dbs_ref, dlng_ref, dlnb_ref, dog_ref, sg_scr, f_scr, dvn_scr):
        @pl.when(pl.program_id(0) == 0)
        def _():
            dws_ref[...] = jnp.zeros_like(dws_ref)
            dbs_ref[...] = jnp.zeros_like(dbs_ref)
            dlng_ref[...] = jnp.zeros_like(dlng_ref)
            dlnb_ref[...] = jnp.zeros_like(dlnb_ref)
            dog_ref[...] = jnp.zeros_like(dog_ref)

        gu, gv = gu_ref[...], gv_ref[...]
        lng = lng_ref[...]
        u, xhat, rstd, vn = _sgu_forward_math(gu, gv, lng, lnb_ref[...])
        vnb = vn.astype(BF16)
        for h in range(NG):
            sl = slice(h * HEAD, (h + 1) * HEAD)
            f = jnp.dot(ws_ref[h], vnb[:, sl], preferred_element_type=F32) + bs_ref[h]
            f_scr[:, sl] = f
            sg_scr[:, sl] = u[:, sl] * f
        sg = sg_scr[...]
        dm = dm_ref[...]
        r = lax.rsqrt(jnp.mean(sg * sg, axis=-1, keepdims=True) + EPS)
        sghat = sg * r
        dmg = dm * og_ref[...]
        dsg = r * (dmg - sghat * jnp.mean(dmg * sghat, axis=-1, keepdims=True))
        dog_ref[...] += jnp.sum(dm * sghat, axis=0, keepdims=True)
        du = dsg * f_scr[...]
        df = dsg * u
        dfb = df.astype(BF16)
        for h in range(NG):
            sl = slice(h * HEAD, (h + 1) * HEAD)
            dvn_scr[:, sl] = jnp.dot(wst_ref[h], dfb[:, sl], preferred_element_type=F32)
            dws_ref[h] += lax.dot_general(dfb[:, sl], vnb[:, sl], (((1,), (1,)), ((), ())),
                                          preferred_element_type=F32)
            dbs_ref[h] += jnp.broadcast_to(jnp.sum(df[:, sl], axis=-1, keepdims=True), (HEAD, HEAD))
        dvn = dvn_scr[...]
        dlng_ref[...] += jnp.sum(dvn * xhat, axis=0, keepdims=True)
        dlnb_ref[...] += jnp.sum(dvn, axis=0, keepdims=True)
        dxh = dvn * lng
        dvv = rstd * ((dxh - jnp.mean(dxh, axis=-1, keepdims=True))
                      - xhat * jnp.mean(dxh * xhat, axis=-1, keepdims=True))
        dgu_ref[...] = (du * _gelu_grad(gu)).astype(BF16)
        dgv_ref[...] = (dvv * _gelu_grad(gv)).astype(BF16)

    vec = pl.BlockSpec((1, GW), lambda c: (0, 0))
    mat = pl.BlockSpec((NG, HEAD, HEAD), lambda c: (0, 0, 0))
    wsp = pl.BlockSpec((None, NG, HEAD, HEAD), lambda c: (l, 0, 0, 0))
    return pl.pallas_call(
        body, name=name, grid=(S // HEAD,),
        in_specs=[pl.BlockSpec((HEAD, GW), lambda c: (c, 1)),
                  pl.BlockSpec((HEAD, GW), lambda c: (c, 2)),
                  pl.BlockSpec((HEAD, GW), lambda c: (c, 1)),
                  _vec_spec(l, GW), _vec_spec(l, GW), wsp, wsp, wsp, _vec_spec(l, GW)],
        out_specs=[pl.BlockSpec((HEAD, GW), lambda c: (c, 0)), pl.BlockSpec((HEAD, GW), lambda c: (c, 0)),
                   mat, mat, vec, vec, vec],
        out_shape=[jax.ShapeDtypeStruct((S, GW), BF16), jax.ShapeDtypeStruct((S, GW), BF16),
                   jax.ShapeDtypeStruct((NG, HEAD, HEAD), F32), jax.ShapeDtypeStruct((NG, HEAD, HEAD), F32),
                   jax.ShapeDtypeStruct((1, GW), F32), jax.ShapeDtypeStruct((1, GW), F32),
                   jax.ShapeDtypeStruct((1, GW), F32)],
        scratch_shapes=[pltpu.VMEM((HEAD, GW), F32), pltpu.VMEM((HEAD, GW), F32), pltpu.VMEM((HEAD, GW), F32)],
        compiler_params=_params(("arbitrary",)),
    )(z, z, dmix, lng3, lnb3, ws_b, wst_b, bs_b, og3)


def _mesh_pos():
    x, y, c = lax.axis_index("x"), lax.axis_index("y"), lax.axis_index("c")
    return x, y, c


def _dev_index(p):
    return 4 * p[0] + 2 * p[1] + p[2]


def _handshake(peers):
    barrier = pltpu.get_barrier_semaphore()
    for p in peers:
        pl.semaphore_signal(barrier, inc=1, device_id=p, device_id_type=MESH)
    pl.semaphore_wait(barrier, len(peers))


def _gather_body(na, slabs):
    def body(*refs):
        ins, outs = refs[:na], refs[na:2 * na]
        send_sems, recv_sems, local_sems = refs[2 * na:]
        x, y, c = _mesh_pos()
        me, sib = (x, y, c), (x, y, 1 - c)
        chips = [(1 - x, y), (x, 1 - y), (1 - x, 1 - y)]
        _handshake([sib] + [(*chip, c) for chip in chips])

        def copy(a, k, block, to, src=None):
            dst = slabs[a](outs[a], _dev_index(block))
            return pltpu.make_async_remote_copy(
                src_ref=dst if src is None else src, dst_ref=dst,
                send_sem=send_sems.at[7 * a + k], recv_sem=recv_sems.at[7 * a + k],
                device_id=to, device_id_type=MESH)

        local = [pltpu.make_async_copy(ins[a], slabs[a](outs[a], _dev_index(me)), local_sems.at[a])
                 for a in range(na)]
        for cp in local:
            cp.start()
        first = []
        for a in range(na):
            first.append(copy(a, 0, me, sib, src=ins[a]))
            first += [copy(a, 1 + j, me, (*chip, c), src=ins[a]) for j, chip in enumerate(chips)]
        for cp in first:
            cp.start()
        passed = []
        for j, chip in enumerate(chips):
            for a in range(na):
                copy(a, 1 + j, (*chip, c), me).wait_recv()
                fwd = copy(a, 4 + j, (*chip, c), sib)
                fwd.start()
                passed.append(fwd)
        for a in range(na):
            copy(a, 0, sib, me).wait_recv()
            for j, chip in enumerate(chips):
                copy(a, 4 + j, (*chip, 1 - c), me).wait_recv()
        for cp in first + passed:
            cp.wait_send()
        for cp in local:
            cp.wait()

    return body


def _all_gather(name, cid, shards, out_shapes, slabs):
    na = len(shards)
    return pl.kernel(
        _gather_body(na, slabs), out_type=out_shapes,
        mesh=plsc.ScalarSubcoreMesh(axis_name="seq", num_cores=1), name=name,
        scratch_types=[pltpu.SemaphoreType.DMA((7 * na,)), pltpu.SemaphoreType.DMA((7 * na,)),
                       pltpu.SemaphoreType.DMA((na,))],
        compiler_params=pltpu.CompilerParams(collective_id=cid),
    )(*shards)


def _d2d_body(nb, ns):
    n = nb + ns

    def body(*refs):
        ins, outs = refs[:n], refs[n:2 * n]
        send_sems, recv_sems = refs[2 * n:]
        x, y, c = _mesh_pos()
        sib = (x, y, 1 - c)
        _handshake([sib])
        copies = []
        for t in range(n):
            cp = pltpu.make_async_remote_copy(
                src_ref=ins[t].at[:, 1 - c] if t < nb else ins[t], dst_ref=outs[t],
                send_sem=send_sems.at[t], recv_sem=recv_sems.at[t],
                device_id=sib, device_id_type=MESH)
            cp.start()
            copies.append(cp)
        for cp in copies:
            cp.wait()

    return body


def _rs_d2d(name, cid, bigs, smalls):
    shapes = [jax.ShapeDtypeStruct((g.shape[0],) + g.shape[2:], g.dtype) for g in bigs]
    shapes += [jax.ShapeDtypeStruct(s.shape, s.dtype) for s in smalls]
    n = len(shapes)
    return pl.kernel(
        _d2d_body(len(bigs), len(smalls)), out_type=shapes,
        mesh=plsc.ScalarSubcoreMesh(axis_name="seq", num_cores=1), name=name,
        scratch_types=[pltpu.SemaphoreType.DMA((n,)), pltpu.SemaphoreType.DMA((n,))],
        compiler_params=pltpu.CompilerParams(collective_id=cid),
    )(*bigs, *smalls)


def _pair_sum(name, g4, recv, cvec, row_mult):
    Q, _, R, C = g4.shape
    tr = _tile(R, 256, row_mult)

    def body(c_ref, a_ref, b_ref, o_ref):
        o_ref[...] = (a_ref[...].astype(F32) + b_ref[...].astype(F32)).astype(BF16)

    grid_spec = pltpu.PrefetchScalarGridSpec(
        num_scalar_prefetch=1, grid=(Q, R // tr),
        in_specs=[pl.BlockSpec((None, None, tr, C), lambda q, i, c_ref: (q, c_ref[0], i, 0)),
                  pl.BlockSpec((None, tr, C), lambda q, i, c_ref: (q, i, 0))],
        out_specs=pl.BlockSpec((None, tr, C), lambda q, i, c_ref: (q, i, 0)))
    return pl.pallas_call(
        body, name=name, grid_spec=grid_spec, out_shape=jax.ShapeDtypeStruct((Q, R, C), BF16),
        compiler_params=_params(("arbitrary", "arbitrary")),
    )(cvec, g4, recv)


def _add2(name, a, b):
    R, C = a.shape
    tr = _tile(R, 512, 8)

    def body(a_ref, b_ref, o_ref):
        o_ref[...] = a_ref[...] + b_ref[...]

    blk = pl.BlockSpec((tr, C), lambda i: (i, 0))
    return pl.pallas_call(body, name=name, grid=(R // tr,), in_specs=[blk, blk], out_specs=blk,
                          out_shape=jax.ShapeDtypeStruct((R, C), a.dtype),
                          compiler_params=_params(("arbitrary",)))(a, b)


def _ici_body(nb, ns):
    n = nb + ns

    def body(*refs):
        ins, outs = refs[:n], refs[n:2 * n]
        send_sems, recv_sems, local_sems = refs[2 * n:]
        x, y, c = _mesh_pos()
        q_me = 2 * x + y
        chips = [(x ^ (k >> 1), y ^ (k & 1)) for k in range(1, 4)]
        _handshake([(px, py, c) for px, py in chips])
        copies = []
        for t in range(n):
            src_of = (lambda q, t=t: ins[t].at[q]) if t < nb else (lambda q, t=t: ins[t])
            dst = outs[t].at[q_me]
            loc = pltpu.make_async_copy(src_of(q_me), dst, local_sems.at[t])
            loc.start()
            copies.append(loc)
            for k, (px, py) in enumerate(chips):
                cp = pltpu.make_async_remote_copy(
                    src_ref=src_of(2 * px + py), dst_ref=dst,
                    send_sem=send_sems.at[3 * t + k], recv_sem=recv_sems.at[3 * t + k],
                    device_id=(px, py, c), device_id_type=MESH)
                cp.start()
                copies.append(cp)
        for cp in copies:
            cp.wait()

    return body


def _rs_ici(name, cid, bigs, smalls):
    shapes = [jax.ShapeDtypeStruct(g.shape, g.dtype) for g in bigs]
    shapes += [jax.ShapeDtypeStruct((N_DEV // 2,) + s.shape, s.dtype) for s in smalls]
    n = len(shapes)
    return pl.kernel(
        _ici_body(len(bigs), len(smalls)), out_type=shapes,
        mesh=plsc.ScalarSubcoreMesh(axis_name="seq", num_cores=1), name=name,
        scratch_types=[pltpu.SemaphoreType.DMA((3 * n,)), pltpu.SemaphoreType.DMA((3 * n,)),
                       pltpu.SemaphoreType.DMA((n,))],
        compiler_params=pltpu.CompilerParams(collective_id=cid),
    )(*bigs, *smalls)


def _adamw_math(w, g, m, v):
    m2 = ADAM_B1 * m + (1.0 - ADAM_B1) * g
    v2 = ADAM_B2 * v + (1.0 - ADAM_B2) * (g * g)
    m_hat = m2 / (1.0 - ADAM_B1 ** ADAM_STEP)
    v_hat = v2 / (1.0 - ADAM_B2 ** ADAM_STEP)
    delta = -ADAM_LR * (m_hat / (jnp.sqrt(v_hat) + ADAM_EPS) + ADAM_WD * w)
    return delta, m2, v2


def _adamw_sum(name, recv, w, m, v, l, prev, row_mult):
    NS, R, C = recv.shape
    L = w.shape[0]
    tr = _tile(R, 128, row_mult)

    def body(*refs):
        r_ref, w_ref, m_ref, v_ref = refs[:4]
        g_ref, d_ref, nm_ref, nv_ref = refs[-4:]
        g = r_ref[0].astype(F32)
        for s in range(1, NS):
            g = g + r_ref[s].astype(F32)
        d, m2, v2 = _adamw_math(w_ref[...], g, m_ref[...], v_ref[...])
        g_ref[...] = g
        d_ref[...] = d
        nm_ref[...] = m2
        nv_ref[...] = v2

    blk = pl.BlockSpec((None, tr, C), lambda i: (l, i, 0))
    shp = jax.ShapeDtypeStruct((L, R, C), F32)
    in_specs = [pl.BlockSpec((NS, tr, C), lambda i: (0, i, 0)), blk, blk, blk]
    args = [recv, w, m, v]
    aliases = {}
    if prev is not None:
        in_specs += [pl.BlockSpec(memory_space=pl.ANY)] * 4
        args += list(prev)
        aliases = {4 + i: i for i in range(4)}
    return pl.pallas_call(
        body, name=name, grid=(R // tr,),
        in_specs=in_specs, out_specs=[blk, blk, blk, blk], out_shape=[shp, shp, shp, shp],
        input_output_aliases=aliases,
        compiler_params=_params(("arbitrary",)),
    )(*args)


def _adamw_plain(name, g, w, m, v):
    def body(g_ref, w_ref, m_ref, v_ref, d_ref, nm_ref, nv_ref):
        d, m2, v2 = _adamw_math(w_ref[...], g_ref[...], m_ref[...], v_ref[...])
        d_ref[...] = d
        nm_ref[...] = m2
        nv_ref[...] = v2

    shp = jax.ShapeDtypeStruct(g.shape, F32)
    return pl.pallas_call(body, name=name, out_shape=[shp, shp, shp], compiler_params=_params())(g, w, m, v)


SMALL = ["norm1_g", "q_norm_g", "k_norm_g", "sink", "sgu_ln_g", "sgu_ln_b", "w_s", "b_s",
         "attn_out_g", "sgu_out_g", "norm2_g", "conv_b"]
PACK_ALIGN = 1024


def _pack(pieces):
    flat = []
    for p in pieces:
        f = p.reshape(-1).astype(F32)
        pad = (-f.shape[0]) % PACK_ALIGN
        flat.append(jnp.pad(f, (0, pad)) if pad else f)
    return jnp.concatenate(flat).reshape(-1, 128)


def _unpack(packed, shapes):
    flat = packed.reshape(-1)
    out, off = [], 0
    for shp in shapes:
        n = 1
        for d in shp:
            n *= d
        out.append(flat[off:off + n].reshape(shp))
        off += n + ((-n) % PACK_ALIGN)
    return out


def kernel(x, norm1_g, w_in, q_norm_g, k_norm_g, sink, sgu_ln_g, sgu_ln_b, w_s, b_s, attn_out_g, sgu_out_g, w_o, norm2_g, w_up, conv_w, conv_b, w_down, loss_target, m_norm1_g, m_w_in, m_q_norm_g, m_k_norm_g, m_sink, m_sgu_ln_g, m_sgu_ln_b, m_w_s, m_b_s, m_attn_out_g, m_sgu_out_g, m_w_o, m_norm2_g, m_w_up, m_conv_w, m_conv_b, m_w_down, v_norm1_g, v_w_in, v_q_norm_g, v_k_norm_g, v_sink, v_sgu_ln_g, v_sgu_ln_b, v_w_s, v_b_s, v_attn_out_g, v_sgu_out_g, v_w_o, v_norm2_g, v_w_up, v_conv_w, v_conv_b, v_w_down):
    weights = dict(norm1_g=norm1_g, w_in=w_in, q_norm_g=q_norm_g, k_norm_g=k_norm_g, sink=sink, sgu_ln_g=sgu_ln_g,
                   sgu_ln_b=sgu_ln_b, w_s=w_s, b_s=b_s, attn_out_g=attn_out_g, sgu_out_g=sgu_out_g, w_o=w_o,
                   norm2_g=norm2_g, w_up=w_up, conv_w=conv_w, conv_b=conv_b, w_down=w_down)
    mom_m = dict(norm1_g=m_norm1_g, w_in=m_w_in, q_norm_g=m_q_norm_g, k_norm_g=m_k_norm_g, sink=m_sink,
                 sgu_ln_g=m_sgu_ln_g, sgu_ln_b=m_sgu_ln_b, w_s=m_w_s, b_s=m_b_s, attn_out_g=m_attn_out_g,
                 sgu_out_g=m_sgu_out_g, w_o=m_w_o, norm2_g=m_norm2_g, w_up=m_w_up, conv_w=m_conv_w,
                 conv_b=m_conv_b, w_down=m_w_down)
    mom_v = dict(norm1_g=v_norm1_g, w_in=v_w_in, q_norm_g=v_q_norm_g, k_norm_g=v_k_norm_g, sink=v_sink,
                 sgu_ln_g=v_sgu_ln_g, sgu_ln_b=v_sgu_ln_b, w_s=v_w_s, b_s=v_b_s, attn_out_g=v_attn_out_g,
                 sgu_out_g=v_sgu_out_g, w_o=v_w_o, norm2_g=v_norm2_g, w_up=v_w_up, conv_w=v_conv_w,
                 conv_b=v_conv_b, w_down=v_w_down)
    order = ["norm1_g", "w_in", "q_norm_g", "k_norm_g", "sink", "sgu_ln_g", "sgu_ln_b", "w_s", "b_s",
             "attn_out_g", "sgu_out_g", "w_o", "norm2_g", "w_up", "conv_w", "conv_b", "w_down"]

    _, S, D = x.shape
    L = w_in.shape[0]
    AW = D // 2
    NQ = AW // HEAD
    NKV = max(1, NQ // 4)
    G = NQ // NKV
    KVW = NKV * HEAD
    GW = D - AW
    NG = GW // HEAD
    IN = AW + 2 * KVW + 2 * GW
    INS = w_in.shape[2]
    OS = w_o.shape[1]
    US = w_up.shape[2]
    DS = w_down.shape[1]
    F2 = US * N_DEV
    F = F2 // 2
    assert INS * N_DEV == IN and OS * N_DEV == D and DS * N_DEV == F and AW == GW and (3 * AW) % (2 * KVW) == 0
    cfg = dict(S=S, D=D, AW=AW, NQ=NQ, NKV=NKV, G=G, KVW=KVW, GW=GW, NG=NG, F=F, F2=F2)

    def rows_slab(n):
        return lambda ref, idx: ref.at[:, pl.ds(pl.multiple_of(idx * n, n), n), :]

    def cols_slab(n):
        return lambda ref, idx: ref.at[:, :, pl.ds(pl.multiple_of(idx * n, n), n)]

    def lead_slab(ref, idx):
        return ref.at[idx]

    def after(first, then):
        return lax.optimization_barrier((first, then))

    ids = iter(range(64))
    w_in_p, w_o_f, w_up_f, w_down_f = [], [], [], []

    def gather_layer(l, not_before=None):
        in_shard = w_in[l:l + 1].astype(BF16)
        if not_before is not None:
            _, in_shard = after(not_before, in_shard)
        (w_in_sh,) = _all_gather("ag_in", next(ids), [in_shard],
                                 [jax.ShapeDtypeStruct((N_DEV, 1, D, INS), BF16)], [lead_slab])
        w_in_f = jnp.transpose(w_in_sh, (1, 2, 0, 3)).reshape(1, D, IN)
        w_in_p.append(jnp.concatenate([w_in_f[:, :, :AW], w_in_f[:, :, AW + 2 * KVW:], w_in_f[:, :, AW:AW + 2 * KVW]],
                                      axis=2))
        w_o_f.extend(_all_gather("ag_o", next(ids), [w_o[l:l + 1].astype(BF16)],
                                 [jax.ShapeDtypeStruct((1, D, D), BF16)], [rows_slab(OS)]))
        w_up_f.extend(_all_gather("ag_up", next(ids), [w_up[l:l + 1].astype(BF16)],
                                  [jax.ShapeDtypeStruct((1, D, F2), BF16)], [cols_slab(US)]))
        w_down_f.extend(_all_gather("ag_down", next(ids), [w_down[l:l + 1].astype(BF16)],
                                    [jax.ShapeDtypeStruct((1, F, D), BF16)], [rows_slab(DS)]))

    gather_layer(0)
    (conv_w_f,) = _all_gather("ag_conv_w", next(ids), [conv_w], [jax.ShapeDtypeStruct((L, 3, F2), F32)],
                              [cols_slab(US)])

    n1g3, n2g3 = norm1_g.reshape(L, 1, D), norm2_g.reshape(L, 1, D)
    qg3, kg3 = q_norm_g.reshape(L, 1, HEAD), k_norm_g.reshape(L, 1, HEAD)
    lng3, lnb3 = sgu_ln_g.reshape(L, 1, GW), sgu_ln_b.reshape(L, 1, GW)
    ag3, og3 = attn_out_g.reshape(L, 1, AW), sgu_out_g.reshape(L, 1, GW)
    cb3 = conv_b.reshape(L, 1, F2)
    ws_b = w_s.astype(BF16)
    wst_b = jnp.swapaxes(w_s, 2, 3).astype(BF16)
    bs_b = jnp.broadcast_to(b_s[..., None], (L, NG, HEAD, HEAD))
    inv_freq = ROPE_THETA ** (-jnp.arange(0, HEAD, 2, dtype=F32) / HEAD)
    ang = jnp.arange(S, dtype=F32)[:, None] * inv_freq[None, :]
    cos2 = jnp.concatenate([jnp.cos(ang), jnp.cos(ang)], axis=1)
    sin2 = jnp.concatenate([-jnp.sin(ang), jnp.sin(ang)], axis=1)

    tn = 512
    t_in, t_d, t_f, t_f2 = _tile(IN, tn, 128), _tile(D, tn, 128), _tile(F, tn, 128), _tile(F2, tn, 128)
    tk_f = _tile(F, 1408, 128)
    tk_in = _tile(IN, 1792, 128)
    tm_f = _tile(F, 512, 128)

    def w_spec(tk, tn_):
        return pl.BlockSpec((None, tk, tn_), lambda i, j, k: (0, k, j))

    def wt_spec(tn_, tk):
        return pl.BlockSpec((None, tn_, tk), lambda i, j, k: (0, j, k))

    def a_spec(tm, tk):
        return pl.BlockSpec((tm, tk), lambda i, j, k: (i, k))

    def at_spec(tk, tm):
        return pl.BlockSpec((tk, tm), lambda i, j, k: (k, i))

    def b_spec(tk, tn_):
        return pl.BlockSpec((tk, tn_), lambda i, j, k: (k, j))

    xs = x.reshape(S, D)
    saved = []
    cur = xs
    for l in range(L):
        h = _rms_fwd("rms1_fwd", cur, n1g3, l)
        z = _mm("mm_in", h, w_in_p[l], M=S, N=IN, K=D, tm=S, tn=t_in, tk=D,
                a_spec=a_spec(S, D), b_spec=w_spec(D, t_in))
        if l + 1 < L:
            gather_layer(l + 1, not_before=z)
        q_r, k_r, v_b = _qkv_prep("qkv_prep", z, qg3, kg3, cos2, sin2, l, cfg)
        attn, mix_l = _attn_fwd("attn_fwd", q_r, k_r, v_b, sink, ag3, l, cfg)
        mix_r = _sgu_fwd("sgu_fwd", z, lng3, lnb3, ws_b, bs_b, og3, l, cfg)
        mixed = jnp.concatenate([mix_l, mix_r], axis=1)
        x1 = _mm("mm_o", mixed, w_o_f[l], M=S, N=D, K=D, tm=S, tn=t_d, tk=D,
                 a_spec=a_spec(S, D), b_spec=w_spec(D, t_d), res=cur)
        h2 = _rms_fwd("rms2_fwd", x1, n2g3, l)
        ap = _mm("mm_up", h2, w_up_f[l], M=S, N=F2, K=D, tm=S, tn=t_f2, tk=D,
                 a_spec=a_spec(S, D), b_spec=w_spec(D, t_f2))
        y_b = _conv_glu_fwd("conv_glu_fwd", ap, conv_w_f, cb3, l, cfg)
        x2 = _mm("mm_down", y_b, w_down_f[l], M=S, N=D, K=F, tm=S, tn=t_d, tk=tk_f,
                 a_spec=a_spec(S, tk_f), b_spec=w_spec(tk_f, t_d), res=x1)
        saved.append(dict(x=cur, h=h, z=z, q=q_r, k=k_r, v=v_b, attn=attn, mixed=mixed, x1=x1, h2=h2, ap=ap, y=y_b))
        cur = x2

    loss_tile, dx, dxb = _loss_bwd("loss", cur, loss_target.reshape(S, D))
    loss = lax.psum(loss_tile[0, 0], ("x", "y", "c"))

    gS = [dict() for _ in range(L)]
    recv = [dict() for _ in range(L)]
    cvec = jnp.reshape(lax.axis_index("c"), (1,)).astype(jnp.int32)

    def rs_swap(name, g, R, C, small=None):
        g4 = g.reshape(N_DEV // 2, 2, R, C)
        got = _rs_d2d("rs_d2d_" + name, next(ids), [g4], [] if small is None else [small])
        return name, g4, got, small

    in_flight = []
    IN_FLIGHT_MAX = 2

    def rs_finish(l, pending, then=None):
        name, g4, got, small = pending
        chip = _pair_sum("pair_sum_" + name, g4, got[0], cvec, 16)
        if then is not None:
            chip, then = after(chip, then)
            while len(in_flight) >= IN_FLIGHT_MAX:
                ll, nn, outs = in_flight.pop(0)
                outs, then = after(outs, then)
                recv[ll][nn] = outs
        chip_small = [] if small is None else [_add2("pair_sum_small", small, got[1])]
        in_flight.append((l, name, _rs_ici("rs_ici_" + name, next(ids), [chip], chip_small)))
        return then

    nkf = F // tk_f
    carried = None
    for l in reversed(range(L)):
        sv = saved[l]
        dy = _mm("mm_dy", dxb, w_down_f[l], M=S, N=F, K=D, tm=S, tn=t_f, tk=D, tb=True,
                 a_spec=a_spec(S, D), b_spec=wt_spec(t_f, D))
        if carried is not None:
            dy = rs_finish(l + 1, carried, dy)
        g_down = _mm("mm_gdown", sv["y"], dxb, M=F, N=D, K=S, tm=tm_f, tn=D, tk=S, ta=True,
                     a_spec=at_spec(S, tm_f), b_spec=b_spec(S, D), out_dtype=BF16)
        g_down, dy = after(g_down, dy)
        swap_down = rs_swap("w_down", g_down, DS, D)
        dap3, dcw, dcb = _glu_conv_bwd("glu_conv_bwd", dy, sv["ap"], conv_w_f, cb3, l, cfg)
        gS[l]["conv_w"] = jnp.concatenate([dcw[0], dcw[1]], axis=1)
        gS[l]["conv_b"] = jnp.concatenate([dcb[0], dcb[1]], axis=1).reshape(F2)
        dap3 = rs_finish(l, swap_down, dap3)
        dh2 = _mm("mm_dh2", dap3, w_up_f[l], M=S, N=D, K=F2, tm=S, tn=t_d, tk=tk_f, tb=True,
                  a_spec=pl.BlockSpec((None, S, tk_f), lambda i, j, k: (k // nkf, 0, k % nkf)),
                  b_spec=wt_spec(t_d, tk_f))
        g_up = _mm("mm_gup", sv["h2"], dap3, M=D, N=F2, K=S, tm=D, tn=US, tk=S, ta=True,
                   a_spec=at_spec(S, D),
                   b_spec=pl.BlockSpec((None, S, US), lambda i, j, k: (j // (N_DEV // 2), 0, j % (N_DEV // 2))),
                   out_dtype=BF16, out_shape=(N_DEV, D, US),
                   out_spec=pl.BlockSpec((None, D, US), lambda i, j, k: (j, 0, 0)))
        g_up, dh2 = after(g_up, dh2)
        swap_up = rs_swap("w_up", g_up, D, US)
        dx1, dx1b, dg2 = _rms_bwd("rms2_bwd", sv["x1"], dh2, dx, n2g3, l)
        gS[l]["norm2_g"] = dg2.reshape(D)
        dmix = _mm("mm_dmix", dx1b, w_o_f[l], M=S, N=D, K=D, tm=S, tn=t_d, tk=D, tb=True,
                   a_spec=a_spec(S, D), b_spec=wt_spec(t_d, D))
        g_o = _mm("mm_go", sv["mixed"], dx1b, M=D, N=D, K=S, tm=D, tn=t_d, tk=S, ta=True,
                  a_spec=at_spec(S, D), b_spec=b_spec(S, t_d), out_dtype=BF16)
        g_o, dmix = after(g_o, dmix)
        swap_o = rs_swap("w_o", g_o, OS, D)
        dmix = rs_finish(l, swap_up, dmix)
        dq_r, dk_pad, dv_pad, dsink, dag = _attn_bwd("attn_bwd", sv["q"], sv["k"], sv["v"], sv["attn"], dmix,
                                                     sink, ag3, l, cfg)
        dq_r = rs_finish(l, swap_o, dq_r)
        gS[l]["sink"] = dsink[:, 0]
        gS[l]["attn_out_g"] = dag.reshape(AW)
        dzgu, dzgv, dws, dbs, dlng, dlnb, dog = _sgu_bwd("sgu_bwd", sv["z"], dmix, lng3, lnb3, ws_b, wst_b, bs_b,
                                                        og3, l, cfg)
        gS[l]["w_s"] = dws
        gS[l]["b_s"] = dbs[:, :, 0]
        gS[l]["sgu_ln_g"] = dlng.reshape(GW)
        gS[l]["sgu_ln_b"] = dlnb.reshape(GW)
        gS[l]["sgu_out_g"] = dog.reshape(GW)
        dzq, dzkv, dqg, dkg = _qkv_prep_bwd("qkv_prep_bwd", sv["z"], dq_r, dk_pad, dv_pad, qg3, kg3, cos2, sin2,
                                            l, cfg)
        gS[l]["q_norm_g"] = dqg.reshape(HEAD)
        gS[l]["k_norm_g"] = dkg.reshape(HEAD)
        dz = jnp.concatenate([dzq, dzgu, dzgv, dzkv], axis=1)
        dh = _mm("mm_dh", dz, w_in_p[l], M=S, N=D, K=IN, tm=S, tn=t_d, tk=tk_in, tb=True,
                 a_spec=a_spec(S, tk_in), b_spec=wt_spec(t_d, tk_in))
        gin_p = _mm("mm_gin", sv["h"], dz, M=D, N=IN, K=S, tm=D, tn=t_in, tk=S, ta=True,
                    a_spec=at_spec(S, D), b_spec=b_spec(S, t_in), out_dtype=BF16)
        gin_p, dh = after(gin_p, dh)
        gin = jnp.concatenate([gin_p[:, :AW], gin_p[:, AW + 2 * GW:], gin_p[:, AW:AW + 2 * GW]], axis=1)
        g_in = jnp.transpose(gin.reshape(D, N_DEV, INS), (1, 0, 2))
        dx, dxb, dg1 = _rms_bwd("rms1_bwd", sv["x"], dh, dx1, n1g3, l)
        gS[l]["norm1_g"] = dg1.reshape(D)
        carried = rs_swap("w_in", g_in, D, INS, small=_pack([gS[l][n] for n in SMALL] + [gS[l]["conv_w"]]))
    rs_finish(0, carried)
    for ll, nn, outs in in_flight:
        recv[ll][nn] = outs
    grad_x = dx.reshape(1, S, D)

    grads, deltas, new_m, new_v = {}, {}, {}, {}
    for name in ["w_down", "w_up", "w_o", "w_in"]:
        res = None
        for l in reversed(range(L)):
            res = _adamw_sum("adamw_" + name, recv[l][name][0], weights[name], mom_m[name], mom_v[name], l, res, 16)
        grads[name], deltas[name], new_m[name], new_v[name] = res

    small_shapes = [weights[n].shape[1:] for n in SMALL] + [(3, F2)]
    SR = recv[0]["w_in"][1].shape[1]
    tr_s = _tile(SR, 512, 8)
    sblk = pl.BlockSpec((tr_s, 128), lambda i: (i, 0))
    sshp = jax.ShapeDtypeStruct((SR, 128), F32)
    small_parts = [[None] * L for _ in range(4)]
    for l in reversed(range(L)):
        def small_body(r_ref, w_ref, m_ref, v_ref, g_ref, d_ref, nm_ref, nv_ref):
            g = r_ref[0]
            for s in range(1, N_DEV // 2):
                g = g + r_ref[s]
            d, m2, v2 = _adamw_math(w_ref[...], g, m_ref[...], v_ref[...])
            g_ref[...] = g
            d_ref[...] = d
            nm_ref[...] = m2
            nv_ref[...] = v2

        def pack_small(src):
            return _pack([src[n][l] for n in SMALL] + [jnp.zeros((3, F2), F32)])

        packed = pl.pallas_call(
            small_body, name="adamw_small", grid=(SR // tr_s,),
            in_specs=[pl.BlockSpec((N_DEV // 2, tr_s, 128), lambda i: (0, i, 0)), sblk, sblk, sblk],
            out_specs=[sblk, sblk, sblk, sblk], out_shape=[sshp, sshp, sshp, sshp],
            compiler_params=_params(("arbitrary",)),
        )(recv[l]["w_in"][1], pack_small(weights), pack_small(mom_m), pack_small(mom_v))
        for t in range(4):
            small_parts[t][l] = _unpack(packed[t], small_shapes)
    for t, store in enumerate([grads, deltas, new_m, new_v]):
        for i, n in enumerate(SMALL):
            store[n] = jnp.stack([small_parts[t][l][i] for l in range(L)])
    conv_full = jnp.stack([small_parts[0][l][len(SMALL)] for l in range(L)])
    me_i = _dev_index(_mesh_pos())
    g_cw = lax.dynamic_slice_in_dim(conv_full, me_i * US, US, axis=2)
    grads["conv_w"] = g_cw
    d_cw, m_cw, v_cw = _adamw_plain("adamw_conv_w", g_cw.reshape(L * 3, US), conv_w.reshape(L * 3, US),
                                    m_conv_w.reshape(L * 3, US), v_conv_w.reshape(L * 3, US))
    deltas["conv_w"] = d_cw.reshape(L, 3, US)
    new_m["conv_w"] = m_cw.reshape(L, 3, US)
    new_v["conv_w"] = v_cw.reshape(L, 3, US)

    return (loss, grad_x, *[grads[n] for n in order], *[deltas[n] for n in order],
            *[new_m[n] for n in order], *[new_v[n] for n in order])
```

```python
import jax
import jax.numpy as jnp
from jax import lax
from jax.experimental import pallas as pl
from jax.experimental.pallas import tpu as pltpu
from jax.experimental.pallas import tpu_sc as plsc

F32 = jnp.float32
BF16 = jnp.bfloat16
MESH = pl.DeviceIdType.MESH

N_DEV = 8
HEAD = 128
EPS = 1e-6
MASK_VALUE = -1e30
ROPE_THETA = 10000.0
GELU_C = 0.7978845608028654
GELU_A = 0.044715

ADAM_LR = 0.001
ADAM_B1 = 0.9
ADAM_B2 = 0.999
ADAM_EPS = 1e-08
ADAM_WD = 0.01
ADAM_STEP = 10

VMEM_LIMIT = 56 * 1024 * 1024


def _tile(n, pref, mult):
    best = None
    for t in range(mult, min(n, pref) + 1, mult):
        if n % t == 0:
            best = t
    return n if best is None else best


def _params(sem=None):
    kw = dict(vmem_limit_bytes=VMEM_LIMIT)
    if sem is not None:
        kw["dimension_semantics"] = sem
    return pltpu.CompilerParams(**kw)


def _gelu(x):
    return x * (0.5 * (1.0 + jnp.tanh(GELU_C * (x + GELU_A * (x * x * x)))))


def _gelu_grad(x):
    t = jnp.tanh(GELU_C * (x + GELU_A * (x * x * x)))
    return 0.5 * (1.0 + t) + 0.5 * x * (1.0 - t * t) * (GELU_C * (1.0 + 3.0 * GELU_A * (x * x)))


def _sigmoid(x):
    return 1.0 / (1.0 + jnp.exp(-x))


def _vec_spec(l, n):
    return pl.BlockSpec((None, 1, n), lambda *_: (l, 0, 0))


def _mm(name, a, b, *, M, N, K, tm, tn, tk, a_spec, b_spec, ta=False, tb=False, out_dtype=F32, res=None,
        out_shape=None, out_spec=None):
    nm, nn, nk = M // tm, N // tn, K // tk
    assert nm * tm == M and nn * tn == N and nk * tk == K
    assert not (ta and nk > 1)
    dims = (((1,), (1,)), ((), ())) if tb else (((1,), (0,)), ((), ()))

    def body(*refs):
        refs = list(refs)
        a_ref = refs.pop(0)
        b_ref = refs.pop(0)
        r_ref = refs.pop(0) if res is not None else None
        o_ref = refs.pop(0)
        acc = refs.pop(0) if nk > 1 else None
        at = refs.pop(0) if ta else None
        k = pl.program_id(2)
        if ta:
            @pl.when(pl.program_id(1) == 0)
            def _():
                at[...] = a_ref[...].T
            lhs = at[...]
        else:
            lhs = a_ref[...]
        p = lax.dot_general(lhs, b_ref[...], dims, preferred_element_type=F32)

        def finish(r):
            if r_ref is not None:
                r = r_ref[...] + r
            o_ref[...] = r.astype(out_dtype)

        if nk == 1:
            finish(p)
        else:
            @pl.when(k == 0)
            def _():
                acc[...] = p

            @pl.when(k > 0)
            def _():
                acc[...] += p

            @pl.when(k == nk - 1)
            def _():
                finish(acc[...])

    in_specs = [a_spec, b_spec]
    args = [a, b]
    if res is not None:
        in_specs.append(pl.BlockSpec((tm, tn), lambda i, j, k: (i, j)))
        args.append(res)
    scratch = []
    if nk > 1:
        scratch.append(pltpu.VMEM((tm, tn), F32))
    if ta:
        scratch.append(pltpu.VMEM((tm, tk), BF16))
    return pl.pallas_call(
        body, name=name, grid=(nm, nn, nk),
        in_specs=in_specs,
        out_specs=pl.BlockSpec((tm, tn), lambda i, j, k: (i, j)) if out_spec is None else out_spec,
        out_shape=jax.ShapeDtypeStruct((M, N) if out_shape is None else out_shape, out_dtype),
        scratch_shapes=scratch,
        compiler_params=_params(("arbitrary", "arbitrary", "arbitrary")),
    )(*args)


def _rms_fwd(name, x, g3, l):
    S, D = x.shape
    tr = _tile(S, 256, 16)

    def body(x_ref, g_ref, h_ref):
        xv = x_ref[...]
        r = lax.rsqrt(jnp.mean(xv * xv, axis=-1, keepdims=True) + EPS)
        h_ref[...] = ((xv * r) * g_ref[...]).astype(BF16)

    return pl.pallas_call(
        body, name=name, grid=(S // tr,),
        in_specs=[pl.BlockSpec((tr, D), lambda i: (i, 0)), _vec_spec(l, D)],
        out_specs=pl.BlockSpec((tr, D), lambda i: (i, 0)),
        out_shape=jax.ShapeDtypeStruct((S, D), BF16),
        compiler_params=_params(("arbitrary",)),
    )(x, g3)


def _rope(t, cos2, sin2):
    return t * cos2 + pltpu.roll(t, HEAD // 2, axis=1) * sin2


def _qkv_prep(name, z, qg3, kg3, cos2, sin2, l, cfg):
    S, AW, KVW, NQ, NKV = cfg["S"], cfg["AW"], cfg["KVW"], cfg["NQ"], cfg["NKV"]
    tr = _tile(S, 256, 16)
    kv_blk = (3 * AW) // (2 * KVW)

    def body(zq_ref, zkv_ref, qg_ref, kg_ref, c_ref, s_ref, q_ref, k_ref, v_ref):
        cosv, sinv = c_ref[...], s_ref[...]

        def norm_rope(t, g):
            r = lax.rsqrt(jnp.mean(t * t, axis=-1, keepdims=True) + EPS)
            return _rope((t * r) * g, cosv, sinv)

        for h in range(NQ):
            sl = slice(h * HEAD, (h + 1) * HEAD)
            q_ref[:, sl] = norm_rope(zq_ref[:, sl], qg_ref[...]).astype(BF16)
        for h in range(NKV):
            sl = slice(h * HEAD, (h + 1) * HEAD)
            k_ref[:, sl] = norm_rope(zkv_ref[:, sl], kg_ref[...]).astype(BF16)
        v_ref[...] = zkv_ref[:, KVW:].astype(BF16)

    return pl.pallas_call(
        body, name=name, grid=(S // tr,),
        in_specs=[pl.BlockSpec((tr, AW), lambda i: (i, 0)),
                  pl.BlockSpec((tr, 2 * KVW), lambda i: (i, kv_blk)),
                  _vec_spec(l, HEAD), _vec_spec(l, HEAD),
                  pl.BlockSpec((tr, HEAD), lambda i: (i, 0)),
                  pl.BlockSpec((tr, HEAD), lambda i: (i, 0))],
        out_specs=[pl.BlockSpec((tr, AW), lambda i: (i, 0)),
                   pl.BlockSpec((tr, KVW), lambda i: (i, 0)),
                   pl.BlockSpec((tr, KVW), lambda i: (i, 0))],
        out_shape=[jax.ShapeDtypeStruct((S, AW), BF16),
                   jax.ShapeDtypeStruct((S, KVW), BF16),
                   jax.ShapeDtypeStruct((S, KVW), BF16)],
        compiler_params=_params(("arbitrary",)),
    )(z, z, qg3, kg3, cos2, sin2)


def _band_specs(width, nb):
    return [pl.BlockSpec((HEAD, width), lambda n: (jnp.maximum(n - 1, 0), 0)),
            pl.BlockSpec((HEAD, width), lambda n: (n, 0)),
            pl.BlockSpec((HEAD, width), lambda n: (jnp.minimum(n + 1, nb - 1), 0))]


def _attn_probs(qs, kj, n, sink_of_row, S, G):
    s = lax.dot_general(qs, kj, (((1,), (1,)), ((), ())), preferred_element_type=F32) * (HEAD ** -0.5)
    rows = lax.broadcasted_iota(jnp.int32, (G * HEAD, 3 * HEAD), 0)
    cols = lax.broadcasted_iota(jnp.int32, (G * HEAD, 3 * HEAD), 1)
    qi = rows & (HEAD - 1)
    kpos = n * HEAD - HEAD + cols
    valid = (cols >= qi) & (cols <= qi + 2 * HEAD) & (kpos >= 0) & (kpos < S)
    s = jnp.where(valid, s, MASK_VALUE)
    m = jnp.maximum(jnp.max(s, axis=-1, keepdims=True), sink_of_row)
    p = jnp.exp(s - m)
    e_sink = jnp.exp(sink_of_row - m)
    inv = 1.0 / (jnp.sum(p, axis=-1, keepdims=True) + e_sink)
    return p * inv, e_sink * inv


def _sink_rows(sink_ref, l, j, G):
    hidx = lax.broadcasted_iota(jnp.int32, (G * HEAD, 1), 0) // HEAD
    col = jnp.full((G * HEAD, 1), sink_ref[l, j * G], F32)
    for g in range(1, G):
        col = jnp.where(hidx == g, sink_ref[l, j * G + g], col)
    return col


def _attn_fwd(name, q, k, v, sink, ag3, l, cfg):
    S, AW, KVW, NKV, G = cfg["S"], cfg["AW"], cfg["KVW"], cfg["NKV"], cfg["G"]
    nb = S // HEAD

    def body(q_ref, kp, kc, kn, vp, vc, vn, sink_ref, ag_ref, a_ref, mix_ref):
        n = pl.program_id(0)
        kb = jnp.concatenate([kp[...], kc[...], kn[...]], axis=0)
        vb = jnp.concatenate([vp[...], vc[...], vn[...]], axis=0)
        for j in range(NKV):
            sl = slice(j * HEAD, (j + 1) * HEAD)
            qs = jnp.concatenate([q_ref[:, (j * G + g) * HEAD:(j * G + g + 1) * HEAD] for g in range(G)], axis=0)
            probs, _ = _attn_probs(qs, kb[:, sl], n, _sink_rows(sink_ref, l, j, G), S, G)
            o = jnp.dot(probs.astype(BF16), vb[:, sl], preferred_element_type=F32)
            for g in range(G):
                a_ref[:, (j * G + g) * HEAD:(j * G + g + 1) * HEAD] = o[g * HEAD:(g + 1) * HEAD]
        a = a_ref[...]
        r = lax.rsqrt(jnp.mean(a * a, axis=-1, keepdims=True) + EPS)
        mix_ref[...] = ((a * r) * ag_ref[...]).astype(BF16)

    return pl.pallas_call(
        body, name=name, grid=(nb,),
        in_specs=[pl.BlockSpec((HEAD, AW), lambda n: (n, 0))] + _band_specs(KVW, nb) + _band_specs(KVW, nb)
                 + [pl.BlockSpec(memory_space=pltpu.SMEM), _vec_spec(l, AW)],
        out_specs=[pl.BlockSpec((HEAD, AW), lambda n: (n, 0)), pl.BlockSpec((HEAD, AW), lambda n: (n, 0))],
        out_shape=[jax.ShapeDtypeStruct((S, AW), F32), jax.ShapeDtypeStruct((S, AW), BF16)],
        compiler_params=_params(("arbitrary",)),
    )(q, k, k, k, v, v, v, sink, ag3)


def _sgu_forward_math(gu, gv, lng, lnb):
    u = _gelu(gu)
    vv = _gelu(gv)
    mu = jnp.mean(vv, axis=-1, keepdims=True)
    xc = vv - mu
    rstd = lax.rsqrt(jnp.mean(xc * xc, axis=-1, keepdims=True) + EPS)
    xhat = xc * rstd
    vn = xhat * lng + lnb
    return u, xhat, rstd, vn


def _sgu_fwd(name, z, lng3, lnb3, ws_b, bs_b, og3, l, cfg):
    S, GW, NG = cfg["S"], cfg["GW"], cfg["NG"]

    def body(gu_ref, gv_ref, lng_ref, lnb_ref, ws_ref, bs_ref, og_ref, mix_ref, sg_ref):
        u, _, _, vn = _sgu_forward_math(gu_ref[...], gv_ref[...], lng_ref[...], lnb_ref[...])
        vnb = vn.astype(BF16)
        for h in range(NG):
            sl = slice(h * HEAD, (h + 1) * HEAD)
            f = jnp.dot(ws_ref[h], vnb[:, sl], preferred_element_type=F32) + bs_ref[h]
            sg_ref[:, sl] = u[:, sl] * f
        sg = sg_ref[...]
        r = lax.rsqrt(jnp.mean(sg * sg, axis=-1, keepdims=True) + EPS)
        mix_ref[...] = ((sg * r) * og_ref[...]).astype(BF16)

    return pl.pallas_call(
        body, name=name, grid=(S // HEAD,),
        in_specs=[pl.BlockSpec((HEAD, GW), lambda c: (c, 1)),
                  pl.BlockSpec((HEAD, GW), lambda c: (c, 2)),
                  _vec_spec(l, GW), _vec_spec(l, GW),
                  pl.BlockSpec((None, NG, HEAD, HEAD), lambda c: (l, 0, 0, 0)),
                  pl.BlockSpec((None, NG, HEAD, HEAD), lambda c: (l, 0, 0, 0)),
                  _vec_spec(l, GW)],
        out_specs=pl.BlockSpec((HEAD, GW), lambda c: (c, 0)),
        out_shape=jax.ShapeDtypeStruct((S, GW), BF16),
        scratch_shapes=[pltpu.VMEM((HEAD, GW), F32)],
        compiler_params=_params(("arbitrary",)),
    )(z, z, lng3, lnb3, ws_b, bs_b, og3)


CONV_HALO = 8
CONV_ROWS = 64


def _for_row_windows(S, fn):
    R, W = CONV_ROWS, CONV_ROWS + 2 * CONV_HALO
    n = S // R
    assert n * R == S and n >= 2
    fn(0, 0, 0)
    if n > 2:
        def mid(k, carry):
            fn(pl.multiple_of(k * R - CONV_HALO, CONV_HALO), CONV_HALO, pl.multiple_of(k * R, R))
            return carry

        lax.fori_loop(1, n - 1, mid, 0)
    fn(S - W, 2 * CONV_HALO, S - R)


def _shifts(t):
    W = t.shape[0]
    row = lax.broadcasted_iota(jnp.int32, t.shape, 0)
    dn = jnp.where(row == 0, 0.0, pltpu.roll(t, 1, axis=0))
    up = jnp.where(row == W - 1, 0.0, pltpu.roll(t, W - 1, axis=0))
    return dn, up


def _conv3(t, w, b):
    dn, up = _shifts(t)
    return ((b + dn * w[0:1]) + t * w[1:2]) + up * w[2:3], dn, up


def _conv_glu_fwd(name, ap, cw, cb3, l, cfg):
    S, F = cfg["S"], cfg["F"]
    tc = _tile(F, 256, 128)
    nf = F // tc
    R, W = CONV_ROWS, CONV_ROWS + 2 * CONV_HALO

    def body(g_ref, u_ref, wg_ref, wu_ref, bg_ref, bu_ref, y_ref):
        wg, wu, bg, bu = wg_ref[...], wu_ref[...], bg_ref[...], bu_ref[...]

        def window(start, lo, out0):
            rows = pl.ds(start, W)
            ag, _, _ = _conv3(g_ref[rows, :], wg, bg)
            au, _, _ = _conv3(u_ref[rows, :], wu, bu)
            y = (ag * _sigmoid(ag)) * au
            y_ref[pl.ds(out0, R), :] = y[lo:lo + R].astype(BF16)

        _for_row_windows(S, window)

    return pl.pallas_call(
        body, name=name, grid=(nf,),
        in_specs=[pl.BlockSpec((S, tc), lambda j: (0, j)),
                  pl.BlockSpec((S, tc), lambda j: (0, j + nf)),
                  pl.BlockSpec((None, 3, tc), lambda j: (l, 0, j)),
                  pl.BlockSpec((None, 3, tc), lambda j: (l, 0, j + nf)),
                  pl.BlockSpec((None, 1, tc), lambda j: (l, 0, j)),
                  pl.BlockSpec((None, 1, tc), lambda j: (l, 0, j + nf))],
        out_specs=pl.BlockSpec((S, tc), lambda j: (0, j)),
        out_shape=jax.ShapeDtypeStruct((S, F), BF16),
        compiler_params=_params(("arbitrary",)),
    )(ap, ap, cw, cw, cb3, cb3)


def _loss_bwd(name, y, target):
    S, D = y.shape
    tr = _tile(S, 256, 16)

    def body(y_ref, t_ref, loss_ref, d_ref, db_ref):
        @pl.when(pl.program_id(0) == 0)
        def _():
            loss_ref[...] = jnp.zeros_like(loss_ref)

        err = y_ref[...] - t_ref[...]
        part = 0.5 * jnp.sum(jnp.mean(err * err, axis=-1, keepdims=True), axis=0, keepdims=True)
        loss_ref[...] += jnp.broadcast_to(part, loss_ref.shape)
        d = err * (1.0 / D)
        d_ref[...] = d
        db_ref[...] = d.astype(BF16)

    return pl.pallas_call(
        body, name=name, grid=(S // tr,),
        in_specs=[pl.BlockSpec((tr, D), lambda i: (i, 0)), pl.BlockSpec((tr, D), lambda i: (i, 0))],
        out_specs=[pl.BlockSpec((8, 128), lambda i: (0, 0)),
                   pl.BlockSpec((tr, D), lambda i: (i, 0)),
                   pl.BlockSpec((tr, D), lambda i: (i, 0))],
        out_shape=[jax.ShapeDtypeStruct((8, 128), F32),
                   jax.ShapeDtypeStruct((S, D), F32),
                   jax.ShapeDtypeStruct((S, D), BF16)],
        compiler_params=_params(("arbitrary",)),
    )(y, target)


def _rms_bwd(name, x, dh, dres, g3, l):
    S, D = x.shape
    tr = _tile(S, 256, 16)

    def body(x_ref, dh_ref, dr_ref, g_ref, dx_ref, dxb_ref, dg_ref):
        @pl.when(pl.program_id(0) == 0)
        def _():
            dg_ref[...] = jnp.zeros_like(dg_ref)

        xv = x_ref[...]
        dhv = dh_ref[...]
        r = lax.rsqrt(jnp.mean(xv * xv, axis=-1, keepdims=True) + EPS)
        xhat = xv * r
        dhg = dhv * g_ref[...]
        dx = dr_ref[...] + r * (dhg - xhat * jnp.mean(dhg * xhat, axis=-1, keepdims=True))
        dx_ref[...] = dx
        dxb_ref[...] = dx.astype(BF16)
        dg_ref[...] += jnp.sum(dhv * xhat, axis=0, keepdims=True)

    return pl.pallas_call(
        body, name=name, grid=(S // tr,),
        in_specs=[pl.BlockSpec((tr, D), lambda i: (i, 0)), pl.BlockSpec((tr, D), lambda i: (i, 0)),
                  pl.BlockSpec((tr, D), lambda i: (i, 0)), _vec_spec(l, D)],
        out_specs=[pl.BlockSpec((tr, D), lambda i: (i, 0)), pl.BlockSpec((tr, D), lambda i: (i, 0)),
                   pl.BlockSpec((1, D), lambda i: (0, 0))],
        out_shape=[jax.ShapeDtypeStruct((S, D), F32), jax.ShapeDtypeStruct((S, D), BF16),
                   jax.ShapeDtypeStruct((1, D), F32)],
        compiler_params=_params(("arbitrary",)),
    )(x, dh, dres, g3)


def _glu_conv_bwd(name, dy, ap, cw, cb3, l, cfg):
    S, F = cfg["S"], cfg["F"]
    tc = 128
    nf = F // tc
    R, W = CONV_ROWS, CONV_ROWS + 2 * CONV_HALO

    def body(dy_ref, g_ref, u_ref, wg_ref, wu_ref, bg_ref, bu_ref, dap_ref, dw_ref, db_ref, acc):
        acc[...] = jnp.zeros_like(acc)
        wg, wu, bg, bu = wg_ref[...], wu_ref[...], bg_ref[...], bu_ref[...]

        def window(start, lo, out0):
            rows = pl.ds(start, W)
            apg, apu, dyv = g_ref[rows, :], u_ref[rows, :], dy_ref[rows, :]
            ag, g_dn, g_up = _conv3(apg, wg, bg)
            au, u_dn, u_up = _conv3(apu, wu, bu)
            sig = _sigmoid(ag)
            da_u = dyv * (ag * sig)
            da_g = (dyv * au) * (sig * (1.0 + ag * (1.0 - sig)))

            def back(da, w):
                prv, nxt = _shifts(da)
                return ((nxt * w[0:1] + da * w[1:2]) + prv * w[2:3])[lo:lo + R].astype(BF16)

            dap_ref[0, pl.ds(out0, R), :] = back(da_g, wg)
            dap_ref[1, pl.ds(out0, R), :] = back(da_u, wu)

            def add(i, prod):
                acc[i] += jnp.sum(prod[lo:lo + R].reshape(R // 8, 8, tc), axis=0)

            for i, other in enumerate([g_dn, apg, g_up]):
                add(i, da_g * other)
            for i, other in enumerate([u_dn, apu, u_up]):
                add(3 + i, da_u * other)
            add(6, da_g)
            add(7, da_u)

        _for_row_windows(S, window)
        col = [jnp.sum(acc[i], axis=0, keepdims=True) for i in range(8)]
        dw_ref[0] = jnp.concatenate(col[0:3], axis=0)
        dw_ref[1] = jnp.concatenate(col[3:6], axis=0)
        db_ref[0] = col[6]
        db_ref[1] = col[7]

    return pl.pallas_call(
        body, name=name, grid=(nf,),
        in_specs=[pl.BlockSpec((S, tc), lambda j: (0, j)),
                  pl.BlockSpec((S, tc), lambda j: (0, j)),
                  pl.BlockSpec((S, tc), lambda j: (0, j + nf)),
                  pl.BlockSpec((None, 3, tc), lambda j: (l, 0, j)),
                  pl.BlockSpec((None, 3, tc), lambda j: (l, 0, j + nf)),
                  pl.BlockSpec((None, 1, tc), lambda j: (l, 0, j)),
                  pl.BlockSpec((None, 1, tc), lambda j: (l, 0, j + nf))],
        out_specs=[pl.BlockSpec((2, S, tc), lambda j: (0, 0, j)),
                   pl.BlockSpec((2, 3, tc), lambda j: (0, 0, j)),
                   pl.BlockSpec((2, 1, tc), lambda j: (0, 0, j))],
        out_shape=[jax.ShapeDtypeStruct((2, S, F), BF16),
                   jax.ShapeDtypeStruct((2, 3, F), F32),
                   jax.ShapeDtypeStruct((2, 1, F), F32)],
        scratch_shapes=[pltpu.VMEM((8, 8, tc), F32)],
        compiler_params=_params(("arbitrary",)),
    )(dy, ap, ap, cw, cw, cb3, cb3)


def _attn_bwd(name, q, k, v, attn, dmix, sink, ag3, l, cfg):
    S, AW, KVW, NQ, NKV, G = cfg["S"], cfg["AW"], cfg["KVW"], cfg["NQ"], cfg["NKV"], cfg["G"]
    nb = S // HEAD
    scale = HEAD ** -0.5

    def body(q_ref, kp, kc, kn, vp, vc, vn, a_ref, dm_ref, sink_ref, ag_ref,
             dq_ref, dk_ref, dv_ref, dsink_ref, dag_ref, da_scr):
        n = pl.program_id(0)

        @pl.when(n == 0)
        def _():
            dk_ref[...] = jnp.zeros_like(dk_ref)
            dv_ref[...] = jnp.zeros_like(dv_ref)
            dsink_ref[...] = jnp.zeros_like(dsink_ref)
            dag_ref[...] = jnp.zeros_like(dag_ref)

        a = a_ref[...]
        dm = dm_ref[...]
        r = lax.rsqrt(jnp.mean(a * a, axis=-1, keepdims=True) + EPS)
        xhat = a * r
        dmg = dm * ag_ref[...]
        da_scr[...] = r * (dmg - xhat * jnp.mean(dmg * xhat, axis=-1, keepdims=True))
        dag_ref[...] += jnp.sum(dm * xhat, axis=0, keepdims=True)

        kb = jnp.concatenate([kp[...], kc[...], kn[...]], axis=0)
        vb = jnp.concatenate([vp[...], vc[...], vn[...]], axis=0)
        band = pl.ds(pl.multiple_of(n * HEAD, HEAD), 3 * HEAD)
        for j in range(NKV):
            sl = slice(j * HEAD, (j + 1) * HEAD)
            heads = [slice((j * G + g) * HEAD, (j * G + g + 1) * HEAD) for g in range(G)]
            qs = jnp.concatenate([q_ref[:, hs] for hs in heads], axis=0)
            do = jnp.concatenate([da_scr[:, hs] for hs in heads], axis=0)
            kj, vj = kb[:, sl], vb[:, sl]
            probs, p_sink = _attn_probs(qs, kj, n, _sink_rows(sink_ref, l, j, G), S, G)
            dob = do.astype(BF16)
            dprobs = lax.dot_general(dob, vj, (((1,), (1,)), ((), ())), preferred_element_type=F32)
            delta = jnp.sum(dprobs * probs, axis=-1, keepdims=True)
            ds = (probs * (dprobs - delta)) * scale
            dsb = ds.astype(BF16)
            dsk = -(p_sink * delta)
            dq = jnp.dot(dsb, kj, preferred_element_type=F32)
            for g in range(G):
                dq_ref[:, heads[g]] = dq[g * HEAD:(g + 1) * HEAD]
                part = jnp.sum(dsk[g * HEAD:(g + 1) * HEAD], axis=0, keepdims=True)
                dsink_ref[j * G + g:j * G + g + 1, :] += jnp.broadcast_to(part, (1, HEAD))
            dk_ref[band, sl] += lax.dot_general(dsb, qs, (((0,), (0,)), ((), ())), preferred_element_type=F32)
            dv_ref[band, sl] += lax.dot_general(probs.astype(BF16), dob, (((0,), (0,)), ((), ())),
                                                preferred_element_type=F32)

    return pl.pallas_call(
        body, name=name, grid=(nb,),
        in_specs=[pl.BlockSpec((HEAD, AW), lambda n: (n, 0))] + _band_specs(KVW, nb) + _band_specs(KVW, nb)
                 + [pl.BlockSpec((HEAD, AW), lambda n: (n, 0)),
                    pl.BlockSpec((HEAD, AW), lambda n: (n, 0)),
                    pl.BlockSpec(memory_space=pltpu.SMEM), _vec_spec(l, AW)],
        out_specs=[pl.BlockSpec((HEAD, AW), lambda n: (n, 0)),
                   pl.BlockSpec((S + 2 * HEAD, KVW), lambda n: (0, 0)),
                   pl.BlockSpec((S + 2 * HEAD, KVW), lambda n: (0, 0)),
                   pl.BlockSpec((NQ, HEAD), lambda n: (0, 0)),
                   pl.BlockSpec((1, AW), lambda n: (0, 0))],
        out_shape=[jax.ShapeDtypeStruct((S, AW), F32),
                   jax.ShapeDtypeStruct((S + 2 * HEAD, KVW), F32),
                   jax.ShapeDtypeStruct((S + 2 * HEAD, KVW), F32),
                   jax.ShapeDtypeStruct((NQ, HEAD), F32),
                   jax.ShapeDtypeStruct((1, AW), F32)],
        scratch_shapes=[pltpu.VMEM((HEAD, AW), F32)],
        compiler_params=_params(("arbitrary",)),
    )(q, k, k, k, v, v, v, attn, dmix, sink, ag3)


def _qkv_prep_bwd(name, z, dq, dk_pad, dv_pad, qg3, kg3, cos2, sin2, l, cfg):
    S, AW, KVW, NQ, NKV = cfg["S"], cfg["AW"], cfg["KVW"], cfg["NQ"], cfg["NKV"]
    kv_blk = (3 * AW) // (2 * KVW)

    def body(zq_ref, zkv_ref, dq_ref, dk_ref, dv_ref, qg_ref, kg_ref, c_ref, s_ref,
             dzq_ref, dzkv_ref, dqg_ref, dkg_ref):
        @pl.when(pl.program_id(0) == 0)
        def _():
            dqg_ref[...] = jnp.zeros_like(dqg_ref)
            dkg_ref[...] = jnp.zeros_like(dkg_ref)

        cosv, sinv = c_ref[...], s_ref[...]

        def back(t, dr, g):
            r = lax.rsqrt(jnp.mean(t * t, axis=-1, keepdims=True) + EPS)
            xhat = t * r
            dn = dr * cosv + pltpu.roll(dr * sinv, HEAD // 2, axis=1)
            dxh = dn * g
            dt = r * (dxh - xhat * jnp.mean(dxh * xhat, axis=-1, keepdims=True))
            return dt, jnp.sum(dn * xhat, axis=0, keepdims=True)

        gq = jnp.zeros((1, HEAD), F32)
        for h in range(NQ):
            sl = slice(h * HEAD, (h + 1) * HEAD)
            dt, gpart = back(zq_ref[:, sl], dq_ref[:, sl], qg_ref[...])
            dzq_ref[:, sl] = dt.astype(BF16)
            gq = gq + gpart
        dqg_ref[...] += gq
        gk = jnp.zeros((1, HEAD), F32)
        for h in range(NKV):
            sl = slice(h * HEAD, (h + 1) * HEAD)
            dt, gpart = back(zkv_ref[:, sl], dk_ref[:, sl], kg_ref[...])
            dzkv_ref[:, sl] = dt.astype(BF16)
            gk = gk + gpart
        dkg_ref[...] += gk
        dzkv_ref[:, KVW:] = dv_ref[...].astype(BF16)

    return pl.pallas_call(
        body, name=name, grid=(S // HEAD,),
        in_specs=[pl.BlockSpec((HEAD, AW), lambda i: (i, 0)),
                  pl.BlockSpec((HEAD, 2 * KVW), lambda i: (i, kv_blk)),
                  pl.BlockSpec((HEAD, AW), lambda i: (i, 0)),
                  pl.BlockSpec((HEAD, KVW), lambda i: (i + 1, 0)),
                  pl.BlockSpec((HEAD, KVW), lambda i: (i + 1, 0)),
                  _vec_spec(l, HEAD), _vec_spec(l, HEAD),
                  pl.BlockSpec((HEAD, HEAD), lambda i: (i, 0)),
                  pl.BlockSpec((HEAD, HEAD), lambda i: (i, 0))],
        out_specs=[pl.BlockSpec((HEAD, AW), lambda i: (i, 0)),
                   pl.BlockSpec((HEAD, 2 * KVW), lambda i: (i, 0)),
                   pl.BlockSpec((1, HEAD), lambda i: (0, 0)),
                   pl.BlockSpec((1, HEAD), lambda i: (0, 0))],
        out_shape=[jax.ShapeDtypeStruct((S, AW), BF16),
                   jax.ShapeDtypeStruct((S, 2 * KVW), BF16),
                   jax.ShapeDtypeStruct((1, HEAD), F32),
                   jax.ShapeDtypeStruct((1, HEAD), F32)],
        compiler_params=_params(("arbitrary",)),
    )(z, z, dq, dk_pad, dv_pad, qg3, kg3, cos2, sin2)


def _sgu_bwd(name, z, dmix, lng3, lnb3, ws_b, wst_b, bs_b, og3, l, cfg):
    S, GW, NG = cfg["S"], cfg["GW"], cfg["NG"]

    def body(gu_ref, gv_ref, dm_ref, lng_ref, lnb_ref, ws_ref, wst_ref, bs_ref, og_ref,
             dgu_ref, dgv_ref, dws_ref, dbs_ref, dlng_ref, dlnb_ref, dog_ref, sg_scr, f_scr, dvn_scr):
        @pl.when(pl.program_id(0) == 0)
        def _():
            dws_ref[...] = jnp.zeros_like(dws_ref)
            dbs_ref[...] = jnp.zeros_like(dbs_ref)
            dlng_ref[...] = jnp.zeros_like(dlng_ref)
            dlnb_ref[...] = jnp.zeros_like(dlnb_ref)
            dog_ref[...] = jnp.zeros_like(dog_ref)

        gu, gv = gu_ref[...], gv_ref[...]
        lng = lng_ref[...]
        u, xhat, rstd, vn = _sgu_forward_math(gu, gv, lng, lnb_ref[...])
        vnb = vn.astype(BF16)
        for h in range(NG):
            sl = slice(h * HEAD, (h + 1) * HEAD)
            f = jnp.dot(ws_ref[h], vnb[:, sl], preferred_element_type=F32) + bs_ref[h]
            f_scr[:, sl] = f
            sg_scr[:, sl] = u[:, sl] * f
        sg = sg_scr[...]
        dm = dm_ref[...]
        r = lax.rsqrt(jnp.mean(sg * sg, axis=-1, keepdims=True) + EPS)
        sghat = sg * r
        dmg = dm * og_ref[...]
        dsg = r * (dmg - sghat * jnp.mean(dmg * sghat, axis=-1, keepdims=True))
        dog_ref[...] += jnp.sum(dm * sghat, axis=0, keepdims=True)
        du = dsg * f_scr[...]
        df = dsg * u
        dfb = df.astype(BF16)
        for h in range(NG):
            sl = slice(h * HEAD, (h + 1) * HEAD)
            dvn_scr[:, sl] = jnp.dot(wst_ref[h], dfb[:, sl], preferred_element_type=F32)
            dws_ref[h] += lax.dot_general(dfb[:, sl], vnb[:, sl], (((1,), (1,)), ((), ())),
                                          preferred_element_type=F32)
            dbs_ref[h] += jnp.broadcast_to(jnp.sum(df[:, sl], axis=-1, keepdims=True), (HEAD, HEAD))
        dvn = dvn_scr[...]
        dlng_ref[...] += jnp.sum(dvn * xhat, axis=0, keepdims=True)
        dlnb_ref[...] += jnp.sum(dvn, axis=0, keepdims=True)
        dxh = dvn * lng
        dvv = rstd * ((dxh - jnp.mean(dxh, axis=-1, keepdims=True))
                      - xhat * jnp.mean(dxh * xhat, axis=-1, keepdims=True))
        dgu_ref[...] = (du * _gelu_grad(gu)).astype(BF16)
        dgv_ref[...] = (dvv * _gelu_grad(gv)).astype(BF16)

    vec = pl.BlockSpec((1, GW), lambda c: (0, 0))
    mat = pl.BlockSpec((NG, HEAD, HEAD), lambda c: (0, 0, 0))
    wsp = pl.BlockSpec((None, NG, HEAD, HEAD), lambda c: (l, 0, 0, 0))
    return pl.pallas_call(
        body, name=name, grid=(S // HEAD,),
        in_specs=[pl.BlockSpec((HEAD, GW), lambda c: (c, 1)),
                  pl.BlockSpec((HEAD, GW), lambda c: (c, 2)),
                  pl.BlockSpec((HEAD, GW), lambda c: (c, 1)),
                  _vec_spec(l, GW), _vec_spec(l, GW), wsp, wsp, wsp, _vec_spec(l, GW)],
        out_specs=[pl.BlockSpec((HEAD, GW), lambda c: (c, 0)), pl.BlockSpec((HEAD, GW), lambda c: (c, 0)),
                   mat, mat, vec, vec, vec],
        out_shape=[jax.ShapeDtypeStruct((S, GW), BF16), jax.ShapeDtypeStruct((S, GW), BF16),
                   jax.ShapeDtypeStruct((NG, HEAD, HEAD), F32), jax.ShapeDtypeStruct((NG, HEAD, HEAD), F32),
                   jax.ShapeDtypeStruct((1, GW), F32), jax.ShapeDtypeStruct((1, GW), F32),
                   jax.ShapeDtypeStruct((1, GW), F32)],
        scratch_shapes=[pltpu.VMEM((HEAD, GW), F32), pltpu.VMEM((HEAD, GW), F32), pltpu.VMEM((HEAD, GW), F32)],
        compiler_params=_params(("arbitrary",)),
    )(z, z, dmix, lng3, lnb3, ws_b, wst_b, bs_b, og3)


def _mesh_pos():
    x, y, c = lax.axis_index("x"), lax.axis_index("y"), lax.axis_index("c")
    return x, y, c


def _dev_index(p):
    return 4 * p[0] + 2 * p[1] + p[2]


def _handshake(peers):
    barrier = pltpu.get_barrier_semaphore()
    for p in peers:
        pl.semaphore_signal(barrier, inc=1, device_id=p, device_id_type=MESH)
    pl.semaphore_wait(barrier, len(peers))


def _gather_body(na, slabs):
    def body(*refs):
        ins, outs = refs[:na], refs[na:2 * na]
        send_sems, recv_sems, local_sems = refs[2 * na:]
        x, y, c = _mesh_pos()
        me, sib = (x, y, c), (x, y, 1 - c)
        chips = [(1 - x, y), (x, 1 - y), (1 - x, 1 - y)]
        _handshake([sib] + [(*chip, c) for chip in chips])

        def copy(a, k, block, to, src=None):
            dst = slabs[a](outs[a], _dev_index(block))
            return pltpu.make_async_remote_copy(
                src_ref=dst if src is None else src, dst_ref=dst,
                send_sem=send_sems.at[7 * a + k], recv_sem=recv_sems.at[7 * a + k],
                device_id=to, device_id_type=MESH)

        local = [pltpu.make_async_copy(ins[a], slabs[a](outs[a], _dev_index(me)), local_sems.at[a])
                 for a in range(na)]
        for cp in local:
            cp.start()
        first = []
        for a in range(na):
            first.append(copy(a, 0, me, sib, src=ins[a]))
            first += [copy(a, 1 + j, me, (*chip, c), src=ins[a]) for j, chip in enumerate(chips)]
        for cp in first:
            cp.start()
        passed = []
        for j, chip in enumerate(chips):
            for a in range(na):
                copy(a, 1 + j, (*chip, c), me).wait_recv()
                fwd = copy(a, 4 + j, (*chip, c), sib)
                fwd.start()
                passed.append(fwd)
        for a in range(na):
            copy(a, 0, sib, me).wait_recv()
            for j, chip in enumerate(chips):
                copy(a, 4 + j, (*chip, 1 - c), me).wait_recv()
        for cp in first + passed:
            cp.wait_send()
        for cp in local:
            cp.wait()

    return body


def _all_gather(name, cid, shards, out_shapes, slabs):
    na = len(shards)
    return pl.kernel(
        _gather_body(na, slabs), out_type=out_shapes,
        mesh=plsc.ScalarSubcoreMesh(axis_name="seq", num_cores=1), name=name,
        scratch_types=[pltpu.SemaphoreType.DMA((7 * na,)), pltpu.SemaphoreType.DMA((7 * na,)),
                       pltpu.SemaphoreType.DMA((na,))],
        compiler_params=pltpu.CompilerParams(collective_id=cid),
    )(*shards)


def _d2d_body(nb, ns):
    n = nb + ns

    def body(*refs):
        ins, outs = refs[:n], refs[n:2 * n]
        send_sems, recv_sems = refs[2 * n:]
        x, y, c = _mesh_pos()
        sib = (x, y, 1 - c)
        _handshake([sib])
        copies = []
        for t in range(n):
            cp = pltpu.make_async_remote_copy(
                src_ref=ins[t].at[:, 1 - c] if t < nb else ins[t], dst_ref=outs[t],
                send_sem=send_sems.at[t], recv_sem=recv_sems.at[t],
                device_id=sib, device_id_type=MESH)
            cp.start()
            copies.append(cp)
        for cp in copies:
            cp.wait()

    return body


def _rs_d2d(name, cid, bigs, smalls):
    shapes = [jax.ShapeDtypeStruct((g.shape[0],) + g.shape[2:], g.dtype) for g in bigs]
    shapes += [jax.ShapeDtypeStruct(s.shape, s.dtype) for s in smalls]
    n = len(shapes)
    return pl.kernel(
        _d2d_body(len(bigs), len(smalls)), out_type=shapes,
        mesh=plsc.ScalarSubcoreMesh(axis_name="seq", num_cores=1), name=name,
        scratch_types=[pltpu.SemaphoreType.DMA((n,)), pltpu.SemaphoreType.DMA((n,))],
        compiler_params=pltpu.CompilerParams(collective_id=cid),
    )(*bigs, *smalls)


def _pair_sum(name, g4, recv, cvec, row_mult):
    Q, _, R, C = g4.shape
    tr = _tile(R, 256, row_mult)

    def body(c_ref, a_ref, b_ref, o_ref):
        o_ref[...] = (a_ref[...].astype(F32) + b_ref[...].astype(F32)).astype(BF16)

    grid_spec = pltpu.PrefetchScalarGridSpec(
        num_scalar_prefetch=1, grid=(Q, R // tr),
        in_specs=[pl.BlockSpec((None, None, tr, C), lambda q, i, c_ref: (q, c_ref[0], i, 0)),
                  pl.BlockSpec((None, tr, C), lambda q, i, c_ref: (q, i, 0))],
        out_specs=pl.BlockSpec((None, tr, C), lambda q, i, c_ref: (q, i, 0)))
    return pl.pallas_call(
        body, name=name, grid_spec=grid_spec, out_shape=jax.ShapeDtypeStruct((Q, R, C), BF16),
        compiler_params=_params(("arbitrary", "arbitrary")),
    )(cvec, g4, recv)


def _add2(name, a, b):
    R, C = a.shape
    tr = _tile(R, 512, 8)

    def body(a_ref, b_ref, o_ref):
        o_ref[...] = a_ref[...] + b_ref[...]

    blk = pl.BlockSpec((tr, C), lambda i: (i, 0))
    return pl.pallas_call(body, name=name, grid=(R // tr,), in_specs=[blk, blk], out_specs=blk,
                          out_shape=jax.ShapeDtypeStruct((R, C), a.dtype),
                          compiler_params=_params(("arbitrary",)))(a, b)


def _ici_body(nb, ns):
    n = nb + ns

    def body(*refs):
        ins, outs = refs[:n], refs[n:2 * n]
        send_sems, recv_sems, local_sems = refs[2 * n:]
        x, y, c = _mesh_pos()
        q_me = 2 * x + y
        chips = [(x ^ (k >> 1), y ^ (k & 1)) for k in range(1, 4)]
        _handshake([(px, py, c) for px, py in chips])
        copies = []
        for t in range(n):
            src_of = (lambda q, t=t: ins[t].at[q]) if t < nb else (lambda q, t=t: ins[t])
            dst = outs[t].at[q_me]
            loc = pltpu.make_async_copy(src_of(q_me), dst, local_sems.at[t])
            loc.start()
            copies.append(loc)
            for k, (px, py) in enumerate(chips):
                cp = pltpu.make_async_remote_copy(
                    src_ref=src_of(2 * px + py), dst_ref=dst,
                    send_sem=send_sems.at[3 * t + k], recv_sem=recv_sems.at[3 * t + k],
                    device_id=(px, py, c), device_id_type=MESH)
                cp.start()
                copies.append(cp)
        for cp in copies:
            cp.wait()

    return body


def _rs_ici(name, cid, bigs, smalls):
    shapes = [jax.ShapeDtypeStruct(g.shape, g.dtype) for g in bigs]
    shapes += [jax.ShapeDtypeStruct((N_DEV // 2,) + s.shape, s.dtype) for s in smalls]
    n = len(shapes)
    return pl.kernel(
        _ici_body(len(bigs), len(smalls)), out_type=shapes,
        mesh=plsc.ScalarSubcoreMesh(axis_name="seq", num_cores=1), name=name,
        scratch_types=[pltpu.SemaphoreType.DMA((3 * n,)), pltpu.SemaphoreType.DMA((3 * n,)),
                       pltpu.SemaphoreType.DMA((n,))],
        compiler_params=pltpu.CompilerParams(collective_id=cid),
    )(*bigs, *smalls)


def _adamw_math(w, g, m, v):
    m2 = ADAM_B1 * m + (1.0 - ADAM_B1) * g
    v2 = ADAM_B2 * v + (1.0 - ADAM_B2) * (g * g)
    m_hat = m2 / (1.0 - ADAM_B1 ** ADAM_STEP)
    v_hat = v2 / (1.0 - ADAM_B2 ** ADAM_STEP)
    delta = -ADAM_LR * (m_hat / (jnp.sqrt(v_hat) + ADAM_EPS) + ADAM_WD * w)
    return delta, m2, v2


def _adamw_sum(name, recv, w, m, v, l, prev, row_mult):
    NS, R, C = recv.shape
    L = w.shape[0]
    tr = _tile(R, 128, row_mult)

    def body(*refs):
        r_ref, w_ref, m_ref, v_ref = refs[:4]
        g_ref, d_ref, nm_ref, nv_ref = refs[-4:]
        g = r_ref[0].astype(F32)
        for s in range(1, NS):
            g = g + r_ref[s].astype(F32)
        d, m2, v2 = _adamw_math(w_ref[...], g, m_ref[...], v_ref[...])
        g_ref[...] = g
        d_ref[...] = d
        nm_ref[...] = m2
        nv_ref[...] = v2

    blk = pl.BlockSpec((None, tr, C), lambda i: (l, i, 0))
    shp = jax.ShapeDtypeStruct((L, R, C), F32)
    in_specs = [pl.BlockSpec((NS, tr, C), lambda i: (0, i, 0)), blk, blk, blk]
    args = [recv, w, m, v]
    aliases = {}
    if prev is not None:
        in_specs += [pl.BlockSpec(memory_space=pl.ANY)] * 4
        args += list(prev)
        aliases = {4 + i: i for i in range(4)}
    return pl.pallas_call(
        body, name=name, grid=(R // tr,),
        in_specs=in_specs, out_specs=[blk, blk, blk, blk], out_shape=[shp, shp, shp, shp],
        input_output_aliases=aliases,
        compiler_params=_params(("arbitrary",)),
    )(*args)


def _adamw_plain(name, g, w, m, v):
    def body(g_ref, w_ref, m_ref, v_ref, d_ref, nm_ref, nv_ref):
        d, m2, v2 = _adamw_math(w_ref[...], g_ref[...], m_ref[...], v_ref[...])
        d_ref[...] = d
        nm_ref[...] = m2
        nv_ref[...] = v2

    shp = jax.ShapeDtypeStruct(g.shape, F32)
    return pl.pallas_call(body, name=name, out_shape=[shp, shp, shp], compiler_params=_params())(g, w, m, v)


SMALL = ["norm1_g", "q_norm_g", "k_norm_g", "sink", "sgu_ln_g", "sgu_ln_b", "w_s", "b_s",
         "attn_out_g", "sgu_out_g", "norm2_g", "conv_b"]
PACK_ALIGN = 1024


def _pack(pieces):
    flat = []
    for p in pieces:
        f = p.reshape(-1).astype(F32)
        pad = (-f.shape[0]) % PACK_ALIGN
        flat.append(jnp.pad(f, (0, pad)) if pad else f)
    return jnp.concatenate(flat).reshape(-1, 128)


def _unpack(packed, shapes):
    flat = packed.reshape(-1)
    out, off = [], 0
    for shp in shapes:
        n = 1
        for d in shp:
            n *= d
        out.append(flat[off:off + n].reshape(shp))
        off += n + ((-n) % PACK_ALIGN)
    return out


def kernel(x, norm1_g, w_in, q_norm_g, k_norm_g, sink, sgu_ln_g, sgu_ln_b, w_s, b_s, attn_out_g, sgu_out_g, w_o, norm2_g, w_up, conv_w, conv_b, w_down, loss_target, m_norm1_g, m_w_in, m_q_norm_g, m_k_norm_g, m_sink, m_sgu_ln_g, m_sgu_ln_b, m_w_s, m_b_s, m_attn_out_g, m_sgu_out_g, m_w_o, m_norm2_g, m_w_up, m_conv_w, m_conv_b, m_w_down, v_norm1_g, v_w_in, v_q_norm_g, v_k_norm_g, v_sink, v_sgu_ln_g, v_sgu_ln_b, v_w_s, v_b_s, v_attn_out_g, v_sgu_out_g, v_w_o, v_norm2_g, v_w_up, v_conv_w, v_conv_b, v_w_down):
    weights = dict(norm1_g=norm1_g, w_in=w_in, q_norm_g=q_norm_g, k_norm_g=k_norm_g, sink=sink, sgu_ln_g=sgu_ln_g,
                   sgu_ln_b=sgu_ln_b, w_s=w_s, b_s=b_s, attn_out_g=attn_out_g, sgu_out_g=sgu_out_g, w_o=w_o,
                   norm2_g=norm2_g, w_up=w_up, conv_w=conv_w, conv_b=conv_b, w_down=w_down)
    mom_m = dict(norm1_g=m_norm1_g, w_in=m_w_in, q_norm_g=m_q_norm_g, k_norm_g=m_k_norm_g, sink=m_sink,
                 sgu_ln_g=m_sgu_ln_g, sgu_ln_b=m_sgu_ln_b, w_s=m_w_s, b_s=m_b_s, attn_out_g=m_attn_out_g,
                 sgu_out_g=m_sgu_out_g, w_o=m_w_o, norm2_g=m_norm2_g, w_up=m_w_up, conv_w=m_conv_w,
                 conv_b=m_conv_b, w_down=m_w_down)
    mom_v = dict(norm1_g=v_norm1_g, w_in=v_w_in, q_norm_g=v_q_norm_g, k_norm_g=v_k_norm_g, sink=v_sink,
                 sgu_ln_g=v_sgu_ln_g, sgu_ln_b=v_sgu_ln_b, w_s=v_w_s, b_s=v_b_s, attn_out_g=v_attn_out_g,
                 sgu_out_g=v_sgu_out_g, w_o=v_w_o, norm2_g=v_norm2_g, w_up=v_w_up, conv_w=v_conv_w,
                 conv_b=v_conv_b, w_down=v_w_down)
    order = ["norm1_g", "w_in", "q_norm_g", "k_norm_g", "sink", "sgu_ln_g", "sgu_ln_b", "w_s", "b_s",
             "attn_out_g", "sgu_out_g", "w_o", "norm2_g", "w_up", "conv_w", "conv_b", "w_down"]

    _, S, D = x.shape
    L = w_in.shape[0]
    AW = D // 2
    NQ = AW // HEAD
    NKV = max(1, NQ // 4)
    G = NQ // NKV
    KVW = NKV * HEAD
    GW = D - AW
    NG = GW // HEAD
    IN = AW + 2 * KVW + 2 * GW
    INS = w_in.shape[2]
    OS = w_o.shape[1]
    US = w_up.shape[2]
    DS = w_down.shape[1]
    F2 = US * N_DEV
    F = F2 // 2
    assert INS * N_DEV == IN and OS * N_DEV == D and DS * N_DEV == F and AW == GW and (3 * AW) % (2 * KVW) == 0
    cfg = dict(S=S, D=D, AW=AW, NQ=NQ, NKV=NKV, G=G, KVW=KVW, GW=GW, NG=NG, F=F, F2=F2)

    def rows_slab(n):
        return lambda ref, idx: ref.at[:, pl.ds(pl.multiple_of(idx * n, n), n), :]

    def cols_slab(n):
        return lambda ref, idx: ref.at[:, :, pl.ds(pl.multiple_of(idx * n, n), n)]

    def lead_slab(ref, idx):
        return ref.at[idx]

    def after(first, then):
        return lax.optimization_barrier((first, then))

    ids = iter(range(64))
    w_in_sh, w_in_p, w_o_f, w_up_f, w_down_f = [], [], [], [], []

    def gather_layer(l, not_before=None):
        in_shard = w_in[l:l + 1].astype(BF16)
        if not_before is not None:
            _, in_shard = after(not_before, in_shard)
        w_in_sh.extend(_all_gather("ag_in", next(ids), [in_shard],
                                   [jax.ShapeDtypeStruct((N_DEV, 1, D, INS), BF16)], [lead_slab]))
        w_o_f.extend(_all_gather("ag_o", next(ids), [w_o[l:l + 1].astype(BF16)],
                                 [jax.ShapeDtypeStruct((1, D, D), BF16)], [rows_slab(OS)]))
        w_up_f.extend(_all_gather("ag_up", next(ids), [w_up[l:l + 1].astype(BF16)],
                                  [jax.ShapeDtypeStruct((1, D, F2), BF16)], [cols_slab(US)]))
        w_down_f.extend(_all_gather("ag_down", next(ids), [w_down[l:l + 1].astype(BF16)],
                                    [jax.ShapeDtypeStruct((1, F, D), BF16)], [rows_slab(DS)]))

    def relayout_in(l, not_before=None):
        sh = w_in_sh[l]
        if not_before is not None:
            _, sh = after(not_before, sh)
        full = jnp.transpose(sh, (1, 2, 0, 3)).reshape(1, D, IN)
        w_in_p.append(jnp.concatenate([full[:, :, :AW], full[:, :, AW + 2 * KVW:], full[:, :, AW:AW + 2 * KVW]], axis=2))

    gather_layer(0)
    relayout_in(0)
    (conv_w_f,) = _all_gather("ag_conv_w", next(ids), [conv_w], [jax.ShapeDtypeStruct((L, 3, F2), F32)],
                              [cols_slab(US)])

    n1g3, n2g3 = norm1_g.reshape(L, 1, D), norm2_g.reshape(L, 1, D)
    qg3, kg3 = q_norm_g.reshape(L, 1, HEAD), k_norm_g.reshape(L, 1, HEAD)
    lng3, lnb3 = sgu_ln_g.reshape(L, 1, GW), sgu_ln_b.reshape(L, 1, GW)
    ag3, og3 = attn_out_g.reshape(L, 1, AW), sgu_out_g.reshape(L, 1, GW)
    cb3 = conv_b.reshape(L, 1, F2)
    ws_b = w_s.astype(BF16)
    wst_b = jnp.swapaxes(w_s, 2, 3).astype(BF16)
    bs_b = jnp.broadcast_to(b_s[..., None], (L, NG, HEAD, HEAD))
    inv_freq = ROPE_THETA ** (-jnp.arange(0, HEAD, 2, dtype=F32) / HEAD)
    ang = jnp.arange(S, dtype=F32)[:, None] * inv_freq[None, :]
    cos2 = jnp.concatenate([jnp.cos(ang), jnp.cos(ang)], axis=1)
    sin2 = jnp.concatenate([-jnp.sin(ang), jnp.sin(ang)], axis=1)

    tn = 512
    t_in, t_d, t_f, t_f2 = _tile(IN, tn, 128), _tile(D, tn, 128), _tile(F, tn, 128), _tile(F2, tn, 128)
    tm_f = _tile(F, 512, 128)
    tm_s = _tile(S, 1024, 128)

    def w_spec(tk, tn_):
        return pl.BlockSpec((None, tk, tn_), lambda i, j, k: (0, k, j))

    def wt_spec(tn_, tk):
        return pl.BlockSpec((None, tn_, tk), lambda i, j, k: (0, j, k))

    def a_spec(tm, tk):
        return pl.BlockSpec((tm, tk), lambda i, j, k: (i, k))

    def at_spec(tk, tm):
        return pl.BlockSpec((tk, tm), lambda i, j, k: (k, i))

    def b_spec(tk, tn_):
        return pl.BlockSpec((tk, tn_), lambda i, j, k: (k, j))

    xs = x.reshape(S, D)
    saved = []
    cur = xs
    for l in range(L):
        h = _rms_fwd("rms1_fwd", cur, n1g3, l)
        z = _mm("mm_in", h, w_in_p[l], M=S, N=IN, K=D, tm=S, tn=t_in, tk=D,
                a_spec=a_spec(S, D), b_spec=w_spec(D, t_in))
        q_r, k_r, v_b = _qkv_prep("qkv_prep", z, qg3, kg3, cos2, sin2, l, cfg)
        attn, mix_l = _attn_fwd("attn_fwd", q_r, k_r, v_b, sink, ag3, l, cfg)
        mix_r = _sgu_fwd("sgu_fwd", z, lng3, lnb3, ws_b, bs_b, og3, l, cfg)
        mixed = jnp.concatenate([mix_l, mix_r], axis=1)
        x1 = _mm("mm_o", mixed, w_o_f[l], M=S, N=D, K=D, tm=S, tn=t_d, tk=D,
                 a_spec=a_spec(S, D), b_spec=w_spec(D, t_d), res=cur)
        if l + 1 < L:
            gather_layer(l + 1, not_before=x1)
        h2 = _rms_fwd("rms2_fwd", x1, n2g3, l)
        ap = _mm("mm_up", h2, w_up_f[l], M=S, N=F2, K=D, tm=S, tn=t_f2, tk=D,
                 a_spec=a_spec(S, D), b_spec=w_spec(D, t_f2))
        y_b = _conv_glu_fwd("conv_glu_fwd", ap, conv_w_f, cb3, l, cfg)
        if l + 1 < L:
            relayout_in(l + 1, not_before=y_b)
        x2 = _mm("mm_down", y_b, w_down_f[l], M=S, N=D, K=F, tm=tm_s, tn=t_d, tk=F,
                 a_spec=a_spec(tm_s, F), b_spec=w_spec(F, t_d), res=x1)
        saved.append(dict(x=cur, h=h, z=z, q=q_r, k=k_r, v=v_b, attn=attn, mixed=mixed, x1=x1, h2=h2, ap=ap, y=y_b))
        cur = x2

    loss_tile, dx, dxb = _loss_bwd("loss", cur, loss_target.reshape(S, D))
    loss = lax.psum(loss_tile[0, 0], ("x", "y", "c"))

    gS = [dict() for _ in range(L)]
    recv = [dict() for _ in range(L)]
    cvec = jnp.reshape(lax.axis_index("c"), (1,)).astype(jnp.int32)

    def rs_swap(name, g, R, C, small=None):
        g4 = g.reshape(N_DEV // 2, 2, R, C)
        got = _rs_d2d("rs_d2d_" + name, next(ids), [g4], [] if small is None else [small])
        return name, g4, got, small

    in_flight = []
    IN_FLIGHT_MAX = 2

    def rs_finish(l, pending, then=None):
        name, g4, got, small = pending
        chip = _pair_sum("pair_sum_" + name, g4, got[0], cvec, 16)
        if then is not None:
            chip, then = after(chip, then)
            while len(in_flight) >= IN_FLIGHT_MAX:
                ll, nn, outs = in_flight.pop(0)
                outs, then = after(outs, then)
                recv[ll][nn] = outs
        chip_small = [] if small is None else [_add2("pair_sum_small", small, got[1])]
        in_flight.append((l, name, _rs_ici("rs_ici_" + name, next(ids), [chip], chip_small)))
        return then

    carried = None
    for l in reversed(range(L)):
        sv = saved[l]
        dy = _mm("mm_dy", dxb, w_down_f[l], M=S, N=F, K=D, tm=S, tn=t_f, tk=D, tb=True,
                 a_spec=a_spec(S, D), b_spec=wt_spec(t_f, D))
        if carried is not None:
            dy = rs_finish(l + 1, carried, dy)
        g_down = _mm("mm_gdown", sv["y"], dxb, M=F, N=D, K=S, tm=tm_f, tn=D, tk=S, ta=True,
                     a_spec=at_spec(S, tm_f), b_spec=b_spec(S, D), out_dtype=BF16)
        g_down, dy = after(g_down, dy)
        swap_down = rs_swap("w_down", g_down, DS, D)
        dap3, dcw, dcb = _glu_conv_bwd("glu_conv_bwd", dy, sv["ap"], conv_w_f, cb3, l, cfg)
        gS[l]["conv_w"] = jnp.concatenate([dcw[0], dcw[1]], axis=1)
        gS[l]["conv_b"] = jnp.concatenate([dcb[0], dcb[1]], axis=1).reshape(F2)
        dap3 = rs_finish(l, swap_down, dap3)
        dh2 = _mm("mm_dh2", dap3, w_up_f[l], M=S, N=D, K=F2, tm=tm_s, tn=t_d, tk=F, tb=True,
                  a_spec=pl.BlockSpec((None, tm_s, F), lambda i, j, k: (k, i, 0)),
                  b_spec=wt_spec(t_d, F))
        g_up = _mm("mm_gup", sv["h2"], dap3, M=D, N=F2, K=S, tm=D, tn=US, tk=S, ta=True,
                   a_spec=at_spec(S, D),
                   b_spec=pl.BlockSpec((None, S, US), lambda i, j, k: (j // (N_DEV // 2), 0, j % (N_DEV // 2))),
                   out_dtype=BF16, out_shape=(N_DEV, D, US),
                   out_spec=pl.BlockSpec((None, D, US), lambda i, j, k: (j, 0, 0)))
        g_up, dh2 = after(g_up, dh2)
        swap_up = rs_swap("w_up", g_up, D, US)
        dx1, dx1b, dg2 = _rms_bwd("rms2_bwd", sv["x1"], dh2, dx, n2g3, l)
        gS[l]["norm2_g"] = dg2.reshape(D)
        dmix = _mm("mm_dmix", dx1b, w_o_f[l], M=S, N=D, K=D, tm=S, tn=t_d, tk=D, tb=True,
                   a_spec=a_spec(S, D), b_spec=wt_spec(t_d, D))
        g_o = _mm("mm_go", sv["mixed"], dx1b, M=D, N=D, K=S, tm=D, tn=t_d, tk=S, ta=True,
                  a_spec=at_spec(S, D), b_spec=b_spec(S, t_d), out_dtype=BF16)
        g_o, dmix = after(g_o, dmix)
        swap_o = rs_swap("w_o", g_o, OS, D)
        dmix = rs_finish(l, swap_up, dmix)
        dq_r, dk_pad, dv_pad, dsink, dag = _attn_bwd("attn_bwd", sv["q"], sv["k"], sv["v"], sv["attn"], dmix,
                                                     sink, ag3, l, cfg)
        dq_r = rs_finish(l, swap_o, dq_r)
        gS[l]["sink"] = dsink[:, 0]
        gS[l]["attn_out_g"] = dag.reshape(AW)
        dzgu, dzgv, dws, dbs, dlng, dlnb, dog = _sgu_bwd("sgu_bwd", sv["z"], dmix, lng3, lnb3, ws_b, wst_b, bs_b,
                                                        og3, l, cfg)
        gS[l]["w_s"] = dws
        gS[l]["b_s"] = dbs[:, :, 0]
        gS[l]["sgu_ln_g"] = dlng.reshape(GW)
        gS[l]["sgu_ln_b"] = dlnb.reshape(GW)
        gS[l]["sgu_out_g"] = dog.reshape(GW)
        dzq, dzkv, dqg, dkg = _qkv_prep_bwd("qkv_prep_bwd", sv["z"], dq_r, dk_pad, dv_pad, qg3, kg3, cos2, sin2,
                                            l, cfg)
        gS[l]["q_norm_g"] = dqg.reshape(HEAD)
        gS[l]["k_norm_g"] = dkg.reshape(HEAD)
        dz = jnp.concatenate([dzq, dzgu, dzgv, dzkv], axis=1)
        dh = _mm("mm_dh", dz, w_in_p[l], M=S, N=D, K=IN, tm=tm_s, tn=t_d, tk=IN, tb=True,
                 a_spec=a_spec(tm_s, IN), b_spec=wt_spec(t_d, IN))
        gin_p = _mm("mm_gin", sv["h"], dz, M=D, N=IN, K=S, tm=D, tn=t_in, tk=S, ta=True,
                    a_spec=at_spec(S, D), b_spec=b_spec(S, t_in), out_dtype=BF16)
        gin_p, dh = after(gin_p, dh)
        gin = jnp.concatenate([gin_p[:, :AW], gin_p[:, AW + 2 * GW:], gin_p[:, AW:AW + 2 * GW]], axis=1)
        g_in = jnp.transpose(gin.reshape(D, N_DEV, INS), (1, 0, 2))
        dx, dxb, dg1 = _rms_bwd("rms1_bwd", sv["x"], dh, dx1, n1g3, l)
        gS[l]["norm1_g"] = dg1.reshape(D)
        carried = rs_swap("w_in", g_in, D, INS, small=_pack([gS[l][n] for n in SMALL] + [gS[l]["conv_w"]]))
    rs_finish(0, carried)
    for ll, nn, outs in in_flight:
        recv[ll][nn] = outs
    grad_x = dx.reshape(1, S, D)

    grads, deltas, new_m, new_v = {}, {}, {}, {}
    for name in ["w_down", "w_up", "w_o", "w_in"]:
        res = None
        for l in reversed(range(L)):
            res = _adamw_sum("adamw_" + name, recv[l][name][0], weights[name], mom_m[name], mom_v[name], l, res, 16)
        grads[name], deltas[name], new_m[name], new_v[name] = res

    small_shapes = [weights[n].shape[1:] for n in SMALL] + [(3, F2)]
    SR = recv[0]["w_in"][1].shape[1]
    tr_s = _tile(SR, 512, 8)
    sblk = pl.BlockSpec((tr_s, 128), lambda i: (i, 0))
    sshp = jax.ShapeDtypeStruct((SR, 128), F32)
    small_parts = [[None] * L for _ in range(4)]
    for l in reversed(range(L)):
        def small_body(r_ref, w_ref, m_ref, v_ref, g_ref, d_ref, nm_ref, nv_ref):
            g = r_ref[0]
            for s in range(1, N_DEV // 2):
                g = g + r_ref[s]
            d, m2, v2 = _adamw_math(w_ref[...], g, m_ref[...], v_ref[...])
            g_ref[...] = g
            d_ref[...] = d
            nm_ref[...] = m2
            nv_ref[...] = v2

        def pack_small(src):
            return _pack([src[n][l] for n in SMALL] + [jnp.zeros((3, F2), F32)])

        packed = pl.pallas_call(
            small_body, name="adamw_small", grid=(SR // tr_s,),
            in_specs=[pl.BlockSpec((N_DEV // 2, tr_s, 128), lambda i: (0, i, 0)), sblk, sblk, sblk],
            out_specs=[sblk, sblk, sblk, sblk], out_shape=[sshp, sshp, sshp, sshp],
            compiler_params=_params(("arbitrary",)),
        )(recv[l]["w_in"][1], pack_small(weights), pack_small(mom_m), pack_small(mom_v))
        for t in range(4):
            small_parts[t][l] = _unpack(packed[t], small_shapes)
    for t, store in enumerate([grads, deltas, new_m, new_v]):
        for i, n in enumerate(SMALL):
            store[n] = jnp.stack([small_parts[t][l][i] for l in range(L)])
    conv_full = jnp.stack([small_parts[0][l][len(SMALL)] for l in range(L)])
    me_i = _dev_index(_mesh_pos())
    g_cw = lax.dynamic_slice_in_dim(conv_full, me_i * US, US, axis=2)
    grads["conv_w"] = g_cw
    d_cw, m_cw, v_cw = _adamw_plain("adamw_conv_w", g_cw.reshape(L * 3, US), conv_w.reshape(L * 3, US),
                                    m_conv_w.reshape(L * 3, US), v_conv_w.reshape(L * 3, US))
    deltas["conv_w"] = d_cw.reshape(L, 3, US)
    new_m["conv_w"] = m_cw.reshape(L, 3, US)
    new_v["conv_w"] = v_cw.reshape(L, 3, US)

    return (loss, grad_x, *[grads[n] for n in order], *[deltas[n] for n in order],
            *[new_m[n] for n in order], *[new_v[n] for n in order])
```

```python
import jax
import jax.numpy as jnp
from jax import lax
from jax.experimental import pallas as pl
from jax.experimental.pallas import tpu as pltpu
from jax.experimental.pallas import tpu_sc as plsc

F32 = jnp.float32
BF16 = jnp.bfloat16
MESH = pl.DeviceIdType.MESH

N_DEV = 8
HEAD = 128
EPS = 1e-6
MASK_VALUE = -1e30
ROPE_THETA = 10000.0
GELU_C = 0.7978845608028654
GELU_A = 0.044715

ADAM_LR = 0.001
ADAM_B1 = 0.9
ADAM_B2 = 0.999
ADAM_EPS = 1e-08
ADAM_WD = 0.01
ADAM_STEP = 10

VMEM_LIMIT = 56 * 1024 * 1024


def _tile(n, pref, mult):
    best = None
    for t in range(mult, min(n, pref) + 1, mult):
        if n % t == 0:
            best = t
    return n if best is None else best


def _params(sem=None):
    kw = dict(vmem_limit_bytes=VMEM_LIMIT)
    if sem is not None:
        kw["dimension_semantics"] = sem
    return pltpu.CompilerParams(**kw)


def _gelu(x):
    return x * (0.5 * (1.0 + jnp.tanh(GELU_C * (x + GELU_A * (x * x * x)))))


def _gelu_grad(x):
    t = jnp.tanh(GELU_C * (x + GELU_A * (x * x * x)))
    return 0.5 * (1.0 + t) + 0.5 * x * (1.0 - t * t) * (GELU_C * (1.0 + 3.0 * GELU_A * (x * x)))


def _sigmoid(x):
    return 1.0 / (1.0 + jnp.exp(-x))


def _vec_spec(l, n):
    return pl.BlockSpec((None, 1, n), lambda *_: (l, 0, 0))


def _mm(name, a, b, *, M, N, K, tm, tn, tk, a_spec, b_spec, ta=False, tb=False, out_dtype=F32, res=None,
        out_shape=None, out_spec=None):
    nm, nn, nk = M // tm, N // tn, K // tk
    assert nm * tm == M and nn * tn == N and nk * tk == K
    assert not (ta and nk > 1)
    dims = (((1,), (1,)), ((), ())) if tb else (((1,), (0,)), ((), ()))

    def body(*refs):
        refs = list(refs)
        a_ref = refs.pop(0)
        b_ref = refs.pop(0)
        r_ref = refs.pop(0) if res is not None else None
        o_ref = refs.pop(0)
        acc = refs.pop(0) if nk > 1 else None
        at = refs.pop(0) if ta else None
        k = pl.program_id(2)
        if ta:
            @pl.when(pl.program_id(1) == 0)
            def _():
                at[...] = a_ref[...].T
            lhs = at[...]
        else:
            lhs = a_ref[...]
        p = lax.dot_general(lhs, b_ref[...], dims, preferred_element_type=F32)

        def finish(r):
            if r_ref is not None:
                r = r_ref[...] + r
            o_ref[...] = r.astype(out_dtype)

        if nk == 1:
            finish(p)
        else:
            @pl.when(k == 0)
            def _():
                acc[...] = p

            @pl.when(k > 0)
            def _():
                acc[...] += p

            @pl.when(k == nk - 1)
            def _():
                finish(acc[...])

    in_specs = [a_spec, b_spec]
    args = [a, b]
    if res is not None:
        in_specs.append(pl.BlockSpec((tm, tn), lambda i, j, k: (i, j)))
        args.append(res)
    scratch = []
    if nk > 1:
        scratch.append(pltpu.VMEM((tm, tn), F32))
    if ta:
        scratch.append(pltpu.VMEM((tm, tk), BF16))
    return pl.pallas_call(
        body, name=name, grid=(nm, nn, nk),
        in_specs=in_specs,
        out_specs=pl.BlockSpec((tm, tn), lambda i, j, k: (i, j)) if out_spec is None else out_spec,
        out_shape=jax.ShapeDtypeStruct((M, N) if out_shape is None else out_shape, out_dtype),
        scratch_shapes=scratch,
        compiler_params=_params(("arbitrary", "arbitrary", "arbitrary")),
    )(*args)


def _rms_fwd(name, x, g3, l):
    S, D = x.shape
    tr = _tile(S, 256, 16)

    def body(x_ref, g_ref, h_ref):
        xv = x_ref[...]
        r = lax.rsqrt(jnp.mean(xv * xv, axis=-1, keepdims=True) + EPS)
        h_ref[...] = ((xv * r) * g_ref[...]).astype(BF16)

    return pl.pallas_call(
        body, name=name, grid=(S // tr,),
        in_specs=[pl.BlockSpec((tr, D), lambda i: (i, 0)), _vec_spec(l, D)],
        out_specs=pl.BlockSpec((tr, D), lambda i: (i, 0)),
        out_shape=jax.ShapeDtypeStruct((S, D), BF16),
        compiler_params=_params(("arbitrary",)),
    )(x, g3)


def _rope(t, cos2, sin2):
    return t * cos2 + pltpu.roll(t, HEAD // 2, axis=1) * sin2


def _qkv_prep(name, z, qg3, kg3, cos2, sin2, l, cfg):
    S, AW, KVW, NQ, NKV = cfg["S"], cfg["AW"], cfg["KVW"], cfg["NQ"], cfg["NKV"]
    tr = _tile(S, 256, 16)
    kv_blk = (3 * AW) // (2 * KVW)

    def body(zq_ref, zkv_ref, qg_ref, kg_ref, c_ref, s_ref, q_ref, k_ref, v_ref):
        cosv, sinv = c_ref[...], s_ref[...]

        def norm_rope(t, g):
            r = lax.rsqrt(jnp.mean(t * t, axis=-1, keepdims=True) + EPS)
            return _rope((t * r) * g, cosv, sinv)

        for h in range(NQ):
            sl = slice(h * HEAD, (h + 1) * HEAD)
            q_ref[:, sl] = norm_rope(zq_ref[:, sl], qg_ref[...]).astype(BF16)
        for h in range(NKV):
            sl = slice(h * HEAD, (h + 1) * HEAD)
            k_ref[:, sl] = norm_rope(zkv_ref[:, sl], kg_ref[...]).astype(BF16)
        v_ref[...] = zkv_ref[:, KVW:].astype(BF16)

    return pl.pallas_call(
        body, name=name, grid=(S // tr,),
        in_specs=[pl.BlockSpec((tr, AW), lambda i: (i, 0)),
                  pl.BlockSpec((tr, 2 * KVW), lambda i: (i, kv_blk)),
                  _vec_spec(l, HEAD), _vec_spec(l, HEAD),
                  pl.BlockSpec((tr, HEAD), lambda i: (i, 0)),
                  pl.BlockSpec((tr, HEAD), lambda i: (i, 0))],
        out_specs=[pl.BlockSpec((tr, AW), lambda i: (i, 0)),
                   pl.BlockSpec((tr, KVW), lambda i: (i, 0)),
                   pl.BlockSpec((tr, KVW), lambda i: (i, 0))],
        out_shape=[jax.ShapeDtypeStruct((S, AW), BF16),
                   jax.ShapeDtypeStruct((S, KVW), BF16),
                   jax.ShapeDtypeStruct((S, KVW), BF16)],
        compiler_params=_params(("arbitrary",)),
    )(z, z, qg3, kg3, cos2, sin2)


def _band_specs(width, nb):
    return [pl.BlockSpec((HEAD, width), lambda n: (jnp.maximum(n - 1, 0), 0)),
            pl.BlockSpec((HEAD, width), lambda n: (n, 0)),
            pl.BlockSpec((HEAD, width), lambda n: (jnp.minimum(n + 1, nb - 1), 0))]


def _attn_probs(qs, kj, n, sink_of_row, S, G):
    s = lax.dot_general(qs, kj, (((1,), (1,)), ((), ())), preferred_element_type=F32) * (HEAD ** -0.5)
    rows = lax.broadcasted_iota(jnp.int32, (G * HEAD, 3 * HEAD), 0)
    cols = lax.broadcasted_iota(jnp.int32, (G * HEAD, 3 * HEAD), 1)
    qi = rows & (HEAD - 1)
    kpos = n * HEAD - HEAD + cols
    valid = (cols >= qi) & (cols <= qi + 2 * HEAD) & (kpos >= 0) & (kpos < S)
    s = jnp.where(valid, s, MASK_VALUE)
    m = jnp.maximum(jnp.max(s, axis=-1, keepdims=True), sink_of_row)
    p = jnp.exp(s - m)
    e_sink = jnp.exp(sink_of_row - m)
    inv = 1.0 / (jnp.sum(p, axis=-1, keepdims=True) + e_sink)
    return p * inv, e_sink * inv


def _sink_rows(sink_ref, l, j, G):
    hidx = lax.broadcasted_iota(jnp.int32, (G * HEAD, 1), 0) // HEAD
    col = jnp.full((G * HEAD, 1), sink_ref[l, j * G], F32)
    for g in range(1, G):
        col = jnp.where(hidx == g, sink_ref[l, j * G + g], col)
    return col


def _attn_fwd(name, q, k, v, sink, ag3, l, cfg):
    S, AW, KVW, NKV, G = cfg["S"], cfg["AW"], cfg["KVW"], cfg["NKV"], cfg["G"]
    nb = S // HEAD

    def body(q_ref, kp, kc, kn, vp, vc, vn, sink_ref, ag_ref, a_ref, mix_ref):
        n = pl.program_id(0)
        kb = jnp.concatenate([kp[...], kc[...], kn[...]], axis=0)
        vb = jnp.concatenate([vp[...], vc[...], vn[...]], axis=0)
        for j in range(NKV):
            sl = slice(j * HEAD, (j + 1) * HEAD)
            qs = jnp.concatenate([q_ref[:, (j * G + g) * HEAD:(j * G + g + 1) * HEAD] for g in range(G)], axis=0)
            probs, _ = _attn_probs(qs, kb[:, sl], n, _sink_rows(sink_ref, l, j, G), S, G)
            o = jnp.dot(probs.astype(BF16), vb[:, sl], preferred_element_type=F32)
            for g in range(G):
                a_ref[:, (j * G + g) * HEAD:(j * G + g + 1) * HEAD] = o[g * HEAD:(g + 1) * HEAD]
        a = a_ref[...]
        r = lax.rsqrt(jnp.mean(a * a, axis=-1, keepdims=True) + EPS)
        mix_ref[...] = ((a * r) * ag_ref[...]).astype(BF16)

    return pl.pallas_call(
        body, name=name, grid=(nb,),
        in_specs=[pl.BlockSpec((HEAD, AW), lambda n: (n, 0))] + _band_specs(KVW, nb) + _band_specs(KVW, nb)
                 + [pl.BlockSpec(memory_space=pltpu.SMEM), _vec_spec(l, AW)],
        out_specs=[pl.BlockSpec((HEAD, AW), lambda n: (n, 0)), pl.BlockSpec((HEAD, AW), lambda n: (n, 0))],
        out_shape=[jax.ShapeDtypeStruct((S, AW), F32), jax.ShapeDtypeStruct((S, AW), BF16)],
        compiler_params=_params(("arbitrary",)),
    )(q, k, k, k, v, v, v, sink, ag3)


def _sgu_forward_math(gu, gv, lng, lnb):
    u = _gelu(gu)
    vv = _gelu(gv)
    mu = jnp.mean(vv, axis=-1, keepdims=True)
    xc = vv - mu
    rstd = lax.rsqrt(jnp.mean(xc * xc, axis=-1, keepdims=True) + EPS)
    xhat = xc * rstd
    vn = xhat * lng + lnb
    return u, xhat, rstd, vn


def _sgu_fwd(name, z, lng3, lnb3, ws_b, bs_b, og3, l, cfg):
    S, GW, NG = cfg["S"], cfg["GW"], cfg["NG"]

    def body(gu_ref, gv_ref, lng_ref, lnb_ref, ws_ref, bs_ref, og_ref, mix_ref, sg_ref):
        u, _, _, vn = _sgu_forward_math(gu_ref[...], gv_ref[...], lng_ref[...], lnb_ref[...])
        vnb = vn.astype(BF16)
        for h in range(NG):
            sl = slice(h * HEAD, (h + 1) * HEAD)
            f = jnp.dot(ws_ref[h], vnb[:, sl], preferred_element_type=F32) + bs_ref[h]
            sg_ref[:, sl] = u[:, sl] * f
        sg = sg_ref[...]
        r = lax.rsqrt(jnp.mean(sg * sg, axis=-1, keepdims=True) + EPS)
        mix_ref[...] = ((sg * r) * og_ref[...]).astype(BF16)

    return pl.pallas_call(
        body, name=name, grid=(S // HEAD,),
        in_specs=[pl.BlockSpec((HEAD, GW), lambda c: (c, 1)),
                  pl.BlockSpec((HEAD, GW), lambda c: (c, 2)),
                  _vec_spec(l, GW), _vec_spec(l, GW),
                  pl.BlockSpec((None, NG, HEAD, HEAD), lambda c: (l, 0, 0, 0)),
                  pl.BlockSpec((None, NG, HEAD, HEAD), lambda c: (l, 0, 0, 0)),
                  _vec_spec(l, GW)],
        out_specs=pl.BlockSpec((HEAD, GW), lambda c: (c, 0)),
        out_shape=jax.ShapeDtypeStruct((S, GW), BF16),
        scratch_shapes=[pltpu.VMEM((HEAD, GW), F32)],
        compiler_params=_params(("arbitrary",)),
    )(z, z, lng3, lnb3, ws_b, bs_b, og3)


CONV_HALO = 8
CONV_ROWS = 64


def _for_row_windows(S, fn):
    R, W = CONV_ROWS, CONV_ROWS + 2 * CONV_HALO
    n = S // R
    assert n * R == S and n >= 2
    fn(0, 0, 0)
    if n > 2:
        def mid(k, carry):
            fn(pl.multiple_of(k * R - CONV_HALO, CONV_HALO), CONV_HALO, pl.multiple_of(k * R, R))
            return carry

        lax.fori_loop(1, n - 1, mid, 0)
    fn(S - W, 2 * CONV_HALO, S - R)


def _shifts(t):
    W = t.shape[0]
    row = lax.broadcasted_iota(jnp.int32, t.shape, 0)
    dn = jnp.where(row == 0, 0.0, pltpu.roll(t, 1, axis=0))
    up = jnp.where(row == W - 1, 0.0, pltpu.roll(t, W - 1, axis=0))
    return dn, up


def _conv3(t, w, b):
    dn, up = _shifts(t)
    return ((b + dn * w[0:1]) + t * w[1:2]) + up * w[2:3], dn, up


def _conv_glu_fwd(name, ap, cw, cb3, l, cfg):
    S, F = cfg["S"], cfg["F"]
    tc = _tile(F, 256, 128)
    nf = F // tc
    R, W = CONV_ROWS, CONV_ROWS + 2 * CONV_HALO

    def body(g_ref, u_ref, wg_ref, wu_ref, bg_ref, bu_ref, y_ref):
        wg, wu, bg, bu = wg_ref[...], wu_ref[...], bg_ref[...], bu_ref[...]

        def window(start, lo, out0):
            rows = pl.ds(start, W)
            ag, _, _ = _conv3(g_ref[rows, :], wg, bg)
            au, _, _ = _conv3(u_ref[rows, :], wu, bu)
            y = (ag * _sigmoid(ag)) * au
            y_ref[pl.ds(out0, R), :] = y[lo:lo + R].astype(BF16)

        _for_row_windows(S, window)

    return pl.pallas_call(
        body, name=name, grid=(nf,),
        in_specs=[pl.BlockSpec((S, tc), lambda j: (0, j)),
                  pl.BlockSpec((S, tc), lambda j: (0, j + nf)),
                  pl.BlockSpec((None, 3, tc), lambda j: (l, 0, j)),
                  pl.BlockSpec((None, 3, tc), lambda j: (l, 0, j + nf)),
                  pl.BlockSpec((None, 1, tc), lambda j: (l, 0, j)),
                  pl.BlockSpec((None, 1, tc), lambda j: (l, 0, j + nf))],
        out_specs=pl.BlockSpec((S, tc), lambda j: (0, j)),
        out_shape=jax.ShapeDtypeStruct((S, F), BF16),
        compiler_params=_params(("arbitrary",)),
    )(ap, ap, cw, cw, cb3, cb3)


def _loss_bwd(name, y, target):
    S, D = y.shape
    tr = _tile(S, 256, 16)

    def body(y_ref, t_ref, loss_ref, d_ref, db_ref):
        @pl.when(pl.program_id(0) == 0)
        def _():
            loss_ref[...] = jnp.zeros_like(loss_ref)

        err = y_ref[...] - t_ref[...]
        part = 0.5 * jnp.sum(jnp.mean(err * err, axis=-1, keepdims=True), axis=0, keepdims=True)
        loss_ref[...] += jnp.broadcast_to(part, loss_ref.shape)
        d = err * (1.0 / D)
        d_ref[...] = d
        db_ref[...] = d.astype(BF16)

    return pl.pallas_call(
        body, name=name, grid=(S // tr,),
        in_specs=[pl.BlockSpec((tr, D), lambda i: (i, 0)), pl.BlockSpec((tr, D), lambda i: (i, 0))],
        out_specs=[pl.BlockSpec((8, 128), lambda i: (0, 0)),
                   pl.BlockSpec((tr, D), lambda i: (i, 0)),
                   pl.BlockSpec((tr, D), lambda i: (i, 0))],
        out_shape=[jax.ShapeDtypeStruct((8, 128), F32),
                   jax.ShapeDtypeStruct((S, D), F32),
                   jax.ShapeDtypeStruct((S, D), BF16)],
        compiler_params=_params(("arbitrary",)),
    )(y, target)


def _rms_bwd(name, x, dh, dres, g3, l):
    S, D = x.shape
    tr = _tile(S, 256, 16)

    def body(x_ref, dh_ref, dr_ref, g_ref, dx_ref, dxb_ref, dg_ref):
        @pl.when(pl.program_id(0) == 0)
        def _():
            dg_ref[...] = jnp.zeros_like(dg_ref)

        xv = x_ref[...]
        dhv = dh_ref[...]
        r = lax.rsqrt(jnp.mean(xv * xv, axis=-1, keepdims=True) + EPS)
        xhat = xv * r
        dhg = dhv * g_ref[...]
        dx = dr_ref[...] + r * (dhg - xhat * jnp.mean(dhg * xhat, axis=-1, keepdims=True))
        dx_ref[...] = dx
        dxb_ref[...] = dx.astype(BF16)
        dg_ref[...] += jnp.sum(dhv * xhat, axis=0, keepdims=True)

    return pl.pallas_call(
        body, name=name, grid=(S // tr,),
        in_specs=[pl.BlockSpec((tr, D), lambda i: (i, 0)), pl.BlockSpec((tr, D), lambda i: (i, 0)),
                  pl.BlockSpec((tr, D), lambda i: (i, 0)), _vec_spec(l, D)],
        out_specs=[pl.BlockSpec((tr, D), lambda i: (i, 0)), pl.BlockSpec((tr, D), lambda i: (i, 0)),
                   pl.BlockSpec((1, D), lambda i: (0, 0))],
        out_shape=[jax.ShapeDtypeStruct((S, D), F32), jax.ShapeDtypeStruct((S, D), BF16),
                   jax.ShapeDtypeStruct((1, D), F32)],
        compiler_params=_params(("arbitrary",)),
    )(x, dh, dres, g3)


def _glu_conv_bwd(name, dy, ap, cw, cb3, l, cfg):
    S, F = cfg["S"], cfg["F"]
    tc = 128
    nf = F // tc
    R, W = CONV_ROWS, CONV_ROWS + 2 * CONV_HALO

    def body(dy_ref, g_ref, u_ref, wg_ref, wu_ref, bg_ref, bu_ref, dap_ref, dw_ref, db_ref, acc):
        acc[...] = jnp.zeros_like(acc)
        wg, wu, bg, bu = wg_ref[...], wu_ref[...], bg_ref[...], bu_ref[...]

        def window(start, lo, out0):
            rows = pl.ds(start, W)
            apg, apu, dyv = g_ref[rows, :], u_ref[rows, :], dy_ref[rows, :]
            ag, g_dn, g_up = _conv3(apg, wg, bg)
            au, u_dn, u_up = _conv3(apu, wu, bu)
            sig = _sigmoid(ag)
            da_u = dyv * (ag * sig)
            da_g = (dyv * au) * (sig * (1.0 + ag * (1.0 - sig)))

            def back(da, w):
                prv, nxt = _shifts(da)
                return ((nxt * w[0:1] + da * w[1:2]) + prv * w[2:3])[lo:lo + R].astype(BF16)

            dap_ref[0, pl.ds(out0, R), :] = back(da_g, wg)
            dap_ref[1, pl.ds(out0, R), :] = back(da_u, wu)

            def add(i, prod):
                acc[i] += jnp.sum(prod[lo:lo + R].reshape(R // 8, 8, tc), axis=0)

            for i, other in enumerate([g_dn, apg, g_up]):
                add(i, da_g * other)
            for i, other in enumerate([u_dn, apu, u_up]):
                add(3 + i, da_u * other)
            add(6, da_g)
            add(7, da_u)

        _for_row_windows(S, window)
        col = [jnp.sum(acc[i], axis=0, keepdims=True) for i in range(8)]
        dw_ref[0] = jnp.concatenate(col[0:3], axis=0)
        dw_ref[1] = jnp.concatenate(col[3:6], axis=0)
        db_ref[0] = col[6]
        db_ref[1] = col[7]

    return pl.pallas_call(
        body, name=name, grid=(nf,),
        in_specs=[pl.BlockSpec((S, tc), lambda j: (0, j)),
                  pl.BlockSpec((S, tc), lambda j: (0, j)),
                  pl.BlockSpec((S, tc), lambda j: (0, j + nf)),
                  pl.BlockSpec((None, 3, tc), lambda j: (l, 0, j)),
                  pl.BlockSpec((None, 3, tc), lambda j: (l, 0, j + nf)),
                  pl.BlockSpec((None, 1, tc), lambda j: (l, 0, j)),
                  pl.BlockSpec((None, 1, tc), lambda j: (l, 0, j + nf))],
        out_specs=[pl.BlockSpec((2, S, tc), lambda j: (0, 0, j)),
                   pl.BlockSpec((2, 3, tc), lambda j: (0, 0, j)),
                   pl.BlockSpec((2, 1, tc), lambda j: (0, 0, j))],
        out_shape=[jax.ShapeDtypeStruct((2, S, F), BF16),
                   jax.ShapeDtypeStruct((2, 3, F), F32),
                   jax.ShapeDtypeStruct((2, 1, F), F32)],
        scratch_shapes=[pltpu.VMEM((8, 8, tc), F32)],
        compiler_params=_params(("arbitrary",)),
    )(dy, ap, ap, cw, cw, cb3, cb3)


def _attn_bwd(name, q, k, v, attn, dmix, sink, ag3, l, cfg):
    S, AW, KVW, NQ, NKV, G = cfg["S"], cfg["AW"], cfg["KVW"], cfg["NQ"], cfg["NKV"], cfg["G"]
    nb = S // HEAD
    scale = HEAD ** -0.5

    def body(q_ref, kp, kc, kn, vp, vc, vn, a_ref, dm_ref, sink_ref, ag_ref,
             dq_ref, dk_ref, dv_ref, dsink_ref, dag_ref, da_scr):
        n = pl.program_id(0)

        @pl.when(n == 0)
        def _():
            dk_ref[...] = jnp.zeros_like(dk_ref)
            dv_ref[...] = jnp.zeros_like(dv_ref)
            dsink_ref[...] = jnp.zeros_like(dsink_ref)
            dag_ref[...] = jnp.zeros_like(dag_ref)

        a = a_ref[...]
        dm = dm_ref[...]
        r = lax.rsqrt(jnp.mean(a * a, axis=-1, keepdims=True) + EPS)
        xhat = a * r
        dmg = dm * ag_ref[...]
        da_scr[...] = r * (dmg - xhat * jnp.mean(dmg * xhat, axis=-1, keepdims=True))
        dag_ref[...] += jnp.sum(dm * xhat, axis=0, keepdims=True)

        kb = jnp.concatenate([kp[...], kc[...], kn[...]], axis=0)
        vb = jnp.concatenate([vp[...], vc[...], vn[...]], axis=0)
        band = pl.ds(pl.multiple_of(n * HEAD, HEAD), 3 * HEAD)
        for j in range(NKV):
            sl = slice(j * HEAD, (j + 1) * HEAD)
            heads = [slice((j * G + g) * HEAD, (j * G + g + 1) * HEAD) for g in range(G)]
            qs = jnp.concatenate([q_ref[:, hs] for hs in heads], axis=0)
            do = jnp.concatenate([da_scr[:, hs] for hs in heads], axis=0)
            kj, vj = kb[:, sl], vb[:, sl]
            probs, p_sink = _attn_probs(qs, kj, n, _sink_rows(sink_ref, l, j, G), S, G)
            dob = do.astype(BF16)
            dprobs = lax.dot_general(dob, vj, (((1,), (1,)), ((), ())), preferred_element_type=F32)
            delta = jnp.sum(dprobs * probs, axis=-1, keepdims=True)
            ds = (probs * (dprobs - delta)) * scale
            dsb = ds.astype(BF16)
            dsk = -(p_sink * delta)
            dq = jnp.dot(dsb, kj, preferred_element_type=F32)
            for g in range(G):
                dq_ref[:, heads[g]] = dq[g * HEAD:(g + 1) * HEAD]
                part = jnp.sum(dsk[g * HEAD:(g + 1) * HEAD], axis=0, keepdims=True)
                dsink_ref[j * G + g:j * G + g + 1, :] += jnp.broadcast_to(part, (1, HEAD))
            dk_ref[band, sl] += lax.dot_general(dsb, qs, (((0,), (0,)), ((), ())), preferred_element_type=F32)
            dv_ref[band, sl] += lax.dot_general(probs.astype(BF16), dob, (((0,), (0,)), ((), ())),
                                                preferred_element_type=F32)

    return pl.pallas_call(
        body, name=name, grid=(nb,),
        in_specs=[pl.BlockSpec((HEAD, AW), lambda n: (n, 0))] + _band_specs(KVW, nb) + _band_specs(KVW, nb)
                 + [pl.BlockSpec((HEAD, AW), lambda n: (n, 0)),
                    pl.BlockSpec((HEAD, AW), lambda n: (n, 0)),
                    pl.BlockSpec(memory_space=pltpu.SMEM), _vec_spec(l, AW)],
        out_specs=[pl.BlockSpec((HEAD, AW), lambda n: (n, 0)),
                   pl.BlockSpec((S + 2 * HEAD, KVW), lambda n: (0, 0)),
                   pl.BlockSpec((S + 2 * HEAD, KVW), lambda n: (0, 0)),
                   pl.BlockSpec((NQ, HEAD), lambda n: (0, 0)),
                   pl.BlockSpec((1, AW), lambda n: (0, 0))],
        out_shape=[jax.ShapeDtypeStruct((S, AW), F32),
                   jax.ShapeDtypeStruct((S + 2 * HEAD, KVW), F32),
                   jax.ShapeDtypeStruct((S + 2 * HEAD, KVW), F32),
                   jax.ShapeDtypeStruct((NQ, HEAD), F32),
                   jax.ShapeDtypeStruct((1, AW), F32)],
        scratch_shapes=[pltpu.VMEM((HEAD, AW), F32)],
        compiler_params=_params(("arbitrary",)),
    )(q, k, k, k, v, v, v, attn, dmix, sink, ag3)


def _qkv_prep_bwd(name, z, dq, dk_pad, dv_pad, qg3, kg3, cos2, sin2, l, cfg):
    S, AW, KVW, NQ, NKV = cfg["S"], cfg["AW"], cfg["KVW"], cfg["NQ"], cfg["NKV"]
    kv_blk = (3 * AW) // (2 * KVW)

    def body(zq_ref, zkv_ref, dq_ref, dk_ref, dv_ref, qg_ref, kg_ref, c_ref, s_ref,
             dzq_ref, dzkv_ref, dqg_ref, dkg_ref):
        @pl.when(pl.program_id(0) == 0)
        def _():
            dqg_ref[...] = jnp.zeros_like(dqg_ref)
            dkg_ref[...] = jnp.zeros_like(dkg_ref)

        cosv, sinv = c_ref[...], s_ref[...]

        def back(t, dr, g):
            r = lax.rsqrt(jnp.mean(t * t, axis=-1, keepdims=True) + EPS)
            xhat = t * r
            dn = dr * cosv + pltpu.roll(dr * sinv, HEAD // 2, axis=1)
            dxh = dn * g
            dt = r * (dxh - xhat * jnp.mean(dxh * xhat, axis=-1, keepdims=True))
            return dt, jnp.sum(dn * xhat, axis=0, keepdims=True)

        gq = jnp.zeros((1, HEAD), F32)
        for h in range(NQ):
            sl = slice(h * HEAD, (h + 1) * HEAD)
            dt, gpart = back(zq_ref[:, sl], dq_ref[:, sl], qg_ref[...])
            dzq_ref[:, sl] = dt.astype(BF16)
            gq = gq + gpart
        dqg_ref[...] += gq
        gk = jnp.zeros((1, HEAD), F32)
        for h in range(NKV):
            sl = slice(h * HEAD, (h + 1) * HEAD)
            dt, gpart = back(zkv_ref[:, sl], dk_ref[:, sl], kg_ref[...])
            dzkv_ref[:, sl] = dt.astype(BF16)
            gk = gk + gpart
        dkg_ref[...] += gk
        dzkv_ref[:, KVW:] = dv_ref[...].astype(BF16)

    return pl.pallas_call(
        body, name=name, grid=(S // HEAD,),
        in_specs=[pl.BlockSpec((HEAD, AW), lambda i: (i, 0)),
                  pl.BlockSpec((HEAD, 2 * KVW), lambda i: (i, kv_blk)),
                  pl.BlockSpec((HEAD, AW), lambda i: (i, 0)),
                  pl.BlockSpec((HEAD, KVW), lambda i: (i + 1, 0)),
                  pl.BlockSpec((HEAD, KVW), lambda i: (i + 1, 0)),
                  _vec_spec(l, HEAD), _vec_spec(l, HEAD),
                  pl.BlockSpec((HEAD, HEAD), lambda i: (i, 0)),
                  pl.BlockSpec((HEAD, HEAD), lambda i: (i, 0))],
        out_specs=[pl.BlockSpec((HEAD, AW), lambda i: (i, 0)),
                   pl.BlockSpec((HEAD, 2 * KVW), lambda i: (i, 0)),
                   pl.BlockSpec((1, HEAD), lambda i: (0, 0)),
                   pl.BlockSpec((1, HEAD), lambda i: (0, 0))],
        out_shape=[jax.ShapeDtypeStruct((S, AW), BF16),
                   jax.ShapeDtypeStruct((S, 2 * KVW), BF16),
                   jax.ShapeDtypeStruct((1, HEAD), F32),
                   jax.ShapeDtypeStruct((1, HEAD), F32)],
        compiler_params=_params(("arbitrary",)),
    )(z, z, dq, dk_pad, dv_pad, qg3, kg3, cos2, sin2)


def _sgu_bwd(name, z, dmix, lng3, lnb3, ws_b, wst_b, bs_b, og3, l, cfg):
    S, GW, NG = cfg["S"], cfg["GW"], cfg["NG"]

    def body(gu_ref, gv_ref, dm_ref, lng_ref, lnb_ref, ws_ref, wst_ref, bs_ref, og_ref,
             dgu_ref, dgv_ref, dws_ref, dbs_ref, dlng_ref, dlnb_ref, dog_ref, sg_scr, f_scr, dvn_scr):
        @pl.when(pl.program_id(0) == 0)
        def _():
            dws_ref[...] = jnp.zeros_like(dws_ref)
            dbs_ref[...] = jnp.zeros_like(dbs_ref)
            dlng_ref[...] = jnp.zeros_like(dlng_ref)
            dlnb_ref[...] = jnp.zeros_like(dlnb_ref)
            dog_ref[...] = jnp.zeros_like(dog_ref)

        gu, gv = gu_ref[...], gv_ref[...]
        lng = lng_ref[...]
        u, xhat, rstd, vn = _sgu_forward_math(gu, gv, lng, lnb_ref[...])
        vnb = vn.astype(BF16)
        for h in range(NG):
            sl = slice(h * HEAD, (h + 1) * HEAD)
            f = jnp.dot(ws_ref[h], vnb[:, sl], preferred_element_type=F32) + bs_ref[h]
            f_scr[:, sl] = f
            sg_scr[:, sl] = u[:, sl] * f
        sg = sg_scr[...]
        dm = dm_ref[...]
        r = lax.rsqrt(jnp.mean(sg * sg, axis=-1, keepdims=True) + EPS)
        sghat = sg * r
        dmg = dm * og_ref[...]
        dsg = r * (dmg - sghat * jnp.mean(dmg * sghat, axis=-1, keepdims=True))
        dog_ref[...] += jnp.sum(dm * sghat, axis=0, keepdims=True)
        du = dsg * f_scr[...]
        df = dsg * u
        dfb = df.astype(BF16)
        for h in range(NG):
            sl = slice(h * HEAD, (h + 1) * HEAD)
            dvn_scr[:, sl] = jnp.dot(wst_ref[h], dfb[:, sl], preferred_element_type=F32)
            dws_ref[h] += lax.dot_general(dfb[:, sl], vnb[:, sl], (((1,), (1,)), ((), ())),
                                          preferred_element_type=F32)
            dbs_ref[h] += jnp.broadcast_to(jnp.sum(df[:, sl], axis=-1, keepdims=True), (HEAD, HEAD))
        dvn = dvn_scr[...]
        dlng_ref[...] += jnp.sum(dvn * xhat, axis=0, keepdims=True)
        dlnb_ref[...] += jnp.sum(dvn, axis=0, keepdims=True)
        dxh = dvn * lng
        dvv = rstd * ((dxh - jnp.mean(dxh, axis=-1, keepdims=True))
                      - xhat * jnp.mean(dxh * xhat, axis=-1, keepdims=True))
        dgu_ref[...] = (du * _gelu_grad(gu)).astype(BF16)
        dgv_ref[...] = (dvv * _gelu_grad(gv)).astype(BF16)

    vec = pl.BlockSpec((1, GW), lambda c: (0, 0))
    mat = pl.BlockSpec((NG, HEAD, HEAD), lambda c: (0, 0, 0))
    wsp = pl.BlockSpec((None, NG, HEAD, HEAD), lambda c: (l, 0, 0, 0))
    return pl.pallas_call(
        body, name=name, grid=(S // HEAD,),
        in_specs=[pl.BlockSpec((HEAD, GW), lambda c: (c, 1)),
                  pl.BlockSpec((HEAD, GW), lambda c: (c, 2)),
                  pl.BlockSpec((HEAD, GW), lambda c: (c, 1)),
                  _vec_spec(l, GW), _vec_spec(l, GW), wsp, wsp, wsp, _vec_spec(l, GW)],
        out_specs=[pl.BlockSpec((HEAD, GW), lambda c: (c, 0)), pl.BlockSpec((HEAD, GW), lambda c: (c, 0)),
                   mat, mat, vec, vec, vec],
        out_shape=[jax.ShapeDtypeStruct((S, GW), BF16), jax.ShapeDtypeStruct((S, GW), BF16),
                   jax.ShapeDtypeStruct((NG, HEAD, HEAD), F32), jax.ShapeDtypeStruct((NG, HEAD, HEAD), F32),
                   jax.ShapeDtypeStruct((1, GW), F32), jax.ShapeDtypeStruct((1, GW), F32),
                   jax.ShapeDtypeStruct((1, GW), F32)],
        scratch_shapes=[pltpu.VMEM((HEAD, GW), F32), pltpu.VMEM((HEAD, GW), F32), pltpu.VMEM((HEAD, GW), F32)],
        compiler_params=_params(("arbitrary",)),
    )(z, z, dmix, lng3, lnb3, ws_b, wst_b, bs_b, og3)


def _mesh_pos():
    x, y, c = lax.axis_index("x"), lax.axis_index("y"), lax.axis_index("c")
    return x, y, c


def _dev_index(p):
    return 4 * p[0] + 2 * p[1] + p[2]


def _handshake(peers):
    barrier = pltpu.get_barrier_semaphore()
    for p in peers:
        pl.semaphore_signal(barrier, inc=1, device_id=p, device_id_type=MESH)
    pl.semaphore_wait(barrier, len(peers))


def _gather_body(na, slabs):
    def body(*refs):
        ins, outs = refs[:na], refs[na:2 * na]
        send_sems, recv_sems, local_sems = refs[2 * na:]
        x, y, c = _mesh_pos()
        me, sib = (x, y, c), (x, y, 1 - c)
        chips = [(1 - x, y), (x, 1 - y), (1 - x, 1 - y)]
        _handshake([sib] + [(*chip, c) for chip in chips])

        def copy(a, k, block, to, src=None):
            dst = slabs[a](outs[a], _dev_index(block))
            return pltpu.make_async_remote_copy(
                src_ref=dst if src is None else src, dst_ref=dst,
                send_sem=send_sems.at[7 * a + k], recv_sem=recv_sems.at[7 * a + k],
                device_id=to, device_id_type=MESH)

        local = [pltpu.make_async_copy(ins[a], slabs[a](outs[a], _dev_index(me)), local_sems.at[a])
                 for a in range(na)]
        for cp in local:
            cp.start()
        first = []
        for a in range(na):
            first.append(copy(a, 0, me, sib, src=ins[a]))
            first += [copy(a, 1 + j, me, (*chip, c), src=ins[a]) for j, chip in enumerate(chips)]
        for cp in first:
            cp.start()
        passed = []
        for j, chip in enumerate(chips):
            for a in range(na):
                copy(a, 1 + j, (*chip, c), me).wait_recv()
                fwd = copy(a, 4 + j, (*chip, c), sib)
                fwd.start()
                passed.append(fwd)
        for a in range(na):
            copy(a, 0, sib, me).wait_recv()
            for j, chip in enumerate(chips):
                copy(a, 4 + j, (*chip, 1 - c), me).wait_recv()
        for cp in first + passed:
            cp.wait_send()
        for cp in local:
            cp.wait()

    return body


def _all_gather(name, cid, shards, out_shapes, slabs):
    na = len(shards)
    return pl.kernel(
        _gather_body(na, slabs), out_type=out_shapes,
        mesh=plsc.ScalarSubcoreMesh(axis_name="seq", num_cores=1), name=name,
        scratch_types=[pltpu.SemaphoreType.DMA((7 * na,)), pltpu.SemaphoreType.DMA((7 * na,)),
                       pltpu.SemaphoreType.DMA((na,))],
        compiler_params=pltpu.CompilerParams(collective_id=cid),
    )(*shards)


def _d2d_body(nb, ns):
    n = nb + ns

    def body(*refs):
        ins, outs = refs[:n], refs[n:2 * n]
        send_sems, recv_sems = refs[2 * n:]
        x, y, c = _mesh_pos()
        sib = (x, y, 1 - c)
        _handshake([sib])
        copies = []
        for t in range(n):
            cp = pltpu.make_async_remote_copy(
                src_ref=ins[t].at[:, 1 - c] if t < nb else ins[t], dst_ref=outs[t],
                send_sem=send_sems.at[t], recv_sem=recv_sems.at[t],
                device_id=sib, device_id_type=MESH)
            cp.start()
            copies.append(cp)
        for cp in copies:
            cp.wait()

    return body


def _rs_d2d(name, cid, bigs, smalls):
    shapes = [jax.ShapeDtypeStruct((g.shape[0],) + g.shape[2:], g.dtype) for g in bigs]
    shapes += [jax.ShapeDtypeStruct(s.shape, s.dtype) for s in smalls]
    n = len(shapes)
    return pl.kernel(
        _d2d_body(len(bigs), len(smalls)), out_type=shapes,
        mesh=plsc.ScalarSubcoreMesh(axis_name="seq", num_cores=1), name=name,
        scratch_types=[pltpu.SemaphoreType.DMA((n,)), pltpu.SemaphoreType.DMA((n,))],
        compiler_params=pltpu.CompilerParams(collective_id=cid),
    )(*bigs, *smalls)


def _pair_sum(name, g4, recv, cvec, row_mult):
    Q, _, R, C = g4.shape
    tr = _tile(R, 256, row_mult)

    def body(c_ref, a_ref, b_ref, o_ref):
        o_ref[...] = (a_ref[...].astype(F32) + b_ref[...].astype(F32)).astype(BF16)

    grid_spec = pltpu.PrefetchScalarGridSpec(
        num_scalar_prefetch=1, grid=(Q, R // tr),
        in_specs=[pl.BlockSpec((None, None, tr, C), lambda q, i, c_ref: (q, c_ref[0], i, 0)),
                  pl.BlockSpec((None, tr, C), lambda q, i, c_ref: (q, i, 0))],
        out_specs=pl.BlockSpec((None, tr, C), lambda q, i, c_ref: (q, i, 0)))
    return pl.pallas_call(
        body, name=name, grid_spec=grid_spec, out_shape=jax.ShapeDtypeStruct((Q, R, C), BF16),
        compiler_params=_params(("arbitrary", "arbitrary")),
    )(cvec, g4, recv)


def _add2(name, a, b):
    R, C = a.shape
    tr = _tile(R, 512, 8)

    def body(a_ref, b_ref, o_ref):
        o_ref[...] = a_ref[...] + b_ref[...]

    blk = pl.BlockSpec((tr, C), lambda i: (i, 0))
    return pl.pallas_call(body, name=name, grid=(R // tr,), in_specs=[blk, blk], out_specs=blk,
                          out_shape=jax.ShapeDtypeStruct((R, C), a.dtype),
                          compiler_params=_params(("arbitrary",)))(a, b)


def _ici_body(nb, ns):
    n = nb + ns

    def body(*refs):
        ins, outs = refs[:n], refs[n:2 * n]
        send_sems, recv_sems, local_sems = refs[2 * n:]
        x, y, c = _mesh_pos()
        q_me = 2 * x + y
        chips = [(x ^ (k >> 1), y ^ (k & 1)) for k in range(1, 4)]
        _handshake([(px, py, c) for px, py in chips])
        copies = []
        for t in range(n):
            src_of = (lambda q, t=t: ins[t].at[q]) if t < nb else (lambda q, t=t: ins[t])
            dst = outs[t].at[q_me]
            loc = pltpu.make_async_copy(src_of(q_me), dst, local_sems.at[t])
            loc.start()
            copies.append(loc)
            for k, (px, py) in enumerate(chips):
                cp = pltpu.make_async_remote_copy(
                    src_ref=src_of(2 * px + py), dst_ref=dst,
                    send_sem=send_sems.at[3 * t + k], recv_sem=recv_sems.at[3 * t + k],
                    device_id=(px, py, c), device_id_type=MESH)
                cp.start()
                copies.append(cp)
        for cp in copies:
            cp.wait()

    return body


def _rs_ici(name, cid, bigs, smalls):
    shapes = [jax.ShapeDtypeStruct(g.shape, g.dtype) for g in bigs]
    shapes += [jax.ShapeDtypeStruct((N_DEV // 2,) + s.shape, s.dtype) for s in smalls]
    n = len(shapes)
    return pl.kernel(
        _ici_body(len(bigs), len(smalls)), out_type=shapes,
        mesh=plsc.ScalarSubcoreMesh(axis_name="seq", num_cores=1), name=name,
        scratch_types=[pltpu.SemaphoreType.DMA((3 * n,)), pltpu.SemaphoreType.DMA((3 * n,)),
                       pltpu.SemaphoreType.DMA((n,))],
        compiler_params=pltpu.CompilerParams(collective_id=cid),
    )(*bigs, *smalls)


def _adamw_math(w, g, m, v):
    m2 = ADAM_B1 * m + (1.0 - ADAM_B1) * g
    v2 = ADAM_B2 * v + (1.0 - ADAM_B2) * (g * g)
    m_hat = m2 / (1.0 - ADAM_B1 ** ADAM_STEP)
    v_hat = v2 / (1.0 - ADAM_B2 ** ADAM_STEP)
    delta = -ADAM_LR * (m_hat / (jnp.sqrt(v_hat) + ADAM_EPS) + ADAM_WD * w)
    return delta, m2, v2


def _adamw_sum(name, recv, w, m, v, l, prev, row_mult):
    NS, R, C = recv.shape
    L = w.shape[0]
    tr = _tile(R, 128, row_mult)

    def body(*refs):
        r_ref, w_ref, m_ref, v_ref = refs[:4]
        g_ref, d_ref, nm_ref, nv_ref = refs[-4:]
        g = r_ref[0].astype(F32)
        for s in range(1, NS):
            g = g + r_ref[s].astype(F32)
        d, m2, v2 = _adamw_math(w_ref[...], g, m_ref[...], v_ref[...])
        g_ref[...] = g
        d_ref[...] = d
        nm_ref[...] = m2
        nv_ref[...] = v2

    blk = pl.BlockSpec((None, tr, C), lambda i: (l, i, 0))
    shp = jax.ShapeDtypeStruct((L, R, C), F32)
    in_specs = [pl.BlockSpec((NS, tr, C), lambda i: (0, i, 0)), blk, blk, blk]
    args = [recv, w, m, v]
    aliases = {}
    if prev is not None:
        in_specs += [pl.BlockSpec(memory_space=pl.ANY)] * 4
        args += list(prev)
        aliases = {4 + i: i for i in range(4)}
    return pl.pallas_call(
        body, name=name, grid=(R // tr,),
        in_specs=in_specs, out_specs=[blk, blk, blk, blk], out_shape=[shp, shp, shp, shp],
        input_output_aliases=aliases,
        compiler_params=_params(("arbitrary",)),
    )(*args)


def _adamw_plain(name, g, w, m, v):
    def body(g_ref, w_ref, m_ref, v_ref, d_ref, nm_ref, nv_ref):
        d, m2, v2 = _adamw_math(w_ref[...], g_ref[...], m_ref[...], v_ref[...])
        d_ref[...] = d
        nm_ref[...] = m2
        nv_ref[...] = v2

    shp = jax.ShapeDtypeStruct(g.shape, F32)
    return pl.pallas_call(body, name=name, out_shape=[shp, shp, shp], compiler_params=_params())(g, w, m, v)


SMALL = ["norm1_g", "q_norm_g", "k_norm_g", "sink", "sgu_ln_g", "sgu_ln_b", "w_s", "b_s",
         "attn_out_g", "sgu_out_g", "norm2_g", "conv_b"]
PACK_ALIGN = 1024


def _pack(pieces):
    flat = []
    for p in pieces:
        f = p.reshape(-1).astype(F32)
        pad = (-f.shape[0]) % PACK_ALIGN
        flat.append(jnp.pad(f, (0, pad)) if pad else f)
    return jnp.concatenate(flat).reshape(-1, 128)


def _unpack(packed, shapes):
    flat = packed.reshape(-1)
    out, off = [], 0
    for shp in shapes:
        n = 1
        for d in shp:
            n *= d
        out.append(flat[off:off + n].reshape(shp))
        off += n + ((-n) % PACK_ALIGN)
    return out


def kernel(x, norm1_g, w_in, q_norm_g, k_norm_g, sink, sgu_ln_g, sgu_ln_b, w_s, b_s, attn_out_g, sgu_out_g, w_o, norm2_g, w_up, conv_w, conv_b, w_down, loss_target, m_norm1_g, m_w_in, m_q_norm_g, m_k_norm_g, m_sink, m_sgu_ln_g, m_sgu_ln_b, m_w_s, m_b_s, m_attn_out_g, m_sgu_out_g, m_w_o, m_norm2_g, m_w_up, m_conv_w, m_conv_b, m_w_down, v_norm1_g, v_w_in, v_q_norm_g, v_k_norm_g, v_sink, v_sgu_ln_g, v_sgu_ln_b, v_w_s, v_b_s, v_attn_out_g, v_sgu_out_g, v_w_o, v_norm2_g, v_w_up, v_conv_w, v_conv_b, v_w_down):
    weights = dict(norm1_g=norm1_g, w_in=w_in, q_norm_g=q_norm_g, k_norm_g=k_norm_g, sink=sink, sgu_ln_g=sgu_ln_g,
                   sgu_ln_b=sgu_ln_b, w_s=w_s, b_s=b_s, attn_out_g=attn_out_g, sgu_out_g=sgu_out_g, w_o=w_o,
                   norm2_g=norm2_g, w_up=w_up, conv_w=conv_w, conv_b=conv_b, w_down=w_down)
    mom_m = dict(norm1_g=m_norm1_g, w_in=m_w_in, q_norm_g=m_q_norm_g, k_norm_g=m_k_norm_g, sink=m_sink,
                 sgu_ln_g=m_sgu_ln_g, sgu_ln_b=m_sgu_ln_b, w_s=m_w_s, b_s=m_b_s, attn_out_g=m_attn_out_g,
                 sgu_out_g=m_sgu_out_g, w_o=m_w_o, norm2_g=m_norm2_g, w_up=m_w_up, conv_w=m_conv_w,
                 conv_b=m_conv_b, w_down=m_w_down)
    mom_v = dict(norm1_g=v_norm1_g, w_in=v_w_in, q_norm_g=v_q_norm_g, k_norm_g=v_k_norm_g, sink=v_sink,
                 sgu_ln_g=v_sgu_ln_g, sgu_ln_b=v_sgu_ln_b, w_s=v_w_s, b_s=v_b_s, attn_out_g=v_attn_out_g,
                 sgu_out_g=v_sgu_out_g, w_o=v_w_o, norm2_g=v_norm2_g, w_up=v_w_up, conv_w=v_conv_w,
                 conv_b=v_conv_b, w_down=v_w_down)
    order = ["norm1_g", "w_in", "q_norm_g", "k_norm_g", "sink", "sgu_ln_g", "sgu_ln_b", "w_s", "b_s",
             "attn_out_g", "sgu_out_g", "w_o", "norm2_g", "w_up", "conv_w", "conv_b", "w_down"]

    _, S, D = x.shape
    L = w_in.shape[0]
    AW = D // 2
    NQ = AW // HEAD
    NKV = max(1, NQ // 4)
    G = NQ // NKV
    KVW = NKV * HEAD
    GW = D - AW
    NG = GW // HEAD
    IN = AW + 2 * KVW + 2 * GW
    INS = w_in.shape[2]
    OS = w_o.shape[1]
    US = w_up.shape[2]
    DS = w_down.shape[1]
    F2 = US * N_DEV
    F = F2 // 2
    assert INS * N_DEV == IN and OS * N_DEV == D and DS * N_DEV == F and AW == GW and (3 * AW) % (2 * KVW) == 0
    cfg = dict(S=S, D=D, AW=AW, NQ=NQ, NKV=NKV, G=G, KVW=KVW, GW=GW, NG=NG, F=F, F2=F2)

    def rows_slab(n):
        return lambda ref, idx: ref.at[:, pl.ds(pl.multiple_of(idx * n, n), n), :]

    def cols_slab(n):
        return lambda ref, idx: ref.at[:, :, pl.ds(pl.multiple_of(idx * n, n), n)]

    def after(first, then):
        return lax.optimization_barrier((first, then))

    w_in_t, m_w_in_t, v_w_in_t = (jnp.swapaxes(a, 1, 2) for a in (w_in, m_w_in, v_w_in))

    ids = iter(range(64))
    w_in_f, w_o_f, w_up_f, w_down_f = [], [], [], []

    def gather_layer(l, not_before=None):
        in_shard = w_in_t[l:l + 1].astype(BF16)
        if not_before is not None:
            _, in_shard = after(not_before, in_shard)
        w_in_f.extend(_all_gather("ag_in", next(ids), [in_shard],
                                  [jax.ShapeDtypeStruct((1, IN, D), BF16)], [rows_slab(INS)]))
        w_o_f.extend(_all_gather("ag_o", next(ids), [w_o[l:l + 1].astype(BF16)],
                                 [jax.ShapeDtypeStruct((1, D, D), BF16)], [rows_slab(OS)]))
        w_up_f.extend(_all_gather("ag_up", next(ids), [w_up[l:l + 1].astype(BF16)],
                                  [jax.ShapeDtypeStruct((1, D, F2), BF16)], [cols_slab(US)]))
        w_down_f.extend(_all_gather("ag_down", next(ids), [w_down[l:l + 1].astype(BF16)],
                                    [jax.ShapeDtypeStruct((1, F, D), BF16)], [rows_slab(DS)]))

    gather_layer(0)
    (conv_w_f,) = _all_gather("ag_conv_w", next(ids), [conv_w], [jax.ShapeDtypeStruct((L, 3, F2), F32)],
                              [cols_slab(US)])

    n1g3, n2g3 = norm1_g.reshape(L, 1, D), norm2_g.reshape(L, 1, D)
    qg3, kg3 = q_norm_g.reshape(L, 1, HEAD), k_norm_g.reshape(L, 1, HEAD)
    lng3, lnb3 = sgu_ln_g.reshape(L, 1, GW), sgu_ln_b.reshape(L, 1, GW)
    ag3, og3 = attn_out_g.reshape(L, 1, AW), sgu_out_g.reshape(L, 1, GW)
    cb3 = conv_b.reshape(L, 1, F2)
    ws_b = w_s.astype(BF16)
    wst_b = jnp.swapaxes(w_s, 2, 3).astype(BF16)
    bs_b = jnp.broadcast_to(b_s[..., None], (L, NG, HEAD, HEAD))
    inv_freq = ROPE_THETA ** (-jnp.arange(0, HEAD, 2, dtype=F32) / HEAD)
    ang = jnp.arange(S, dtype=F32)[:, None] * inv_freq[None, :]
    cos2 = jnp.concatenate([jnp.cos(ang), jnp.cos(ang)], axis=1)
    sin2 = jnp.concatenate([-jnp.sin(ang), jnp.sin(ang)], axis=1)

    tn = 512
    t_in, t_d, t_f, t_f2 = _tile(IN, tn, 128), _tile(D, tn, 128), _tile(F, tn, 128), _tile(F2, tn, 128)
    tm_f = _tile(F, 512, 128)
    tm_s = _tile(S, 1024, 128)
    assert AW % t_in == 0 and (2 * KVW) % t_in == 0
    n_q, n_kv, n_g = AW // t_in, (2 * KVW) // t_in, (2 * GW) // t_in

    def in_tile(j):
        return jnp.where(j < n_q, j, jnp.where(j < n_q + n_g, j + n_kv, j - n_g))

    def w_spec(tk, tn_):
        return pl.BlockSpec((None, tk, tn_), lambda i, j, k: (0, k, j))

    def wt_spec(tn_, tk):
        return pl.BlockSpec((None, tn_, tk), lambda i, j, k: (0, j, k))

    def a_spec(tm, tk):
        return pl.BlockSpec((tm, tk), lambda i, j, k: (i, k))

    def at_spec(tk, tm):
        return pl.BlockSpec((tk, tm), lambda i, j, k: (k, i))

    def b_spec(tk, tn_):
        return pl.BlockSpec((tk, tn_), lambda i, j, k: (k, j))

    xs = x.reshape(S, D)
    saved = []
    cur = xs
    for l in range(L):
        h = _rms_fwd("rms1_fwd", cur, n1g3, l)
        z = _mm("mm_in", h, w_in_f[l], M=S, N=IN, K=D, tm=S, tn=t_in, tk=D, tb=True,
                a_spec=a_spec(S, D),
                b_spec=pl.BlockSpec((None, t_in, D), lambda i, j, k: (0, in_tile(j), 0)))
        q_r, k_r, v_b = _qkv_prep("qkv_prep", z, qg3, kg3, cos2, sin2, l, cfg)
        attn, mix_l = _attn_fwd("attn_fwd", q_r, k_r, v_b, sink, ag3, l, cfg)
        mix_r = _sgu_fwd("sgu_fwd", z, lng3, lnb3, ws_b, bs_b, og3, l, cfg)
        mixed = jnp.concatenate([mix_l, mix_r], axis=1)
        x1 = _mm("mm_o", mixed, w_o_f[l], M=S, N=D, K=D, tm=S, tn=t_d, tk=D,
                 a_spec=a_spec(S, D), b_spec=w_spec(D, t_d), res=cur)
        if l + 1 < L:
            gather_layer(l + 1, not_before=x1)
        h2 = _rms_fwd("rms2_fwd", x1, n2g3, l)
        ap = _mm("mm_up", h2, w_up_f[l], M=S, N=F2, K=D, tm=S, tn=t_f2, tk=D,
                 a_spec=a_spec(S, D), b_spec=w_spec(D, t_f2))
        y_b = _conv_glu_fwd("conv_glu_fwd", ap, conv_w_f, cb3, l, cfg)
        x2 = _mm("mm_down", y_b, w_down_f[l], M=S, N=D, K=F, tm=tm_s, tn=t_d, tk=F,
                 a_spec=a_spec(tm_s, F), b_spec=w_spec(F, t_d), res=x1)
        saved.append(dict(x=cur, h=h, z=z, q=q_r, k=k_r, v=v_b, attn=attn, mixed=mixed, x1=x1, h2=h2, ap=ap, y=y_b))
        cur = x2

    loss_tile, dx, dxb = _loss_bwd("loss", cur, loss_target.reshape(S, D))
    loss = lax.psum(loss_tile[0, 0], ("x", "y", "c"))

    gS = [dict() for _ in range(L)]
    recv = [dict() for _ in range(L)]
    cvec = jnp.reshape(lax.axis_index("c"), (1,)).astype(jnp.int32)

    def rs_swap(name, g, R, C, small=None):
        g4 = g.reshape(N_DEV // 2, 2, R, C)
        got = _rs_d2d("rs_d2d_" + name, next(ids), [g4], [] if small is None else [small])
        return name, g4, got, small

    in_flight = []
    IN_FLIGHT_MAX = 2

    def rs_finish(l, pending, then=None):
        name, g4, got, small = pending
        chip = _pair_sum("pair_sum_" + name, g4, got[0], cvec, 16)
        if then is not None:
            chip, then = after(chip, then)
            while len(in_flight) >= IN_FLIGHT_MAX:
                ll, nn, outs = in_flight.pop(0)
                outs, then = after(outs, then)
                recv[ll][nn] = outs
        chip_small = [] if small is None else [_add2("pair_sum_small", small, got[1])]
        in_flight.append((l, name, _rs_ici("rs_ici_" + name, next(ids), [chip], chip_small)))
        return then

    carried = None
    for l in reversed(range(L)):
        sv = saved[l]
        dy = _mm("mm_dy", dxb, w_down_f[l], M=S, N=F, K=D, tm=S, tn=t_f, tk=D, tb=True,
                 a_spec=a_spec(S, D), b_spec=wt_spec(t_f, D))
        if carried is not None:
            dy = rs_finish(l + 1, carried, dy)
        g_down = _mm("mm_gdown", sv["y"], dxb, M=F, N=D, K=S, tm=tm_f, tn=D, tk=S, ta=True,
                     a_spec=at_spec(S, tm_f), b_spec=b_spec(S, D), out_dtype=BF16)
        g_down, dy = after(g_down, dy)
        swap_down = rs_swap("w_down", g_down, DS, D)
        dap3, dcw, dcb = _glu_conv_bwd("glu_conv_bwd", dy, sv["ap"], conv_w_f, cb3, l, cfg)
        gS[l]["conv_w"] = jnp.concatenate([dcw[0], dcw[1]], axis=1)
        gS[l]["conv_b"] = jnp.concatenate([dcb[0], dcb[1]], axis=1).reshape(F2)
        dap3 = rs_finish(l, swap_down, dap3)
        dh2 = _mm("mm_dh2", dap3, w_up_f[l], M=S, N=D, K=F2, tm=tm_s, tn=t_d, tk=F, tb=True,
                  a_spec=pl.BlockSpec((None, tm_s, F), lambda i, j, k: (k, i, 0)),
                  b_spec=wt_spec(t_d, F))
        g_up = _mm("mm_gup", sv["h2"], dap3, M=D, N=F2, K=S, tm=D, tn=US, tk=S, ta=True,
                   a_spec=at_spec(S, D),
                   b_spec=pl.BlockSpec((None, S, US), lambda i, j, k: (j // (N_DEV // 2), 0, j % (N_DEV // 2))),
                   out_dtype=BF16, out_shape=(N_DEV, D, US),
                   out_spec=pl.BlockSpec((None, D, US), lambda i, j, k: (j, 0, 0)))
        g_up, dh2 = after(g_up, dh2)
        swap_up = rs_swap("w_up", g_up, D, US)
        dx1, dx1b, dg2 = _rms_bwd("rms2_bwd", sv["x1"], dh2, dx, n2g3, l)
        gS[l]["norm2_g"] = dg2.reshape(D)
        dmix = _mm("mm_dmix", dx1b, w_o_f[l], M=S, N=D, K=D, tm=S, tn=t_d, tk=D, tb=True,
                   a_spec=a_spec(S, D), b_spec=wt_spec(t_d, D))
        g_o = _mm("mm_go", sv["mixed"], dx1b, M=D, N=D, K=S, tm=D, tn=t_d, tk=S, ta=True,
                  a_spec=at_spec(S, D), b_spec=b_spec(S, t_d), out_dtype=BF16)
        g_o, dmix = after(g_o, dmix)
        swap_o = rs_swap("w_o", g_o, OS, D)
        dmix = rs_finish(l, swap_up, dmix)
        dq_r, dk_pad, dv_pad, dsink, dag = _attn_bwd("attn_bwd", sv["q"], sv["k"], sv["v"], sv["attn"], dmix,
                                                     sink, ag3, l, cfg)
        dq_r = rs_finish(l, swap_o, dq_r)
        gS[l]["sink"] = dsink[:, 0]
        gS[l]["attn_out_g"] = dag.reshape(AW)
        dzgu, dzgv, dws, dbs, dlng, dlnb, dog = _sgu_bwd("sgu_bwd", sv["z"], dmix, lng3, lnb3, ws_b, wst_b, bs_b,
                                                        og3, l, cfg)
        gS[l]["w_s"] = dws
        gS[l]["b_s"] = dbs[:, :, 0]
        gS[l]["sgu_ln_g"] = dlng.reshape(GW)
        gS[l]["sgu_ln_b"] = dlnb.reshape(GW)
        gS[l]["sgu_out_g"] = dog.reshape(GW)
        dzq, dzkv, dqg, dkg = _qkv_prep_bwd("qkv_prep_bwd", sv["z"], dq_r, dk_pad, dv_pad, qg3, kg3, cos2, sin2,
                                            l, cfg)
        gS[l]["q_norm_g"] = dqg.reshape(HEAD)
        gS[l]["k_norm_g"] = dkg.reshape(HEAD)
        dz = jnp.concatenate([dzq, dzkv, dzgu, dzgv], axis=1)
        dh = _mm("mm_dh", dz, w_in_f[l], M=S, N=D, K=IN, tm=tm_s, tn=t_d, tk=IN,
                 a_spec=a_spec(tm_s, IN), b_spec=w_spec(IN, t_d))
        g_in = _mm("mm_gin", dz, sv["h"], M=IN, N=D, K=S, tm=t_in, tn=D, tk=S, ta=True,
                   a_spec=at_spec(S, t_in), b_spec=b_spec(S, D), out_dtype=BF16)
        g_in, dh = after(g_in, dh)
        dx, dxb, dg1 = _rms_bwd("rms1_bwd", sv["x"], dh, dx1, n1g3, l)
        gS[l]["norm1_g"] = dg1.reshape(D)
        carried = rs_swap("w_in", g_in, INS, D, small=_pack([gS[l][n] for n in SMALL] + [gS[l]["conv_w"]]))
    rs_finish(0, carried)
    for ll, nn, outs in in_flight:
        recv[ll][nn] = outs
    grad_x = dx.reshape(1, S, D)

    grads, deltas, new_m, new_v = {}, {}, {}, {}
    for name in ["w_down", "w_up", "w_o", "w_in"]:
        wmv = (w_in_t, m_w_in_t, v_w_in_t) if name == "w_in" else (weights[name], mom_m[name], mom_v[name])
        res = None
        for l in reversed(range(L)):
            res = _adamw_sum("adamw_" + name, recv[l][name][0], *wmv, l, res, 16)
        if name == "w_in":
            res = [jnp.swapaxes(r, 1, 2) for r in res]
        grads[name], deltas[name], new_m[name], new_v[name] = res

    small_shapes = [weights[n].shape[1:] for n in SMALL] + [(3, F2)]
    SR = recv[0]["w_in"][1].shape[1]
    tr_s = _tile(SR, 512, 8)
    sblk = pl.BlockSpec((tr_s, 128), lambda i: (i, 0))
    sshp = jax.ShapeDtypeStruct((SR, 128), F32)
    small_parts = [[None] * L for _ in range(4)]
    for l in reversed(range(L)):
        def small_body(r_ref, w_ref, m_ref, v_ref, g_ref, d_ref, nm_ref, nv_ref):
            g = r_ref[0]
            for s in range(1, N_DEV // 2):
                g = g + r_ref[s]
            d, m2, v2 = _adamw_math(w_ref[...], g, m_ref[...], v_ref[...])
            g_ref[...] = g
            d_ref[...] = d
            nm_ref[...] = m2
            nv_ref[...] = v2

        def pack_small(src):
            return _pack([src[n][l] for n in SMALL] + [jnp.zeros((3, F2), F32)])

        packed = pl.pallas_call(
            small_body, name="adamw_small", grid=(SR // tr_s,),
            in_specs=[pl.BlockSpec((N_DEV // 2, tr_s, 128), lambda i: (0, i, 0)), sblk, sblk, sblk],
            out_specs=[sblk, sblk, sblk, sblk], out_shape=[sshp, sshp, sshp, sshp],
            compiler_params=_params(("arbitrary",)),
        )(recv[l]["w_in"][1], pack_small(weights), pack_small(mom_m), pack_small(mom_v))
        for t in range(4):
            small_parts[t][l] = _unpack(packed[t], small_shapes)
    for t, store in enumerate([grads, deltas, new_m, new_v]):
        for i, n in enumerate(SMALL):
            store[n] = jnp.stack([small_parts[t][l][i] for l in range(L)])
    conv_full = jnp.stack([small_parts[0][l][len(SMALL)] for l in range(L)])
    me_i = _dev_index(_mesh_pos())
    g_cw = lax.dynamic_slice_in_dim(conv_full, me_i * US, US, axis=2)
    grads["conv_w"] = g_cw
    d_cw, m_cw, v_cw = _adamw_plain("adamw_conv_w", g_cw.reshape(L * 3, US), conv_w.reshape(L * 3, US),
                                    m_conv_w.reshape(L * 3, US), v_conv_w.reshape(L * 3, US))
    deltas["conv_w"] = d_cw.reshape(L, 3, US)
    new_m["conv_w"] = m_cw.reshape(L, 3, US)
    new_v["conv_w"] = v_cw.reshape(L, 3, US)

    return (loss, grad_x, *[grads[n] for n in order], *[deltas[n] for n in order],
            *[new_m[n] for n in order], *[new_v[n] for n in order])
```

```python
import jax
import jax.numpy as jnp
from jax import lax
from jax.experimental import pallas as pl
from jax.experimental.pallas import tpu as pltpu
from jax.experimental.pallas import tpu_sc as plsc

F32 = jnp.float32
BF16 = jnp.bfloat16
MESH = pl.DeviceIdType.MESH

N_DEV = 8
HEAD = 128
EPS = 1e-6
MASK_VALUE = -1e30
ROPE_THETA = 10000.0
GELU_C = 0.7978845608028654
GELU_A = 0.044715

ADAM_LR = 0.001
ADAM_B1 = 0.9
ADAM_B2 = 0.999
ADAM_EPS = 1e-08
ADAM_WD = 0.01
ADAM_STEP = 10

VMEM_LIMIT = 56 * 1024 * 1024


def _tile(n, pref, mult):
    best = None
    for t in range(mult, min(n, pref) + 1, mult):
        if n % t == 0:
            best = t
    return n if best is None else best


def _params(sem=None):
    kw = dict(vmem_limit_bytes=VMEM_LIMIT)
    if sem is not None:
        kw["dimension_semantics"] = sem
    return pltpu.CompilerParams(**kw)


def _gelu(x):
    return x * (0.5 * (1.0 + jnp.tanh(GELU_C * (x + GELU_A * (x * x * x)))))


def _gelu_grad(x):
    t = jnp.tanh(GELU_C * (x + GELU_A * (x * x * x)))
    return 0.5 * (1.0 + t) + 0.5 * x * (1.0 - t * t) * (GELU_C * (1.0 + 3.0 * GELU_A * (x * x)))


def _sigmoid(x):
    return 1.0 / (1.0 + jnp.exp(-x))


def _vec_spec(l, n):
    return pl.BlockSpec((None, 1, n), lambda *_: (l, 0, 0))


def _mm(name, a, b, *, M, N, K, tm, tn, tk, a_spec, b_spec, ta=False, tb=False, out_dtype=F32, res=None,
        out_shape=None, out_spec=None):
    nm, nn, nk = M // tm, N // tn, K // tk
    assert nm * tm == M and nn * tn == N and nk * tk == K
    assert not (ta and nk > 1)
    dims = (((1,), (1,)), ((), ())) if tb else (((1,), (0,)), ((), ()))

    def body(*refs):
        refs = list(refs)
        a_ref = refs.pop(0)
        b_ref = refs.pop(0)
        r_ref = refs.pop(0) if res is not None else None
        o_ref = refs.pop(0)
        acc = refs.pop(0) if nk > 1 else None
        at = refs.pop(0) if ta else None
        k = pl.program_id(2)
        if ta:
            @pl.when(pl.program_id(1) == 0)
            def _():
                at[...] = a_ref[...].T
            lhs = at[...]
        else:
            lhs = a_ref[...]
        p = lax.dot_general(lhs, b_ref[...], dims, preferred_element_type=F32)

        def finish(r):
            if r_ref is not None:
                r = r_ref[...] + r
            o_ref[...] = r.astype(out_dtype)

        if nk == 1:
            finish(p)
        else:
            @pl.when(k == 0)
            def _():
                acc[...] = p

            @pl.when(k > 0)
            def _():
                acc[...] += p

            @pl.when(k == nk - 1)
            def _():
                finish(acc[...])

    in_specs = [a_spec, b_spec]
    args = [a, b]
    if res is not None:
        in_specs.append(pl.BlockSpec((tm, tn), lambda i, j, k: (i, j)))
        args.append(res)
    scratch = []
    if nk > 1:
        scratch.append(pltpu.VMEM((tm, tn), F32))
    if ta:
        scratch.append(pltpu.VMEM((tm, tk), BF16))
    return pl.pallas_call(
        body, name=name, grid=(nm, nn, nk),
        in_specs=in_specs,
        out_specs=pl.BlockSpec((tm, tn), lambda i, j, k: (i, j)) if out_spec is None else out_spec,
        out_shape=jax.ShapeDtypeStruct((M, N) if out_shape is None else out_shape, out_dtype),
        scratch_shapes=scratch,
        compiler_params=_params(("arbitrary", "arbitrary", "arbitrary")),
    )(*args)


def _rms_fwd(name, x, g3, l):
    S, D = x.shape
    tr = _tile(S, 256, 16)

    def body(x_ref, g_ref, h_ref):
        xv = x_ref[...]
        r = lax.rsqrt(jnp.mean(xv * xv, axis=-1, keepdims=True) + EPS)
        h_ref[...] = ((xv * r) * g_ref[...]).astype(BF16)

    return pl.pallas_call(
        body, name=name, grid=(S // tr,),
        in_specs=[pl.BlockSpec((tr, D), lambda i: (i, 0)), _vec_spec(l, D)],
        out_specs=pl.BlockSpec((tr, D), lambda i: (i, 0)),
        out_shape=jax.ShapeDtypeStruct((S, D), BF16),
        compiler_params=_params(("arbitrary",)),
    )(x, g3)


def _rope(t, cos2, sin2):
    return t * cos2 + pltpu.roll(t, HEAD // 2, axis=1) * sin2


def _qkv_prep(name, z, qg3, kg3, cos2, sin2, l, cfg):
    S, AW, KVW, NQ, NKV = cfg["S"], cfg["AW"], cfg["KVW"], cfg["NQ"], cfg["NKV"]
    tr = _tile(S, 256, 16)
    kv_blk = (3 * AW) // (2 * KVW)

    def body(zq_ref, zkv_ref, qg_ref, kg_ref, c_ref, s_ref, q_ref, k_ref, v_ref):
        cosv, sinv = c_ref[...], s_ref[...]

        def norm_rope(t, g):
            r = lax.rsqrt(jnp.mean(t * t, axis=-1, keepdims=True) + EPS)
            return _rope((t * r) * g, cosv, sinv)

        for h in range(NQ):
            sl = slice(h * HEAD, (h + 1) * HEAD)
            q_ref[:, sl] = norm_rope(zq_ref[:, sl], qg_ref[...]).astype(BF16)
        for h in range(NKV):
            sl = slice(h * HEAD, (h + 1) * HEAD)
            k_ref[:, sl] = norm_rope(zkv_ref[:, sl], kg_ref[...]).astype(BF16)
        v_ref[...] = zkv_ref[:, KVW:].astype(BF16)

    return pl.pallas_call(
        body, name=name, grid=(S // tr,),
        in_specs=[pl.BlockSpec((tr, AW), lambda i: (i, 0)),
                  pl.BlockSpec((tr, 2 * KVW), lambda i: (i, kv_blk)),
                  _vec_spec(l, HEAD), _vec_spec(l, HEAD),
                  pl.BlockSpec((tr, HEAD), lambda i: (i, 0)),
                  pl.BlockSpec((tr, HEAD), lambda i: (i, 0))],
        out_specs=[pl.BlockSpec((tr, AW), lambda i: (i, 0)),
                   pl.BlockSpec((tr, KVW), lambda i: (i, 0)),
                   pl.BlockSpec((tr, KVW), lambda i: (i, 0))],
        out_shape=[jax.ShapeDtypeStruct((S, AW), BF16),
                   jax.ShapeDtypeStruct((S, KVW), BF16),
                   jax.ShapeDtypeStruct((S, KVW), BF16)],
        compiler_params=_params(("arbitrary",)),
    )(z, z, qg3, kg3, cos2, sin2)


def _band_specs(width, nb):
    return [pl.BlockSpec((HEAD, width), lambda n: (jnp.maximum(n - 1, 0), 0)),
            pl.BlockSpec((HEAD, width), lambda n: (n, 0)),
            pl.BlockSpec((HEAD, width), lambda n: (jnp.minimum(n + 1, nb - 1), 0))]


def _attn_probs(qs, kj, n, sink_of_row, S, G):
    s = lax.dot_general(qs, kj, (((1,), (1,)), ((), ())), preferred_element_type=F32) * (HEAD ** -0.5)
    rows = lax.broadcasted_iota(jnp.int32, (G * HEAD, 3 * HEAD), 0)
    cols = lax.broadcasted_iota(jnp.int32, (G * HEAD, 3 * HEAD), 1)
    qi = rows & (HEAD - 1)
    kpos = n * HEAD - HEAD + cols
    valid = (cols >= qi) & (cols <= qi + 2 * HEAD) & (kpos >= 0) & (kpos < S)
    s = jnp.where(valid, s, MASK_VALUE)
    m = jnp.maximum(jnp.max(s, axis=-1, keepdims=True), sink_of_row)
    p = jnp.exp(s - m)
    e_sink = jnp.exp(sink_of_row - m)
    inv = 1.0 / (jnp.sum(p, axis=-1, keepdims=True) + e_sink)
    return p * inv, e_sink * inv


def _sink_rows(sink_ref, l, j, G):
    hidx = lax.broadcasted_iota(jnp.int32, (G * HEAD, 1), 0) // HEAD
    col = jnp.full((G * HEAD, 1), sink_ref[l, j * G], F32)
    for g in range(1, G):
        col = jnp.where(hidx == g, sink_ref[l, j * G + g], col)
    return col


def _attn_fwd(name, q, k, v, sink, ag3, l, cfg):
    S, AW, KVW, NKV, G = cfg["S"], cfg["AW"], cfg["KVW"], cfg["NKV"], cfg["G"]
    nb = S // HEAD

    def body(q_ref, kp, kc, kn, vp, vc, vn, sink_ref, ag_ref, a_ref, mix_ref):
        n = pl.program_id(0)
        kb = jnp.concatenate([kp[...], kc[...], kn[...]], axis=0)
        vb = jnp.concatenate([vp[...], vc[...], vn[...]], axis=0)
        for j in range(NKV):
            sl = slice(j * HEAD, (j + 1) * HEAD)
            qs = jnp.concatenate([q_ref[:, (j * G + g) * HEAD:(j * G + g + 1) * HEAD] for g in range(G)], axis=0)
            probs, _ = _attn_probs(qs, kb[:, sl], n, _sink_rows(sink_ref, l, j, G), S, G)
            o = jnp.dot(probs.astype(BF16), vb[:, sl], preferred_element_type=F32)
            for g in range(G):
                a_ref[:, (j * G + g) * HEAD:(j * G + g + 1) * HEAD] = o[g * HEAD:(g + 1) * HEAD]
        a = a_ref[...]
        r = lax.rsqrt(jnp.mean(a * a, axis=-1, keepdims=True) + EPS)
        mix_ref[...] = ((a * r) * ag_ref[...]).astype(BF16)

    return pl.pallas_call(
        body, name=name, grid=(nb,),
        in_specs=[pl.BlockSpec((HEAD, AW), lambda n: (n, 0))] + _band_specs(KVW, nb) + _band_specs(KVW, nb)
                 + [pl.BlockSpec(memory_space=pltpu.SMEM), _vec_spec(l, AW)],
        out_specs=[pl.BlockSpec((HEAD, AW), lambda n: (n, 0)), pl.BlockSpec((HEAD, AW), lambda n: (n, 0))],
        out_shape=[jax.ShapeDtypeStruct((S, AW), F32), jax.ShapeDtypeStruct((S, AW), BF16)],
        compiler_params=_params(("arbitrary",)),
    )(q, k, k, k, v, v, v, sink, ag3)


def _sgu_forward_math(gu, gv, lng, lnb):
    u = _gelu(gu)
    vv = _gelu(gv)
    mu = jnp.mean(vv, axis=-1, keepdims=True)
    xc = vv - mu
    rstd = lax.rsqrt(jnp.mean(xc * xc, axis=-1, keepdims=True) + EPS)
    xhat = xc * rstd
    vn = xhat * lng + lnb
    return u, xhat, rstd, vn


def _sgu_fwd(name, z, lng3, lnb3, ws_b, bs_b, og3, l, cfg):
    S, GW, NG = cfg["S"], cfg["GW"], cfg["NG"]

    def body(gu_ref, gv_ref, lng_ref, lnb_ref, ws_ref, bs_ref, og_ref, mix_ref, sg_ref):
        u, _, _, vn = _sgu_forward_math(gu_ref[...], gv_ref[...], lng_ref[...], lnb_ref[...])
        vnb = vn.astype(BF16)
        for h in range(NG):
            sl = slice(h * HEAD, (h + 1) * HEAD)
            f = jnp.dot(ws_ref[h], vnb[:, sl], preferred_element_type=F32) + bs_ref[h]
            sg_ref[:, sl] = u[:, sl] * f
        sg = sg_ref[...]
        r = lax.rsqrt(jnp.mean(sg * sg, axis=-1, keepdims=True) + EPS)
        mix_ref[...] = ((sg * r) * og_ref[...]).astype(BF16)

    return pl.pallas_call(
        body, name=name, grid=(S // HEAD,),
        in_specs=[pl.BlockSpec((HEAD, GW), lambda c: (c, 1)),
                  pl.BlockSpec((HEAD, GW), lambda c: (c, 2)),
                  _vec_spec(l, GW), _vec_spec(l, GW),
                  pl.BlockSpec((None, NG, HEAD, HEAD), lambda c: (l, 0, 0, 0)),
                  pl.BlockSpec((None, NG, HEAD, HEAD), lambda c: (l, 0, 0, 0)),
                  _vec_spec(l, GW)],
        out_specs=pl.BlockSpec((HEAD, GW), lambda c: (c, 0)),
        out_shape=jax.ShapeDtypeStruct((S, GW), BF16),
        scratch_shapes=[pltpu.VMEM((HEAD, GW), F32)],
        compiler_params=_params(("arbitrary",)),
    )(z, z, lng3, lnb3, ws_b, bs_b, og3)


CONV_HALO = 8
CONV_ROWS = 64


def _for_row_windows(S, fn):
    R, W = CONV_ROWS, CONV_ROWS + 2 * CONV_HALO
    n = S // R
    assert n * R == S and n >= 2
    fn(0, 0, 0)
    if n > 2:
        def mid(k, carry):
            fn(pl.multiple_of(k * R - CONV_HALO, CONV_HALO), CONV_HALO, pl.multiple_of(k * R, R))
            return carry

        lax.fori_loop(1, n - 1, mid, 0)
    fn(S - W, 2 * CONV_HALO, S - R)


def _shifts(t):
    W = t.shape[0]
    row = lax.broadcasted_iota(jnp.int32, t.shape, 0)
    dn = jnp.where(row == 0, 0.0, pltpu.roll(t, 1, axis=0))
    up = jnp.where(row == W - 1, 0.0, pltpu.roll(t, W - 1, axis=0))
    return dn, up


def _conv3(t, w, b):
    dn, up = _shifts(t)
    return ((b + dn * w[0:1]) + t * w[1:2]) + up * w[2:3], dn, up


def _conv_glu_fwd(name, ap, cw, cb3, l, cfg):
    S, F = cfg["S"], cfg["F"]
    tc = _tile(F, 256, 128)
    nf = F // tc
    R, W = CONV_ROWS, CONV_ROWS + 2 * CONV_HALO

    def body(g_ref, u_ref, wg_ref, wu_ref, bg_ref, bu_ref, y_ref):
        wg, wu, bg, bu = wg_ref[...], wu_ref[...], bg_ref[...], bu_ref[...]

        def window(start, lo, out0):
            rows = pl.ds(start, W)
            ag, _, _ = _conv3(g_ref[rows, :], wg, bg)
            au, _, _ = _conv3(u_ref[rows, :], wu, bu)
            y = (ag * _sigmoid(ag)) * au
            y_ref[pl.ds(out0, R), :] = y[lo:lo + R].astype(BF16)

        _for_row_windows(S, window)

    return pl.pallas_call(
        body, name=name, grid=(nf,),
        in_specs=[pl.BlockSpec((S, tc), lambda j: (0, j)),
                  pl.BlockSpec((S, tc), lambda j: (0, j + nf)),
                  pl.BlockSpec((None, 3, tc), lambda j: (l, 0, j)),
                  pl.BlockSpec((None, 3, tc), lambda j: (l, 0, j + nf)),
                  pl.BlockSpec((None, 1, tc), lambda j: (l, 0, j)),
                  pl.BlockSpec((None, 1, tc), lambda j: (l, 0, j + nf))],
        out_specs=pl.BlockSpec((S, tc), lambda j: (0, j)),
        out_shape=jax.ShapeDtypeStruct((S, F), BF16),
        compiler_params=_params(("arbitrary",)),
    )(ap, ap, cw, cw, cb3, cb3)


def _loss_bwd(name, y, target):
    S, D = y.shape
    tr = _tile(S, 256, 16)

    def body(y_ref, t_ref, loss_ref, d_ref, db_ref):
        @pl.when(pl.program_id(0) == 0)
        def _():
            loss_ref[...] = jnp.zeros_like(loss_ref)

        err = y_ref[...] - t_ref[...]
        part = 0.5 * jnp.sum(jnp.mean(err * err, axis=-1, keepdims=True), axis=0, keepdims=True)
        loss_ref[...] += jnp.broadcast_to(part, loss_ref.shape)
        d = err * (1.0 / D)
        d_ref[...] = d
        db_ref[...] = d.astype(BF16)

    return pl.pallas_call(
        body, name=name, grid=(S // tr,),
        in_specs=[pl.BlockSpec((tr, D), lambda i: (i, 0)), pl.BlockSpec((tr, D), lambda i: (i, 0))],
        out_specs=[pl.BlockSpec((8, 128), lambda i: (0, 0)),
                   pl.BlockSpec((tr, D), lambda i: (i, 0)),
                   pl.BlockSpec((tr, D), lambda i: (i, 0))],
        out_shape=[jax.ShapeDtypeStruct((8, 128), F32),
                   jax.ShapeDtypeStruct((S, D), F32),
                   jax.ShapeDtypeStruct((S, D), BF16)],
        compiler_params=_params(("arbitrary",)),
    )(y, target)


def _rms_bwd(name, x, dh, dres, g3, l):
    S, D = x.shape
    tr = _tile(S, 256, 16)

    def body(x_ref, dh_ref, dr_ref, g_ref, dx_ref, dxb_ref, dg_ref):
        @pl.when(pl.program_id(0) == 0)
        def _():
            dg_ref[...] = jnp.zeros_like(dg_ref)

        xv = x_ref[...]
        dhv = dh_ref[...]
        r = lax.rsqrt(jnp.mean(xv * xv, axis=-1, keepdims=True) + EPS)
        xhat = xv * r
        dhg = dhv * g_ref[...]
        dx = dr_ref[...] + r * (dhg - xhat * jnp.mean(dhg * xhat, axis=-1, keepdims=True))
        dx_ref[...] = dx
        dxb_ref[...] = dx.astype(BF16)
        dg_ref[...] += jnp.sum(dhv * xhat, axis=0, keepdims=True)

    return pl.pallas_call(
        body, name=name, grid=(S // tr,),
        in_specs=[pl.BlockSpec((tr, D), lambda i: (i, 0)), pl.BlockSpec((tr, D), lambda i: (i, 0)),
                  pl.BlockSpec((tr, D), lambda i: (i, 0)), _vec_spec(l, D)],
        out_specs=[pl.BlockSpec((tr, D), lambda i: (i, 0)), pl.BlockSpec((tr, D), lambda i: (i, 0)),
                   pl.BlockSpec((1, D), lambda i: (0, 0))],
        out_shape=[jax.ShapeDtypeStruct((S, D), F32), jax.ShapeDtypeStruct((S, D), BF16),
                   jax.ShapeDtypeStruct((1, D), F32)],
        compiler_params=_params(("arbitrary",)),
    )(x, dh, dres, g3)


def _glu_conv_bwd(name, dy, ap, cw, cb3, l, cfg):
    S, F = cfg["S"], cfg["F"]
    tc = 128
    nf = F // tc
    R, W = CONV_ROWS, CONV_ROWS + 2 * CONV_HALO

    def body(dy_ref, g_ref, u_ref, wg_ref, wu_ref, bg_ref, bu_ref, dap_ref, dw_ref, db_ref, acc):
        acc[...] = jnp.zeros_like(acc)
        wg, wu, bg, bu = wg_ref[...], wu_ref[...], bg_ref[...], bu_ref[...]

        def window(start, lo, out0):
            rows = pl.ds(start, W)
            apg, apu, dyv = g_ref[rows, :], u_ref[rows, :], dy_ref[rows, :]
            ag, g_dn, g_up = _conv3(apg, wg, bg)
            au, u_dn, u_up = _conv3(apu, wu, bu)
            sig = _sigmoid(ag)
            da_u = dyv * (ag * sig)
            da_g = (dyv * au) * (sig * (1.0 + ag * (1.0 - sig)))

            def back(da, w):
                prv, nxt = _shifts(da)
                return ((nxt * w[0:1] + da * w[1:2]) + prv * w[2:3])[lo:lo + R].astype(BF16)

            dap_ref[0, pl.ds(out0, R), :] = back(da_g, wg)
            dap_ref[1, pl.ds(out0, R), :] = back(da_u, wu)

            def add(i, prod):
                acc[i] += jnp.sum(prod[lo:lo + R].reshape(R // 8, 8, tc), axis=0)

            for i, other in enumerate([g_dn, apg, g_up]):
                add(i, da_g * other)
            for i, other in enumerate([u_dn, apu, u_up]):
                add(3 + i, da_u * other)
            add(6, da_g)
            add(7, da_u)

        _for_row_windows(S, window)
        col = [jnp.sum(acc[i], axis=0, keepdims=True) for i in range(8)]
        dw_ref[0] = jnp.concatenate(col[0:3], axis=0)
        dw_ref[1] = jnp.concatenate(col[3:6], axis=0)
        db_ref[0] = col[6]
        db_ref[1] = col[7]

    return pl.pallas_call(
        body, name=name, grid=(nf,),
        in_specs=[pl.BlockSpec((S, tc), lambda j: (0, j)),
                  pl.BlockSpec((S, tc), lambda j: (0, j)),
                  pl.BlockSpec((S, tc), lambda j: (0, j + nf)),
                  pl.BlockSpec((None, 3, tc), lambda j: (l, 0, j)),
                  pl.BlockSpec((None, 3, tc), lambda j: (l, 0, j + nf)),
                  pl.BlockSpec((None, 1, tc), lambda j: (l, 0, j)),
                  pl.BlockSpec((None, 1, tc), lambda j: (l, 0, j + nf))],
        out_specs=[pl.BlockSpec((2, S, tc), lambda j: (0, 0, j)),
                   pl.BlockSpec((2, 3, tc), lambda j: (0, 0, j)),
                   pl.BlockSpec((2, 1, tc), lambda j: (0, 0, j))],
        out_shape=[jax.ShapeDtypeStruct((2, S, F), BF16),
                   jax.ShapeDtypeStruct((2, 3, F), F32),
                   jax.ShapeDtypeStruct((2, 1, F), F32)],
        scratch_shapes=[pltpu.VMEM((8, 8, tc), F32)],
        compiler_params=_params(("arbitrary",)),
    )(dy, ap, ap, cw, cw, cb3, cb3)


def _attn_bwd(name, q, k, v, attn, dmix, sink, ag3, l, cfg):
    S, AW, KVW, NQ, NKV, G = cfg["S"], cfg["AW"], cfg["KVW"], cfg["NQ"], cfg["NKV"], cfg["G"]
    nb = S // HEAD
    scale = HEAD ** -0.5

    def body(q_ref, kp, kc, kn, vp, vc, vn, a_ref, dm_ref, sink_ref, ag_ref,
             dq_ref, dk_ref, dv_ref, dsink_ref, dag_ref, da_scr):
        n = pl.program_id(0)

        @pl.when(n == 0)
        def _():
            dk_ref[...] = jnp.zeros_like(dk_ref)
            dv_ref[...] = jnp.zeros_like(dv_ref)
            dsink_ref[...] = jnp.zeros_like(dsink_ref)
            dag_ref[...] = jnp.zeros_like(dag_ref)

        a = a_ref[...]
        dm = dm_ref[...]
        r = lax.rsqrt(jnp.mean(a * a, axis=-1, keepdims=True) + EPS)
        xhat = a * r
        dmg = dm * ag_ref[...]
        da_scr[...] = r * (dmg - xhat * jnp.mean(dmg * xhat, axis=-1, keepdims=True))
        dag_ref[...] += jnp.sum(dm * xhat, axis=0, keepdims=True)

        kb = jnp.concatenate([kp[...], kc[...], kn[...]], axis=0)
        vb = jnp.concatenate([vp[...], vc[...], vn[...]], axis=0)
        band = pl.ds(pl.multiple_of(n * HEAD, HEAD), 3 * HEAD)
        for j in range(NKV):
            sl = slice(j * HEAD, (j + 1) * HEAD)
            heads = [slice((j * G + g) * HEAD, (j * G + g + 1) * HEAD) for g in range(G)]
            qs = jnp.concatenate([q_ref[:, hs] for hs in heads], axis=0)
            do = jnp.concatenate([da_scr[:, hs] for hs in heads], axis=0)
            kj, vj = kb[:, sl], vb[:, sl]
            probs, p_sink = _attn_probs(qs, kj, n, _sink_rows(sink_ref, l, j, G), S, G)
            dob = do.astype(BF16)
            dprobs = lax.dot_general(dob, vj, (((1,), (1,)), ((), ())), preferred_element_type=F32)
            delta = jnp.sum(dprobs * probs, axis=-1, keepdims=True)
            ds = (probs * (dprobs - delta)) * scale
            dsb = ds.astype(BF16)
            dsk = -(p_sink * delta)
            dq = jnp.dot(dsb, kj, preferred_element_type=F32)
            for g in range(G):
                dq_ref[:, heads[g]] = dq[g * HEAD:(g + 1) * HEAD]
                part = jnp.sum(dsk[g * HEAD:(g + 1) * HEAD], axis=0, keepdims=True)
                dsink_ref[j * G + g:j * G + g + 1, :] += jnp.broadcast_to(part, (1, HEAD))
            dk_ref[band, sl] += lax.dot_general(dsb, qs, (((0,), (0,)), ((), ())), preferred_element_type=F32)
            dv_ref[band, sl] += lax.dot_general(probs.astype(BF16), dob, (((0,), (0,)), ((), ())),
                                                preferred_element_type=F32)

    return pl.pallas_call(
        body, name=name, grid=(nb,),
        in_specs=[pl.BlockSpec((HEAD, AW), lambda n: (n, 0))] + _band_specs(KVW, nb) + _band_specs(KVW, nb)
                 + [pl.BlockSpec((HEAD, AW), lambda n: (n, 0)),
                    pl.BlockSpec((HEAD, AW), lambda n: (n, 0)),
                    pl.BlockSpec(memory_space=pltpu.SMEM), _vec_spec(l, AW)],
        out_specs=[pl.BlockSpec((HEAD, AW), lambda n: (n, 0)),
                   pl.BlockSpec((S + 2 * HEAD, KVW), lambda n: (0, 0)),
                   pl.BlockSpec((S + 2 * HEAD, KVW), lambda n: (0, 0)),
                   pl.BlockSpec((NQ, HEAD), lambda n: (0, 0)),
                   pl.BlockSpec((1, AW), lambda n: (0, 0))],
        out_shape=[jax.ShapeDtypeStruct((S, AW), F32),
                   jax.ShapeDtypeStruct((S + 2 * HEAD, KVW), F32),
                   jax.ShapeDtypeStruct((S + 2 * HEAD, KVW), F32),
                   jax.ShapeDtypeStruct((NQ, HEAD), F32),
                   jax.ShapeDtypeStruct((1, AW), F32)],
        scratch_shapes=[pltpu.VMEM((HEAD, AW), F32)],
        compiler_params=_params(("arbitrary",)),
    )(q, k, k, k, v, v, v, attn, dmix, sink, ag3)


def _qkv_prep_bwd(name, z, dq, dk_pad, dv_pad, qg3, kg3, cos2, sin2, l, cfg):
    S, AW, KVW, NQ, NKV = cfg["S"], cfg["AW"], cfg["KVW"], cfg["NQ"], cfg["NKV"]
    kv_blk = (3 * AW) // (2 * KVW)

    def body(zq_ref, zkv_ref, dq_ref, dk_ref, dv_ref, qg_ref, kg_ref, c_ref, s_ref,
             dzq_ref, dzkv_ref, dqg_ref, dkg_ref):
        @pl.when(pl.program_id(0) == 0)
        def _():
            dqg_ref[...] = jnp.zeros_like(dqg_ref)
            dkg_ref[...] = jnp.zeros_like(dkg_ref)

        cosv, sinv = c_ref[...], s_ref[...]

        def back(t, dr, g):
            r = lax.rsqrt(jnp.mean(t * t, axis=-1, keepdims=True) + EPS)
            xhat = t * r
            dn = dr * cosv + pltpu.roll(dr * sinv, HEAD // 2, axis=1)
            dxh = dn * g
            dt = r * (dxh - xhat * jnp.mean(dxh * xhat, axis=-1, keepdims=True))
            return dt, jnp.sum(dn * xhat, axis=0, keepdims=True)

        gq = jnp.zeros((1, HEAD), F32)
        for h in range(NQ):
            sl = slice(h * HEAD, (h + 1) * HEAD)
            dt, gpart = back(zq_ref[:, sl], dq_ref[:, sl], qg_ref[...])
            dzq_ref[:, sl] = dt.astype(BF16)
            gq = gq + gpart
        dqg_ref[...] += gq
        gk = jnp.zeros((1, HEAD), F32)
        for h in range(NKV):
            sl = slice(h * HEAD, (h + 1) * HEAD)
            dt, gpart = back(zkv_ref[:, sl], dk_ref[:, sl], kg_ref[...])
            dzkv_ref[:, sl] = dt.astype(BF16)
            gk = gk + gpart
        dkg_ref[...] += gk
        dzkv_ref[:, KVW:] = dv_ref[...].astype(BF16)

    return pl.pallas_call(
        body, name=name, grid=(S // HEAD,),
        in_specs=[pl.BlockSpec((HEAD, AW), lambda i: (i, 0)),
                  pl.BlockSpec((HEAD, 2 * KVW), lambda i: (i, kv_blk)),
                  pl.BlockSpec((HEAD, AW), lambda i: (i, 0)),
                  pl.BlockSpec((HEAD, KVW), lambda i: (i + 1, 0)),
                  pl.BlockSpec((HEAD, KVW), lambda i: (i + 1, 0)),
                  _vec_spec(l, HEAD), _vec_spec(l, HEAD),
                  pl.BlockSpec((HEAD, HEAD), lambda i: (i, 0)),
                  pl.BlockSpec((HEAD, HEAD), lambda i: (i, 0))],
        out_specs=[pl.BlockSpec((HEAD, AW), lambda i: (i, 0)),
                   pl.BlockSpec((HEAD, 2 * KVW), lambda i: (i, 0)),
                   pl.BlockSpec((1, HEAD), lambda i: (0, 0)),
                   pl.BlockSpec((1, HEAD), lambda i: (0, 0))],
        out_shape=[jax.ShapeDtypeStruct((S, AW), BF16),
                   jax.ShapeDtypeStruct((S, 2 * KVW), BF16),
                   jax.ShapeDtypeStruct((1, HEAD), F32),
                   jax.ShapeDtypeStruct((1, HEAD), F32)],
        compiler_params=_params(("arbitrary",)),
    )(z, z, dq, dk_pad, dv_pad, qg3, kg3, cos2, sin2)


def _sgu_bwd(name, z, dmix, lng3, lnb3, ws_b, wst_b, bs_b, og3, l, cfg):
    S, GW, NG = cfg["S"], cfg["GW"], cfg["NG"]

    def body(gu_ref, gv_ref, dm_ref, lng_ref, lnb_ref, ws_ref, wst_ref, bs_ref, og_ref,
             dgu_ref, dgv_ref, dws_ref, dbs_ref, dlng_ref, dlnb_ref, dog_ref, sg_scr, f_scr, dvn_scr):
        @pl.when(pl.program_id(0) == 0)
        def _():
            dws_ref[...] = jnp.zeros_like(dws_ref)
            dbs_ref[...] = jnp.zeros_like(dbs_ref)
            dlng_ref[...] = jnp.zeros_like(dlng_ref)
            dlnb_ref[...] = jnp.zeros_like(dlnb_ref)
            dog_ref[...] = jnp.zeros_like(dog_ref)

        gu, gv = gu_ref[...], gv_ref[...]
        lng = lng_ref[...]
        u, xhat, rstd, vn = _sgu_forward_math(gu, gv, lng, lnb_ref[...])
        vnb = vn.astype(BF16)
        for h in range(NG):
            sl = slice(h * HEAD, (h + 1) * HEAD)
            f = jnp.dot(ws_ref[h], vnb[:, sl], preferred_element_type=F32) + bs_ref[h]
            f_scr[:, sl] = f
            sg_scr[:, sl] = u[:, sl] * f
        sg = sg_scr[...]
        dm = dm_ref[...]
        r = lax.rsqrt(jnp.mean(sg * sg, axis=-1, keepdims=True) + EPS)
        sghat = sg * r
        dmg = dm * og_ref[...]
        dsg = r * (dmg - sghat * jnp.mean(dmg * sghat, axis=-1, keepdims=True))
        dog_ref[...] += jnp.sum(dm * sghat, axis=0, keepdims=True)
        du = dsg * f_scr[...]
        df = dsg * u
        dfb = df.astype(BF16)
        for h in range(NG):
            sl = slice(h * HEAD, (h + 1) * HEAD)
            dvn_scr[:, sl] = jnp.dot(wst_ref[h], dfb[:, sl], preferred_element_type=F32)
            dws_ref[h] += lax.dot_general(dfb[:, sl], vnb[:, sl], (((1,), (1,)), ((), ())),
                                          preferred_element_type=F32)
            dbs_ref[h] += jnp.broadcast_to(jnp.sum(df[:, sl], axis=-1, keepdims=True), (HEAD, HEAD))
        dvn = dvn_scr[...]
        dlng_ref[...] += jnp.sum(dvn * xhat, axis=0, keepdims=True)
        dlnb_ref[...] += jnp.sum(dvn, axis=0, keepdims=True)
        dxh = dvn * lng
        dvv = rstd * ((dxh - jnp.mean(dxh, axis=-1, keepdims=True))
                      - xhat * jnp.mean(dxh * xhat, axis=-1, keepdims=True))
        dgu_ref[...] = (du * _gelu_grad(gu)).astype(BF16)
        dgv_ref[...] = (dvv * _gelu_grad(gv)).astype(BF16)

    vec = pl.BlockSpec((1, GW), lambda c: (0, 0))
    mat = pl.BlockSpec((NG, HEAD, HEAD), lambda c: (0, 0, 0))
    wsp = pl.BlockSpec((None, NG, HEAD, HEAD), lambda c: (l, 0, 0, 0))
    return pl.pallas_call(
        body, name=name, grid=(S // HEAD,),
        in_specs=[pl.BlockSpec((HEAD, GW), lambda c: (c, 1)),
                  pl.BlockSpec((HEAD, GW), lambda c: (c, 2)),
                  pl.BlockSpec((HEAD, GW), lambda c: (c, 1)),
                  _vec_spec(l, GW), _vec_spec(l, GW), wsp, wsp, wsp, _vec_spec(l, GW)],
        out_specs=[pl.BlockSpec((HEAD, GW), lambda c: (c, 0)), pl.BlockSpec((HEAD, GW), lambda c: (c, 0)),
                   mat, mat, vec, vec, vec],
        out_shape=[jax.ShapeDtypeStruct((S, GW), BF16), jax.ShapeDtypeStruct((S, GW), BF16),
                   jax.ShapeDtypeStruct((NG, HEAD, HEAD), F32), jax.ShapeDtypeStruct((NG, HEAD, HEAD), F32),
                   jax.ShapeDtypeStruct((1, GW), F32), jax.ShapeDtypeStruct((1, GW), F32),
                   jax.ShapeDtypeStruct((1, GW), F32)],
        scratch_shapes=[pltpu.VMEM((HEAD, GW), F32), pltpu.VMEM((HEAD, GW), F32), pltpu.VMEM((HEAD, GW), F32)],
        compiler_params=_params(("arbitrary",)),
    )(z, z, dmix, lng3, lnb3, ws_b, wst_b, bs_b, og3)


def _mesh_pos():
    x, y, c = lax.axis_index("x"), lax.axis_index("y"), lax.axis_index("c")
    return x, y, c


def _dev_index(p):
    return 4 * p[0] + 2 * p[1] + p[2]


def _handshake(peers):
    barrier = pltpu.get_barrier_semaphore()
    for p in peers:
        pl.semaphore_signal(barrier, inc=1, device_id=p, device_id_type=MESH)
    pl.semaphore_wait(barrier, len(peers))


def _gather_body(na, slabs):
    def body(*refs):
        ins, outs = refs[:na], refs[na:2 * na]
        send_sems, recv_sems, local_sems = refs[2 * na:]
        x, y, c = _mesh_pos()
        me, sib = (x, y, c), (x, y, 1 - c)
        chips = [(1 - x, y), (x, 1 - y), (1 - x, 1 - y)]
        _handshake([sib] + [(*chip, c) for chip in chips])

        def copy(a, k, block, to, src=None):
            dst = slabs[a](outs[a], _dev_index(block))
            return pltpu.make_async_remote_copy(
                src_ref=dst if src is None else src, dst_ref=dst,
                send_sem=send_sems.at[7 * a + k], recv_sem=recv_sems.at[7 * a + k],
                device_id=to, device_id_type=MESH)

        local = [pltpu.make_async_copy(ins[a], slabs[a](outs[a], _dev_index(me)), local_sems.at[a])
                 for a in range(na)]
        for cp in local:
            cp.start()
        first = []
        for a in range(na):
            first.append(copy(a, 0, me, sib, src=ins[a]))
            first += [copy(a, 1 + j, me, (*chip, c), src=ins[a]) for j, chip in enumerate(chips)]
        for cp in first:
            cp.start()
        passed = []
        for j, chip in enumerate(chips):
            for a in range(na):
                copy(a, 1 + j, (*chip, c), me).wait_recv()
                fwd = copy(a, 4 + j, (*chip, c), sib)
                fwd.start()
                passed.append(fwd)
        for a in range(na):
            copy(a, 0, sib, me).wait_recv()
            for j, chip in enumerate(chips):
                copy(a, 4 + j, (*chip, 1 - c), me).wait_recv()
        for cp in first + passed:
            cp.wait_send()
        for cp in local:
            cp.wait()

    return body


def _all_gather(name, cid, shards, out_shapes, slabs):
    na = len(shards)
    return pl.kernel(
        _gather_body(na, slabs), out_type=out_shapes,
        mesh=plsc.ScalarSubcoreMesh(axis_name="seq", num_cores=1), name=name,
        scratch_types=[pltpu.SemaphoreType.DMA((7 * na,)), pltpu.SemaphoreType.DMA((7 * na,)),
                       pltpu.SemaphoreType.DMA((na,))],
        compiler_params=pltpu.CompilerParams(collective_id=cid),
    )(*shards)


def _d2d_body(nb, ns):
    n = nb + ns

    def body(*refs):
        ins, outs = refs[:n], refs[n:2 * n]
        send_sems, recv_sems = refs[2 * n:]
        x, y, c = _mesh_pos()
        sib = (x, y, 1 - c)
        _handshake([sib])
        copies = []
        for t in range(n):
            cp = pltpu.make_async_remote_copy(
                src_ref=ins[t].at[:, 1 - c] if t < nb else ins[t], dst_ref=outs[t],
                send_sem=send_sems.at[t], recv_sem=recv_sems.at[t],
                device_id=sib, device_id_type=MESH)
            cp.start()
            copies.append(cp)
        for cp in copies:
            cp.wait()

    return body


def _rs_d2d(name, cid, bigs, smalls):
    shapes = [jax.ShapeDtypeStruct((g.shape[0],) + g.shape[2:], g.dtype) for g in bigs]
    shapes += [jax.ShapeDtypeStruct(s.shape, s.dtype) for s in smalls]
    n = len(shapes)
    return pl.kernel(
        _d2d_body(len(bigs), len(smalls)), out_type=shapes,
        mesh=plsc.ScalarSubcoreMesh(axis_name="seq", num_cores=1), name=name,
        scratch_types=[pltpu.SemaphoreType.DMA((n,)), pltpu.SemaphoreType.DMA((n,))],
        compiler_params=pltpu.CompilerParams(collective_id=cid),
    )(*bigs, *smalls)


def _pair_sum(name, g4, recv, cvec, row_mult):
    Q, _, R, C = g4.shape
    tr = _tile(R, 256, row_mult)

    def body(c_ref, a_ref, b_ref, o_ref):
        o_ref[...] = (a_ref[...].astype(F32) + b_ref[...].astype(F32)).astype(BF16)

    grid_spec = pltpu.PrefetchScalarGridSpec(
        num_scalar_prefetch=1, grid=(Q, R // tr),
        in_specs=[pl.BlockSpec((None, None, tr, C), lambda q, i, c_ref: (q, c_ref[0], i, 0)),
                  pl.BlockSpec((None, tr, C), lambda q, i, c_ref: (q, i, 0))],
        out_specs=pl.BlockSpec((None, tr, C), lambda q, i, c_ref: (q, i, 0)))
    return pl.pallas_call(
        body, name=name, grid_spec=grid_spec, out_shape=jax.ShapeDtypeStruct((Q, R, C), BF16),
        compiler_params=_params(("arbitrary", "arbitrary")),
    )(cvec, g4, recv)


def _add2(name, a, b):
    R, C = a.shape
    tr = _tile(R, 512, 8)

    def body(a_ref, b_ref, o_ref):
        o_ref[...] = a_ref[...] + b_ref[...]

    blk = pl.BlockSpec((tr, C), lambda i: (i, 0))
    return pl.pallas_call(body, name=name, grid=(R // tr,), in_specs=[blk, blk], out_specs=blk,
                          out_shape=jax.ShapeDtypeStruct((R, C), a.dtype),
                          compiler_params=_params(("arbitrary",)))(a, b)


def _ici_copies(nb, n, srcs, lands, send_sems, recv_sems):
    x, y, c = _mesh_pos()
    q_me = 2 * x + y
    copies = []
    for t in range(n):
        for k in range(1, 4):
            px, py = x ^ (k >> 1), y ^ (k & 1)
            copies.append(pltpu.make_async_remote_copy(
                src_ref=srcs[t].at[2 * px + py] if t < nb else srcs[t], dst_ref=lands[t].at[q_me],
                send_sem=send_sems.at[3 * t + k - 1], recv_sem=recv_sems.at[3 * t + k - 1],
                device_id=(px, py, c), device_id_type=MESH))
    return copies


_HBM = pl.BlockSpec(memory_space=pltpu.HBM)
_SEM = pl.BlockSpec(memory_space=pltpu.SEMAPHORE)
_DATAFLOW = pltpu.SideEffectType.DATAFLOW_SIDE_EFFECTING


def _ici_start(name, bigs, smalls):
    nb, n = len(bigs), len(bigs) + len(smalls)
    srcs = list(bigs) + list(smalls)
    lands = [lax.empty(g.shape, g.dtype) for g in bigs] + [lax.empty((N_DEV // 2,) + s.shape, s.dtype) for s in smalls]

    def body(*refs):
        src_refs, land_refs = refs[:n], refs[n:2 * n]
        send_sems, recv_sems = refs[2 * n], refs[2 * n + 1]
        token, local_sems = refs[-2], refs[-1]
        x, y, _ = _mesh_pos()
        q_me = 2 * x + y
        own = [pltpu.make_async_copy(src_refs[t].at[q_me] if t < nb else src_refs[t], land_refs[t].at[q_me],
                                     local_sems.at[t]) for t in range(n)]
        for cp in own:
            cp.start()
        for cp in _ici_copies(nb, n, src_refs, land_refs, send_sems, recv_sems):
            cp.start()
        for cp in own:
            cp.wait()
        token[...] = jnp.zeros_like(token)

    hbm = [pltpu.HBM(a.shape, a.dtype) for a in srcs + lands]
    args = [pltpu.with_memory_space_constraint(a, pltpu.HBM) for a in srcs + lands]
    out = pl.pallas_call(
        body, name=name,
        out_shape=[pltpu.SemaphoreType.DMA((3 * n,)), pltpu.SemaphoreType.DMA((3 * n,))] + hbm
                  + [jax.ShapeDtypeStruct((8, 128), F32)],
        in_specs=[_HBM] * (2 * n),
        out_specs=[_SEM, _SEM] + [_HBM] * (2 * n) + [pl.BlockSpec(memory_space=pltpu.VMEM)],
        input_output_aliases={i: 2 + i for i in range(2 * n)},
        scratch_shapes=[pltpu.SemaphoreType.DMA((n,))],
        compiler_params=pltpu.CompilerParams(has_side_effects=_DATAFLOW),
    )(*args)
    return out[0], out[1], list(out[2:2 + n]), list(out[2 + n:2 + 2 * n]), out[-1]


def _ici_wait(name, started, nb, not_before):
    send_sems, recv_sems, srcs, lands, token = started
    n = len(srcs)

    def body(*refs):
        src_refs, land_refs = refs[:n], refs[n:2 * n]
        send_refs, recv_refs = refs[2 * n], refs[2 * n + 1]
        for cp in _ici_copies(nb, n, src_refs, land_refs, send_refs, recv_refs):
            cp.wait_send()
            cp.wait_recv()

    hbm = [pltpu.HBM(a.shape, a.dtype) for a in srcs + lands]
    out = pl.pallas_call(
        body, name=name, out_shape=hbm,
        in_specs=[_HBM] * (2 * n) + [_SEM, _SEM] + [pl.BlockSpec(memory_space=pl.ANY)] * 2,
        out_specs=[_HBM] * (2 * n),
        input_output_aliases={i: i for i in range(2 * n)},
        compiler_params=pltpu.CompilerParams(has_side_effects=_DATAFLOW),
    )(*srcs, *lands, send_sems, recv_sems, not_before, token)
    return list(out[n:])


def _adamw_math(w, g, m, v):
    m2 = ADAM_B1 * m + (1.0 - ADAM_B1) * g
    v2 = ADAM_B2 * v + (1.0 - ADAM_B2) * (g * g)
    m_hat = m2 / (1.0 - ADAM_B1 ** ADAM_STEP)
    v_hat = v2 / (1.0 - ADAM_B2 ** ADAM_STEP)
    delta = -ADAM_LR * (m_hat / (jnp.sqrt(v_hat) + ADAM_EPS) + ADAM_WD * w)
    return delta, m2, v2


def _adamw_sum(name, recv, w, m, v, l, prev, row_mult):
    NS, R, C = recv.shape
    L = w.shape[0]
    tr = _tile(R, 128, row_mult)

    def body(*refs):
        r_ref, w_ref, m_ref, v_ref = refs[:4]
        g_ref, d_ref, nm_ref, nv_ref = refs[-4:]
        g = r_ref[0].astype(F32)
        for s in range(1, NS):
            g = g + r_ref[s].astype(F32)
        d, m2, v2 = _adamw_math(w_ref[...], g, m_ref[...], v_ref[...])
        g_ref[...] = g
        d_ref[...] = d
        nm_ref[...] = m2
        nv_ref[...] = v2

    blk = pl.BlockSpec((None, tr, C), lambda i: (l, i, 0))
    shp = jax.ShapeDtypeStruct((L, R, C), F32)
    in_specs = [pl.BlockSpec((NS, tr, C), lambda i: (0, i, 0)), blk, blk, blk]
    args = [recv, w, m, v]
    aliases = {}
    if prev is not None:
        in_specs += [pl.BlockSpec(memory_space=pl.ANY)] * 4
        args += list(prev)
        aliases = {4 + i: i for i in range(4)}
    return pl.pallas_call(
        body, name=name, grid=(R // tr,),
        in_specs=in_specs, out_specs=[blk, blk, blk, blk], out_shape=[shp, shp, shp, shp],
        input_output_aliases=aliases,
        compiler_params=_params(("arbitrary",)),
    )(*args)


def _adamw_plain(name, g, w, m, v):
    def body(g_ref, w_ref, m_ref, v_ref, d_ref, nm_ref, nv_ref):
        d, m2, v2 = _adamw_math(w_ref[...], g_ref[...], m_ref[...], v_ref[...])
        d_ref[...] = d
        nm_ref[...] = m2
        nv_ref[...] = v2

    shp = jax.ShapeDtypeStruct(g.shape, F32)
    return pl.pallas_call(body, name=name, out_shape=[shp, shp, shp], compiler_params=_params())(g, w, m, v)


SMALL = ["norm1_g", "q_norm_g", "k_norm_g", "sink", "sgu_ln_g", "sgu_ln_b", "w_s", "b_s",
         "attn_out_g", "sgu_out_g", "norm2_g", "conv_b"]
PACK_ALIGN = 1024


def _pack(pieces):
    flat = []
    for p in pieces:
        f = p.reshape(-1).astype(F32)
        pad = (-f.shape[0]) % PACK_ALIGN
        flat.append(jnp.pad(f, (0, pad)) if pad else f)
    return jnp.concatenate(flat).reshape(-1, 128)


def _unpack(packed, shapes):
    flat = packed.reshape(-1)
    out, off = [], 0
    for shp in shapes:
        n = 1
        for d in shp:
            n *= d
        out.append(flat[off:off + n].reshape(shp))
        off += n + ((-n) % PACK_ALIGN)
    return out


def kernel(x, norm1_g, w_in, q_norm_g, k_norm_g, sink, sgu_ln_g, sgu_ln_b, w_s, b_s, attn_out_g, sgu_out_g, w_o, norm2_g, w_up, conv_w, conv_b, w_down, loss_target, m_norm1_g, m_w_in, m_q_norm_g, m_k_norm_g, m_sink, m_sgu_ln_g, m_sgu_ln_b, m_w_s, m_b_s, m_attn_out_g, m_sgu_out_g, m_w_o, m_norm2_g, m_w_up, m_conv_w, m_conv_b, m_w_down, v_norm1_g, v_w_in, v_q_norm_g, v_k_norm_g, v_sink, v_sgu_ln_g, v_sgu_ln_b, v_w_s, v_b_s, v_attn_out_g, v_sgu_out_g, v_w_o, v_norm2_g, v_w_up, v_conv_w, v_conv_b, v_w_down):
    weights = dict(norm1_g=norm1_g, w_in=w_in, q_norm_g=q_norm_g, k_norm_g=k_norm_g, sink=sink, sgu_ln_g=sgu_ln_g,
                   sgu_ln_b=sgu_ln_b, w_s=w_s, b_s=b_s, attn_out_g=attn_out_g, sgu_out_g=sgu_out_g, w_o=w_o,
                   norm2_g=norm2_g, w_up=w_up, conv_w=conv_w, conv_b=conv_b, w_down=w_down)
    mom_m = dict(norm1_g=m_norm1_g, w_in=m_w_in, q_norm_g=m_q_norm_g, k_norm_g=m_k_norm_g, sink=m_sink,
                 sgu_ln_g=m_sgu_ln_g, sgu_ln_b=m_sgu_ln_b, w_s=m_w_s, b_s=m_b_s, attn_out_g=m_attn_out_g,
                 sgu_out_g=m_sgu_out_g, w_o=m_w_o, norm2_g=m_norm2_g, w_up=m_w_up, conv_w=m_conv_w,
                 conv_b=m_conv_b, w_down=m_w_down)
    mom_v = dict(norm1_g=v_norm1_g, w_in=v_w_in, q_norm_g=v_q_norm_g, k_norm_g=v_k_norm_g, sink=v_sink,
                 sgu_ln_g=v_sgu_ln_g, sgu_ln_b=v_sgu_ln_b, w_s=v_w_s, b_s=v_b_s, attn_out_g=v_attn_out_g,
                 sgu_out_g=v_sgu_out_g, w_o=v_w_o, norm2_g=v_norm2_g, w_up=v_w_up, conv_w=v_conv_w,
                 conv_b=v_conv_b, w_down=v_w_down)
    order = ["norm1_g", "w_in", "q_norm_g", "k_norm_g", "sink", "sgu_ln_g", "sgu_ln_b", "w_s", "b_s",
             "attn_out_g", "sgu_out_g", "w_o", "norm2_g", "w_up", "conv_w", "conv_b", "w_down"]

    _, S, D = x.shape
    L = w_in.shape[0]
    AW = D // 2
    NQ = AW // HEAD
    NKV = max(1, NQ // 4)
    G = NQ // NKV
    KVW = NKV * HEAD
    GW = D - AW
    NG = GW // HEAD
    IN = AW + 2 * KVW + 2 * GW
    INS = w_in.shape[2]
    OS = w_o.shape[1]
    US = w_up.shape[2]
    DS = w_down.shape[1]
    F2 = US * N_DEV
    F = F2 // 2
    assert INS * N_DEV == IN and OS * N_DEV == D and DS * N_DEV == F and AW == GW and (3 * AW) % (2 * KVW) == 0
    cfg = dict(S=S, D=D, AW=AW, NQ=NQ, NKV=NKV, G=G, KVW=KVW, GW=GW, NG=NG, F=F, F2=F2)

    def rows_slab(n):
        return lambda ref, idx: ref.at[:, pl.ds(pl.multiple_of(idx * n, n), n), :]

    def cols_slab(n):
        return lambda ref, idx: ref.at[:, :, pl.ds(pl.multiple_of(idx * n, n), n)]

    def after(first, then):
        return lax.optimization_barrier((first, then))

    w_in_t, m_w_in_t, v_w_in_t = (jnp.swapaxes(a, 1, 2) for a in (w_in, m_w_in, v_w_in))

    ids = iter(range(64))
    w_in_f, w_o_f, w_up_f, w_down_f = [], [], [], []

    def gather_layer(l, not_before=None):
        in_shard = w_in_t[l:l + 1].astype(BF16)
        if not_before is not None:
            _, in_shard = after(not_before, in_shard)
        w_in_f.extend(_all_gather("ag_in", next(ids), [in_shard],
                                  [jax.ShapeDtypeStruct((1, IN, D), BF16)], [rows_slab(INS)]))
        w_o_f.extend(_all_gather("ag_o", next(ids), [w_o[l:l + 1].astype(BF16)],
                                 [jax.ShapeDtypeStruct((1, D, D), BF16)], [rows_slab(OS)]))
        w_up_f.extend(_all_gather("ag_up", next(ids), [w_up[l:l + 1].astype(BF16)],
                                  [jax.ShapeDtypeStruct((1, D, F2), BF16)], [cols_slab(US)]))
        w_down_f.extend(_all_gather("ag_down", next(ids), [w_down[l:l + 1].astype(BF16)],
                                    [jax.ShapeDtypeStruct((1, F, D), BF16)], [rows_slab(DS)]))

    gather_layer(0)
    (conv_w_f,) = _all_gather("ag_conv_w", next(ids), [conv_w], [jax.ShapeDtypeStruct((L, 3, F2), F32)],
                              [cols_slab(US)])

    n1g3, n2g3 = norm1_g.reshape(L, 1, D), norm2_g.reshape(L, 1, D)
    qg3, kg3 = q_norm_g.reshape(L, 1, HEAD), k_norm_g.reshape(L, 1, HEAD)
    lng3, lnb3 = sgu_ln_g.reshape(L, 1, GW), sgu_ln_b.reshape(L, 1, GW)
    ag3, og3 = attn_out_g.reshape(L, 1, AW), sgu_out_g.reshape(L, 1, GW)
    cb3 = conv_b.reshape(L, 1, F2)
    ws_b = w_s.astype(BF16)
    wst_b = jnp.swapaxes(w_s, 2, 3).astype(BF16)
    bs_b = jnp.broadcast_to(b_s[..., None], (L, NG, HEAD, HEAD))
    inv_freq = ROPE_THETA ** (-jnp.arange(0, HEAD, 2, dtype=F32) / HEAD)
    ang = jnp.arange(S, dtype=F32)[:, None] * inv_freq[None, :]
    cos2 = jnp.concatenate([jnp.cos(ang), jnp.cos(ang)], axis=1)
    sin2 = jnp.concatenate([-jnp.sin(ang), jnp.sin(ang)], axis=1)

    tn = 512
    t_in, t_d, t_f, t_f2 = _tile(IN, tn, 128), _tile(D, tn, 128), _tile(F, tn, 128), _tile(F2, tn, 128)
    tm_f = _tile(F, 512, 128)
    tm_s = _tile(S, 1024, 128)
    assert AW % t_in == 0 and (2 * KVW) % t_in == 0
    n_q, n_kv, n_g = AW // t_in, (2 * KVW) // t_in, (2 * GW) // t_in

    def in_tile(j):
        return jnp.where(j < n_q, j, jnp.where(j < n_q + n_g, j + n_kv, j - n_g))

    def w_spec(tk, tn_):
        return pl.BlockSpec((None, tk, tn_), lambda i, j, k: (0, k, j))

    def wt_spec(tn_, tk):
        return pl.BlockSpec((None, tn_, tk), lambda i, j, k: (0, j, k))

    def a_spec(tm, tk):
        return pl.BlockSpec((tm, tk), lambda i, j, k: (i, k))

    def at_spec(tk, tm):
        return pl.BlockSpec((tk, tm), lambda i, j, k: (k, i))

    def b_spec(tk, tn_):
        return pl.BlockSpec((tk, tn_), lambda i, j, k: (k, j))

    xs = x.reshape(S, D)
    saved = []
    cur = xs
    for l in range(L):
        h = _rms_fwd("rms1_fwd", cur, n1g3, l)
        z = _mm("mm_in", h, w_in_f[l], M=S, N=IN, K=D, tm=S, tn=t_in, tk=D, tb=True,
                a_spec=a_spec(S, D),
                b_spec=pl.BlockSpec((None, t_in, D), lambda i, j, k: (0, in_tile(j), 0)))
        q_r, k_r, v_b = _qkv_prep("qkv_prep", z, qg3, kg3, cos2, sin2, l, cfg)
        attn, mix_l = _attn_fwd("attn_fwd", q_r, k_r, v_b, sink, ag3, l, cfg)
        mix_r = _sgu_fwd("sgu_fwd", z, lng3, lnb3, ws_b, bs_b, og3, l, cfg)
        mixed = jnp.concatenate([mix_l, mix_r], axis=1)
        x1 = _mm("mm_o", mixed, w_o_f[l], M=S, N=D, K=D, tm=S, tn=t_d, tk=D,
                 a_spec=a_spec(S, D), b_spec=w_spec(D, t_d), res=cur)
        if l + 1 < L:
            gather_layer(l + 1, not_before=x1)
        h2 = _rms_fwd("rms2_fwd", x1, n2g3, l)
        ap = _mm("mm_up", h2, w_up_f[l], M=S, N=F2, K=D, tm=S, tn=t_f2, tk=D,
                 a_spec=a_spec(S, D), b_spec=w_spec(D, t_f2))
        y_b = _conv_glu_fwd("conv_glu_fwd", ap, conv_w_f, cb3, l, cfg)
        x2 = _mm("mm_down", y_b, w_down_f[l], M=S, N=D, K=F, tm=tm_s, tn=t_d, tk=F,
                 a_spec=a_spec(tm_s, F), b_spec=w_spec(F, t_d), res=x1)
        saved.append(dict(x=cur, h=h, z=z, q=q_r, k=k_r, v=v_b, attn=attn, mixed=mixed, x1=x1, h2=h2, ap=ap, y=y_b))
        cur = x2

    loss_tile, dx, dxb = _loss_bwd("loss", cur, loss_target.reshape(S, D))
    loss = lax.psum(loss_tile[0, 0], ("x", "y", "c"))

    gS = [dict() for _ in range(L)]
    recv = [dict() for _ in range(L)]
    cvec = jnp.reshape(lax.axis_index("c"), (1,)).astype(jnp.int32)

    def rs_swap(name, g, R, C, small=None):
        g4 = g.reshape(N_DEV // 2, 2, R, C)
        got = _rs_d2d("rs_d2d_" + name, next(ids), [g4], [] if small is None else [small])
        return name, g4, got, small

    started = []

    def rs_finish(l, pending, then=None):
        name, g4, got, small = pending
        chip = _pair_sum("pair_sum_" + name, g4, got[0], cvec, 16)
        chip_small = [] if small is None else [_add2("pair_sum_small", small, got[1])]
        begun = _ici_start("rs_ici_" + name, [chip], chip_small)
        if then is not None:
            token, then = after(begun[4], then)
            begun = begun[:4] + (token,)
        started.append((l, name, begun))
        return then

    carried = None
    for l in reversed(range(L)):
        sv = saved[l]
        dy = _mm("mm_dy", dxb, w_down_f[l], M=S, N=F, K=D, tm=S, tn=t_f, tk=D, tb=True,
                 a_spec=a_spec(S, D), b_spec=wt_spec(t_f, D))
        if carried is not None:
            dy = rs_finish(l + 1, carried, dy)
        g_down = _mm("mm_gdown", sv["y"], dxb, M=F, N=D, K=S, tm=tm_f, tn=D, tk=S, ta=True,
                     a_spec=at_spec(S, tm_f), b_spec=b_spec(S, D), out_dtype=BF16)
        g_down, dy = after(g_down, dy)
        swap_down = rs_swap("w_down", g_down, DS, D)
        dap3, dcw, dcb = _glu_conv_bwd("glu_conv_bwd", dy, sv["ap"], conv_w_f, cb3, l, cfg)
        gS[l]["conv_w"] = jnp.concatenate([dcw[0], dcw[1]], axis=1)
        gS[l]["conv_b"] = jnp.concatenate([dcb[0], dcb[1]], axis=1).reshape(F2)
        dap3 = rs_finish(l, swap_down, dap3)
        dh2 = _mm("mm_dh2", dap3, w_up_f[l], M=S, N=D, K=F2, tm=tm_s, tn=t_d, tk=F, tb=True,
                  a_spec=pl.BlockSpec((None, tm_s, F), lambda i, j, k: (k, i, 0)),
                  b_spec=wt_spec(t_d, F))
        g_up = _mm("mm_gup", sv["h2"], dap3, M=D, N=F2, K=S, tm=D, tn=US, tk=S, ta=True,
                   a_spec=at_spec(S, D),
                   b_spec=pl.BlockSpec((None, S, US), lambda i, j, k: (j // (N_DEV // 2), 0, j % (N_DEV // 2))),
                   out_dtype=BF16, out_shape=(N_DEV, D, US),
                   out_spec=pl.BlockSpec((None, D, US), lambda i, j, k: (j, 0, 0)))
        g_up, dh2 = after(g_up, dh2)
        swap_up = rs_swap("w_up", g_up, D, US)
        dx1, dx1b, dg2 = _rms_bwd("rms2_bwd", sv["x1"], dh2, dx, n2g3, l)
        gS[l]["norm2_g"] = dg2.reshape(D)
        dmix = _mm("mm_dmix", dx1b, w_o_f[l], M=S, N=D, K=D, tm=S, tn=t_d, tk=D, tb=True,
                   a_spec=a_spec(S, D), b_spec=wt_spec(t_d, D))
        g_o = _mm("mm_go", sv["mixed"], dx1b, M=D, N=D, K=S, tm=D, tn=t_d, tk=S, ta=True,
                  a_spec=at_spec(S, D), b_spec=b_spec(S, t_d), out_dtype=BF16)
        g_o, dmix = after(g_o, dmix)
        swap_o = rs_swap("w_o", g_o, OS, D)
        dmix = rs_finish(l, swap_up, dmix)
        dq_r, dk_pad, dv_pad, dsink, dag = _attn_bwd("attn_bwd", sv["q"], sv["k"], sv["v"], sv["attn"], dmix,
                                                     sink, ag3, l, cfg)
        dq_r = rs_finish(l, swap_o, dq_r)
        gS[l]["sink"] = dsink[:, 0]
        gS[l]["attn_out_g"] = dag.reshape(AW)
        dzgu, dzgv, dws, dbs, dlng, dlnb, dog = _sgu_bwd("sgu_bwd", sv["z"], dmix, lng3, lnb3, ws_b, wst_b, bs_b,
                                                        og3, l, cfg)
        gS[l]["w_s"] = dws
        gS[l]["b_s"] = dbs[:, :, 0]
        gS[l]["sgu_ln_g"] = dlng.reshape(GW)
        gS[l]["sgu_ln_b"] = dlnb.reshape(GW)
        gS[l]["sgu_out_g"] = dog.reshape(GW)
        dzq, dzkv, dqg, dkg = _qkv_prep_bwd("qkv_prep_bwd", sv["z"], dq_r, dk_pad, dv_pad, qg3, kg3, cos2, sin2,
                                            l, cfg)
        gS[l]["q_norm_g"] = dqg.reshape(HEAD)
        gS[l]["k_norm_g"] = dkg.reshape(HEAD)
        dz = jnp.concatenate([dzq, dzkv, dzgu, dzgv], axis=1)
        dh = _mm("mm_dh", dz, w_in_f[l], M=S, N=D, K=IN, tm=tm_s, tn=t_d, tk=IN,
                 a_spec=a_spec(tm_s, IN), b_spec=w_spec(IN, t_d))
        g_in = _mm("mm_gin", dz, sv["h"], M=IN, N=D, K=S, tm=t_in, tn=D, tk=S, ta=True,
                   a_spec=at_spec(S, t_in), b_spec=b_spec(S, D), out_dtype=BF16)
        g_in, dh = after(g_in, dh)
        dx, dxb, dg1 = _rms_bwd("rms1_bwd", sv["x"], dh, dx1, n1g3, l)
        gS[l]["norm1_g"] = dg1.reshape(D)
        carried = rs_swap("w_in", g_in, INS, D, small=_pack([gS[l][n] for n in SMALL] + [gS[l]["conv_w"]]))
    dx = rs_finish(0, carried, dx)
    grad_x = dx.reshape(1, S, D)

    grads, deltas, new_m, new_v = {}, {}, {}, {}
    results, placed = {}, dx
    for l, name, begun in started:
        recv[l][name] = _ici_wait("rs_wait_" + name, begun, 1, placed)
        wmv = (w_in_t, m_w_in_t, v_w_in_t) if name == "w_in" else (weights[name], mom_m[name], mom_v[name])
        results[name] = _adamw_sum("adamw_" + name, recv[l][name][0], *wmv, l, results.get(name), 16)
        placed = results[name][0]
    results["w_in"] = [jnp.swapaxes(r, 1, 2) for r in results["w_in"]]
    for name, res in results.items():
        grads[name], deltas[name], new_m[name], new_v[name] = res

    small_shapes = [weights[n].shape[1:] for n in SMALL] + [(3, F2)]
    SR = recv[0]["w_in"][1].shape[1]
    tr_s = _tile(SR, 512, 8)
    sblk = pl.BlockSpec((tr_s, 128), lambda i: (i, 0))
    sshp = jax.ShapeDtypeStruct((SR, 128), F32)
    small_parts = [[None] * L for _ in range(4)]
    for l in reversed(range(L)):
        def small_body(r_ref, w_ref, m_ref, v_ref, g_ref, d_ref, nm_ref, nv_ref):
            g = r_ref[0]
            for s in range(1, N_DEV // 2):
                g = g + r_ref[s]
            d, m2, v2 = _adamw_math(w_ref[...], g, m_ref[...], v_ref[...])
            g_ref[...] = g
            d_ref[...] = d
            nm_ref[...] = m2
            nv_ref[...] = v2

        def pack_small(src):
            return _pack([src[n][l] for n in SMALL] + [jnp.zeros((3, F2), F32)])

        packed = pl.pallas_call(
            small_body, name="adamw_small", grid=(SR // tr_s,),
            in_specs=[pl.BlockSpec((N_DEV // 2, tr_s, 128), lambda i: (0, i, 0)), sblk, sblk, sblk],
            out_specs=[sblk, sblk, sblk, sblk], out_shape=[sshp, sshp, sshp, sshp],
            compiler_params=_params(("arbitrary",)),
        )(recv[l]["w_in"][1], pack_small(weights), pack_small(mom_m), pack_small(mom_v))
        for t in range(4):
            small_parts[t][l] = _unpack(packed[t], small_shapes)
    for t, store in enumerate([grads, deltas, new_m, new_v]):
        for i, n in enumerate(SMALL):
            store[n] = jnp.stack([small_parts[t][l][i] for l in range(L)])
    conv_full = jnp.stack([small_parts[0][l][len(SMALL)] for l in range(L)])
    me_i = _dev_index(_mesh_pos())
    g_cw = lax.dynamic_slice_in_dim(conv_full, me_i * US, US, axis=2)
    grads["conv_w"] = g_cw
    d_cw, m_cw, v_cw = _adamw_plain("adamw_conv_w", g_cw.reshape(L * 3, US), conv_w.reshape(L * 3, US),
                                    m_conv_w.reshape(L * 3, US), v_conv_w.reshape(L * 3, US))
    deltas["conv_w"] = d_cw.reshape(L, 3, US)
    new_m["conv_w"] = m_cw.reshape(L, 3, US)
    new_v["conv_w"] = v_cw.reshape(L, 3, US)

    return (loss, grad_x, *[grads[n] for n in order], *[deltas[n] for n in order],
            *[new_m[n] for n in order], *[new_v[n] for n in order])
```

```python
import jax
import jax.numpy as jnp
from jax import lax
from jax.experimental import pallas as pl
from jax.experimental.pallas import tpu as pltpu
from jax.experimental.pallas import tpu_sc as plsc

F32 = jnp.float32
BF16 = jnp.bfloat16
MESH = pl.DeviceIdType.MESH

N_DEV = 8
HEAD = 128
EPS = 1e-6
MASK_VALUE = -1e30
ROPE_THETA = 10000.0
GELU_C = 0.7978845608028654
GELU_A = 0.044715

ADAM_LR = 0.001
ADAM_B1 = 0.9
ADAM_B2 = 0.999
ADAM_EPS = 1e-08
ADAM_WD = 0.01
ADAM_STEP = 10

VMEM_LIMIT = 56 * 1024 * 1024


def _tile(n, pref, mult):
    best = None
    for t in range(mult, min(n, pref) + 1, mult):
        if n % t == 0:
            best = t
    return n if best is None else best


def _params(sem=None):
    kw = dict(vmem_limit_bytes=VMEM_LIMIT)
    if sem is not None:
        kw["dimension_semantics"] = sem
    return pltpu.CompilerParams(**kw)


def _gelu(x):
    return x * (0.5 * (1.0 + jnp.tanh(GELU_C * (x + GELU_A * (x * x * x)))))


def _gelu_grad(x):
    t = jnp.tanh(GELU_C * (x + GELU_A * (x * x * x)))
    return 0.5 * (1.0 + t) + 0.5 * x * (1.0 - t * t) * (GELU_C * (1.0 + 3.0 * GELU_A * (x * x)))


def _sigmoid(x):
    return 1.0 / (1.0 + jnp.exp(-x))


def _vec_spec(l, n):
    return pl.BlockSpec((None, 1, n), lambda *_: (l, 0, 0))


def _mm(name, a, b, *, M, N, K, tm, tn, tk, a_spec, b_spec, ta=False, tb=False, out_dtype=F32, res=None,
        out_shape=None, out_spec=None):
    nm, nn, nk = M // tm, N // tn, K // tk
    assert nm * tm == M and nn * tn == N and nk * tk == K
    assert not (ta and nk > 1)
    dims = (((1,), (1,)), ((), ())) if tb else (((1,), (0,)), ((), ()))

    def body(*refs):
        refs = list(refs)
        a_ref = refs.pop(0)
        b_ref = refs.pop(0)
        r_ref = refs.pop(0) if res is not None else None
        o_ref = refs.pop(0)
        acc = refs.pop(0) if nk > 1 else None
        at = refs.pop(0) if ta else None
        k = pl.program_id(2)
        if ta:
            @pl.when(pl.program_id(1) == 0)
            def _():
                at[...] = a_ref[...].T
            lhs = at[...]
        else:
            lhs = a_ref[...]
        p = lax.dot_general(lhs, b_ref[...], dims, preferred_element_type=F32)

        def finish(r):
            if r_ref is not None:
                r = r_ref[...] + r
            o_ref[...] = r.astype(out_dtype)

        if nk == 1:
            finish(p)
        else:
            @pl.when(k == 0)
            def _():
                acc[...] = p

            @pl.when(k > 0)
            def _():
                acc[...] += p

            @pl.when(k == nk - 1)
            def _():
                finish(acc[...])

    in_specs = [a_spec, b_spec]
    args = [a, b]
    if res is not None:
        in_specs.append(pl.BlockSpec((tm, tn), lambda i, j, k: (i, j)))
        args.append(res)
    scratch = []
    if nk > 1:
        scratch.append(pltpu.VMEM((tm, tn), F32))
    if ta:
        scratch.append(pltpu.VMEM((tm, tk), BF16))
    return pl.pallas_call(
        body, name=name, grid=(nm, nn, nk),
        in_specs=in_specs,
        out_specs=pl.BlockSpec((tm, tn), lambda i, j, k: (i, j)) if out_spec is None else out_spec,
        out_shape=jax.ShapeDtypeStruct((M, N) if out_shape is None else out_shape, out_dtype),
        scratch_shapes=scratch,
        compiler_params=_params(("arbitrary", "arbitrary", "arbitrary")),
    )(*args)


def _rms_fwd(name, x, g3, l):
    S, D = x.shape
    tr = _tile(S, 256, 16)

    def body(x_ref, g_ref, h_ref):
        xv = x_ref[...]
        r = lax.rsqrt(jnp.mean(xv * xv, axis=-1, keepdims=True) + EPS)
        h_ref[...] = ((xv * r) * g_ref[...]).astype(BF16)

    return pl.pallas_call(
        body, name=name, grid=(S // tr,),
        in_specs=[pl.BlockSpec((tr, D), lambda i: (i, 0)), _vec_spec(l, D)],
        out_specs=pl.BlockSpec((tr, D), lambda i: (i, 0)),
        out_shape=jax.ShapeDtypeStruct((S, D), BF16),
        compiler_params=_params(("arbitrary",)),
    )(x, g3)


def _rope(t, cos2, sin2):
    return t * cos2 + pltpu.roll(t, HEAD // 2, axis=1) * sin2


def _qkv_prep(name, z, qg3, kg3, cos2, sin2, l, cfg):
    S, AW, KVW, NQ, NKV = cfg["S"], cfg["AW"], cfg["KVW"], cfg["NQ"], cfg["NKV"]
    tr = _tile(S, 256, 16)
    kv_blk = (3 * AW) // (2 * KVW)

    def body(zq_ref, zkv_ref, qg_ref, kg_ref, c_ref, s_ref, q_ref, k_ref, v_ref):
        cosv, sinv = c_ref[...], s_ref[...]

        def norm_rope(t, g):
            r = lax.rsqrt(jnp.mean(t * t, axis=-1, keepdims=True) + EPS)
            return _rope((t * r) * g, cosv, sinv)

        for h in range(NQ):
            sl = slice(h * HEAD, (h + 1) * HEAD)
            q_ref[:, sl] = norm_rope(zq_ref[:, sl], qg_ref[...]).astype(BF16)
        for h in range(NKV):
            sl = slice(h * HEAD, (h + 1) * HEAD)
            k_ref[:, sl] = norm_rope(zkv_ref[:, sl], kg_ref[...]).astype(BF16)
        v_ref[...] = zkv_ref[:, KVW:].astype(BF16)

    return pl.pallas_call(
        body, name=name, grid=(S // tr,),
        in_specs=[pl.BlockSpec((tr, AW), lambda i: (i, 0)),
                  pl.BlockSpec((tr, 2 * KVW), lambda i: (i, kv_blk)),
                  _vec_spec(l, HEAD), _vec_spec(l, HEAD),
                  pl.BlockSpec((tr, HEAD), lambda i: (i, 0)),
                  pl.BlockSpec((tr, HEAD), lambda i: (i, 0))],
        out_specs=[pl.BlockSpec((tr, AW), lambda i: (i, 0)),
                   pl.BlockSpec((tr, KVW), lambda i: (i, 0)),
                   pl.BlockSpec((tr, KVW), lambda i: (i, 0))],
        out_shape=[jax.ShapeDtypeStruct((S, AW), BF16),
                   jax.ShapeDtypeStruct((S, KVW), BF16),
                   jax.ShapeDtypeStruct((S, KVW), BF16)],
        compiler_params=_params(("arbitrary",)),
    )(z, z, qg3, kg3, cos2, sin2)


def _band_specs(width, nb):
    return [pl.BlockSpec((HEAD, width), lambda n: (jnp.maximum(n - 1, 0), 0)),
            pl.BlockSpec((HEAD, width), lambda n: (n, 0)),
            pl.BlockSpec((HEAD, width), lambda n: (jnp.minimum(n + 1, nb - 1), 0))]


def _attn_probs(qs, kj, n, sink_of_row, S, G):
    s = lax.dot_general(qs, kj, (((1,), (1,)), ((), ())), preferred_element_type=F32) * (HEAD ** -0.5)
    rows = lax.broadcasted_iota(jnp.int32, (G * HEAD, 3 * HEAD), 0)
    cols = lax.broadcasted_iota(jnp.int32, (G * HEAD, 3 * HEAD), 1)
    qi = rows & (HEAD - 1)
    kpos = n * HEAD - HEAD + cols
    valid = (cols >= qi) & (cols <= qi + 2 * HEAD) & (kpos >= 0) & (kpos < S)
    s = jnp.where(valid, s, MASK_VALUE)
    m = jnp.maximum(jnp.max(s, axis=-1, keepdims=True), sink_of_row)
    p = jnp.exp(s - m)
    e_sink = jnp.exp(sink_of_row - m)
    inv = 1.0 / (jnp.sum(p, axis=-1, keepdims=True) + e_sink)
    return p * inv, e_sink * inv


def _sink_rows(sink_ref, l, j, G):
    hidx = lax.broadcasted_iota(jnp.int32, (G * HEAD, 1), 0) // HEAD
    col = jnp.full((G * HEAD, 1), sink_ref[l, j * G], F32)
    for g in range(1, G):
        col = jnp.where(hidx == g, sink_ref[l, j * G + g], col)
    return col


def _attn_fwd(name, q, k, v, sink, ag3, l, cfg):
    S, AW, KVW, NKV, G = cfg["S"], cfg["AW"], cfg["KVW"], cfg["NKV"], cfg["G"]
    nb = S // HEAD

    def body(q_ref, kp, kc, kn, vp, vc, vn, sink_ref, ag_ref, a_ref, mix_ref):
        n = pl.program_id(0)
        kb = jnp.concatenate([kp[...], kc[...], kn[...]], axis=0)
        vb = jnp.concatenate([vp[...], vc[...], vn[...]], axis=0)
        for j in range(NKV):
            sl = slice(j * HEAD, (j + 1) * HEAD)
            qs = jnp.concatenate([q_ref[:, (j * G + g) * HEAD:(j * G + g + 1) * HEAD] for g in range(G)], axis=0)
            probs, _ = _attn_probs(qs, kb[:, sl], n, _sink_rows(sink_ref, l, j, G), S, G)
            o = jnp.dot(probs.astype(BF16), vb[:, sl], preferred_element_type=F32)
            for g in range(G):
                a_ref[:, (j * G + g) * HEAD:(j * G + g + 1) * HEAD] = o[g * HEAD:(g + 1) * HEAD]
        a = a_ref[...]
        r = lax.rsqrt(jnp.mean(a * a, axis=-1, keepdims=True) + EPS)
        mix_ref[...] = ((a * r) * ag_ref[...]).astype(BF16)

    return pl.pallas_call(
        body, name=name, grid=(nb,),
        in_specs=[pl.BlockSpec((HEAD, AW), lambda n: (n, 0))] + _band_specs(KVW, nb) + _band_specs(KVW, nb)
                 + [pl.BlockSpec(memory_space=pltpu.SMEM), _vec_spec(l, AW)],
        out_specs=[pl.BlockSpec((HEAD, AW), lambda n: (n, 0)), pl.BlockSpec((HEAD, AW), lambda n: (n, 0))],
        out_shape=[jax.ShapeDtypeStruct((S, AW), F32), jax.ShapeDtypeStruct((S, AW), BF16)],
        compiler_params=_params(("arbitrary",)),
    )(q, k, k, k, v, v, v, sink, ag3)


def _sgu_forward_math(gu, gv, lng, lnb):
    u = _gelu(gu)
    vv = _gelu(gv)
    mu = jnp.mean(vv, axis=-1, keepdims=True)
    xc = vv - mu
    rstd = lax.rsqrt(jnp.mean(xc * xc, axis=-1, keepdims=True) + EPS)
    xhat = xc * rstd
    vn = xhat * lng + lnb
    return u, xhat, rstd, vn


def _sgu_fwd(name, z, lng3, lnb3, ws_b, bs_b, og3, l, cfg):
    S, GW, NG = cfg["S"], cfg["GW"], cfg["NG"]

    def body(gu_ref, gv_ref, lng_ref, lnb_ref, ws_ref, bs_ref, og_ref, mix_ref, sg_ref):
        u, _, _, vn = _sgu_forward_math(gu_ref[...], gv_ref[...], lng_ref[...], lnb_ref[...])
        vnb = vn.astype(BF16)
        for h in range(NG):
            sl = slice(h * HEAD, (h + 1) * HEAD)
            f = jnp.dot(ws_ref[h], vnb[:, sl], preferred_element_type=F32) + bs_ref[h]
            sg_ref[:, sl] = u[:, sl] * f
        sg = sg_ref[...]
        r = lax.rsqrt(jnp.mean(sg * sg, axis=-1, keepdims=True) + EPS)
        mix_ref[...] = ((sg * r) * og_ref[...]).astype(BF16)

    return pl.pallas_call(
        body, name=name, grid=(S // HEAD,),
        in_specs=[pl.BlockSpec((HEAD, GW), lambda c: (c, 1)),
                  pl.BlockSpec((HEAD, GW), lambda c: (c, 2)),
                  _vec_spec(l, GW), _vec_spec(l, GW),
                  pl.BlockSpec((None, NG, HEAD, HEAD), lambda c: (l, 0, 0, 0)),
                  pl.BlockSpec((None, NG, HEAD, HEAD), lambda c: (l, 0, 0, 0)),
                  _vec_spec(l, GW)],
        out_specs=pl.BlockSpec((HEAD, GW), lambda c: (c, 0)),
        out_shape=jax.ShapeDtypeStruct((S, GW), BF16),
        scratch_shapes=[pltpu.VMEM((HEAD, GW), F32)],
        compiler_params=_params(("arbitrary",)),
    )(z, z, lng3, lnb3, ws_b, bs_b, og3)


CONV_HALO = 8
CONV_ROWS = 64


def _for_row_windows(S, fn):
    R, W = CONV_ROWS, CONV_ROWS + 2 * CONV_HALO
    n = S // R
    assert n * R == S and n >= 2
    fn(0, 0, 0)
    if n > 2:
        def mid(k, carry):
            fn(pl.multiple_of(k * R - CONV_HALO, CONV_HALO), CONV_HALO, pl.multiple_of(k * R, R))
            return carry

        lax.fori_loop(1, n - 1, mid, 0)
    fn(S - W, 2 * CONV_HALO, S - R)


def _shifts(t):
    W = t.shape[0]
    row = lax.broadcasted_iota(jnp.int32, t.shape, 0)
    dn = jnp.where(row == 0, 0.0, pltpu.roll(t, 1, axis=0))
    up = jnp.where(row == W - 1, 0.0, pltpu.roll(t, W - 1, axis=0))
    return dn, up


def _conv3(t, w, b):
    dn, up = _shifts(t)
    return ((b + dn * w[0:1]) + t * w[1:2]) + up * w[2:3], dn, up


def _conv_glu_fwd(name, ap, cw, cb3, l, cfg):
    S, F = cfg["S"], cfg["F"]
    tc = _tile(F, 256, 128)
    nf = F // tc
    R, W = CONV_ROWS, CONV_ROWS + 2 * CONV_HALO

    def body(g_ref, u_ref, wg_ref, wu_ref, bg_ref, bu_ref, y_ref):
        wg, wu, bg, bu = wg_ref[...], wu_ref[...], bg_ref[...], bu_ref[...]

        def window(start, lo, out0):
            rows = pl.ds(start, W)
            ag, _, _ = _conv3(g_ref[rows, :], wg, bg)
            au, _, _ = _conv3(u_ref[rows, :], wu, bu)
            y = (ag * _sigmoid(ag)) * au
            y_ref[pl.ds(out0, R), :] = y[lo:lo + R].astype(BF16)

        _for_row_windows(S, window)

    return pl.pallas_call(
        body, name=name, grid=(nf,),
        in_specs=[pl.BlockSpec((S, tc), lambda j: (0, j)),
                  pl.BlockSpec((S, tc), lambda j: (0, j + nf)),
                  pl.BlockSpec((None, 3, tc), lambda j: (l, 0, j)),
                  pl.BlockSpec((None, 3, tc), lambda j: (l, 0, j + nf)),
                  pl.BlockSpec((None, 1, tc), lambda j: (l, 0, j)),
                  pl.BlockSpec((None, 1, tc), lambda j: (l, 0, j + nf))],
        out_specs=pl.BlockSpec((S, tc), lambda j: (0, j)),
        out_shape=jax.ShapeDtypeStruct((S, F), BF16),
        compiler_params=_params(("arbitrary",)),
    )(ap, ap, cw, cw, cb3, cb3)


def _loss_bwd(name, y, target):
    S, D = y.shape
    tr = _tile(S, 256, 16)

    def body(y_ref, t_ref, loss_ref, d_ref, db_ref):
        @pl.when(pl.program_id(0) == 0)
        def _():
            loss_ref[...] = jnp.zeros_like(loss_ref)

        err = y_ref[...] - t_ref[...]
        part = 0.5 * jnp.sum(jnp.mean(err * err, axis=-1, keepdims=True), axis=0, keepdims=True)
        loss_ref[...] += jnp.broadcast_to(part, loss_ref.shape)
        d = err * (1.0 / D)
        d_ref[...] = d
        db_ref[...] = d.astype(BF16)

    return pl.pallas_call(
        body, name=name, grid=(S // tr,),
        in_specs=[pl.BlockSpec((tr, D), lambda i: (i, 0)), pl.BlockSpec((tr, D), lambda i: (i, 0))],
        out_specs=[pl.BlockSpec((8, 128), lambda i: (0, 0)),
                   pl.BlockSpec((tr, D), lambda i: (i, 0)),
                   pl.BlockSpec((tr, D), lambda i: (i, 0))],
        out_shape=[jax.ShapeDtypeStruct((8, 128), F32),
                   jax.ShapeDtypeStruct((S, D), F32),
                   jax.ShapeDtypeStruct((S, D), BF16)],
        compiler_params=_params(("arbitrary",)),
    )(y, target)


def _rms_bwd(name, x, dh, dres, g3, l):
    S, D = x.shape
    tr = _tile(S, 256, 16)

    def body(x_ref, dh_ref, dr_ref, g_ref, dx_ref, dxb_ref, dg_ref):
        @pl.when(pl.program_id(0) == 0)
        def _():
            dg_ref[...] = jnp.zeros_like(dg_ref)

        xv = x_ref[...]
        dhv = dh_ref[...]
        r = lax.rsqrt(jnp.mean(xv * xv, axis=-1, keepdims=True) + EPS)
        xhat = xv * r
        dhg = dhv * g_ref[...]
        dx = dr_ref[...] + r * (dhg - xhat * jnp.mean(dhg * xhat, axis=-1, keepdims=True))
        dx_ref[...] = dx
        dxb_ref[...] = dx.astype(BF16)
        dg_ref[...] += jnp.sum(dhv * xhat, axis=0, keepdims=True)

    return pl.pallas_call(
        body, name=name, grid=(S // tr,),
        in_specs=[pl.BlockSpec((tr, D), lambda i: (i, 0)), pl.BlockSpec((tr, D), lambda i: (i, 0)),
                  pl.BlockSpec((tr, D), lambda i: (i, 0)), _vec_spec(l, D)],
        out_specs=[pl.BlockSpec((tr, D), lambda i: (i, 0)), pl.BlockSpec((tr, D), lambda i: (i, 0)),
                   pl.BlockSpec((1, D), lambda i: (0, 0))],
        out_shape=[jax.ShapeDtypeStruct((S, D), F32), jax.ShapeDtypeStruct((S, D), BF16),
                   jax.ShapeDtypeStruct((1, D), F32)],
        compiler_params=_params(("arbitrary",)),
    )(x, dh, dres, g3)


def _glu_conv_bwd(name, dy, ap, cw, cb3, l, cfg):
    S, F = cfg["S"], cfg["F"]
    tc = 128
    nf = F // tc
    R, W = CONV_ROWS, CONV_ROWS + 2 * CONV_HALO

    def body(dy_ref, g_ref, u_ref, wg_ref, wu_ref, bg_ref, bu_ref, dap_ref, dw_ref, db_ref, acc):
        acc[...] = jnp.zeros_like(acc)
        wg, wu, bg, bu = wg_ref[...], wu_ref[...], bg_ref[...], bu_ref[...]

        def window(start, lo, out0):
            rows = pl.ds(start, W)
            apg, apu, dyv = g_ref[rows, :], u_ref[rows, :], dy_ref[rows, :]
            ag, g_dn, g_up = _conv3(apg, wg, bg)
            au, u_dn, u_up = _conv3(apu, wu, bu)
            sig = _sigmoid(ag)
            da_u = dyv * (ag * sig)
            da_g = (dyv * au) * (sig * (1.0 + ag * (1.0 - sig)))

            def back(da, w):
                prv, nxt = _shifts(da)
                return ((nxt * w[0:1] + da * w[1:2]) + prv * w[2:3])[lo:lo + R].astype(BF16)

            dap_ref[0, pl.ds(out0, R), :] = back(da_g, wg)
            dap_ref[1, pl.ds(out0, R), :] = back(da_u, wu)

            def add(i, prod):
                acc[i] += jnp.sum(prod[lo:lo + R].reshape(R // 8, 8, tc), axis=0)

            for i, other in enumerate([g_dn, apg, g_up]):
                add(i, da_g * other)
            for i, other in enumerate([u_dn, apu, u_up]):
                add(3 + i, da_u * other)
            add(6, da_g)
            add(7, da_u)

        _for_row_windows(S, window)
        col = [jnp.sum(acc[i], axis=0, keepdims=True) for i in range(8)]
        dw_ref[0] = jnp.concatenate(col[0:3], axis=0)
        dw_ref[1] = jnp.concatenate(col[3:6], axis=0)
        db_ref[0] = col[6]
        db_ref[1] = col[7]

    return pl.pallas_call(
        body, name=name, grid=(nf,),
        in_specs=[pl.BlockSpec((S, tc), lambda j: (0, j)),
                  pl.BlockSpec((S, tc), lambda j: (0, j)),
                  pl.BlockSpec((S, tc), lambda j: (0, j + nf)),
                  pl.BlockSpec((None, 3, tc), lambda j: (l, 0, j)),
                  pl.BlockSpec((None, 3, tc), lambda j: (l, 0, j + nf)),
                  pl.BlockSpec((None, 1, tc), lambda j: (l, 0, j)),
                  pl.BlockSpec((None, 1, tc), lambda j: (l, 0, j + nf))],
        out_specs=[pl.BlockSpec((2, S, tc), lambda j: (0, 0, j)),
                   pl.BlockSpec((2, 3, tc), lambda j: (0, 0, j)),
                   pl.BlockSpec((2, 1, tc), lambda j: (0, 0, j))],
        out_shape=[jax.ShapeDtypeStruct((2, S, F), BF16),
                   jax.ShapeDtypeStruct((2, 3, F), F32),
                   jax.ShapeDtypeStruct((2, 1, F), F32)],
        scratch_shapes=[pltpu.VMEM((8, 8, tc), F32)],
        compiler_params=_params(("arbitrary",)),
    )(dy, ap, ap, cw, cw, cb3, cb3)


def _attn_bwd(name, q, k, v, attn, dmix, sink, ag3, l, cfg):
    S, AW, KVW, NQ, NKV, G = cfg["S"], cfg["AW"], cfg["KVW"], cfg["NQ"], cfg["NKV"], cfg["G"]
    nb = S // HEAD
    scale = HEAD ** -0.5

    def body(q_ref, kp, kc, kn, vp, vc, vn, a_ref, dm_ref, sink_ref, ag_ref,
             dq_ref, dk_ref, dv_ref, dsink_ref, dag_ref, da_scr):
        n = pl.program_id(0)

        @pl.when(n == 0)
        def _():
            dk_ref[...] = jnp.zeros_like(dk_ref)
            dv_ref[...] = jnp.zeros_like(dv_ref)
            dsink_ref[...] = jnp.zeros_like(dsink_ref)
            dag_ref[...] = jnp.zeros_like(dag_ref)

        a = a_ref[...]
        dm = dm_ref[...]
        r = lax.rsqrt(jnp.mean(a * a, axis=-1, keepdims=True) + EPS)
        xhat = a * r
        dmg = dm * ag_ref[...]
        da_scr[...] = r * (dmg - xhat * jnp.mean(dmg * xhat, axis=-1, keepdims=True))
        dag_ref[...] += jnp.sum(dm * xhat, axis=0, keepdims=True)

        kb = jnp.concatenate([kp[...], kc[...], kn[...]], axis=0)
        vb = jnp.concatenate([vp[...], vc[...], vn[...]], axis=0)
        band = pl.ds(pl.multiple_of(n * HEAD, HEAD), 3 * HEAD)
        for j in range(NKV):
            sl = slice(j * HEAD, (j + 1) * HEAD)
            heads = [slice((j * G + g) * HEAD, (j * G + g + 1) * HEAD) for g in range(G)]
            qs = jnp.concatenate([q_ref[:, hs] for hs in heads], axis=0)
            do = jnp.concatenate([da_scr[:, hs] for hs in heads], axis=0)
            kj, vj = kb[:, sl], vb[:, sl]
            probs, p_sink = _attn_probs(qs, kj, n, _sink_rows(sink_ref, l, j, G), S, G)
            dob = do.astype(BF16)
            dprobs = lax.dot_general(dob, vj, (((1,), (1,)), ((), ())), preferred_element_type=F32)
            delta = jnp.sum(dprobs * probs, axis=-1, keepdims=True)
            ds = (probs * (dprobs - delta)) * scale
            dsb = ds.astype(BF16)
            dsk = -(p_sink * delta)
            dq = jnp.dot(dsb, kj, preferred_element_type=F32)
            for g in range(G):
                dq_ref[:, heads[g]] = dq[g * HEAD:(g + 1) * HEAD]
                part = jnp.sum(dsk[g * HEAD:(g + 1) * HEAD], axis=0, keepdims=True)
                dsink_ref[j * G + g:j * G + g + 1, :] += jnp.broadcast_to(part, (1, HEAD))
            dk_ref[band, sl] += lax.dot_general(dsb, qs, (((0,), (0,)), ((), ())), preferred_element_type=F32)
            dv_ref[band, sl] += lax.dot_general(probs.astype(BF16), dob, (((0,), (0,)), ((), ())),
                                                preferred_element_type=F32)

    return pl.pallas_call(
        body, name=name, grid=(nb,),
        in_specs=[pl.BlockSpec((HEAD, AW), lambda n: (n, 0))] + _band_specs(KVW, nb) + _band_specs(KVW, nb)
                 + [pl.BlockSpec((HEAD, AW), lambda n: (n, 0)),
                    pl.BlockSpec((HEAD, AW), lambda n: (n, 0)),
                    pl.BlockSpec(memory_space=pltpu.SMEM), _vec_spec(l, AW)],
        out_specs=[pl.BlockSpec((HEAD, AW), lambda n: (n, 0)),
                   pl.BlockSpec((S + 2 * HEAD, KVW), lambda n: (0, 0)),
                   pl.BlockSpec((S + 2 * HEAD, KVW), lambda n: (0, 0)),
                   pl.BlockSpec((NQ, HEAD), lambda n: (0, 0)),
                   pl.BlockSpec((1, AW), lambda n: (0, 0))],
        out_shape=[jax.ShapeDtypeStruct((S, AW), F32),
                   jax.ShapeDtypeStruct((S + 2 * HEAD, KVW), F32),
                   jax.ShapeDtypeStruct((S + 2 * HEAD, KVW), F32),
                   jax.ShapeDtypeStruct((NQ, HEAD), F32),
                   jax.ShapeDtypeStruct((1, AW), F32)],
        scratch_shapes=[pltpu.VMEM((HEAD, AW), F32)],
        compiler_params=_params(("arbitrary",)),
    )(q, k, k, k, v, v, v, attn, dmix, sink, ag3)


def _qkv_prep_bwd(name, z, dq, dk_pad, dv_pad, qg3, kg3, cos2, sin2, l, cfg):
    S, AW, KVW, NQ, NKV = cfg["S"], cfg["AW"], cfg["KVW"], cfg["NQ"], cfg["NKV"]
    kv_blk = (3 * AW) // (2 * KVW)

    def body(zq_ref, zkv_ref, dq_ref, dk_ref, dv_ref, qg_ref, kg_ref, c_ref, s_ref,
             dzq_ref, dzkv_ref, dqg_ref, dkg_ref):
        @pl.when(pl.program_id(0) == 0)
        def _():
            dqg_ref[...] = jnp.zeros_like(dqg_ref)
            dkg_ref[...] = jnp.zeros_like(dkg_ref)

        cosv, sinv = c_ref[...], s_ref[...]

        def back(t, dr, g):
            r = lax.rsqrt(jnp.mean(t * t, axis=-1, keepdims=True) + EPS)
            xhat = t * r
            dn = dr * cosv + pltpu.roll(dr * sinv, HEAD // 2, axis=1)
            dxh = dn * g
            dt = r * (dxh - xhat * jnp.mean(dxh * xhat, axis=-1, keepdims=True))
            return dt, jnp.sum(dn * xhat, axis=0, keepdims=True)

        gq = jnp.zeros((1, HEAD), F32)
        for h in range(NQ):
            sl = slice(h * HEAD, (h + 1) * HEAD)
            dt, gpart = back(zq_ref[:, sl], dq_ref[:, sl], qg_ref[...])
            dzq_ref[:, sl] = dt.astype(BF16)
            gq = gq + gpart
        dqg_ref[...] += gq
        gk = jnp.zeros((1, HEAD), F32)
        for h in range(NKV):
            sl = slice(h * HEAD, (h + 1) * HEAD)
            dt, gpart = back(zkv_ref[:, sl], dk_ref[:, sl], kg_ref[...])
            dzkv_ref[:, sl] = dt.astype(BF16)
            gk = gk + gpart
        dkg_ref[...] += gk
        dzkv_ref[:, KVW:] = dv_ref[...].astype(BF16)

    return pl.pallas_call(
        body, name=name, grid=(S // HEAD,),
        in_specs=[pl.BlockSpec((HEAD, AW), lambda i: (i, 0)),
                  pl.BlockSpec((HEAD, 2 * KVW), lambda i: (i, kv_blk)),
                  pl.BlockSpec((HEAD, AW), lambda i: (i, 0)),
                  pl.BlockSpec((HEAD, KVW), lambda i: (i + 1, 0)),
                  pl.BlockSpec((HEAD, KVW), lambda i: (i + 1, 0)),
                  _vec_spec(l, HEAD), _vec_spec(l, HEAD),
                  pl.BlockSpec((HEAD, HEAD), lambda i: (i, 0)),
                  pl.BlockSpec((HEAD, HEAD), lambda i: (i, 0))],
        out_specs=[pl.BlockSpec((HEAD, AW), lambda i: (i, 0)),
                   pl.BlockSpec((HEAD, 2 * KVW), lambda i: (i, 0)),
                   pl.BlockSpec((1, HEAD), lambda i: (0, 0)),
                   pl.BlockSpec((1, HEAD), lambda i: (0, 0))],
        out_shape=[jax.ShapeDtypeStruct((S, AW), BF16),
                   jax.ShapeDtypeStruct((S, 2 * KVW), BF16),
                   jax.ShapeDtypeStruct((1, HEAD), F32),
                   jax.ShapeDtypeStruct((1, HEAD), F32)],
        compiler_params=_params(("arbitrary",)),
    )(z, z, dq, dk_pad, dv_pad, qg3, kg3, cos2, sin2)


def _sgu_bwd(name, z, dmix, lng3, lnb3, ws_b, wst_b, bs_b, og3, l, cfg):
    S, GW, NG = cfg["S"], cfg["GW"], cfg["NG"]

    def body(gu_ref, gv_ref, dm_ref, lng_ref, lnb_ref, ws_ref, wst_ref, bs_ref, og_ref,
             dgu_ref, dgv_ref, dws_ref, dbs_ref, dlng_ref, dlnb_ref, dog_ref, sg_scr, f_scr, dvn_scr):
        @pl.when(pl.program_id(0) == 0)
        def _():
            dws_ref[...] = jnp.zeros_like(dws_ref)
            dbs_ref[...] = jnp.zeros_like(dbs_ref)
            dlng_ref[...] = jnp.zeros_like(dlng_ref)
            dlnb_ref[...] = jnp.zeros_like(dlnb_ref)
            dog_ref[...] = jnp.zeros_like(dog_ref)

        gu, gv = gu_ref[...], gv_ref[...]
        lng = lng_ref[...]
        u, xhat, rstd, vn = _sgu_forward_math(gu, gv, lng, lnb_ref[...])
        vnb = vn.astype(BF16)
        for h in range(NG):
            sl = slice(h * HEAD, (h + 1) * HEAD)
            f = jnp.dot(ws_ref[h], vnb[:, sl], preferred_element_type=F32) + bs_ref[h]
            f_scr[:, sl] = f
            sg_scr[:, sl] = u[:, sl] * f
        sg = sg_scr[...]
        dm = dm_ref[...]
        r = lax.rsqrt(jnp.mean(sg * sg, axis=-1, keepdims=True) + EPS)
        sghat = sg * r
        dmg = dm * og_ref[...]
        dsg = r * (dmg - sghat * jnp.mean(dmg * sghat, axis=-1, keepdims=True))
        dog_ref[...] += jnp.sum(dm * sghat, axis=0, keepdims=True)
        du = dsg * f_scr[...]
        df = dsg * u
        dfb = df.astype(BF16)
        for h in range(NG):
            sl = slice(h * HEAD, (h + 1) * HEAD)
            dvn_scr[:, sl] = jnp.dot(wst_ref[h], dfb[:, sl], preferred_element_type=F32)
            dws_ref[h] += lax.dot_general(dfb[:, sl], vnb[:, sl], (((1,), (1,)), ((), ())),
                                          preferred_element_type=F32)
            dbs_ref[h] += jnp.broadcast_to(jnp.sum(df[:, sl], axis=-1, keepdims=True), (HEAD, HEAD))
        dvn = dvn_scr[...]
        dlng_ref[...] += jnp.sum(dvn * xhat, axis=0, keepdims=True)
        dlnb_ref[...] += jnp.sum(dvn, axis=0, keepdims=True)
        dxh = dvn * lng
        dvv = rstd * ((dxh - jnp.mean(dxh, axis=-1, keepdims=True))
                      - xhat * jnp.mean(dxh * xhat, axis=-1, keepdims=True))
        dgu_ref[...] = (du * _gelu_grad(gu)).astype(BF16)
        dgv_ref[...] = (dvv * _gelu_grad(gv)).astype(BF16)

    vec = pl.BlockSpec((1, GW), lambda c: (0, 0))
    mat = pl.BlockSpec((NG, HEAD, HEAD), lambda c: (0, 0, 0))
    wsp = pl.BlockSpec((None, NG, HEAD, HEAD), lambda c: (l, 0, 0, 0))
    return pl.pallas_call(
        body, name=name, grid=(S // HEAD,),
        in_specs=[pl.BlockSpec((HEAD, GW), lambda c: (c, 1)),
                  pl.BlockSpec((HEAD, GW), lambda c: (c, 2)),
                  pl.BlockSpec((HEAD, GW), lambda c: (c, 1)),
                  _vec_spec(l, GW), _vec_spec(l, GW), wsp, wsp, wsp, _vec_spec(l, GW)],
        out_specs=[pl.BlockSpec((HEAD, GW), lambda c: (c, 0)), pl.BlockSpec((HEAD, GW), lambda c: (c, 0)),
                   mat, mat, vec, vec, vec],
        out_shape=[jax.ShapeDtypeStruct((S, GW), BF16), jax.ShapeDtypeStruct((S, GW), BF16),
                   jax.ShapeDtypeStruct((NG, HEAD, HEAD), F32), jax.ShapeDtypeStruct((NG, HEAD, HEAD), F32),
                   jax.ShapeDtypeStruct((1, GW), F32), jax.ShapeDtypeStruct((1, GW), F32),
                   jax.ShapeDtypeStruct((1, GW), F32)],
        scratch_shapes=[pltpu.VMEM((HEAD, GW), F32), pltpu.VMEM((HEAD, GW), F32), pltpu.VMEM((HEAD, GW), F32)],
        compiler_params=_params(("arbitrary",)),
    )(z, z, dmix, lng3, lnb3, ws_b, wst_b, bs_b, og3)


def _mesh_pos():
    x, y, c = lax.axis_index("x"), lax.axis_index("y"), lax.axis_index("c")
    return x, y, c


def _dev_index(p):
    return 4 * p[0] + 2 * p[1] + p[2]


def _handshake(peers):
    barrier = pltpu.get_barrier_semaphore()
    for p in peers:
        pl.semaphore_signal(barrier, inc=1, device_id=p, device_id_type=MESH)
    pl.semaphore_wait(barrier, len(peers))


def _gather_body(na, slabs):
    def body(*refs):
        ins, outs = refs[:na], refs[na:2 * na]
        send_sems, recv_sems, local_sems = refs[2 * na:]
        x, y, c = _mesh_pos()
        me, sib = (x, y, c), (x, y, 1 - c)
        chips = [(1 - x, y), (x, 1 - y), (1 - x, 1 - y)]
        _handshake([sib] + [(*chip, c) for chip in chips])

        def copy(a, k, block, to, src=None):
            dst = slabs[a](outs[a], _dev_index(block))
            return pltpu.make_async_remote_copy(
                src_ref=dst if src is None else src, dst_ref=dst,
                send_sem=send_sems.at[7 * a + k], recv_sem=recv_sems.at[7 * a + k],
                device_id=to, device_id_type=MESH)

        local = [pltpu.make_async_copy(ins[a], slabs[a](outs[a], _dev_index(me)), local_sems.at[a])
                 for a in range(na)]
        for cp in local:
            cp.start()
        first = []
        for a in range(na):
            first.append(copy(a, 0, me, sib, src=ins[a]))
            first += [copy(a, 1 + j, me, (*chip, c), src=ins[a]) for j, chip in enumerate(chips)]
        for cp in first:
            cp.start()
        passed = []
        for j, chip in enumerate(chips):
            for a in range(na):
                copy(a, 1 + j, (*chip, c), me).wait_recv()
                fwd = copy(a, 4 + j, (*chip, c), sib)
                fwd.start()
                passed.append(fwd)
        for a in range(na):
            copy(a, 0, sib, me).wait_recv()
            for j, chip in enumerate(chips):
                copy(a, 4 + j, (*chip, 1 - c), me).wait_recv()
        for cp in first + passed:
            cp.wait_send()
        for cp in local:
            cp.wait()

    return body


def _all_gather(name, cid, shards, out_shapes, slabs):
    na = len(shards)
    return pl.kernel(
        _gather_body(na, slabs), out_type=out_shapes,
        mesh=plsc.ScalarSubcoreMesh(axis_name="seq", num_cores=1), name=name,
        scratch_types=[pltpu.SemaphoreType.DMA((7 * na,)), pltpu.SemaphoreType.DMA((7 * na,)),
                       pltpu.SemaphoreType.DMA((na,))],
        compiler_params=pltpu.CompilerParams(collective_id=cid),
    )(*shards)


def _d2d_body(nb, ns):
    n = nb + ns

    def body(*refs):
        ins, outs = refs[:n], refs[n:2 * n]
        send_sems, recv_sems = refs[2 * n:]
        x, y, c = _mesh_pos()
        sib = (x, y, 1 - c)
        _handshake([sib])
        copies = []
        for t in range(n):
            cp = pltpu.make_async_remote_copy(
                src_ref=ins[t].at[:, 1 - c] if t < nb else ins[t], dst_ref=outs[t],
                send_sem=send_sems.at[t], recv_sem=recv_sems.at[t],
                device_id=sib, device_id_type=MESH)
            cp.start()
            copies.append(cp)
        for cp in copies:
            cp.wait()

    return body


def _rs_d2d(name, cid, bigs, smalls):
    shapes = [jax.ShapeDtypeStruct((g.shape[0],) + g.shape[2:], g.dtype) for g in bigs]
    shapes += [jax.ShapeDtypeStruct(s.shape, s.dtype) for s in smalls]
    n = len(shapes)
    return pl.kernel(
        _d2d_body(len(bigs), len(smalls)), out_type=shapes,
        mesh=plsc.ScalarSubcoreMesh(axis_name="seq", num_cores=1), name=name,
        scratch_types=[pltpu.SemaphoreType.DMA((n,)), pltpu.SemaphoreType.DMA((n,))],
        compiler_params=pltpu.CompilerParams(collective_id=cid),
    )(*bigs, *smalls)


def _pair_sum(name, g4, recv, cvec, row_mult):
    Q, _, R, C = g4.shape
    tr = _tile(R, 256, row_mult)

    def body(c_ref, a_ref, b_ref, o_ref):
        o_ref[...] = (a_ref[...].astype(F32) + b_ref[...].astype(F32)).astype(BF16)

    grid_spec = pltpu.PrefetchScalarGridSpec(
        num_scalar_prefetch=1, grid=(Q, R // tr),
        in_specs=[pl.BlockSpec((None, None, tr, C), lambda q, i, c_ref: (q, c_ref[0], i, 0)),
                  pl.BlockSpec((None, tr, C), lambda q, i, c_ref: (q, i, 0))],
        out_specs=pl.BlockSpec((None, tr, C), lambda q, i, c_ref: (q, i, 0)))
    return pl.pallas_call(
        body, name=name, grid_spec=grid_spec, out_shape=jax.ShapeDtypeStruct((Q, R, C), BF16),
        compiler_params=_params(("arbitrary", "arbitrary")),
    )(cvec, g4, recv)


def _add2(name, a, b):
    R, C = a.shape
    tr = _tile(R, 512, 8)

    def body(a_ref, b_ref, o_ref):
        o_ref[...] = a_ref[...] + b_ref[...]

    blk = pl.BlockSpec((tr, C), lambda i: (i, 0))
    return pl.pallas_call(body, name=name, grid=(R // tr,), in_specs=[blk, blk], out_specs=blk,
                          out_shape=jax.ShapeDtypeStruct((R, C), a.dtype),
                          compiler_params=_params(("arbitrary",)))(a, b)


def _ici_copies(nb, n, srcs, lands, send_sems, recv_sems):
    x, y, c = _mesh_pos()
    q_me = 2 * x + y
    copies = []
    for t in range(n):
        for k in range(1, 4):
            px, py = x ^ (k >> 1), y ^ (k & 1)
            copies.append(pltpu.make_async_remote_copy(
                src_ref=srcs[t].at[2 * px + py] if t < nb else srcs[t], dst_ref=lands[t].at[q_me],
                send_sem=send_sems.at[3 * t + k - 1], recv_sem=recv_sems.at[3 * t + k - 1],
                device_id=(px, py, c), device_id_type=MESH))
    return copies


_HBM = pl.BlockSpec(memory_space=pltpu.HBM)
_SEM = pl.BlockSpec(memory_space=pltpu.SEMAPHORE)
_DATAFLOW = pltpu.SideEffectType.DATAFLOW_SIDE_EFFECTING


def _ici_start(name, bigs, smalls):
    nb, n = len(bigs), len(bigs) + len(smalls)
    srcs = list(bigs) + list(smalls)
    lands = [lax.empty(g.shape, g.dtype) for g in bigs] + [lax.empty((N_DEV // 2,) + s.shape, s.dtype) for s in smalls]

    def body(*refs):
        src_refs, land_refs = refs[:n], refs[n:2 * n]
        send_sems, recv_sems = refs[2 * n], refs[2 * n + 1]
        token = refs[-1]
        for cp in _ici_copies(nb, n, src_refs, land_refs, send_sems, recv_sems):
            cp.start()
        token[...] = jnp.zeros_like(token)

    hbm = [pltpu.HBM(a.shape, a.dtype) for a in srcs + lands]
    args = [pltpu.with_memory_space_constraint(a, pltpu.HBM) for a in srcs + lands]
    out = pl.pallas_call(
        body, name=name,
        out_shape=[pltpu.SemaphoreType.DMA((3 * n,)), pltpu.SemaphoreType.DMA((3 * n,))] + hbm
                  + [jax.ShapeDtypeStruct((8, 128), F32)],
        in_specs=[_HBM] * (2 * n),
        out_specs=[_SEM, _SEM] + [_HBM] * (2 * n) + [pl.BlockSpec(memory_space=pltpu.VMEM)],
        input_output_aliases={i: 2 + i for i in range(2 * n)},
        compiler_params=pltpu.CompilerParams(has_side_effects=_DATAFLOW),
    )(*args)
    return out[0], out[1], list(out[2:2 + n]), list(out[2 + n:2 + 2 * n]), out[-1]


def _ici_wait(name, started, nb, not_before):
    send_sems, recv_sems, srcs, lands, token = started
    n = len(srcs)

    def body(*refs):
        src_refs, land_refs = refs[:n], refs[n:2 * n]
        send_refs, recv_refs = refs[2 * n], refs[2 * n + 1]
        for cp in _ici_copies(nb, n, src_refs, land_refs, send_refs, recv_refs):
            cp.wait_send()
            cp.wait_recv()

    hbm = [pltpu.HBM(a.shape, a.dtype) for a in srcs + lands]
    out = pl.pallas_call(
        body, name=name, out_shape=hbm,
        in_specs=[_HBM] * (2 * n) + [_SEM, _SEM] + [pl.BlockSpec(memory_space=pl.ANY)] * 2,
        out_specs=[_HBM] * (2 * n),
        input_output_aliases={i: i for i in range(2 * n)},
        compiler_params=pltpu.CompilerParams(has_side_effects=_DATAFLOW),
    )(*srcs, *lands, send_sems, recv_sems, not_before, token)
    return list(out[:n]), list(out[n:])


def _adamw_math(w, g, m, v):
    m2 = ADAM_B1 * m + (1.0 - ADAM_B1) * g
    v2 = ADAM_B2 * v + (1.0 - ADAM_B2) * (g * g)
    m_hat = m2 / (1.0 - ADAM_B1 ** ADAM_STEP)
    v_hat = v2 / (1.0 - ADAM_B2 ** ADAM_STEP)
    delta = -ADAM_LR * (m_hat / (jnp.sqrt(v_hat) + ADAM_EPS) + ADAM_WD * w)
    return delta, m2, v2


def _chip_terms(q, own, land_ref, n):
    return [jnp.where(q == s, own, land_ref[s]).astype(F32) for s in range(n)]


def _adamw_sum(name, land, own, qvec, w, m, v, l, prev, row_mult):
    NS, R, C = land.shape
    L = w.shape[0]
    tr = _tile(R, 128, row_mult)

    def body(*refs):
        q_ref, r_ref, o_ref, w_ref, m_ref, v_ref = refs[:6]
        g_ref, d_ref, nm_ref, nv_ref = refs[-4:]
        terms = _chip_terms(q_ref[0], o_ref[...], r_ref, NS)
        g = terms[0]
        for t in terms[1:]:
            g = g + t
        d, m2, v2 = _adamw_math(w_ref[...], g, m_ref[...], v_ref[...])
        g_ref[...] = g
        d_ref[...] = d
        nm_ref[...] = m2
        nv_ref[...] = v2

    blk = pl.BlockSpec((None, tr, C), lambda i, q: (l, i, 0))
    shp = jax.ShapeDtypeStruct((L, R, C), F32)
    in_specs = [pl.BlockSpec((NS, tr, C), lambda i, q: (0, i, 0)),
                pl.BlockSpec((None, tr, C), lambda i, q: (q[0], i, 0)), blk, blk, blk]
    args = [land, own, w, m, v]
    aliases = {}
    if prev is not None:
        in_specs += [pl.BlockSpec(memory_space=pl.ANY)] * 4
        args += list(prev)
        aliases = {6 + i: i for i in range(4)}
    grid_spec = pltpu.PrefetchScalarGridSpec(num_scalar_prefetch=1, grid=(R // tr,), in_specs=in_specs,
                                             out_specs=[blk, blk, blk, blk])
    return pl.pallas_call(
        body, name=name, grid_spec=grid_spec, out_shape=[shp, shp, shp, shp],
        input_output_aliases=aliases,
        compiler_params=_params(("arbitrary",)),
    )(qvec, *args)


def _adamw_plain(name, g, w, m, v):
    def body(g_ref, w_ref, m_ref, v_ref, d_ref, nm_ref, nv_ref):
        d, m2, v2 = _adamw_math(w_ref[...], g_ref[...], m_ref[...], v_ref[...])
        d_ref[...] = d
        nm_ref[...] = m2
        nv_ref[...] = v2

    shp = jax.ShapeDtypeStruct(g.shape, F32)
    return pl.pallas_call(body, name=name, out_shape=[shp, shp, shp], compiler_params=_params())(g, w, m, v)


SMALL = ["norm1_g", "q_norm_g", "k_norm_g", "sink", "sgu_ln_g", "sgu_ln_b", "w_s", "b_s",
         "attn_out_g", "sgu_out_g", "norm2_g", "conv_b"]
PACK_ALIGN = 1024


def _pack(pieces):
    flat = []
    for p in pieces:
        f = p.reshape(-1).astype(F32)
        pad = (-f.shape[0]) % PACK_ALIGN
        flat.append(jnp.pad(f, (0, pad)) if pad else f)
    return jnp.concatenate(flat).reshape(-1, 128)


def _unpack(packed, shapes):
    flat = packed.reshape(-1)
    out, off = [], 0
    for shp in shapes:
        n = 1
        for d in shp:
            n *= d
        out.append(flat[off:off + n].reshape(shp))
        off += n + ((-n) % PACK_ALIGN)
    return out


def kernel(x, norm1_g, w_in, q_norm_g, k_norm_g, sink, sgu_ln_g, sgu_ln_b, w_s, b_s, attn_out_g, sgu_out_g, w_o, norm2_g, w_up, conv_w, conv_b, w_down, loss_target, m_norm1_g, m_w_in, m_q_norm_g, m_k_norm_g, m_sink, m_sgu_ln_g, m_sgu_ln_b, m_w_s, m_b_s, m_attn_out_g, m_sgu_out_g, m_w_o, m_norm2_g, m_w_up, m_conv_w, m_conv_b, m_w_down, v_norm1_g, v_w_in, v_q_norm_g, v_k_norm_g, v_sink, v_sgu_ln_g, v_sgu_ln_b, v_w_s, v_b_s, v_attn_out_g, v_sgu_out_g, v_w_o, v_norm2_g, v_w_up, v_conv_w, v_conv_b, v_w_down):
    weights = dict(norm1_g=norm1_g, w_in=w_in, q_norm_g=q_norm_g, k_norm_g=k_norm_g, sink=sink, sgu_ln_g=sgu_ln_g,
                   sgu_ln_b=sgu_ln_b, w_s=w_s, b_s=b_s, attn_out_g=attn_out_g, sgu_out_g=sgu_out_g, w_o=w_o,
                   norm2_g=norm2_g, w_up=w_up, conv_w=conv_w, conv_b=conv_b, w_down=w_down)
    mom_m = dict(norm1_g=m_norm1_g, w_in=m_w_in, q_norm_g=m_q_norm_g, k_norm_g=m_k_norm_g, sink=m_sink,
                 sgu_ln_g=m_sgu_ln_g, sgu_ln_b=m_sgu_ln_b, w_s=m_w_s, b_s=m_b_s, attn_out_g=m_attn_out_g,
                 sgu_out_g=m_sgu_out_g, w_o=m_w_o, norm2_g=m_norm2_g, w_up=m_w_up, conv_w=m_conv_w,
                 conv_b=m_conv_b, w_down=m_w_down)
    mom_v = dict(norm1_g=v_norm1_g, w_in=v_w_in, q_norm_g=v_q_norm_g, k_norm_g=v_k_norm_g, sink=v_sink,
                 sgu_ln_g=v_sgu_ln_g, sgu_ln_b=v_sgu_ln_b, w_s=v_w_s, b_s=v_b_s, attn_out_g=v_attn_out_g,
                 sgu_out_g=v_sgu_out_g, w_o=v_w_o, norm2_g=v_norm2_g, w_up=v_w_up, conv_w=v_conv_w,
                 conv_b=v_conv_b, w_down=v_w_down)
    order = ["norm1_g", "w_in", "q_norm_g", "k_norm_g", "sink", "sgu_ln_g", "sgu_ln_b", "w_s", "b_s",
             "attn_out_g", "sgu_out_g", "w_o", "norm2_g", "w_up", "conv_w", "conv_b", "w_down"]

    _, S, D = x.shape
    L = w_in.shape[0]
    AW = D // 2
    NQ = AW // HEAD
    NKV = max(1, NQ // 4)
    G = NQ // NKV
    KVW = NKV * HEAD
    GW = D - AW
    NG = GW // HEAD
    IN = AW + 2 * KVW + 2 * GW
    INS = w_in.shape[2]
    OS = w_o.shape[1]
    US = w_up.shape[2]
    DS = w_down.shape[1]
    F2 = US * N_DEV
    F = F2 // 2
    assert INS * N_DEV == IN and OS * N_DEV == D and DS * N_DEV == F and AW == GW and (3 * AW) % (2 * KVW) == 0
    cfg = dict(S=S, D=D, AW=AW, NQ=NQ, NKV=NKV, G=G, KVW=KVW, GW=GW, NG=NG, F=F, F2=F2)

    def rows_slab(n):
        return lambda ref, idx: ref.at[:, pl.ds(pl.multiple_of(idx * n, n), n), :]

    def cols_slab(n):
        return lambda ref, idx: ref.at[:, :, pl.ds(pl.multiple_of(idx * n, n), n)]

    def after(first, then):
        return lax.optimization_barrier((first, then))

    w_in_t, m_w_in_t, v_w_in_t = (jnp.swapaxes(a, 1, 2) for a in (w_in, m_w_in, v_w_in))

    ids = iter(range(64))
    w_in_f, w_o_f, w_up_f, w_down_f = [], [], [], []

    def gather_layer(l, not_before=None):
        in_shard = w_in_t[l:l + 1].astype(BF16)
        if not_before is not None:
            _, in_shard = after(not_before, in_shard)
        w_in_f.extend(_all_gather("ag_in", next(ids), [in_shard],
                                  [jax.ShapeDtypeStruct((1, IN, D), BF16)], [rows_slab(INS)]))
        w_o_f.extend(_all_gather("ag_o", next(ids), [w_o[l:l + 1].astype(BF16)],
                                 [jax.ShapeDtypeStruct((1, D, D), BF16)], [rows_slab(OS)]))
        w_up_f.extend(_all_gather("ag_up", next(ids), [w_up[l:l + 1].astype(BF16)],
                                  [jax.ShapeDtypeStruct((1, D, F2), BF16)], [cols_slab(US)]))
        w_down_f.extend(_all_gather("ag_down", next(ids), [w_down[l:l + 1].astype(BF16)],
                                    [jax.ShapeDtypeStruct((1, F, D), BF16)], [rows_slab(DS)]))

    gather_layer(0)
    (conv_w_f,) = _all_gather("ag_conv_w", next(ids), [conv_w], [jax.ShapeDtypeStruct((L, 3, F2), F32)],
                              [cols_slab(US)])

    n1g3, n2g3 = norm1_g.reshape(L, 1, D), norm2_g.reshape(L, 1, D)
    qg3, kg3 = q_norm_g.reshape(L, 1, HEAD), k_norm_g.reshape(L, 1, HEAD)
    lng3, lnb3 = sgu_ln_g.reshape(L, 1, GW), sgu_ln_b.reshape(L, 1, GW)
    ag3, og3 = attn_out_g.reshape(L, 1, AW), sgu_out_g.reshape(L, 1, GW)
    cb3 = conv_b.reshape(L, 1, F2)
    ws_b = w_s.astype(BF16)
    wst_b = jnp.swapaxes(w_s, 2, 3).astype(BF16)
    bs_b = jnp.broadcast_to(b_s[..., None], (L, NG, HEAD, HEAD))
    inv_freq = ROPE_THETA ** (-jnp.arange(0, HEAD, 2, dtype=F32) / HEAD)
    ang = jnp.arange(S, dtype=F32)[:, None] * inv_freq[None, :]
    cos2 = jnp.concatenate([jnp.cos(ang), jnp.cos(ang)], axis=1)
    sin2 = jnp.concatenate([-jnp.sin(ang), jnp.sin(ang)], axis=1)

    tn = 512
    t_in, t_d, t_f, t_f2 = _tile(IN, tn, 128), _tile(D, tn, 128), _tile(F, tn, 128), _tile(F2, tn, 128)
    tm_f = _tile(F, 512, 128)
    tm_s = _tile(S, 1024, 128)
    assert AW % t_in == 0 and (2 * KVW) % t_in == 0
    n_q, n_kv, n_g = AW // t_in, (2 * KVW) // t_in, (2 * GW) // t_in

    def in_tile(j):
        return jnp.where(j < n_q, j, jnp.where(j < n_q + n_g, j + n_kv, j - n_g))

    def w_spec(tk, tn_):
        return pl.BlockSpec((None, tk, tn_), lambda i, j, k: (0, k, j))

    def wt_spec(tn_, tk):
        return pl.BlockSpec((None, tn_, tk), lambda i, j, k: (0, j, k))

    def a_spec(tm, tk):
        return pl.BlockSpec((tm, tk), lambda i, j, k: (i, k))

    def at_spec(tk, tm):
        return pl.BlockSpec((tk, tm), lambda i, j, k: (k, i))

    def b_spec(tk, tn_):
        return pl.BlockSpec((tk, tn_), lambda i, j, k: (k, j))

    xs = x.reshape(S, D)
    saved = []
    cur = xs
    for l in range(L):
        h = _rms_fwd("rms1_fwd", cur, n1g3, l)
        z = _mm("mm_in", h, w_in_f[l], M=S, N=IN, K=D, tm=S, tn=t_in, tk=D, tb=True,
                a_spec=a_spec(S, D),
                b_spec=pl.BlockSpec((None, t_in, D), lambda i, j, k: (0, in_tile(j), 0)))
        q_r, k_r, v_b = _qkv_prep("qkv_prep", z, qg3, kg3, cos2, sin2, l, cfg)
        attn, mix_l = _attn_fwd("attn_fwd", q_r, k_r, v_b, sink, ag3, l, cfg)
        mix_r = _sgu_fwd("sgu_fwd", z, lng3, lnb3, ws_b, bs_b, og3, l, cfg)
        mixed = jnp.concatenate([mix_l, mix_r], axis=1)
        x1 = _mm("mm_o", mixed, w_o_f[l], M=S, N=D, K=D, tm=S, tn=t_d, tk=D,
                 a_spec=a_spec(S, D), b_spec=w_spec(D, t_d), res=cur)
        if l + 1 < L:
            gather_layer(l + 1, not_before=x1)
        h2 = _rms_fwd("rms2_fwd", x1, n2g3, l)
        ap = _mm("mm_up", h2, w_up_f[l], M=S, N=F2, K=D, tm=S, tn=t_f2, tk=D,
                 a_spec=a_spec(S, D), b_spec=w_spec(D, t_f2))
        y_b = _conv_glu_fwd("conv_glu_fwd", ap, conv_w_f, cb3, l, cfg)
        x2 = _mm("mm_down", y_b, w_down_f[l], M=S, N=D, K=F, tm=tm_s, tn=t_d, tk=F,
                 a_spec=a_spec(tm_s, F), b_spec=w_spec(F, t_d), res=x1)
        saved.append(dict(x=cur, h=h, z=z, q=q_r, k=k_r, v=v_b, attn=attn, mixed=mixed, x1=x1, h2=h2, ap=ap, y=y_b))
        cur = x2

    loss_tile, dx, dxb = _loss_bwd("loss", cur, loss_target.reshape(S, D))
    loss = lax.psum(loss_tile[0, 0], ("x", "y", "c"))

    gS = [dict() for _ in range(L)]
    recv = [dict() for _ in range(L)]
    cvec = jnp.reshape(lax.axis_index("c"), (1,)).astype(jnp.int32)

    def rs_swap(name, g, R, C, small=None):
        g4 = g.reshape(N_DEV // 2, 2, R, C)
        got = _rs_d2d("rs_d2d_" + name, next(ids), [g4], [] if small is None else [small])
        return name, g4, got, small

    started = []

    def rs_finish(l, pending, then=None):
        name, g4, got, small = pending
        chip = _pair_sum("pair_sum_" + name, g4, got[0], cvec, 16)
        chip_small = [] if small is None else [_add2("pair_sum_small", small, got[1])]
        begun = _ici_start("rs_ici_" + name, [chip], chip_small)
        if then is not None:
            token, then = after(begun[4], then)
            begun = begun[:4] + (token,)
        started.append((l, name, begun))
        return then

    carried = None
    for l in reversed(range(L)):
        sv = saved[l]
        dy = _mm("mm_dy", dxb, w_down_f[l], M=S, N=F, K=D, tm=S, tn=t_f, tk=D, tb=True,
                 a_spec=a_spec(S, D), b_spec=wt_spec(t_f, D))
        if carried is not None:
            dy = rs_finish(l + 1, carried, dy)
        g_down = _mm("mm_gdown", sv["y"], dxb, M=F, N=D, K=S, tm=tm_f, tn=D, tk=S, ta=True,
                     a_spec=at_spec(S, tm_f), b_spec=b_spec(S, D), out_dtype=BF16)
        g_down, dy = after(g_down, dy)
        swap_down = rs_swap("w_down", g_down, DS, D)
        dap3, dcw, dcb = _glu_conv_bwd("glu_conv_bwd", dy, sv["ap"], conv_w_f, cb3, l, cfg)
        gS[l]["conv_w"] = jnp.concatenate([dcw[0], dcw[1]], axis=1)
        gS[l]["conv_b"] = jnp.concatenate([dcb[0], dcb[1]], axis=1).reshape(F2)
        dap3 = rs_finish(l, swap_down, dap3)
        dh2 = _mm("mm_dh2", dap3, w_up_f[l], M=S, N=D, K=F2, tm=tm_s, tn=t_d, tk=F, tb=True,
                  a_spec=pl.BlockSpec((None, tm_s, F), lambda i, j, k: (k, i, 0)),
                  b_spec=wt_spec(t_d, F))
        g_up = _mm("mm_gup", sv["h2"], dap3, M=D, N=F2, K=S, tm=D, tn=US, tk=S, ta=True,
                   a_spec=at_spec(S, D),
                   b_spec=pl.BlockSpec((None, S, US), lambda i, j, k: (j // (N_DEV // 2), 0, j % (N_DEV // 2))),
                   out_dtype=BF16, out_shape=(N_DEV, D, US),
                   out_spec=pl.BlockSpec((None, D, US), lambda i, j, k: (j, 0, 0)))
        g_up, dh2 = after(g_up, dh2)
        swap_up = rs_swap("w_up", g_up, D, US)
        dx1, dx1b, dg2 = _rms_bwd("rms2_bwd", sv["x1"], dh2, dx, n2g3, l)
        gS[l]["norm2_g"] = dg2.reshape(D)
        dmix = _mm("mm_dmix", dx1b, w_o_f[l], M=S, N=D, K=D, tm=S, tn=t_d, tk=D, tb=True,
                   a_spec=a_spec(S, D), b_spec=wt_spec(t_d, D))
        g_o = _mm("mm_go", sv["mixed"], dx1b, M=D, N=D, K=S, tm=D, tn=t_d, tk=S, ta=True,
                  a_spec=at_spec(S, D), b_spec=b_spec(S, t_d), out_dtype=BF16)
        g_o, dmix = after(g_o, dmix)
        swap_o = rs_swap("w_o", g_o, OS, D)
        dmix = rs_finish(l, swap_up, dmix)
        dq_r, dk_pad, dv_pad, dsink, dag = _attn_bwd("attn_bwd", sv["q"], sv["k"], sv["v"], sv["attn"], dmix,
                                                     sink, ag3, l, cfg)
        dq_r = rs_finish(l, swap_o, dq_r)
        gS[l]["sink"] = dsink[:, 0]
        gS[l]["attn_out_g"] = dag.reshape(AW)
        dzgu, dzgv, dws, dbs, dlng, dlnb, dog = _sgu_bwd("sgu_bwd", sv["z"], dmix, lng3, lnb3, ws_b, wst_b, bs_b,
                                                        og3, l, cfg)
        gS[l]["w_s"] = dws
        gS[l]["b_s"] = dbs[:, :, 0]
        gS[l]["sgu_ln_g"] = dlng.reshape(GW)
        gS[l]["sgu_ln_b"] = dlnb.reshape(GW)
        gS[l]["sgu_out_g"] = dog.reshape(GW)
        dzq, dzkv, dqg, dkg = _qkv_prep_bwd("qkv_prep_bwd", sv["z"], dq_r, dk_pad, dv_pad, qg3, kg3, cos2, sin2,
                                            l, cfg)
        gS[l]["q_norm_g"] = dqg.reshape(HEAD)
        gS[l]["k_norm_g"] = dkg.reshape(HEAD)
        dz = jnp.concatenate([dzq, dzkv, dzgu, dzgv], axis=1)
        dh = _mm("mm_dh", dz, w_in_f[l], M=S, N=D, K=IN, tm=tm_s, tn=t_d, tk=IN,
                 a_spec=a_spec(tm_s, IN), b_spec=w_spec(IN, t_d))
        g_in = _mm("mm_gin", dz, sv["h"], M=IN, N=D, K=S, tm=t_in, tn=D, tk=S, ta=True,
                   a_spec=at_spec(S, t_in), b_spec=b_spec(S, D), out_dtype=BF16)
        g_in, dh = after(g_in, dh)
        dx, dxb, dg1 = _rms_bwd("rms1_bwd", sv["x"], dh, dx1, n1g3, l)
        gS[l]["norm1_g"] = dg1.reshape(D)
        carried = rs_swap("w_in", g_in, INS, D, small=_pack([gS[l][n] for n in SMALL] + [gS[l]["conv_w"]]))
    dx = rs_finish(0, carried, dx)
    grad_x = dx.reshape(1, S, D)

    grads, deltas, new_m, new_v = {}, {}, {}, {}
    results, placed = {}, dx
    qvec = jnp.reshape(2 * lax.axis_index("x") + lax.axis_index("y"), (1,)).astype(jnp.int32)
    for l, name, begun in started:
        sent, landed = _ici_wait("rs_wait_" + name, begun, 1, placed)
        recv[l][name] = (sent, landed)
        wmv = (w_in_t, m_w_in_t, v_w_in_t) if name == "w_in" else (weights[name], mom_m[name], mom_v[name])
        results[name] = _adamw_sum("adamw_" + name, landed[0], sent[0], qvec, *wmv, l, results.get(name), 16)
        placed = results[name][0]
    results["w_in"] = [jnp.swapaxes(r, 1, 2) for r in results["w_in"]]
    for name, res in results.items():
        grads[name], deltas[name], new_m[name], new_v[name] = res

    small_shapes = [weights[n].shape[1:] for n in SMALL] + [(3, F2)]
    SR = recv[0]["w_in"][0][1].shape[0]
    tr_s = _tile(SR, 512, 8)
    sblk = pl.BlockSpec((tr_s, 128), lambda i, q: (i, 0))
    sshp = jax.ShapeDtypeStruct((SR, 128), F32)
    small_parts = [[None] * L for _ in range(4)]
    for l in reversed(range(L)):
        def small_body(q_ref, r_ref, o_ref, w_ref, m_ref, v_ref, g_ref, d_ref, nm_ref, nv_ref):
            terms = _chip_terms(q_ref[0], o_ref[...], r_ref, N_DEV // 2)
            g = terms[0]
            for t in terms[1:]:
                g = g + t
            d, m2, v2 = _adamw_math(w_ref[...], g, m_ref[...], v_ref[...])
            g_ref[...] = g
            d_ref[...] = d
            nm_ref[...] = m2
            nv_ref[...] = v2

        def pack_small(src):
            return _pack([src[n][l] for n in SMALL] + [jnp.zeros((3, F2), F32)])

        sent, landed = recv[l]["w_in"]
        packed = pl.pallas_call(
            small_body, name="adamw_small",
            grid_spec=pltpu.PrefetchScalarGridSpec(
                num_scalar_prefetch=1, grid=(SR // tr_s,),
                in_specs=[pl.BlockSpec((N_DEV // 2, tr_s, 128), lambda i, q: (0, i, 0)), sblk, sblk, sblk, sblk],
                out_specs=[sblk, sblk, sblk, sblk]),
            out_shape=[sshp, sshp, sshp, sshp],
            compiler_params=_params(("arbitrary",)),
        )(qvec, landed[1], sent[1], pack_small(weights), pack_small(mom_m), pack_small(mom_v))
        for t in range(4):
            small_parts[t][l] = _unpack(packed[t], small_shapes)
    for t, store in enumerate([grads, deltas, new_m, new_v]):
        for i, n in enumerate(SMALL):
            store[n] = jnp.stack([small_parts[t][l][i] for l in range(L)])
    conv_full = jnp.stack([small_parts[0][l][len(SMALL)] for l in range(L)])
    me_i = _dev_index(_mesh_pos())
    g_cw = lax.dynamic_slice_in_dim(conv_full, me_i * US, US, axis=2)
    grads["conv_w"] = g_cw
    d_cw, m_cw, v_cw = _adamw_plain("adamw_conv_w", g_cw.reshape(L * 3, US), conv_w.reshape(L * 3, US),
                                    m_conv_w.reshape(L * 3, US), v_conv_w.reshape(L * 3, US))
    deltas["conv_w"] = d_cw.reshape(L, 3, US)
    new_m["conv_w"] = m_cw.reshape(L, 3, US)
    new_v["conv_w"] = v_cw.reshape(L, 3, US)

    return (loss, grad_x, *[grads[n] for n in order], *[deltas[n] for n in order],
            *[new_m[n] for n in order], *[new_v[n] for n in order])
```

```python
import jax
import jax.numpy as jnp
from jax import lax
from jax.experimental import pallas as pl
from jax.experimental.pallas import tpu as pltpu
from jax.experimental.pallas import tpu_sc as plsc

F32 = jnp.float32
BF16 = jnp.bfloat16
MESH = pl.DeviceIdType.MESH

N_DEV = 8
HEAD = 128
EPS = 1e-6
MASK_VALUE = -1e30
ROPE_THETA = 10000.0
GELU_C = 0.7978845608028654
GELU_A = 0.044715

ADAM_LR = 0.001
ADAM_B1 = 0.9
ADAM_B2 = 0.999
ADAM_EPS = 1e-08
ADAM_WD = 0.01
ADAM_STEP = 10

VMEM_LIMIT = 56 * 1024 * 1024


def _tile(n, pref, mult):
    best = None
    for t in range(mult, min(n, pref) + 1, mult):
        if n % t == 0:
            best = t
    return n if best is None else best


def _params(sem=None):
    kw = dict(vmem_limit_bytes=VMEM_LIMIT)
    if sem is not None:
        kw["dimension_semantics"] = sem
    return pltpu.CompilerParams(**kw)


def _gelu(x):
    return x * (0.5 * (1.0 + jnp.tanh(GELU_C * (x + GELU_A * (x * x * x)))))


def _gelu_grad(x):
    t = jnp.tanh(GELU_C * (x + GELU_A * (x * x * x)))
    return 0.5 * (1.0 + t) + 0.5 * x * (1.0 - t * t) * (GELU_C * (1.0 + 3.0 * GELU_A * (x * x)))


def _sigmoid(x):
    return 1.0 / (1.0 + jnp.exp(-x))


def _vec_spec(l, n):
    return pl.BlockSpec((None, 1, n), lambda *_: (l, 0, 0))


def _mm(name, a, b, *, M, N, K, tm, tn, tk, a_spec, b_spec, ta=False, tb=False, out_dtype=F32, res=None,
        out_shape=None, out_spec=None):
    nm, nn, nk = M // tm, N // tn, K // tk
    assert nm * tm == M and nn * tn == N and nk * tk == K
    assert not (ta and nk > 1)
    dims = (((1,), (1,)), ((), ())) if tb else (((1,), (0,)), ((), ()))

    def body(*refs):
        refs = list(refs)
        a_ref = refs.pop(0)
        b_ref = refs.pop(0)
        r_ref = refs.pop(0) if res is not None else None
        o_ref = refs.pop(0)
        acc = refs.pop(0) if nk > 1 else None
        at = refs.pop(0) if ta else None
        k = pl.program_id(2)
        if ta:
            @pl.when(pl.program_id(1) == 0)
            def _():
                at[...] = a_ref[...].T
            lhs = at[...]
        else:
            lhs = a_ref[...]
        p = lax.dot_general(lhs, b_ref[...], dims, preferred_element_type=F32)

        def finish(r):
            if r_ref is not None:
                r = r_ref[...] + r
            o_ref[...] = r.astype(out_dtype)

        if nk == 1:
            finish(p)
        else:
            @pl.when(k == 0)
            def _():
                acc[...] = p

            @pl.when(k > 0)
            def _():
                acc[...] += p

            @pl.when(k == nk - 1)
            def _():
                finish(acc[...])

    in_specs = [a_spec, b_spec]
    args = [a, b]
    if res is not None:
        in_specs.append(pl.BlockSpec((tm, tn), lambda i, j, k: (i, j)))
        args.append(res)
    scratch = []
    if nk > 1:
        scratch.append(pltpu.VMEM((tm, tn), F32))
    if ta:
        scratch.append(pltpu.VMEM((tm, tk), BF16))
    return pl.pallas_call(
        body, name=name, grid=(nm, nn, nk),
        in_specs=in_specs,
        out_specs=pl.BlockSpec((tm, tn), lambda i, j, k: (i, j)) if out_spec is None else out_spec,
        out_shape=jax.ShapeDtypeStruct((M, N) if out_shape is None else out_shape, out_dtype),
        scratch_shapes=scratch,
        compiler_params=_params(("arbitrary", "arbitrary", "arbitrary")),
    )(*args)


def _rms_fwd(name, x, g3, l):
    S, D = x.shape
    tr = _tile(S, 256, 16)

    def body(x_ref, g_ref, h_ref):
        xv = x_ref[...]
        r = lax.rsqrt(jnp.mean(xv * xv, axis=-1, keepdims=True) + EPS)
        h_ref[...] = ((xv * r) * g_ref[...]).astype(BF16)

    return pl.pallas_call(
        body, name=name, grid=(S // tr,),
        in_specs=[pl.BlockSpec((tr, D), lambda i: (i, 0)), _vec_spec(l, D)],
        out_specs=pl.BlockSpec((tr, D), lambda i: (i, 0)),
        out_shape=jax.ShapeDtypeStruct((S, D), BF16),
        compiler_params=_params(("arbitrary",)),
    )(x, g3)


def _rope(t, cos2, sin2):
    return t * cos2 + pltpu.roll(t, HEAD // 2, axis=1) * sin2


def _qkv_prep(name, z, qg3, kg3, cos2, sin2, l, cfg):
    S, AW, KVW, NQ, NKV = cfg["S"], cfg["AW"], cfg["KVW"], cfg["NQ"], cfg["NKV"]
    tr = _tile(S, 256, 16)
    kv_blk = (3 * AW) // (2 * KVW)

    def body(zq_ref, zkv_ref, qg_ref, kg_ref, c_ref, s_ref, q_ref, k_ref, v_ref):
        cosv, sinv = c_ref[...], s_ref[...]

        def norm_rope(t, g):
            r = lax.rsqrt(jnp.mean(t * t, axis=-1, keepdims=True) + EPS)
            return _rope((t * r) * g, cosv, sinv)

        for h in range(NQ):
            sl = slice(h * HEAD, (h + 1) * HEAD)
            q_ref[:, sl] = norm_rope(zq_ref[:, sl], qg_ref[...]).astype(BF16)
        for h in range(NKV):
            sl = slice(h * HEAD, (h + 1) * HEAD)
            k_ref[:, sl] = norm_rope(zkv_ref[:, sl], kg_ref[...]).astype(BF16)
        v_ref[...] = zkv_ref[:, KVW:].astype(BF16)

    return pl.pallas_call(
        body, name=name, grid=(S // tr,),
        in_specs=[pl.BlockSpec((tr, AW), lambda i: (i, 0)),
                  pl.BlockSpec((tr, 2 * KVW), lambda i: (i, kv_blk)),
                  _vec_spec(l, HEAD), _vec_spec(l, HEAD),
                  pl.BlockSpec((tr, HEAD), lambda i: (i, 0)),
                  pl.BlockSpec((tr, HEAD), lambda i: (i, 0))],
        out_specs=[pl.BlockSpec((tr, AW), lambda i: (i, 0)),
                   pl.BlockSpec((tr, KVW), lambda i: (i, 0)),
                   pl.BlockSpec((tr, KVW), lambda i: (i, 0))],
        out_shape=[jax.ShapeDtypeStruct((S, AW), BF16),
                   jax.ShapeDtypeStruct((S, KVW), BF16),
                   jax.ShapeDtypeStruct((S, KVW), BF16)],
        compiler_params=_params(("arbitrary",)),
    )(z, z, qg3, kg3, cos2, sin2)


def _band_specs(width, nb):
    return [pl.BlockSpec((HEAD, width), lambda n: (jnp.maximum(n - 1, 0), 0)),
            pl.BlockSpec((HEAD, width), lambda n: (n, 0)),
            pl.BlockSpec((HEAD, width), lambda n: (jnp.minimum(n + 1, nb - 1), 0))]


def _attn_probs(qs, kj, n, sink_of_row, S, G):
    s = lax.dot_general(qs, kj, (((1,), (1,)), ((), ())), preferred_element_type=F32) * (HEAD ** -0.5)
    rows = lax.broadcasted_iota(jnp.int32, (G * HEAD, 3 * HEAD), 0)
    cols = lax.broadcasted_iota(jnp.int32, (G * HEAD, 3 * HEAD), 1)
    qi = rows & (HEAD - 1)
    kpos = n * HEAD - HEAD + cols
    valid = (cols >= qi) & (cols <= qi + 2 * HEAD) & (kpos >= 0) & (kpos < S)
    s = jnp.where(valid, s, MASK_VALUE)
    m = jnp.maximum(jnp.max(s, axis=-1, keepdims=True), sink_of_row)
    p = jnp.exp(s - m)
    e_sink = jnp.exp(sink_of_row - m)
    inv = 1.0 / (jnp.sum(p, axis=-1, keepdims=True) + e_sink)
    return p * inv, e_sink * inv


def _sink_rows(sink_ref, l, j, G):
    hidx = lax.broadcasted_iota(jnp.int32, (G * HEAD, 1), 0) // HEAD
    col = jnp.full((G * HEAD, 1), sink_ref[l, j * G], F32)
    for g in range(1, G):
        col = jnp.where(hidx == g, sink_ref[l, j * G + g], col)
    return col


def _attn_fwd(name, q, k, v, sink, ag3, l, cfg):
    S, AW, KVW, NKV, G = cfg["S"], cfg["AW"], cfg["KVW"], cfg["NKV"], cfg["G"]
    nb = S // HEAD

    def body(q_ref, kp, kc, kn, vp, vc, vn, sink_ref, ag_ref, a_ref, mix_ref):
        n = pl.program_id(0)
        kb = jnp.concatenate([kp[...], kc[...], kn[...]], axis=0)
        vb = jnp.concatenate([vp[...], vc[...], vn[...]], axis=0)
        for j in range(NKV):
            sl = slice(j * HEAD, (j + 1) * HEAD)
            qs = jnp.concatenate([q_ref[:, (j * G + g) * HEAD:(j * G + g + 1) * HEAD] for g in range(G)], axis=0)
            probs, _ = _attn_probs(qs, kb[:, sl], n, _sink_rows(sink_ref, l, j, G), S, G)
            o = jnp.dot(probs.astype(BF16), vb[:, sl], preferred_element_type=F32)
            for g in range(G):
                a_ref[:, (j * G + g) * HEAD:(j * G + g + 1) * HEAD] = o[g * HEAD:(g + 1) * HEAD]
        a = a_ref[...]
        r = lax.rsqrt(jnp.mean(a * a, axis=-1, keepdims=True) + EPS)
        mix_ref[...] = ((a * r) * ag_ref[...]).astype(BF16)

    return pl.pallas_call(
        body, name=name, grid=(nb,),
        in_specs=[pl.BlockSpec((HEAD, AW), lambda n: (n, 0))] + _band_specs(KVW, nb) + _band_specs(KVW, nb)
                 + [pl.BlockSpec(memory_space=pltpu.SMEM), _vec_spec(l, AW)],
        out_specs=[pl.BlockSpec((HEAD, AW), lambda n: (n, 0)), pl.BlockSpec((HEAD, AW), lambda n: (n, 0))],
        out_shape=[jax.ShapeDtypeStruct((S, AW), F32), jax.ShapeDtypeStruct((S, AW), BF16)],
        compiler_params=_params(("arbitrary",)),
    )(q, k, k, k, v, v, v, sink, ag3)


def _sgu_forward_math(gu, gv, lng, lnb):
    u = _gelu(gu)
    vv = _gelu(gv)
    mu = jnp.mean(vv, axis=-1, keepdims=True)
    xc = vv - mu
    rstd = lax.rsqrt(jnp.mean(xc * xc, axis=-1, keepdims=True) + EPS)
    xhat = xc * rstd
    vn = xhat * lng + lnb
    return u, xhat, rstd, vn


def _sgu_fwd(name, z, lng3, lnb3, ws_b, bs_b, og3, l, cfg):
    S, GW, NG = cfg["S"], cfg["GW"], cfg["NG"]

    def body(gu_ref, gv_ref, lng_ref, lnb_ref, ws_ref, bs_ref, og_ref, mix_ref, sg_ref):
        u, _, _, vn = _sgu_forward_math(gu_ref[...], gv_ref[...], lng_ref[...], lnb_ref[...])
        vnb = vn.astype(BF16)
        for h in range(NG):
            sl = slice(h * HEAD, (h + 1) * HEAD)
            f = jnp.dot(ws_ref[h], vnb[:, sl], preferred_element_type=F32) + bs_ref[h]
            sg_ref[:, sl] = u[:, sl] * f
        sg = sg_ref[...]
        r = lax.rsqrt(jnp.mean(sg * sg, axis=-1, keepdims=True) + EPS)
        mix_ref[...] = ((sg * r) * og_ref[...]).astype(BF16)

    return pl.pallas_call(
        body, name=name, grid=(S // HEAD,),
        in_specs=[pl.BlockSpec((HEAD, GW), lambda c: (c, 1)),
                  pl.BlockSpec((HEAD, GW), lambda c: (c, 2)),
                  _vec_spec(l, GW), _vec_spec(l, GW),
                  pl.BlockSpec((None, NG, HEAD, HEAD), lambda c: (l, 0, 0, 0)),
                  pl.BlockSpec((None, NG, HEAD, HEAD), lambda c: (l, 0, 0, 0)),
                  _vec_spec(l, GW)],
        out_specs=pl.BlockSpec((HEAD, GW), lambda c: (c, 0)),
        out_shape=jax.ShapeDtypeStruct((S, GW), BF16),
        scratch_shapes=[pltpu.VMEM((HEAD, GW), F32)],
        compiler_params=_params(("arbitrary",)),
    )(z, z, lng3, lnb3, ws_b, bs_b, og3)


CONV_HALO = 8
CONV_ROWS = 64


def _for_row_windows(S, fn):
    R, W = CONV_ROWS, CONV_ROWS + 2 * CONV_HALO
    n = S // R
    assert n * R == S and n >= 2
    fn(0, 0, 0)
    if n > 2:
        def mid(k, carry):
            fn(pl.multiple_of(k * R - CONV_HALO, CONV_HALO), CONV_HALO, pl.multiple_of(k * R, R))
            return carry

        lax.fori_loop(1, n - 1, mid, 0)
    fn(S - W, 2 * CONV_HALO, S - R)


def _shifts(t):
    W = t.shape[0]
    row = lax.broadcasted_iota(jnp.int32, t.shape, 0)
    dn = jnp.where(row == 0, 0.0, pltpu.roll(t, 1, axis=0))
    up = jnp.where(row == W - 1, 0.0, pltpu.roll(t, W - 1, axis=0))
    return dn, up


def _conv3(t, w, b):
    dn, up = _shifts(t)
    return ((b + dn * w[0:1]) + t * w[1:2]) + up * w[2:3], dn, up


def _conv_glu_fwd(name, ap, cw, cb3, l, cfg):
    S, F = cfg["S"], cfg["F"]
    tc = _tile(F, 256, 128)
    nf = F // tc
    R, W = CONV_ROWS, CONV_ROWS + 2 * CONV_HALO

    def body(g_ref, u_ref, wg_ref, wu_ref, bg_ref, bu_ref, y_ref):
        wg, wu, bg, bu = wg_ref[...], wu_ref[...], bg_ref[...], bu_ref[...]

        def window(start, lo, out0):
            rows = pl.ds(start, W)
            ag, _, _ = _conv3(g_ref[rows, :], wg, bg)
            au, _, _ = _conv3(u_ref[rows, :], wu, bu)
            y = (ag * _sigmoid(ag)) * au
            y_ref[pl.ds(out0, R), :] = y[lo:lo + R].astype(BF16)

        _for_row_windows(S, window)

    return pl.pallas_call(
        body, name=name, grid=(nf,),
        in_specs=[pl.BlockSpec((S, tc), lambda j: (0, j)),
                  pl.BlockSpec((S, tc), lambda j: (0, j + nf)),
                  pl.BlockSpec((None, 3, tc), lambda j: (l, 0, j)),
                  pl.BlockSpec((None, 3, tc), lambda j: (l, 0, j + nf)),
                  pl.BlockSpec((None, 1, tc), lambda j: (l, 0, j)),
                  pl.BlockSpec((None, 1, tc), lambda j: (l, 0, j + nf))],
        out_specs=pl.BlockSpec((S, tc), lambda j: (0, j)),
        out_shape=jax.ShapeDtypeStruct((S, F), BF16),
        compiler_params=_params(("arbitrary",)),
    )(ap, ap, cw, cw, cb3, cb3)


def _loss_bwd(name, y, target):
    S, D = y.shape
    tr = _tile(S, 256, 16)

    def body(y_ref, t_ref, loss_ref, d_ref, db_ref):
        @pl.when(pl.program_id(0) == 0)
        def _():
            loss_ref[...] = jnp.zeros_like(loss_ref)

        err = y_ref[...] - t_ref[...]
        part = 0.5 * jnp.sum(jnp.mean(err * err, axis=-1, keepdims=True), axis=0, keepdims=True)
        loss_ref[...] += jnp.broadcast_to(part, loss_ref.shape)
        d = err * (1.0 / D)
        d_ref[...] = d
        db_ref[...] = d.astype(BF16)

    return pl.pallas_call(
        body, name=name, grid=(S // tr,),
        in_specs=[pl.BlockSpec((tr, D), lambda i: (i, 0)), pl.BlockSpec((tr, D), lambda i: (i, 0))],
        out_specs=[pl.BlockSpec((8, 128), lambda i: (0, 0)),
                   pl.BlockSpec((tr, D), lambda i: (i, 0)),
                   pl.BlockSpec((tr, D), lambda i: (i, 0))],
        out_shape=[jax.ShapeDtypeStruct((8, 128), F32),
                   jax.ShapeDtypeStruct((S, D), F32),
                   jax.ShapeDtypeStruct((S, D), BF16)],
        compiler_params=_params(("arbitrary",)),
    )(y, target)


def _rms_bwd(name, x, dh, dres, g3, l):
    S, D = x.shape
    tr = _tile(S, 256, 16)

    def body(x_ref, dh_ref, dr_ref, g_ref, dx_ref, dxb_ref, dg_ref):
        @pl.when(pl.program_id(0) == 0)
        def _():
            dg_ref[...] = jnp.zeros_like(dg_ref)

        xv = x_ref[...]
        dhv = dh_ref[...]
        r = lax.rsqrt(jnp.mean(xv * xv, axis=-1, keepdims=True) + EPS)
        xhat = xv * r
        dhg = dhv * g_ref[...]
        dx = dr_ref[...] + r * (dhg - xhat * jnp.mean(dhg * xhat, axis=-1, keepdims=True))
        dx_ref[...] = dx
        dxb_ref[...] = dx.astype(BF16)
        dg_ref[...] += jnp.sum(dhv * xhat, axis=0, keepdims=True)

    return pl.pallas_call(
        body, name=name, grid=(S // tr,),
        in_specs=[pl.BlockSpec((tr, D), lambda i: (i, 0)), pl.BlockSpec((tr, D), lambda i: (i, 0)),
                  pl.BlockSpec((tr, D), lambda i: (i, 0)), _vec_spec(l, D)],
        out_specs=[pl.BlockSpec((tr, D), lambda i: (i, 0)), pl.BlockSpec((tr, D), lambda i: (i, 0)),
                   pl.BlockSpec((1, D), lambda i: (0, 0))],
        out_shape=[jax.ShapeDtypeStruct((S, D), F32), jax.ShapeDtypeStruct((S, D), BF16),
                   jax.ShapeDtypeStruct((1, D), F32)],
        compiler_params=_params(("arbitrary",)),
    )(x, dh, dres, g3)


def _glu_conv_bwd(name, dy, ap, cw, cb3, l, cfg):
    S, F = cfg["S"], cfg["F"]
    tc = 128
    nf = F // tc
    R, W = CONV_ROWS, CONV_ROWS + 2 * CONV_HALO

    def body(dy_ref, g_ref, u_ref, wg_ref, wu_ref, bg_ref, bu_ref, dap_ref, dw_ref, db_ref, acc):
        acc[...] = jnp.zeros_like(acc)
        wg, wu, bg, bu = wg_ref[...], wu_ref[...], bg_ref[...], bu_ref[...]

        def window(start, lo, out0):
            rows = pl.ds(start, W)
            apg, apu, dyv = g_ref[rows, :], u_ref[rows, :], dy_ref[rows, :]
            ag, g_dn, g_up = _conv3(apg, wg, bg)
            au, u_dn, u_up = _conv3(apu, wu, bu)
            sig = _sigmoid(ag)
            da_u = dyv * (ag * sig)
            da_g = (dyv * au) * (sig * (1.0 + ag * (1.0 - sig)))

            def back(da, w):
                prv, nxt = _shifts(da)
                return ((nxt * w[0:1] + da * w[1:2]) + prv * w[2:3])[lo:lo + R].astype(BF16)

            dap_ref[0, pl.ds(out0, R), :] = back(da_g, wg)
            dap_ref[1, pl.ds(out0, R), :] = back(da_u, wu)

            def add(i, prod):
                acc[i] += jnp.sum(prod[lo:lo + R].reshape(R // 8, 8, tc), axis=0)

            for i, other in enumerate([g_dn, apg, g_up]):
                add(i, da_g * other)
            for i, other in enumerate([u_dn, apu, u_up]):
                add(3 + i, da_u * other)
            add(6, da_g)
            add(7, da_u)

        _for_row_windows(S, window)
        col = [jnp.sum(acc[i], axis=0, keepdims=True) for i in range(8)]
        dw_ref[0] = jnp.concatenate(col[0:3], axis=0)
        dw_ref[1] = jnp.concatenate(col[3:6], axis=0)
        db_ref[0] = col[6]
        db_ref[1] = col[7]

    return pl.pallas_call(
        body, name=name, grid=(nf,),
        in_specs=[pl.BlockSpec((S, tc), lambda j: (0, j)),
                  pl.BlockSpec((S, tc), lambda j: (0, j)),
                  pl.BlockSpec((S, tc), lambda j: (0, j + nf)),
                  pl.BlockSpec((None, 3, tc), lambda j: (l, 0, j)),
                  pl.BlockSpec((None, 3, tc), lambda j: (l, 0, j + nf)),
                  pl.BlockSpec((None, 1, tc), lambda j: (l, 0, j)),
                  pl.BlockSpec((None, 1, tc), lambda j: (l, 0, j + nf))],
        out_specs=[pl.BlockSpec((2, S, tc), lambda j: (0, 0, j)),
                   pl.BlockSpec((2, 3, tc), lambda j: (0, 0, j)),
                   pl.BlockSpec((2, 1, tc), lambda j: (0, 0, j))],
        out_shape=[jax.ShapeDtypeStruct((2, S, F), BF16),
                   jax.ShapeDtypeStruct((2, 3, F), F32),
                   jax.ShapeDtypeStruct((2, 1, F), F32)],
        scratch_shapes=[pltpu.VMEM((8, 8, tc), F32)],
        compiler_params=_params(("arbitrary",)),
    )(dy, ap, ap, cw, cw, cb3, cb3)


def _attn_bwd(name, q, k, v, attn, dmix, sink, ag3, l, cfg):
    S, AW, KVW, NQ, NKV, G = cfg["S"], cfg["AW"], cfg["KVW"], cfg["NQ"], cfg["NKV"], cfg["G"]
    nb = S // HEAD
    scale = HEAD ** -0.5

    def body(q_ref, kp, kc, kn, vp, vc, vn, a_ref, dm_ref, sink_ref, ag_ref,
             dq_ref, dk_ref, dv_ref, dsink_ref, dag_ref, da_scr):
        n = pl.program_id(0)

        @pl.when(n == 0)
        def _():
            dk_ref[...] = jnp.zeros_like(dk_ref)
            dv_ref[...] = jnp.zeros_like(dv_ref)
            dsink_ref[...] = jnp.zeros_like(dsink_ref)
            dag_ref[...] = jnp.zeros_like(dag_ref)

        a = a_ref[...]
        dm = dm_ref[...]
        r = lax.rsqrt(jnp.mean(a * a, axis=-1, keepdims=True) + EPS)
        xhat = a * r
        dmg = dm * ag_ref[...]
        da_scr[...] = r * (dmg - xhat * jnp.mean(dmg * xhat, axis=-1, keepdims=True))
        dag_ref[...] += jnp.sum(dm * xhat, axis=0, keepdims=True)

        kb = jnp.concatenate([kp[...], kc[...], kn[...]], axis=0)
        vb = jnp.concatenate([vp[...], vc[...], vn[...]], axis=0)
        band = pl.ds(pl.multiple_of(n * HEAD, HEAD), 3 * HEAD)
        for j in range(NKV):
            sl = slice(j * HEAD, (j + 1) * HEAD)
            heads = [slice((j * G + g) * HEAD, (j * G + g + 1) * HEAD) for g in range(G)]
            qs = jnp.concatenate([q_ref[:, hs] for hs in heads], axis=0)
            do = jnp.concatenate([da_scr[:, hs] for hs in heads], axis=0)
            kj, vj = kb[:, sl], vb[:, sl]
            probs, p_sink = _attn_probs(qs, kj, n, _sink_rows(sink_ref, l, j, G), S, G)
            dob = do.astype(BF16)
            dprobs = lax.dot_general(dob, vj, (((1,), (1,)), ((), ())), preferred_element_type=F32)
            delta = jnp.sum(dprobs * probs, axis=-1, keepdims=True)
            ds = (probs * (dprobs - delta)) * scale
            dsb = ds.astype(BF16)
            dsk = -(p_sink * delta)
            dq = jnp.dot(dsb, kj, preferred_element_type=F32)
            for g in range(G):
                dq_ref[:, heads[g]] = dq[g * HEAD:(g + 1) * HEAD]
                part = jnp.sum(dsk[g * HEAD:(g + 1) * HEAD], axis=0, keepdims=True)
                dsink_ref[j * G + g:j * G + g + 1, :] += jnp.broadcast_to(part, (1, HEAD))
            dk_ref[band, sl] += lax.dot_general(dsb, qs, (((0,), (0,)), ((), ())), preferred_element_type=F32)
            dv_ref[band, sl] += lax.dot_general(probs.astype(BF16), dob, (((0,), (0,)), ((), ())),
                                                preferred_element_type=F32)

    return pl.pallas_call(
        body, name=name, grid=(nb,),
        in_specs=[pl.BlockSpec((HEAD, AW), lambda n: (n, 0))] + _band_specs(KVW, nb) + _band_specs(KVW, nb)
                 + [pl.BlockSpec((HEAD, AW), lambda n: (n, 0)),
                    pl.BlockSpec((HEAD, AW), lambda n: (n, 0)),
                    pl.BlockSpec(memory_space=pltpu.SMEM), _vec_spec(l, AW)],
        out_specs=[pl.BlockSpec((HEAD, AW), lambda n: (n, 0)),
                   pl.BlockSpec((S + 2 * HEAD, KVW), lambda n: (0, 0)),
                   pl.BlockSpec((S + 2 * HEAD, KVW), lambda n: (0, 0)),
                   pl.BlockSpec((NQ, HEAD), lambda n: (0, 0)),
                   pl.BlockSpec((1, AW), lambda n: (0, 0))],
        out_shape=[jax.ShapeDtypeStruct((S, AW), F32),
                   jax.ShapeDtypeStruct((S + 2 * HEAD, KVW), F32),
                   jax.ShapeDtypeStruct((S + 2 * HEAD, KVW), F32),
                   jax.ShapeDtypeStruct((NQ, HEAD), F32),
                   jax.ShapeDtypeStruct((1, AW), F32)],
        scratch_shapes=[pltpu.VMEM((HEAD, AW), F32)],
        compiler_params=_params(("arbitrary",)),
    )(q, k, k, k, v, v, v, attn, dmix, sink, ag3)


def _qkv_prep_bwd(name, z, dq, dk_pad, dv_pad, qg3, kg3, cos2, sin2, l, cfg):
    S, AW, KVW, NQ, NKV = cfg["S"], cfg["AW"], cfg["KVW"], cfg["NQ"], cfg["NKV"]
    kv_blk = (3 * AW) // (2 * KVW)

    def body(zq_ref, zkv_ref, dq_ref, dk_ref, dv_ref, qg_ref, kg_ref, c_ref, s_ref,
             dzq_ref, dzkv_ref, dqg_ref, dkg_ref):
        @pl.when(pl.program_id(0) == 0)
        def _():
            dqg_ref[...] = jnp.zeros_like(dqg_ref)
            dkg_ref[...] = jnp.zeros_like(dkg_ref)

        cosv, sinv = c_ref[...], s_ref[...]

        def back(t, dr, g):
            r = lax.rsqrt(jnp.mean(t * t, axis=-1, keepdims=True) + EPS)
            xhat = t * r
            dn = dr * cosv + pltpu.roll(dr * sinv, HEAD // 2, axis=1)
            dxh = dn * g
            dt = r * (dxh - xhat * jnp.mean(dxh * xhat, axis=-1, keepdims=True))
            return dt, jnp.sum(dn * xhat, axis=0, keepdims=True)

        gq = jnp.zeros((1, HEAD), F32)
        for h in range(NQ):
            sl = slice(h * HEAD, (h + 1) * HEAD)
            dt, gpart = back(zq_ref[:, sl], dq_ref[:, sl], qg_ref[...])
            dzq_ref[:, sl] = dt.astype(BF16)
            gq = gq + gpart
        dqg_ref[...] += gq
        gk = jnp.zeros((1, HEAD), F32)
        for h in range(NKV):
            sl = slice(h * HEAD, (h + 1) * HEAD)
            dt, gpart = back(zkv_ref[:, sl], dk_ref[:, sl], kg_ref[...])
            dzkv_ref[:, sl] = dt.astype(BF16)
            gk = gk + gpart
        dkg_ref[...] += gk
        dzkv_ref[:, KVW:] = dv_ref[...].astype(BF16)

    return pl.pallas_call(
        body, name=name, grid=(S // HEAD,),
        in_specs=[pl.BlockSpec((HEAD, AW), lambda i: (i, 0)),
                  pl.BlockSpec((HEAD, 2 * KVW), lambda i: (i, kv_blk)),
                  pl.BlockSpec((HEAD, AW), lambda i: (i, 0)),
                  pl.BlockSpec((HEAD, KVW), lambda i: (i + 1, 0)),
                  pl.BlockSpec((HEAD, KVW), lambda i: (i + 1, 0)),
                  _vec_spec(l, HEAD), _vec_spec(l, HEAD),
                  pl.BlockSpec((HEAD, HEAD), lambda i: (i, 0)),
                  pl.BlockSpec((HEAD, HEAD), lambda i: (i, 0))],
        out_specs=[pl.BlockSpec((HEAD, AW), lambda i: (i, 0)),
                   pl.BlockSpec((HEAD, 2 * KVW), lambda i: (i, 0)),
                   pl.BlockSpec((1, HEAD), lambda i: (0, 0)),
                   pl.BlockSpec((1, HEAD), lambda i: (0, 0))],
        out_shape=[jax.ShapeDtypeStruct((S, AW), BF16),
                   jax.ShapeDtypeStruct((S, 2 * KVW), BF16),
                   jax.ShapeDtypeStruct((1, HEAD), F32),
                   jax.ShapeDtypeStruct((1, HEAD), F32)],
        compiler_params=_params(("arbitrary",)),
    )(z, z, dq, dk_pad, dv_pad, qg3, kg3, cos2, sin2)


def _sgu_bwd(name, z, dmix, lng3, lnb3, ws_b, wst_b, bs_b, og3, l, cfg):
    S, GW, NG = cfg["S"], cfg["GW"], cfg["NG"]

    def body(gu_ref, gv_ref, dm_ref, lng_ref, lnb_ref, ws_ref, wst_ref, bs_ref, og_ref,
             dgu_ref, dgv_ref, dws_ref, dbs_ref, dlng_ref, dlnb_ref, dog_ref, sg_scr, f_scr, dvn_scr):
        @pl.when(pl.program_id(0) == 0)
        def _():
            dws_ref[...] = jnp.zeros_like(dws_ref)
            dbs_ref[...] = jnp.zeros_like(dbs_ref)
            dlng_ref[...] = jnp.zeros_like(dlng_ref)
            dlnb_ref[...] = jnp.zeros_like(dlnb_ref)
            dog_ref[...] = jnp.zeros_like(dog_ref)

        gu, gv = gu_ref[...], gv_ref[...]
        lng = lng_ref[...]
        u, xhat, rstd, vn = _sgu_forward_math(gu, gv, lng, lnb_ref[...])
        vnb = vn.astype(BF16)
        for h in range(NG):
            sl = slice(h * HEAD, (h + 1) * HEAD)
            f = jnp.dot(ws_ref[h], vnb[:, sl], preferred_element_type=F32) + bs_ref[h]
            f_scr[:, sl] = f
            sg_scr[:, sl] = u[:, sl] * f
        sg = sg_scr[...]
        dm = dm_ref[...]
        r = lax.rsqrt(jnp.mean(sg * sg, axis=-1, keepdims=True) + EPS)
        sghat = sg * r
        dmg = dm * og_ref[...]
        dsg = r * (dmg - sghat * jnp.mean(dmg * sghat, axis=-1, keepdims=True))
        dog_ref[...] += jnp.sum(dm * sghat, axis=0, keepdims=True)
        du = dsg * f_scr[...]
        df = dsg * u
        dfb = df.astype(BF16)
        for h in range(NG):
            sl = slice(h * HEAD, (h + 1) * HEAD)
            dvn_scr[:, sl] = jnp.dot(wst_ref[h], dfb[:, sl], preferred_element_type=F32)
            dws_ref[h] += lax.dot_general(dfb[:, sl], vnb[:, sl], (((1,), (1,)), ((), ())),
                                          preferred_element_type=F32)
            dbs_ref[h] += jnp.broadcast_to(jnp.sum(df[:, sl], axis=-1, keepdims=True), (HEAD, HEAD))
        dvn = dvn_scr[...]
        dlng_ref[...] += jnp.sum(dvn * xhat, axis=0, keepdims=True)
        dlnb_ref[...] += jnp.sum(dvn, axis=0, keepdims=True)
        dxh = dvn * lng
        dvv = rstd * ((dxh - jnp.mean(dxh, axis=-1, keepdims=True))
                      - xhat * jnp.mean(dxh * xhat, axis=-1, keepdims=True))
        dgu_ref[...] = (du * _gelu_grad(gu)).astype(BF16)
        dgv_ref[...] = (dvv * _gelu_grad(gv)).astype(BF16)

    vec = pl.BlockSpec((1, GW), lambda c: (0, 0))
    mat = pl.BlockSpec((NG, HEAD, HEAD), lambda c: (0, 0, 0))
    wsp = pl.BlockSpec((None, NG, HEAD, HEAD), lambda c: (l, 0, 0, 0))
    return pl.pallas_call(
        body, name=name, grid=(S // HEAD,),
        in_specs=[pl.BlockSpec((HEAD, GW), lambda c: (c, 1)),
                  pl.BlockSpec((HEAD, GW), lambda c: (c, 2)),
                  pl.BlockSpec((HEAD, GW), lambda c: (c, 1)),
                  _vec_spec(l, GW), _vec_spec(l, GW), wsp, wsp, wsp, _vec_spec(l, GW)],
        out_specs=[pl.BlockSpec((HEAD, GW), lambda c: (c, 0)), pl.BlockSpec((HEAD, GW), lambda c: (c, 0)),
                   mat, mat, vec, vec, vec],
        out_shape=[jax.ShapeDtypeStruct((S, GW), BF16), jax.ShapeDtypeStruct((S, GW), BF16),
                   jax.ShapeDtypeStruct((NG, HEAD, HEAD), F32), jax.ShapeDtypeStruct((NG, HEAD, HEAD), F32),
                   jax.ShapeDtypeStruct((1, GW), F32), jax.ShapeDtypeStruct((1, GW), F32),
                   jax.ShapeDtypeStruct((1, GW), F32)],
        scratch_shapes=[pltpu.VMEM((HEAD, GW), F32), pltpu.VMEM((HEAD, GW), F32), pltpu.VMEM((HEAD, GW), F32)],
        compiler_params=_params(("arbitrary",)),
    )(z, z, dmix, lng3, lnb3, ws_b, wst_b, bs_b, og3)


def _mesh_pos():
    x, y, c = lax.axis_index("x"), lax.axis_index("y"), lax.axis_index("c")
    return x, y, c


def _dev_index(p):
    return 4 * p[0] + 2 * p[1] + p[2]


def _handshake(peers):
    barrier = pltpu.get_barrier_semaphore()
    for p in peers:
        pl.semaphore_signal(barrier, inc=1, device_id=p, device_id_type=MESH)
    pl.semaphore_wait(barrier, len(peers))


GATHER_COPIES = 10


def _gather_body(full, half, in_half):
    def body(in_ref, out_ref, send_sems, recv_sems, local_sem):
        x, y, c = _mesh_pos()
        me, sib = (x, y, c), (x, y, 1 - c)
        xn, yn, dg = (1 - x, y, c), (x, 1 - y, c), (1 - x, 1 - y, c)
        _handshake([sib, xn, yn])

        def copy(k, dst, to, src=None):
            return pltpu.make_async_remote_copy(
                src_ref=dst if src is None else src, dst_ref=dst,
                send_sem=send_sems.at[k], recv_sem=recv_sems.at[k], device_id=to, device_id_type=MESH)

        def win(dev, h=None):
            idx = _dev_index(dev)
            return full(out_ref, idx) if h is None else half(out_ref, idx, h)

        local = pltpu.make_async_copy(in_ref, win(me), local_sem)
        local.start()
        sends = [copy(0, win(me), sib, src=in_ref),
                 copy(1, win(me, 0), xn, src=in_half(in_ref, 0)),
                 copy(2, win(me, 1), yn, src=in_half(in_ref, 1)),
                 copy(4, win(me, 0), yn, src=in_half(in_ref, 0)),
                 copy(6, win(me, 1), xn, src=in_half(in_ref, 1))]
        for cp in sends:
            cp.start()
        copy(1, win(xn, 0), me).wait_recv()
        sends.append(copy(3, win(xn, 0), yn))
        sends[-1].start()
        copy(2, win(yn, 1), me).wait_recv()
        sends.append(copy(5, win(yn, 1), xn))
        sends[-1].start()
        copy(6, win(xn, 1), me).wait_recv()
        sends.append(copy(7, win(xn), sib))
        sends[-1].start()
        copy(4, win(yn, 0), me).wait_recv()
        sends.append(copy(8, win(yn), sib))
        sends[-1].start()
        copy(3, win(dg, 0), me).wait_recv()
        copy(5, win(dg, 1), me).wait_recv()
        sends.append(copy(9, win(dg), sib))
        sends[-1].start()
        sib_xn, sib_yn, sib_dg = (1 - x, y, 1 - c), (x, 1 - y, 1 - c), (1 - x, 1 - y, 1 - c)
        for k, dev in [(0, sib), (7, sib_xn), (8, sib_yn), (9, sib_dg)]:
            copy(k, win(dev), me).wait_recv()
        for cp in sends:
            cp.wait_send()
        local.wait()

    return body


def _all_gather(name, cid, shard, out_shape, windows):
    return pl.kernel(
        _gather_body(*windows), out_type=out_shape,
        mesh=plsc.ScalarSubcoreMesh(axis_name="seq", num_cores=1), name=name,
        scratch_types=[pltpu.SemaphoreType.DMA((GATHER_COPIES,)), pltpu.SemaphoreType.DMA((GATHER_COPIES,)),
                       pltpu.SemaphoreType.DMA],
        compiler_params=pltpu.CompilerParams(collective_id=cid),
    )(shard)


def _d2d_body(nb, ns):
    n = nb + ns

    def body(*refs):
        ins, outs = refs[:n], refs[n:2 * n]
        send_sems, recv_sems = refs[2 * n:]
        x, y, c = _mesh_pos()
        sib = (x, y, 1 - c)
        _handshake([sib])
        copies = []
        for t in range(n):
            cp = pltpu.make_async_remote_copy(
                src_ref=ins[t].at[:, 1 - c] if t < nb else ins[t], dst_ref=outs[t],
                send_sem=send_sems.at[t], recv_sem=recv_sems.at[t],
                device_id=sib, device_id_type=MESH)
            cp.start()
            copies.append(cp)
        for cp in copies:
            cp.wait()

    return body


def _rs_d2d(name, cid, bigs, smalls):
    shapes = [jax.ShapeDtypeStruct((g.shape[0],) + g.shape[2:], g.dtype) for g in bigs]
    shapes += [jax.ShapeDtypeStruct(s.shape, s.dtype) for s in smalls]
    n = len(shapes)
    return pl.kernel(
        _d2d_body(len(bigs), len(smalls)), out_type=shapes,
        mesh=plsc.ScalarSubcoreMesh(axis_name="seq", num_cores=1), name=name,
        scratch_types=[pltpu.SemaphoreType.DMA((n,)), pltpu.SemaphoreType.DMA((n,))],
        compiler_params=pltpu.CompilerParams(collective_id=cid),
    )(*bigs, *smalls)


def _pair_sum(name, g4, recv, cvec, row_mult):
    Q, _, R, C = g4.shape
    tr = _tile(R, 256, row_mult)

    def body(c_ref, a_ref, b_ref, o_ref):
        o_ref[...] = (a_ref[...].astype(F32) + b_ref[...].astype(F32)).astype(BF16)

    grid_spec = pltpu.PrefetchScalarGridSpec(
        num_scalar_prefetch=1, grid=(Q, R // tr),
        in_specs=[pl.BlockSpec((None, None, tr, C), lambda q, i, c_ref: (q, c_ref[0], i, 0)),
                  pl.BlockSpec((None, tr, C), lambda q, i, c_ref: (q, i, 0))],
        out_specs=pl.BlockSpec((None, tr, C), lambda q, i, c_ref: (q, i, 0)))
    return pl.pallas_call(
        body, name=name, grid_spec=grid_spec, out_shape=jax.ShapeDtypeStruct((Q, R, C), BF16),
        compiler_params=_params(("arbitrary", "arbitrary")),
    )(cvec, g4, recv)


def _add2(name, a, b):
    R, C = a.shape
    tr = _tile(R, 512, 8)

    def body(a_ref, b_ref, o_ref):
        o_ref[...] = a_ref[...] + b_ref[...]

    blk = pl.BlockSpec((tr, C), lambda i: (i, 0))
    return pl.pallas_call(body, name=name, grid=(R // tr,), in_specs=[blk, blk], out_specs=blk,
                          out_shape=jax.ShapeDtypeStruct((R, C), a.dtype),
                          compiler_params=_params(("arbitrary",)))(a, b)


def _ici_copies(nb, n, srcs, lands, send_sems, recv_sems):
    x, y, c = _mesh_pos()
    q_me = 2 * x + y
    copies = []
    for t in range(n):
        for k in range(1, 4):
            px, py = x ^ (k >> 1), y ^ (k & 1)
            copies.append(pltpu.make_async_remote_copy(
                src_ref=srcs[t].at[2 * px + py] if t < nb else srcs[t], dst_ref=lands[t].at[q_me],
                send_sem=send_sems.at[3 * t + k - 1], recv_sem=recv_sems.at[3 * t + k - 1],
                device_id=(px, py, c), device_id_type=MESH))
    return copies


_HBM = pl.BlockSpec(memory_space=pltpu.HBM)
_SEM = pl.BlockSpec(memory_space=pltpu.SEMAPHORE)
_DATAFLOW = pltpu.SideEffectType.DATAFLOW_SIDE_EFFECTING


def _ici_start(name, bigs, smalls):
    nb, n = len(bigs), len(bigs) + len(smalls)
    srcs = list(bigs) + list(smalls)
    lands = [lax.empty(g.shape, g.dtype) for g in bigs] + [lax.empty((N_DEV // 2,) + s.shape, s.dtype) for s in smalls]

    def body(*refs):
        src_refs, land_refs = refs[:n], refs[n:2 * n]
        send_sems, recv_sems = refs[2 * n], refs[2 * n + 1]
        token = refs[-1]
        for cp in _ici_copies(nb, n, src_refs, land_refs, send_sems, recv_sems):
            cp.start()
        token[...] = jnp.zeros_like(token)

    hbm = [pltpu.HBM(a.shape, a.dtype) for a in srcs + lands]
    args = [pltpu.with_memory_space_constraint(a, pltpu.HBM) for a in srcs + lands]
    out = pl.pallas_call(
        body, name=name,
        out_shape=[pltpu.SemaphoreType.DMA((3 * n,)), pltpu.SemaphoreType.DMA((3 * n,))] + hbm
                  + [jax.ShapeDtypeStruct((8, 128), F32)],
        in_specs=[_HBM] * (2 * n),
        out_specs=[_SEM, _SEM] + [_HBM] * (2 * n) + [pl.BlockSpec(memory_space=pltpu.VMEM)],
        input_output_aliases={i: 2 + i for i in range(2 * n)},
        compiler_params=pltpu.CompilerParams(has_side_effects=_DATAFLOW),
    )(*args)
    return out[0], out[1], list(out[2:2 + n]), list(out[2 + n:2 + 2 * n]), out[-1]


def _ici_wait(name, started, nb, not_before):
    send_sems, recv_sems, srcs, lands, token = started
    n = len(srcs)

    def body(*refs):
        src_refs, land_refs = refs[:n], refs[n:2 * n]
        send_refs, recv_refs = refs[2 * n], refs[2 * n + 1]
        for cp in _ici_copies(nb, n, src_refs, land_refs, send_refs, recv_refs):
            cp.wait_send()
            cp.wait_recv()

    hbm = [pltpu.HBM(a.shape, a.dtype) for a in srcs + lands]
    out = pl.pallas_call(
        body, name=name, out_shape=hbm,
        in_specs=[_HBM] * (2 * n) + [_SEM, _SEM] + [pl.BlockSpec(memory_space=pl.ANY)] * 2,
        out_specs=[_HBM] * (2 * n),
        input_output_aliases={i: i for i in range(2 * n)},
        compiler_params=pltpu.CompilerParams(has_side_effects=_DATAFLOW),
    )(*srcs, *lands, send_sems, recv_sems, not_before, token)
    return list(out[:n]), list(out[n:])


def _adamw_math(w, g, m, v):
    m2 = ADAM_B1 * m + (1.0 - ADAM_B1) * g
    v2 = ADAM_B2 * v + (1.0 - ADAM_B2) * (g * g)
    m_hat = m2 / (1.0 - ADAM_B1 ** ADAM_STEP)
    v_hat = v2 / (1.0 - ADAM_B2 ** ADAM_STEP)
    delta = -ADAM_LR * (m_hat / (jnp.sqrt(v_hat) + ADAM_EPS) + ADAM_WD * w)
    return delta, m2, v2


def _chip_terms(q, own, land_ref, n):
    return [jnp.where(q == s, own, land_ref[s]).astype(F32) for s in range(n)]


def _adamw_sum(name, land, own, qvec, w, m, v, l, prev, row_mult):
    NS, R, C = land.shape
    L = w.shape[0]
    tr = _tile(R, 128, row_mult)

    def body(*refs):
        q_ref, r_ref, o_ref, w_ref, m_ref, v_ref = refs[:6]
        g_ref, d_ref, nm_ref, nv_ref = refs[-4:]
        terms = _chip_terms(q_ref[0], o_ref[...], r_ref, NS)
        g = terms[0]
        for t in terms[1:]:
            g = g + t
        d, m2, v2 = _adamw_math(w_ref[...], g, m_ref[...], v_ref[...])
        g_ref[...] = g
        d_ref[...] = d
        nm_ref[...] = m2
        nv_ref[...] = v2

    blk = pl.BlockSpec((None, tr, C), lambda i, q: (l, i, 0))
    shp = jax.ShapeDtypeStruct((L, R, C), F32)
    in_specs = [pl.BlockSpec((NS, tr, C), lambda i, q: (0, i, 0)),
                pl.BlockSpec((None, tr, C), lambda i, q: (q[0], i, 0)), blk, blk, blk]
    args = [land, own, w, m, v]
    aliases = {}
    if prev is not None:
        in_specs += [pl.BlockSpec(memory_space=pl.ANY)] * 4
        args += list(prev)
        aliases = {6 + i: i for i in range(4)}
    grid_spec = pltpu.PrefetchScalarGridSpec(num_scalar_prefetch=1, grid=(R // tr,), in_specs=in_specs,
                                             out_specs=[blk, blk, blk, blk])
    return pl.pallas_call(
        body, name=name, grid_spec=grid_spec, out_shape=[shp, shp, shp, shp],
        input_output_aliases=aliases,
        compiler_params=_params(("arbitrary",)),
    )(qvec, *args)


def _adamw_plain(name, g, w, m, v):
    def body(g_ref, w_ref, m_ref, v_ref, d_ref, nm_ref, nv_ref):
        d, m2, v2 = _adamw_math(w_ref[...], g_ref[...], m_ref[...], v_ref[...])
        d_ref[...] = d
        nm_ref[...] = m2
        nv_ref[...] = v2

    shp = jax.ShapeDtypeStruct(g.shape, F32)
    return pl.pallas_call(body, name=name, out_shape=[shp, shp, shp], compiler_params=_params())(g, w, m, v)


SMALL = ["norm1_g", "q_norm_g", "k_norm_g", "sink", "sgu_ln_g", "sgu_ln_b", "w_s", "b_s",
         "attn_out_g", "sgu_out_g", "norm2_g", "conv_b"]
PACK_ALIGN = 1024


def _pack(pieces):
    flat = []
    for p in pieces:
        f = p.reshape(-1).astype(F32)
        pad = (-f.shape[0]) % PACK_ALIGN
        flat.append(jnp.pad(f, (0, pad)) if pad else f)
    return jnp.concatenate(flat).reshape(-1, 128)


def _unpack(packed, shapes):
    flat = packed.reshape(-1)
    out, off = [], 0
    for shp in shapes:
        n = 1
        for d in shp:
            n *= d
        out.append(flat[off:off + n].reshape(shp))
        off += n + ((-n) % PACK_ALIGN)
    return out


def kernel(x, norm1_g, w_in, q_norm_g, k_norm_g, sink, sgu_ln_g, sgu_ln_b, w_s, b_s, attn_out_g, sgu_out_g, w_o, norm2_g, w_up, conv_w, conv_b, w_down, loss_target, m_norm1_g, m_w_in, m_q_norm_g, m_k_norm_g, m_sink, m_sgu_ln_g, m_sgu_ln_b, m_w_s, m_b_s, m_attn_out_g, m_sgu_out_g, m_w_o, m_norm2_g, m_w_up, m_conv_w, m_conv_b, m_w_down, v_norm1_g, v_w_in, v_q_norm_g, v_k_norm_g, v_sink, v_sgu_ln_g, v_sgu_ln_b, v_w_s, v_b_s, v_attn_out_g, v_sgu_out_g, v_w_o, v_norm2_g, v_w_up, v_conv_w, v_conv_b, v_w_down):
    weights = dict(norm1_g=norm1_g, w_in=w_in, q_norm_g=q_norm_g, k_norm_g=k_norm_g, sink=sink, sgu_ln_g=sgu_ln_g,
                   sgu_ln_b=sgu_ln_b, w_s=w_s, b_s=b_s, attn_out_g=attn_out_g, sgu_out_g=sgu_out_g, w_o=w_o,
                   norm2_g=norm2_g, w_up=w_up, conv_w=conv_w, conv_b=conv_b, w_down=w_down)
    mom_m = dict(norm1_g=m_norm1_g, w_in=m_w_in, q_norm_g=m_q_norm_g, k_norm_g=m_k_norm_g, sink=m_sink,
                 sgu_ln_g=m_sgu_ln_g, sgu_ln_b=m_sgu_ln_b, w_s=m_w_s, b_s=m_b_s, attn_out_g=m_attn_out_g,
                 sgu_out_g=m_sgu_out_g, w_o=m_w_o, norm2_g=m_norm2_g, w_up=m_w_up, conv_w=m_conv_w,
                 conv_b=m_conv_b, w_down=m_w_down)
    mom_v = dict(norm1_g=v_norm1_g, w_in=v_w_in, q_norm_g=v_q_norm_g, k_norm_g=v_k_norm_g, sink=v_sink,
                 sgu_ln_g=v_sgu_ln_g, sgu_ln_b=v_sgu_ln_b, w_s=v_w_s, b_s=v_b_s, attn_out_g=v_attn_out_g,
                 sgu_out_g=v_sgu_out_g, w_o=v_w_o, norm2_g=v_norm2_g, w_up=v_w_up, conv_w=v_conv_w,
                 conv_b=v_conv_b, w_down=v_w_down)
    order = ["norm1_g", "w_in", "q_norm_g", "k_norm_g", "sink", "sgu_ln_g", "sgu_ln_b", "w_s", "b_s",
             "attn_out_g", "sgu_out_g", "w_o", "norm2_g", "w_up", "conv_w", "conv_b", "w_down"]

    _, S, D = x.shape
    L = w_in.shape[0]
    AW = D // 2
    NQ = AW // HEAD
    NKV = max(1, NQ // 4)
    G = NQ // NKV
    KVW = NKV * HEAD
    GW = D - AW
    NG = GW // HEAD
    IN = AW + 2 * KVW + 2 * GW
    INS = w_in.shape[2]
    OS = w_o.shape[1]
    US = w_up.shape[2]
    DS = w_down.shape[1]
    F2 = US * N_DEV
    F = F2 // 2
    assert INS * N_DEV == IN and OS * N_DEV == D and DS * N_DEV == F and AW == GW and (3 * AW) % (2 * KVW) == 0
    cfg = dict(S=S, D=D, AW=AW, NQ=NQ, NKV=NKV, G=G, KVW=KVW, GW=GW, NG=NG, F=F, F2=F2)

    def row_windows(n):
        m = n // 2
        return (lambda ref, idx: ref.at[:, pl.ds(pl.multiple_of(idx * n, n), n), :],
                lambda ref, idx, h: ref.at[:, pl.ds(pl.multiple_of(idx * n + h * m, m), m), :],
                lambda ref, h: ref.at[:, pl.ds(h * m, m), :])

    def col_windows(n, rows):
        m = rows // 2
        return (lambda ref, idx: ref.at[:, :, pl.ds(pl.multiple_of(idx * n, n), n)],
                lambda ref, idx, h: ref.at[:, pl.ds(h * m, m), pl.ds(pl.multiple_of(idx * n, n), n)],
                lambda ref, h: ref.at[:, pl.ds(h * m, m), :])

    def lead_col_windows(n, lead):
        m = lead // 2
        return (lambda ref, idx: ref.at[:, :, pl.ds(pl.multiple_of(idx * n, n), n)],
                lambda ref, idx, h: ref.at[pl.ds(h * m, m), :, pl.ds(pl.multiple_of(idx * n, n), n)],
                lambda ref, h: ref.at[pl.ds(h * m, m)])

    def after(first, then):
        return lax.optimization_barrier((first, then))

    w_in_t, m_w_in_t, v_w_in_t = (jnp.swapaxes(a, 1, 2) for a in (w_in, m_w_in, v_w_in))

    ids = iter(range(64))
    w_in_f, w_o_f, w_up_f, w_down_f = [], [], [], []

    def gather_layer(l, not_before=None):
        in_shard = w_in_t[l:l + 1].astype(BF16)
        if not_before is not None:
            _, in_shard = after(not_before, in_shard)
        w_in_f.append(_all_gather("ag_in", next(ids), in_shard,
                                  jax.ShapeDtypeStruct((1, IN, D), BF16), row_windows(INS)))
        w_o_f.append(_all_gather("ag_o", next(ids), w_o[l:l + 1].astype(BF16),
                                 jax.ShapeDtypeStruct((1, D, D), BF16), row_windows(OS)))
        w_up_f.append(_all_gather("ag_up", next(ids), w_up[l:l + 1].astype(BF16),
                                  jax.ShapeDtypeStruct((1, D, F2), BF16), col_windows(US, D)))
        w_down_f.append(_all_gather("ag_down", next(ids), w_down[l:l + 1].astype(BF16),
                                    jax.ShapeDtypeStruct((1, F, D), BF16), row_windows(DS)))

    assert L % 2 == 0 and INS % 32 == 0 and OS % 32 == 0 and DS % 32 == 0 and D % 32 == 0
    gather_layer(0)
    conv_w_f = _all_gather("ag_conv_w", next(ids), conv_w, jax.ShapeDtypeStruct((L, 3, F2), F32),
                           lead_col_windows(US, L))

    n1g3, n2g3 = norm1_g.reshape(L, 1, D), norm2_g.reshape(L, 1, D)
    qg3, kg3 = q_norm_g.reshape(L, 1, HEAD), k_norm_g.reshape(L, 1, HEAD)
    lng3, lnb3 = sgu_ln_g.reshape(L, 1, GW), sgu_ln_b.reshape(L, 1, GW)
    ag3, og3 = attn_out_g.reshape(L, 1, AW), sgu_out_g.reshape(L, 1, GW)
    cb3 = conv_b.reshape(L, 1, F2)
    ws_b = w_s.astype(BF16)
    wst_b = jnp.swapaxes(w_s, 2, 3).astype(BF16)
    bs_b = jnp.broadcast_to(b_s[..., None], (L, NG, HEAD, HEAD))
    inv_freq = ROPE_THETA ** (-jnp.arange(0, HEAD, 2, dtype=F32) / HEAD)
    ang = jnp.arange(S, dtype=F32)[:, None] * inv_freq[None, :]
    cos2 = jnp.concatenate([jnp.cos(ang), jnp.cos(ang)], axis=1)
    sin2 = jnp.concatenate([-jnp.sin(ang), jnp.sin(ang)], axis=1)

    tn = 512
    t_in, t_d, t_f, t_f2 = _tile(IN, tn, 128), _tile(D, tn, 128), _tile(F, tn, 128), _tile(F2, tn, 128)
    tm_f = _tile(F, 512, 128)
    tm_s = _tile(S, 1024, 128)
    assert AW % t_in == 0 and (2 * KVW) % t_in == 0
    n_q, n_kv, n_g = AW // t_in, (2 * KVW) // t_in, (2 * GW) // t_in

    def in_tile(j):
        return jnp.where(j < n_q, j, jnp.where(j < n_q + n_g, j + n_kv, j - n_g))

    def w_spec(tk, tn_):
        return pl.BlockSpec((None, tk, tn_), lambda i, j, k: (0, k, j))

    def wt_spec(tn_, tk):
        return pl.BlockSpec((None, tn_, tk), lambda i, j, k: (0, j, k))

    def a_spec(tm, tk):
        return pl.BlockSpec((tm, tk), lambda i, j, k: (i, k))

    def at_spec(tk, tm):
        return pl.BlockSpec((tk, tm), lambda i, j, k: (k, i))

    def b_spec(tk, tn_):
        return pl.BlockSpec((tk, tn_), lambda i, j, k: (k, j))

    xs = x.reshape(S, D)
    saved = []
    cur = xs
    for l in range(L):
        h = _rms_fwd("rms1_fwd", cur, n1g3, l)
        z = _mm("mm_in", h, w_in_f[l], M=S, N=IN, K=D, tm=S, tn=t_in, tk=D, tb=True,
                a_spec=a_spec(S, D),
                b_spec=pl.BlockSpec((None, t_in, D), lambda i, j, k: (0, in_tile(j), 0)))
        q_r, k_r, v_b = _qkv_prep("qkv_prep", z, qg3, kg3, cos2, sin2, l, cfg)
        attn, mix_l = _attn_fwd("attn_fwd", q_r, k_r, v_b, sink, ag3, l, cfg)
        mix_r = _sgu_fwd("sgu_fwd", z, lng3, lnb3, ws_b, bs_b, og3, l, cfg)
        mixed = jnp.concatenate([mix_l, mix_r], axis=1)
        x1 = _mm("mm_o", mixed, w_o_f[l], M=S, N=D, K=D, tm=S, tn=t_d, tk=D,
                 a_spec=a_spec(S, D), b_spec=w_spec(D, t_d), res=cur)
        if l + 1 < L:
            gather_layer(l + 1, not_before=x1)
        h2 = _rms_fwd("rms2_fwd", x1, n2g3, l)
        ap = _mm("mm_up", h2, w_up_f[l], M=S, N=F2, K=D, tm=S, tn=t_f2, tk=D,
                 a_spec=a_spec(S, D), b_spec=w_spec(D, t_f2))
        y_b = _conv_glu_fwd("conv_glu_fwd", ap, conv_w_f, cb3, l, cfg)
        x2 = _mm("mm_down", y_b, w_down_f[l], M=S, N=D, K=F, tm=tm_s, tn=t_d, tk=F,
                 a_spec=a_spec(tm_s, F), b_spec=w_spec(F, t_d), res=x1)
        saved.append(dict(x=cur, h=h, z=z, q=q_r, k=k_r, v=v_b, attn=attn, mixed=mixed, x1=x1, h2=h2, ap=ap, y=y_b))
        cur = x2

    loss_tile, dx, dxb = _loss_bwd("loss", cur, loss_target.reshape(S, D))
    loss = lax.psum(loss_tile[0, 0], ("x", "y", "c"))

    gS = [dict() for _ in range(L)]
    recv = [dict() for _ in range(L)]
    cvec = jnp.reshape(lax.axis_index("c"), (1,)).astype(jnp.int32)

    def rs_swap(name, g, R, C, small=None):
        g4 = g.reshape(N_DEV // 2, 2, R, C)
        got = _rs_d2d("rs_d2d_" + name, next(ids), [g4], [] if small is None else [small])
        return name, g4, got, small

    started = []

    def rs_finish(l, pending, then=None):
        name, g4, got, small = pending
        chip = _pair_sum("pair_sum_" + name, g4, got[0], cvec, 16)
        chip_small = [] if small is None else [_add2("pair_sum_small", small, got[1])]
        begun = _ici_start("rs_ici_" + name, [chip], chip_small)
        if then is not None:
            token, then = after(begun[4], then)
            begun = begun[:4] + (token,)
        started.append((l, name, begun))
        return then

    carried = None
    for l in reversed(range(L)):
        sv = saved[l]
        dy = _mm("mm_dy", dxb, w_down_f[l], M=S, N=F, K=D, tm=S, tn=t_f, tk=D, tb=True,
                 a_spec=a_spec(S, D), b_spec=wt_spec(t_f, D))
        if carried is not None:
            dy = rs_finish(l + 1, carried, dy)
        g_down = _mm("mm_gdown", sv["y"], dxb, M=F, N=D, K=S, tm=tm_f, tn=D, tk=S, ta=True,
                     a_spec=at_spec(S, tm_f), b_spec=b_spec(S, D), out_dtype=BF16)
        g_down, dy = after(g_down, dy)
        swap_down = rs_swap("w_down", g_down, DS, D)
        dap3, dcw, dcb = _glu_conv_bwd("glu_conv_bwd", dy, sv["ap"], conv_w_f, cb3, l, cfg)
        gS[l]["conv_w"] = jnp.concatenate([dcw[0], dcw[1]], axis=1)
        gS[l]["conv_b"] = jnp.concatenate([dcb[0], dcb[1]], axis=1).reshape(F2)
        dap3 = rs_finish(l, swap_down, dap3)
        dh2 = _mm("mm_dh2", dap3, w_up_f[l], M=S, N=D, K=F2, tm=tm_s, tn=t_d, tk=F, tb=True,
                  a_spec=pl.BlockSpec((None, tm_s, F), lambda i, j, k: (k, i, 0)),
                  b_spec=wt_spec(t_d, F))
        g_up = _mm("mm_gup", sv["h2"], dap3, M=D, N=F2, K=S, tm=D, tn=US, tk=S, ta=True,
                   a_spec=at_spec(S, D),
                   b_spec=pl.BlockSpec((None, S, US), lambda i, j, k: (j // (N_DEV // 2), 0, j % (N_DEV // 2))),
                   out_dtype=BF16, out_shape=(N_DEV, D, US),
                   out_spec=pl.BlockSpec((None, D, US), lambda i, j, k: (j, 0, 0)))
        g_up, dh2 = after(g_up, dh2)
        swap_up = rs_swap("w_up", g_up, D, US)
        dx1, dx1b, dg2 = _rms_bwd("rms2_bwd", sv["x1"], dh2, dx, n2g3, l)
        gS[l]["norm2_g"] = dg2.reshape(D)
        dmix = _mm("mm_dmix", dx1b, w_o_f[l], M=S, N=D, K=D, tm=S, tn=t_d, tk=D, tb=True,
                   a_spec=a_spec(S, D), b_spec=wt_spec(t_d, D))
        g_o = _mm("mm_go", sv["mixed"], dx1b, M=D, N=D, K=S, tm=D, tn=t_d, tk=S, ta=True,
                  a_spec=at_spec(S, D), b_spec=b_spec(S, t_d), out_dtype=BF16)
        g_o, dmix = after(g_o, dmix)
        swap_o = rs_swap("w_o", g_o, OS, D)
        dmix = rs_finish(l, swap_up, dmix)
        dq_r, dk_pad, dv_pad, dsink, dag = _attn_bwd("attn_bwd", sv["q"], sv["k"], sv["v"], sv["attn"], dmix,
                                                     sink, ag3, l, cfg)
        dq_r = rs_finish(l, swap_o, dq_r)
        gS[l]["sink"] = dsink[:, 0]
        gS[l]["attn_out_g"] = dag.reshape(AW)
        dzgu, dzgv, dws, dbs, dlng, dlnb, dog = _sgu_bwd("sgu_bwd", sv["z"], dmix, lng3, lnb3, ws_b, wst_b, bs_b,
                                                        og3, l, cfg)
        gS[l]["w_s"] = dws
        gS[l]["b_s"] = dbs[:, :, 0]
        gS[l]["sgu_ln_g"] = dlng.reshape(GW)
        gS[l]["sgu_ln_b"] = dlnb.reshape(GW)
        gS[l]["sgu_out_g"] = dog.reshape(GW)
        dzq, dzkv, dqg, dkg = _qkv_prep_bwd("qkv_prep_bwd", sv["z"], dq_r, dk_pad, dv_pad, qg3, kg3, cos2, sin2,
                                            l, cfg)
        gS[l]["q_norm_g"] = dqg.reshape(HEAD)
        gS[l]["k_norm_g"] = dkg.reshape(HEAD)
        dz = jnp.concatenate([dzq, dzkv, dzgu, dzgv], axis=1)
        dh = _mm("mm_dh", dz, w_in_f[l], M=S, N=D, K=IN, tm=tm_s, tn=t_d, tk=IN,
                 a_spec=a_spec(tm_s, IN), b_spec=w_spec(IN, t_d))
        g_in = _mm("mm_gin", dz, sv["h"], M=IN, N=D, K=S, tm=t_in, tn=D, tk=S, ta=True,
                   a_spec=at_spec(S, t_in), b_spec=b_spec(S, D), out_dtype=BF16)
        g_in, dh = after(g_in, dh)
        dx, dxb, dg1 = _rms_bwd("rms1_bwd", sv["x"], dh, dx1, n1g3, l)
        gS[l]["norm1_g"] = dg1.reshape(D)
        carried = rs_swap("w_in", g_in, INS, D, small=_pack([gS[l][n] for n in SMALL] + [gS[l]["conv_w"]]))
    dx = rs_finish(0, carried, dx)
    grad_x = dx.reshape(1, S, D)

    grads, deltas, new_m, new_v = {}, {}, {}, {}
    results, placed = {}, dx
    qvec = jnp.reshape(2 * lax.axis_index("x") + lax.axis_index("y"), (1,)).astype(jnp.int32)
    for l, name, begun in started:
        sent, landed = _ici_wait("rs_wait_" + name, begun, 1, placed)
        recv[l][name] = (sent, landed)
        wmv = (w_in_t, m_w_in_t, v_w_in_t) if name == "w_in" else (weights[name], mom_m[name], mom_v[name])
        results[name] = _adamw_sum("adamw_" + name, landed[0], sent[0], qvec, *wmv, l, results.get(name), 16)
        placed = results[name][0]
    results["w_in"] = [jnp.swapaxes(r, 1, 2) for r in results["w_in"]]
    for name, res in results.items():
        grads[name], deltas[name], new_m[name], new_v[name] = res

    small_shapes = [weights[n].shape[1:] for n in SMALL] + [(3, F2)]
    SR = recv[0]["w_in"][0][1].shape[0]
    tr_s = _tile(SR, 512, 8)
    sblk = pl.BlockSpec((tr_s, 128), lambda i, q: (i, 0))
    sshp = jax.ShapeDtypeStruct((SR, 128), F32)
    small_parts = [[None] * L for _ in range(4)]
    for l in reversed(range(L)):
        def small_body(q_ref, r_ref, o_ref, w_ref, m_ref, v_ref, g_ref, d_ref, nm_ref, nv_ref):
            terms = _chip_terms(q_ref[0], o_ref[...], r_ref, N_DEV // 2)
            g = terms[0]
            for t in terms[1:]:
                g = g + t
            d, m2, v2 = _adamw_math(w_ref[...], g, m_ref[...], v_ref[...])
            g_ref[...] = g
            d_ref[...] = d
            nm_ref[...] = m2
            nv_ref[...] = v2

        def pack_small(src):
            return _pack([src[n][l] for n in SMALL] + [jnp.zeros((3, F2), F32)])

        sent, landed = recv[l]["w_in"]
        packed = pl.pallas_call(
            small_body, name="adamw_small",
            grid_spec=pltpu.PrefetchScalarGridSpec(
                num_scalar_prefetch=1, grid=(SR // tr_s,),
                in_specs=[pl.BlockSpec((N_DEV // 2, tr_s, 128), lambda i, q: (0, i, 0)), sblk, sblk, sblk, sblk],
                out_specs=[sblk, sblk, sblk, sblk]),
            out_shape=[sshp, sshp, sshp, sshp],
            compiler_params=_params(("arbitrary",)),
        )(qvec, landed[1], sent[1], pack_small(weights), pack_small(mom_m), pack_small(mom_v))
        for t in range(4):
            small_parts[t][l] = _unpack(packed[t], small_shapes)
    for t, store in enumerate([grads, deltas, new_m, new_v]):
        for i, n in enumerate(SMALL):
            store[n] = jnp.stack([small_parts[t][l][i] for l in range(L)])
    conv_full = jnp.stack([small_parts[0][l][len(SMALL)] for l in range(L)])
    me_i = _dev_index(_mesh_pos())
    g_cw = lax.dynamic_slice_in_dim(conv_full, me_i * US, US, axis=2)
    grads["conv_w"] = g_cw
    d_cw, m_cw, v_cw = _adamw_plain("adamw_conv_w", g_cw.reshape(L * 3, US), conv_w.reshape(L * 3, US),
                                    m_conv_w.reshape(L * 3, US), v_conv_w.reshape(L * 3, US))
    deltas["conv_w"] = d_cw.reshape(L, 3, US)
    new_m["conv_w"] = m_cw.reshape(L, 3, US)
    new_v["conv_w"] = v_cw.reshape(L, 3, US)

    return (loss, grad_x, *[grads[n] for n in order], *[deltas[n] for n in order],
            *[new_m[n] for n in order], *[new_v[n] for n in order])
```

```python
import jax
import jax.numpy as jnp
from jax import lax
from jax.experimental import pallas as pl
from jax.experimental.pallas import tpu as pltpu
from jax.experimental.pallas import tpu_sc as plsc

F32 = jnp.float32
BF16 = jnp.bfloat16
MESH = pl.DeviceIdType.MESH

N_DEV = 8
HEAD = 128
EPS = 1e-6
MASK_VALUE = -1e30
ROPE_THETA = 10000.0
GELU_C = 0.7978845608028654
GELU_A = 0.044715

ADAM_LR = 0.001
ADAM_B1 = 0.9
ADAM_B2 = 0.999
ADAM_EPS = 1e-08
ADAM_WD = 0.01
ADAM_STEP = 10

VMEM_LIMIT = 56 * 1024 * 1024


def _tile(n, pref, mult):
    best = None
    for t in range(mult, min(n, pref) + 1, mult):
        if n % t == 0:
            best = t
    return n if best is None else best


def _params(sem=None):
    kw = dict(vmem_limit_bytes=VMEM_LIMIT)
    if sem is not None:
        kw["dimension_semantics"] = sem
    return pltpu.CompilerParams(**kw)


def _gelu(x):
    return x * (0.5 * (1.0 + jnp.tanh(GELU_C * (x + GELU_A * (x * x * x)))))


def _gelu_grad(x):
    t = jnp.tanh(GELU_C * (x + GELU_A * (x * x * x)))
    return 0.5 * (1.0 + t) + 0.5 * x * (1.0 - t * t) * (GELU_C * (1.0 + 3.0 * GELU_A * (x * x)))


def _sigmoid(x):
    return 1.0 / (1.0 + jnp.exp(-x))


def _vec_spec(l, n):
    return pl.BlockSpec((None, 1, n), lambda *_: (l, 0, 0))


def _mm(name, a, b, *, M, N, K, tm, tn, tk, a_spec, b_spec, ta=False, tb=False, out_dtype=F32, res=None,
        out_shape=None, out_spec=None):
    nm, nn, nk = M // tm, N // tn, K // tk
    assert nm * tm == M and nn * tn == N and nk * tk == K
    assert not (ta and nk > 1)
    dims = (((1,), (1,)), ((), ())) if tb else (((1,), (0,)), ((), ()))

    def body(*refs):
        refs = list(refs)
        a_ref = refs.pop(0)
        b_ref = refs.pop(0)
        r_ref = refs.pop(0) if res is not None else None
        o_ref = refs.pop(0)
        acc = refs.pop(0) if nk > 1 else None
        at = refs.pop(0) if ta else None
        k = pl.program_id(2)
        if ta:
            @pl.when(pl.program_id(1) == 0)
            def _():
                at[...] = a_ref[...].T
            lhs = at[...]
        else:
            lhs = a_ref[...]
        p = lax.dot_general(lhs, b_ref[...], dims, preferred_element_type=F32)

        def finish(r):
            if r_ref is not None:
                r = r_ref[...] + r
            o_ref[...] = r.astype(out_dtype)

        if nk == 1:
            finish(p)
        else:
            @pl.when(k == 0)
            def _():
                acc[...] = p

            @pl.when(k > 0)
            def _():
                acc[...] += p

            @pl.when(k == nk - 1)
            def _():
                finish(acc[...])

    in_specs = [a_spec, b_spec]
    args = [a, b]
    if res is not None:
        in_specs.append(pl.BlockSpec((tm, tn), lambda i, j, k: (i, j)))
        args.append(res)
    scratch = []
    if nk > 1:
        scratch.append(pltpu.VMEM((tm, tn), F32))
    if ta:
        scratch.append(pltpu.VMEM((tm, tk), BF16))
    return pl.pallas_call(
        body, name=name, grid=(nm, nn, nk),
        in_specs=in_specs,
        out_specs=pl.BlockSpec((tm, tn), lambda i, j, k: (i, j)) if out_spec is None else out_spec,
        out_shape=jax.ShapeDtypeStruct((M, N) if out_shape is None else out_shape, out_dtype),
        scratch_shapes=scratch,
        compiler_params=_params(("arbitrary", "arbitrary", "arbitrary")),
    )(*args)


def _rms_fwd(name, x, g3, l):
    S, D = x.shape
    tr = _tile(S, 256, 16)

    def body(x_ref, g_ref, h_ref):
        xv = x_ref[...]
        r = lax.rsqrt(jnp.mean(xv * xv, axis=-1, keepdims=True) + EPS)
        h_ref[...] = ((xv * r) * g_ref[...]).astype(BF16)

    return pl.pallas_call(
        body, name=name, grid=(S // tr,),
        in_specs=[pl.BlockSpec((tr, D), lambda i: (i, 0)), _vec_spec(l, D)],
        out_specs=pl.BlockSpec((tr, D), lambda i: (i, 0)),
        out_shape=jax.ShapeDtypeStruct((S, D), BF16),
        compiler_params=_params(("arbitrary",)),
    )(x, g3)


def _rope(t, cos2, sin2):
    return t * cos2 + pltpu.roll(t, HEAD // 2, axis=1) * sin2


def _qkv_prep(name, z, qg3, kg3, cos2, sin2, l, cfg):
    S, AW, KVW, NQ, NKV = cfg["S"], cfg["AW"], cfg["KVW"], cfg["NQ"], cfg["NKV"]
    tr = _tile(S, 256, 16)
    kv_blk = (3 * AW) // (2 * KVW)

    def body(zq_ref, zkv_ref, qg_ref, kg_ref, c_ref, s_ref, q_ref, k_ref, v_ref):
        cosv, sinv = c_ref[...], s_ref[...]

        def norm_rope(t, g):
            r = lax.rsqrt(jnp.mean(t * t, axis=-1, keepdims=True) + EPS)
            return _rope((t * r) * g, cosv, sinv)

        for h in range(NQ):
            sl = slice(h * HEAD, (h + 1) * HEAD)
            q_ref[:, sl] = norm_rope(zq_ref[:, sl], qg_ref[...]).astype(BF16)
        for h in range(NKV):
            sl = slice(h * HEAD, (h + 1) * HEAD)
            k_ref[:, sl] = norm_rope(zkv_ref[:, sl], kg_ref[...]).astype(BF16)
        v_ref[...] = zkv_ref[:, KVW:].astype(BF16)

    return pl.pallas_call(
        body, name=name, grid=(S // tr,),
        in_specs=[pl.BlockSpec((tr, AW), lambda i: (i, 0)),
                  pl.BlockSpec((tr, 2 * KVW), lambda i: (i, kv_blk)),
                  _vec_spec(l, HEAD), _vec_spec(l, HEAD),
                  pl.BlockSpec((tr, HEAD), lambda i: (i, 0)),
                  pl.BlockSpec((tr, HEAD), lambda i: (i, 0))],
        out_specs=[pl.BlockSpec((tr, AW), lambda i: (i, 0)),
                   pl.BlockSpec((tr, KVW), lambda i: (i, 0)),
                   pl.BlockSpec((tr, KVW), lambda i: (i, 0))],
        out_shape=[jax.ShapeDtypeStruct((S, AW), BF16),
                   jax.ShapeDtypeStruct((S, KVW), BF16),
                   jax.ShapeDtypeStruct((S, KVW), BF16)],
        compiler_params=_params(("arbitrary",)),
    )(z, z, qg3, kg3, cos2, sin2)


def _band_specs(width, nb):
    return [pl.BlockSpec((HEAD, width), lambda n: (jnp.maximum(n - 1, 0), 0)),
            pl.BlockSpec((HEAD, width), lambda n: (n, 0)),
            pl.BlockSpec((HEAD, width), lambda n: (jnp.minimum(n + 1, nb - 1), 0))]


def _attn_probs(qs, kj, n, sink_of_row, S, G):
    s = lax.dot_general(qs, kj, (((1,), (1,)), ((), ())), preferred_element_type=F32) * (HEAD ** -0.5)
    rows = lax.broadcasted_iota(jnp.int32, (G * HEAD, 3 * HEAD), 0)
    cols = lax.broadcasted_iota(jnp.int32, (G * HEAD, 3 * HEAD), 1)
    qi = rows & (HEAD - 1)
    kpos = n * HEAD - HEAD + cols
    valid = (cols >= qi) & (cols <= qi + 2 * HEAD) & (kpos >= 0) & (kpos < S)
    s = jnp.where(valid, s, MASK_VALUE)
    m = jnp.maximum(jnp.max(s, axis=-1, keepdims=True), sink_of_row)
    p = jnp.exp(s - m)
    e_sink = jnp.exp(sink_of_row - m)
    inv = 1.0 / (jnp.sum(p, axis=-1, keepdims=True) + e_sink)
    return p * inv, e_sink * inv


def _sink_rows(sink_ref, l, j, G):
    hidx = lax.broadcasted_iota(jnp.int32, (G * HEAD, 1), 0) // HEAD
    col = jnp.full((G * HEAD, 1), sink_ref[l, j * G], F32)
    for g in range(1, G):
        col = jnp.where(hidx == g, sink_ref[l, j * G + g], col)
    return col


def _attn_fwd(name, q, k, v, sink, ag3, l, cfg):
    S, AW, KVW, NKV, G = cfg["S"], cfg["AW"], cfg["KVW"], cfg["NKV"], cfg["G"]
    nb = S // HEAD

    def body(q_ref, kp, kc, kn, vp, vc, vn, sink_ref, ag_ref, a_ref, mix_ref):
        n = pl.program_id(0)
        kb = jnp.concatenate([kp[...], kc[...], kn[...]], axis=0)
        vb = jnp.concatenate([vp[...], vc[...], vn[...]], axis=0)
        for j in range(NKV):
            sl = slice(j * HEAD, (j + 1) * HEAD)
            qs = jnp.concatenate([q_ref[:, (j * G + g) * HEAD:(j * G + g + 1) * HEAD] for g in range(G)], axis=0)
            probs, _ = _attn_probs(qs, kb[:, sl], n, _sink_rows(sink_ref, l, j, G), S, G)
            o = jnp.dot(probs.astype(BF16), vb[:, sl], preferred_element_type=F32)
            for g in range(G):
                a_ref[:, (j * G + g) * HEAD:(j * G + g + 1) * HEAD] = o[g * HEAD:(g + 1) * HEAD]
        a = a_ref[...]
        r = lax.rsqrt(jnp.mean(a * a, axis=-1, keepdims=True) + EPS)
        mix_ref[...] = ((a * r) * ag_ref[...]).astype(BF16)

    return pl.pallas_call(
        body, name=name, grid=(nb,),
        in_specs=[pl.BlockSpec((HEAD, AW), lambda n: (n, 0))] + _band_specs(KVW, nb) + _band_specs(KVW, nb)
                 + [pl.BlockSpec(memory_space=pltpu.SMEM), _vec_spec(l, AW)],
        out_specs=[pl.BlockSpec((HEAD, AW), lambda n: (n, 0)), pl.BlockSpec((HEAD, AW), lambda n: (n, 0))],
        out_shape=[jax.ShapeDtypeStruct((S, AW), F32), jax.ShapeDtypeStruct((S, AW), BF16)],
        compiler_params=_params(("arbitrary",)),
    )(q, k, k, k, v, v, v, sink, ag3)


def _sgu_forward_math(gu, gv, lng, lnb):
    u = _gelu(gu)
    vv = _gelu(gv)
    mu = jnp.mean(vv, axis=-1, keepdims=True)
    xc = vv - mu
    rstd = lax.rsqrt(jnp.mean(xc * xc, axis=-1, keepdims=True) + EPS)
    xhat = xc * rstd
    vn = xhat * lng + lnb
    return u, xhat, rstd, vn


def _sgu_fwd(name, z, lng3, lnb3, ws_b, bs_b, og3, l, cfg):
    S, GW, NG = cfg["S"], cfg["GW"], cfg["NG"]

    def body(gu_ref, gv_ref, lng_ref, lnb_ref, ws_ref, bs_ref, og_ref, mix_ref, sg_ref):
        u, _, _, vn = _sgu_forward_math(gu_ref[...], gv_ref[...], lng_ref[...], lnb_ref[...])
        vnb = vn.astype(BF16)
        for h in range(NG):
            sl = slice(h * HEAD, (h + 1) * HEAD)
            f = jnp.dot(ws_ref[h], vnb[:, sl], preferred_element_type=F32) + bs_ref[h]
            sg_ref[:, sl] = u[:, sl] * f
        sg = sg_ref[...]
        r = lax.rsqrt(jnp.mean(sg * sg, axis=-1, keepdims=True) + EPS)
        mix_ref[...] = ((sg * r) * og_ref[...]).astype(BF16)

    return pl.pallas_call(
        body, name=name, grid=(S // HEAD,),
        in_specs=[pl.BlockSpec((HEAD, GW), lambda c: (c, 1)),
                  pl.BlockSpec((HEAD, GW), lambda c: (c, 2)),
                  _vec_spec(l, GW), _vec_spec(l, GW),
                  pl.BlockSpec((None, NG, HEAD, HEAD), lambda c: (l, 0, 0, 0)),
                  pl.BlockSpec((None, NG, HEAD, HEAD), lambda c: (l, 0, 0, 0)),
                  _vec_spec(l, GW)],
        out_specs=pl.BlockSpec((HEAD, GW), lambda c: (c, 0)),
        out_shape=jax.ShapeDtypeStruct((S, GW), BF16),
        scratch_shapes=[pltpu.VMEM((HEAD, GW), F32)],
        compiler_params=_params(("arbitrary",)),
    )(z, z, lng3, lnb3, ws_b, bs_b, og3)


CONV_HALO = 8
CONV_ROWS = 128


def _for_row_windows(S, fn):
    R, W = CONV_ROWS, CONV_ROWS + 2 * CONV_HALO
    n = S // R
    assert n * R == S and n >= 2
    fn(0, 0, 0)
    if n > 2:
        def mid(k, carry):
            fn(pl.multiple_of(k * R - CONV_HALO, CONV_HALO), CONV_HALO, pl.multiple_of(k * R, R))
            return carry

        lax.fori_loop(1, n - 1, mid, 0)
    fn(S - W, 2 * CONV_HALO, S - R)


def _shifts(t):
    W = t.shape[0]
    row = lax.broadcasted_iota(jnp.int32, t.shape, 0)
    dn = jnp.where(row == 0, 0.0, pltpu.roll(t, 1, axis=0))
    up = jnp.where(row == W - 1, 0.0, pltpu.roll(t, W - 1, axis=0))
    return dn, up


def _conv3(t, w, b):
    dn, up = _shifts(t)
    return ((b + dn * w[0:1]) + t * w[1:2]) + up * w[2:3], dn, up


def _conv_glu_fwd(name, ap, cw, cb3, l, cfg):
    S, F = cfg["S"], cfg["F"]
    tc = _tile(F, 256, 128)
    nf = F // tc
    R, W = CONV_ROWS, CONV_ROWS + 2 * CONV_HALO

    def body(g_ref, u_ref, wg_ref, wu_ref, bg_ref, bu_ref, y_ref):
        wg, wu, bg, bu = wg_ref[...], wu_ref[...], bg_ref[...], bu_ref[...]

        def window(start, lo, out0):
            rows = pl.ds(start, W)
            ag, _, _ = _conv3(g_ref[rows, :], wg, bg)
            au, _, _ = _conv3(u_ref[rows, :], wu, bu)
            y = (ag * _sigmoid(ag)) * au
            y_ref[pl.ds(out0, R), :] = y[lo:lo + R].astype(BF16)

        _for_row_windows(S, window)

    return pl.pallas_call(
        body, name=name, grid=(nf,),
        in_specs=[pl.BlockSpec((S, tc), lambda j: (0, j)),
                  pl.BlockSpec((S, tc), lambda j: (0, j + nf)),
                  pl.BlockSpec((None, 3, tc), lambda j: (l, 0, j)),
                  pl.BlockSpec((None, 3, tc), lambda j: (l, 0, j + nf)),
                  pl.BlockSpec((None, 1, tc), lambda j: (l, 0, j)),
                  pl.BlockSpec((None, 1, tc), lambda j: (l, 0, j + nf))],
        out_specs=pl.BlockSpec((S, tc), lambda j: (0, j)),
        out_shape=jax.ShapeDtypeStruct((S, F), BF16),
        compiler_params=_params(("arbitrary",)),
    )(ap, ap, cw, cw, cb3, cb3)


def _loss_bwd(name, y, target):
    S, D = y.shape
    tr = _tile(S, 256, 16)

    def body(y_ref, t_ref, loss_ref, d_ref, db_ref):
        @pl.when(pl.program_id(0) == 0)
        def _():
            loss_ref[...] = jnp.zeros_like(loss_ref)

        err = y_ref[...] - t_ref[...]
        part = 0.5 * jnp.sum(jnp.mean(err * err, axis=-1, keepdims=True), axis=0, keepdims=True)
        loss_ref[...] += jnp.broadcast_to(part, loss_ref.shape)
        d = err * (1.0 / D)
        d_ref[...] = d
        db_ref[...] = d.astype(BF16)

    return pl.pallas_call(
        body, name=name, grid=(S // tr,),
        in_specs=[pl.BlockSpec((tr, D), lambda i: (i, 0)), pl.BlockSpec((tr, D), lambda i: (i, 0))],
        out_specs=[pl.BlockSpec((8, 128), lambda i: (0, 0)),
                   pl.BlockSpec((tr, D), lambda i: (i, 0)),
                   pl.BlockSpec((tr, D), lambda i: (i, 0))],
        out_shape=[jax.ShapeDtypeStruct((8, 128), F32),
                   jax.ShapeDtypeStruct((S, D), F32),
                   jax.ShapeDtypeStruct((S, D), BF16)],
        compiler_params=_params(("arbitrary",)),
    )(y, target)


def _rms_bwd(name, x, dh, dres, g3, l):
    S, D = x.shape
    tr = _tile(S, 256, 16)

    def body(x_ref, dh_ref, dr_ref, g_ref, dx_ref, dxb_ref, dg_ref):
        @pl.when(pl.program_id(0) == 0)
        def _():
            dg_ref[...] = jnp.zeros_like(dg_ref)

        xv = x_ref[...]
        dhv = dh_ref[...]
        r = lax.rsqrt(jnp.mean(xv * xv, axis=-1, keepdims=True) + EPS)
        xhat = xv * r
        dhg = dhv * g_ref[...]
        dx = dr_ref[...] + r * (dhg - xhat * jnp.mean(dhg * xhat, axis=-1, keepdims=True))
        dx_ref[...] = dx
        dxb_ref[...] = dx.astype(BF16)
        dg_ref[...] += jnp.sum(dhv * xhat, axis=0, keepdims=True)

    return pl.pallas_call(
        body, name=name, grid=(S // tr,),
        in_specs=[pl.BlockSpec((tr, D), lambda i: (i, 0)), pl.BlockSpec((tr, D), lambda i: (i, 0)),
                  pl.BlockSpec((tr, D), lambda i: (i, 0)), _vec_spec(l, D)],
        out_specs=[pl.BlockSpec((tr, D), lambda i: (i, 0)), pl.BlockSpec((tr, D), lambda i: (i, 0)),
                   pl.BlockSpec((1, D), lambda i: (0, 0))],
        out_shape=[jax.ShapeDtypeStruct((S, D), F32), jax.ShapeDtypeStruct((S, D), BF16),
                   jax.ShapeDtypeStruct((1, D), F32)],
        compiler_params=_params(("arbitrary",)),
    )(x, dh, dres, g3)


def _glu_conv_bwd(name, dy, ap, cw, cb3, l, cfg):
    S, F = cfg["S"], cfg["F"]
    tc = 128
    nf = F // tc
    R, W = CONV_ROWS, CONV_ROWS + 2 * CONV_HALO

    def body(dy_ref, g_ref, u_ref, wg_ref, wu_ref, bg_ref, bu_ref, dap_ref, dw_ref, db_ref, acc):
        acc[...] = jnp.zeros_like(acc)
        wg, wu, bg, bu = wg_ref[...], wu_ref[...], bg_ref[...], bu_ref[...]

        def window(start, lo, out0):
            rows = pl.ds(start, W)
            apg, apu, dyv = g_ref[rows, :], u_ref[rows, :], dy_ref[rows, :]
            ag, g_dn, g_up = _conv3(apg, wg, bg)
            au, u_dn, u_up = _conv3(apu, wu, bu)
            sig = _sigmoid(ag)
            da_u = dyv * (ag * sig)
            da_g = (dyv * au) * (sig * (1.0 + ag * (1.0 - sig)))

            def back(da, w):
                prv, nxt = _shifts(da)
                return ((nxt * w[0:1] + da * w[1:2]) + prv * w[2:3])[lo:lo + R].astype(BF16)

            dap_ref[0, pl.ds(out0, R), :] = back(da_g, wg)
            dap_ref[1, pl.ds(out0, R), :] = back(da_u, wu)

            def add(i, prod):
                acc[i] += jnp.sum(prod[lo:lo + R].reshape(R // 8, 8, tc), axis=0)

            for i, other in enumerate([g_dn, apg, g_up]):
                add(i, da_g * other)
            for i, other in enumerate([u_dn, apu, u_up]):
                add(3 + i, da_u * other)
            add(6, da_g)
            add(7, da_u)

        _for_row_windows(S, window)
        col = [jnp.sum(acc[i], axis=0, keepdims=True) for i in range(8)]
        dw_ref[0] = jnp.concatenate(col[0:3], axis=0)
        dw_ref[1] = jnp.concatenate(col[3:6], axis=0)
        db_ref[0] = col[6]
        db_ref[1] = col[7]

    return pl.pallas_call(
        body, name=name, grid=(nf,),
        in_specs=[pl.BlockSpec((S, tc), lambda j: (0, j)),
                  pl.BlockSpec((S, tc), lambda j: (0, j)),
                  pl.BlockSpec((S, tc), lambda j: (0, j + nf)),
                  pl.BlockSpec((None, 3, tc), lambda j: (l, 0, j)),
                  pl.BlockSpec((None, 3, tc), lambda j: (l, 0, j + nf)),
                  pl.BlockSpec((None, 1, tc), lambda j: (l, 0, j)),
                  pl.BlockSpec((None, 1, tc), lambda j: (l, 0, j + nf))],
        out_specs=[pl.BlockSpec((2, S, tc), lambda j: (0, 0, j)),
                   pl.BlockSpec((2, 3, tc), lambda j: (0, 0, j)),
                   pl.BlockSpec((2, 1, tc), lambda j: (0, 0, j))],
        out_shape=[jax.ShapeDtypeStruct((2, S, F), BF16),
                   jax.ShapeDtypeStruct((2, 3, F), F32),
                   jax.ShapeDtypeStruct((2, 1, F), F32)],
        scratch_shapes=[pltpu.VMEM((8, 8, tc), F32)],
        compiler_params=_params(("arbitrary",)),
    )(dy, ap, ap, cw, cw, cb3, cb3)


def _attn_bwd(name, q, k, v, attn, dmix, sink, ag3, l, cfg):
    S, AW, KVW, NQ, NKV, G = cfg["S"], cfg["AW"], cfg["KVW"], cfg["NQ"], cfg["NKV"], cfg["G"]
    nb = S // HEAD
    scale = HEAD ** -0.5

    def body(q_ref, kp, kc, kn, vp, vc, vn, a_ref, dm_ref, sink_ref, ag_ref,
             dq_ref, dk_ref, dv_ref, dsink_ref, dag_ref, da_scr):
        n = pl.program_id(0)

        @pl.when(n == 0)
        def _():
            dk_ref[...] = jnp.zeros_like(dk_ref)
            dv_ref[...] = jnp.zeros_like(dv_ref)
            dsink_ref[...] = jnp.zeros_like(dsink_ref)
            dag_ref[...] = jnp.zeros_like(dag_ref)

        a = a_ref[...]
        dm = dm_ref[...]
        r = lax.rsqrt(jnp.mean(a * a, axis=-1, keepdims=True) + EPS)
        xhat = a * r
        dmg = dm * ag_ref[...]
        da_scr[...] = r * (dmg - xhat * jnp.mean(dmg * xhat, axis=-1, keepdims=True))
        dag_ref[...] += jnp.sum(dm * xhat, axis=0, keepdims=True)

        kb = jnp.concatenate([kp[...], kc[...], kn[...]], axis=0)
        vb = jnp.concatenate([vp[...], vc[...], vn[...]], axis=0)
        band = pl.ds(pl.multiple_of(n * HEAD, HEAD), 3 * HEAD)
        for j in range(NKV):
            sl = slice(j * HEAD, (j + 1) * HEAD)
            heads = [slice((j * G + g) * HEAD, (j * G + g + 1) * HEAD) for g in range(G)]
            qs = jnp.concatenate([q_ref[:, hs] for hs in heads], axis=0)
            do = jnp.concatenate([da_scr[:, hs] for hs in heads], axis=0)
            kj, vj = kb[:, sl], vb[:, sl]
            probs, p_sink = _attn_probs(qs, kj, n, _sink_rows(sink_ref, l, j, G), S, G)
            dob = do.astype(BF16)
            dprobs = lax.dot_general(dob, vj, (((1,), (1,)), ((), ())), preferred_element_type=F32)
            delta = jnp.sum(dprobs * probs, axis=-1, keepdims=True)
            ds = (probs * (dprobs - delta)) * scale
            dsb = ds.astype(BF16)
            dsk = -(p_sink * delta)
            dq = jnp.dot(dsb, kj, preferred_element_type=F32)
            for g in range(G):
                dq_ref[:, heads[g]] = dq[g * HEAD:(g + 1) * HEAD]
                part = jnp.sum(dsk[g * HEAD:(g + 1) * HEAD], axis=0, keepdims=True)
                dsink_ref[j * G + g:j * G + g + 1, :] += jnp.broadcast_to(part, (1, HEAD))
            dk_ref[band, sl] += lax.dot_general(dsb, qs, (((0,), (0,)), ((), ())), preferred_element_type=F32)
            dv_ref[band, sl] += lax.dot_general(probs.astype(BF16), dob, (((0,), (0,)), ((), ())),
                                                preferred_element_type=F32)

    return pl.pallas_call(
        body, name=name, grid=(nb,),
        in_specs=[pl.BlockSpec((HEAD, AW), lambda n: (n, 0))] + _band_specs(KVW, nb) + _band_specs(KVW, nb)
                 + [pl.BlockSpec((HEAD, AW), lambda n: (n, 0)),
                    pl.BlockSpec((HEAD, AW), lambda n: (n, 0)),
                    pl.BlockSpec(memory_space=pltpu.SMEM), _vec_spec(l, AW)],
        out_specs=[pl.BlockSpec((HEAD, AW), lambda n: (n, 0)),
                   pl.BlockSpec((S + 2 * HEAD, KVW), lambda n: (0, 0)),
                   pl.BlockSpec((S + 2 * HEAD, KVW), lambda n: (0, 0)),
                   pl.BlockSpec((NQ, HEAD), lambda n: (0, 0)),
                   pl.BlockSpec((1, AW), lambda n: (0, 0))],
        out_shape=[jax.ShapeDtypeStruct((S, AW), F32),
                   jax.ShapeDtypeStruct((S + 2 * HEAD, KVW), F32),
                   jax.ShapeDtypeStruct((S + 2 * HEAD, KVW), F32),
                   jax.ShapeDtypeStruct((NQ, HEAD), F32),
                   jax.ShapeDtypeStruct((1, AW), F32)],
        scratch_shapes=[pltpu.VMEM((HEAD, AW), F32)],
        compiler_params=_params(("arbitrary",)),
    )(q, k, k, k, v, v, v, attn, dmix, sink, ag3)


def _qkv_prep_bwd(name, z, dq, dk_pad, dv_pad, qg3, kg3, cos2, sin2, l, cfg):
    S, AW, KVW, NQ, NKV = cfg["S"], cfg["AW"], cfg["KVW"], cfg["NQ"], cfg["NKV"]
    kv_blk = (3 * AW) // (2 * KVW)

    def body(zq_ref, zkv_ref, dq_ref, dk_ref, dv_ref, qg_ref, kg_ref, c_ref, s_ref,
             dzq_ref, dzkv_ref, dqg_ref, dkg_ref):
        @pl.when(pl.program_id(0) == 0)
        def _():
            dqg_ref[...] = jnp.zeros_like(dqg_ref)
            dkg_ref[...] = jnp.zeros_like(dkg_ref)

        cosv, sinv = c_ref[...], s_ref[...]

        def back(t, dr, g):
            r = lax.rsqrt(jnp.mean(t * t, axis=-1, keepdims=True) + EPS)
            xhat = t * r
            dn = dr * cosv + pltpu.roll(dr * sinv, HEAD // 2, axis=1)
            dxh = dn * g
            dt = r * (dxh - xhat * jnp.mean(dxh * xhat, axis=-1, keepdims=True))
            return dt, jnp.sum(dn * xhat, axis=0, keepdims=True)

        gq = jnp.zeros((1, HEAD), F32)
        for h in range(NQ):
            sl = slice(h * HEAD, (h + 1) * HEAD)
            dt, gpart = back(zq_ref[:, sl], dq_ref[:, sl], qg_ref[...])
            dzq_ref[:, sl] = dt.astype(BF16)
            gq = gq + gpart
        dqg_ref[...] += gq
        gk = jnp.zeros((1, HEAD), F32)
        for h in range(NKV):
            sl = slice(h * HEAD, (h + 1) * HEAD)
            dt, gpart = back(zkv_ref[:, sl], dk_ref[:, sl], kg_ref[...])
            dzkv_ref[:, sl] = dt.astype(BF16)
            gk = gk + gpart
        dkg_ref[...] += gk
        dzkv_ref[:, KVW:] = dv_ref[...].astype(BF16)

    return pl.pallas_call(
        body, name=name, grid=(S // HEAD,),
        in_specs=[pl.BlockSpec((HEAD, AW), lambda i: (i, 0)),
                  pl.BlockSpec((HEAD, 2 * KVW), lambda i: (i, kv_blk)),
                  pl.BlockSpec((HEAD, AW), lambda i: (i, 0)),
                  pl.BlockSpec((HEAD, KVW), lambda i: (i + 1, 0)),
                  pl.BlockSpec((HEAD, KVW), lambda i: (i + 1, 0)),
                  _vec_spec(l, HEAD), _vec_spec(l, HEAD),
                  pl.BlockSpec((HEAD, HEAD), lambda i: (i, 0)),
                  pl.BlockSpec((HEAD, HEAD), lambda i: (i, 0))],
        out_specs=[pl.BlockSpec((HEAD, AW), lambda i: (i, 0)),
                   pl.BlockSpec((HEAD, 2 * KVW), lambda i: (i, 0)),
                   pl.BlockSpec((1, HEAD), lambda i: (0, 0)),
                   pl.BlockSpec((1, HEAD), lambda i: (0, 0))],
        out_shape=[jax.ShapeDtypeStruct((S, AW), BF16),
                   jax.ShapeDtypeStruct((S, 2 * KVW), BF16),
                   jax.ShapeDtypeStruct((1, HEAD), F32),
                   jax.ShapeDtypeStruct((1, HEAD), F32)],
        compiler_params=_params(("arbitrary",)),
    )(z, z, dq, dk_pad, dv_pad, qg3, kg3, cos2, sin2)


def _sgu_bwd(name, z, dmix, lng3, lnb3, ws_b, wst_b, bs_b, og3, l, cfg):
    S, GW, NG = cfg["S"], cfg["GW"], cfg["NG"]

    def body(gu_ref, gv_ref, dm_ref, lng_ref, lnb_ref, ws_ref, wst_ref, bs_ref, og_ref,
             dgu_ref, dgv_ref, dws_ref, dbs_ref, dlng_ref, dlnb_ref, dog_ref, sg_scr, f_scr, dvn_scr):
        @pl.when(pl.program_id(0) == 0)
        def _():
            dws_ref[...] = jnp.zeros_like(dws_ref)
            dbs_ref[...] = jnp.zeros_like(dbs_ref)
            dlng_ref[...] = jnp.zeros_like(dlng_ref)
            dlnb_ref[...] = jnp.zeros_like(dlnb_ref)
            dog_ref[...] = jnp.zeros_like(dog_ref)

        gu, gv = gu_ref[...], gv_ref[...]
        lng = lng_ref[...]
        u, xhat, rstd, vn = _sgu_forward_math(gu, gv, lng, lnb_ref[...])
        vnb = vn.astype(BF16)
        for h in range(NG):
            sl = slice(h * HEAD, (h + 1) * HEAD)
            f = jnp.dot(ws_ref[h], vnb[:, sl], preferred_element_type=F32) + bs_ref[h]
            f_scr[:, sl] = f
            sg_scr[:, sl] = u[:, sl] * f
        sg = sg_scr[...]
        dm = dm_ref[...]
        r = lax.rsqrt(jnp.mean(sg * sg, axis=-1, keepdims=True) + EPS)
        sghat = sg * r
        dmg = dm * og_ref[...]
        dsg = r * (dmg - sghat * jnp.mean(dmg * sghat, axis=-1, keepdims=True))
        dog_ref[...] += jnp.sum(dm * sghat, axis=0, keepdims=True)
        du = dsg * f_scr[...]
        df = dsg * u
        dfb = df.astype(BF16)
        for h in range(NG):
            sl = slice(h * HEAD, (h + 1) * HEAD)
            dvn_scr[:, sl] = jnp.dot(wst_ref[h], dfb[:, sl], preferred_element_type=F32)
            dws_ref[h] += lax.dot_general(dfb[:, sl], vnb[:, sl], (((1,), (1,)), ((), ())),
                                          preferred_element_type=F32)
            dbs_ref[h] += jnp.broadcast_to(jnp.sum(df[:, sl], axis=-1, keepdims=True), (HEAD, HEAD))
        dvn = dvn_scr[...]
        dlng_ref[...] += jnp.sum(dvn * xhat, axis=0, keepdims=True)
        dlnb_ref[...] += jnp.sum(dvn, axis=0, keepdims=True)
        dxh = dvn * lng
        dvv = rstd * ((dxh - jnp.mean(dxh, axis=-1, keepdims=True))
                      - xhat * jnp.mean(dxh * xhat, axis=-1, keepdims=True))
        dgu_ref[...] = (du * _gelu_grad(gu)).astype(BF16)
        dgv_ref[...] = (dvv * _gelu_grad(gv)).astype(BF16)

    vec = pl.BlockSpec((1, GW), lambda c: (0, 0))
    mat = pl.BlockSpec((NG, HEAD, HEAD), lambda c: (0, 0, 0))
    wsp = pl.BlockSpec((None, NG, HEAD, HEAD), lambda c: (l, 0, 0, 0))
    return pl.pallas_call(
        body, name=name, grid=(S // HEAD,),
        in_specs=[pl.BlockSpec((HEAD, GW), lambda c: (c, 1)),
                  pl.BlockSpec((HEAD, GW), lambda c: (c, 2)),
                  pl.BlockSpec((HEAD, GW), lambda c: (c, 1)),
                  _vec_spec(l, GW), _vec_spec(l, GW), wsp, wsp, wsp, _vec_spec(l, GW)],
        out_specs=[pl.BlockSpec((HEAD, GW), lambda c: (c, 0)), pl.BlockSpec((HEAD, GW), lambda c: (c, 0)),
                   mat, mat, vec, vec, vec],
        out_shape=[jax.ShapeDtypeStruct((S, GW), BF16), jax.ShapeDtypeStruct((S, GW), BF16),
                   jax.ShapeDtypeStruct((NG, HEAD, HEAD), F32), jax.ShapeDtypeStruct((NG, HEAD, HEAD), F32),
                   jax.ShapeDtypeStruct((1, GW), F32), jax.ShapeDtypeStruct((1, GW), F32),
                   jax.ShapeDtypeStruct((1, GW), F32)],
        scratch_shapes=[pltpu.VMEM((HEAD, GW), F32), pltpu.VMEM((HEAD, GW), F32), pltpu.VMEM((HEAD, GW), F32)],
        compiler_params=_params(("arbitrary",)),
    )(z, z, dmix, lng3, lnb3, ws_b, wst_b, bs_b, og3)


def _mesh_pos():
    x, y, c = lax.axis_index("x"), lax.axis_index("y"), lax.axis_index("c")
    return x, y, c


def _dev_index(p):
    return 4 * p[0] + 2 * p[1] + p[2]


def _handshake(peers):
    barrier = pltpu.get_barrier_semaphore()
    for p in peers:
        pl.semaphore_signal(barrier, inc=1, device_id=p, device_id_type=MESH)
    pl.semaphore_wait(barrier, len(peers))


GATHER_COPIES = 10


def _gather_body(full, half, in_half):
    def body(in_ref, out_ref, send_sems, recv_sems, local_sem):
        x, y, c = _mesh_pos()
        me, sib = (x, y, c), (x, y, 1 - c)
        xn, yn, dg = (1 - x, y, c), (x, 1 - y, c), (1 - x, 1 - y, c)
        _handshake([sib, xn, yn])

        def copy(k, dst, to, src=None):
            return pltpu.make_async_remote_copy(
                src_ref=dst if src is None else src, dst_ref=dst,
                send_sem=send_sems.at[k], recv_sem=recv_sems.at[k], device_id=to, device_id_type=MESH)

        def win(dev, h=None):
            idx = _dev_index(dev)
            return full(out_ref, idx) if h is None else half(out_ref, idx, h)

        local = pltpu.make_async_copy(in_ref, win(me), local_sem)
        local.start()
        sends = [copy(0, win(me), sib, src=in_ref),
                 copy(1, win(me, 0), xn, src=in_half(in_ref, 0)),
                 copy(2, win(me, 1), yn, src=in_half(in_ref, 1)),
                 copy(4, win(me, 0), yn, src=in_half(in_ref, 0)),
                 copy(6, win(me, 1), xn, src=in_half(in_ref, 1))]
        for cp in sends:
            cp.start()
        copy(1, win(xn, 0), me).wait_recv()
        sends.append(copy(3, win(xn, 0), yn))
        sends[-1].start()
        copy(2, win(yn, 1), me).wait_recv()
        sends.append(copy(5, win(yn, 1), xn))
        sends[-1].start()
        copy(6, win(xn, 1), me).wait_recv()
        sends.append(copy(7, win(xn), sib))
        sends[-1].start()
        copy(4, win(yn, 0), me).wait_recv()
        sends.append(copy(8, win(yn), sib))
        sends[-1].start()
        copy(3, win(dg, 0), me).wait_recv()
        copy(5, win(dg, 1), me).wait_recv()
        sends.append(copy(9, win(dg), sib))
        sends[-1].start()
        sib_xn, sib_yn, sib_dg = (1 - x, y, 1 - c), (x, 1 - y, 1 - c), (1 - x, 1 - y, 1 - c)
        for k, dev in [(0, sib), (7, sib_xn), (8, sib_yn), (9, sib_dg)]:
            copy(k, win(dev), me).wait_recv()
        for cp in sends:
            cp.wait_send()
        local.wait()

    return body


def _all_gather(name, cid, shard, out_shape, windows):
    return pl.kernel(
        _gather_body(*windows), out_type=out_shape,
        mesh=plsc.ScalarSubcoreMesh(axis_name="seq", num_cores=1), name=name,
        scratch_types=[pltpu.SemaphoreType.DMA((GATHER_COPIES,)), pltpu.SemaphoreType.DMA((GATHER_COPIES,)),
                       pltpu.SemaphoreType.DMA],
        compiler_params=pltpu.CompilerParams(collective_id=cid),
    )(shard)


def _d2d_body(nb, ns):
    n = nb + ns

    def body(*refs):
        ins, outs = refs[:n], refs[n:2 * n]
        send_sems, recv_sems = refs[2 * n:]
        x, y, c = _mesh_pos()
        sib = (x, y, 1 - c)
        _handshake([sib])
        copies = []
        for t in range(n):
            cp = pltpu.make_async_remote_copy(
                src_ref=ins[t].at[:, 1 - c] if t < nb else ins[t], dst_ref=outs[t],
                send_sem=send_sems.at[t], recv_sem=recv_sems.at[t],
                device_id=sib, device_id_type=MESH)
            cp.start()
            copies.append(cp)
        for cp in copies:
            cp.wait()

    return body


def _rs_d2d(name, cid, bigs, smalls):
    shapes = [jax.ShapeDtypeStruct((g.shape[0],) + g.shape[2:], g.dtype) for g in bigs]
    shapes += [jax.ShapeDtypeStruct(s.shape, s.dtype) for s in smalls]
    n = len(shapes)
    return pl.kernel(
        _d2d_body(len(bigs), len(smalls)), out_type=shapes,
        mesh=plsc.ScalarSubcoreMesh(axis_name="seq", num_cores=1), name=name,
        scratch_types=[pltpu.SemaphoreType.DMA((n,)), pltpu.SemaphoreType.DMA((n,))],
        compiler_params=pltpu.CompilerParams(collective_id=cid),
    )(*bigs, *smalls)


def _pair_sum(name, g4, recv, cvec, row_mult):
    Q, _, R, C = g4.shape
    tr = _tile(R, 256, row_mult)

    def body(c_ref, a_ref, b_ref, o_ref):
        o_ref[...] = (a_ref[...].astype(F32) + b_ref[...].astype(F32)).astype(BF16)

    grid_spec = pltpu.PrefetchScalarGridSpec(
        num_scalar_prefetch=1, grid=(Q, R // tr),
        in_specs=[pl.BlockSpec((None, None, tr, C), lambda q, i, c_ref: (q, c_ref[0], i, 0)),
                  pl.BlockSpec((None, tr, C), lambda q, i, c_ref: (q, i, 0))],
        out_specs=pl.BlockSpec((None, tr, C), lambda q, i, c_ref: (q, i, 0)))
    return pl.pallas_call(
        body, name=name, grid_spec=grid_spec, out_shape=jax.ShapeDtypeStruct((Q, R, C), BF16),
        compiler_params=_params(("arbitrary", "arbitrary")),
    )(cvec, g4, recv)


def _add2(name, a, b):
    R, C = a.shape
    tr = _tile(R, 512, 8)

    def body(a_ref, b_ref, o_ref):
        o_ref[...] = a_ref[...] + b_ref[...]

    blk = pl.BlockSpec((tr, C), lambda i: (i, 0))
    return pl.pallas_call(body, name=name, grid=(R // tr,), in_specs=[blk, blk], out_specs=blk,
                          out_shape=jax.ShapeDtypeStruct((R, C), a.dtype),
                          compiler_params=_params(("arbitrary",)))(a, b)


def _ici_copies(nb, n, srcs, lands, send_sems, recv_sems):
    x, y, c = _mesh_pos()
    q_me = 2 * x + y
    copies = []
    for t in range(n):
        for k in range(1, 4):
            px, py = x ^ (k >> 1), y ^ (k & 1)
            copies.append(pltpu.make_async_remote_copy(
                src_ref=srcs[t].at[2 * px + py] if t < nb else srcs[t], dst_ref=lands[t].at[q_me],
                send_sem=send_sems.at[3 * t + k - 1], recv_sem=recv_sems.at[3 * t + k - 1],
                device_id=(px, py, c), device_id_type=MESH))
    return copies


_HBM = pl.BlockSpec(memory_space=pltpu.HBM)
_SEM = pl.BlockSpec(memory_space=pltpu.SEMAPHORE)
_DATAFLOW = pltpu.SideEffectType.DATAFLOW_SIDE_EFFECTING


def _ici_start(name, bigs, smalls):
    nb, n = len(bigs), len(bigs) + len(smalls)
    srcs = list(bigs) + list(smalls)
    lands = [lax.empty(g.shape, g.dtype) for g in bigs] + [lax.empty((N_DEV // 2,) + s.shape, s.dtype) for s in smalls]

    def body(*refs):
        src_refs, land_refs = refs[:n], refs[n:2 * n]
        send_sems, recv_sems = refs[2 * n], refs[2 * n + 1]
        token = refs[-1]
        for cp in _ici_copies(nb, n, src_refs, land_refs, send_sems, recv_sems):
            cp.start()
        token[...] = jnp.zeros_like(token)

    hbm = [pltpu.HBM(a.shape, a.dtype) for a in srcs + lands]
    args = [pltpu.with_memory_space_constraint(a, pltpu.HBM) for a in srcs + lands]
    out = pl.pallas_call(
        body, name=name,
        out_shape=[pltpu.SemaphoreType.DMA((3 * n,)), pltpu.SemaphoreType.DMA((3 * n,))] + hbm
                  + [jax.ShapeDtypeStruct((8, 128), F32)],
        in_specs=[_HBM] * (2 * n),
        out_specs=[_SEM, _SEM] + [_HBM] * (2 * n) + [pl.BlockSpec(memory_space=pltpu.VMEM)],
        input_output_aliases={i: 2 + i for i in range(2 * n)},
        compiler_params=pltpu.CompilerParams(has_side_effects=_DATAFLOW),
    )(*args)
    return out[0], out[1], list(out[2:2 + n]), list(out[2 + n:2 + 2 * n]), out[-1]


def _ici_wait(name, started, nb, not_before):
    send_sems, recv_sems, srcs, lands, token = started
    n = len(srcs)

    def body(*refs):
        src_refs, land_refs = refs[:n], refs[n:2 * n]
        send_refs, recv_refs = refs[2 * n], refs[2 * n + 1]
        for cp in _ici_copies(nb, n, src_refs, land_refs, send_refs, recv_refs):
            cp.wait_send()
            cp.wait_recv()

    hbm = [pltpu.HBM(a.shape, a.dtype) for a in srcs + lands]
    out = pl.pallas_call(
        body, name=name, out_shape=hbm,
        in_specs=[_HBM] * (2 * n) + [_SEM, _SEM] + [pl.BlockSpec(memory_space=pl.ANY)] * 2,
        out_specs=[_HBM] * (2 * n),
        input_output_aliases={i: i for i in range(2 * n)},
        compiler_params=pltpu.CompilerParams(has_side_effects=_DATAFLOW),
    )(*srcs, *lands, send_sems, recv_sems, not_before, token)
    return list(out[:n]), list(out[n:])


def _adamw_math(w, g, m, v):
    m2 = ADAM_B1 * m + (1.0 - ADAM_B1) * g
    v2 = ADAM_B2 * v + (1.0 - ADAM_B2) * (g * g)
    m_hat = m2 / (1.0 - ADAM_B1 ** ADAM_STEP)
    v_hat = v2 / (1.0 - ADAM_B2 ** ADAM_STEP)
    delta = -ADAM_LR * (m_hat / (jnp.sqrt(v_hat) + ADAM_EPS) + ADAM_WD * w)
    return delta, m2, v2


def _chip_terms(q, own, land_ref, n):
    return [jnp.where(q == s, own, land_ref[s]).astype(F32) for s in range(n)]


def _adamw_sum(name, land, own, qvec, w, m, v, l, prev, row_mult):
    NS, R, C = land.shape
    L = w.shape[0]
    tr = _tile(R, 128, row_mult)

    def body(*refs):
        q_ref, r_ref, o_ref, w_ref, m_ref, v_ref = refs[:6]
        g_ref, d_ref, nm_ref, nv_ref = refs[-4:]
        terms = _chip_terms(q_ref[0], o_ref[...], r_ref, NS)
        g = terms[0]
        for t in terms[1:]:
            g = g + t
        d, m2, v2 = _adamw_math(w_ref[...], g, m_ref[...], v_ref[...])
        g_ref[...] = g
        d_ref[...] = d
        nm_ref[...] = m2
        nv_ref[...] = v2

    blk = pl.BlockSpec((None, tr, C), lambda i, q: (l, i, 0))
    shp = jax.ShapeDtypeStruct((L, R, C), F32)
    in_specs = [pl.BlockSpec((NS, tr, C), lambda i, q: (0, i, 0)),
                pl.BlockSpec((None, tr, C), lambda i, q: (q[0], i, 0)), blk, blk, blk]
    args = [land, own, w, m, v]
    aliases = {}
    if prev is not None:
        in_specs += [pl.BlockSpec(memory_space=pl.ANY)] * 4
        args += list(prev)
        aliases = {6 + i: i for i in range(4)}
    grid_spec = pltpu.PrefetchScalarGridSpec(num_scalar_prefetch=1, grid=(R // tr,), in_specs=in_specs,
                                             out_specs=[blk, blk, blk, blk])
    return pl.pallas_call(
        body, name=name, grid_spec=grid_spec, out_shape=[shp, shp, shp, shp],
        input_output_aliases=aliases,
        compiler_params=_params(("arbitrary",)),
    )(qvec, *args)


def _adamw_plain(name, g, w, m, v):
    def body(g_ref, w_ref, m_ref, v_ref, d_ref, nm_ref, nv_ref):
        d, m2, v2 = _adamw_math(w_ref[...], g_ref[...], m_ref[...], v_ref[...])
        d_ref[...] = d
        nm_ref[...] = m2
        nv_ref[...] = v2

    shp = jax.ShapeDtypeStruct(g.shape, F32)
    return pl.pallas_call(body, name=name, out_shape=[shp, shp, shp], compiler_params=_params())(g, w, m, v)


SMALL = ["norm1_g", "q_norm_g", "k_norm_g", "sink", "sgu_ln_g", "sgu_ln_b", "w_s", "b_s",
         "attn_out_g", "sgu_out_g", "norm2_g", "conv_b"]
PACK_ALIGN = 1024


def _pack(pieces):
    flat = []
    for p in pieces:
        f = p.reshape(-1).astype(F32)
        pad = (-f.shape[0]) % PACK_ALIGN
        flat.append(jnp.pad(f, (0, pad)) if pad else f)
    return jnp.concatenate(flat).reshape(-1, 128)


def kernel(x, norm1_g, w_in, q_norm_g, k_norm_g, sink, sgu_ln_g, sgu_ln_b, w_s, b_s, attn_out_g, sgu_out_g, w_o, norm2_g, w_up, conv_w, conv_b, w_down, loss_target, m_norm1_g, m_w_in, m_q_norm_g, m_k_norm_g, m_sink, m_sgu_ln_g, m_sgu_ln_b, m_w_s, m_b_s, m_attn_out_g, m_sgu_out_g, m_w_o, m_norm2_g, m_w_up, m_conv_w, m_conv_b, m_w_down, v_norm1_g, v_w_in, v_q_norm_g, v_k_norm_g, v_sink, v_sgu_ln_g, v_sgu_ln_b, v_w_s, v_b_s, v_attn_out_g, v_sgu_out_g, v_w_o, v_norm2_g, v_w_up, v_conv_w, v_conv_b, v_w_down):
    weights = dict(norm1_g=norm1_g, w_in=w_in, q_norm_g=q_norm_g, k_norm_g=k_norm_g, sink=sink, sgu_ln_g=sgu_ln_g,
                   sgu_ln_b=sgu_ln_b, w_s=w_s, b_s=b_s, attn_out_g=attn_out_g, sgu_out_g=sgu_out_g, w_o=w_o,
                   norm2_g=norm2_g, w_up=w_up, conv_w=conv_w, conv_b=conv_b, w_down=w_down)
    mom_m = dict(norm1_g=m_norm1_g, w_in=m_w_in, q_norm_g=m_q_norm_g, k_norm_g=m_k_norm_g, sink=m_sink,
                 sgu_ln_g=m_sgu_ln_g, sgu_ln_b=m_sgu_ln_b, w_s=m_w_s, b_s=m_b_s, attn_out_g=m_attn_out_g,
                 sgu_out_g=m_sgu_out_g, w_o=m_w_o, norm2_g=m_norm2_g, w_up=m_w_up, conv_w=m_conv_w,
                 conv_b=m_conv_b, w_down=m_w_down)
    mom_v = dict(norm1_g=v_norm1_g, w_in=v_w_in, q_norm_g=v_q_norm_g, k_norm_g=v_k_norm_g, sink=v_sink,
                 sgu_ln_g=v_sgu_ln_g, sgu_ln_b=v_sgu_ln_b, w_s=v_w_s, b_s=v_b_s, attn_out_g=v_attn_out_g,
                 sgu_out_g=v_sgu_out_g, w_o=v_w_o, norm2_g=v_norm2_g, w_up=v_w_up, conv_w=v_conv_w,
                 conv_b=v_conv_b, w_down=v_w_down)
    order = ["norm1_g", "w_in", "q_norm_g", "k_norm_g", "sink", "sgu_ln_g", "sgu_ln_b", "w_s", "b_s",
             "attn_out_g", "sgu_out_g", "w_o", "norm2_g", "w_up", "conv_w", "conv_b", "w_down"]

    _, S, D = x.shape
    L = w_in.shape[0]
    AW = D // 2
    NQ = AW // HEAD
    NKV = max(1, NQ // 4)
    G = NQ // NKV
    KVW = NKV * HEAD
    GW = D - AW
    NG = GW // HEAD
    IN = AW + 2 * KVW + 2 * GW
    INS = w_in.shape[2]
    OS = w_o.shape[1]
    US = w_up.shape[2]
    DS = w_down.shape[1]
    F2 = US * N_DEV
    F = F2 // 2
    assert INS * N_DEV == IN and OS * N_DEV == D and DS * N_DEV == F and AW == GW and (3 * AW) % (2 * KVW) == 0
    cfg = dict(S=S, D=D, AW=AW, NQ=NQ, NKV=NKV, G=G, KVW=KVW, GW=GW, NG=NG, F=F, F2=F2)

    def row_windows(n):
        m = n // 2
        return (lambda ref, idx: ref.at[:, pl.ds(pl.multiple_of(idx * n, n), n), :],
                lambda ref, idx, h: ref.at[:, pl.ds(pl.multiple_of(idx * n + h * m, m), m), :],
                lambda ref, h: ref.at[:, pl.ds(h * m, m), :])

    def col_windows(n, rows):
        m = rows // 2
        return (lambda ref, idx: ref.at[:, :, pl.ds(pl.multiple_of(idx * n, n), n)],
                lambda ref, idx, h: ref.at[:, pl.ds(h * m, m), pl.ds(pl.multiple_of(idx * n, n), n)],
                lambda ref, h: ref.at[:, pl.ds(h * m, m), :])

    def lead_col_windows(n, lead):
        m = lead // 2
        return (lambda ref, idx: ref.at[:, :, pl.ds(pl.multiple_of(idx * n, n), n)],
                lambda ref, idx, h: ref.at[pl.ds(h * m, m), :, pl.ds(pl.multiple_of(idx * n, n), n)],
                lambda ref, h: ref.at[pl.ds(h * m, m)])

    def after(first, then):
        return lax.optimization_barrier((first, then))

    w_in_t, m_w_in_t, v_w_in_t = (jnp.swapaxes(a, 1, 2) for a in (w_in, m_w_in, v_w_in))

    ids = iter(range(64))
    w_in_f, w_o_f, w_up_f, w_down_f = [], [], [], []

    def gather_layer(l, not_before=None):
        in_shard = w_in_t[l:l + 1].astype(BF16)
        if not_before is not None:
            _, in_shard = after(not_before, in_shard)
        w_in_f.append(_all_gather("ag_in", next(ids), in_shard,
                                  jax.ShapeDtypeStruct((1, IN, D), BF16), row_windows(INS)))
        w_o_f.append(_all_gather("ag_o", next(ids), w_o[l:l + 1].astype(BF16),
                                 jax.ShapeDtypeStruct((1, D, D), BF16), row_windows(OS)))
        w_up_f.append(_all_gather("ag_up", next(ids), w_up[l:l + 1].astype(BF16),
                                  jax.ShapeDtypeStruct((1, D, F2), BF16), col_windows(US, D)))
        w_down_f.append(_all_gather("ag_down", next(ids), w_down[l:l + 1].astype(BF16),
                                    jax.ShapeDtypeStruct((1, F, D), BF16), row_windows(DS)))

    assert L % 2 == 0 and INS % 32 == 0 and OS % 32 == 0 and DS % 32 == 0 and D % 32 == 0
    gather_layer(0)
    conv_w_f = _all_gather("ag_conv_w", next(ids), conv_w, jax.ShapeDtypeStruct((L, 3, F2), F32),
                           lead_col_windows(US, L))

    n1g3, n2g3 = norm1_g.reshape(L, 1, D), norm2_g.reshape(L, 1, D)
    qg3, kg3 = q_norm_g.reshape(L, 1, HEAD), k_norm_g.reshape(L, 1, HEAD)
    lng3, lnb3 = sgu_ln_g.reshape(L, 1, GW), sgu_ln_b.reshape(L, 1, GW)
    ag3, og3 = attn_out_g.reshape(L, 1, AW), sgu_out_g.reshape(L, 1, GW)
    cb3 = conv_b.reshape(L, 1, F2)
    ws_b = w_s.astype(BF16)
    wst_b = jnp.swapaxes(w_s, 2, 3).astype(BF16)
    bs_b = jnp.broadcast_to(b_s[..., None], (L, NG, HEAD, HEAD))
    inv_freq = ROPE_THETA ** (-jnp.arange(0, HEAD, 2, dtype=F32) / HEAD)
    ang = jnp.arange(S, dtype=F32)[:, None] * inv_freq[None, :]
    cos2 = jnp.concatenate([jnp.cos(ang), jnp.cos(ang)], axis=1)
    sin2 = jnp.concatenate([-jnp.sin(ang), jnp.sin(ang)], axis=1)

    tn = 512
    t_in, t_d, t_f, t_f2 = _tile(IN, tn, 128), _tile(D, tn, 128), _tile(F, tn, 128), _tile(F2, tn, 128)
    tm_f = _tile(F, 512, 128)
    tm_s = _tile(S, 1024, 128)
    assert AW % t_in == 0 and (2 * KVW) % t_in == 0
    n_q, n_kv, n_g = AW // t_in, (2 * KVW) // t_in, (2 * GW) // t_in

    def in_tile(j):
        return jnp.where(j < n_q, j, jnp.where(j < n_q + n_g, j + n_kv, j - n_g))

    def w_spec(tk, tn_):
        return pl.BlockSpec((None, tk, tn_), lambda i, j, k: (0, k, j))

    def wt_spec(tn_, tk):
        return pl.BlockSpec((None, tn_, tk), lambda i, j, k: (0, j, k))

    def a_spec(tm, tk):
        return pl.BlockSpec((tm, tk), lambda i, j, k: (i, k))

    def at_spec(tk, tm):
        return pl.BlockSpec((tk, tm), lambda i, j, k: (k, i))

    def b_spec(tk, tn_):
        return pl.BlockSpec((tk, tn_), lambda i, j, k: (k, j))

    xs = x.reshape(S, D)
    saved = []
    cur = xs
    for l in range(L):
        h = _rms_fwd("rms1_fwd", cur, n1g3, l)
        z = _mm("mm_in", h, w_in_f[l], M=S, N=IN, K=D, tm=S, tn=t_in, tk=D, tb=True,
                a_spec=a_spec(S, D),
                b_spec=pl.BlockSpec((None, t_in, D), lambda i, j, k: (0, in_tile(j), 0)))
        q_r, k_r, v_b = _qkv_prep("qkv_prep", z, qg3, kg3, cos2, sin2, l, cfg)
        attn, mix_l = _attn_fwd("attn_fwd", q_r, k_r, v_b, sink, ag3, l, cfg)
        mix_r = _sgu_fwd("sgu_fwd", z, lng3, lnb3, ws_b, bs_b, og3, l, cfg)
        mixed = jnp.concatenate([mix_l, mix_r], axis=1)
        x1 = _mm("mm_o", mixed, w_o_f[l], M=S, N=D, K=D, tm=S, tn=t_d, tk=D,
                 a_spec=a_spec(S, D), b_spec=w_spec(D, t_d), res=cur)
        if l + 1 < L:
            gather_layer(l + 1, not_before=x1)
        h2 = _rms_fwd("rms2_fwd", x1, n2g3, l)
        ap = _mm("mm_up", h2, w_up_f[l], M=S, N=F2, K=D, tm=S, tn=t_f2, tk=D,
                 a_spec=a_spec(S, D), b_spec=w_spec(D, t_f2))
        y_b = _conv_glu_fwd("conv_glu_fwd", ap, conv_w_f, cb3, l, cfg)
        x2 = _mm("mm_down", y_b, w_down_f[l], M=S, N=D, K=F, tm=tm_s, tn=t_d, tk=F,
                 a_spec=a_spec(tm_s, F), b_spec=w_spec(F, t_d), res=x1)
        saved.append(dict(x=cur, h=h, z=z, q=q_r, k=k_r, v=v_b, attn=attn, mixed=mixed, x1=x1, h2=h2, ap=ap, y=y_b))
        cur = x2

    loss_tile, dx, dxb = _loss_bwd("loss", cur, loss_target.reshape(S, D))
    loss = lax.psum(loss_tile[0, 0], ("x", "y", "c"))

    gS = [dict() for _ in range(L)]
    recv = [dict() for _ in range(L)]
    cvec = jnp.reshape(lax.axis_index("c"), (1,)).astype(jnp.int32)

    def rs_swap(name, g, R, C, small=None):
        g4 = g.reshape(N_DEV // 2, 2, R, C)
        got = _rs_d2d("rs_d2d_" + name, next(ids), [g4], [] if small is None else [small])
        return name, g4, got, small

    started = []

    def rs_finish(l, pending, then=None):
        name, g4, got, small = pending
        chip = _pair_sum("pair_sum_" + name, g4, got[0], cvec, 16)
        chip_small = [] if small is None else [_add2("pair_sum_small", small, got[1])]
        begun = _ici_start("rs_ici_" + name, [chip], chip_small)
        if then is not None:
            token, then = after(begun[4], then)
            begun = begun[:4] + (token,)
        started.append((l, name, begun))
        return then

    carried = None
    for l in reversed(range(L)):
        sv = saved[l]
        dy = _mm("mm_dy", dxb, w_down_f[l], M=S, N=F, K=D, tm=S, tn=t_f, tk=D, tb=True,
                 a_spec=a_spec(S, D), b_spec=wt_spec(t_f, D))
        if carried is not None:
            dy = rs_finish(l + 1, carried, dy)
        g_down = _mm("mm_gdown", sv["y"], dxb, M=F, N=D, K=S, tm=tm_f, tn=D, tk=S, ta=True,
                     a_spec=at_spec(S, tm_f), b_spec=b_spec(S, D), out_dtype=BF16)
        g_down, dy = after(g_down, dy)
        swap_down = rs_swap("w_down", g_down, DS, D)
        dap3, dcw, dcb = _glu_conv_bwd("glu_conv_bwd", dy, sv["ap"], conv_w_f, cb3, l, cfg)
        gS[l]["conv_w"] = jnp.concatenate([dcw[0], dcw[1]], axis=1)
        gS[l]["conv_b"] = jnp.concatenate([dcb[0], dcb[1]], axis=1).reshape(F2)
        dap3 = rs_finish(l, swap_down, dap3)
        dh2 = _mm("mm_dh2", dap3, w_up_f[l], M=S, N=D, K=F2, tm=tm_s, tn=t_d, tk=F, tb=True,
                  a_spec=pl.BlockSpec((None, tm_s, F), lambda i, j, k: (k, i, 0)),
                  b_spec=wt_spec(t_d, F))
        g_up = _mm("mm_gup", sv["h2"], dap3, M=D, N=F2, K=S, tm=D, tn=US, tk=S, ta=True,
                   a_spec=at_spec(S, D),
                   b_spec=pl.BlockSpec((None, S, US), lambda i, j, k: (j // (N_DEV // 2), 0, j % (N_DEV // 2))),
                   out_dtype=BF16, out_shape=(N_DEV, D, US),
                   out_spec=pl.BlockSpec((None, D, US), lambda i, j, k: (j, 0, 0)))
        g_up, dh2 = after(g_up, dh2)
        swap_up = rs_swap("w_up", g_up, D, US)
        dx1, dx1b, dg2 = _rms_bwd("rms2_bwd", sv["x1"], dh2, dx, n2g3, l)
        gS[l]["norm2_g"] = dg2.reshape(D)
        dmix = _mm("mm_dmix", dx1b, w_o_f[l], M=S, N=D, K=D, tm=S, tn=t_d, tk=D, tb=True,
                   a_spec=a_spec(S, D), b_spec=wt_spec(t_d, D))
        g_o = _mm("mm_go", sv["mixed"], dx1b, M=D, N=D, K=S, tm=D, tn=t_d, tk=S, ta=True,
                  a_spec=at_spec(S, D), b_spec=b_spec(S, t_d), out_dtype=BF16)
        g_o, dmix = after(g_o, dmix)
        swap_o = rs_swap("w_o", g_o, OS, D)
        dmix = rs_finish(l, swap_up, dmix)
        dq_r, dk_pad, dv_pad, dsink, dag = _attn_bwd("attn_bwd", sv["q"], sv["k"], sv["v"], sv["attn"], dmix,
                                                     sink, ag3, l, cfg)
        dq_r = rs_finish(l, swap_o, dq_r)
        gS[l]["sink"] = dsink[:, 0]
        gS[l]["attn_out_g"] = dag.reshape(AW)
        dzgu, dzgv, dws, dbs, dlng, dlnb, dog = _sgu_bwd("sgu_bwd", sv["z"], dmix, lng3, lnb3, ws_b, wst_b, bs_b,
                                                        og3, l, cfg)
        gS[l]["w_s"] = dws
        gS[l]["b_s"] = dbs[:, :, 0]
        gS[l]["sgu_ln_g"] = dlng.reshape(GW)
        gS[l]["sgu_ln_b"] = dlnb.reshape(GW)
        gS[l]["sgu_out_g"] = dog.reshape(GW)
        dzq, dzkv, dqg, dkg = _qkv_prep_bwd("qkv_prep_bwd", sv["z"], dq_r, dk_pad, dv_pad, qg3, kg3, cos2, sin2,
                                            l, cfg)
        gS[l]["q_norm_g"] = dqg.reshape(HEAD)
        gS[l]["k_norm_g"] = dkg.reshape(HEAD)
        dz = jnp.concatenate([dzq, dzkv, dzgu, dzgv], axis=1)
        dh = _mm("mm_dh", dz, w_in_f[l], M=S, N=D, K=IN, tm=tm_s, tn=t_d, tk=IN,
                 a_spec=a_spec(tm_s, IN), b_spec=w_spec(IN, t_d))
        g_in = _mm("mm_gin", dz, sv["h"], M=IN, N=D, K=S, tm=t_in, tn=D, tk=S, ta=True,
                   a_spec=at_spec(S, t_in), b_spec=b_spec(S, D), out_dtype=BF16)
        g_in, dh = after(g_in, dh)
        dx, dxb, dg1 = _rms_bwd("rms1_bwd", sv["x"], dh, dx1, n1g3, l)
        gS[l]["norm1_g"] = dg1.reshape(D)
        carried = rs_swap("w_in", g_in, INS, D, small=_pack([gS[l][n] for n in SMALL] + [gS[l]["conv_w"]]))
    dx = rs_finish(0, carried, dx)
    grad_x = dx.reshape(1, S, D)

    grads, deltas, new_m, new_v = {}, {}, {}, {}
    results, placed = {}, dx
    qvec = jnp.reshape(2 * lax.axis_index("x") + lax.axis_index("y"), (1,)).astype(jnp.int32)
    for l, name, begun in started:
        sent, landed = _ici_wait("rs_wait_" + name, begun, 1, placed)
        recv[l][name] = (sent, landed)
        wmv = (w_in_t, m_w_in_t, v_w_in_t) if name == "w_in" else (weights[name], mom_m[name], mom_v[name])
        results[name] = _adamw_sum("adamw_" + name, landed[0], sent[0], qvec, *wmv, l, results.get(name), 16)
        placed = results[name][0]
    results["w_in"] = [jnp.swapaxes(r, 1, 2) for r in results["w_in"]]
    for name, res in results.items():
        grads[name], deltas[name], new_m[name], new_v[name] = res

    def as_rows(a):
        flat = a.reshape(L, -1)
        short = (-flat.shape[1]) % 128
        if short:
            flat = jnp.pad(flat, ((0, 0), (0, short)))
        return flat.reshape(L, -1, 128)

    sizes = [weights[n][0].size for n in SMALL] + [3 * F2]
    offsets, off = [], 0
    for sz in sizes:
        offsets.append(off // 128)
        off += sz + (-sz) % PACK_ALIGN
    n_small = len(SMALL)
    rows_of = [-(-sz // 128) for sz in sizes]

    def small_body(*refs):
        q_ref = refs[0]
        lands, owns = refs[1:1 + L], refs[1 + L:1 + 2 * L]
        wmv_refs = refs[1 + 2 * L:1 + 2 * L + 3 * n_small]
        outs = refs[1 + 2 * L + 3 * n_small:]
        for l in range(L):
            for p in range(n_small + 1):
                rows = slice(offsets[p], offsets[p] + rows_of[p])
                terms = [jnp.where(q_ref[0] == s, owns[l][rows, :], lands[l][s, rows, :]) for s in range(N_DEV // 2)]
                g = terms[0]
                for t in terms[1:]:
                    g = g + t
                if p == n_small:
                    outs[4 * n_small][l] = g
                    continue
                d, m2, v2 = _adamw_math(wmv_refs[3 * p][l], g, wmv_refs[3 * p + 1][l], wmv_refs[3 * p + 2][l])
                for t, val in enumerate([g, d, m2, v2]):
                    outs[4 * p + t][l] = val

    vmem = pl.BlockSpec(memory_space=pltpu.VMEM)
    wmv_args = []
    for n in SMALL:
        wmv_args += [as_rows(weights[n]), as_rows(mom_m[n]), as_rows(mom_v[n])]
    small_out = pl.pallas_call(
        small_body, name="adamw_small",
        in_specs=[pl.BlockSpec(memory_space=pltpu.SMEM)] + [vmem] * (2 * L + 3 * n_small),
        out_specs=[vmem] * (4 * n_small + 1),
        out_shape=[jax.ShapeDtypeStruct((L, rows_of[p], 128), F32) for p in range(n_small) for _ in range(4)]
                  + [jax.ShapeDtypeStruct((L, rows_of[n_small], 128), F32)],
        compiler_params=_params(),
    )(qvec, *[recv[l]["w_in"][1][1] for l in range(L)], *[recv[l]["w_in"][0][1] for l in range(L)], *wmv_args)
    for p, n in enumerate(SMALL):
        for t, store in enumerate([grads, deltas, new_m, new_v]):
            store[n] = small_out[4 * p + t].reshape(L, -1)[:, :sizes[p]].reshape(weights[n].shape)
    conv_full = small_out[4 * n_small].reshape(L, 3, F2)
    me_i = _dev_index(_mesh_pos())
    g_cw = lax.dynamic_slice_in_dim(conv_full, me_i * US, US, axis=2)
    grads["conv_w"] = g_cw
    d_cw, m_cw, v_cw = _adamw_plain("adamw_conv_w", g_cw.reshape(L * 3, US), conv_w.reshape(L * 3, US),
                                    m_conv_w.reshape(L * 3, US), v_conv_w.reshape(L * 3, US))
    deltas["conv_w"] = d_cw.reshape(L, 3, US)
    new_m["conv_w"] = m_cw.reshape(L, 3, US)
    new_v["conv_w"] = v_cw.reshape(L, 3, US)

    return (loss, grad_x, *[grads[n] for n in order], *[deltas[n] for n in order],
            *[new_m[n] for n in order], *[new_v[n] for n in order])
```

```python
import jax
import jax.numpy as jnp
from jax import lax
from jax.experimental import pallas as pl
from jax.experimental.pallas import tpu as pltpu
from jax.experimental.pallas import tpu_sc as plsc

F32 = jnp.float32
BF16 = jnp.bfloat16
MESH = pl.DeviceIdType.MESH

N_DEV = 8
HEAD = 128
EPS = 1e-6
MASK_VALUE = -1e30
ROPE_THETA = 10000.0
GELU_C = 0.7978845608028654
GELU_A = 0.044715

ADAM_LR = 0.001
ADAM_B1 = 0.9
ADAM_B2 = 0.999
ADAM_EPS = 1e-08
ADAM_WD = 0.01
ADAM_STEP = 10

VMEM_LIMIT = 56 * 1024 * 1024


def _tile(n, pref, mult):
    best = None
    for t in range(mult, min(n, pref) + 1, mult):
        if n % t == 0:
            best = t
    return n if best is None else best


def _params(sem=None):
    kw = dict(vmem_limit_bytes=VMEM_LIMIT)
    if sem is not None:
        kw["dimension_semantics"] = sem
    return pltpu.CompilerParams(**kw)


def _gelu(x):
    return x * (0.5 * (1.0 + jnp.tanh(GELU_C * (x + GELU_A * (x * x * x)))))


def _gelu_grad(x):
    t = jnp.tanh(GELU_C * (x + GELU_A * (x * x * x)))
    return 0.5 * (1.0 + t) + 0.5 * x * (1.0 - t * t) * (GELU_C * (1.0 + 3.0 * GELU_A * (x * x)))


def _sigmoid(x):
    return 1.0 / (1.0 + jnp.exp(-x))


def _vec_spec(l, n):
    return pl.BlockSpec((None, 1, n), lambda *_: (l, 0, 0))


def _mm(name, a, b, *, M, N, K, tm, tn, tk, a_spec, b_spec, ta=False, tb=False, out_dtype=F32, res=None,
        out_shape=None, out_spec=None):
    nm, nn, nk = M // tm, N // tn, K // tk
    assert nm * tm == M and nn * tn == N and nk * tk == K
    assert not (ta and nk > 1)
    dims = (((1,), (1,)), ((), ())) if tb else (((1,), (0,)), ((), ()))

    def body(*refs):
        refs = list(refs)
        a_ref = refs.pop(0)
        b_ref = refs.pop(0)
        r_ref = refs.pop(0) if res is not None else None
        o_ref = refs.pop(0)
        acc = refs.pop(0) if nk > 1 else None
        at = refs.pop(0) if ta else None
        k = pl.program_id(2)
        if ta:
            @pl.when(pl.program_id(1) == 0)
            def _():
                at[...] = a_ref[...].T
            lhs = at[...]
        else:
            lhs = a_ref[...]
        p = lax.dot_general(lhs, b_ref[...], dims, preferred_element_type=F32)

        def finish(r):
            if r_ref is not None:
                r = r_ref[...] + r
            o_ref[...] = r.astype(out_dtype)

        if nk == 1:
            finish(p)
        else:
            @pl.when(k == 0)
            def _():
                acc[...] = p

            @pl.when(k > 0)
            def _():
                acc[...] += p

            @pl.when(k == nk - 1)
            def _():
                finish(acc[...])

    in_specs = [a_spec, b_spec]
    args = [a, b]
    if res is not None:
        in_specs.append(pl.BlockSpec((tm, tn), lambda i, j, k: (i, j)))
        args.append(res)
    scratch = []
    if nk > 1:
        scratch.append(pltpu.VMEM((tm, tn), F32))
    if ta:
        scratch.append(pltpu.VMEM((tm, tk), BF16))
    return pl.pallas_call(
        body, name=name, grid=(nm, nn, nk),
        in_specs=in_specs,
        out_specs=pl.BlockSpec((tm, tn), lambda i, j, k: (i, j)) if out_spec is None else out_spec,
        out_shape=jax.ShapeDtypeStruct((M, N) if out_shape is None else out_shape, out_dtype),
        scratch_shapes=scratch,
        compiler_params=_params(("arbitrary", "arbitrary", "arbitrary")),
    )(*args)


def _rms_fwd(name, x, g3, l):
    S, D = x.shape
    tr = _tile(S, 256, 16)

    def body(x_ref, g_ref, h_ref):
        xv = x_ref[...]
        r = lax.rsqrt(jnp.mean(xv * xv, axis=-1, keepdims=True) + EPS)
        h_ref[...] = ((xv * r) * g_ref[...]).astype(BF16)

    return pl.pallas_call(
        body, name=name, grid=(S // tr,),
        in_specs=[pl.BlockSpec((tr, D), lambda i: (i, 0)), _vec_spec(l, D)],
        out_specs=pl.BlockSpec((tr, D), lambda i: (i, 0)),
        out_shape=jax.ShapeDtypeStruct((S, D), BF16),
        compiler_params=_params(("arbitrary",)),
    )(x, g3)


def _rope(t, cos2, sin2):
    return t * cos2 + pltpu.roll(t, HEAD // 2, axis=1) * sin2


def _qkv_prep(name, z, qg3, kg3, cos2, sin2, l, cfg):
    S, AW, KVW, NQ, NKV = cfg["S"], cfg["AW"], cfg["KVW"], cfg["NQ"], cfg["NKV"]
    tr = _tile(S, 256, 16)
    kv_blk = (3 * AW) // (2 * KVW)

    def body(zq_ref, zkv_ref, qg_ref, kg_ref, c_ref, s_ref, q_ref, k_ref, v_ref):
        cosv, sinv = c_ref[...], s_ref[...]

        def norm_rope(t, g):
            r = lax.rsqrt(jnp.mean(t * t, axis=-1, keepdims=True) + EPS)
            return _rope((t * r) * g, cosv, sinv)

        for h in range(NQ):
            sl = slice(h * HEAD, (h + 1) * HEAD)
            q_ref[:, sl] = norm_rope(zq_ref[:, sl], qg_ref[...]).astype(BF16)
        for h in range(NKV):
            sl = slice(h * HEAD, (h + 1) * HEAD)
            k_ref[:, sl] = norm_rope(zkv_ref[:, sl], kg_ref[...]).astype(BF16)
        v_ref[...] = zkv_ref[:, KVW:].astype(BF16)

    return pl.pallas_call(
        body, name=name, grid=(S // tr,),
        in_specs=[pl.BlockSpec((tr, AW), lambda i: (i, 0)),
                  pl.BlockSpec((tr, 2 * KVW), lambda i: (i, kv_blk)),
                  _vec_spec(l, HEAD), _vec_spec(l, HEAD),
                  pl.BlockSpec((tr, HEAD), lambda i: (i, 0)),
                  pl.BlockSpec((tr, HEAD), lambda i: (i, 0))],
        out_specs=[pl.BlockSpec((tr, AW), lambda i: (i, 0)),
                   pl.BlockSpec((tr, KVW), lambda i: (i, 0)),
                   pl.BlockSpec((tr, KVW), lambda i: (i, 0))],
        out_shape=[jax.ShapeDtypeStruct((S, AW), BF16),
                   jax.ShapeDtypeStruct((S, KVW), BF16),
                   jax.ShapeDtypeStruct((S, KVW), BF16)],
        compiler_params=_params(("arbitrary",)),
    )(z, z, qg3, kg3, cos2, sin2)


def _band_specs(width, nb):
    return [pl.BlockSpec((HEAD, width), lambda n: (jnp.maximum(n - 1, 0), 0)),
            pl.BlockSpec((HEAD, width), lambda n: (n, 0)),
            pl.BlockSpec((HEAD, width), lambda n: (jnp.minimum(n + 1, nb - 1), 0))]


def _attn_probs(qs, kj, n, sink_of_row, S, G):
    s = lax.dot_general(qs, kj, (((1,), (1,)), ((), ())), preferred_element_type=F32) * (HEAD ** -0.5)
    rows = lax.broadcasted_iota(jnp.int32, (G * HEAD, 3 * HEAD), 0)
    cols = lax.broadcasted_iota(jnp.int32, (G * HEAD, 3 * HEAD), 1)
    qi = rows & (HEAD - 1)
    kpos = n * HEAD - HEAD + cols
    valid = (cols >= qi) & (cols <= qi + 2 * HEAD) & (kpos >= 0) & (kpos < S)
    s = jnp.where(valid, s, MASK_VALUE)
    m = jnp.maximum(jnp.max(s, axis=-1, keepdims=True), sink_of_row)
    p = jnp.exp(s - m)
    e_sink = jnp.exp(sink_of_row - m)
    inv = 1.0 / (jnp.sum(p, axis=-1, keepdims=True) + e_sink)
    return p * inv, e_sink * inv


def _sink_rows(sink_ref, l, j, G):
    hidx = lax.broadcasted_iota(jnp.int32, (G * HEAD, 1), 0) // HEAD
    col = jnp.full((G * HEAD, 1), sink_ref[l, j * G], F32)
    for g in range(1, G):
        col = jnp.where(hidx == g, sink_ref[l, j * G + g], col)
    return col


def _attn_fwd(name, q, k, v, sink, ag3, l, cfg):
    S, AW, KVW, NKV, G = cfg["S"], cfg["AW"], cfg["KVW"], cfg["NKV"], cfg["G"]
    nb = S // HEAD

    def body(q_ref, kp, kc, kn, vp, vc, vn, sink_ref, ag_ref, a_ref, mix_ref):
        n = pl.program_id(0)
        kb = jnp.concatenate([kp[...], kc[...], kn[...]], axis=0)
        vb = jnp.concatenate([vp[...], vc[...], vn[...]], axis=0)
        for j in range(NKV):
            sl = slice(j * HEAD, (j + 1) * HEAD)
            qs = jnp.concatenate([q_ref[:, (j * G + g) * HEAD:(j * G + g + 1) * HEAD] for g in range(G)], axis=0)
            probs, _ = _attn_probs(qs, kb[:, sl], n, _sink_rows(sink_ref, l, j, G), S, G)
            o = jnp.dot(probs.astype(BF16), vb[:, sl], preferred_element_type=F32)
            for g in range(G):
                a_ref[:, (j * G + g) * HEAD:(j * G + g + 1) * HEAD] = o[g * HEAD:(g + 1) * HEAD]
        a = a_ref[...]
        r = lax.rsqrt(jnp.mean(a * a, axis=-1, keepdims=True) + EPS)
        mix_ref[...] = ((a * r) * ag_ref[...]).astype(BF16)

    return pl.pallas_call(
        body, name=name, grid=(nb,),
        in_specs=[pl.BlockSpec((HEAD, AW), lambda n: (n, 0))] + _band_specs(KVW, nb) + _band_specs(KVW, nb)
                 + [pl.BlockSpec(memory_space=pltpu.SMEM), _vec_spec(l, AW)],
        out_specs=[pl.BlockSpec((HEAD, AW), lambda n: (n, 0)), pl.BlockSpec((HEAD, AW), lambda n: (n, 0))],
        out_shape=[jax.ShapeDtypeStruct((S, AW), F32), jax.ShapeDtypeStruct((S, AW), BF16)],
        compiler_params=_params(("arbitrary",)),
    )(q, k, k, k, v, v, v, sink, ag3)


def _sgu_forward_math(gu, gv, lng, lnb):
    u = _gelu(gu)
    vv = _gelu(gv)
    mu = jnp.mean(vv, axis=-1, keepdims=True)
    xc = vv - mu
    rstd = lax.rsqrt(jnp.mean(xc * xc, axis=-1, keepdims=True) + EPS)
    xhat = xc * rstd
    vn = xhat * lng + lnb
    return u, xhat, rstd, vn


def _sgu_fwd(name, z, lng3, lnb3, ws_b, bs_b, og3, l, cfg):
    S, GW, NG = cfg["S"], cfg["GW"], cfg["NG"]

    def body(gu_ref, gv_ref, lng_ref, lnb_ref, ws_ref, bs_ref, og_ref, mix_ref, sg_ref):
        u, _, _, vn = _sgu_forward_math(gu_ref[...], gv_ref[...], lng_ref[...], lnb_ref[...])
        vnb = vn.astype(BF16)
        for h in range(NG):
            sl = slice(h * HEAD, (h + 1) * HEAD)
            f = jnp.dot(ws_ref[h], vnb[:, sl], preferred_element_type=F32) + bs_ref[h]
            sg_ref[:, sl] = u[:, sl] * f
        sg = sg_ref[...]
        r = lax.rsqrt(jnp.mean(sg * sg, axis=-1, keepdims=True) + EPS)
        mix_ref[...] = ((sg * r) * og_ref[...]).astype(BF16)

    return pl.pallas_call(
        body, name=name, grid=(S // HEAD,),
        in_specs=[pl.BlockSpec((HEAD, GW), lambda c: (c, 1)),
                  pl.BlockSpec((HEAD, GW), lambda c: (c, 2)),
                  _vec_spec(l, GW), _vec_spec(l, GW),
                  pl.BlockSpec((None, NG, HEAD, HEAD), lambda c: (l, 0, 0, 0)),
                  pl.BlockSpec((None, NG, HEAD, HEAD), lambda c: (l, 0, 0, 0)),
                  _vec_spec(l, GW)],
        out_specs=pl.BlockSpec((HEAD, GW), lambda c: (c, 0)),
        out_shape=jax.ShapeDtypeStruct((S, GW), BF16),
        scratch_shapes=[pltpu.VMEM((HEAD, GW), F32)],
        compiler_params=_params(("arbitrary",)),
    )(z, z, lng3, lnb3, ws_b, bs_b, og3)


CONV_HALO = 8
CONV_ROWS = 128


def _for_row_windows(S, fn):
    R, W = CONV_ROWS, CONV_ROWS + 2 * CONV_HALO
    n = S // R
    assert n * R == S and n >= 2
    fn(0, 0, 0)
    if n > 2:
        def mid(k, carry):
            fn(pl.multiple_of(k * R - CONV_HALO, CONV_HALO), CONV_HALO, pl.multiple_of(k * R, R))
            return carry

        lax.fori_loop(1, n - 1, mid, 0)
    fn(S - W, 2 * CONV_HALO, S - R)


def _shifts(t):
    W = t.shape[0]
    row = lax.broadcasted_iota(jnp.int32, t.shape, 0)
    dn = jnp.where(row == 0, 0.0, pltpu.roll(t, 1, axis=0))
    up = jnp.where(row == W - 1, 0.0, pltpu.roll(t, W - 1, axis=0))
    return dn, up


def _conv3(t, w, b):
    dn, up = _shifts(t)
    return ((b + dn * w[0:1]) + t * w[1:2]) + up * w[2:3], dn, up


def _conv_glu_fwd(name, ap, cw, cb3, l, cfg):
    S, F = cfg["S"], cfg["F"]
    tc = _tile(F, 256, 128)
    nf = F // tc
    R, W = CONV_ROWS, CONV_ROWS + 2 * CONV_HALO

    def body(g_ref, u_ref, wg_ref, wu_ref, bg_ref, bu_ref, y_ref):
        wg, wu, bg, bu = wg_ref[...], wu_ref[...], bg_ref[...], bu_ref[...]

        def window(start, lo, out0):
            rows = pl.ds(start, W)
            ag, _, _ = _conv3(g_ref[rows, :], wg, bg)
            au, _, _ = _conv3(u_ref[rows, :], wu, bu)
            y = (ag * _sigmoid(ag)) * au
            y_ref[pl.ds(out0, R), :] = y[lo:lo + R].astype(BF16)

        _for_row_windows(S, window)

    return pl.pallas_call(
        body, name=name, grid=(nf,),
        in_specs=[pl.BlockSpec((S, tc), lambda j: (0, j)),
                  pl.BlockSpec((S, tc), lambda j: (0, j + nf)),
                  pl.BlockSpec((None, 3, tc), lambda j: (l, 0, j)),
                  pl.BlockSpec((None, 3, tc), lambda j: (l, 0, j + nf)),
                  pl.BlockSpec((None, 1, tc), lambda j: (l, 0, j)),
                  pl.BlockSpec((None, 1, tc), lambda j: (l, 0, j + nf))],
        out_specs=pl.BlockSpec((S, tc), lambda j: (0, j)),
        out_shape=jax.ShapeDtypeStruct((S, F), BF16),
        compiler_params=_params(("arbitrary",)),
    )(ap, ap, cw, cw, cb3, cb3)


def _loss_bwd(name, y, target):
    S, D = y.shape
    tr = _tile(S, 256, 16)

    def body(y_ref, t_ref, loss_ref, d_ref, db_ref):
        @pl.when(pl.program_id(0) == 0)
        def _():
            loss_ref[...] = jnp.zeros_like(loss_ref)

        err = y_ref[...] - t_ref[...]
        part = 0.5 * jnp.sum(jnp.mean(err * err, axis=-1, keepdims=True), axis=0, keepdims=True)
        loss_ref[...] += jnp.broadcast_to(part, loss_ref.shape)
        d = err * (1.0 / D)
        d_ref[...] = d
        db_ref[...] = d.astype(BF16)

    return pl.pallas_call(
        body, name=name, grid=(S // tr,),
        in_specs=[pl.BlockSpec((tr, D), lambda i: (i, 0)), pl.BlockSpec((tr, D), lambda i: (i, 0))],
        out_specs=[pl.BlockSpec((8, 128), lambda i: (0, 0)),
                   pl.BlockSpec((tr, D), lambda i: (i, 0)),
                   pl.BlockSpec((tr, D), lambda i: (i, 0))],
        out_shape=[jax.ShapeDtypeStruct((8, 128), F32),
                   jax.ShapeDtypeStruct((S, D), F32),
                   jax.ShapeDtypeStruct((S, D), BF16)],
        compiler_params=_params(("arbitrary",)),
    )(y, target)


def _rms_bwd(name, x, dh, dres, g3, l):
    S, D = x.shape
    tr = _tile(S, 256, 16)

    def body(x_ref, dh_ref, dr_ref, g_ref, dx_ref, dxb_ref, dg_ref):
        @pl.when(pl.program_id(0) == 0)
        def _():
            dg_ref[...] = jnp.zeros_like(dg_ref)

        xv = x_ref[...]
        dhv = dh_ref[...]
        r = lax.rsqrt(jnp.mean(xv * xv, axis=-1, keepdims=True) + EPS)
        xhat = xv * r
        dhg = dhv * g_ref[...]
        dx = dr_ref[...] + r * (dhg - xhat * jnp.mean(dhg * xhat, axis=-1, keepdims=True))
        dx_ref[...] = dx
        dxb_ref[...] = dx.astype(BF16)
        dg_ref[...] += jnp.sum(dhv * xhat, axis=0, keepdims=True)

    return pl.pallas_call(
        body, name=name, grid=(S // tr,),
        in_specs=[pl.BlockSpec((tr, D), lambda i: (i, 0)), pl.BlockSpec((tr, D), lambda i: (i, 0)),
                  pl.BlockSpec((tr, D), lambda i: (i, 0)), _vec_spec(l, D)],
        out_specs=[pl.BlockSpec((tr, D), lambda i: (i, 0)), pl.BlockSpec((tr, D), lambda i: (i, 0)),
                   pl.BlockSpec((1, D), lambda i: (0, 0))],
        out_shape=[jax.ShapeDtypeStruct((S, D), F32), jax.ShapeDtypeStruct((S, D), BF16),
                   jax.ShapeDtypeStruct((1, D), F32)],
        compiler_params=_params(("arbitrary",)),
    )(x, dh, dres, g3)


def _glu_conv_bwd(name, dy, ap, cw, cb3, l, cfg):
    S, F = cfg["S"], cfg["F"]
    tc = 128
    nf = F // tc
    R, W = CONV_ROWS, CONV_ROWS + 2 * CONV_HALO

    def body(dy_ref, g_ref, u_ref, wg_ref, wu_ref, bg_ref, bu_ref, dap_ref, dw_ref, db_ref, acc):
        acc[...] = jnp.zeros_like(acc)
        wg, wu = wg_ref[...], wu_ref[...]
        bg, bu = bg_ref[...], bu_ref[...]

        def window(start, lo, out0):
            rows = pl.ds(start, W)
            apg, apu, dyv = g_ref[rows, :], u_ref[rows, :], dy_ref[rows, :]
            ag, _, _ = _conv3(apg, wg, bg)
            au, _, _ = _conv3(apu, wu, bu)
            sig = _sigmoid(ag)
            da_u = dyv * (ag * sig)
            da_g = (dyv * au) * (sig * (1.0 + ag * (1.0 - sig)))

            def add(i, prod):
                acc[i] += jnp.sum(prod[lo:lo + R].reshape(R // 8, 8, tc), axis=0)

            for half, (da, t, w) in enumerate([(da_g, apg, wg), (da_u, apu, wu)]):
                prv, nxt = _shifts(da)
                dap = (nxt * w[0:1] + da * w[1:2]) + prv * w[2:3]
                dap_ref[half, pl.ds(out0, R), :] = dap[lo:lo + R].astype(BF16)
                add(3 * half, nxt * t)
                add(3 * half + 1, da * t)
                add(3 * half + 2, prv * t)
                add(6 + half, da)

        _for_row_windows(S, window)
        col = [jnp.sum(acc[i], axis=0, keepdims=True) for i in range(8)]
        dw_ref[0] = jnp.concatenate(col[0:3], axis=0)
        dw_ref[1] = jnp.concatenate(col[3:6], axis=0)
        db_ref[0] = col[6]
        db_ref[1] = col[7]

    return pl.pallas_call(
        body, name=name, grid=(nf,),
        in_specs=[pl.BlockSpec((S, tc), lambda j: (0, j)),
                  pl.BlockSpec((S, tc), lambda j: (0, j)),
                  pl.BlockSpec((S, tc), lambda j: (0, j + nf)),
                  pl.BlockSpec((None, 3, tc), lambda j: (l, 0, j)),
                  pl.BlockSpec((None, 3, tc), lambda j: (l, 0, j + nf)),
                  pl.BlockSpec((None, 1, tc), lambda j: (l, 0, j)),
                  pl.BlockSpec((None, 1, tc), lambda j: (l, 0, j + nf))],
        out_specs=[pl.BlockSpec((2, S, tc), lambda j: (0, 0, j)),
                   pl.BlockSpec((2, 3, tc), lambda j: (0, 0, j)),
                   pl.BlockSpec((2, 1, tc), lambda j: (0, 0, j))],
        out_shape=[jax.ShapeDtypeStruct((2, S, F), BF16),
                   jax.ShapeDtypeStruct((2, 3, F), F32),
                   jax.ShapeDtypeStruct((2, 1, F), F32)],
        scratch_shapes=[pltpu.VMEM((8, 8, tc), F32)],
        compiler_params=_params(("arbitrary",)),
    )(dy, ap, ap, cw, cw, cb3, cb3)


def _attn_bwd(name, q, k, v, attn, dmix, sink, ag3, l, cfg):
    S, AW, KVW, NQ, NKV, G = cfg["S"], cfg["AW"], cfg["KVW"], cfg["NQ"], cfg["NKV"], cfg["G"]
    nb = S // HEAD
    scale = HEAD ** -0.5

    def body(q_ref, kp, kc, kn, vp, vc, vn, a_ref, dm_ref, sink_ref, ag_ref,
             dq_ref, dk_ref, dv_ref, dsink_ref, dag_ref, da_scr):
        n = pl.program_id(0)

        @pl.when(n == 0)
        def _():
            dk_ref[...] = jnp.zeros_like(dk_ref)
            dv_ref[...] = jnp.zeros_like(dv_ref)
            dsink_ref[...] = jnp.zeros_like(dsink_ref)
            dag_ref[...] = jnp.zeros_like(dag_ref)

        a = a_ref[...]
        dm = dm_ref[...]
        r = lax.rsqrt(jnp.mean(a * a, axis=-1, keepdims=True) + EPS)
        xhat = a * r
        dmg = dm * ag_ref[...]
        da_scr[...] = r * (dmg - xhat * jnp.mean(dmg * xhat, axis=-1, keepdims=True))
        dag_ref[...] += jnp.sum(dm * xhat, axis=0, keepdims=True)

        kb = jnp.concatenate([kp[...], kc[...], kn[...]], axis=0)
        vb = jnp.concatenate([vp[...], vc[...], vn[...]], axis=0)
        band = pl.ds(pl.multiple_of(n * HEAD, HEAD), 3 * HEAD)
        for j in range(NKV):
            sl = slice(j * HEAD, (j + 1) * HEAD)
            heads = [slice((j * G + g) * HEAD, (j * G + g + 1) * HEAD) for g in range(G)]
            qs = jnp.concatenate([q_ref[:, hs] for hs in heads], axis=0)
            do = jnp.concatenate([da_scr[:, hs] for hs in heads], axis=0)
            kj, vj = kb[:, sl], vb[:, sl]
            probs, p_sink = _attn_probs(qs, kj, n, _sink_rows(sink_ref, l, j, G), S, G)
            dob = do.astype(BF16)
            dprobs = lax.dot_general(dob, vj, (((1,), (1,)), ((), ())), preferred_element_type=F32)
            delta = jnp.sum(dprobs * probs, axis=-1, keepdims=True)
            ds = (probs * (dprobs - delta)) * scale
            dsb = ds.astype(BF16)
            dsk = -(p_sink * delta)
            dq = jnp.dot(dsb, kj, preferred_element_type=F32)
            for g in range(G):
                dq_ref[:, heads[g]] = dq[g * HEAD:(g + 1) * HEAD]
                part = jnp.sum(dsk[g * HEAD:(g + 1) * HEAD], axis=0, keepdims=True)
                dsink_ref[j * G + g:j * G + g + 1, :] += jnp.broadcast_to(part, (1, HEAD))
            dk_ref[band, sl] += lax.dot_general(dsb, qs, (((0,), (0,)), ((), ())), preferred_element_type=F32)
            dv_ref[band, sl] += lax.dot_general(probs.astype(BF16), dob, (((0,), (0,)), ((), ())),
                                                preferred_element_type=F32)

    return pl.pallas_call(
        body, name=name, grid=(nb,),
        in_specs=[pl.BlockSpec((HEAD, AW), lambda n: (n, 0))] + _band_specs(KVW, nb) + _band_specs(KVW, nb)
                 + [pl.BlockSpec((HEAD, AW), lambda n: (n, 0)),
                    pl.BlockSpec((HEAD, AW), lambda n: (n, 0)),
                    pl.BlockSpec(memory_space=pltpu.SMEM), _vec_spec(l, AW)],
        out_specs=[pl.BlockSpec((HEAD, AW), lambda n: (n, 0)),
                   pl.BlockSpec((S + 2 * HEAD, KVW), lambda n: (0, 0)),
                   pl.BlockSpec((S + 2 * HEAD, KVW), lambda n: (0, 0)),
                   pl.BlockSpec((NQ, HEAD), lambda n: (0, 0)),
                   pl.BlockSpec((1, AW), lambda n: (0, 0))],
        out_shape=[jax.ShapeDtypeStruct((S, AW), F32),
                   jax.ShapeDtypeStruct((S + 2 * HEAD, KVW), F32),
                   jax.ShapeDtypeStruct((S + 2 * HEAD, KVW), F32),
                   jax.ShapeDtypeStruct((NQ, HEAD), F32),
                   jax.ShapeDtypeStruct((1, AW), F32)],
        scratch_shapes=[pltpu.VMEM((HEAD, AW), F32)],
        compiler_params=_params(("arbitrary",)),
    )(q, k, k, k, v, v, v, attn, dmix, sink, ag3)


def _qkv_prep_bwd(name, z, dq, dk_pad, dv_pad, qg3, kg3, cos2, sin2, l, cfg):
    S, AW, KVW, NQ, NKV = cfg["S"], cfg["AW"], cfg["KVW"], cfg["NQ"], cfg["NKV"]
    kv_blk = (3 * AW) // (2 * KVW)

    def body(zq_ref, zkv_ref, dq_ref, dk_ref, dv_ref, qg_ref, kg_ref, c_ref, s_ref,
             dzq_ref, dzkv_ref, dqg_ref, dkg_ref):
        @pl.when(pl.program_id(0) == 0)
        def _():
            dqg_ref[...] = jnp.zeros_like(dqg_ref)
            dkg_ref[...] = jnp.zeros_like(dkg_ref)

        cosv, sinv = c_ref[...], s_ref[...]

        def back(t, dr, g):
            r = lax.rsqrt(jnp.mean(t * t, axis=-1, keepdims=True) + EPS)
            xhat = t * r
            dn = dr * cosv + pltpu.roll(dr * sinv, HEAD // 2, axis=1)
            dxh = dn * g
            dt = r * (dxh - xhat * jnp.mean(dxh * xhat, axis=-1, keepdims=True))
            return dt, jnp.sum(dn * xhat, axis=0, keepdims=True)

        gq = jnp.zeros((1, HEAD), F32)
        for h in range(NQ):
            sl = slice(h * HEAD, (h + 1) * HEAD)
            dt, gpart = back(zq_ref[:, sl], dq_ref[:, sl], qg_ref[...])
            dzq_ref[:, sl] = dt.astype(BF16)
            gq = gq + gpart
        dqg_ref[...] += gq
        gk = jnp.zeros((1, HEAD), F32)
        for h in range(NKV):
            sl = slice(h * HEAD, (h + 1) * HEAD)
            dt, gpart = back(zkv_ref[:, sl], dk_ref[:, sl], kg_ref[...])
            dzkv_ref[:, sl] = dt.astype(BF16)
            gk = gk + gpart
        dkg_ref[...] += gk
        dzkv_ref[:, KVW:] = dv_ref[...].astype(BF16)

    return pl.pallas_call(
        body, name=name, grid=(S // HEAD,),
        in_specs=[pl.BlockSpec((HEAD, AW), lambda i: (i, 0)),
                  pl.BlockSpec((HEAD, 2 * KVW), lambda i: (i, kv_blk)),
                  pl.BlockSpec((HEAD, AW), lambda i: (i, 0)),
                  pl.BlockSpec((HEAD, KVW), lambda i: (i + 1, 0)),
                  pl.BlockSpec((HEAD, KVW), lambda i: (i + 1, 0)),
                  _vec_spec(l, HEAD), _vec_spec(l, HEAD),
                  pl.BlockSpec((HEAD, HEAD), lambda i: (i, 0)),
                  pl.BlockSpec((HEAD, HEAD), lambda i: (i, 0))],
        out_specs=[pl.BlockSpec((HEAD, AW), lambda i: (i, 0)),
                   pl.BlockSpec((HEAD, 2 * KVW), lambda i: (i, 0)),
                   pl.BlockSpec((1, HEAD), lambda i: (0, 0)),
                   pl.BlockSpec((1, HEAD), lambda i: (0, 0))],
        out_shape=[jax.ShapeDtypeStruct((S, AW), BF16),
                   jax.ShapeDtypeStruct((S, 2 * KVW), BF16),
                   jax.ShapeDtypeStruct((1, HEAD), F32),
                   jax.ShapeDtypeStruct((1, HEAD), F32)],
        compiler_params=_params(("arbitrary",)),
    )(z, z, dq, dk_pad, dv_pad, qg3, kg3, cos2, sin2)


def _sgu_bwd(name, z, dmix, lng3, lnb3, ws_b, wst_b, bs_b, og3, l, cfg):
    S, GW, NG = cfg["S"], cfg["GW"], cfg["NG"]

    def body(gu_ref, gv_ref, dm_ref, lng_ref, lnb_ref, ws_ref, wst_ref, bs_ref, og_ref,
             dgu_ref, dgv_ref, dws_ref, dbs_ref, dlng_ref, dlnb_ref, dog_ref, sg_scr, f_scr, dvn_scr):
        @pl.when(pl.program_id(0) == 0)
        def _():
            dws_ref[...] = jnp.zeros_like(dws_ref)
            dbs_ref[...] = jnp.zeros_like(dbs_ref)
            dlng_ref[...] = jnp.zeros_like(dlng_ref)
            dlnb_ref[...] = jnp.zeros_like(dlnb_ref)
            dog_ref[...] = jnp.zeros_like(dog_ref)

        gu, gv = gu_ref[...], gv_ref[...]
        lng = lng_ref[...]
        u, xhat, rstd, vn = _sgu_forward_math(gu, gv, lng, lnb_ref[...])
        vnb = vn.astype(BF16)
        for h in range(NG):
            sl = slice(h * HEAD, (h + 1) * HEAD)
            f = jnp.dot(ws_ref[h], vnb[:, sl], preferred_element_type=F32) + bs_ref[h]
            f_scr[:, sl] = f
            sg_scr[:, sl] = u[:, sl] * f
        sg = sg_scr[...]
        dm = dm_ref[...]
        r = lax.rsqrt(jnp.mean(sg * sg, axis=-1, keepdims=True) + EPS)
        sghat = sg * r
        dmg = dm * og_ref[...]
        dsg = r * (dmg - sghat * jnp.mean(dmg * sghat, axis=-1, keepdims=True))
        dog_ref[...] += jnp.sum(dm * sghat, axis=0, keepdims=True)
        du = dsg * f_scr[...]
        df = dsg * u
        dfb = df.astype(BF16)
        for h in range(NG):
            sl = slice(h * HEAD, (h + 1) * HEAD)
            dvn_scr[:, sl] = jnp.dot(wst_ref[h], dfb[:, sl], preferred_element_type=F32)
            dws_ref[h] += lax.dot_general(dfb[:, sl], vnb[:, sl], (((1,), (1,)), ((), ())),
                                          preferred_element_type=F32)
            dbs_ref[h] += jnp.broadcast_to(jnp.sum(df[:, sl], axis=-1, keepdims=True), (HEAD, HEAD))
        dvn = dvn_scr[...]
        dlng_ref[...] += jnp.sum(dvn * xhat, axis=0, keepdims=True)
        dlnb_ref[...] += jnp.sum(dvn, axis=0, keepdims=True)
        dxh = dvn * lng
        dvv = rstd * ((dxh - jnp.mean(dxh, axis=-1, keepdims=True))
                      - xhat * jnp.mean(dxh * xhat, axis=-1, keepdims=True))
        dgu_ref[...] = (du * _gelu_grad(gu)).astype(BF16)
        dgv_ref[...] = (dvv * _gelu_grad(gv)).astype(BF16)

    vec = pl.BlockSpec((1, GW), lambda c: (0, 0))
    mat = pl.BlockSpec((NG, HEAD, HEAD), lambda c: (0, 0, 0))
    wsp = pl.BlockSpec((None, NG, HEAD, HEAD), lambda c: (l, 0, 0, 0))
    return pl.pallas_call(
        body, name=name, grid=(S // HEAD,),
        in_specs=[pl.BlockSpec((HEAD, GW), lambda c: (c, 1)),
                  pl.BlockSpec((HEAD, GW), lambda c: (c, 2)),
                  pl.BlockSpec((HEAD, GW), lambda c: (c, 1)),
                  _vec_spec(l, GW), _vec_spec(l, GW), wsp, wsp, wsp, _vec_spec(l, GW)],
        out_specs=[pl.BlockSpec((HEAD, GW), lambda c: (c, 0)), pl.BlockSpec((HEAD, GW), lambda c: (c, 0)),
                   mat, mat, vec, vec, vec],
        out_shape=[jax.ShapeDtypeStruct((S, GW), BF16), jax.ShapeDtypeStruct((S, GW), BF16),
                   jax.ShapeDtypeStruct((NG, HEAD, HEAD), F32), jax.ShapeDtypeStruct((NG, HEAD, HEAD), F32),
                   jax.ShapeDtypeStruct((1, GW), F32), jax.ShapeDtypeStruct((1, GW), F32),
                   jax.ShapeDtypeStruct((1, GW), F32)],
        scratch_shapes=[pltpu.VMEM((HEAD, GW), F32), pltpu.VMEM((HEAD, GW), F32), pltpu.VMEM((HEAD, GW), F32)],
        compiler_params=_params(("arbitrary",)),
    )(z, z, dmix, lng3, lnb3, ws_b, wst_b, bs_b, og3)


def _mesh_pos():
    x, y, c = lax.axis_index("x"), lax.axis_index("y"), lax.axis_index("c")
    return x, y, c


def _dev_index(p):
    return 4 * p[0] + 2 * p[1] + p[2]


def _handshake(peers):
    barrier = pltpu.get_barrier_semaphore()
    for p in peers:
        pl.semaphore_signal(barrier, inc=1, device_id=p, device_id_type=MESH)
    pl.semaphore_wait(barrier, len(peers))


GATHER_COPIES = 10


def _gather_body(full, half, in_half):
    def body(in_ref, out_ref, send_sems, recv_sems, local_sem):
        x, y, c = _mesh_pos()
        me, sib = (x, y, c), (x, y, 1 - c)
        xn, yn, dg = (1 - x, y, c), (x, 1 - y, c), (1 - x, 1 - y, c)
        _handshake([sib, xn, yn])

        def copy(k, dst, to, src=None):
            return pltpu.make_async_remote_copy(
                src_ref=dst if src is None else src, dst_ref=dst,
                send_sem=send_sems.at[k], recv_sem=recv_sems.at[k], device_id=to, device_id_type=MESH)

        def win(dev, h=None):
            idx = _dev_index(dev)
            return full(out_ref, idx) if h is None else half(out_ref, idx, h)

        local = pltpu.make_async_copy(in_ref, win(me), local_sem)
        local.start()
        sends = [copy(0, win(me), sib, src=in_ref),
                 copy(1, win(me, 0), xn, src=in_half(in_ref, 0)),
                 copy(2, win(me, 1), yn, src=in_half(in_ref, 1)),
                 copy(4, win(me, 0), yn, src=in_half(in_ref, 0)),
                 copy(6, win(me, 1), xn, src=in_half(in_ref, 1))]
        for cp in sends:
            cp.start()
        copy(1, win(xn, 0), me).wait_recv()
        sends.append(copy(3, win(xn, 0), yn))
        sends[-1].start()
        copy(2, win(yn, 1), me).wait_recv()
        sends.append(copy(5, win(yn, 1), xn))
        sends[-1].start()
        copy(6, win(xn, 1), me).wait_recv()
        sends.append(copy(7, win(xn), sib))
        sends[-1].start()
        copy(4, win(yn, 0), me).wait_recv()
        sends.append(copy(8, win(yn), sib))
        sends[-1].start()
        copy(3, win(dg, 0), me).wait_recv()
        copy(5, win(dg, 1), me).wait_recv()
        sends.append(copy(9, win(dg), sib))
        sends[-1].start()
        sib_xn, sib_yn, sib_dg = (1 - x, y, 1 - c), (x, 1 - y, 1 - c), (1 - x, 1 - y, 1 - c)
        for k, dev in [(0, sib), (7, sib_xn), (8, sib_yn), (9, sib_dg)]:
            copy(k, win(dev), me).wait_recv()
        for cp in sends:
            cp.wait_send()
        local.wait()

    return body


def _all_gather(name, cid, shard, out_shape, windows):
    return pl.kernel(
        _gather_body(*windows), out_type=out_shape,
        mesh=plsc.ScalarSubcoreMesh(axis_name="seq", num_cores=1), name=name,
        scratch_types=[pltpu.SemaphoreType.DMA((GATHER_COPIES,)), pltpu.SemaphoreType.DMA((GATHER_COPIES,)),
                       pltpu.SemaphoreType.DMA],
        compiler_params=pltpu.CompilerParams(collective_id=cid),
    )(shard)


def _d2d_body(nb, ns):
    n = nb + ns

    def body(*refs):
        ins, outs = refs[:n], refs[n:2 * n]
        send_sems, recv_sems = refs[2 * n:]
        x, y, c = _mesh_pos()
        sib = (x, y, 1 - c)
        _handshake([sib])
        copies = []
        for t in range(n):
            cp = pltpu.make_async_remote_copy(
                src_ref=ins[t].at[:, 1 - c] if t < nb else ins[t], dst_ref=outs[t],
                send_sem=send_sems.at[t], recv_sem=recv_sems.at[t],
                device_id=sib, device_id_type=MESH)
            cp.start()
            copies.append(cp)
        for cp in copies:
            cp.wait()

    return body


def _rs_d2d(name, cid, bigs, smalls):
    shapes = [jax.ShapeDtypeStruct((g.shape[0],) + g.shape[2:], g.dtype) for g in bigs]
    shapes += [jax.ShapeDtypeStruct(s.shape, s.dtype) for s in smalls]
    n = len(shapes)
    return pl.kernel(
        _d2d_body(len(bigs), len(smalls)), out_type=shapes,
        mesh=plsc.ScalarSubcoreMesh(axis_name="seq", num_cores=1), name=name,
        scratch_types=[pltpu.SemaphoreType.DMA((n,)), pltpu.SemaphoreType.DMA((n,))],
        compiler_params=pltpu.CompilerParams(collective_id=cid),
    )(*bigs, *smalls)


def _pair_sum(name, g4, recv, cvec, row_mult):
    Q, _, R, C = g4.shape
    tr = _tile(R, 256, row_mult)

    def body(c_ref, a_ref, b_ref, o_ref):
        o_ref[...] = (a_ref[...].astype(F32) + b_ref[...].astype(F32)).astype(BF16)

    grid_spec = pltpu.PrefetchScalarGridSpec(
        num_scalar_prefetch=1, grid=(Q, R // tr),
        in_specs=[pl.BlockSpec((None, None, tr, C), lambda q, i, c_ref: (q, c_ref[0], i, 0)),
                  pl.BlockSpec((None, tr, C), lambda q, i, c_ref: (q, i, 0))],
        out_specs=pl.BlockSpec((None, tr, C), lambda q, i, c_ref: (q, i, 0)))
    return pl.pallas_call(
        body, name=name, grid_spec=grid_spec, out_shape=jax.ShapeDtypeStruct((Q, R, C), BF16),
        compiler_params=_params(("arbitrary", "arbitrary")),
    )(cvec, g4, recv)


def _add2(name, a, b):
    R, C = a.shape
    tr = _tile(R, 512, 8)

    def body(a_ref, b_ref, o_ref):
        o_ref[...] = a_ref[...] + b_ref[...]

    blk = pl.BlockSpec((tr, C), lambda i: (i, 0))
    return pl.pallas_call(body, name=name, grid=(R // tr,), in_specs=[blk, blk], out_specs=blk,
                          out_shape=jax.ShapeDtypeStruct((R, C), a.dtype),
                          compiler_params=_params(("arbitrary",)))(a, b)


def _ici_copies(nb, n, srcs, lands, send_sems, recv_sems):
    x, y, c = _mesh_pos()
    q_me = 2 * x + y
    copies = []
    for t in range(n):
        for k in range(1, 4):
            px, py = x ^ (k >> 1), y ^ (k & 1)
            copies.append(pltpu.make_async_remote_copy(
                src_ref=srcs[t].at[2 * px + py] if t < nb else srcs[t], dst_ref=lands[t].at[q_me],
                send_sem=send_sems.at[3 * t + k - 1], recv_sem=recv_sems.at[3 * t + k - 1],
                device_id=(px, py, c), device_id_type=MESH))
    return copies


_HBM = pl.BlockSpec(memory_space=pltpu.HBM)
_SEM = pl.BlockSpec(memory_space=pltpu.SEMAPHORE)
_DATAFLOW = pltpu.SideEffectType.DATAFLOW_SIDE_EFFECTING


def _ici_start(name, bigs, smalls):
    nb, n = len(bigs), len(bigs) + len(smalls)
    srcs = list(bigs) + list(smalls)
    lands = [lax.empty(g.shape, g.dtype) for g in bigs] + [lax.empty((N_DEV // 2,) + s.shape, s.dtype) for s in smalls]

    def body(*refs):
        src_refs, land_refs = refs[:n], refs[n:2 * n]
        send_sems, recv_sems = refs[2 * n], refs[2 * n + 1]
        token = refs[-1]
        for cp in _ici_copies(nb, n, src_refs, land_refs, send_sems, recv_sems):
            cp.start()
        token[...] = jnp.zeros_like(token)

    hbm = [pltpu.HBM(a.shape, a.dtype) for a in srcs + lands]
    args = [pltpu.with_memory_space_constraint(a, pltpu.HBM) for a in srcs + lands]
    out = pl.pallas_call(
        body, name=name,
        out_shape=[pltpu.SemaphoreType.DMA((3 * n,)), pltpu.SemaphoreType.DMA((3 * n,))] + hbm
                  + [jax.ShapeDtypeStruct((8, 128), F32)],
        in_specs=[_HBM] * (2 * n),
        out_specs=[_SEM, _SEM] + [_HBM] * (2 * n) + [pl.BlockSpec(memory_space=pltpu.VMEM)],
        input_output_aliases={i: 2 + i for i in range(2 * n)},
        compiler_params=pltpu.CompilerParams(has_side_effects=_DATAFLOW),
    )(*args)
    return out[0], out[1], list(out[2:2 + n]), list(out[2 + n:2 + 2 * n]), out[-1]


def _ici_wait(name, started, nb, not_before):
    send_sems, recv_sems, srcs, lands, token = started
    n = len(srcs)

    def body(*refs):
        src_refs, land_refs = refs[:n], refs[n:2 * n]
        send_refs, recv_refs = refs[2 * n], refs[2 * n + 1]
        for cp in _ici_copies(nb, n, src_refs, land_refs, send_refs, recv_refs):
            cp.wait_send()
            cp.wait_recv()

    hbm = [pltpu.HBM(a.shape, a.dtype) for a in srcs + lands]
    out = pl.pallas_call(
        body, name=name, out_shape=hbm,
        in_specs=[_HBM] * (2 * n) + [_SEM, _SEM] + [pl.BlockSpec(memory_space=pl.ANY)] * 2,
        out_specs=[_HBM] * (2 * n),
        input_output_aliases={i: i for i in range(2 * n)},
        compiler_params=pltpu.CompilerParams(has_side_effects=_DATAFLOW),
    )(*srcs, *lands, send_sems, recv_sems, not_before, token)
    return list(out[:n]), list(out[n:])


def _adamw_math(w, g, m, v):
    m2 = ADAM_B1 * m + (1.0 - ADAM_B1) * g
    v2 = ADAM_B2 * v + (1.0 - ADAM_B2) * (g * g)
    m_hat = m2 / (1.0 - ADAM_B1 ** ADAM_STEP)
    v_hat = v2 / (1.0 - ADAM_B2 ** ADAM_STEP)
    delta = -ADAM_LR * (m_hat / (jnp.sqrt(v_hat) + ADAM_EPS) + ADAM_WD * w)
    return delta, m2, v2


def _chip_terms(q, own, land_ref, n):
    return [jnp.where(q == s, own, land_ref[s]).astype(F32) for s in range(n)]


def _adamw_sum(name, land, own, qvec, w, m, v, l, prev, row_mult):
    NS, R, C = land.shape
    L = w.shape[0]
    tr = _tile(R, 128, row_mult)

    def body(*refs):
        q_ref, r_ref, o_ref, w_ref, m_ref, v_ref = refs[:6]
        g_ref, d_ref, nm_ref, nv_ref = refs[-4:]
        terms = _chip_terms(q_ref[0], o_ref[...], r_ref, NS)
        g = terms[0]
        for t in terms[1:]:
            g = g + t
        d, m2, v2 = _adamw_math(w_ref[...], g, m_ref[...], v_ref[...])
        g_ref[...] = g
        d_ref[...] = d
        nm_ref[...] = m2
        nv_ref[...] = v2

    blk = pl.BlockSpec((None, tr, C), lambda i, q: (l, i, 0))
    shp = jax.ShapeDtypeStruct((L, R, C), F32)
    in_specs = [pl.BlockSpec((NS, tr, C), lambda i, q: (0, i, 0)),
                pl.BlockSpec((None, tr, C), lambda i, q: (q[0], i, 0)), blk, blk, blk]
    args = [land, own, w, m, v]
    aliases = {}
    if prev is not None:
        in_specs += [pl.BlockSpec(memory_space=pl.ANY)] * 4
        args += list(prev)
        aliases = {6 + i: i for i in range(4)}
    grid_spec = pltpu.PrefetchScalarGridSpec(num_scalar_prefetch=1, grid=(R // tr,), in_specs=in_specs,
                                             out_specs=[blk, blk, blk, blk])
    return pl.pallas_call(
        body, name=name, grid_spec=grid_spec, out_shape=[shp, shp, shp, shp],
        input_output_aliases=aliases,
        compiler_params=_params(("arbitrary",)),
    )(qvec, *args)


def _adamw_plain(name, g, w, m, v):
    def body(g_ref, w_ref, m_ref, v_ref, d_ref, nm_ref, nv_ref):
        d, m2, v2 = _adamw_math(w_ref[...], g_ref[...], m_ref[...], v_ref[...])
        d_ref[...] = d
        nm_ref[...] = m2
        nv_ref[...] = v2

    shp = jax.ShapeDtypeStruct(g.shape, F32)
    return pl.pallas_call(body, name=name, out_shape=[shp, shp, shp], compiler_params=_params())(g, w, m, v)


SMALL = ["norm1_g", "q_norm_g", "k_norm_g", "sink", "sgu_ln_g", "sgu_ln_b", "w_s", "b_s",
         "attn_out_g", "sgu_out_g", "norm2_g", "conv_b"]
PACK_ALIGN = 1024


def _pack(pieces):
    flat = []
    for p in pieces:
        f = p.reshape(-1).astype(F32)
        pad = (-f.shape[0]) % PACK_ALIGN
        flat.append(jnp.pad(f, (0, pad)) if pad else f)
    return jnp.concatenate(flat).reshape(-1, 128)


def kernel(x, norm1_g, w_in, q_norm_g, k_norm_g, sink, sgu_ln_g, sgu_ln_b, w_s, b_s, attn_out_g, sgu_out_g, w_o, norm2_g, w_up, conv_w, conv_b, w_down, loss_target, m_norm1_g, m_w_in, m_q_norm_g, m_k_norm_g, m_sink, m_sgu_ln_g, m_sgu_ln_b, m_w_s, m_b_s, m_attn_out_g, m_sgu_out_g, m_w_o, m_norm2_g, m_w_up, m_conv_w, m_conv_b, m_w_down, v_norm1_g, v_w_in, v_q_norm_g, v_k_norm_g, v_sink, v_sgu_ln_g, v_sgu_ln_b, v_w_s, v_b_s, v_attn_out_g, v_sgu_out_g, v_w_o, v_norm2_g, v_w_up, v_conv_w, v_conv_b, v_w_down):
    weights = dict(norm1_g=norm1_g, w_in=w_in, q_norm_g=q_norm_g, k_norm_g=k_norm_g, sink=sink, sgu_ln_g=sgu_ln_g,
                   sgu_ln_b=sgu_ln_b, w_s=w_s, b_s=b_s, attn_out_g=attn_out_g, sgu_out_g=sgu_out_g, w_o=w_o,
                   norm2_g=norm2_g, w_up=w_up, conv_w=conv_w, conv_b=conv_b, w_down=w_down)
    mom_m = dict(norm1_g=m_norm1_g, w_in=m_w_in, q_norm_g=m_q_norm_g, k_norm_g=m_k_norm_g, sink=m_sink,
                 sgu_ln_g=m_sgu_ln_g, sgu_ln_b=m_sgu_ln_b, w_s=m_w_s, b_s=m_b_s, attn_out_g=m_attn_out_g,
                 sgu_out_g=m_sgu_out_g, w_o=m_w_o, norm2_g=m_norm2_g, w_up=m_w_up, conv_w=m_conv_w,
                 conv_b=m_conv_b, w_down=m_w_down)
    mom_v = dict(norm1_g=v_norm1_g, w_in=v_w_in, q_norm_g=v_q_norm_g, k_norm_g=v_k_norm_g, sink=v_sink,
                 sgu_ln_g=v_sgu_ln_g, sgu_ln_b=v_sgu_ln_b, w_s=v_w_s, b_s=v_b_s, attn_out_g=v_attn_out_g,
                 sgu_out_g=v_sgu_out_g, w_o=v_w_o, norm2_g=v_norm2_g, w_up=v_w_up, conv_w=v_conv_w,
                 conv_b=v_conv_b, w_down=v_w_down)
    order = ["norm1_g", "w_in", "q_norm_g", "k_norm_g", "sink", "sgu_ln_g", "sgu_ln_b", "w_s", "b_s",
             "attn_out_g", "sgu_out_g", "w_o", "norm2_g", "w_up", "conv_w", "conv_b", "w_down"]

    _, S, D = x.shape
    L = w_in.shape[0]
    AW = D // 2
    NQ = AW // HEAD
    NKV = max(1, NQ // 4)
    G = NQ // NKV
    KVW = NKV * HEAD
    GW = D - AW
    NG = GW // HEAD
    IN = AW + 2 * KVW + 2 * GW
    INS = w_in.shape[2]
    OS = w_o.shape[1]
    US = w_up.shape[2]
    DS = w_down.shape[1]
    F2 = US * N_DEV
    F = F2 // 2
    assert INS * N_DEV == IN and OS * N_DEV == D and DS * N_DEV == F and AW == GW and (3 * AW) % (2 * KVW) == 0
    cfg = dict(S=S, D=D, AW=AW, NQ=NQ, NKV=NKV, G=G, KVW=KVW, GW=GW, NG=NG, F=F, F2=F2)

    def row_windows(n):
        m = n // 2
        return (lambda ref, idx: ref.at[:, pl.ds(pl.multiple_of(idx * n, n), n), :],
                lambda ref, idx, h: ref.at[:, pl.ds(pl.multiple_of(idx * n + h * m, m), m), :],
                lambda ref, h: ref.at[:, pl.ds(h * m, m), :])

    def col_windows(n, rows):
        m = rows // 2
        return (lambda ref, idx: ref.at[:, :, pl.ds(pl.multiple_of(idx * n, n), n)],
                lambda ref, idx, h: ref.at[:, pl.ds(h * m, m), pl.ds(pl.multiple_of(idx * n, n), n)],
                lambda ref, h: ref.at[:, pl.ds(h * m, m), :])

    def lead_col_windows(n, lead):
        m = lead // 2
        return (lambda ref, idx: ref.at[:, :, pl.ds(pl.multiple_of(idx * n, n), n)],
                lambda ref, idx, h: ref.at[pl.ds(h * m, m), :, pl.ds(pl.multiple_of(idx * n, n), n)],
                lambda ref, h: ref.at[pl.ds(h * m, m)])

    def after(first, then):
        return lax.optimization_barrier((first, then))

    w_in_t, m_w_in_t, v_w_in_t = (jnp.swapaxes(a, 1, 2) for a in (w_in, m_w_in, v_w_in))

    ids = iter(range(64))
    w_in_f, w_o_f, w_up_f, w_down_f = [], [], [], []

    def gather_layer(l, not_before=None):
        in_shard = w_in_t[l:l + 1].astype(BF16)
        if not_before is not None:
            _, in_shard = after(not_before, in_shard)
        w_in_f.append(_all_gather("ag_in", next(ids), in_shard,
                                  jax.ShapeDtypeStruct((1, IN, D), BF16), row_windows(INS)))
        w_o_f.append(_all_gather("ag_o", next(ids), w_o[l:l + 1].astype(BF16),
                                 jax.ShapeDtypeStruct((1, D, D), BF16), row_windows(OS)))
        w_up_f.append(_all_gather("ag_up", next(ids), w_up[l:l + 1].astype(BF16),
                                  jax.ShapeDtypeStruct((1, D, F2), BF16), col_windows(US, D)))
        w_down_f.append(_all_gather("ag_down", next(ids), w_down[l:l + 1].astype(BF16),
                                    jax.ShapeDtypeStruct((1, F, D), BF16), row_windows(DS)))

    assert L % 2 == 0 and INS % 32 == 0 and OS % 32 == 0 and DS % 32 == 0 and D % 32 == 0
    gather_layer(0)
    conv_w_f = _all_gather("ag_conv_w", next(ids), conv_w, jax.ShapeDtypeStruct((L, 3, F2), F32),
                           lead_col_windows(US, L))

    n1g3, n2g3 = norm1_g.reshape(L, 1, D), norm2_g.reshape(L, 1, D)
    qg3, kg3 = q_norm_g.reshape(L, 1, HEAD), k_norm_g.reshape(L, 1, HEAD)
    lng3, lnb3 = sgu_ln_g.reshape(L, 1, GW), sgu_ln_b.reshape(L, 1, GW)
    ag3, og3 = attn_out_g.reshape(L, 1, AW), sgu_out_g.reshape(L, 1, GW)
    cb3 = conv_b.reshape(L, 1, F2)
    ws_b = w_s.astype(BF16)
    wst_b = jnp.swapaxes(w_s, 2, 3).astype(BF16)
    bs_b = jnp.broadcast_to(b_s[..., None], (L, NG, HEAD, HEAD))
    inv_freq = ROPE_THETA ** (-jnp.arange(0, HEAD, 2, dtype=F32) / HEAD)
    ang = jnp.arange(S, dtype=F32)[:, None] * inv_freq[None, :]
    cos2 = jnp.concatenate([jnp.cos(ang), jnp.cos(ang)], axis=1)
    sin2 = jnp.concatenate([-jnp.sin(ang), jnp.sin(ang)], axis=1)

    tn = 512
    t_in, t_d, t_f, t_f2 = _tile(IN, tn, 128), _tile(D, tn, 128), _tile(F, tn, 128), _tile(F2, tn, 128)
    tm_f = _tile(F, 512, 128)
    tm_s = _tile(S, 1024, 128)
    assert AW % t_in == 0 and (2 * KVW) % t_in == 0
    n_q, n_kv, n_g = AW // t_in, (2 * KVW) // t_in, (2 * GW) // t_in

    def in_tile(j):
        return jnp.where(j < n_q, j, jnp.where(j < n_q + n_g, j + n_kv, j - n_g))

    def w_spec(tk, tn_):
        return pl.BlockSpec((None, tk, tn_), lambda i, j, k: (0, k, j))

    def wt_spec(tn_, tk):
        return pl.BlockSpec((None, tn_, tk), lambda i, j, k: (0, j, k))

    def a_spec(tm, tk):
        return pl.BlockSpec((tm, tk), lambda i, j, k: (i, k))

    def at_spec(tk, tm):
        return pl.BlockSpec((tk, tm), lambda i, j, k: (k, i))

    def b_spec(tk, tn_):
        return pl.BlockSpec((tk, tn_), lambda i, j, k: (k, j))

    xs = x.reshape(S, D)
    saved = []
    cur = xs
    for l in range(L):
        h = _rms_fwd("rms1_fwd", cur, n1g3, l)
        z = _mm("mm_in", h, w_in_f[l], M=S, N=IN, K=D, tm=S, tn=t_in, tk=D, tb=True,
                a_spec=a_spec(S, D),
                b_spec=pl.BlockSpec((None, t_in, D), lambda i, j, k: (0, in_tile(j), 0)))
        q_r, k_r, v_b = _qkv_prep("qkv_prep", z, qg3, kg3, cos2, sin2, l, cfg)
        attn, mix_l = _attn_fwd("attn_fwd", q_r, k_r, v_b, sink, ag3, l, cfg)
        mix_r = _sgu_fwd("sgu_fwd", z, lng3, lnb3, ws_b, bs_b, og3, l, cfg)
        mixed = jnp.concatenate([mix_l, mix_r], axis=1)
        x1 = _mm("mm_o", mixed, w_o_f[l], M=S, N=D, K=D, tm=S, tn=t_d, tk=D,
                 a_spec=a_spec(S, D), b_spec=w_spec(D, t_d), res=cur)
        if l + 1 < L:
            gather_layer(l + 1, not_before=x1)
        h2 = _rms_fwd("rms2_fwd", x1, n2g3, l)
        ap = _mm("mm_up", h2, w_up_f[l], M=S, N=F2, K=D, tm=S, tn=t_f2, tk=D,
                 a_spec=a_spec(S, D), b_spec=w_spec(D, t_f2))
        y_b = _conv_glu_fwd("conv_glu_fwd", ap, conv_w_f, cb3, l, cfg)
        x2 = _mm("mm_down", y_b, w_down_f[l], M=S, N=D, K=F, tm=tm_s, tn=t_d, tk=F,
                 a_spec=a_spec(tm_s, F), b_spec=w_spec(F, t_d), res=x1)
        saved.append(dict(x=cur, h=h, z=z, q=q_r, k=k_r, v=v_b, attn=attn, mixed=mixed, x1=x1, h2=h2, ap=ap, y=y_b))
        cur = x2

    loss_tile, dx, dxb = _loss_bwd("loss", cur, loss_target.reshape(S, D))

    gS = [dict() for _ in range(L)]
    recv = [dict() for _ in range(L)]
    cvec = jnp.reshape(lax.axis_index("c"), (1,)).astype(jnp.int32)

    def rs_swap(name, g, R, C, small=None):
        g4 = g.reshape(N_DEV // 2, 2, R, C)
        got = _rs_d2d("rs_d2d_" + name, next(ids), [g4], [] if small is None else [small])
        return name, g4, got, small

    started = []

    def rs_finish(l, pending, then=None):
        name, g4, got, small = pending
        chip = _pair_sum("pair_sum_" + name, g4, got[0], cvec, 16)
        chip_small = [] if small is None else [_add2("pair_sum_small", small, got[1])]
        begun = _ici_start("rs_ici_" + name, [chip], chip_small)
        if then is not None:
            token, then = after(begun[4], then)
            begun = begun[:4] + (token,)
        started.append((l, name, begun))
        return then

    carried = None
    for l in reversed(range(L)):
        sv = saved[l]
        dy = _mm("mm_dy", dxb, w_down_f[l], M=S, N=F, K=D, tm=S, tn=t_f, tk=D, tb=True,
                 a_spec=a_spec(S, D), b_spec=wt_spec(t_f, D))
        if carried is not None:
            dy = rs_finish(l + 1, carried, dy)
        g_down = _mm("mm_gdown", sv["y"], dxb, M=F, N=D, K=S, tm=tm_f, tn=D, tk=S, ta=True,
                     a_spec=at_spec(S, tm_f), b_spec=b_spec(S, D), out_dtype=BF16)
        g_down, dy = after(g_down, dy)
        swap_down = rs_swap("w_down", g_down, DS, D)
        dap3, dcw, dcb = _glu_conv_bwd("glu_conv_bwd", dy, sv["ap"], conv_w_f, cb3, l, cfg)
        gS[l]["conv_w"] = jnp.concatenate([dcw[0], dcw[1]], axis=1)
        gS[l]["conv_b"] = jnp.concatenate([dcb[0], dcb[1]], axis=1).reshape(F2)
        dap3 = rs_finish(l, swap_down, dap3)
        dh2 = _mm("mm_dh2", dap3, w_up_f[l], M=S, N=D, K=F2, tm=tm_s, tn=t_d, tk=F, tb=True,
                  a_spec=pl.BlockSpec((None, tm_s, F), lambda i, j, k: (k, i, 0)),
                  b_spec=wt_spec(t_d, F))
        g_up = _mm("mm_gup", sv["h2"], dap3, M=D, N=F2, K=S, tm=D, tn=US, tk=S, ta=True,
                   a_spec=at_spec(S, D),
                   b_spec=pl.BlockSpec((None, S, US), lambda i, j, k: (j // (N_DEV // 2), 0, j % (N_DEV // 2))),
                   out_dtype=BF16, out_shape=(N_DEV, D, US),
                   out_spec=pl.BlockSpec((None, D, US), lambda i, j, k: (j, 0, 0)))
        g_up, dh2 = after(g_up, dh2)
        swap_up = rs_swap("w_up", g_up, D, US)
        dx1, dx1b, dg2 = _rms_bwd("rms2_bwd", sv["x1"], dh2, dx, n2g3, l)
        gS[l]["norm2_g"] = dg2.reshape(D)
        dmix = _mm("mm_dmix", dx1b, w_o_f[l], M=S, N=D, K=D, tm=S, tn=t_d, tk=D, tb=True,
                   a_spec=a_spec(S, D), b_spec=wt_spec(t_d, D))
        g_o = _mm("mm_go", sv["mixed"], dx1b, M=D, N=D, K=S, tm=D, tn=t_d, tk=S, ta=True,
                  a_spec=at_spec(S, D), b_spec=b_spec(S, t_d), out_dtype=BF16)
        g_o, dmix = after(g_o, dmix)
        swap_o = rs_swap("w_o", g_o, OS, D)
        dmix = rs_finish(l, swap_up, dmix)
        dq_r, dk_pad, dv_pad, dsink, dag = _attn_bwd("attn_bwd", sv["q"], sv["k"], sv["v"], sv["attn"], dmix,
                                                     sink, ag3, l, cfg)
        dq_r = rs_finish(l, swap_o, dq_r)
        gS[l]["sink"] = dsink[:, 0]
        gS[l]["attn_out_g"] = dag.reshape(AW)
        dzgu, dzgv, dws, dbs, dlng, dlnb, dog = _sgu_bwd("sgu_bwd", sv["z"], dmix, lng3, lnb3, ws_b, wst_b, bs_b,
                                                        og3, l, cfg)
        gS[l]["w_s"] = dws
        gS[l]["b_s"] = dbs[:, :, 0]
        gS[l]["sgu_ln_g"] = dlng.reshape(GW)
        gS[l]["sgu_ln_b"] = dlnb.reshape(GW)
        gS[l]["sgu_out_g"] = dog.reshape(GW)
        dzq, dzkv, dqg, dkg = _qkv_prep_bwd("qkv_prep_bwd", sv["z"], dq_r, dk_pad, dv_pad, qg3, kg3, cos2, sin2,
                                            l, cfg)
        gS[l]["q_norm_g"] = dqg.reshape(HEAD)
        gS[l]["k_norm_g"] = dkg.reshape(HEAD)
        dz = jnp.concatenate([dzq, dzkv, dzgu, dzgv], axis=1)
        dh = _mm("mm_dh", dz, w_in_f[l], M=S, N=D, K=IN, tm=tm_s, tn=t_d, tk=IN,
                 a_spec=a_spec(tm_s, IN), b_spec=w_spec(IN, t_d))
        g_in = _mm("mm_gin", dz, sv["h"], M=IN, N=D, K=S, tm=t_in, tn=D, tk=S, ta=True,
                   a_spec=at_spec(S, t_in), b_spec=b_spec(S, D), out_dtype=BF16)
        g_in, dh = after(g_in, dh)
        dx, dxb, dg1 = _rms_bwd("rms1_bwd", sv["x"], dh, dx1, n1g3, l)
        gS[l]["norm1_g"] = dg1.reshape(D)
        carried = rs_swap("w_in", g_in, INS, D, small=_pack([gS[l][n] for n in SMALL] + [gS[l]["conv_w"]]))
    dx = rs_finish(0, carried, dx)
    grad_x = dx.reshape(1, S, D)

    grads, deltas, new_m, new_v = {}, {}, {}, {}
    results, placed = {}, dx
    qvec = jnp.reshape(2 * lax.axis_index("x") + lax.axis_index("y"), (1,)).astype(jnp.int32)
    for l, name, begun in started:
        sent, landed = _ici_wait("rs_wait_" + name, begun, 1, placed)
        recv[l][name] = (sent, landed)
        wmv = (w_in_t, m_w_in_t, v_w_in_t) if name == "w_in" else (weights[name], mom_m[name], mom_v[name])
        results[name] = _adamw_sum("adamw_" + name, landed[0], sent[0], qvec, *wmv, l, results.get(name), 16)
        placed = results[name][0]
    _, loss_local = after(placed, loss_tile[0, 0])
    loss = lax.psum(loss_local, ("x", "y", "c"))
    results["w_in"] = [jnp.swapaxes(r, 1, 2) for r in results["w_in"]]
    for name, res in results.items():
        grads[name], deltas[name], new_m[name], new_v[name] = res

    def as_rows(a):
        flat = a.reshape(L, -1)
        short = (-flat.shape[1]) % 128
        if short:
            flat = jnp.pad(flat, ((0, 0), (0, short)))
        return flat.reshape(L, -1, 128)

    sizes = [weights[n][0].size for n in SMALL] + [3 * F2]
    offsets, off = [], 0
    for sz in sizes:
        offsets.append(off // 128)
        off += sz + (-sz) % PACK_ALIGN
    n_small = len(SMALL)
    rows_of = [-(-sz // 128) for sz in sizes]

    def small_body(*refs):
        q_ref = refs[0]
        lands, owns = refs[1:1 + L], refs[1 + L:1 + 2 * L]
        wmv_refs = refs[1 + 2 * L:1 + 2 * L + 3 * n_small]
        outs = refs[1 + 2 * L + 3 * n_small:]
        for l in range(L):
            for p in range(n_small + 1):
                rows = slice(offsets[p], offsets[p] + rows_of[p])
                terms = [jnp.where(q_ref[0] == s, owns[l][rows, :], lands[l][s, rows, :]) for s in range(N_DEV // 2)]
                g = terms[0]
                for t in terms[1:]:
                    g = g + t
                if p == n_small:
                    outs[4 * n_small][l] = g
                    continue
                d, m2, v2 = _adamw_math(wmv_refs[3 * p][l], g, wmv_refs[3 * p + 1][l], wmv_refs[3 * p + 2][l])
                for t, val in enumerate([g, d, m2, v2]):
                    outs[4 * p + t][l] = val

    vmem = pl.BlockSpec(memory_space=pltpu.VMEM)
    wmv_args = []
    for n in SMALL:
        wmv_args += [as_rows(weights[n]), as_rows(mom_m[n]), as_rows(mom_v[n])]
    small_out = pl.pallas_call(
        small_body, name="adamw_small",
        in_specs=[pl.BlockSpec(memory_space=pltpu.SMEM)] + [vmem] * (2 * L + 3 * n_small),
        out_specs=[vmem] * (4 * n_small + 1),
        out_shape=[jax.ShapeDtypeStruct((L, rows_of[p], 128), F32) for p in range(n_small) for _ in range(4)]
                  + [jax.ShapeDtypeStruct((L, rows_of[n_small], 128), F32)],
        compiler_params=_params(),
    )(qvec, *[recv[l]["w_in"][1][1] for l in range(L)], *[recv[l]["w_in"][0][1] for l in range(L)], *wmv_args)
    for p, n in enumerate(SMALL):
        for t, store in enumerate([grads, deltas, new_m, new_v]):
            store[n] = small_out[4 * p + t].reshape(L, -1)[:, :sizes[p]].reshape(weights[n].shape)
    conv_full = small_out[4 * n_small].reshape(L, 3, F2)
    me_i = _dev_index(_mesh_pos())
    g_cw = lax.dynamic_slice_in_dim(conv_full, me_i * US, US, axis=2)
    grads["conv_w"] = g_cw
    d_cw, m_cw, v_cw = _adamw_plain("adamw_conv_w", g_cw.reshape(L * 3, US), conv_w.reshape(L * 3, US),
                                    m_conv_w.reshape(L * 3, US), v_conv_w.reshape(L * 3, US))
    deltas["conv_w"] = d_cw.reshape(L, 3, US)
    new_m["conv_w"] = m_cw.reshape(L, 3, US)
    new_v["conv_w"] = v_cw.reshape(L, 3, US)

    return (loss, grad_x, *[grads[n] for n in order], *[deltas[n] for n in order],
            *[new_m[n] for n in order], *[new_v[n] for n in order])
```

```python
import jax
import jax.numpy as jnp
from jax import lax
from jax.experimental import pallas as pl
from jax.experimental.pallas import tpu as pltpu
from jax.experimental.pallas import tpu_sc as plsc

F32 = jnp.float32
BF16 = jnp.bfloat16
MESH = pl.DeviceIdType.MESH

N_DEV = 8
HEAD = 128
EPS = 1e-6
MASK_VALUE = -1e30
ROPE_THETA = 10000.0
GELU_C = 0.7978845608028654
GELU_A = 0.044715

ADAM_LR = 0.001
ADAM_B1 = 0.9
ADAM_B2 = 0.999
ADAM_EPS = 1e-08
ADAM_WD = 0.01
ADAM_STEP = 10

VMEM_LIMIT = 56 * 1024 * 1024


def _tile(n, pref, mult):
    best = None
    for t in range(mult, min(n, pref) + 1, mult):
        if n % t == 0:
            best = t
    return n if best is None else best


def _params(sem=None):
    kw = dict(vmem_limit_bytes=VMEM_LIMIT)
    if sem is not None:
        kw["dimension_semantics"] = sem
    return pltpu.CompilerParams(**kw)


def _gelu(x):
    return x * (0.5 * (1.0 + jnp.tanh(GELU_C * (x + GELU_A * (x * x * x)))))


def _gelu_grad(x):
    t = jnp.tanh(GELU_C * (x + GELU_A * (x * x * x)))
    return 0.5 * (1.0 + t) + 0.5 * x * (1.0 - t * t) * (GELU_C * (1.0 + 3.0 * GELU_A * (x * x)))


def _sigmoid(x):
    return 1.0 / (1.0 + jnp.exp(-x))


def _vec_spec(l, n):
    return pl.BlockSpec((None, 1, n), lambda *_: (l, 0, 0))


def _mm(name, a, b, *, M, N, K, tm, tn, tk, a_spec, b_spec, ta=False, tb=False, out_dtype=F32, res=None,
        out_shape=None, out_spec=None):
    nm, nn, nk = M // tm, N // tn, K // tk
    assert nm * tm == M and nn * tn == N and nk * tk == K
    assert not (ta and nk > 1)
    dims = (((1,), (1,)), ((), ())) if tb else (((1,), (0,)), ((), ()))

    def body(*refs):
        refs = list(refs)
        a_ref = refs.pop(0)
        b_ref = refs.pop(0)
        r_ref = refs.pop(0) if res is not None else None
        o_ref = refs.pop(0)
        acc = refs.pop(0) if nk > 1 else None
        at = refs.pop(0) if ta else None
        k = pl.program_id(2)
        if ta:
            @pl.when(pl.program_id(1) == 0)
            def _():
                at[...] = a_ref[...].T
            lhs = at[...]
        else:
            lhs = a_ref[...]
        p = lax.dot_general(lhs, b_ref[...], dims, preferred_element_type=F32)

        def finish(r):
            if r_ref is not None:
                r = r_ref[...] + r
            o_ref[...] = r.astype(out_dtype)

        if nk == 1:
            finish(p)
        else:
            @pl.when(k == 0)
            def _():
                acc[...] = p

            @pl.when(k > 0)
            def _():
                acc[...] += p

            @pl.when(k == nk - 1)
            def _():
                finish(acc[...])

    in_specs = [a_spec, b_spec]
    args = [a, b]
    if res is not None:
        in_specs.append(pl.BlockSpec((tm, tn), lambda i, j, k: (i, j)))
        args.append(res)
    scratch = []
    if nk > 1:
        scratch.append(pltpu.VMEM((tm, tn), F32))
    if ta:
        scratch.append(pltpu.VMEM((tm, tk), BF16))
    return pl.pallas_call(
        body, name=name, grid=(nm, nn, nk),
        in_specs=in_specs,
        out_specs=pl.BlockSpec((tm, tn), lambda i, j, k: (i, j)) if out_spec is None else out_spec,
        out_shape=jax.ShapeDtypeStruct((M, N) if out_shape is None else out_shape, out_dtype),
        scratch_shapes=scratch,
        compiler_params=_params(("arbitrary", "arbitrary", "arbitrary")),
    )(*args)


def _rms_fwd(name, x, g3, l):
    S, D = x.shape
    tr = _tile(S, 256, 16)

    def body(x_ref, g_ref, h_ref):
        xv = x_ref[...]
        r = lax.rsqrt(jnp.mean(xv * xv, axis=-1, keepdims=True) + EPS)
        h_ref[...] = ((xv * r) * g_ref[...]).astype(BF16)

    return pl.pallas_call(
        body, name=name, grid=(S // tr,),
        in_specs=[pl.BlockSpec((tr, D), lambda i: (i, 0)), _vec_spec(l, D)],
        out_specs=pl.BlockSpec((tr, D), lambda i: (i, 0)),
        out_shape=jax.ShapeDtypeStruct((S, D), BF16),
        compiler_params=_params(("arbitrary",)),
    )(x, g3)


def _rope(t, cos2, sin2):
    return t * cos2 + pltpu.roll(t, HEAD // 2, axis=1) * sin2


def _qkv_prep(name, z, qg3, kg3, cos2, sin2, l, cfg):
    S, AW, KVW, NQ, NKV = cfg["S"], cfg["AW"], cfg["KVW"], cfg["NQ"], cfg["NKV"]
    tr = _tile(S, 256, 16)
    kv_blk = (3 * AW) // (2 * KVW)

    def body(zq_ref, zkv_ref, qg_ref, kg_ref, c_ref, s_ref, q_ref, k_ref, v_ref):
        cosv, sinv = c_ref[...], s_ref[...]

        def norm_rope(t, g):
            r = lax.rsqrt(jnp.mean(t * t, axis=-1, keepdims=True) + EPS)
            return _rope((t * r) * g, cosv, sinv)

        for h in range(NQ):
            sl = slice(h * HEAD, (h + 1) * HEAD)
            q_ref[:, sl] = norm_rope(zq_ref[:, sl], qg_ref[...]).astype(BF16)
        for h in range(NKV):
            sl = slice(h * HEAD, (h + 1) * HEAD)
            k_ref[:, sl] = norm_rope(zkv_ref[:, sl], kg_ref[...]).astype(BF16)
        v_ref[...] = zkv_ref[:, KVW:].astype(BF16)

    return pl.pallas_call(
        body, name=name, grid=(S // tr,),
        in_specs=[pl.BlockSpec((tr, AW), lambda i: (i, 0)),
                  pl.BlockSpec((tr, 2 * KVW), lambda i: (i, kv_blk)),
                  _vec_spec(l, HEAD), _vec_spec(l, HEAD),
                  pl.BlockSpec((tr, HEAD), lambda i: (i, 0)),
                  pl.BlockSpec((tr, HEAD), lambda i: (i, 0))],
        out_specs=[pl.BlockSpec((tr, AW), lambda i: (i, 0)),
                   pl.BlockSpec((tr, KVW), lambda i: (i, 0)),
                   pl.BlockSpec((tr, KVW), lambda i: (i, 0))],
        out_shape=[jax.ShapeDtypeStruct((S, AW), BF16),
                   jax.ShapeDtypeStruct((S, KVW), BF16),
                   jax.ShapeDtypeStruct((S, KVW), BF16)],
        compiler_params=_params(("arbitrary",)),
    )(z, z, qg3, kg3, cos2, sin2)


def _band_specs(width, nb):
    return [pl.BlockSpec((HEAD, width), lambda n: (jnp.maximum(n - 1, 0), 0)),
            pl.BlockSpec((HEAD, width), lambda n: (n, 0)),
            pl.BlockSpec((HEAD, width), lambda n: (jnp.minimum(n + 1, nb - 1), 0))]


def _attn_probs(qs, kj, n, sink_of_row, S, G):
    s = lax.dot_general(qs, kj, (((1,), (1,)), ((), ())), preferred_element_type=F32) * (HEAD ** -0.5)
    rows = lax.broadcasted_iota(jnp.int32, (G * HEAD, 3 * HEAD), 0)
    cols = lax.broadcasted_iota(jnp.int32, (G * HEAD, 3 * HEAD), 1)
    qi = rows & (HEAD - 1)
    kpos = n * HEAD - HEAD + cols
    valid = (cols >= qi) & (cols <= qi + 2 * HEAD) & (kpos >= 0) & (kpos < S)
    s = jnp.where(valid, s, MASK_VALUE)
    m = jnp.maximum(jnp.max(s, axis=-1, keepdims=True), sink_of_row)
    p = jnp.exp(s - m)
    e_sink = jnp.exp(sink_of_row - m)
    inv = 1.0 / (jnp.sum(p, axis=-1, keepdims=True) + e_sink)
    return p * inv, e_sink * inv


def _sink_rows(sink_ref, l, j, G):
    hidx = lax.broadcasted_iota(jnp.int32, (G * HEAD, 1), 0) // HEAD
    col = jnp.full((G * HEAD, 1), sink_ref[l, j * G], F32)
    for g in range(1, G):
        col = jnp.where(hidx == g, sink_ref[l, j * G + g], col)
    return col


def _attn_fwd(name, q, k, v, sink, ag3, l, cfg):
    S, AW, KVW, NKV, G = cfg["S"], cfg["AW"], cfg["KVW"], cfg["NKV"], cfg["G"]
    nb = S // HEAD

    def body(q_ref, kp, kc, kn, vp, vc, vn, sink_ref, ag_ref, a_ref, mix_ref):
        n = pl.program_id(0)
        kb = jnp.concatenate([kp[...], kc[...], kn[...]], axis=0)
        vb = jnp.concatenate([vp[...], vc[...], vn[...]], axis=0)
        for j in range(NKV):
            sl = slice(j * HEAD, (j + 1) * HEAD)
            qs = jnp.concatenate([q_ref[:, (j * G + g) * HEAD:(j * G + g + 1) * HEAD] for g in range(G)], axis=0)
            probs, _ = _attn_probs(qs, kb[:, sl], n, _sink_rows(sink_ref, l, j, G), S, G)
            o = jnp.dot(probs.astype(BF16), vb[:, sl], preferred_element_type=F32)
            for g in range(G):
                a_ref[:, (j * G + g) * HEAD:(j * G + g + 1) * HEAD] = o[g * HEAD:(g + 1) * HEAD]
        a = a_ref[...]
        r = lax.rsqrt(jnp.mean(a * a, axis=-1, keepdims=True) + EPS)
        mix_ref[...] = ((a * r) * ag_ref[...]).astype(BF16)

    return pl.pallas_call(
        body, name=name, grid=(nb,),
        in_specs=[pl.BlockSpec((HEAD, AW), lambda n: (n, 0))] + _band_specs(KVW, nb) + _band_specs(KVW, nb)
                 + [pl.BlockSpec(memory_space=pltpu.SMEM), _vec_spec(l, AW)],
        out_specs=[pl.BlockSpec((HEAD, AW), lambda n: (n, 0)), pl.BlockSpec((HEAD, AW), lambda n: (n, 0))],
        out_shape=[jax.ShapeDtypeStruct((S, AW), F32), jax.ShapeDtypeStruct((S, AW), BF16)],
        compiler_params=_params(("arbitrary",)),
    )(q, k, k, k, v, v, v, sink, ag3)


def _sgu_forward_math(gu, gv, lng, lnb):
    u = _gelu(gu)
    vv = _gelu(gv)
    mu = jnp.mean(vv, axis=-1, keepdims=True)
    xc = vv - mu
    rstd = lax.rsqrt(jnp.mean(xc * xc, axis=-1, keepdims=True) + EPS)
    xhat = xc * rstd
    vn = xhat * lng + lnb
    return u, xhat, rstd, vn


def _sgu_fwd(name, z, lng3, lnb3, ws_b, bs_b, og3, l, cfg):
    S, GW, NG = cfg["S"], cfg["GW"], cfg["NG"]

    def body(gu_ref, gv_ref, lng_ref, lnb_ref, ws_ref, bs_ref, og_ref, mix_ref, sg_ref):
        u, _, _, vn = _sgu_forward_math(gu_ref[...], gv_ref[...], lng_ref[...], lnb_ref[...])
        vnb = vn.astype(BF16)
        for h in range(NG):
            sl = slice(h * HEAD, (h + 1) * HEAD)
            f = jnp.dot(ws_ref[h], vnb[:, sl], preferred_element_type=F32) + bs_ref[h]
            sg_ref[:, sl] = u[:, sl] * f
        sg = sg_ref[...]
        r = lax.rsqrt(jnp.mean(sg * sg, axis=-1, keepdims=True) + EPS)
        mix_ref[...] = ((sg * r) * og_ref[...]).astype(BF16)

    return pl.pallas_call(
        body, name=name, grid=(S // HEAD,),
        in_specs=[pl.BlockSpec((HEAD, GW), lambda c: (c, 1)),
                  pl.BlockSpec((HEAD, GW), lambda c: (c, 2)),
                  _vec_spec(l, GW), _vec_spec(l, GW),
                  pl.BlockSpec((None, NG, HEAD, HEAD), lambda c: (l, 0, 0, 0)),
                  pl.BlockSpec((None, NG, HEAD, HEAD), lambda c: (l, 0, 0, 0)),
                  _vec_spec(l, GW)],
        out_specs=pl.BlockSpec((HEAD, GW), lambda c: (c, 0)),
        out_shape=jax.ShapeDtypeStruct((S, GW), BF16),
        scratch_shapes=[pltpu.VMEM((HEAD, GW), F32)],
        compiler_params=_params(("arbitrary",)),
    )(z, z, lng3, lnb3, ws_b, bs_b, og3)


CONV_HALO = 8
CONV_ROWS = 128
CONV_COLS = 512


def _for_row_windows(S, fn):
    R, W = CONV_ROWS, CONV_ROWS + 2 * CONV_HALO
    n = S // R
    assert n * R == S and n >= 2
    fn(0, 0, 0, "top")
    if n > 2:
        def mid(k, carry):
            fn(pl.multiple_of(k * R - CONV_HALO, CONV_HALO), CONV_HALO, pl.multiple_of(k * R, R), "mid")
            return carry

        lax.fori_loop(1, n - 1, mid, 0, unroll=2 if (n - 2) % 2 == 0 else 1)
    fn(S - W, 2 * CONV_HALO, S - R, "bottom")


def _shifts(t, edge):
    W = t.shape[0]
    dn, up = pltpu.roll(t, 1, axis=0), pltpu.roll(t, W - 1, axis=0)
    if edge == "top":
        dn = jnp.where(lax.broadcasted_iota(jnp.int32, t.shape, 0) == 0, 0.0, dn)
    if edge == "bottom":
        up = jnp.where(lax.broadcasted_iota(jnp.int32, t.shape, 0) == W - 1, 0.0, up)
    return dn, up


def _conv3(t, w, b, edge):
    dn, up = _shifts(t, edge)
    return ((b + dn * w[0:1]) + t * w[1:2]) + up * w[2:3]


def _conv_glu_fwd(name, ap, cw, cb3, l, cfg):
    S, F = cfg["S"], cfg["F"]
    tc = _tile(F, CONV_COLS, 128)
    nf = F // tc
    R, W = CONV_ROWS, CONV_ROWS + 2 * CONV_HALO

    def body(g_ref, u_ref, wg_ref, wu_ref, bg_ref, bu_ref, y_ref):
        def window(start, lo, out0, edge):
            rows = pl.ds(start, W)
            for c0 in range(0, tc, 128):
                cols = slice(c0, c0 + 128)
                ag = _conv3(g_ref[rows, cols], wg_ref[:, cols], bg_ref[:, cols], edge)
                au = _conv3(u_ref[rows, cols], wu_ref[:, cols], bu_ref[:, cols], edge)
                y = (ag * _sigmoid(ag)) * au
                y_ref[pl.ds(out0, R), cols] = y[lo:lo + R].astype(BF16)

        _for_row_windows(S, window)

    return pl.pallas_call(
        body, name=name, grid=(nf,),
        in_specs=[pl.BlockSpec((S, tc), lambda j: (0, j)),
                  pl.BlockSpec((S, tc), lambda j: (0, j + nf)),
                  pl.BlockSpec((None, 3, tc), lambda j: (l, 0, j)),
                  pl.BlockSpec((None, 3, tc), lambda j: (l, 0, j + nf)),
                  pl.BlockSpec((None, 1, tc), lambda j: (l, 0, j)),
                  pl.BlockSpec((None, 1, tc), lambda j: (l, 0, j + nf))],
        out_specs=pl.BlockSpec((S, tc), lambda j: (0, j)),
        out_shape=jax.ShapeDtypeStruct((S, F), BF16),
        compiler_params=_params(("arbitrary",)),
    )(ap, ap, cw, cw, cb3, cb3)


def _loss_bwd(name, y, target):
    S, D = y.shape
    tr = _tile(S, 256, 16)

    def body(y_ref, t_ref, loss_ref, d_ref, db_ref):
        @pl.when(pl.program_id(0) == 0)
        def _():
            loss_ref[...] = jnp.zeros_like(loss_ref)

        err = y_ref[...] - t_ref[...]
        part = 0.5 * jnp.sum(jnp.mean(err * err, axis=-1, keepdims=True), axis=0, keepdims=True)
        loss_ref[...] += jnp.broadcast_to(part, loss_ref.shape)
        d = err * (1.0 / D)
        d_ref[...] = d
        db_ref[...] = d.astype(BF16)

    return pl.pallas_call(
        body, name=name, grid=(S // tr,),
        in_specs=[pl.BlockSpec((tr, D), lambda i: (i, 0)), pl.BlockSpec((tr, D), lambda i: (i, 0))],
        out_specs=[pl.BlockSpec((8, 128), lambda i: (0, 0)),
                   pl.BlockSpec((tr, D), lambda i: (i, 0)),
                   pl.BlockSpec((tr, D), lambda i: (i, 0))],
        out_shape=[jax.ShapeDtypeStruct((8, 128), F32),
                   jax.ShapeDtypeStruct((S, D), F32),
                   jax.ShapeDtypeStruct((S, D), BF16)],
        compiler_params=_params(("arbitrary",)),
    )(y, target)


def _rms_bwd(name, x, dh, dres, g3, l):
    S, D = x.shape
    tr = _tile(S, 256, 16)

    def body(x_ref, dh_ref, dr_ref, g_ref, dx_ref, dxb_ref, dg_ref):
        @pl.when(pl.program_id(0) == 0)
        def _():
            dg_ref[...] = jnp.zeros_like(dg_ref)

        xv = x_ref[...]
        dhv = dh_ref[...]
        r = lax.rsqrt(jnp.mean(xv * xv, axis=-1, keepdims=True) + EPS)
        xhat = xv * r
        dhg = dhv * g_ref[...]
        dx = dr_ref[...] + r * (dhg - xhat * jnp.mean(dhg * xhat, axis=-1, keepdims=True))
        dx_ref[...] = dx
        dxb_ref[...] = dx.astype(BF16)
        dg_ref[...] += jnp.sum(dhv * xhat, axis=0, keepdims=True)

    return pl.pallas_call(
        body, name=name, grid=(S // tr,),
        in_specs=[pl.BlockSpec((tr, D), lambda i: (i, 0)), pl.BlockSpec((tr, D), lambda i: (i, 0)),
                  pl.BlockSpec((tr, D), lambda i: (i, 0)), _vec_spec(l, D)],
        out_specs=[pl.BlockSpec((tr, D), lambda i: (i, 0)), pl.BlockSpec((tr, D), lambda i: (i, 0)),
                   pl.BlockSpec((1, D), lambda i: (0, 0))],
        out_shape=[jax.ShapeDtypeStruct((S, D), F32), jax.ShapeDtypeStruct((S, D), BF16),
                   jax.ShapeDtypeStruct((1, D), F32)],
        compiler_params=_params(("arbitrary",)),
    )(x, dh, dres, g3)


def _glu_conv_bwd(name, dy, ap, cw, cb3, l, cfg):
    S, F = cfg["S"], cfg["F"]
    tc = _tile(F, CONV_COLS, 128)
    nf = F // tc
    R, W = CONV_ROWS, CONV_ROWS + 2 * CONV_HALO

    def body(dy_ref, g_ref, u_ref, wg_ref, wu_ref, bg_ref, bu_ref, dap_ref, dw_ref, db_ref, acc):
        acc[...] = jnp.zeros_like(acc)

        def window(start, lo, out0, edge):
            rows = pl.ds(start, W)
            for c0 in range(0, tc, 128):
                cols = slice(c0, c0 + 128)
                apg, apu, dyv = g_ref[rows, cols], u_ref[rows, cols], dy_ref[rows, cols]
                wg, wu = wg_ref[:, cols], wu_ref[:, cols]
                ag = _conv3(apg, wg, bg_ref[:, cols], edge)
                au = _conv3(apu, wu, bu_ref[:, cols], edge)
                sig = _sigmoid(ag)
                da_u = dyv * (ag * sig)
                da_g = (dyv * au) * (sig * (1.0 + ag * (1.0 - sig)))

                def add(i, prod):
                    acc[i, :, cols] += jnp.sum(prod[lo:lo + R].reshape(R // 8, 8, 128), axis=0)

                for half, (da, t, w) in enumerate([(da_g, apg, wg), (da_u, apu, wu)]):
                    prv, nxt = _shifts(da, edge)
                    dap = (nxt * w[0:1] + da * w[1:2]) + prv * w[2:3]
                    dap_ref[half, pl.ds(out0, R), cols] = dap[lo:lo + R].astype(BF16)
                    add(3 * half, nxt * t)
                    add(3 * half + 1, da * t)
                    add(3 * half + 2, prv * t)
                    add(6 + half, da)

        _for_row_windows(S, window)
        col = [jnp.sum(acc[i], axis=0, keepdims=True) for i in range(8)]
        dw_ref[0] = jnp.concatenate(col[0:3], axis=0)
        dw_ref[1] = jnp.concatenate(col[3:6], axis=0)
        db_ref[0] = col[6]
        db_ref[1] = col[7]

    return pl.pallas_call(
        body, name=name, grid=(nf,),
        in_specs=[pl.BlockSpec((S, tc), lambda j: (0, j)),
                  pl.BlockSpec((S, tc), lambda j: (0, j)),
                  pl.BlockSpec((S, tc), lambda j: (0, j + nf)),
                  pl.BlockSpec((None, 3, tc), lambda j: (l, 0, j)),
                  pl.BlockSpec((None, 3, tc), lambda j: (l, 0, j + nf)),
                  pl.BlockSpec((None, 1, tc), lambda j: (l, 0, j)),
                  pl.BlockSpec((None, 1, tc), lambda j: (l, 0, j + nf))],
        out_specs=[pl.BlockSpec((2, S, tc), lambda j: (0, 0, j)),
                   pl.BlockSpec((2, 3, tc), lambda j: (0, 0, j)),
                   pl.BlockSpec((2, 1, tc), lambda j: (0, 0, j))],
        out_shape=[jax.ShapeDtypeStruct((2, S, F), BF16),
                   jax.ShapeDtypeStruct((2, 3, F), F32),
                   jax.ShapeDtypeStruct((2, 1, F), F32)],
        scratch_shapes=[pltpu.VMEM((8, 8, tc), F32)],
        compiler_params=_params(("arbitrary",)),
    )(dy, ap, ap, cw, cw, cb3, cb3)


def _attn_bwd(name, q, k, v, attn, dmix, sink, ag3, l, cfg):
    S, AW, KVW, NQ, NKV, G = cfg["S"], cfg["AW"], cfg["KVW"], cfg["NQ"], cfg["NKV"], cfg["G"]
    nb = S // HEAD
    scale = HEAD ** -0.5

    def body(q_ref, kp, kc, kn, vp, vc, vn, a_ref, dm_ref, sink_ref, ag_ref,
             dq_ref, dk_ref, dv_ref, dsink_ref, dag_ref, da_scr):
        n = pl.program_id(0)

        @pl.when(n == 0)
        def _():
            dk_ref[...] = jnp.zeros_like(dk_ref)
            dv_ref[...] = jnp.zeros_like(dv_ref)
            dsink_ref[...] = jnp.zeros_like(dsink_ref)
            dag_ref[...] = jnp.zeros_like(dag_ref)

        a = a_ref[...]
        dm = dm_ref[...]
        r = lax.rsqrt(jnp.mean(a * a, axis=-1, keepdims=True) + EPS)
        xhat = a * r
        dmg = dm * ag_ref[...]
        da_scr[...] = r * (dmg - xhat * jnp.mean(dmg * xhat, axis=-1, keepdims=True))
        dag_ref[...] += jnp.sum(dm * xhat, axis=0, keepdims=True)

        kb = jnp.concatenate([kp[...], kc[...], kn[...]], axis=0)
        vb = jnp.concatenate([vp[...], vc[...], vn[...]], axis=0)
        band = pl.ds(pl.multiple_of(n * HEAD, HEAD), 3 * HEAD)
        for j in range(NKV):
            sl = slice(j * HEAD, (j + 1) * HEAD)
            heads = [slice((j * G + g) * HEAD, (j * G + g + 1) * HEAD) for g in range(G)]
            qs = jnp.concatenate([q_ref[:, hs] for hs in heads], axis=0)
            do = jnp.concatenate([da_scr[:, hs] for hs in heads], axis=0)
            kj, vj = kb[:, sl], vb[:, sl]
            probs, p_sink = _attn_probs(qs, kj, n, _sink_rows(sink_ref, l, j, G), S, G)
            dob = do.astype(BF16)
            dprobs = lax.dot_general(dob, vj, (((1,), (1,)), ((), ())), preferred_element_type=F32)
            delta = jnp.sum(dprobs * probs, axis=-1, keepdims=True)
            ds = (probs * (dprobs - delta)) * scale
            dsb = ds.astype(BF16)
            dsk = -(p_sink * delta)
            dq = jnp.dot(dsb, kj, preferred_element_type=F32)
            for g in range(G):
                dq_ref[:, heads[g]] = dq[g * HEAD:(g + 1) * HEAD]
                part = jnp.sum(dsk[g * HEAD:(g + 1) * HEAD], axis=0, keepdims=True)
                dsink_ref[j * G + g:j * G + g + 1, :] += jnp.broadcast_to(part, (1, HEAD))
            dk_ref[band, sl] += lax.dot_general(dsb, qs, (((0,), (0,)), ((), ())), preferred_element_type=F32)
            dv_ref[band, sl] += lax.dot_general(probs.astype(BF16), dob, (((0,), (0,)), ((), ())),
                                                preferred_element_type=F32)

    return pl.pallas_call(
        body, name=name, grid=(nb,),
        in_specs=[pl.BlockSpec((HEAD, AW), lambda n: (n, 0))] + _band_specs(KVW, nb) + _band_specs(KVW, nb)
                 + [pl.BlockSpec((HEAD, AW), lambda n: (n, 0)),
                    pl.BlockSpec((HEAD, AW), lambda n: (n, 0)),
                    pl.BlockSpec(memory_space=pltpu.SMEM), _vec_spec(l, AW)],
        out_specs=[pl.BlockSpec((HEAD, AW), lambda n: (n, 0)),
                   pl.BlockSpec((S + 2 * HEAD, KVW), lambda n: (0, 0)),
                   pl.BlockSpec((S + 2 * HEAD, KVW), lambda n: (0, 0)),
                   pl.BlockSpec((NQ, HEAD), lambda n: (0, 0)),
                   pl.BlockSpec((1, AW), lambda n: (0, 0))],
        out_shape=[jax.ShapeDtypeStruct((S, AW), F32),
                   jax.ShapeDtypeStruct((S + 2 * HEAD, KVW), F32),
                   jax.ShapeDtypeStruct((S + 2 * HEAD, KVW), F32),
                   jax.ShapeDtypeStruct((NQ, HEAD), F32),
                   jax.ShapeDtypeStruct((1, AW), F32)],
        scratch_shapes=[pltpu.VMEM((HEAD, AW), F32)],
        compiler_params=_params(("arbitrary",)),
    )(q, k, k, k, v, v, v, attn, dmix, sink, ag3)


def _qkv_prep_bwd(name, z, dq, dk_pad, dv_pad, qg3, kg3, cos2, sin2, l, cfg):
    S, AW, KVW, NQ, NKV = cfg["S"], cfg["AW"], cfg["KVW"], cfg["NQ"], cfg["NKV"]
    kv_blk = (3 * AW) // (2 * KVW)

    def body(zq_ref, zkv_ref, dq_ref, dk_ref, dv_ref, qg_ref, kg_ref, c_ref, s_ref,
             dzq_ref, dzkv_ref, dqg_ref, dkg_ref):
        @pl.when(pl.program_id(0) == 0)
        def _():
            dqg_ref[...] = jnp.zeros_like(dqg_ref)
            dkg_ref[...] = jnp.zeros_like(dkg_ref)

        cosv, sinv = c_ref[...], s_ref[...]

        def back(t, dr, g):
            r = lax.rsqrt(jnp.mean(t * t, axis=-1, keepdims=True) + EPS)
            xhat = t * r
            dn = dr * cosv + pltpu.roll(dr * sinv, HEAD // 2, axis=1)
            dxh = dn * g
            dt = r * (dxh - xhat * jnp.mean(dxh * xhat, axis=-1, keepdims=True))
            return dt, jnp.sum(dn * xhat, axis=0, keepdims=True)

        gq = jnp.zeros((1, HEAD), F32)
        for h in range(NQ):
            sl = slice(h * HEAD, (h + 1) * HEAD)
            dt, gpart = back(zq_ref[:, sl], dq_ref[:, sl], qg_ref[...])
            dzq_ref[:, sl] = dt.astype(BF16)
            gq = gq + gpart
        dqg_ref[...] += gq
        gk = jnp.zeros((1, HEAD), F32)
        for h in range(NKV):
            sl = slice(h * HEAD, (h + 1) * HEAD)
            dt, gpart = back(zkv_ref[:, sl], dk_ref[:, sl], kg_ref[...])
            dzkv_ref[:, sl] = dt.astype(BF16)
            gk = gk + gpart
        dkg_ref[...] += gk
        dzkv_ref[:, KVW:] = dv_ref[...].astype(BF16)

    return pl.pallas_call(
        body, name=name, grid=(S // HEAD,),
        in_specs=[pl.BlockSpec((HEAD, AW), lambda i: (i, 0)),
                  pl.BlockSpec((HEAD, 2 * KVW), lambda i: (i, kv_blk)),
                  pl.BlockSpec((HEAD, AW), lambda i: (i, 0)),
                  pl.BlockSpec((HEAD, KVW), lambda i: (i + 1, 0)),
                  pl.BlockSpec((HEAD, KVW), lambda i: (i + 1, 0)),
                  _vec_spec(l, HEAD), _vec_spec(l, HEAD),
                  pl.BlockSpec((HEAD, HEAD), lambda i: (i, 0)),
                  pl.BlockSpec((HEAD, HEAD), lambda i: (i, 0))],
        out_specs=[pl.BlockSpec((HEAD, AW), lambda i: (i, 0)),
                   pl.BlockSpec((HEAD, 2 * KVW), lambda i: (i, 0)),
                   pl.BlockSpec((1, HEAD), lambda i: (0, 0)),
                   pl.BlockSpec((1, HEAD), lambda i: (0, 0))],
        out_shape=[jax.ShapeDtypeStruct((S, AW), BF16),
                   jax.ShapeDtypeStruct((S, 2 * KVW), BF16),
                   jax.ShapeDtypeStruct((1, HEAD), F32),
                   jax.ShapeDtypeStruct((1, HEAD), F32)],
        compiler_params=_params(("arbitrary",)),
    )(z, z, dq, dk_pad, dv_pad, qg3, kg3, cos2, sin2)


def _sgu_bwd(name, z, dmix, lng3, lnb3, ws_b, wst_b, bs_b, og3, l, cfg):
    S, GW, NG = cfg["S"], cfg["GW"], cfg["NG"]

    def body(gu_ref, gv_ref, dm_ref, lng_ref, lnb_ref, ws_ref, wst_ref, bs_ref, og_ref,
             dgu_ref, dgv_ref, dws_ref, dbs_ref, dlng_ref, dlnb_ref, dog_ref, sg_scr, f_scr, dvn_scr):
        @pl.when(pl.program_id(0) == 0)
        def _():
            dws_ref[...] = jnp.zeros_like(dws_ref)
            dbs_ref[...] = jnp.zeros_like(dbs_ref)
            dlng_ref[...] = jnp.zeros_like(dlng_ref)
            dlnb_ref[...] = jnp.zeros_like(dlnb_ref)
            dog_ref[...] = jnp.zeros_like(dog_ref)

        gu, gv = gu_ref[...], gv_ref[...]
        lng = lng_ref[...]
        u, xhat, rstd, vn = _sgu_forward_math(gu, gv, lng, lnb_ref[...])
        vnb = vn.astype(BF16)
        for h in range(NG):
            sl = slice(h * HEAD, (h + 1) * HEAD)
            f = jnp.dot(ws_ref[h], vnb[:, sl], preferred_element_type=F32) + bs_ref[h]
            f_scr[:, sl] = f
            sg_scr[:, sl] = u[:, sl] * f
        sg = sg_scr[...]
        dm = dm_ref[...]
        r = lax.rsqrt(jnp.mean(sg * sg, axis=-1, keepdims=True) + EPS)
        sghat = sg * r
        dmg = dm * og_ref[...]
        dsg = r * (dmg - sghat * jnp.mean(dmg * sghat, axis=-1, keepdims=True))
        dog_ref[...] += jnp.sum(dm * sghat, axis=0, keepdims=True)
        du = dsg * f_scr[...]
        df = dsg * u
        dfb = df.astype(BF16)
        for h in range(NG):
            sl = slice(h * HEAD, (h + 1) * HEAD)
            dvn_scr[:, sl] = jnp.dot(wst_ref[h], dfb[:, sl], preferred_element_type=F32)
            dws_ref[h] += lax.dot_general(dfb[:, sl], vnb[:, sl], (((1,), (1,)), ((), ())),
                                          preferred_element_type=F32)
            dbs_ref[h] += jnp.broadcast_to(jnp.sum(df[:, sl], axis=-1, keepdims=True), (HEAD, HEAD))
        dvn = dvn_scr[...]
        dlng_ref[...] += jnp.sum(dvn * xhat, axis=0, keepdims=True)
        dlnb_ref[...] += jnp.sum(dvn, axis=0, keepdims=True)
        dxh = dvn * lng
        dvv = rstd * ((dxh - jnp.mean(dxh, axis=-1, keepdims=True))
                      - xhat * jnp.mean(dxh * xhat, axis=-1, keepdims=True))
        dgu_ref[...] = (du * _gelu_grad(gu)).astype(BF16)
        dgv_ref[...] = (dvv * _gelu_grad(gv)).astype(BF16)

    vec = pl.BlockSpec((1, GW), lambda c: (0, 0))
    mat = pl.BlockSpec((NG, HEAD, HEAD), lambda c: (0, 0, 0))
    wsp = pl.BlockSpec((None, NG, HEAD, HEAD), lambda c: (l, 0, 0, 0))
    return pl.pallas_call(
        body, name=name, grid=(S // HEAD,),
        in_specs=[pl.BlockSpec((HEAD, GW), lambda c: (c, 1)),
                  pl.BlockSpec((HEAD, GW), lambda c: (c, 2)),
                  pl.BlockSpec((HEAD, GW), lambda c: (c, 1)),
                  _vec_spec(l, GW), _vec_spec(l, GW), wsp, wsp, wsp, _vec_spec(l, GW)],
        out_specs=[pl.BlockSpec((HEAD, GW), lambda c: (c, 0)), pl.BlockSpec((HEAD, GW), lambda c: (c, 0)),
                   mat, mat, vec, vec, vec],
        out_shape=[jax.ShapeDtypeStruct((S, GW), BF16), jax.ShapeDtypeStruct((S, GW), BF16),
                   jax.ShapeDtypeStruct((NG, HEAD, HEAD), F32), jax.ShapeDtypeStruct((NG, HEAD, HEAD), F32),
                   jax.ShapeDtypeStruct((1, GW), F32), jax.ShapeDtypeStruct((1, GW), F32),
                   jax.ShapeDtypeStruct((1, GW), F32)],
        scratch_shapes=[pltpu.VMEM((HEAD, GW), F32), pltpu.VMEM((HEAD, GW), F32), pltpu.VMEM((HEAD, GW), F32)],
        compiler_params=_params(("arbitrary",)),
    )(z, z, dmix, lng3, lnb3, ws_b, wst_b, bs_b, og3)


def _mesh_pos():
    x, y, c = lax.axis_index("x"), lax.axis_index("y"), lax.axis_index("c")
    return x, y, c


def _dev_index(p):
    return 4 * p[0] + 2 * p[1] + p[2]


def _handshake(peers):
    barrier = pltpu.get_barrier_semaphore()
    for p in peers:
        pl.semaphore_signal(barrier, inc=1, device_id=p, device_id_type=MESH)
    pl.semaphore_wait(barrier, len(peers))


GATHER_COPIES = 10


def _gather_body(full, half, in_half):
    def body(in_ref, out_ref, send_sems, recv_sems, local_sem):
        x, y, c = _mesh_pos()
        me, sib = (x, y, c), (x, y, 1 - c)
        xn, yn, dg = (1 - x, y, c), (x, 1 - y, c), (1 - x, 1 - y, c)
        _handshake([sib, xn, yn])

        def copy(k, dst, to, src=None):
            return pltpu.make_async_remote_copy(
                src_ref=dst if src is None else src, dst_ref=dst,
                send_sem=send_sems.at[k], recv_sem=recv_sems.at[k], device_id=to, device_id_type=MESH)

        def win(dev, h=None):
            idx = _dev_index(dev)
            return full(out_ref, idx) if h is None else half(out_ref, idx, h)

        local = pltpu.make_async_copy(in_ref, win(me), local_sem)
        local.start()
        sends = [copy(0, win(me), sib, src=in_ref),
                 copy(1, win(me, 0), xn, src=in_half(in_ref, 0)),
                 copy(2, win(me, 1), yn, src=in_half(in_ref, 1)),
                 copy(4, win(me, 0), yn, src=in_half(in_ref, 0)),
                 copy(6, win(me, 1), xn, src=in_half(in_ref, 1))]
        for cp in sends:
            cp.start()
        copy(1, win(xn, 0), me).wait_recv()
        sends.append(copy(3, win(xn, 0), yn))
        sends[-1].start()
        copy(2, win(yn, 1), me).wait_recv()
        sends.append(copy(5, win(yn, 1), xn))
        sends[-1].start()
        copy(6, win(xn, 1), me).wait_recv()
        sends.append(copy(7, win(xn), sib))
        sends[-1].start()
        copy(4, win(yn, 0), me).wait_recv()
        sends.append(copy(8, win(yn), sib))
        sends[-1].start()
        copy(3, win(dg, 0), me).wait_recv()
        copy(5, win(dg, 1), me).wait_recv()
        sends.append(copy(9, win(dg), sib))
        sends[-1].start()
        sib_xn, sib_yn, sib_dg = (1 - x, y, 1 - c), (x, 1 - y, 1 - c), (1 - x, 1 - y, 1 - c)
        for k, dev in [(0, sib), (7, sib_xn), (8, sib_yn), (9, sib_dg)]:
            copy(k, win(dev), me).wait_recv()
        for cp in sends:
            cp.wait_send()
        local.wait()

    return body


def _all_gather(name, cid, shard, out_shape, windows):
    return pl.kernel(
        _gather_body(*windows), out_type=out_shape,
        mesh=plsc.ScalarSubcoreMesh(axis_name="seq", num_cores=1), name=name,
        scratch_types=[pltpu.SemaphoreType.DMA((GATHER_COPIES,)), pltpu.SemaphoreType.DMA((GATHER_COPIES,)),
                       pltpu.SemaphoreType.DMA],
        compiler_params=pltpu.CompilerParams(collective_id=cid),
    )(shard)


def _d2d_body(nb, ns):
    n = nb + ns

    def body(*refs):
        ins, outs = refs[:n], refs[n:2 * n]
        send_sems, recv_sems = refs[2 * n:]
        x, y, c = _mesh_pos()
        sib = (x, y, 1 - c)
        _handshake([sib])
        copies = []
        for t in range(n):
            cp = pltpu.make_async_remote_copy(
                src_ref=ins[t].at[:, 1 - c] if t < nb else ins[t], dst_ref=outs[t],
                send_sem=send_sems.at[t], recv_sem=recv_sems.at[t],
                device_id=sib, device_id_type=MESH)
            cp.start()
            copies.append(cp)
        for cp in copies:
            cp.wait()

    return body


def _rs_d2d(name, cid, bigs, smalls):
    shapes = [jax.ShapeDtypeStruct((g.shape[0],) + g.shape[2:], g.dtype) for g in bigs]
    shapes += [jax.ShapeDtypeStruct(s.shape, s.dtype) for s in smalls]
    n = len(shapes)
    return pl.kernel(
        _d2d_body(len(bigs), len(smalls)), out_type=shapes,
        mesh=plsc.ScalarSubcoreMesh(axis_name="seq", num_cores=1), name=name,
        scratch_types=[pltpu.SemaphoreType.DMA((n,)), pltpu.SemaphoreType.DMA((n,))],
        compiler_params=pltpu.CompilerParams(collective_id=cid),
    )(*bigs, *smalls)


def _pair_sum(name, g4, recv, cvec, row_mult):
    Q, _, R, C = g4.shape
    tr = _tile(R, 256, row_mult)

    def body(c_ref, a_ref, b_ref, o_ref):
        o_ref[...] = (a_ref[...].astype(F32) + b_ref[...].astype(F32)).astype(BF16)

    grid_spec = pltpu.PrefetchScalarGridSpec(
        num_scalar_prefetch=1, grid=(Q, R // tr),
        in_specs=[pl.BlockSpec((None, None, tr, C), lambda q, i, c_ref: (q, c_ref[0], i, 0)),
                  pl.BlockSpec((None, tr, C), lambda q, i, c_ref: (q, i, 0))],
        out_specs=pl.BlockSpec((None, tr, C), lambda q, i, c_ref: (q, i, 0)))
    return pl.pallas_call(
        body, name=name, grid_spec=grid_spec, out_shape=jax.ShapeDtypeStruct((Q, R, C), BF16),
        compiler_params=_params(("arbitrary", "arbitrary")),
    )(cvec, g4, recv)


def _add2(name, a, b):
    R, C = a.shape
    tr = _tile(R, 512, 8)

    def body(a_ref, b_ref, o_ref):
        o_ref[...] = a_ref[...] + b_ref[...]

    blk = pl.BlockSpec((tr, C), lambda i: (i, 0))
    return pl.pallas_call(body, name=name, grid=(R // tr,), in_specs=[blk, blk], out_specs=blk,
                          out_shape=jax.ShapeDtypeStruct((R, C), a.dtype),
                          compiler_params=_params(("arbitrary",)))(a, b)


def _ici_copies(nb, n, srcs, lands, send_sems, recv_sems):
    x, y, c = _mesh_pos()
    q_me = 2 * x + y
    copies = []
    for t in range(n):
        for k in range(1, 4):
            px, py = x ^ (k >> 1), y ^ (k & 1)
            copies.append(pltpu.make_async_remote_copy(
                src_ref=srcs[t].at[2 * px + py] if t < nb else srcs[t], dst_ref=lands[t].at[q_me],
                send_sem=send_sems.at[3 * t + k - 1], recv_sem=recv_sems.at[3 * t + k - 1],
                device_id=(px, py, c), device_id_type=MESH))
    return copies


_HBM = pl.BlockSpec(memory_space=pltpu.HBM)
_SEM = pl.BlockSpec(memory_space=pltpu.SEMAPHORE)
_DATAFLOW = pltpu.SideEffectType.DATAFLOW_SIDE_EFFECTING


def _ici_start(name, bigs, smalls):
    nb, n = len(bigs), len(bigs) + len(smalls)
    srcs = list(bigs) + list(smalls)
    lands = [lax.empty(g.shape, g.dtype) for g in bigs] + [lax.empty((N_DEV // 2,) + s.shape, s.dtype) for s in smalls]

    def body(*refs):
        src_refs, land_refs = refs[:n], refs[n:2 * n]
        send_sems, recv_sems = refs[2 * n], refs[2 * n + 1]
        token = refs[-1]
        for cp in _ici_copies(nb, n, src_refs, land_refs, send_sems, recv_sems):
            cp.start()
        token[...] = jnp.zeros_like(token)

    hbm = [pltpu.HBM(a.shape, a.dtype) for a in srcs + lands]
    args = [pltpu.with_memory_space_constraint(a, pltpu.HBM) for a in srcs + lands]
    out = pl.pallas_call(
        body, name=name,
        out_shape=[pltpu.SemaphoreType.DMA((3 * n,)), pltpu.SemaphoreType.DMA((3 * n,))] + hbm
                  + [jax.ShapeDtypeStruct((8, 128), F32)],
        in_specs=[_HBM] * (2 * n),
        out_specs=[_SEM, _SEM] + [_HBM] * (2 * n) + [pl.BlockSpec(memory_space=pltpu.VMEM)],
        input_output_aliases={i: 2 + i for i in range(2 * n)},
        compiler_params=pltpu.CompilerParams(has_side_effects=_DATAFLOW),
    )(*args)
    return out[0], out[1], list(out[2:2 + n]), list(out[2 + n:2 + 2 * n]), out[-1]


def _ici_wait(name, started, nb, not_before):
    send_sems, recv_sems, srcs, lands, token = started
    n = len(srcs)

    def body(*refs):
        src_refs, land_refs = refs[:n], refs[n:2 * n]
        send_refs, recv_refs = refs[2 * n], refs[2 * n + 1]
        for cp in _ici_copies(nb, n, src_refs, land_refs, send_refs, recv_refs):
            cp.wait_send()
            cp.wait_recv()

    hbm = [pltpu.HBM(a.shape, a.dtype) for a in srcs + lands]
    out = pl.pallas_call(
        body, name=name, out_shape=hbm,
        in_specs=[_HBM] * (2 * n) + [_SEM, _SEM] + [pl.BlockSpec(memory_space=pl.ANY)] * 2,
        out_specs=[_HBM] * (2 * n),
        input_output_aliases={i: i for i in range(2 * n)},
        compiler_params=pltpu.CompilerParams(has_side_effects=_DATAFLOW),
    )(*srcs, *lands, send_sems, recv_sems, not_before, token)
    return list(out[:n]), list(out[n:])


def _adamw_math(w, g, m, v):
    m2 = ADAM_B1 * m + (1.0 - ADAM_B1) * g
    v2 = ADAM_B2 * v + (1.0 - ADAM_B2) * (g * g)
    m_hat = m2 / (1.0 - ADAM_B1 ** ADAM_STEP)
    v_hat = v2 / (1.0 - ADAM_B2 ** ADAM_STEP)
    delta = -ADAM_LR * (m_hat / (jnp.sqrt(v_hat) + ADAM_EPS) + ADAM_WD * w)
    return delta, m2, v2


def _chip_terms(q, own, land_ref, n):
    return [jnp.where(q == s, own, land_ref[s]).astype(F32) for s in range(n)]


def _adamw_sum(name, land, own, qvec, w, m, v, l, prev, row_mult):
    NS, R, C = land.shape
    L = w.shape[0]
    tr = _tile(R, 128, row_mult)

    def body(*refs):
        q_ref, r_ref, o_ref, w_ref, m_ref, v_ref = refs[:6]
        g_ref, d_ref, nm_ref, nv_ref = refs[-4:]
        terms = _chip_terms(q_ref[0], o_ref[...], r_ref, NS)
        g = terms[0]
        for t in terms[1:]:
            g = g + t
        d, m2, v2 = _adamw_math(w_ref[...], g, m_ref[...], v_ref[...])
        g_ref[...] = g
        d_ref[...] = d
        nm_ref[...] = m2
        nv_ref[...] = v2

    blk = pl.BlockSpec((None, tr, C), lambda i, q: (l, i, 0))
    shp = jax.ShapeDtypeStruct((L, R, C), F32)
    in_specs = [pl.BlockSpec((NS, tr, C), lambda i, q: (0, i, 0)),
                pl.BlockSpec((None, tr, C), lambda i, q: (q[0], i, 0)), blk, blk, blk]
    args = [land, own, w, m, v]
    aliases = {}
    if prev is not None:
        in_specs += [pl.BlockSpec(memory_space=pl.ANY)] * 4
        args += list(prev)
        aliases = {6 + i: i for i in range(4)}
    grid_spec = pltpu.PrefetchScalarGridSpec(num_scalar_prefetch=1, grid=(R // tr,), in_specs=in_specs,
                                             out_specs=[blk, blk, blk, blk])
    return pl.pallas_call(
        body, name=name, grid_spec=grid_spec, out_shape=[shp, shp, shp, shp],
        input_output_aliases=aliases,
        compiler_params=_params(("arbitrary",)),
    )(qvec, *args)


def _adamw_plain(name, g, w, m, v):
    def body(g_ref, w_ref, m_ref, v_ref, d_ref, nm_ref, nv_ref):
        d, m2, v2 = _adamw_math(w_ref[...], g_ref[...], m_ref[...], v_ref[...])
        d_ref[...] = d
        nm_ref[...] = m2
        nv_ref[...] = v2

    shp = jax.ShapeDtypeStruct(g.shape, F32)
    return pl.pallas_call(body, name=name, out_shape=[shp, shp, shp], compiler_params=_params())(g, w, m, v)


SMALL = ["norm1_g", "q_norm_g", "k_norm_g", "sink", "sgu_ln_g", "sgu_ln_b", "w_s", "b_s",
         "attn_out_g", "sgu_out_g", "norm2_g", "conv_b"]
PACK_ALIGN = 1024


def _pack(pieces):
    flat = []
    for p in pieces:
        f = p.reshape(-1).astype(F32)
        pad = (-f.shape[0]) % PACK_ALIGN
        flat.append(jnp.pad(f, (0, pad)) if pad else f)
    return jnp.concatenate(flat).reshape(-1, 128)


def kernel(x, norm1_g, w_in, q_norm_g, k_norm_g, sink, sgu_ln_g, sgu_ln_b, w_s, b_s, attn_out_g, sgu_out_g, w_o, norm2_g, w_up, conv_w, conv_b, w_down, loss_target, m_norm1_g, m_w_in, m_q_norm_g, m_k_norm_g, m_sink, m_sgu_ln_g, m_sgu_ln_b, m_w_s, m_b_s, m_attn_out_g, m_sgu_out_g, m_w_o, m_norm2_g, m_w_up, m_conv_w, m_conv_b, m_w_down, v_norm1_g, v_w_in, v_q_norm_g, v_k_norm_g, v_sink, v_sgu_ln_g, v_sgu_ln_b, v_w_s, v_b_s, v_attn_out_g, v_sgu_out_g, v_w_o, v_norm2_g, v_w_up, v_conv_w, v_conv_b, v_w_down):
    weights = dict(norm1_g=norm1_g, w_in=w_in, q_norm_g=q_norm_g, k_norm_g=k_norm_g, sink=sink, sgu_ln_g=sgu_ln_g,
                   sgu_ln_b=sgu_ln_b, w_s=w_s, b_s=b_s, attn_out_g=attn_out_g, sgu_out_g=sgu_out_g, w_o=w_o,
                   norm2_g=norm2_g, w_up=w_up, conv_w=conv_w, conv_b=conv_b, w_down=w_down)
    mom_m = dict(norm1_g=m_norm1_g, w_in=m_w_in, q_norm_g=m_q_norm_g, k_norm_g=m_k_norm_g, sink=m_sink,
                 sgu_ln_g=m_sgu_ln_g, sgu_ln_b=m_sgu_ln_b, w_s=m_w_s, b_s=m_b_s, attn_out_g=m_attn_out_g,
                 sgu_out_g=m_sgu_out_g, w_o=m_w_o, norm2_g=m_norm2_g, w_up=m_w_up, conv_w=m_conv_w,
                 conv_b=m_conv_b, w_down=m_w_down)
    mom_v = dict(norm1_g=v_norm1_g, w_in=v_w_in, q_norm_g=v_q_norm_g, k_norm_g=v_k_norm_g, sink=v_sink,
                 sgu_ln_g=v_sgu_ln_g, sgu_ln_b=v_sgu_ln_b, w_s=v_w_s, b_s=v_b_s, attn_out_g=v_attn_out_g,
                 sgu_out_g=v_sgu_out_g, w_o=v_w_o, norm2_g=v_norm2_g, w_up=v_w_up, conv_w=v_conv_w,
                 conv_b=v_conv_b, w_down=v_w_down)
    order = ["norm1_g", "w_in", "q_norm_g", "k_norm_g", "sink", "sgu_ln_g", "sgu_ln_b", "w_s", "b_s",
             "attn_out_g", "sgu_out_g", "w_o", "norm2_g", "w_up", "conv_w", "conv_b", "w_down"]

    _, S, D = x.shape
    L = w_in.shape[0]
    AW = D // 2
    NQ = AW // HEAD
    NKV = max(1, NQ // 4)
    G = NQ // NKV
    KVW = NKV * HEAD
    GW = D - AW
    NG = GW // HEAD
    IN = AW + 2 * KVW + 2 * GW
    INS = w_in.shape[2]
    OS = w_o.shape[1]
    US = w_up.shape[2]
    DS = w_down.shape[1]
    F2 = US * N_DEV
    F = F2 // 2
    assert INS * N_DEV == IN and OS * N_DEV == D and DS * N_DEV == F and AW == GW and (3 * AW) % (2 * KVW) == 0
    cfg = dict(S=S, D=D, AW=AW, NQ=NQ, NKV=NKV, G=G, KVW=KVW, GW=GW, NG=NG, F=F, F2=F2)

    def row_windows(n):
        m = n // 2
        return (lambda ref, idx: ref.at[:, pl.ds(pl.multiple_of(idx * n, n), n), :],
                lambda ref, idx, h: ref.at[:, pl.ds(pl.multiple_of(idx * n + h * m, m), m), :],
                lambda ref, h: ref.at[:, pl.ds(h * m, m), :])

    def col_windows(n, rows):
        m = rows // 2
        return (lambda ref, idx: ref.at[:, :, pl.ds(pl.multiple_of(idx * n, n), n)],
                lambda ref, idx, h: ref.at[:, pl.ds(h * m, m), pl.ds(pl.multiple_of(idx * n, n), n)],
                lambda ref, h: ref.at[:, pl.ds(h * m, m), :])

    def lead_col_windows(n, lead):
        m = lead // 2
        return (lambda ref, idx: ref.at[:, :, pl.ds(pl.multiple_of(idx * n, n), n)],
                lambda ref, idx, h: ref.at[pl.ds(h * m, m), :, pl.ds(pl.multiple_of(idx * n, n), n)],
                lambda ref, h: ref.at[pl.ds(h * m, m)])

    def after(first, then):
        return lax.optimization_barrier((first, then))

    w_in_t, m_w_in_t, v_w_in_t = (jnp.swapaxes(a, 1, 2) for a in (w_in, m_w_in, v_w_in))

    ids = iter(range(64))
    w_in_f, w_o_f, w_up_f, w_down_f = [], [], [], []

    def gather_layer(l, not_before=None):
        in_shard = w_in_t[l:l + 1].astype(BF16)
        if not_before is not None:
            _, in_shard = after(not_before, in_shard)
        w_in_f.append(_all_gather("ag_in", next(ids), in_shard,
                                  jax.ShapeDtypeStruct((1, IN, D), BF16), row_windows(INS)))
        w_o_f.append(_all_gather("ag_o", next(ids), w_o[l:l + 1].astype(BF16),
                                 jax.ShapeDtypeStruct((1, D, D), BF16), row_windows(OS)))
        w_up_f.append(_all_gather("ag_up", next(ids), w_up[l:l + 1].astype(BF16),
                                  jax.ShapeDtypeStruct((1, D, F2), BF16), col_windows(US, D)))
        w_down_f.append(_all_gather("ag_down", next(ids), w_down[l:l + 1].astype(BF16),
                                    jax.ShapeDtypeStruct((1, F, D), BF16), row_windows(DS)))

    assert L % 2 == 0 and INS % 32 == 0 and OS % 32 == 0 and DS % 32 == 0 and D % 32 == 0
    gather_layer(0)
    conv_w_f = _all_gather("ag_conv_w", next(ids), conv_w, jax.ShapeDtypeStruct((L, 3, F2), F32),
                           lead_col_windows(US, L))

    n1g3, n2g3 = norm1_g.reshape(L, 1, D), norm2_g.reshape(L, 1, D)
    qg3, kg3 = q_norm_g.reshape(L, 1, HEAD), k_norm_g.reshape(L, 1, HEAD)
    lng3, lnb3 = sgu_ln_g.reshape(L, 1, GW), sgu_ln_b.reshape(L, 1, GW)
    ag3, og3 = attn_out_g.reshape(L, 1, AW), sgu_out_g.reshape(L, 1, GW)
    cb3 = conv_b.reshape(L, 1, F2)
    ws_b = w_s.astype(BF16)
    wst_b = jnp.swapaxes(w_s, 2, 3).astype(BF16)
    bs_b = jnp.broadcast_to(b_s[..., None], (L, NG, HEAD, HEAD))
    inv_freq = ROPE_THETA ** (-jnp.arange(0, HEAD, 2, dtype=F32) / HEAD)
    ang = jnp.arange(S, dtype=F32)[:, None] * inv_freq[None, :]
    cos2 = jnp.concatenate([jnp.cos(ang), jnp.cos(ang)], axis=1)
    sin2 = jnp.concatenate([-jnp.sin(ang), jnp.sin(ang)], axis=1)

    tn = 512
    t_in, t_d, t_f, t_f2 = _tile(IN, tn, 128), _tile(D, tn, 128), _tile(F, tn, 128), _tile(F2, tn, 128)
    tm_f = _tile(F, 512, 128)
    tm_s = _tile(S, 1024, 128)
    assert AW % t_in == 0 and (2 * KVW) % t_in == 0
    n_q, n_kv, n_g = AW // t_in, (2 * KVW) // t_in, (2 * GW) // t_in

    def in_tile(j):
        return jnp.where(j < n_q, j, jnp.where(j < n_q + n_g, j + n_kv, j - n_g))

    def w_spec(tk, tn_):
        return pl.BlockSpec((None, tk, tn_), lambda i, j, k: (0, k, j))

    def wt_spec(tn_, tk):
        return pl.BlockSpec((None, tn_, tk), lambda i, j, k: (0, j, k))

    def a_spec(tm, tk):
        return pl.BlockSpec((tm, tk), lambda i, j, k: (i, k))

    def at_spec(tk, tm):
        return pl.BlockSpec((tk, tm), lambda i, j, k: (k, i))

    def b_spec(tk, tn_):
        return pl.BlockSpec((tk, tn_), lambda i, j, k: (k, j))

    xs = x.reshape(S, D)
    saved = []
    cur = xs
    for l in range(L):
        h = _rms_fwd("rms1_fwd", cur, n1g3, l)
        z = _mm("mm_in", h, w_in_f[l], M=S, N=IN, K=D, tm=S, tn=t_in, tk=D, tb=True,
                a_spec=a_spec(S, D),
                b_spec=pl.BlockSpec((None, t_in, D), lambda i, j, k: (0, in_tile(j), 0)))
        q_r, k_r, v_b = _qkv_prep("qkv_prep", z, qg3, kg3, cos2, sin2, l, cfg)
        attn, mix_l = _attn_fwd("attn_fwd", q_r, k_r, v_b, sink, ag3, l, cfg)
        mix_r = _sgu_fwd("sgu_fwd", z, lng3, lnb3, ws_b, bs_b, og3, l, cfg)
        mixed = jnp.concatenate([mix_l, mix_r], axis=1)
        x1 = _mm("mm_o", mixed, w_o_f[l], M=S, N=D, K=D, tm=S, tn=t_d, tk=D,
                 a_spec=a_spec(S, D), b_spec=w_spec(D, t_d), res=cur)
        if l + 1 < L:
            gather_layer(l + 1, not_before=x1)
        h2 = _rms_fwd("rms2_fwd", x1, n2g3, l)
        ap = _mm("mm_up", h2, w_up_f[l], M=S, N=F2, K=D, tm=S, tn=t_f2, tk=D,
                 a_spec=a_spec(S, D), b_spec=w_spec(D, t_f2))
        y_b = _conv_glu_fwd("conv_glu_fwd", ap, conv_w_f, cb3, l, cfg)
        x2 = _mm("mm_down", y_b, w_down_f[l], M=S, N=D, K=F, tm=tm_s, tn=t_d, tk=F,
                 a_spec=a_spec(tm_s, F), b_spec=w_spec(F, t_d), res=x1)
        saved.append(dict(x=cur, h=h, z=z, q=q_r, k=k_r, v=v_b, attn=attn, mixed=mixed, x1=x1, h2=h2, ap=ap, y=y_b))
        cur = x2

    loss_tile, dx, dxb = _loss_bwd("loss", cur, loss_target.reshape(S, D))

    gS = [dict() for _ in range(L)]
    recv = [dict() for _ in range(L)]
    cvec = jnp.reshape(lax.axis_index("c"), (1,)).astype(jnp.int32)

    def rs_swap(name, g, R, C, small=None):
        g4 = g.reshape(N_DEV // 2, 2, R, C)
        got = _rs_d2d("rs_d2d_" + name, next(ids), [g4], [] if small is None else [small])
        return name, g4, got, small

    started = []

    def rs_finish(l, pending, then=None):
        name, g4, got, small = pending
        chip = _pair_sum("pair_sum_" + name, g4, got[0], cvec, 16)
        chip_small = [] if small is None else [_add2("pair_sum_small", small, got[1])]
        begun = _ici_start("rs_ici_" + name, [chip], chip_small)
        if then is not None:
            token, then = after(begun[4], then)
            begun = begun[:4] + (token,)
        started.append((l, name, begun))
        return then

    carried = None
    for l in reversed(range(L)):
        sv = saved[l]
        dy = _mm("mm_dy", dxb, w_down_f[l], M=S, N=F, K=D, tm=S, tn=t_f, tk=D, tb=True,
                 a_spec=a_spec(S, D), b_spec=wt_spec(t_f, D))
        if carried is not None:
            dy = rs_finish(l + 1, carried, dy)
        g_down = _mm("mm_gdown", sv["y"], dxb, M=F, N=D, K=S, tm=tm_f, tn=D, tk=S, ta=True,
                     a_spec=at_spec(S, tm_f), b_spec=b_spec(S, D), out_dtype=BF16)
        g_down, dy = after(g_down, dy)
        swap_down = rs_swap("w_down", g_down, DS, D)
        dap3, dcw, dcb = _glu_conv_bwd("glu_conv_bwd", dy, sv["ap"], conv_w_f, cb3, l, cfg)
        gS[l]["conv_w"] = jnp.concatenate([dcw[0], dcw[1]], axis=1)
        gS[l]["conv_b"] = jnp.concatenate([dcb[0], dcb[1]], axis=1).reshape(F2)
        dap3 = rs_finish(l, swap_down, dap3)
        dh2 = _mm("mm_dh2", dap3, w_up_f[l], M=S, N=D, K=F2, tm=tm_s, tn=t_d, tk=F, tb=True,
                  a_spec=pl.BlockSpec((None, tm_s, F), lambda i, j, k: (k, i, 0)),
                  b_spec=wt_spec(t_d, F))
        g_up = _mm("mm_gup", sv["h2"], dap3, M=D, N=F2, K=S, tm=D, tn=US, tk=S, ta=True,
                   a_spec=at_spec(S, D),
                   b_spec=pl.BlockSpec((None, S, US), lambda i, j, k: (j // (N_DEV // 2), 0, j % (N_DEV // 2))),
                   out_dtype=BF16, out_shape=(N_DEV, D, US),
                   out_spec=pl.BlockSpec((None, D, US), lambda i, j, k: (j, 0, 0)))
        g_up, dh2 = after(g_up, dh2)
        swap_up = rs_swap("w_up", g_up, D, US)
        dx1, dx1b, dg2 = _rms_bwd("rms2_bwd", sv["x1"], dh2, dx, n2g3, l)
        gS[l]["norm2_g"] = dg2.reshape(D)
        dmix = _mm("mm_dmix", dx1b, w_o_f[l], M=S, N=D, K=D, tm=S, tn=t_d, tk=D, tb=True,
                   a_spec=a_spec(S, D), b_spec=wt_spec(t_d, D))
        g_o = _mm("mm_go", sv["mixed"], dx1b, M=D, N=D, K=S, tm=D, tn=t_d, tk=S, ta=True,
                  a_spec=at_spec(S, D), b_spec=b_spec(S, t_d), out_dtype=BF16)
        g_o, dmix = after(g_o, dmix)
        swap_o = rs_swap("w_o", g_o, OS, D)
        dmix = rs_finish(l, swap_up, dmix)
        dq_r, dk_pad, dv_pad, dsink, dag = _attn_bwd("attn_bwd", sv["q"], sv["k"], sv["v"], sv["attn"], dmix,
                                                     sink, ag3, l, cfg)
        dq_r = rs_finish(l, swap_o, dq_r)
        gS[l]["sink"] = dsink[:, 0]
        gS[l]["attn_out_g"] = dag.reshape(AW)
        dzgu, dzgv, dws, dbs, dlng, dlnb, dog = _sgu_bwd("sgu_bwd", sv["z"], dmix, lng3, lnb3, ws_b, wst_b, bs_b,
                                                        og3, l, cfg)
        gS[l]["w_s"] = dws
        gS[l]["b_s"] = dbs[:, :, 0]
        gS[l]["sgu_ln_g"] = dlng.reshape(GW)
        gS[l]["sgu_ln_b"] = dlnb.reshape(GW)
        gS[l]["sgu_out_g"] = dog.reshape(GW)
        dzq, dzkv, dqg, dkg = _qkv_prep_bwd("qkv_prep_bwd", sv["z"], dq_r, dk_pad, dv_pad, qg3, kg3, cos2, sin2,
                                            l, cfg)
        gS[l]["q_norm_g"] = dqg.reshape(HEAD)
        gS[l]["k_norm_g"] = dkg.reshape(HEAD)
        dz = jnp.concatenate([dzq, dzkv, dzgu, dzgv], axis=1)
        dh = _mm("mm_dh", dz, w_in_f[l], M=S, N=D, K=IN, tm=tm_s, tn=t_d, tk=IN,
                 a_spec=a_spec(tm_s, IN), b_spec=w_spec(IN, t_d))
        g_in = _mm("mm_gin", dz, sv["h"], M=IN, N=D, K=S, tm=t_in, tn=D, tk=S, ta=True,
                   a_spec=at_spec(S, t_in), b_spec=b_spec(S, D), out_dtype=BF16)
        g_in, dh = after(g_in, dh)
        dx, dxb, dg1 = _rms_bwd("rms1_bwd", sv["x"], dh, dx1, n1g3, l)
        gS[l]["norm1_g"] = dg1.reshape(D)
        carried = rs_swap("w_in", g_in, INS, D, small=_pack([gS[l][n] for n in SMALL] + [gS[l]["conv_w"]]))
    dx = rs_finish(0, carried, dx)
    grad_x = dx.reshape(1, S, D)

    grads, deltas, new_m, new_v = {}, {}, {}, {}
    results, placed = {}, dx
    qvec = jnp.reshape(2 * lax.axis_index("x") + lax.axis_index("y"), (1,)).astype(jnp.int32)
    for l, name, begun in started:
        sent, landed = _ici_wait("rs_wait_" + name, begun, 1, placed)
        recv[l][name] = (sent, landed)
        wmv = (w_in_t, m_w_in_t, v_w_in_t) if name == "w_in" else (weights[name], mom_m[name], mom_v[name])
        results[name] = _adamw_sum("adamw_" + name, landed[0], sent[0], qvec, *wmv, l, results.get(name), 16)
        placed = results[name][0]
    _, loss_local = after(placed, loss_tile[0, 0])
    loss = lax.psum(loss_local, ("x", "y", "c"))
    results["w_in"] = [jnp.swapaxes(r, 1, 2) for r in results["w_in"]]
    for name, res in results.items():
        grads[name], deltas[name], new_m[name], new_v[name] = res

    def as_rows(a):
        flat = a.reshape(L, -1)
        short = (-flat.shape[1]) % 128
        if short:
            flat = jnp.pad(flat, ((0, 0), (0, short)))
        return flat.reshape(L, -1, 128)

    sizes = [weights[n][0].size for n in SMALL] + [3 * F2]
    offsets, off = [], 0
    for sz in sizes:
        offsets.append(off // 128)
        off += sz + (-sz) % PACK_ALIGN
    n_small = len(SMALL)
    rows_of = [-(-sz // 128) for sz in sizes]

    def small_body(*refs):
        q_ref = refs[0]
        lands, owns = refs[1:1 + L], refs[1 + L:1 + 2 * L]
        wmv_refs = refs[1 + 2 * L:1 + 2 * L + 3 * n_small]
        outs = refs[1 + 2 * L + 3 * n_small:]
        for l in range(L):
            for p in range(n_small + 1):
                rows = slice(offsets[p], offsets[p] + rows_of[p])
                terms = [jnp.where(q_ref[0] == s, owns[l][rows, :], lands[l][s, rows, :]) for s in range(N_DEV // 2)]
                g = terms[0]
                for t in terms[1:]:
                    g = g + t
                if p == n_small:
                    outs[4 * n_small][l] = g
                    continue
                d, m2, v2 = _adamw_math(wmv_refs[3 * p][l], g, wmv_refs[3 * p + 1][l], wmv_refs[3 * p + 2][l])
                for t, val in enumerate([g, d, m2, v2]):
                    outs[4 * p + t][l] = val

    vmem = pl.BlockSpec(memory_space=pltpu.VMEM)
    wmv_args = []
    for n in SMALL:
        wmv_args += [as_rows(weights[n]), as_rows(mom_m[n]), as_rows(mom_v[n])]
    small_out = pl.pallas_call(
        small_body, name="adamw_small",
        in_specs=[pl.BlockSpec(memory_space=pltpu.SMEM)] + [vmem] * (2 * L + 3 * n_small),
        out_specs=[vmem] * (4 * n_small + 1),
        out_shape=[jax.ShapeDtypeStruct((L, rows_of[p], 128), F32) for p in range(n_small) for _ in range(4)]
                  + [jax.ShapeDtypeStruct((L, rows_of[n_small], 128), F32)],
        compiler_params=_params(),
    )(qvec, *[recv[l]["w_in"][1][1] for l in range(L)], *[recv[l]["w_in"][0][1] for l in range(L)], *wmv_args)
    for p, n in enumerate(SMALL):
        for t, store in enumerate([grads, deltas, new_m, new_v]):
            store[n] = small_out[4 * p + t].reshape(L, -1)[:, :sizes[p]].reshape(weights[n].shape)
    conv_full = small_out[4 * n_small].reshape(L, 3, F2)
    me_i = _dev_index(_mesh_pos())
    g_cw = lax.dynamic_slice_in_dim(conv_full, me_i * US, US, axis=2)
    grads["conv_w"] = g_cw
    d_cw, m_cw, v_cw = _adamw_plain("adamw_conv_w", g_cw.reshape(L * 3, US), conv_w.reshape(L * 3, US),
                                    m_conv_w.reshape(L * 3, US), v_conv_w.reshape(L * 3, US))
    deltas["conv_w"] = d_cw.reshape(L, 3, US)
    new_m["conv_w"] = m_cw.reshape(L, 3, US)
    new_v["conv_w"] = v_cw.reshape(L, 3, US)

    return (loss, grad_x, *[grads[n] for n in order], *[deltas[n] for n in order],
            *[new_m[n] for n in order], *[new_v[n] for n in order])
```

```python
import jax
import jax.numpy as jnp
from jax import lax
from jax.experimental import pallas as pl
from jax.experimental.pallas import tpu as pltpu
from jax.experimental.pallas import tpu_sc as plsc

F32 = jnp.float32
BF16 = jnp.bfloat16
MESH = pl.DeviceIdType.MESH

N_DEV = 8
HEAD = 128
EPS = 1e-6
MASK_VALUE = -1e30
ROPE_THETA = 10000.0
GELU_C = 0.7978845608028654
GELU_A = 0.044715

ADAM_LR = 0.001
ADAM_B1 = 0.9
ADAM_B2 = 0.999
ADAM_EPS = 1e-08
ADAM_WD = 0.01
ADAM_STEP = 10

VMEM_LIMIT = 56 * 1024 * 1024


def _tile(n, pref, mult):
    best = None
    for t in range(mult, min(n, pref) + 1, mult):
        if n % t == 0:
            best = t
    return n if best is None else best


def _params(sem=None):
    kw = dict(vmem_limit_bytes=VMEM_LIMIT)
    if sem is not None:
        kw["dimension_semantics"] = sem
    return pltpu.CompilerParams(**kw)


def _gelu(x):
    return x * (0.5 * (1.0 + jnp.tanh(GELU_C * (x + GELU_A * (x * x * x)))))


def _gelu_grad(x):
    t = jnp.tanh(GELU_C * (x + GELU_A * (x * x * x)))
    return 0.5 * (1.0 + t) + 0.5 * x * (1.0 - t * t) * (GELU_C * (1.0 + 3.0 * GELU_A * (x * x)))


def _sigmoid(x):
    return 1.0 / (1.0 + jnp.exp(-x))


def _vec_spec(l, n):
    return pl.BlockSpec((None, 1, n), lambda *_: (l, 0, 0))


def _mm(name, a, b, *, M, N, K, tm, tn, tk, a_spec, b_spec, ta=False, tb=False, out_dtype=F32, res=None,
        out_shape=None, out_spec=None):
    nm, nn, nk = M // tm, N // tn, K // tk
    assert nm * tm == M and nn * tn == N and nk * tk == K
    assert not (ta and nk > 1)
    dims = (((1,), (1,)), ((), ())) if tb else (((1,), (0,)), ((), ()))

    def body(*refs):
        refs = list(refs)
        a_ref = refs.pop(0)
        b_ref = refs.pop(0)
        r_ref = refs.pop(0) if res is not None else None
        o_ref = refs.pop(0)
        acc = refs.pop(0) if nk > 1 else None
        at = refs.pop(0) if ta else None
        k = pl.program_id(2)
        if ta:
            @pl.when(pl.program_id(1) == 0)
            def _():
                at[...] = a_ref[...].T
            lhs = at[...]
        else:
            lhs = a_ref[...]
        p = lax.dot_general(lhs, b_ref[...], dims, preferred_element_type=F32)

        def finish(r):
            if r_ref is not None:
                r = r_ref[...] + r
            o_ref[...] = r.astype(out_dtype)

        if nk == 1:
            finish(p)
        else:
            @pl.when(k == 0)
            def _():
                acc[...] = p

            @pl.when(k > 0)
            def _():
                acc[...] += p

            @pl.when(k == nk - 1)
            def _():
                finish(acc[...])

    in_specs = [a_spec, b_spec]
    args = [a, b]
    if res is not None:
        in_specs.append(pl.BlockSpec((tm, tn), lambda i, j, k: (i, j)))
        args.append(res)
    scratch = []
    if nk > 1:
        scratch.append(pltpu.VMEM((tm, tn), F32))
    if ta:
        scratch.append(pltpu.VMEM((tm, tk), BF16))
    return pl.pallas_call(
        body, name=name, grid=(nm, nn, nk),
        in_specs=in_specs,
        out_specs=pl.BlockSpec((tm, tn), lambda i, j, k: (i, j)) if out_spec is None else out_spec,
        out_shape=jax.ShapeDtypeStruct((M, N) if out_shape is None else out_shape, out_dtype),
        scratch_shapes=scratch,
        compiler_params=_params(("arbitrary", "arbitrary", "arbitrary")),
    )(*args)


def _rms_fwd(name, x, g3, l):
    S, D = x.shape
    tr = _tile(S, 512, 16)

    def body(x_ref, g_ref, h_ref):
        xv = x_ref[...]
        r = lax.rsqrt(jnp.mean(xv * xv, axis=-1, keepdims=True) + EPS)
        h_ref[...] = ((xv * r) * g_ref[...]).astype(BF16)

    return pl.pallas_call(
        body, name=name, grid=(S // tr,),
        in_specs=[pl.BlockSpec((tr, D), lambda i: (i, 0)), _vec_spec(l, D)],
        out_specs=pl.BlockSpec((tr, D), lambda i: (i, 0)),
        out_shape=jax.ShapeDtypeStruct((S, D), BF16),
        compiler_params=_params(("arbitrary",)),
    )(x, g3)


def _rope(t, cos2, sin2):
    return t * cos2 + pltpu.roll(t, HEAD // 2, axis=1) * sin2


def _qkv_prep(name, z, qg3, kg3, cos2, sin2, l, cfg):
    S, AW, KVW, NQ, NKV = cfg["S"], cfg["AW"], cfg["KVW"], cfg["NQ"], cfg["NKV"]
    tr = _tile(S, 512, 16)
    kv_blk = (3 * AW) // (2 * KVW)

    def body(zq_ref, zkv_ref, qg_ref, kg_ref, c_ref, s_ref, q_ref, k_ref, v_ref):
        cosv, sinv = c_ref[...], s_ref[...]

        def norm_rope(t, g):
            r = lax.rsqrt(jnp.mean(t * t, axis=-1, keepdims=True) + EPS)
            return _rope((t * r) * g, cosv, sinv)

        for h in range(NQ):
            sl = slice(h * HEAD, (h + 1) * HEAD)
            q_ref[:, sl] = norm_rope(zq_ref[:, sl], qg_ref[...]).astype(BF16)
        for h in range(NKV):
            sl = slice(h * HEAD, (h + 1) * HEAD)
            k_ref[:, sl] = norm_rope(zkv_ref[:, sl], kg_ref[...]).astype(BF16)
        v_ref[...] = zkv_ref[:, KVW:].astype(BF16)

    return pl.pallas_call(
        body, name=name, grid=(S // tr,),
        in_specs=[pl.BlockSpec((tr, AW), lambda i: (i, 0)),
                  pl.BlockSpec((tr, 2 * KVW), lambda i: (i, kv_blk)),
                  _vec_spec(l, HEAD), _vec_spec(l, HEAD),
                  pl.BlockSpec((tr, HEAD), lambda i: (i, 0)),
                  pl.BlockSpec((tr, HEAD), lambda i: (i, 0))],
        out_specs=[pl.BlockSpec((tr, AW), lambda i: (i, 0)),
                   pl.BlockSpec((tr, KVW), lambda i: (i, 0)),
                   pl.BlockSpec((tr, KVW), lambda i: (i, 0))],
        out_shape=[jax.ShapeDtypeStruct((S, AW), BF16),
                   jax.ShapeDtypeStruct((S, KVW), BF16),
                   jax.ShapeDtypeStruct((S, KVW), BF16)],
        compiler_params=_params(("arbitrary",)),
    )(z, z, qg3, kg3, cos2, sin2)


def _band_specs(width, nb):
    return [pl.BlockSpec((HEAD, width), lambda n: (jnp.maximum(n - 1, 0), 0)),
            pl.BlockSpec((HEAD, width), lambda n: (n, 0)),
            pl.BlockSpec((HEAD, width), lambda n: (jnp.minimum(n + 1, nb - 1), 0))]


def _attn_probs(qs, kj, n, sink_of_row, S, G):
    s = lax.dot_general(qs, kj, (((1,), (1,)), ((), ())), preferred_element_type=F32) * (HEAD ** -0.5)
    rows = lax.broadcasted_iota(jnp.int32, (G * HEAD, 3 * HEAD), 0)
    cols = lax.broadcasted_iota(jnp.int32, (G * HEAD, 3 * HEAD), 1)
    qi = rows & (HEAD - 1)
    kpos = n * HEAD - HEAD + cols
    valid = (cols >= qi) & (cols <= qi + 2 * HEAD) & (kpos >= 0) & (kpos < S)
    s = jnp.where(valid, s, MASK_VALUE)
    m = jnp.maximum(jnp.max(s, axis=-1, keepdims=True), sink_of_row)
    p = jnp.exp(s - m)
    e_sink = jnp.exp(sink_of_row - m)
    inv = 1.0 / (jnp.sum(p, axis=-1, keepdims=True) + e_sink)
    return p * inv, e_sink * inv


def _sink_rows(sink_ref, l, j, G):
    hidx = lax.broadcasted_iota(jnp.int32, (G * HEAD, 1), 0) // HEAD
    col = jnp.full((G * HEAD, 1), sink_ref[l, j * G], F32)
    for g in range(1, G):
        col = jnp.where(hidx == g, sink_ref[l, j * G + g], col)
    return col


def _attn_fwd(name, q, k, v, sink, ag3, l, cfg):
    S, AW, KVW, NKV, G = cfg["S"], cfg["AW"], cfg["KVW"], cfg["NKV"], cfg["G"]
    nb = S // HEAD

    def body(q_ref, kp, kc, kn, vp, vc, vn, sink_ref, ag_ref, a_ref, mix_ref):
        n = pl.program_id(0)
        kb = jnp.concatenate([kp[...], kc[...], kn[...]], axis=0)
        vb = jnp.concatenate([vp[...], vc[...], vn[...]], axis=0)
        for j in range(NKV):
            sl = slice(j * HEAD, (j + 1) * HEAD)
            qs = jnp.concatenate([q_ref[:, (j * G + g) * HEAD:(j * G + g + 1) * HEAD] for g in range(G)], axis=0)
            probs, _ = _attn_probs(qs, kb[:, sl], n, _sink_rows(sink_ref, l, j, G), S, G)
            o = jnp.dot(probs.astype(BF16), vb[:, sl], preferred_element_type=F32)
            for g in range(G):
                a_ref[:, (j * G + g) * HEAD:(j * G + g + 1) * HEAD] = o[g * HEAD:(g + 1) * HEAD]
        a = a_ref[...]
        r = lax.rsqrt(jnp.mean(a * a, axis=-1, keepdims=True) + EPS)
        mix_ref[...] = ((a * r) * ag_ref[...]).astype(BF16)

    return pl.pallas_call(
        body, name=name, grid=(nb,),
        in_specs=[pl.BlockSpec((HEAD, AW), lambda n: (n, 0))] + _band_specs(KVW, nb) + _band_specs(KVW, nb)
                 + [pl.BlockSpec(memory_space=pltpu.SMEM), _vec_spec(l, AW)],
        out_specs=[pl.BlockSpec((HEAD, AW), lambda n: (n, 0)), pl.BlockSpec((HEAD, AW), lambda n: (n, 0))],
        out_shape=[jax.ShapeDtypeStruct((S, AW), F32), jax.ShapeDtypeStruct((S, AW), BF16)],
        compiler_params=_params(("arbitrary",)),
    )(q, k, k, k, v, v, v, sink, ag3)


def _sgu_forward_math(gu, gv, lng, lnb):
    u = _gelu(gu)
    vv = _gelu(gv)
    mu = jnp.mean(vv, axis=-1, keepdims=True)
    xc = vv - mu
    rstd = lax.rsqrt(jnp.mean(xc * xc, axis=-1, keepdims=True) + EPS)
    xhat = xc * rstd
    vn = xhat * lng + lnb
    return u, xhat, rstd, vn


def _sgu_fwd(name, z, lng3, lnb3, ws_b, bs_b, og3, l, cfg):
    S, GW, NG = cfg["S"], cfg["GW"], cfg["NG"]

    def body(gu_ref, gv_ref, lng_ref, lnb_ref, ws_ref, bs_ref, og_ref, mix_ref, sg_ref):
        u, _, _, vn = _sgu_forward_math(gu_ref[...], gv_ref[...], lng_ref[...], lnb_ref[...])
        vnb = vn.astype(BF16)
        for h in range(NG):
            sl = slice(h * HEAD, (h + 1) * HEAD)
            f = jnp.dot(ws_ref[h], vnb[:, sl], preferred_element_type=F32) + bs_ref[h]
            sg_ref[:, sl] = u[:, sl] * f
        sg = sg_ref[...]
        r = lax.rsqrt(jnp.mean(sg * sg, axis=-1, keepdims=True) + EPS)
        mix_ref[...] = ((sg * r) * og_ref[...]).astype(BF16)

    return pl.pallas_call(
        body, name=name, grid=(S // HEAD,),
        in_specs=[pl.BlockSpec((HEAD, GW), lambda c: (c, 1)),
                  pl.BlockSpec((HEAD, GW), lambda c: (c, 2)),
                  _vec_spec(l, GW), _vec_spec(l, GW),
                  pl.BlockSpec((None, NG, HEAD, HEAD), lambda c: (l, 0, 0, 0)),
                  pl.BlockSpec((None, NG, HEAD, HEAD), lambda c: (l, 0, 0, 0)),
                  _vec_spec(l, GW)],
        out_specs=pl.BlockSpec((HEAD, GW), lambda c: (c, 0)),
        out_shape=jax.ShapeDtypeStruct((S, GW), BF16),
        scratch_shapes=[pltpu.VMEM((HEAD, GW), F32)],
        compiler_params=_params(("arbitrary",)),
    )(z, z, lng3, lnb3, ws_b, bs_b, og3)


CONV_HALO = 8
CONV_ROWS = 128
CONV_COLS = 512


def _for_row_windows(S, fn):
    R, W = CONV_ROWS, CONV_ROWS + 2 * CONV_HALO
    n = S // R
    assert n * R == S and n >= 2
    fn(0, 0, 0, "top")
    if n > 2:
        def mid(k, carry):
            fn(pl.multiple_of(k * R - CONV_HALO, CONV_HALO), CONV_HALO, pl.multiple_of(k * R, R), "mid")
            return carry

        lax.fori_loop(1, n - 1, mid, 0, unroll=2 if (n - 2) % 2 == 0 else 1)
    fn(S - W, 2 * CONV_HALO, S - R, "bottom")


def _shifts(t, edge):
    W = t.shape[0]
    dn, up = pltpu.roll(t, 1, axis=0), pltpu.roll(t, W - 1, axis=0)
    if edge == "top":
        dn = jnp.where(lax.broadcasted_iota(jnp.int32, t.shape, 0) == 0, 0.0, dn)
    if edge == "bottom":
        up = jnp.where(lax.broadcasted_iota(jnp.int32, t.shape, 0) == W - 1, 0.0, up)
    return dn, up


def _conv3(t, w, b, edge):
    dn, up = _shifts(t, edge)
    return ((b + dn * w[0:1]) + t * w[1:2]) + up * w[2:3]


def _conv_glu_fwd(name, ap, cw, cb3, l, cfg):
    S, F = cfg["S"], cfg["F"]
    tc = _tile(F, CONV_COLS, 128)
    nf = F // tc
    R, W = CONV_ROWS, CONV_ROWS + 2 * CONV_HALO

    def body(g_ref, u_ref, wg_ref, wu_ref, bg_ref, bu_ref, y_ref):
        def window(start, lo, out0, edge):
            rows = pl.ds(start, W)
            for c0 in range(0, tc, 128):
                cols = slice(c0, c0 + 128)
                ag = _conv3(g_ref[rows, cols], wg_ref[:, cols], bg_ref[:, cols], edge)
                au = _conv3(u_ref[rows, cols], wu_ref[:, cols], bu_ref[:, cols], edge)
                y = (ag * _sigmoid(ag)) * au
                y_ref[pl.ds(out0, R), cols] = y[lo:lo + R].astype(BF16)

        _for_row_windows(S, window)

    return pl.pallas_call(
        body, name=name, grid=(nf,),
        in_specs=[pl.BlockSpec((S, tc), lambda j: (0, j)),
                  pl.BlockSpec((S, tc), lambda j: (0, j + nf)),
                  pl.BlockSpec((None, 3, tc), lambda j: (l, 0, j)),
                  pl.BlockSpec((None, 3, tc), lambda j: (l, 0, j + nf)),
                  pl.BlockSpec((None, 1, tc), lambda j: (l, 0, j)),
                  pl.BlockSpec((None, 1, tc), lambda j: (l, 0, j + nf))],
        out_specs=pl.BlockSpec((S, tc), lambda j: (0, j)),
        out_shape=jax.ShapeDtypeStruct((S, F), BF16),
        compiler_params=_params(("arbitrary",)),
    )(ap, ap, cw, cw, cb3, cb3)


def _loss_bwd(name, y, target):
    S, D = y.shape
    tr = _tile(S, 512, 16)

    def body(y_ref, t_ref, loss_ref, d_ref, db_ref):
        @pl.when(pl.program_id(0) == 0)
        def _():
            loss_ref[...] = jnp.zeros_like(loss_ref)

        err = y_ref[...] - t_ref[...]
        part = 0.5 * jnp.sum(jnp.mean(err * err, axis=-1, keepdims=True), axis=0, keepdims=True)
        loss_ref[...] += jnp.broadcast_to(part, loss_ref.shape)
        d = err * (1.0 / D)
        d_ref[...] = d
        db_ref[...] = d.astype(BF16)

    return pl.pallas_call(
        body, name=name, grid=(S // tr,),
        in_specs=[pl.BlockSpec((tr, D), lambda i: (i, 0)), pl.BlockSpec((tr, D), lambda i: (i, 0))],
        out_specs=[pl.BlockSpec((8, 128), lambda i: (0, 0)),
                   pl.BlockSpec((tr, D), lambda i: (i, 0)),
                   pl.BlockSpec((tr, D), lambda i: (i, 0))],
        out_shape=[jax.ShapeDtypeStruct((8, 128), F32),
                   jax.ShapeDtypeStruct((S, D), F32),
                   jax.ShapeDtypeStruct((S, D), BF16)],
        compiler_params=_params(("arbitrary",)),
    )(y, target)


def _rms_bwd(name, x, dh, dres, g3, l):
    S, D = x.shape
    tr = _tile(S, 512, 16)

    def body(x_ref, dh_ref, dr_ref, g_ref, dx_ref, dxb_ref, dg_ref):
        @pl.when(pl.program_id(0) == 0)
        def _():
            dg_ref[...] = jnp.zeros_like(dg_ref)

        xv = x_ref[...]
        dhv = dh_ref[...]
        r = lax.rsqrt(jnp.mean(xv * xv, axis=-1, keepdims=True) + EPS)
        xhat = xv * r
        dhg = dhv * g_ref[...]
        dx = dr_ref[...] + r * (dhg - xhat * jnp.mean(dhg * xhat, axis=-1, keepdims=True))
        dx_ref[...] = dx
        dxb_ref[...] = dx.astype(BF16)
        dg_ref[...] += jnp.sum(dhv * xhat, axis=0, keepdims=True)

    return pl.pallas_call(
        body, name=name, grid=(S // tr,),
        in_specs=[pl.BlockSpec((tr, D), lambda i: (i, 0)), pl.BlockSpec((tr, D), lambda i: (i, 0)),
                  pl.BlockSpec((tr, D), lambda i: (i, 0)), _vec_spec(l, D)],
        out_specs=[pl.BlockSpec((tr, D), lambda i: (i, 0)), pl.BlockSpec((tr, D), lambda i: (i, 0)),
                   pl.BlockSpec((1, D), lambda i: (0, 0))],
        out_shape=[jax.ShapeDtypeStruct((S, D), F32), jax.ShapeDtypeStruct((S, D), BF16),
                   jax.ShapeDtypeStruct((1, D), F32)],
        compiler_params=_params(("arbitrary",)),
    )(x, dh, dres, g3)


def _glu_conv_bwd(name, dy, ap, cw, cb3, l, cfg):
    S, F = cfg["S"], cfg["F"]
    tc = _tile(F, CONV_COLS, 128)
    nf = F // tc
    R, W = CONV_ROWS, CONV_ROWS + 2 * CONV_HALO

    def body(dy_ref, g_ref, u_ref, wg_ref, wu_ref, bg_ref, bu_ref, dap_ref, dw_ref, db_ref, acc):
        acc[...] = jnp.zeros_like(acc)

        def window(start, lo, out0, edge):
            rows = pl.ds(start, W)
            for c0 in range(0, tc, 128):
                cols = slice(c0, c0 + 128)
                apg, apu, dyv = g_ref[rows, cols], u_ref[rows, cols], dy_ref[rows, cols]
                wg, wu = wg_ref[:, cols], wu_ref[:, cols]
                ag = _conv3(apg, wg, bg_ref[:, cols], edge)
                au = _conv3(apu, wu, bu_ref[:, cols], edge)
                sig = _sigmoid(ag)
                da_u = dyv * (ag * sig)
                da_g = (dyv * au) * (sig * (1.0 + ag * (1.0 - sig)))

                def add(i, prod):
                    acc[i, :, cols] += jnp.sum(prod[lo:lo + R].reshape(R // 8, 8, 128), axis=0)

                for half, (da, t, w) in enumerate([(da_g, apg, wg), (da_u, apu, wu)]):
                    prv, nxt = _shifts(da, edge)
                    dap = (nxt * w[0:1] + da * w[1:2]) + prv * w[2:3]
                    dap_ref[half, pl.ds(out0, R), cols] = dap[lo:lo + R].astype(BF16)
                    add(3 * half, nxt * t)
                    add(3 * half + 1, da * t)
                    add(3 * half + 2, prv * t)
                    add(6 + half, da)

        _for_row_windows(S, window)
        col = [jnp.sum(acc[i], axis=0, keepdims=True) for i in range(8)]
        dw_ref[0] = jnp.concatenate(col[0:3], axis=0)
        dw_ref[1] = jnp.concatenate(col[3:6], axis=0)
        db_ref[0] = col[6]
        db_ref[1] = col[7]

    return pl.pallas_call(
        body, name=name, grid=(nf,),
        in_specs=[pl.BlockSpec((S, tc), lambda j: (0, j)),
                  pl.BlockSpec((S, tc), lambda j: (0, j)),
                  pl.BlockSpec((S, tc), lambda j: (0, j + nf)),
                  pl.BlockSpec((None, 3, tc), lambda j: (l, 0, j)),
                  pl.BlockSpec((None, 3, tc), lambda j: (l, 0, j + nf)),
                  pl.BlockSpec((None, 1, tc), lambda j: (l, 0, j)),
                  pl.BlockSpec((None, 1, tc), lambda j: (l, 0, j + nf))],
        out_specs=[pl.BlockSpec((2, S, tc), lambda j: (0, 0, j)),
                   pl.BlockSpec((2, 3, tc), lambda j: (0, 0, j)),
                   pl.BlockSpec((2, 1, tc), lambda j: (0, 0, j))],
        out_shape=[jax.ShapeDtypeStruct((2, S, F), BF16),
                   jax.ShapeDtypeStruct((2, 3, F), F32),
                   jax.ShapeDtypeStruct((2, 1, F), F32)],
        scratch_shapes=[pltpu.VMEM((8, 8, tc), F32)],
        compiler_params=_params(("arbitrary",)),
    )(dy, ap, ap, cw, cw, cb3, cb3)


def _attn_bwd(name, q, k, v, attn, dmix, sink, ag3, l, cfg):
    S, AW, KVW, NQ, NKV, G = cfg["S"], cfg["AW"], cfg["KVW"], cfg["NQ"], cfg["NKV"], cfg["G"]
    nb = S // HEAD
    scale = HEAD ** -0.5

    def body(q_ref, kp, kc, kn, vp, vc, vn, a_ref, dm_ref, sink_ref, ag_ref,
             dq_ref, dk_ref, dv_ref, dsink_ref, dag_ref, da_scr):
        n = pl.program_id(0)

        @pl.when(n == 0)
        def _():
            dk_ref[...] = jnp.zeros_like(dk_ref)
            dv_ref[...] = jnp.zeros_like(dv_ref)
            dsink_ref[...] = jnp.zeros_like(dsink_ref)
            dag_ref[...] = jnp.zeros_like(dag_ref)

        a = a_ref[...]
        dm = dm_ref[...]
        r = lax.rsqrt(jnp.mean(a * a, axis=-1, keepdims=True) + EPS)
        xhat = a * r
        dmg = dm * ag_ref[...]
        da_scr[...] = r * (dmg - xhat * jnp.mean(dmg * xhat, axis=-1, keepdims=True))
        dag_ref[...] += jnp.sum(dm * xhat, axis=0, keepdims=True)

        kb = jnp.concatenate([kp[...], kc[...], kn[...]], axis=0)
        vb = jnp.concatenate([vp[...], vc[...], vn[...]], axis=0)
        band = pl.ds(pl.multiple_of(n * HEAD, HEAD), 3 * HEAD)
        for j in range(NKV):
            sl = slice(j * HEAD, (j + 1) * HEAD)
            heads = [slice((j * G + g) * HEAD, (j * G + g + 1) * HEAD) for g in range(G)]
            qs = jnp.concatenate([q_ref[:, hs] for hs in heads], axis=0)
            do = jnp.concatenate([da_scr[:, hs] for hs in heads], axis=0)
            kj, vj = kb[:, sl], vb[:, sl]
            probs, p_sink = _attn_probs(qs, kj, n, _sink_rows(sink_ref, l, j, G), S, G)
            dob = do.astype(BF16)
            dprobs = lax.dot_general(dob, vj, (((1,), (1,)), ((), ())), preferred_element_type=F32)
            delta = jnp.sum(dprobs * probs, axis=-1, keepdims=True)
            ds = (probs * (dprobs - delta)) * scale
            dsb = ds.astype(BF16)
            dsk = -(p_sink * delta)
            dq = jnp.dot(dsb, kj, preferred_element_type=F32)
            for g in range(G):
                dq_ref[:, heads[g]] = dq[g * HEAD:(g + 1) * HEAD]
                part = jnp.sum(dsk[g * HEAD:(g + 1) * HEAD], axis=0, keepdims=True)
                dsink_ref[j * G + g:j * G + g + 1, :] += jnp.broadcast_to(part, (1, HEAD))
            dk_ref[band, sl] += lax.dot_general(dsb, qs, (((0,), (0,)), ((), ())), preferred_element_type=F32)
            dv_ref[band, sl] += lax.dot_general(probs.astype(BF16), dob, (((0,), (0,)), ((), ())),
                                                preferred_element_type=F32)

    return pl.pallas_call(
        body, name=name, grid=(nb,),
        in_specs=[pl.BlockSpec((HEAD, AW), lambda n: (n, 0))] + _band_specs(KVW, nb) + _band_specs(KVW, nb)
                 + [pl.BlockSpec((HEAD, AW), lambda n: (n, 0)),
                    pl.BlockSpec((HEAD, AW), lambda n: (n, 0)),
                    pl.BlockSpec(memory_space=pltpu.SMEM), _vec_spec(l, AW)],
        out_specs=[pl.BlockSpec((HEAD, AW), lambda n: (n, 0)),
                   pl.BlockSpec((S + 2 * HEAD, KVW), lambda n: (0, 0)),
                   pl.BlockSpec((S + 2 * HEAD, KVW), lambda n: (0, 0)),
                   pl.BlockSpec((NQ, HEAD), lambda n: (0, 0)),
                   pl.BlockSpec((1, AW), lambda n: (0, 0))],
        out_shape=[jax.ShapeDtypeStruct((S, AW), F32),
                   jax.ShapeDtypeStruct((S + 2 * HEAD, KVW), F32),
                   jax.ShapeDtypeStruct((S + 2 * HEAD, KVW), F32),
                   jax.ShapeDtypeStruct((NQ, HEAD), F32),
                   jax.ShapeDtypeStruct((1, AW), F32)],
        scratch_shapes=[pltpu.VMEM((HEAD, AW), F32)],
        compiler_params=_params(("arbitrary",)),
    )(q, k, k, k, v, v, v, attn, dmix, sink, ag3)


def _qkv_prep_bwd(name, z, dq, dk_pad, dv_pad, qg3, kg3, cos2, sin2, l, cfg):
    S, AW, KVW, NQ, NKV = cfg["S"], cfg["AW"], cfg["KVW"], cfg["NQ"], cfg["NKV"]
    kv_blk = (3 * AW) // (2 * KVW)

    def body(zq_ref, zkv_ref, dq_ref, dk_ref, dv_ref, qg_ref, kg_ref, c_ref, s_ref,
             dzq_ref, dzkv_ref, dqg_ref, dkg_ref):
        @pl.when(pl.program_id(0) == 0)
        def _():
            dqg_ref[...] = jnp.zeros_like(dqg_ref)
            dkg_ref[...] = jnp.zeros_like(dkg_ref)

        cosv, sinv = c_ref[...], s_ref[...]

        def back(t, dr, g):
            r = lax.rsqrt(jnp.mean(t * t, axis=-1, keepdims=True) + EPS)
            xhat = t * r
            dn = dr * cosv + pltpu.roll(dr * sinv, HEAD // 2, axis=1)
            dxh = dn * g
            dt = r * (dxh - xhat * jnp.mean(dxh * xhat, axis=-1, keepdims=True))
            return dt, jnp.sum(dn * xhat, axis=0, keepdims=True)

        gq = jnp.zeros((1, HEAD), F32)
        for h in range(NQ):
            sl = slice(h * HEAD, (h + 1) * HEAD)
            dt, gpart = back(zq_ref[:, sl], dq_ref[:, sl], qg_ref[...])
            dzq_ref[:, sl] = dt.astype(BF16)
            gq = gq + gpart
        dqg_ref[...] += gq
        gk = jnp.zeros((1, HEAD), F32)
        for h in range(NKV):
            sl = slice(h * HEAD, (h + 1) * HEAD)
            dt, gpart = back(zkv_ref[:, sl], dk_ref[:, sl], kg_ref[...])
            dzkv_ref[:, sl] = dt.astype(BF16)
            gk = gk + gpart
        dkg_ref[...] += gk
        dzkv_ref[:, KVW:] = dv_ref[...].astype(BF16)

    return pl.pallas_call(
        body, name=name, grid=(S // HEAD,),
        in_specs=[pl.BlockSpec((HEAD, AW), lambda i: (i, 0)),
                  pl.BlockSpec((HEAD, 2 * KVW), lambda i: (i, kv_blk)),
                  pl.BlockSpec((HEAD, AW), lambda i: (i, 0)),
                  pl.BlockSpec((HEAD, KVW), lambda i: (i + 1, 0)),
                  pl.BlockSpec((HEAD, KVW), lambda i: (i + 1, 0)),
                  _vec_spec(l, HEAD), _vec_spec(l, HEAD),
                  pl.BlockSpec((HEAD, HEAD), lambda i: (i, 0)),
                  pl.BlockSpec((HEAD, HEAD), lambda i: (i, 0))],
        out_specs=[pl.BlockSpec((HEAD, AW), lambda i: (i, 0)),
                   pl.BlockSpec((HEAD, 2 * KVW), lambda i: (i, 0)),
                   pl.BlockSpec((1, HEAD), lambda i: (0, 0)),
                   pl.BlockSpec((1, HEAD), lambda i: (0, 0))],
        out_shape=[jax.ShapeDtypeStruct((S, AW), BF16),
                   jax.ShapeDtypeStruct((S, 2 * KVW), BF16),
                   jax.ShapeDtypeStruct((1, HEAD), F32),
                   jax.ShapeDtypeStruct((1, HEAD), F32)],
        compiler_params=_params(("arbitrary",)),
    )(z, z, dq, dk_pad, dv_pad, qg3, kg3, cos2, sin2)


def _sgu_bwd(name, z, dmix, lng3, lnb3, ws_b, wst_b, bs_b, og3, l, cfg):
    S, GW, NG = cfg["S"], cfg["GW"], cfg["NG"]

    def body(gu_ref, gv_ref, dm_ref, lng_ref, lnb_ref, ws_ref, wst_ref, bs_ref, og_ref,
             dgu_ref, dgv_ref, dws_ref, dbs_ref, dlng_ref, dlnb_ref, dog_ref, sg_scr, f_scr, dvn_scr):
        @pl.when(pl.program_id(0) == 0)
        def _():
            dws_ref[...] = jnp.zeros_like(dws_ref)
            dbs_ref[...] = jnp.zeros_like(dbs_ref)
            dlng_ref[...] = jnp.zeros_like(dlng_ref)
            dlnb_ref[...] = jnp.zeros_like(dlnb_ref)
            dog_ref[...] = jnp.zeros_like(dog_ref)

        gu, gv = gu_ref[...], gv_ref[...]
        lng = lng_ref[...]
        u, xhat, rstd, vn = _sgu_forward_math(gu, gv, lng, lnb_ref[...])
        vnb = vn.astype(BF16)
        for h in range(NG):
            sl = slice(h * HEAD, (h + 1) * HEAD)
            f = jnp.dot(ws_ref[h], vnb[:, sl], preferred_element_type=F32) + bs_ref[h]
            f_scr[:, sl] = f
            sg_scr[:, sl] = u[:, sl] * f
        sg = sg_scr[...]
        dm = dm_ref[...]
        r = lax.rsqrt(jnp.mean(sg * sg, axis=-1, keepdims=True) + EPS)
        sghat = sg * r
        dmg = dm * og_ref[...]
        dsg = r * (dmg - sghat * jnp.mean(dmg * sghat, axis=-1, keepdims=True))
        dog_ref[...] += jnp.sum(dm * sghat, axis=0, keepdims=True)
        du = dsg * f_scr[...]
        df = dsg * u
        dfb = df.astype(BF16)
        for h in range(NG):
            sl = slice(h * HEAD, (h + 1) * HEAD)
            dvn_scr[:, sl] = jnp.dot(wst_ref[h], dfb[:, sl], preferred_element_type=F32)
            dws_ref[h] += lax.dot_general(dfb[:, sl], vnb[:, sl], (((1,), (1,)), ((), ())),
                                          preferred_element_type=F32)
            dbs_ref[h] += jnp.broadcast_to(jnp.sum(df[:, sl], axis=-1, keepdims=True), (HEAD, HEAD))
        dvn = dvn_scr[...]
        dlng_ref[...] += jnp.sum(dvn * xhat, axis=0, keepdims=True)
        dlnb_ref[...] += jnp.sum(dvn, axis=0, keepdims=True)
        dxh = dvn * lng
        dvv = rstd * ((dxh - jnp.mean(dxh, axis=-1, keepdims=True))
                      - xhat * jnp.mean(dxh * xhat, axis=-1, keepdims=True))
        dgu_ref[...] = (du * _gelu_grad(gu)).astype(BF16)
        dgv_ref[...] = (dvv * _gelu_grad(gv)).astype(BF16)

    vec = pl.BlockSpec((1, GW), lambda c: (0, 0))
    mat = pl.BlockSpec((NG, HEAD, HEAD), lambda c: (0, 0, 0))
    wsp = pl.BlockSpec((None, NG, HEAD, HEAD), lambda c: (l, 0, 0, 0))
    return pl.pallas_call(
        body, name=name, grid=(S // HEAD,),
        in_specs=[pl.BlockSpec((HEAD, GW), lambda c: (c, 1)),
                  pl.BlockSpec((HEAD, GW), lambda c: (c, 2)),
                  pl.BlockSpec((HEAD, GW), lambda c: (c, 1)),
                  _vec_spec(l, GW), _vec_spec(l, GW), wsp, wsp, wsp, _vec_spec(l, GW)],
        out_specs=[pl.BlockSpec((HEAD, GW), lambda c: (c, 0)), pl.BlockSpec((HEAD, GW), lambda c: (c, 0)),
                   mat, mat, vec, vec, vec],
        out_shape=[jax.ShapeDtypeStruct((S, GW), BF16), jax.ShapeDtypeStruct((S, GW), BF16),
                   jax.ShapeDtypeStruct((NG, HEAD, HEAD), F32), jax.ShapeDtypeStruct((NG, HEAD, HEAD), F32),
                   jax.ShapeDtypeStruct((1, GW), F32), jax.ShapeDtypeStruct((1, GW), F32),
                   jax.ShapeDtypeStruct((1, GW), F32)],
        scratch_shapes=[pltpu.VMEM((HEAD, GW), F32), pltpu.VMEM((HEAD, GW), F32), pltpu.VMEM((HEAD, GW), F32)],
        compiler_params=_params(("arbitrary",)),
    )(z, z, dmix, lng3, lnb3, ws_b, wst_b, bs_b, og3)


def _mesh_pos():
    x, y, c = lax.axis_index("x"), lax.axis_index("y"), lax.axis_index("c")
    return x, y, c


def _dev_index(p):
    return 4 * p[0] + 2 * p[1] + p[2]


def _handshake(peers):
    barrier = pltpu.get_barrier_semaphore()
    for p in peers:
        pl.semaphore_signal(barrier, inc=1, device_id=p, device_id_type=MESH)
    pl.semaphore_wait(barrier, len(peers))


GATHER_COPIES = 10


def _gather_body(full, half, in_half):
    def body(in_ref, out_ref, send_sems, recv_sems, local_sem):
        x, y, c = _mesh_pos()
        me, sib = (x, y, c), (x, y, 1 - c)
        xn, yn, dg = (1 - x, y, c), (x, 1 - y, c), (1 - x, 1 - y, c)
        _handshake([sib, xn, yn])

        def copy(k, dst, to, src=None):
            return pltpu.make_async_remote_copy(
                src_ref=dst if src is None else src, dst_ref=dst,
                send_sem=send_sems.at[k], recv_sem=recv_sems.at[k], device_id=to, device_id_type=MESH)

        def win(dev, h=None):
            idx = _dev_index(dev)
            return full(out_ref, idx) if h is None else half(out_ref, idx, h)

        local = pltpu.make_async_copy(in_ref, win(me), local_sem)
        local.start()
        sends = [copy(0, win(me), sib, src=in_ref),
                 copy(1, win(me, 0), xn, src=in_half(in_ref, 0)),
                 copy(2, win(me, 1), yn, src=in_half(in_ref, 1)),
                 copy(4, win(me, 0), yn, src=in_half(in_ref, 0)),
                 copy(6, win(me, 1), xn, src=in_half(in_ref, 1))]
        for cp in sends:
            cp.start()
        copy(1, win(xn, 0), me).wait_recv()
        sends.append(copy(3, win(xn, 0), yn))
        sends[-1].start()
        copy(2, win(yn, 1), me).wait_recv()
        sends.append(copy(5, win(yn, 1), xn))
        sends[-1].start()
        copy(6, win(xn, 1), me).wait_recv()
        sends.append(copy(7, win(xn), sib))
        sends[-1].start()
        copy(4, win(yn, 0), me).wait_recv()
        sends.append(copy(8, win(yn), sib))
        sends[-1].start()
        copy(3, win(dg, 0), me).wait_recv()
        copy(5, win(dg, 1), me).wait_recv()
        sends.append(copy(9, win(dg), sib))
        sends[-1].start()
        sib_xn, sib_yn, sib_dg = (1 - x, y, 1 - c), (x, 1 - y, 1 - c), (1 - x, 1 - y, 1 - c)
        for k, dev in [(0, sib), (7, sib_xn), (8, sib_yn), (9, sib_dg)]:
            copy(k, win(dev), me).wait_recv()
        for cp in sends:
            cp.wait_send()
        local.wait()

    return body


def _all_gather(name, cid, shard, out_shape, windows):
    return pl.kernel(
        _gather_body(*windows), out_type=out_shape,
        mesh=plsc.ScalarSubcoreMesh(axis_name="seq", num_cores=1), name=name,
        scratch_types=[pltpu.SemaphoreType.DMA((GATHER_COPIES,)), pltpu.SemaphoreType.DMA((GATHER_COPIES,)),
                       pltpu.SemaphoreType.DMA],
        compiler_params=pltpu.CompilerParams(collective_id=cid),
    )(shard)


def _d2d_body(nb, ns):
    n = nb + ns

    def body(*refs):
        ins, outs = refs[:n], refs[n:2 * n]
        send_sems, recv_sems = refs[2 * n:]
        x, y, c = _mesh_pos()
        sib = (x, y, 1 - c)
        _handshake([sib])
        copies = []
        for t in range(n):
            cp = pltpu.make_async_remote_copy(
                src_ref=ins[t].at[:, 1 - c] if t < nb else ins[t], dst_ref=outs[t],
                send_sem=send_sems.at[t], recv_sem=recv_sems.at[t],
                device_id=sib, device_id_type=MESH)
            cp.start()
            copies.append(cp)
        for cp in copies:
            cp.wait()

    return body


def _rs_d2d(name, cid, bigs, smalls):
    shapes = [jax.ShapeDtypeStruct((g.shape[0],) + g.shape[2:], g.dtype) for g in bigs]
    shapes += [jax.ShapeDtypeStruct(s.shape, s.dtype) for s in smalls]
    n = len(shapes)
    return pl.kernel(
        _d2d_body(len(bigs), len(smalls)), out_type=shapes,
        mesh=plsc.ScalarSubcoreMesh(axis_name="seq", num_cores=1), name=name,
        scratch_types=[pltpu.SemaphoreType.DMA((n,)), pltpu.SemaphoreType.DMA((n,))],
        compiler_params=pltpu.CompilerParams(collective_id=cid),
    )(*bigs, *smalls)


def _pair_sum(name, g4, recv, cvec, row_mult):
    Q, _, R, C = g4.shape
    tr = _tile(R, 1024, row_mult)

    def body(c_ref, a_ref, b_ref, o_ref):
        o_ref[...] = (a_ref[...].astype(F32) + b_ref[...].astype(F32)).astype(BF16)

    grid_spec = pltpu.PrefetchScalarGridSpec(
        num_scalar_prefetch=1, grid=(Q, R // tr),
        in_specs=[pl.BlockSpec((None, None, tr, C), lambda q, i, c_ref: (q, c_ref[0], i, 0)),
                  pl.BlockSpec((None, tr, C), lambda q, i, c_ref: (q, i, 0))],
        out_specs=pl.BlockSpec((None, tr, C), lambda q, i, c_ref: (q, i, 0)))
    return pl.pallas_call(
        body, name=name, grid_spec=grid_spec, out_shape=jax.ShapeDtypeStruct((Q, R, C), BF16),
        compiler_params=_params(("arbitrary", "arbitrary")),
    )(cvec, g4, recv)


def _add2(name, a, b):
    R, C = a.shape
    tr = _tile(R, 512, 8)

    def body(a_ref, b_ref, o_ref):
        o_ref[...] = a_ref[...] + b_ref[...]

    blk = pl.BlockSpec((tr, C), lambda i: (i, 0))
    return pl.pallas_call(body, name=name, grid=(R // tr,), in_specs=[blk, blk], out_specs=blk,
                          out_shape=jax.ShapeDtypeStruct((R, C), a.dtype),
                          compiler_params=_params(("arbitrary",)))(a, b)


def _ici_copies(nb, n, srcs, lands, send_sems, recv_sems):
    x, y, c = _mesh_pos()
    q_me = 2 * x + y
    copies = []
    for t in range(n):
        for k in range(1, 4):
            px, py = x ^ (k >> 1), y ^ (k & 1)
            copies.append(pltpu.make_async_remote_copy(
                src_ref=srcs[t].at[2 * px + py] if t < nb else srcs[t], dst_ref=lands[t].at[q_me],
                send_sem=send_sems.at[3 * t + k - 1], recv_sem=recv_sems.at[3 * t + k - 1],
                device_id=(px, py, c), device_id_type=MESH))
    return copies


_HBM = pl.BlockSpec(memory_space=pltpu.HBM)
_SEM = pl.BlockSpec(memory_space=pltpu.SEMAPHORE)
_DATAFLOW = pltpu.SideEffectType.DATAFLOW_SIDE_EFFECTING


def _ici_start(name, bigs, smalls):
    nb, n = len(bigs), len(bigs) + len(smalls)
    srcs = list(bigs) + list(smalls)
    lands = [lax.empty(g.shape, g.dtype) for g in bigs] + [lax.empty((N_DEV // 2,) + s.shape, s.dtype) for s in smalls]

    def body(*refs):
        src_refs, land_refs = refs[:n], refs[n:2 * n]
        send_sems, recv_sems = refs[2 * n], refs[2 * n + 1]
        token = refs[-1]
        for cp in _ici_copies(nb, n, src_refs, land_refs, send_sems, recv_sems):
            cp.start()
        token[...] = jnp.zeros_like(token)

    hbm = [pltpu.HBM(a.shape, a.dtype) for a in srcs + lands]
    args = [pltpu.with_memory_space_constraint(a, pltpu.HBM) for a in srcs + lands]
    out = pl.pallas_call(
        body, name=name,
        out_shape=[pltpu.SemaphoreType.DMA((3 * n,)), pltpu.SemaphoreType.DMA((3 * n,))] + hbm
                  + [jax.ShapeDtypeStruct((8, 128), F32)],
        in_specs=[_HBM] * (2 * n),
        out_specs=[_SEM, _SEM] + [_HBM] * (2 * n) + [pl.BlockSpec(memory_space=pltpu.VMEM)],
        input_output_aliases={i: 2 + i for i in range(2 * n)},
        compiler_params=pltpu.CompilerParams(has_side_effects=_DATAFLOW),
    )(*args)
    return out[0], out[1], list(out[2:2 + n]), list(out[2 + n:2 + 2 * n]), out[-1]


def _ici_wait(name, started, nb, not_before):
    send_sems, recv_sems, srcs, lands, token = started
    n = len(srcs)

    def body(*refs):
        src_refs, land_refs = refs[:n], refs[n:2 * n]
        send_refs, recv_refs = refs[2 * n], refs[2 * n + 1]
        for cp in _ici_copies(nb, n, src_refs, land_refs, send_refs, recv_refs):
            cp.wait_send()
            cp.wait_recv()

    hbm = [pltpu.HBM(a.shape, a.dtype) for a in srcs + lands]
    out = pl.pallas_call(
        body, name=name, out_shape=hbm,
        in_specs=[_HBM] * (2 * n) + [_SEM, _SEM] + [pl.BlockSpec(memory_space=pl.ANY)] * 2,
        out_specs=[_HBM] * (2 * n),
        input_output_aliases={i: i for i in range(2 * n)},
        compiler_params=pltpu.CompilerParams(has_side_effects=_DATAFLOW),
    )(*srcs, *lands, send_sems, recv_sems, not_before, token)
    return list(out[:n]), list(out[n:])


def _adamw_math(w, g, m, v):
    m2 = ADAM_B1 * m + (1.0 - ADAM_B1) * g
    v2 = ADAM_B2 * v + (1.0 - ADAM_B2) * (g * g)
    m_hat = m2 / (1.0 - ADAM_B1 ** ADAM_STEP)
    v_hat = v2 / (1.0 - ADAM_B2 ** ADAM_STEP)
    delta = -ADAM_LR * (m_hat / (jnp.sqrt(v_hat) + ADAM_EPS) + ADAM_WD * w)
    return delta, m2, v2


def _chip_terms(q, own, land_ref, n):
    return [jnp.where(q == s, own, land_ref[s]).astype(F32) for s in range(n)]


def _adamw_sum(name, land, own, qvec, w, m, v, l, prev, row_mult):
    NS, R, C = land.shape
    L = w.shape[0]
    tr = _tile(R, 256, row_mult)

    def body(*refs):
        q_ref, r_ref, o_ref, w_ref, m_ref, v_ref = refs[:6]
        g_ref, d_ref, nm_ref, nv_ref = refs[-4:]
        terms = _chip_terms(q_ref[0], o_ref[...], r_ref, NS)
        g = terms[0]
        for t in terms[1:]:
            g = g + t
        d, m2, v2 = _adamw_math(w_ref[...], g, m_ref[...], v_ref[...])
        g_ref[...] = g
        d_ref[...] = d
        nm_ref[...] = m2
        nv_ref[...] = v2

    blk = pl.BlockSpec((None, tr, C), lambda i, q: (l, i, 0))
    shp = jax.ShapeDtypeStruct((L, R, C), F32)
    in_specs = [pl.BlockSpec((NS, tr, C), lambda i, q: (0, i, 0)),
                pl.BlockSpec((None, tr, C), lambda i, q: (q[0], i, 0)), blk, blk, blk]
    args = [land, own, w, m, v]
    aliases = {}
    if prev is not None:
        in_specs += [pl.BlockSpec(memory_space=pl.ANY)] * 4
        args += list(prev)
        aliases = {6 + i: i for i in range(4)}
    grid_spec = pltpu.PrefetchScalarGridSpec(num_scalar_prefetch=1, grid=(R // tr,), in_specs=in_specs,
                                             out_specs=[blk, blk, blk, blk])
    return pl.pallas_call(
        body, name=name, grid_spec=grid_spec, out_shape=[shp, shp, shp, shp],
        input_output_aliases=aliases,
        compiler_params=_params(("arbitrary",)),
    )(qvec, *args)


def _adamw_plain(name, g, w, m, v):
    def body(g_ref, w_ref, m_ref, v_ref, d_ref, nm_ref, nv_ref):
        d, m2, v2 = _adamw_math(w_ref[...], g_ref[...], m_ref[...], v_ref[...])
        d_ref[...] = d
        nm_ref[...] = m2
        nv_ref[...] = v2

    shp = jax.ShapeDtypeStruct(g.shape, F32)
    return pl.pallas_call(body, name=name, out_shape=[shp, shp, shp], compiler_params=_params())(g, w, m, v)


SMALL = ["norm1_g", "q_norm_g", "k_norm_g", "sink", "sgu_ln_g", "sgu_ln_b", "w_s", "b_s",
         "attn_out_g", "sgu_out_g", "norm2_g", "conv_b"]
PACK_ALIGN = 1024


def _pack(pieces):
    flat = []
    for p in pieces:
        f = p.reshape(-1).astype(F32)
        pad = (-f.shape[0]) % PACK_ALIGN
        flat.append(jnp.pad(f, (0, pad)) if pad else f)
    return jnp.concatenate(flat).reshape(-1, 128)


def kernel(x, norm1_g, w_in, q_norm_g, k_norm_g, sink, sgu_ln_g, sgu_ln_b, w_s, b_s, attn_out_g, sgu_out_g, w_o, norm2_g, w_up, conv_w, conv_b, w_down, loss_target, m_norm1_g, m_w_in, m_q_norm_g, m_k_norm_g, m_sink, m_sgu_ln_g, m_sgu_ln_b, m_w_s, m_b_s, m_attn_out_g, m_sgu_out_g, m_w_o, m_norm2_g, m_w_up, m_conv_w, m_conv_b, m_w_down, v_norm1_g, v_w_in, v_q_norm_g, v_k_norm_g, v_sink, v_sgu_ln_g, v_sgu_ln_b, v_w_s, v_b_s, v_attn_out_g, v_sgu_out_g, v_w_o, v_norm2_g, v_w_up, v_conv_w, v_conv_b, v_w_down):
    weights = dict(norm1_g=norm1_g, w_in=w_in, q_norm_g=q_norm_g, k_norm_g=k_norm_g, sink=sink, sgu_ln_g=sgu_ln_g,
                   sgu_ln_b=sgu_ln_b, w_s=w_s, b_s=b_s, attn_out_g=attn_out_g, sgu_out_g=sgu_out_g, w_o=w_o,
                   norm2_g=norm2_g, w_up=w_up, conv_w=conv_w, conv_b=conv_b, w_down=w_down)
    mom_m = dict(norm1_g=m_norm1_g, w_in=m_w_in, q_norm_g=m_q_norm_g, k_norm_g=m_k_norm_g, sink=m_sink,
                 sgu_ln_g=m_sgu_ln_g, sgu_ln_b=m_sgu_ln_b, w_s=m_w_s, b_s=m_b_s, attn_out_g=m_attn_out_g,
                 sgu_out_g=m_sgu_out_g, w_o=m_w_o, norm2_g=m_norm2_g, w_up=m_w_up, conv_w=m_conv_w,
                 conv_b=m_conv_b, w_down=m_w_down)
    mom_v = dict(norm1_g=v_norm1_g, w_in=v_w_in, q_norm_g=v_q_norm_g, k_norm_g=v_k_norm_g, sink=v_sink,
                 sgu_ln_g=v_sgu_ln_g, sgu_ln_b=v_sgu_ln_b, w_s=v_w_s, b_s=v_b_s, attn_out_g=v_attn_out_g,
                 sgu_out_g=v_sgu_out_g, w_o=v_w_o, norm2_g=v_norm2_g, w_up=v_w_up, conv_w=v_conv_w,
                 conv_b=v_conv_b, w_down=v_w_down)
    order = ["norm1_g", "w_in", "q_norm_g", "k_norm_g", "sink", "sgu_ln_g", "sgu_ln_b", "w_s", "b_s",
             "attn_out_g", "sgu_out_g", "w_o", "norm2_g", "w_up", "conv_w", "conv_b", "w_down"]

    _, S, D = x.shape
    L = w_in.shape[0]
    AW = D // 2
    NQ = AW // HEAD
    NKV = max(1, NQ // 4)
    G = NQ // NKV
    KVW = NKV * HEAD
    GW = D - AW
    NG = GW // HEAD
    IN = AW + 2 * KVW + 2 * GW
    INS = w_in.shape[2]
    OS = w_o.shape[1]
    US = w_up.shape[2]
    DS = w_down.shape[1]
    F2 = US * N_DEV
    F = F2 // 2
    assert INS * N_DEV == IN and OS * N_DEV == D and DS * N_DEV == F and AW == GW and (3 * AW) % (2 * KVW) == 0
    cfg = dict(S=S, D=D, AW=AW, NQ=NQ, NKV=NKV, G=G, KVW=KVW, GW=GW, NG=NG, F=F, F2=F2)

    def row_windows(n):
        m = n // 2
        return (lambda ref, idx: ref.at[:, pl.ds(pl.multiple_of(idx * n, n), n), :],
                lambda ref, idx, h: ref.at[:, pl.ds(pl.multiple_of(idx * n + h * m, m), m), :],
                lambda ref, h: ref.at[:, pl.ds(h * m, m), :])

    def col_windows(n, rows):
        m = rows // 2
        return (lambda ref, idx: ref.at[:, :, pl.ds(pl.multiple_of(idx * n, n), n)],
                lambda ref, idx, h: ref.at[:, pl.ds(h * m, m), pl.ds(pl.multiple_of(idx * n, n), n)],
                lambda ref, h: ref.at[:, pl.ds(h * m, m), :])

    def lead_col_windows(n, lead):
        m = lead // 2
        return (lambda ref, idx: ref.at[:, :, pl.ds(pl.multiple_of(idx * n, n), n)],
                lambda ref, idx, h: ref.at[pl.ds(h * m, m), :, pl.ds(pl.multiple_of(idx * n, n), n)],
                lambda ref, h: ref.at[pl.ds(h * m, m)])

    def after(first, then):
        return lax.optimization_barrier((first, then))

    w_in_t, m_w_in_t, v_w_in_t = (jnp.swapaxes(a, 1, 2) for a in (w_in, m_w_in, v_w_in))

    ids = iter(range(64))
    w_in_f, w_o_f, w_up_f, w_down_f = [], [], [], []

    def gather_layer(l, not_before=None):
        in_shard = w_in_t[l:l + 1].astype(BF16)
        if not_before is not None:
            _, in_shard = after(not_before, in_shard)
        w_in_f.append(_all_gather("ag_in", next(ids), in_shard,
                                  jax.ShapeDtypeStruct((1, IN, D), BF16), row_windows(INS)))
        w_o_f.append(_all_gather("ag_o", next(ids), w_o[l:l + 1].astype(BF16),
                                 jax.ShapeDtypeStruct((1, D, D), BF16), row_windows(OS)))
        w_up_f.append(_all_gather("ag_up", next(ids), w_up[l:l + 1].astype(BF16),
                                  jax.ShapeDtypeStruct((1, D, F2), BF16), col_windows(US, D)))
        w_down_f.append(_all_gather("ag_down", next(ids), w_down[l:l + 1].astype(BF16),
                                    jax.ShapeDtypeStruct((1, F, D), BF16), row_windows(DS)))

    assert L % 2 == 0 and INS % 32 == 0 and OS % 32 == 0 and DS % 32 == 0 and D % 32 == 0
    gather_layer(0)
    conv_w_f = _all_gather("ag_conv_w", next(ids), conv_w, jax.ShapeDtypeStruct((L, 3, F2), F32),
                           lead_col_windows(US, L))

    n1g3, n2g3 = norm1_g.reshape(L, 1, D), norm2_g.reshape(L, 1, D)
    qg3, kg3 = q_norm_g.reshape(L, 1, HEAD), k_norm_g.reshape(L, 1, HEAD)
    lng3, lnb3 = sgu_ln_g.reshape(L, 1, GW), sgu_ln_b.reshape(L, 1, GW)
    ag3, og3 = attn_out_g.reshape(L, 1, AW), sgu_out_g.reshape(L, 1, GW)
    cb3 = conv_b.reshape(L, 1, F2)
    ws_b = w_s.astype(BF16)
    wst_b = jnp.swapaxes(w_s, 2, 3).astype(BF16)
    bs_b = jnp.broadcast_to(b_s[..., None], (L, NG, HEAD, HEAD))
    inv_freq = ROPE_THETA ** (-jnp.arange(0, HEAD, 2, dtype=F32) / HEAD)
    ang = jnp.arange(S, dtype=F32)[:, None] * inv_freq[None, :]
    cos2 = jnp.concatenate([jnp.cos(ang), jnp.cos(ang)], axis=1)
    sin2 = jnp.concatenate([-jnp.sin(ang), jnp.sin(ang)], axis=1)

    tn = 512
    t_in, t_d, t_f, t_f2 = _tile(IN, tn, 128), _tile(D, tn, 128), _tile(F, tn, 128), _tile(F2, tn, 128)
    tm_f = _tile(F, 512, 128)
    tm_s = _tile(S, 1024, 128)
    assert AW % t_in == 0 and (2 * KVW) % t_in == 0
    n_q, n_kv, n_g = AW // t_in, (2 * KVW) // t_in, (2 * GW) // t_in

    def in_tile(j):
        return jnp.where(j < n_q, j, jnp.where(j < n_q + n_g, j + n_kv, j - n_g))

    def w_spec(tk, tn_):
        return pl.BlockSpec((None, tk, tn_), lambda i, j, k: (0, k, j))

    def wt_spec(tn_, tk):
        return pl.BlockSpec((None, tn_, tk), lambda i, j, k: (0, j, k))

    def a_spec(tm, tk):
        return pl.BlockSpec((tm, tk), lambda i, j, k: (i, k))

    def at_spec(tk, tm):
        return pl.BlockSpec((tk, tm), lambda i, j, k: (k, i))

    def b_spec(tk, tn_):
        return pl.BlockSpec((tk, tn_), lambda i, j, k: (k, j))

    xs = x.reshape(S, D)
    saved = []
    cur = xs
    for l in range(L):
        h = _rms_fwd("rms1_fwd", cur, n1g3, l)
        z = _mm("mm_in", h, w_in_f[l], M=S, N=IN, K=D, tm=S, tn=t_in, tk=D, tb=True,
                a_spec=a_spec(S, D),
                b_spec=pl.BlockSpec((None, t_in, D), lambda i, j, k: (0, in_tile(j), 0)))
        q_r, k_r, v_b = _qkv_prep("qkv_prep", z, qg3, kg3, cos2, sin2, l, cfg)
        attn, mix_l = _attn_fwd("attn_fwd", q_r, k_r, v_b, sink, ag3, l, cfg)
        mix_r = _sgu_fwd("sgu_fwd", z, lng3, lnb3, ws_b, bs_b, og3, l, cfg)
        mixed = jnp.concatenate([mix_l, mix_r], axis=1)
        x1 = _mm("mm_o", mixed, w_o_f[l], M=S, N=D, K=D, tm=S, tn=t_d, tk=D,
                 a_spec=a_spec(S, D), b_spec=w_spec(D, t_d), res=cur)
        if l + 1 < L:
            gather_layer(l + 1, not_before=x1)
        h2 = _rms_fwd("rms2_fwd", x1, n2g3, l)
        ap = _mm("mm_up", h2, w_up_f[l], M=S, N=F2, K=D, tm=S, tn=t_f2, tk=D,
                 a_spec=a_spec(S, D), b_spec=w_spec(D, t_f2))
        y_b = _conv_glu_fwd("conv_glu_fwd", ap, conv_w_f, cb3, l, cfg)
        x2 = _mm("mm_down", y_b, w_down_f[l], M=S, N=D, K=F, tm=tm_s, tn=t_d, tk=F,
                 a_spec=a_spec(tm_s, F), b_spec=w_spec(F, t_d), res=x1)
        saved.append(dict(x=cur, h=h, z=z, q=q_r, k=k_r, v=v_b, attn=attn, mixed=mixed, x1=x1, h2=h2, ap=ap, y=y_b))
        cur = x2

    loss_tile, dx, dxb = _loss_bwd("loss", cur, loss_target.reshape(S, D))

    gS = [dict() for _ in range(L)]
    recv = [dict() for _ in range(L)]
    cvec = jnp.reshape(lax.axis_index("c"), (1,)).astype(jnp.int32)

    def rs_swap(name, g, R, C, small=None):
        g4 = g.reshape(N_DEV // 2, 2, R, C)
        got = _rs_d2d("rs_d2d_" + name, next(ids), [g4], [] if small is None else [small])
        return name, g4, got, small

    started = []

    def rs_finish(l, pending, then=None):
        name, g4, got, small = pending
        chip = _pair_sum("pair_sum_" + name, g4, got[0], cvec, 16)
        chip_small = [] if small is None else [_add2("pair_sum_small", small, got[1])]
        begun = _ici_start("rs_ici_" + name, [chip], chip_small)
        if then is not None:
            token, then = after(begun[4], then)
            begun = begun[:4] + (token,)
        started.append((l, name, begun))
        return then

    carried = None
    for l in reversed(range(L)):
        sv = saved[l]
        dy = _mm("mm_dy", dxb, w_down_f[l], M=S, N=F, K=D, tm=S, tn=t_f, tk=D, tb=True,
                 a_spec=a_spec(S, D), b_spec=wt_spec(t_f, D))
        if carried is not None:
            dy = rs_finish(l + 1, carried, dy)
        g_down = _mm("mm_gdown", sv["y"], dxb, M=F, N=D, K=S, tm=tm_f, tn=D, tk=S, ta=True,
                     a_spec=at_spec(S, tm_f), b_spec=b_spec(S, D), out_dtype=BF16)
        g_down, dy = after(g_down, dy)
        swap_down = rs_swap("w_down", g_down, DS, D)
        dap3, dcw, dcb = _glu_conv_bwd("glu_conv_bwd", dy, sv["ap"], conv_w_f, cb3, l, cfg)
        gS[l]["conv_w"] = jnp.concatenate([dcw[0], dcw[1]], axis=1)
        gS[l]["conv_b"] = jnp.concatenate([dcb[0], dcb[1]], axis=1).reshape(F2)
        dap3 = rs_finish(l, swap_down, dap3)
        dh2 = _mm("mm_dh2", dap3, w_up_f[l], M=S, N=D, K=F2, tm=tm_s, tn=t_d, tk=F, tb=True,
                  a_spec=pl.BlockSpec((None, tm_s, F), lambda i, j, k: (k, i, 0)),
                  b_spec=wt_spec(t_d, F))
        g_up = _mm("mm_gup", sv["h2"], dap3, M=D, N=F2, K=S, tm=D, tn=US, tk=S, ta=True,
                   a_spec=at_spec(S, D),
                   b_spec=pl.BlockSpec((None, S, US), lambda i, j, k: (j // (N_DEV // 2), 0, j % (N_DEV // 2))),
                   out_dtype=BF16, out_shape=(N_DEV, D, US),
                   out_spec=pl.BlockSpec((None, D, US), lambda i, j, k: (j, 0, 0)))
        g_up, dh2 = after(g_up, dh2)
        swap_up = rs_swap("w_up", g_up, D, US)
        dx1, dx1b, dg2 = _rms_bwd("rms2_bwd", sv["x1"], dh2, dx, n2g3, l)
        gS[l]["norm2_g"] = dg2.reshape(D)
        dmix = _mm("mm_dmix", dx1b, w_o_f[l], M=S, N=D, K=D, tm=S, tn=t_d, tk=D, tb=True,
                   a_spec=a_spec(S, D), b_spec=wt_spec(t_d, D))
        g_o = _mm("mm_go", sv["mixed"], dx1b, M=D, N=D, K=S, tm=D, tn=t_d, tk=S, ta=True,
                  a_spec=at_spec(S, D), b_spec=b_spec(S, t_d), out_dtype=BF16)
        g_o, dmix = after(g_o, dmix)
        swap_o = rs_swap("w_o", g_o, OS, D)
        dmix = rs_finish(l, swap_up, dmix)
        dq_r, dk_pad, dv_pad, dsink, dag = _attn_bwd("attn_bwd", sv["q"], sv["k"], sv["v"], sv["attn"], dmix,
                                                     sink, ag3, l, cfg)
        dq_r = rs_finish(l, swap_o, dq_r)
        gS[l]["sink"] = dsink[:, 0]
        gS[l]["attn_out_g"] = dag.reshape(AW)
        dzgu, dzgv, dws, dbs, dlng, dlnb, dog = _sgu_bwd("sgu_bwd", sv["z"], dmix, lng3, lnb3, ws_b, wst_b, bs_b,
                                                        og3, l, cfg)
        gS[l]["w_s"] = dws
        gS[l]["b_s"] = dbs[:, :, 0]
        gS[l]["sgu_ln_g"] = dlng.reshape(GW)
        gS[l]["sgu_ln_b"] = dlnb.reshape(GW)
        gS[l]["sgu_out_g"] = dog.reshape(GW)
        dzq, dzkv, dqg, dkg = _qkv_prep_bwd("qkv_prep_bwd", sv["z"], dq_r, dk_pad, dv_pad, qg3, kg3, cos2, sin2,
                                            l, cfg)
        gS[l]["q_norm_g"] = dqg.reshape(HEAD)
        gS[l]["k_norm_g"] = dkg.reshape(HEAD)
        dz = jnp.concatenate([dzq, dzkv, dzgu, dzgv], axis=1)
        dh = _mm("mm_dh", dz, w_in_f[l], M=S, N=D, K=IN, tm=tm_s, tn=t_d, tk=IN,
                 a_spec=a_spec(tm_s, IN), b_spec=w_spec(IN, t_d))
        g_in = _mm("mm_gin", dz, sv["h"], M=IN, N=D, K=S, tm=t_in, tn=D, tk=S, ta=True,
                   a_spec=at_spec(S, t_in), b_spec=b_spec(S, D), out_dtype=BF16)
        g_in, dh = after(g_in, dh)
        dx, dxb, dg1 = _rms_bwd("rms1_bwd", sv["x"], dh, dx1, n1g3, l)
        gS[l]["norm1_g"] = dg1.reshape(D)
        carried = rs_swap("w_in", g_in, INS, D, small=_pack([gS[l][n] for n in SMALL] + [gS[l]["conv_w"]]))
    dx = rs_finish(0, carried, dx)
    grad_x = dx.reshape(1, S, D)

    grads, deltas, new_m, new_v = {}, {}, {}, {}
    results, placed = {}, dx
    qvec = jnp.reshape(2 * lax.axis_index("x") + lax.axis_index("y"), (1,)).astype(jnp.int32)
    for l, name, begun in started:
        sent, landed = _ici_wait("rs_wait_" + name, begun, 1, placed)
        recv[l][name] = (sent, landed)
        wmv = (w_in_t, m_w_in_t, v_w_in_t) if name == "w_in" else (weights[name], mom_m[name], mom_v[name])
        results[name] = _adamw_sum("adamw_" + name, landed[0], sent[0], qvec, *wmv, l, results.get(name), 16)
        placed = results[name][0]
    _, loss_local = after(placed, loss_tile[0, 0])
    loss = lax.psum(loss_local, ("x", "y", "c"))
    results["w_in"] = [jnp.swapaxes(r, 1, 2) for r in results["w_in"]]
    for name, res in results.items():
        grads[name], deltas[name], new_m[name], new_v[name] = res

    def as_rows(a):
        flat = a.reshape(L, -1)
        short = (-flat.shape[1]) % 128
        if short:
            flat = jnp.pad(flat, ((0, 0), (0, short)))
        return flat.reshape(L, -1, 128)

    sizes = [weights[n][0].size for n in SMALL] + [3 * F2]
    offsets, off = [], 0
    for sz in sizes:
        offsets.append(off // 128)
        off += sz + (-sz) % PACK_ALIGN
    n_small = len(SMALL)
    rows_of = [-(-sz // 128) for sz in sizes]

    def small_body(*refs):
        q_ref = refs[0]
        lands, owns = refs[1:1 + L], refs[1 + L:1 + 2 * L]
        wmv_refs = refs[1 + 2 * L:1 + 2 * L + 3 * n_small]
        outs = refs[1 + 2 * L + 3 * n_small:]
        for l in range(L):
            for p in range(n_small + 1):
                rows = slice(offsets[p], offsets[p] + rows_of[p])
                terms = [jnp.where(q_ref[0] == s, owns[l][rows, :], lands[l][s, rows, :]) for s in range(N_DEV // 2)]
                g = terms[0]
                for t in terms[1:]:
                    g = g + t
                if p == n_small:
                    outs[4 * n_small][l] = g
                    continue
                d, m2, v2 = _adamw_math(wmv_refs[3 * p][l], g, wmv_refs[3 * p + 1][l], wmv_refs[3 * p + 2][l])
                for t, val in enumerate([g, d, m2, v2]):
                    outs[4 * p + t][l] = val

    vmem = pl.BlockSpec(memory_space=pltpu.VMEM)
    wmv_args = []
    for n in SMALL:
        wmv_args += [as_rows(weights[n]), as_rows(mom_m[n]), as_rows(mom_v[n])]
    small_out = pl.pallas_call(
        small_body, name="adamw_small",
        in_specs=[pl.BlockSpec(memory_space=pltpu.SMEM)] + [vmem] * (2 * L + 3 * n_small),
        out_specs=[vmem] * (4 * n_small + 1),
        out_shape=[jax.ShapeDtypeStruct((L, rows_of[p], 128), F32) for p in range(n_small) for _ in range(4)]
                  + [jax.ShapeDtypeStruct((L, rows_of[n_small], 128), F32)],
        compiler_params=_params(),
    )(qvec, *[recv[l]["w_in"][1][1] for l in range(L)], *[recv[l]["w_in"][0][1] for l in range(L)], *wmv_args)
    for p, n in enumerate(SMALL):
        for t, store in enumerate([grads, deltas, new_m, new_v]):
            store[n] = small_out[4 * p + t].reshape(L, -1)[:, :sizes[p]].reshape(weights[n].shape)
    conv_full = small_out[4 * n_small].reshape(L, 3, F2)
    me_i = _dev_index(_mesh_pos())
    g_cw = lax.dynamic_slice_in_dim(conv_full, me_i * US, US, axis=2)
    grads["conv_w"] = g_cw
    d_cw, m_cw, v_cw = _adamw_plain("adamw_conv_w", g_cw.reshape(L * 3, US), conv_w.reshape(L * 3, US),
                                    m_conv_w.reshape(L * 3, US), v_conv_w.reshape(L * 3, US))
    deltas["conv_w"] = d_cw.reshape(L, 3, US)
    new_m["conv_w"] = m_cw.reshape(L, 3, US)
    new_v["conv_w"] = v_cw.reshape(L, 3, US)

    return (loss, grad_x, *[grads[n] for n in order], *[deltas[n] for n in order],
            *[new_m[n] for n in order], *[new_v[n] for n in order])
```

```python
import jax
import jax.numpy as jnp
from jax import lax
from jax.experimental import pallas as pl
from jax.experimental.pallas import tpu as pltpu
from jax.experimental.pallas import tpu_sc as plsc

F32 = jnp.float32
BF16 = jnp.bfloat16
MESH = pl.DeviceIdType.MESH

N_DEV = 8
HEAD = 128
EPS = 1e-6
MASK_VALUE = -1e30
ROPE_THETA = 10000.0
GELU_C = 0.7978845608028654
GELU_A = 0.044715

ADAM_LR = 0.001
ADAM_B1 = 0.9
ADAM_B2 = 0.999
ADAM_EPS = 1e-08
ADAM_WD = 0.01
ADAM_STEP = 10

VMEM_LIMIT = 56 * 1024 * 1024


def _tile(n, pref, mult):
    best = None
    for t in range(mult, min(n, pref) + 1, mult):
        if n % t == 0:
            best = t
    return n if best is None else best


def _params(sem=None):
    kw = dict(vmem_limit_bytes=VMEM_LIMIT)
    if sem is not None:
        kw["dimension_semantics"] = sem
    return pltpu.CompilerParams(**kw)


def _gelu(x):
    return x * (0.5 * (1.0 + jnp.tanh(GELU_C * (x + GELU_A * (x * x * x)))))


def _gelu_grad(x):
    t = jnp.tanh(GELU_C * (x + GELU_A * (x * x * x)))
    return 0.5 * (1.0 + t) + 0.5 * x * (1.0 - t * t) * (GELU_C * (1.0 + 3.0 * GELU_A * (x * x)))


def _sigmoid(x):
    return 1.0 / (1.0 + jnp.exp(-x))


def _vec_spec(l, n):
    return pl.BlockSpec((None, 1, n), lambda *_: (l, 0, 0))


def _mm(name, a, b, *, M, N, K, tm, tn, tk, a_spec, b_spec, ta=False, tb=False, out_dtype=F32, res=None,
        out_shape=None, out_spec=None):
    nm, nn, nk = M // tm, N // tn, K // tk
    assert nm * tm == M and nn * tn == N and nk * tk == K
    assert not (ta and nk > 1)
    dims = (((1,), (1,)), ((), ())) if tb else (((1,), (0,)), ((), ()))

    def body(*refs):
        refs = list(refs)
        a_ref = refs.pop(0)
        b_ref = refs.pop(0)
        r_ref = refs.pop(0) if res is not None else None
        o_ref = refs.pop(0)
        acc = refs.pop(0) if nk > 1 else None
        at = refs.pop(0) if ta else None
        k = pl.program_id(2)
        if ta:
            @pl.when(pl.program_id(1) == 0)
            def _():
                at[...] = a_ref[...].T
            lhs = at[...]
        else:
            lhs = a_ref[...]
        p = lax.dot_general(lhs, b_ref[...], dims, preferred_element_type=F32)

        def finish(r):
            if r_ref is not None:
                r = r_ref[...] + r
            if len(o_ref.shape) == 3:
                w = o_ref.shape[2]
                for s in range(o_ref.shape[0]):
                    o_ref[s] = r[:, s * w:(s + 1) * w].astype(out_dtype)
            else:
                o_ref[...] = r.astype(out_dtype)

        if nk == 1:
            finish(p)
        else:
            @pl.when(k == 0)
            def _():
                acc[...] = p

            @pl.when(k > 0)
            def _():
                acc[...] += p

            @pl.when(k == nk - 1)
            def _():
                finish(acc[...])

    in_specs = [a_spec, b_spec]
    args = [a, b]
    if res is not None:
        in_specs.append(pl.BlockSpec((tm, tn), lambda i, j, k: (i, j)))
        args.append(res)
    scratch = []
    if nk > 1:
        scratch.append(pltpu.VMEM((tm, tn), F32))
    if ta:
        scratch.append(pltpu.VMEM((tm, tk), BF16))
    return pl.pallas_call(
        body, name=name, grid=(nm, nn, nk),
        in_specs=in_specs,
        out_specs=pl.BlockSpec((tm, tn), lambda i, j, k: (i, j)) if out_spec is None else out_spec,
        out_shape=jax.ShapeDtypeStruct((M, N) if out_shape is None else out_shape, out_dtype),
        scratch_shapes=scratch,
        compiler_params=_params(("arbitrary", "arbitrary", "arbitrary")),
    )(*args)


def _rms_fwd(name, x, g3, l):
    S, D = x.shape
    tr = _tile(S, 512, 16)

    def body(x_ref, g_ref, h_ref):
        xv = x_ref[...]
        r = lax.rsqrt(jnp.mean(xv * xv, axis=-1, keepdims=True) + EPS)
        h_ref[...] = ((xv * r) * g_ref[...]).astype(BF16)

    return pl.pallas_call(
        body, name=name, grid=(S // tr,),
        in_specs=[pl.BlockSpec((tr, D), lambda i: (i, 0)), _vec_spec(l, D)],
        out_specs=pl.BlockSpec((tr, D), lambda i: (i, 0)),
        out_shape=jax.ShapeDtypeStruct((S, D), BF16),
        compiler_params=_params(("arbitrary",)),
    )(x, g3)


def _rope(t, cos2, sin2):
    return t * cos2 + pltpu.roll(t, HEAD // 2, axis=1) * sin2


def _qkv_prep(name, z, qg3, kg3, cos2, sin2, l, cfg):
    S, AW, KVW, NQ, NKV = cfg["S"], cfg["AW"], cfg["KVW"], cfg["NQ"], cfg["NKV"]
    tr = _tile(S, 512, 16)
    kv_blk = (3 * AW) // (2 * KVW)

    def body(zq_ref, zkv_ref, qg_ref, kg_ref, c_ref, s_ref, q_ref, k_ref, v_ref):
        cosv, sinv = c_ref[...], s_ref[...]

        def norm_rope(t, g):
            r = lax.rsqrt(jnp.mean(t * t, axis=-1, keepdims=True) + EPS)
            return _rope((t * r) * g, cosv, sinv)

        for h in range(NQ):
            sl = slice(h * HEAD, (h + 1) * HEAD)
            q_ref[:, sl] = norm_rope(zq_ref[:, sl], qg_ref[...]).astype(BF16)
        for h in range(NKV):
            sl = slice(h * HEAD, (h + 1) * HEAD)
            k_ref[:, sl] = norm_rope(zkv_ref[:, sl], kg_ref[...]).astype(BF16)
        v_ref[...] = zkv_ref[:, KVW:].astype(BF16)

    return pl.pallas_call(
        body, name=name, grid=(S // tr,),
        in_specs=[pl.BlockSpec((tr, AW), lambda i: (i, 0)),
                  pl.BlockSpec((tr, 2 * KVW), lambda i: (i, kv_blk)),
                  _vec_spec(l, HEAD), _vec_spec(l, HEAD),
                  pl.BlockSpec((tr, HEAD), lambda i: (i, 0)),
                  pl.BlockSpec((tr, HEAD), lambda i: (i, 0))],
        out_specs=[pl.BlockSpec((tr, AW), lambda i: (i, 0)),
                   pl.BlockSpec((tr, KVW), lambda i: (i, 0)),
                   pl.BlockSpec((tr, KVW), lambda i: (i, 0))],
        out_shape=[jax.ShapeDtypeStruct((S, AW), BF16),
                   jax.ShapeDtypeStruct((S, KVW), BF16),
                   jax.ShapeDtypeStruct((S, KVW), BF16)],
        compiler_params=_params(("arbitrary",)),
    )(z, z, qg3, kg3, cos2, sin2)


def _band_specs(width, nb):
    return [pl.BlockSpec((HEAD, width), lambda n: (jnp.maximum(n - 1, 0), 0)),
            pl.BlockSpec((HEAD, width), lambda n: (n, 0)),
            pl.BlockSpec((HEAD, width), lambda n: (jnp.minimum(n + 1, nb - 1), 0))]


def _attn_probs(qs, kj, n, sink_of_row, S, G):
    s = lax.dot_general(qs, kj, (((1,), (1,)), ((), ())), preferred_element_type=F32) * (HEAD ** -0.5)
    rows = lax.broadcasted_iota(jnp.int32, (G * HEAD, 3 * HEAD), 0)
    cols = lax.broadcasted_iota(jnp.int32, (G * HEAD, 3 * HEAD), 1)
    qi = rows & (HEAD - 1)
    kpos = n * HEAD - HEAD + cols
    valid = (cols >= qi) & (cols <= qi + 2 * HEAD) & (kpos >= 0) & (kpos < S)
    s = jnp.where(valid, s, MASK_VALUE)
    m = jnp.maximum(jnp.max(s, axis=-1, keepdims=True), sink_of_row)
    p = jnp.exp(s - m)
    e_sink = jnp.exp(sink_of_row - m)
    inv = 1.0 / (jnp.sum(p, axis=-1, keepdims=True) + e_sink)
    return p * inv, e_sink * inv


def _sink_rows(sink_ref, l, j, G):
    hidx = lax.broadcasted_iota(jnp.int32, (G * HEAD, 1), 0) // HEAD
    col = jnp.full((G * HEAD, 1), sink_ref[l, j * G], F32)
    for g in range(1, G):
        col = jnp.where(hidx == g, sink_ref[l, j * G + g], col)
    return col


def _attn_fwd(name, q, k, v, sink, ag3, l, cfg):
    S, AW, KVW, NKV, G = cfg["S"], cfg["AW"], cfg["KVW"], cfg["NKV"], cfg["G"]
    nb = S // HEAD

    def body(q_ref, kp, kc, kn, vp, vc, vn, sink_ref, ag_ref, a_ref, mix_ref):
        n = pl.program_id(0)
        kb = jnp.concatenate([kp[...], kc[...], kn[...]], axis=0)
        vb = jnp.concatenate([vp[...], vc[...], vn[...]], axis=0)
        for j in range(NKV):
            sl = slice(j * HEAD, (j + 1) * HEAD)
            qs = jnp.concatenate([q_ref[:, (j * G + g) * HEAD:(j * G + g + 1) * HEAD] for g in range(G)], axis=0)
            probs, _ = _attn_probs(qs, kb[:, sl], n, _sink_rows(sink_ref, l, j, G), S, G)
            o = jnp.dot(probs.astype(BF16), vb[:, sl], preferred_element_type=F32)
            for g in range(G):
                a_ref[:, (j * G + g) * HEAD:(j * G + g + 1) * HEAD] = o[g * HEAD:(g + 1) * HEAD]
        a = a_ref[...]
        r = lax.rsqrt(jnp.mean(a * a, axis=-1, keepdims=True) + EPS)
        mix_ref[...] = ((a * r) * ag_ref[...]).astype(BF16)

    return pl.pallas_call(
        body, name=name, grid=(nb,),
        in_specs=[pl.BlockSpec((HEAD, AW), lambda n: (n, 0))] + _band_specs(KVW, nb) + _band_specs(KVW, nb)
                 + [pl.BlockSpec(memory_space=pltpu.SMEM), _vec_spec(l, AW)],
        out_specs=[pl.BlockSpec((HEAD, AW), lambda n: (n, 0)), pl.BlockSpec((HEAD, AW), lambda n: (n, 0))],
        out_shape=[jax.ShapeDtypeStruct((S, AW), F32), jax.ShapeDtypeStruct((S, AW), BF16)],
        compiler_params=_params(("arbitrary",)),
    )(q, k, k, k, v, v, v, sink, ag3)


def _sgu_forward_math(gu, gv, lng, lnb):
    u = _gelu(gu)
    vv = _gelu(gv)
    mu = jnp.mean(vv, axis=-1, keepdims=True)
    xc = vv - mu
    rstd = lax.rsqrt(jnp.mean(xc * xc, axis=-1, keepdims=True) + EPS)
    xhat = xc * rstd
    vn = xhat * lng + lnb
    return u, xhat, rstd, vn


def _sgu_fwd(name, z, lng3, lnb3, ws_b, bs_b, og3, l, cfg):
    S, GW, NG = cfg["S"], cfg["GW"], cfg["NG"]

    def body(gu_ref, gv_ref, lng_ref, lnb_ref, ws_ref, bs_ref, og_ref, mix_ref, sg_ref):
        u, _, _, vn = _sgu_forward_math(gu_ref[...], gv_ref[...], lng_ref[...], lnb_ref[...])
        vnb = vn.astype(BF16)
        for h in range(NG):
            sl = slice(h * HEAD, (h + 1) * HEAD)
            f = jnp.dot(ws_ref[h], vnb[:, sl], preferred_element_type=F32) + bs_ref[h]
            sg_ref[:, sl] = u[:, sl] * f
        sg = sg_ref[...]
        r = lax.rsqrt(jnp.mean(sg * sg, axis=-1, keepdims=True) + EPS)
        mix_ref[...] = ((sg * r) * og_ref[...]).astype(BF16)

    return pl.pallas_call(
        body, name=name, grid=(S // HEAD,),
        in_specs=[pl.BlockSpec((HEAD, GW), lambda c: (c, 1)),
                  pl.BlockSpec((HEAD, GW), lambda c: (c, 2)),
                  _vec_spec(l, GW), _vec_spec(l, GW),
                  pl.BlockSpec((None, NG, HEAD, HEAD), lambda c: (l, 0, 0, 0)),
                  pl.BlockSpec((None, NG, HEAD, HEAD), lambda c: (l, 0, 0, 0)),
                  _vec_spec(l, GW)],
        out_specs=pl.BlockSpec((HEAD, GW), lambda c: (c, 0)),
        out_shape=jax.ShapeDtypeStruct((S, GW), BF16),
        scratch_shapes=[pltpu.VMEM((HEAD, GW), F32)],
        compiler_params=_params(("arbitrary",)),
    )(z, z, lng3, lnb3, ws_b, bs_b, og3)


CONV_HALO = 8
CONV_ROWS = 128
CONV_COLS = 512


def _for_row_windows(S, fn):
    R, W = CONV_ROWS, CONV_ROWS + 2 * CONV_HALO
    n = S // R
    assert n * R == S and n >= 2
    fn(0, 0, 0, "top")
    if n > 2:
        def mid(k, carry):
            fn(pl.multiple_of(k * R - CONV_HALO, CONV_HALO), CONV_HALO, pl.multiple_of(k * R, R), "mid")
            return carry

        lax.fori_loop(1, n - 1, mid, 0, unroll=2 if (n - 2) % 2 == 0 else 1)
    fn(S - W, 2 * CONV_HALO, S - R, "bottom")


def _shifts(t, edge):
    W = t.shape[0]
    dn, up = pltpu.roll(t, 1, axis=0), pltpu.roll(t, W - 1, axis=0)
    if edge == "top":
        dn = jnp.where(lax.broadcasted_iota(jnp.int32, t.shape, 0) == 0, 0.0, dn)
    if edge == "bottom":
        up = jnp.where(lax.broadcasted_iota(jnp.int32, t.shape, 0) == W - 1, 0.0, up)
    return dn, up


def _conv3(t, w, b, edge):
    dn, up = _shifts(t, edge)
    return ((b + dn * w[0:1]) + t * w[1:2]) + up * w[2:3]


def _conv_glu_fwd(name, ap, cw, cb3, l, cfg):
    S, F = cfg["S"], cfg["F"]
    tc = _tile(F, CONV_COLS, 128)
    nf = F // tc
    R, W = CONV_ROWS, CONV_ROWS + 2 * CONV_HALO

    def body(g_ref, u_ref, wg_ref, wu_ref, bg_ref, bu_ref, y_ref):
        def window(start, lo, out0, edge):
            rows = pl.ds(start, W)
            for c0 in range(0, tc, 128):
                cols = slice(c0, c0 + 128)
                ag = _conv3(g_ref[rows, cols], wg_ref[:, cols], bg_ref[:, cols], edge)
                au = _conv3(u_ref[rows, cols], wu_ref[:, cols], bu_ref[:, cols], edge)
                y = (ag * _sigmoid(ag)) * au
                y_ref[pl.ds(out0, R), cols] = y[lo:lo + R].astype(BF16)

        _for_row_windows(S, window)

    return pl.pallas_call(
        body, name=name, grid=(nf,),
        in_specs=[pl.BlockSpec((S, tc), lambda j: (0, j)),
                  pl.BlockSpec((S, tc), lambda j: (0, j + nf)),
                  pl.BlockSpec((None, 3, tc), lambda j: (l, 0, j)),
                  pl.BlockSpec((None, 3, tc), lambda j: (l, 0, j + nf)),
                  pl.BlockSpec((None, 1, tc), lambda j: (l, 0, j)),
                  pl.BlockSpec((None, 1, tc), lambda j: (l, 0, j + nf))],
        out_specs=pl.BlockSpec((S, tc), lambda j: (0, j)),
        out_shape=jax.ShapeDtypeStruct((S, F), BF16),
        compiler_params=_params(("arbitrary",)),
    )(ap, ap, cw, cw, cb3, cb3)


def _loss_bwd(name, y, target):
    S, D = y.shape
    tr = _tile(S, 512, 16)

    def body(y_ref, t_ref, loss_ref, d_ref, db_ref):
        @pl.when(pl.program_id(0) == 0)
        def _():
            loss_ref[...] = jnp.zeros_like(loss_ref)

        err = y_ref[...] - t_ref[...]
        part = 0.5 * jnp.sum(jnp.mean(err * err, axis=-1, keepdims=True), axis=0, keepdims=True)
        loss_ref[...] += jnp.broadcast_to(part, loss_ref.shape)
        d = err * (1.0 / D)
        d_ref[...] = d
        db_ref[...] = d.astype(BF16)

    return pl.pallas_call(
        body, name=name, grid=(S // tr,),
        in_specs=[pl.BlockSpec((tr, D), lambda i: (i, 0)), pl.BlockSpec((tr, D), lambda i: (i, 0))],
        out_specs=[pl.BlockSpec((8, 128), lambda i: (0, 0)),
                   pl.BlockSpec((tr, D), lambda i: (i, 0)),
                   pl.BlockSpec((tr, D), lambda i: (i, 0))],
        out_shape=[jax.ShapeDtypeStruct((8, 128), F32),
                   jax.ShapeDtypeStruct((S, D), F32),
                   jax.ShapeDtypeStruct((S, D), BF16)],
        compiler_params=_params(("arbitrary",)),
    )(y, target)


def _rms_bwd(name, x, dh, dres, g3, l):
    S, D = x.shape
    tr = _tile(S, 512, 16)

    def body(x_ref, dh_ref, dr_ref, g_ref, dx_ref, dxb_ref, dg_ref):
        @pl.when(pl.program_id(0) == 0)
        def _():
            dg_ref[...] = jnp.zeros_like(dg_ref)

        xv = x_ref[...]
        dhv = dh_ref[...]
        r = lax.rsqrt(jnp.mean(xv * xv, axis=-1, keepdims=True) + EPS)
        xhat = xv * r
        dhg = dhv * g_ref[...]
        dx = dr_ref[...] + r * (dhg - xhat * jnp.mean(dhg * xhat, axis=-1, keepdims=True))
        dx_ref[...] = dx
        dxb_ref[...] = dx.astype(BF16)
        dg_ref[...] += jnp.sum(dhv * xhat, axis=0, keepdims=True)

    return pl.pallas_call(
        body, name=name, grid=(S // tr,),
        in_specs=[pl.BlockSpec((tr, D), lambda i: (i, 0)), pl.BlockSpec((tr, D), lambda i: (i, 0)),
                  pl.BlockSpec((tr, D), lambda i: (i, 0)), _vec_spec(l, D)],
        out_specs=[pl.BlockSpec((tr, D), lambda i: (i, 0)), pl.BlockSpec((tr, D), lambda i: (i, 0)),
                   pl.BlockSpec((1, D), lambda i: (0, 0))],
        out_shape=[jax.ShapeDtypeStruct((S, D), F32), jax.ShapeDtypeStruct((S, D), BF16),
                   jax.ShapeDtypeStruct((1, D), F32)],
        compiler_params=_params(("arbitrary",)),
    )(x, dh, dres, g3)


def _glu_conv_bwd(name, dy, ap, cw, cb3, l, cfg):
    S, F = cfg["S"], cfg["F"]
    tc = _tile(F, CONV_COLS, 128)
    nf = F // tc
    R, W = CONV_ROWS, CONV_ROWS + 2 * CONV_HALO

    def body(dy_ref, g_ref, u_ref, wg_ref, wu_ref, bg_ref, bu_ref, dap_ref, dw_ref, db_ref, acc):
        acc[...] = jnp.zeros_like(acc)

        def window(start, lo, out0, edge):
            rows = pl.ds(start, W)
            for c0 in range(0, tc, 128):
                cols = slice(c0, c0 + 128)
                apg, apu, dyv = g_ref[rows, cols], u_ref[rows, cols], dy_ref[rows, cols]
                wg, wu = wg_ref[:, cols], wu_ref[:, cols]
                ag = _conv3(apg, wg, bg_ref[:, cols], edge)
                au = _conv3(apu, wu, bu_ref[:, cols], edge)
                sig = _sigmoid(ag)
                da_u = dyv * (ag * sig)
                da_g = (dyv * au) * (sig * (1.0 + ag * (1.0 - sig)))

                def add(i, prod):
                    acc[i, :, cols] += jnp.sum(prod[lo:lo + R].reshape(R // 8, 8, 128), axis=0)

                for half, (da, t, w) in enumerate([(da_g, apg, wg), (da_u, apu, wu)]):
                    prv, nxt = _shifts(da, edge)
                    dap = (nxt * w[0:1] + da * w[1:2]) + prv * w[2:3]
                    dap_ref[half, pl.ds(out0, R), cols] = dap[lo:lo + R].astype(BF16)
                    add(3 * half, nxt * t)
                    add(3 * half + 1, da * t)
                    add(3 * half + 2, prv * t)
                    add(6 + half, da)

        _for_row_windows(S, window)
        col = [jnp.sum(acc[i], axis=0, keepdims=True) for i in range(8)]
        dw_ref[0] = jnp.concatenate(col[0:3], axis=0)
        dw_ref[1] = jnp.concatenate(col[3:6], axis=0)
        db_ref[0] = col[6]
        db_ref[1] = col[7]

    return pl.pallas_call(
        body, name=name, grid=(nf,),
        in_specs=[pl.BlockSpec((S, tc), lambda j: (0, j)),
                  pl.BlockSpec((S, tc), lambda j: (0, j)),
                  pl.BlockSpec((S, tc), lambda j: (0, j + nf)),
                  pl.BlockSpec((None, 3, tc), lambda j: (l, 0, j)),
                  pl.BlockSpec((None, 3, tc), lambda j: (l, 0, j + nf)),
                  pl.BlockSpec((None, 1, tc), lambda j: (l, 0, j)),
                  pl.BlockSpec((None, 1, tc), lambda j: (l, 0, j + nf))],
        out_specs=[pl.BlockSpec((2, S, tc), lambda j: (0, 0, j)),
                   pl.BlockSpec((2, 3, tc), lambda j: (0, 0, j)),
                   pl.BlockSpec((2, 1, tc), lambda j: (0, 0, j))],
        out_shape=[jax.ShapeDtypeStruct((2, S, F), BF16),
                   jax.ShapeDtypeStruct((2, 3, F), F32),
                   jax.ShapeDtypeStruct((2, 1, F), F32)],
        scratch_shapes=[pltpu.VMEM((8, 8, tc), F32)],
        compiler_params=_params(("arbitrary",)),
    )(dy, ap, ap, cw, cw, cb3, cb3)


def _attn_bwd(name, q, k, v, attn, dmix, sink, ag3, l, cfg):
    S, AW, KVW, NQ, NKV, G = cfg["S"], cfg["AW"], cfg["KVW"], cfg["NQ"], cfg["NKV"], cfg["G"]
    nb = S // HEAD
    scale = HEAD ** -0.5

    def body(q_ref, kp, kc, kn, vp, vc, vn, a_ref, dm_ref, sink_ref, ag_ref,
             dq_ref, dk_ref, dv_ref, dsink_ref, dag_ref, da_scr):
        n = pl.program_id(0)

        @pl.when(n == 0)
        def _():
            dk_ref[...] = jnp.zeros_like(dk_ref)
            dv_ref[...] = jnp.zeros_like(dv_ref)
            dsink_ref[...] = jnp.zeros_like(dsink_ref)
            dag_ref[...] = jnp.zeros_like(dag_ref)

        a = a_ref[...]
        dm = dm_ref[...]
        r = lax.rsqrt(jnp.mean(a * a, axis=-1, keepdims=True) + EPS)
        xhat = a * r
        dmg = dm * ag_ref[...]
        da_scr[...] = r * (dmg - xhat * jnp.mean(dmg * xhat, axis=-1, keepdims=True))
        dag_ref[...] += jnp.sum(dm * xhat, axis=0, keepdims=True)

        kb = jnp.concatenate([kp[...], kc[...], kn[...]], axis=0)
        vb = jnp.concatenate([vp[...], vc[...], vn[...]], axis=0)
        band = pl.ds(pl.multiple_of(n * HEAD, HEAD), 3 * HEAD)
        for j in range(NKV):
            sl = slice(j * HEAD, (j + 1) * HEAD)
            heads = [slice((j * G + g) * HEAD, (j * G + g + 1) * HEAD) for g in range(G)]
            qs = jnp.concatenate([q_ref[:, hs] for hs in heads], axis=0)
            do = jnp.concatenate([da_scr[:, hs] for hs in heads], axis=0)
            kj, vj = kb[:, sl], vb[:, sl]
            probs, p_sink = _attn_probs(qs, kj, n, _sink_rows(sink_ref, l, j, G), S, G)
            dob = do.astype(BF16)
            dprobs = lax.dot_general(dob, vj, (((1,), (1,)), ((), ())), preferred_element_type=F32)
            delta = jnp.sum(dprobs * probs, axis=-1, keepdims=True)
            ds = (probs * (dprobs - delta)) * scale
            dsb = ds.astype(BF16)
            dsk = -(p_sink * delta)
            dq = jnp.dot(dsb, kj, preferred_element_type=F32)
            for g in range(G):
                dq_ref[:, heads[g]] = dq[g * HEAD:(g + 1) * HEAD]
                part = jnp.sum(dsk[g * HEAD:(g + 1) * HEAD], axis=0, keepdims=True)
                dsink_ref[j * G + g:j * G + g + 1, :] += jnp.broadcast_to(part, (1, HEAD))
            dk_ref[band, sl] += lax.dot_general(dsb, qs, (((0,), (0,)), ((), ())), preferred_element_type=F32)
            dv_ref[band, sl] += lax.dot_general(probs.astype(BF16), dob, (((0,), (0,)), ((), ())),
                                                preferred_element_type=F32)

    return pl.pallas_call(
        body, name=name, grid=(nb,),
        in_specs=[pl.BlockSpec((HEAD, AW), lambda n: (n, 0))] + _band_specs(KVW, nb) + _band_specs(KVW, nb)
                 + [pl.BlockSpec((HEAD, AW), lambda n: (n, 0)),
                    pl.BlockSpec((HEAD, AW), lambda n: (n, 0)),
                    pl.BlockSpec(memory_space=pltpu.SMEM), _vec_spec(l, AW)],
        out_specs=[pl.BlockSpec((HEAD, AW), lambda n: (n, 0)),
                   pl.BlockSpec((S + 2 * HEAD, KVW), lambda n: (0, 0)),
                   pl.BlockSpec((S + 2 * HEAD, KVW), lambda n: (0, 0)),
                   pl.BlockSpec((NQ, HEAD), lambda n: (0, 0)),
                   pl.BlockSpec((1, AW), lambda n: (0, 0))],
        out_shape=[jax.ShapeDtypeStruct((S, AW), F32),
                   jax.ShapeDtypeStruct((S + 2 * HEAD, KVW), F32),
                   jax.ShapeDtypeStruct((S + 2 * HEAD, KVW), F32),
                   jax.ShapeDtypeStruct((NQ, HEAD), F32),
                   jax.ShapeDtypeStruct((1, AW), F32)],
        scratch_shapes=[pltpu.VMEM((HEAD, AW), F32)],
        compiler_params=_params(("arbitrary",)),
    )(q, k, k, k, v, v, v, attn, dmix, sink, ag3)


def _qkv_prep_bwd(name, z, dq, dk_pad, dv_pad, qg3, kg3, cos2, sin2, l, cfg):
    S, AW, KVW, NQ, NKV = cfg["S"], cfg["AW"], cfg["KVW"], cfg["NQ"], cfg["NKV"]
    kv_blk = (3 * AW) // (2 * KVW)

    def body(zq_ref, zkv_ref, dq_ref, dk_ref, dv_ref, qg_ref, kg_ref, c_ref, s_ref,
             dzq_ref, dzkv_ref, dqg_ref, dkg_ref):
        @pl.when(pl.program_id(0) == 0)
        def _():
            dqg_ref[...] = jnp.zeros_like(dqg_ref)
            dkg_ref[...] = jnp.zeros_like(dkg_ref)

        cosv, sinv = c_ref[...], s_ref[...]

        def back(t, dr, g):
            r = lax.rsqrt(jnp.mean(t * t, axis=-1, keepdims=True) + EPS)
            xhat = t * r
            dn = dr * cosv + pltpu.roll(dr * sinv, HEAD // 2, axis=1)
            dxh = dn * g
            dt = r * (dxh - xhat * jnp.mean(dxh * xhat, axis=-1, keepdims=True))
            return dt, jnp.sum(dn * xhat, axis=0, keepdims=True)

        gq = jnp.zeros((1, HEAD), F32)
        for h in range(NQ):
            sl = slice(h * HEAD, (h + 1) * HEAD)
            dt, gpart = back(zq_ref[:, sl], dq_ref[:, sl], qg_ref[...])
            dzq_ref[:, sl] = dt.astype(BF16)
            gq = gq + gpart
        dqg_ref[...] += gq
        gk = jnp.zeros((1, HEAD), F32)
        for h in range(NKV):
            sl = slice(h * HEAD, (h + 1) * HEAD)
            dt, gpart = back(zkv_ref[:, sl], dk_ref[:, sl], kg_ref[...])
            dzkv_ref[:, sl] = dt.astype(BF16)
            gk = gk + gpart
        dkg_ref[...] += gk
        dzkv_ref[:, KVW:] = dv_ref[...].astype(BF16)

    return pl.pallas_call(
        body, name=name, grid=(S // HEAD,),
        in_specs=[pl.BlockSpec((HEAD, AW), lambda i: (i, 0)),
                  pl.BlockSpec((HEAD, 2 * KVW), lambda i: (i, kv_blk)),
                  pl.BlockSpec((HEAD, AW), lambda i: (i, 0)),
                  pl.BlockSpec((HEAD, KVW), lambda i: (i + 1, 0)),
                  pl.BlockSpec((HEAD, KVW), lambda i: (i + 1, 0)),
                  _vec_spec(l, HEAD), _vec_spec(l, HEAD),
                  pl.BlockSpec((HEAD, HEAD), lambda i: (i, 0)),
                  pl.BlockSpec((HEAD, HEAD), lambda i: (i, 0))],
        out_specs=[pl.BlockSpec((HEAD, AW), lambda i: (i, 0)),
                   pl.BlockSpec((HEAD, 2 * KVW), lambda i: (i, 0)),
                   pl.BlockSpec((1, HEAD), lambda i: (0, 0)),
                   pl.BlockSpec((1, HEAD), lambda i: (0, 0))],
        out_shape=[jax.ShapeDtypeStruct((S, AW), BF16),
                   jax.ShapeDtypeStruct((S, 2 * KVW), BF16),
                   jax.ShapeDtypeStruct((1, HEAD), F32),
                   jax.ShapeDtypeStruct((1, HEAD), F32)],
        compiler_params=_params(("arbitrary",)),
    )(z, z, dq, dk_pad, dv_pad, qg3, kg3, cos2, sin2)


def _sgu_bwd(name, z, dmix, lng3, lnb3, ws_b, wst_b, bs_b, og3, l, cfg):
    S, GW, NG = cfg["S"], cfg["GW"], cfg["NG"]

    def body(gu_ref, gv_ref, dm_ref, lng_ref, lnb_ref, ws_ref, wst_ref, bs_ref, og_ref,
             dgu_ref, dgv_ref, dws_ref, dbs_ref, dlng_ref, dlnb_ref, dog_ref, sg_scr, f_scr, dvn_scr):
        @pl.when(pl.program_id(0) == 0)
        def _():
            dws_ref[...] = jnp.zeros_like(dws_ref)
            dbs_ref[...] = jnp.zeros_like(dbs_ref)
            dlng_ref[...] = jnp.zeros_like(dlng_ref)
            dlnb_ref[...] = jnp.zeros_like(dlnb_ref)
            dog_ref[...] = jnp.zeros_like(dog_ref)

        gu, gv = gu_ref[...], gv_ref[...]
        lng = lng_ref[...]
        u, xhat, rstd, vn = _sgu_forward_math(gu, gv, lng, lnb_ref[...])
        vnb = vn.astype(BF16)
        for h in range(NG):
            sl = slice(h * HEAD, (h + 1) * HEAD)
            f = jnp.dot(ws_ref[h], vnb[:, sl], preferred_element_type=F32) + bs_ref[h]
            f_scr[:, sl] = f
            sg_scr[:, sl] = u[:, sl] * f
        sg = sg_scr[...]
        dm = dm_ref[...]
        r = lax.rsqrt(jnp.mean(sg * sg, axis=-1, keepdims=True) + EPS)
        sghat = sg * r
        dmg = dm * og_ref[...]
        dsg = r * (dmg - sghat * jnp.mean(dmg * sghat, axis=-1, keepdims=True))
        dog_ref[...] += jnp.sum(dm * sghat, axis=0, keepdims=True)
        du = dsg * f_scr[...]
        df = dsg * u
        dfb = df.astype(BF16)
        for h in range(NG):
            sl = slice(h * HEAD, (h + 1) * HEAD)
            dvn_scr[:, sl] = jnp.dot(wst_ref[h], dfb[:, sl], preferred_element_type=F32)
            dws_ref[h] += lax.dot_general(dfb[:, sl], vnb[:, sl], (((1,), (1,)), ((), ())),
                                          preferred_element_type=F32)
            dbs_ref[h] += jnp.broadcast_to(jnp.sum(df[:, sl], axis=-1, keepdims=True), (HEAD, HEAD))
        dvn = dvn_scr[...]
        dlng_ref[...] += jnp.sum(dvn * xhat, axis=0, keepdims=True)
        dlnb_ref[...] += jnp.sum(dvn, axis=0, keepdims=True)
        dxh = dvn * lng
        dvv = rstd * ((dxh - jnp.mean(dxh, axis=-1, keepdims=True))
                      - xhat * jnp.mean(dxh * xhat, axis=-1, keepdims=True))
        dgu_ref[...] = (du * _gelu_grad(gu)).astype(BF16)
        dgv_ref[...] = (dvv * _gelu_grad(gv)).astype(BF16)

    vec = pl.BlockSpec((1, GW), lambda c: (0, 0))
    mat = pl.BlockSpec((NG, HEAD, HEAD), lambda c: (0, 0, 0))
    wsp = pl.BlockSpec((None, NG, HEAD, HEAD), lambda c: (l, 0, 0, 0))
    return pl.pallas_call(
        body, name=name, grid=(S // HEAD,),
        in_specs=[pl.BlockSpec((HEAD, GW), lambda c: (c, 1)),
                  pl.BlockSpec((HEAD, GW), lambda c: (c, 2)),
                  pl.BlockSpec((HEAD, GW), lambda c: (c, 1)),
                  _vec_spec(l, GW), _vec_spec(l, GW), wsp, wsp, wsp, _vec_spec(l, GW)],
        out_specs=[pl.BlockSpec((HEAD, GW), lambda c: (c, 0)), pl.BlockSpec((HEAD, GW), lambda c: (c, 0)),
                   mat, mat, vec, vec, vec],
        out_shape=[jax.ShapeDtypeStruct((S, GW), BF16), jax.ShapeDtypeStruct((S, GW), BF16),
                   jax.ShapeDtypeStruct((NG, HEAD, HEAD), F32), jax.ShapeDtypeStruct((NG, HEAD, HEAD), F32),
                   jax.ShapeDtypeStruct((1, GW), F32), jax.ShapeDtypeStruct((1, GW), F32),
                   jax.ShapeDtypeStruct((1, GW), F32)],
        scratch_shapes=[pltpu.VMEM((HEAD, GW), F32), pltpu.VMEM((HEAD, GW), F32), pltpu.VMEM((HEAD, GW), F32)],
        compiler_params=_params(("arbitrary",)),
    )(z, z, dmix, lng3, lnb3, ws_b, wst_b, bs_b, og3)


def _mesh_pos():
    x, y, c = lax.axis_index("x"), lax.axis_index("y"), lax.axis_index("c")
    return x, y, c


def _dev_index(p):
    return 4 * p[0] + 2 * p[1] + p[2]


def _handshake(peers):
    barrier = pltpu.get_barrier_semaphore()
    for p in peers:
        pl.semaphore_signal(barrier, inc=1, device_id=p, device_id_type=MESH)
    pl.semaphore_wait(barrier, len(peers))


GATHER_COPIES = 10


def _gather_body(full, half, in_half):
    def body(in_ref, out_ref, send_sems, recv_sems, local_sem):
        x, y, c = _mesh_pos()
        me, sib = (x, y, c), (x, y, 1 - c)
        xn, yn, dg = (1 - x, y, c), (x, 1 - y, c), (1 - x, 1 - y, c)
        _handshake([sib, xn, yn])

        def copy(k, dst, to, src=None):
            return pltpu.make_async_remote_copy(
                src_ref=dst if src is None else src, dst_ref=dst,
                send_sem=send_sems.at[k], recv_sem=recv_sems.at[k], device_id=to, device_id_type=MESH)

        def win(dev, h=None):
            idx = _dev_index(dev)
            return full(out_ref, idx) if h is None else half(out_ref, idx, h)

        local = pltpu.make_async_copy(in_ref, win(me), local_sem)
        local.start()
        sends = [copy(0, win(me), sib, src=in_ref),
                 copy(1, win(me, 0), xn, src=in_half(in_ref, 0)),
                 copy(2, win(me, 1), yn, src=in_half(in_ref, 1)),
                 copy(4, win(me, 0), yn, src=in_half(in_ref, 0)),
                 copy(6, win(me, 1), xn, src=in_half(in_ref, 1))]
        for cp in sends:
            cp.start()
        copy(1, win(xn, 0), me).wait_recv()
        sends.append(copy(3, win(xn, 0), yn))
        sends[-1].start()
        copy(2, win(yn, 1), me).wait_recv()
        sends.append(copy(5, win(yn, 1), xn))
        sends[-1].start()
        copy(6, win(xn, 1), me).wait_recv()
        sends.append(copy(7, win(xn), sib))
        sends[-1].start()
        copy(4, win(yn, 0), me).wait_recv()
        sends.append(copy(8, win(yn), sib))
        sends[-1].start()
        copy(3, win(dg, 0), me).wait_recv()
        copy(5, win(dg, 1), me).wait_recv()
        sends.append(copy(9, win(dg), sib))
        sends[-1].start()
        sib_xn, sib_yn, sib_dg = (1 - x, y, 1 - c), (x, 1 - y, 1 - c), (1 - x, 1 - y, 1 - c)
        for k, dev in [(0, sib), (7, sib_xn), (8, sib_yn), (9, sib_dg)]:
            copy(k, win(dev), me).wait_recv()
        for cp in sends:
            cp.wait_send()
        local.wait()

    return body


def _all_gather(name, cid, shard, out_shape, windows):
    return pl.kernel(
        _gather_body(*windows), out_type=out_shape,
        mesh=plsc.ScalarSubcoreMesh(axis_name="seq", num_cores=1), name=name,
        scratch_types=[pltpu.SemaphoreType.DMA((GATHER_COPIES,)), pltpu.SemaphoreType.DMA((GATHER_COPIES,)),
                       pltpu.SemaphoreType.DMA],
        compiler_params=pltpu.CompilerParams(collective_id=cid),
    )(shard)


def _d2d_body(nb, ns):
    n = nb + ns

    def body(*refs):
        ins, outs = refs[:n], refs[n:2 * n]
        send_sems, recv_sems = refs[2 * n:]
        x, y, c = _mesh_pos()
        sib = (x, y, 1 - c)
        _handshake([sib])
        copies = []
        for t in range(n):
            cp = pltpu.make_async_remote_copy(
                src_ref=ins[t].at[:, 1 - c] if t < nb else ins[t], dst_ref=outs[t],
                send_sem=send_sems.at[t], recv_sem=recv_sems.at[t],
                device_id=sib, device_id_type=MESH)
            cp.start()
            copies.append(cp)
        for cp in copies:
            cp.wait()

    return body


def _rs_d2d(name, cid, bigs, smalls):
    shapes = [jax.ShapeDtypeStruct((g.shape[0],) + g.shape[2:], g.dtype) for g in bigs]
    shapes += [jax.ShapeDtypeStruct(s.shape, s.dtype) for s in smalls]
    n = len(shapes)
    return pl.kernel(
        _d2d_body(len(bigs), len(smalls)), out_type=shapes,
        mesh=plsc.ScalarSubcoreMesh(axis_name="seq", num_cores=1), name=name,
        scratch_types=[pltpu.SemaphoreType.DMA((n,)), pltpu.SemaphoreType.DMA((n,))],
        compiler_params=pltpu.CompilerParams(collective_id=cid),
    )(*bigs, *smalls)


def _pair_sum(name, g4, recv, cvec, row_mult):
    Q, _, R, C = g4.shape
    tr = _tile(R, 1024, row_mult)

    def body(c_ref, a_ref, b_ref, o_ref):
        o_ref[...] = (a_ref[...].astype(F32) + b_ref[...].astype(F32)).astype(BF16)

    grid_spec = pltpu.PrefetchScalarGridSpec(
        num_scalar_prefetch=1, grid=(Q, R // tr),
        in_specs=[pl.BlockSpec((None, None, tr, C), lambda q, i, c_ref: (q, c_ref[0], i, 0)),
                  pl.BlockSpec((None, tr, C), lambda q, i, c_ref: (q, i, 0))],
        out_specs=pl.BlockSpec((None, tr, C), lambda q, i, c_ref: (q, i, 0)))
    return pl.pallas_call(
        body, name=name, grid_spec=grid_spec, out_shape=jax.ShapeDtypeStruct((Q, R, C), BF16),
        compiler_params=_params(("arbitrary", "arbitrary")),
    )(cvec, g4, recv)


def _add2(name, a, b):
    R, C = a.shape
    tr = _tile(R, 512, 8)

    def body(a_ref, b_ref, o_ref):
        o_ref[...] = a_ref[...] + b_ref[...]

    blk = pl.BlockSpec((tr, C), lambda i: (i, 0))
    return pl.pallas_call(body, name=name, grid=(R // tr,), in_specs=[blk, blk], out_specs=blk,
                          out_shape=jax.ShapeDtypeStruct((R, C), a.dtype),
                          compiler_params=_params(("arbitrary",)))(a, b)


def _ici_copies(nb, n, srcs, lands, send_sems, recv_sems):
    x, y, c = _mesh_pos()
    q_me = 2 * x + y
    copies = []
    for t in range(n):
        for k in range(1, 4):
            px, py = x ^ (k >> 1), y ^ (k & 1)
            copies.append(pltpu.make_async_remote_copy(
                src_ref=srcs[t].at[2 * px + py] if t < nb else srcs[t], dst_ref=lands[t].at[q_me],
                send_sem=send_sems.at[3 * t + k - 1], recv_sem=recv_sems.at[3 * t + k - 1],
                device_id=(px, py, c), device_id_type=MESH))
    return copies


_HBM = pl.BlockSpec(memory_space=pltpu.HBM)
_SEM = pl.BlockSpec(memory_space=pltpu.SEMAPHORE)
_DATAFLOW = pltpu.SideEffectType.DATAFLOW_SIDE_EFFECTING


def _ici_start(name, bigs, smalls):
    nb, n = len(bigs), len(bigs) + len(smalls)
    srcs = list(bigs) + list(smalls)
    lands = [lax.empty(g.shape, g.dtype) for g in bigs] + [lax.empty((N_DEV // 2,) + s.shape, s.dtype) for s in smalls]

    def body(*refs):
        src_refs, land_refs = refs[:n], refs[n:2 * n]
        send_sems, recv_sems = refs[2 * n], refs[2 * n + 1]
        token = refs[-1]
        for cp in _ici_copies(nb, n, src_refs, land_refs, send_sems, recv_sems):
            cp.start()
        token[...] = jnp.zeros_like(token)

    hbm = [pltpu.HBM(a.shape, a.dtype) for a in srcs + lands]
    args = [pltpu.with_memory_space_constraint(a, pltpu.HBM) for a in srcs + lands]
    out = pl.pallas_call(
        body, name=name,
        out_shape=[pltpu.SemaphoreType.DMA((3 * n,)), pltpu.SemaphoreType.DMA((3 * n,))] + hbm
                  + [jax.ShapeDtypeStruct((8, 128), F32)],
        in_specs=[_HBM] * (2 * n),
        out_specs=[_SEM, _SEM] + [_HBM] * (2 * n) + [pl.BlockSpec(memory_space=pltpu.VMEM)],
        input_output_aliases={i: 2 + i for i in range(2 * n)},
        compiler_params=pltpu.CompilerParams(has_side_effects=_DATAFLOW),
    )(*args)
    return out[0], out[1], list(out[2:2 + n]), list(out[2 + n:2 + 2 * n]), out[-1]


def _ici_wait(name, started, nb, not_before):
    send_sems, recv_sems, srcs, lands, token = started
    n = len(srcs)

    def body(*refs):
        src_refs, land_refs = refs[:n], refs[n:2 * n]
        send_refs, recv_refs = refs[2 * n], refs[2 * n + 1]
        for cp in _ici_copies(nb, n, src_refs, land_refs, send_refs, recv_refs):
            cp.wait_send()
            cp.wait_recv()

    hbm = [pltpu.HBM(a.shape, a.dtype) for a in srcs + lands]
    out = pl.pallas_call(
        body, name=name, out_shape=hbm,
        in_specs=[_HBM] * (2 * n) + [_SEM, _SEM] + [pl.BlockSpec(memory_space=pl.ANY)] * 2,
        out_specs=[_HBM] * (2 * n),
        input_output_aliases={i: i for i in range(2 * n)},
        compiler_params=pltpu.CompilerParams(has_side_effects=_DATAFLOW),
    )(*srcs, *lands, send_sems, recv_sems, not_before, token)
    return list(out[:n]), list(out[n:])


def _adamw_math(w, g, m, v):
    m2 = ADAM_B1 * m + (1.0 - ADAM_B1) * g
    v2 = ADAM_B2 * v + (1.0 - ADAM_B2) * (g * g)
    m_hat = m2 / (1.0 - ADAM_B1 ** ADAM_STEP)
    v_hat = v2 / (1.0 - ADAM_B2 ** ADAM_STEP)
    delta = -ADAM_LR * (m_hat / (jnp.sqrt(v_hat) + ADAM_EPS) + ADAM_WD * w)
    return delta, m2, v2


def _chip_terms(q, own, land_ref, n):
    return [jnp.where(q == s, own, land_ref[s]).astype(F32) for s in range(n)]


def _adamw_sum(name, land, own, qvec, w, m, v, l, prev, row_mult):
    NS, R, C = land.shape
    L = w.shape[0]
    tr = _tile(R, 256, row_mult)

    def body(*refs):
        q_ref, r_ref, o_ref, w_ref, m_ref, v_ref = refs[:6]
        g_ref, d_ref, nm_ref, nv_ref = refs[-4:]
        terms = _chip_terms(q_ref[0], o_ref[...], r_ref, NS)
        g = terms[0]
        for t in terms[1:]:
            g = g + t
        d, m2, v2 = _adamw_math(w_ref[...], g, m_ref[...], v_ref[...])
        g_ref[...] = g
        d_ref[...] = d
        nm_ref[...] = m2
        nv_ref[...] = v2

    blk = pl.BlockSpec((None, tr, C), lambda i, q: (l, i, 0))
    shp = jax.ShapeDtypeStruct((L, R, C), F32)
    in_specs = [pl.BlockSpec((NS, tr, C), lambda i, q: (0, i, 0)),
                pl.BlockSpec((None, tr, C), lambda i, q: (q[0], i, 0)), blk, blk, blk]
    args = [land, own, w, m, v]
    aliases = {}
    if prev is not None:
        in_specs += [pl.BlockSpec(memory_space=pl.ANY)] * 4
        args += list(prev)
        aliases = {6 + i: i for i in range(4)}
    grid_spec = pltpu.PrefetchScalarGridSpec(num_scalar_prefetch=1, grid=(R // tr,), in_specs=in_specs,
                                             out_specs=[blk, blk, blk, blk])
    return pl.pallas_call(
        body, name=name, grid_spec=grid_spec, out_shape=[shp, shp, shp, shp],
        input_output_aliases=aliases,
        compiler_params=_params(("arbitrary",)),
    )(qvec, *args)


def _adamw_plain(name, g, w, m, v):
    def body(g_ref, w_ref, m_ref, v_ref, d_ref, nm_ref, nv_ref):
        d, m2, v2 = _adamw_math(w_ref[...], g_ref[...], m_ref[...], v_ref[...])
        d_ref[...] = d
        nm_ref[...] = m2
        nv_ref[...] = v2

    shp = jax.ShapeDtypeStruct(g.shape, F32)
    return pl.pallas_call(body, name=name, out_shape=[shp, shp, shp], compiler_params=_params())(g, w, m, v)


SMALL = ["norm1_g", "q_norm_g", "k_norm_g", "sink", "sgu_ln_g", "sgu_ln_b", "w_s", "b_s",
         "attn_out_g", "sgu_out_g", "norm2_g", "conv_b"]
PACK_ALIGN = 1024


def _pack(pieces):
    flat = []
    for p in pieces:
        f = p.reshape(-1).astype(F32)
        pad = (-f.shape[0]) % PACK_ALIGN
        flat.append(jnp.pad(f, (0, pad)) if pad else f)
    return jnp.concatenate(flat).reshape(-1, 128)


def kernel(x, norm1_g, w_in, q_norm_g, k_norm_g, sink, sgu_ln_g, sgu_ln_b, w_s, b_s, attn_out_g, sgu_out_g, w_o, norm2_g, w_up, conv_w, conv_b, w_down, loss_target, m_norm1_g, m_w_in, m_q_norm_g, m_k_norm_g, m_sink, m_sgu_ln_g, m_sgu_ln_b, m_w_s, m_b_s, m_attn_out_g, m_sgu_out_g, m_w_o, m_norm2_g, m_w_up, m_conv_w, m_conv_b, m_w_down, v_norm1_g, v_w_in, v_q_norm_g, v_k_norm_g, v_sink, v_sgu_ln_g, v_sgu_ln_b, v_w_s, v_b_s, v_attn_out_g, v_sgu_out_g, v_w_o, v_norm2_g, v_w_up, v_conv_w, v_conv_b, v_w_down):
    weights = dict(norm1_g=norm1_g, w_in=w_in, q_norm_g=q_norm_g, k_norm_g=k_norm_g, sink=sink, sgu_ln_g=sgu_ln_g,
                   sgu_ln_b=sgu_ln_b, w_s=w_s, b_s=b_s, attn_out_g=attn_out_g, sgu_out_g=sgu_out_g, w_o=w_o,
                   norm2_g=norm2_g, w_up=w_up, conv_w=conv_w, conv_b=conv_b, w_down=w_down)
    mom_m = dict(norm1_g=m_norm1_g, w_in=m_w_in, q_norm_g=m_q_norm_g, k_norm_g=m_k_norm_g, sink=m_sink,
                 sgu_ln_g=m_sgu_ln_g, sgu_ln_b=m_sgu_ln_b, w_s=m_w_s, b_s=m_b_s, attn_out_g=m_attn_out_g,
                 sgu_out_g=m_sgu_out_g, w_o=m_w_o, norm2_g=m_norm2_g, w_up=m_w_up, conv_w=m_conv_w,
                 conv_b=m_conv_b, w_down=m_w_down)
    mom_v = dict(norm1_g=v_norm1_g, w_in=v_w_in, q_norm_g=v_q_norm_g, k_norm_g=v_k_norm_g, sink=v_sink,
                 sgu_ln_g=v_sgu_ln_g, sgu_ln_b=v_sgu_ln_b, w_s=v_w_s, b_s=v_b_s, attn_out_g=v_attn_out_g,
                 sgu_out_g=v_sgu_out_g, w_o=v_w_o, norm2_g=v_norm2_g, w_up=v_w_up, conv_w=v_conv_w,
                 conv_b=v_conv_b, w_down=v_w_down)
    order = ["norm1_g", "w_in", "q_norm_g", "k_norm_g", "sink", "sgu_ln_g", "sgu_ln_b", "w_s", "b_s",
             "attn_out_g", "sgu_out_g", "w_o", "norm2_g", "w_up", "conv_w", "conv_b", "w_down"]

    _, S, D = x.shape
    L = w_in.shape[0]
    AW = D // 2
    NQ = AW // HEAD
    NKV = max(1, NQ // 4)
    G = NQ // NKV
    KVW = NKV * HEAD
    GW = D - AW
    NG = GW // HEAD
    IN = AW + 2 * KVW + 2 * GW
    INS = w_in.shape[2]
    OS = w_o.shape[1]
    US = w_up.shape[2]
    DS = w_down.shape[1]
    F2 = US * N_DEV
    F = F2 // 2
    assert INS * N_DEV == IN and OS * N_DEV == D and DS * N_DEV == F and AW == GW and (3 * AW) % (2 * KVW) == 0
    cfg = dict(S=S, D=D, AW=AW, NQ=NQ, NKV=NKV, G=G, KVW=KVW, GW=GW, NG=NG, F=F, F2=F2)

    def row_windows(n):
        m = n // 2
        return (lambda ref, idx: ref.at[:, pl.ds(pl.multiple_of(idx * n, n), n), :],
                lambda ref, idx, h: ref.at[:, pl.ds(pl.multiple_of(idx * n + h * m, m), m), :],
                lambda ref, h: ref.at[:, pl.ds(h * m, m), :])

    def col_windows(n, rows):
        m = rows // 2
        return (lambda ref, idx: ref.at[:, :, pl.ds(pl.multiple_of(idx * n, n), n)],
                lambda ref, idx, h: ref.at[:, pl.ds(h * m, m), pl.ds(pl.multiple_of(idx * n, n), n)],
                lambda ref, h: ref.at[:, pl.ds(h * m, m), :])

    def lead_col_windows(n, lead):
        m = lead // 2
        return (lambda ref, idx: ref.at[:, :, pl.ds(pl.multiple_of(idx * n, n), n)],
                lambda ref, idx, h: ref.at[pl.ds(h * m, m), :, pl.ds(pl.multiple_of(idx * n, n), n)],
                lambda ref, h: ref.at[pl.ds(h * m, m)])

    def after(first, then):
        return lax.optimization_barrier((first, then))

    w_in_t, m_w_in_t, v_w_in_t = (jnp.swapaxes(a, 1, 2) for a in (w_in, m_w_in, v_w_in))

    ids = iter(range(64))
    w_in_f, w_o_f, w_up_f, w_down_f = [], [], [], []

    def gather_layer(l, not_before=None):
        in_shard = w_in_t[l:l + 1].astype(BF16)
        if not_before is not None:
            _, in_shard = after(not_before, in_shard)
        w_in_f.append(_all_gather("ag_in", next(ids), in_shard,
                                  jax.ShapeDtypeStruct((1, IN, D), BF16), row_windows(INS)))
        w_o_f.append(_all_gather("ag_o", next(ids), w_o[l:l + 1].astype(BF16),
                                 jax.ShapeDtypeStruct((1, D, D), BF16), row_windows(OS)))
        w_up_f.append(_all_gather("ag_up", next(ids), w_up[l:l + 1].astype(BF16),
                                  jax.ShapeDtypeStruct((1, D, F2), BF16), col_windows(US, D)))
        w_down_f.append(_all_gather("ag_down", next(ids), w_down[l:l + 1].astype(BF16),
                                    jax.ShapeDtypeStruct((1, F, D), BF16), row_windows(DS)))

    assert L % 2 == 0 and INS % 32 == 0 and OS % 32 == 0 and DS % 32 == 0 and D % 32 == 0
    gather_layer(0)
    conv_w_f = _all_gather("ag_conv_w", next(ids), conv_w, jax.ShapeDtypeStruct((L, 3, F2), F32),
                           lead_col_windows(US, L))

    n1g3, n2g3 = norm1_g.reshape(L, 1, D), norm2_g.reshape(L, 1, D)
    qg3, kg3 = q_norm_g.reshape(L, 1, HEAD), k_norm_g.reshape(L, 1, HEAD)
    lng3, lnb3 = sgu_ln_g.reshape(L, 1, GW), sgu_ln_b.reshape(L, 1, GW)
    ag3, og3 = attn_out_g.reshape(L, 1, AW), sgu_out_g.reshape(L, 1, GW)
    cb3 = conv_b.reshape(L, 1, F2)
    ws_b = w_s.astype(BF16)
    wst_b = jnp.swapaxes(w_s, 2, 3).astype(BF16)
    bs_b = jnp.broadcast_to(b_s[..., None], (L, NG, HEAD, HEAD))
    inv_freq = ROPE_THETA ** (-jnp.arange(0, HEAD, 2, dtype=F32) / HEAD)
    ang = jnp.arange(S, dtype=F32)[:, None] * inv_freq[None, :]
    cos2 = jnp.concatenate([jnp.cos(ang), jnp.cos(ang)], axis=1)
    sin2 = jnp.concatenate([-jnp.sin(ang), jnp.sin(ang)], axis=1)

    tn = 512
    t_in, t_d, t_f, t_f2 = _tile(IN, tn, 128), _tile(D, tn, 128), _tile(F, tn, 128), _tile(F2, tn, 128)
    tm_f = _tile(F, 512, 128)
    tm_s = _tile(S, 1024, 128)
    tm_g = _tile(D, 512, 128)
    assert AW % t_in == 0 and (2 * KVW) % t_in == 0
    n_q, n_kv, n_g = AW // t_in, (2 * KVW) // t_in, (2 * GW) // t_in

    def in_tile(j):
        return jnp.where(j < n_q, j, jnp.where(j < n_q + n_g, j + n_kv, j - n_g))

    def w_spec(tk, tn_):
        return pl.BlockSpec((None, tk, tn_), lambda i, j, k: (0, k, j))

    def wt_spec(tn_, tk):
        return pl.BlockSpec((None, tn_, tk), lambda i, j, k: (0, j, k))

    def a_spec(tm, tk):
        return pl.BlockSpec((tm, tk), lambda i, j, k: (i, k))

    def at_spec(tk, tm):
        return pl.BlockSpec((tk, tm), lambda i, j, k: (k, i))

    def b_spec(tk, tn_):
        return pl.BlockSpec((tk, tn_), lambda i, j, k: (k, j))

    xs = x.reshape(S, D)
    saved = []
    cur = xs
    for l in range(L):
        h = _rms_fwd("rms1_fwd", cur, n1g3, l)
        z = _mm("mm_in", h, w_in_f[l], M=S, N=IN, K=D, tm=S, tn=t_in, tk=D, tb=True,
                a_spec=a_spec(S, D),
                b_spec=pl.BlockSpec((None, t_in, D), lambda i, j, k: (0, in_tile(j), 0)))
        q_r, k_r, v_b = _qkv_prep("qkv_prep", z, qg3, kg3, cos2, sin2, l, cfg)
        attn, mix_l = _attn_fwd("attn_fwd", q_r, k_r, v_b, sink, ag3, l, cfg)
        mix_r = _sgu_fwd("sgu_fwd", z, lng3, lnb3, ws_b, bs_b, og3, l, cfg)
        mixed = jnp.concatenate([mix_l, mix_r], axis=1)
        x1 = _mm("mm_o", mixed, w_o_f[l], M=S, N=D, K=D, tm=S, tn=t_d, tk=D,
                 a_spec=a_spec(S, D), b_spec=w_spec(D, t_d), res=cur)
        if l + 1 < L:
            gather_layer(l + 1, not_before=x1)
        h2 = _rms_fwd("rms2_fwd", x1, n2g3, l)
        ap = _mm("mm_up", h2, w_up_f[l], M=S, N=F2, K=D, tm=S, tn=t_f2, tk=D,
                 a_spec=a_spec(S, D), b_spec=w_spec(D, t_f2))
        y_b = _conv_glu_fwd("conv_glu_fwd", ap, conv_w_f, cb3, l, cfg)
        x2 = _mm("mm_down", y_b, w_down_f[l], M=S, N=D, K=F, tm=tm_s, tn=t_d, tk=F,
                 a_spec=a_spec(tm_s, F), b_spec=w_spec(F, t_d), res=x1)
        saved.append(dict(x=cur, h=h, z=z, q=q_r, k=k_r, v=v_b, attn=attn, mixed=mixed, x1=x1, h2=h2, ap=ap, y=y_b))
        cur = x2

    loss_tile, dx, dxb = _loss_bwd("loss", cur, loss_target.reshape(S, D))

    gS = [dict() for _ in range(L)]
    recv = [dict() for _ in range(L)]
    cvec = jnp.reshape(lax.axis_index("c"), (1,)).astype(jnp.int32)

    def rs_swap(name, g, R, C, small=None):
        g4 = g.reshape(N_DEV // 2, 2, R, C)
        got = _rs_d2d("rs_d2d_" + name, next(ids), [g4], [] if small is None else [small])
        return name, g4, got, small

    started = []

    def rs_finish(l, pending, then=None):
        name, g4, got, small = pending
        chip = _pair_sum("pair_sum_" + name, g4, got[0], cvec, 16)
        chip_small = [] if small is None else [_add2("pair_sum_small", small, got[1])]
        begun = _ici_start("rs_ici_" + name, [chip], chip_small)
        if then is not None:
            token, then = after(begun[4], then)
            begun = begun[:4] + (token,)
        started.append((l, name, begun))
        return then

    carried = None
    for l in reversed(range(L)):
        sv = saved[l]
        dy = _mm("mm_dy", dxb, w_down_f[l], M=S, N=F, K=D, tm=S, tn=t_f, tk=D, tb=True,
                 a_spec=a_spec(S, D), b_spec=wt_spec(t_f, D))
        if carried is not None:
            dy = rs_finish(l + 1, carried, dy)
        g_down = _mm("mm_gdown", sv["y"], dxb, M=F, N=D, K=S, tm=tm_f, tn=D, tk=S, ta=True,
                     a_spec=at_spec(S, tm_f), b_spec=b_spec(S, D), out_dtype=BF16)
        g_down, dy = after(g_down, dy)
        swap_down = rs_swap("w_down", g_down, DS, D)
        dap3, dcw, dcb = _glu_conv_bwd("glu_conv_bwd", dy, sv["ap"], conv_w_f, cb3, l, cfg)
        gS[l]["conv_w"] = jnp.concatenate([dcw[0], dcw[1]], axis=1)
        gS[l]["conv_b"] = jnp.concatenate([dcb[0], dcb[1]], axis=1).reshape(F2)
        dap3 = rs_finish(l, swap_down, dap3)
        dh2 = _mm("mm_dh2", dap3, w_up_f[l], M=S, N=D, K=F2, tm=tm_s, tn=t_d, tk=F, tb=True,
                  a_spec=pl.BlockSpec((None, tm_s, F), lambda i, j, k: (k, i, 0)),
                  b_spec=wt_spec(t_d, F))
        g_up = _mm("mm_gup", sv["h2"], dap3, M=D, N=F2, K=S, tm=tm_g, tn=2 * US, tk=S, ta=True,
                   a_spec=at_spec(S, tm_g),
                   b_spec=pl.BlockSpec((None, S, 2 * US), lambda i, j, k: (j // 2, 0, j % 2)),
                   out_dtype=BF16, out_shape=(N_DEV, D, US),
                   out_spec=pl.BlockSpec((2, tm_g, US), lambda i, j, k: (j, i, 0)))
        g_up, dh2 = after(g_up, dh2)
        swap_up = rs_swap("w_up", g_up, D, US)
        dx1, dx1b, dg2 = _rms_bwd("rms2_bwd", sv["x1"], dh2, dx, n2g3, l)
        gS[l]["norm2_g"] = dg2.reshape(D)
        dmix = _mm("mm_dmix", dx1b, w_o_f[l], M=S, N=D, K=D, tm=S, tn=t_d, tk=D, tb=True,
                   a_spec=a_spec(S, D), b_spec=wt_spec(t_d, D))
        g_o = _mm("mm_go", sv["mixed"], dx1b, M=D, N=D, K=S, tm=D, tn=t_d, tk=S, ta=True,
                  a_spec=at_spec(S, D), b_spec=b_spec(S, t_d), out_dtype=BF16)
        g_o, dmix = after(g_o, dmix)
        swap_o = rs_swap("w_o", g_o, OS, D)
        dmix = rs_finish(l, swap_up, dmix)
        dq_r, dk_pad, dv_pad, dsink, dag = _attn_bwd("attn_bwd", sv["q"], sv["k"], sv["v"], sv["attn"], dmix,
                                                     sink, ag3, l, cfg)
        dq_r = rs_finish(l, swap_o, dq_r)
        gS[l]["sink"] = dsink[:, 0]
        gS[l]["attn_out_g"] = dag.reshape(AW)
        dzgu, dzgv, dws, dbs, dlng, dlnb, dog = _sgu_bwd("sgu_bwd", sv["z"], dmix, lng3, lnb3, ws_b, wst_b, bs_b,
                                                        og3, l, cfg)
        gS[l]["w_s"] = dws
        gS[l]["b_s"] = dbs[:, :, 0]
        gS[l]["sgu_ln_g"] = dlng.reshape(GW)
        gS[l]["sgu_ln_b"] = dlnb.reshape(GW)
        gS[l]["sgu_out_g"] = dog.reshape(GW)
        dzq, dzkv, dqg, dkg = _qkv_prep_bwd("qkv_prep_bwd", sv["z"], dq_r, dk_pad, dv_pad, qg3, kg3, cos2, sin2,
                                            l, cfg)
        gS[l]["q_norm_g"] = dqg.reshape(HEAD)
        gS[l]["k_norm_g"] = dkg.reshape(HEAD)
        dz = jnp.concatenate([dzq, dzkv, dzgu, dzgv], axis=1)
        dh = _mm("mm_dh", dz, w_in_f[l], M=S, N=D, K=IN, tm=tm_s, tn=t_d, tk=IN,
                 a_spec=a_spec(tm_s, IN), b_spec=w_spec(IN, t_d))
        g_in = _mm("mm_gin", dz, sv["h"], M=IN, N=D, K=S, tm=t_in, tn=D, tk=S, ta=True,
                   a_spec=at_spec(S, t_in), b_spec=b_spec(S, D), out_dtype=BF16)
        g_in, dh = after(g_in, dh)
        dx, dxb, dg1 = _rms_bwd("rms1_bwd", sv["x"], dh, dx1, n1g3, l)
        gS[l]["norm1_g"] = dg1.reshape(D)
        carried = rs_swap("w_in", g_in, INS, D, small=_pack([gS[l][n] for n in SMALL] + [gS[l]["conv_w"]]))
    dx = rs_finish(0, carried, dx)
    grad_x = dx.reshape(1, S, D)

    grads, deltas, new_m, new_v = {}, {}, {}, {}
    results, placed = {}, dx
    qvec = jnp.reshape(2 * lax.axis_index("x") + lax.axis_index("y"), (1,)).astype(jnp.int32)
    for l, name, begun in started:
        sent, landed = _ici_wait("rs_wait_" + name, begun, 1, placed)
        recv[l][name] = (sent, landed)
        wmv = (w_in_t, m_w_in_t, v_w_in_t) if name == "w_in" else (weights[name], mom_m[name], mom_v[name])
        results[name] = _adamw_sum("adamw_" + name, landed[0], sent[0], qvec, *wmv, l, results.get(name), 16)
        placed = results[name][0]
    _, loss_local = after(placed, loss_tile[0, 0])
    loss = lax.psum(loss_local, ("x", "y", "c"))
    results["w_in"] = [jnp.swapaxes(r, 1, 2) for r in results["w_in"]]
    for name, res in results.items():
        grads[name], deltas[name], new_m[name], new_v[name] = res

    def as_rows(a):
        flat = a.reshape(L, -1)
        short = (-flat.shape[1]) % 128
        if short:
            flat = jnp.pad(flat, ((0, 0), (0, short)))
        return flat.reshape(L, -1, 128)

    sizes = [weights[n][0].size for n in SMALL] + [3 * F2]
    offsets, off = [], 0
    for sz in sizes:
        offsets.append(off // 128)
        off += sz + (-sz) % PACK_ALIGN
    n_small = len(SMALL)
    rows_of = [-(-sz // 128) for sz in sizes]

    def small_body(*refs):
        q_ref = refs[0]
        lands, owns = refs[1:1 + L], refs[1 + L:1 + 2 * L]
        wmv_refs = refs[1 + 2 * L:1 + 2 * L + 3 * n_small]
        outs = refs[1 + 2 * L + 3 * n_small:]
        for l in range(L):
            for p in range(n_small + 1):
                rows = slice(offsets[p], offsets[p] + rows_of[p])
                terms = [jnp.where(q_ref[0] == s, owns[l][rows, :], lands[l][s, rows, :]) for s in range(N_DEV // 2)]
                g = terms[0]
                for t in terms[1:]:
                    g = g + t
                if p == n_small:
                    outs[4 * n_small][l] = g
                    continue
                d, m2, v2 = _adamw_math(wmv_refs[3 * p][l], g, wmv_refs[3 * p + 1][l], wmv_refs[3 * p + 2][l])
                for t, val in enumerate([g, d, m2, v2]):
                    outs[4 * p + t][l] = val

    vmem = pl.BlockSpec(memory_space=pltpu.VMEM)
    wmv_args = []
    for n in SMALL:
        wmv_args += [as_rows(weights[n]), as_rows(mom_m[n]), as_rows(mom_v[n])]
    small_out = pl.pallas_call(
        small_body, name="adamw_small",
        in_specs=[pl.BlockSpec(memory_space=pltpu.SMEM)] + [vmem] * (2 * L + 3 * n_small),
        out_specs=[vmem] * (4 * n_small + 1),
        out_shape=[jax.ShapeDtypeStruct((L, rows_of[p], 128), F32) for p in range(n_small) for _ in range(4)]
                  + [jax.ShapeDtypeStruct((L, rows_of[n_small], 128), F32)],
        compiler_params=_params(),
    )(qvec, *[recv[l]["w_in"][1][1] for l in range(L)], *[recv[l]["w_in"][0][1] for l in range(L)], *wmv_args)
    for p, n in enumerate(SMALL):
        for t, store in enumerate([grads, deltas, new_m, new_v]):
            store[n] = small_out[4 * p + t].reshape(L, -1)[:, :sizes[p]].reshape(weights[n].shape)
    conv_full = small_out[4 * n_small].reshape(L, 3, F2)
    me_i = _dev_index(_mesh_pos())
    g_cw = lax.dynamic_slice_in_dim(conv_full, me_i * US, US, axis=2)
    grads["conv_w"] = g_cw
    d_cw, m_cw, v_cw = _adamw_plain("adamw_conv_w", g_cw.reshape(L * 3, US), conv_w.reshape(L * 3, US),
                                    m_conv_w.reshape(L * 3, US), v_conv_w.reshape(L * 3, US))
    deltas["conv_w"] = d_cw.reshape(L, 3, US)
    new_m["conv_w"] = m_cw.reshape(L, 3, US)
    new_v["conv_w"] = v_cw.reshape(L, 3, US)

    return (loss, grad_x, *[grads[n] for n in order], *[deltas[n] for n in order],
            *[new_m[n] for n in order], *[new_v[n] for n in order])
```

```python
import jax
import jax.numpy as jnp
from jax import lax
from jax.experimental import pallas as pl
from jax.experimental.pallas import tpu as pltpu
from jax.experimental.pallas import tpu_sc as plsc

F32 = jnp.float32
BF16 = jnp.bfloat16
MESH = pl.DeviceIdType.MESH

N_DEV = 8
HEAD = 128
EPS = 1e-6
MASK_VALUE = -1e30
ROPE_THETA = 10000.0
GELU_C = 0.7978845608028654
GELU_A = 0.044715

ADAM_LR = 0.001
ADAM_B1 = 0.9
ADAM_B2 = 0.999
ADAM_EPS = 1e-08
ADAM_WD = 0.01
ADAM_STEP = 10

VMEM_LIMIT = 56 * 1024 * 1024


def _tile(n, pref, mult):
    best = None
    for t in range(mult, min(n, pref) + 1, mult):
        if n % t == 0:
            best = t
    return n if best is None else best


def _params(sem=None):
    kw = dict(vmem_limit_bytes=VMEM_LIMIT)
    if sem is not None:
        kw["dimension_semantics"] = sem
    return pltpu.CompilerParams(**kw)


def _gelu(x):
    return x * (0.5 * (1.0 + jnp.tanh(GELU_C * (x + GELU_A * (x * x * x)))))


def _gelu_grad(x):
    t = jnp.tanh(GELU_C * (x + GELU_A * (x * x * x)))
    return 0.5 * (1.0 + t) + 0.5 * x * (1.0 - t * t) * (GELU_C * (1.0 + 3.0 * GELU_A * (x * x)))


def _sigmoid(x):
    return 1.0 / (1.0 + jnp.exp(-x))


def _vec_spec(l, n):
    return pl.BlockSpec((None, 1, n), lambda *_: (l, 0, 0))


def _mm(name, a, b, *, M, N, K, tm, tn, tk, a_spec, b_spec, ta=False, tb=False, out_dtype=F32, res=None,
        out_shape=None, out_spec=None):
    nm, nn, nk = M // tm, N // tn, K // tk
    assert nm * tm == M and nn * tn == N and nk * tk == K
    assert not (ta and nk > 1)
    dims = (((1,), (1,)), ((), ())) if tb else (((1,), (0,)), ((), ()))

    def body(*refs):
        refs = list(refs)
        a_ref = refs.pop(0)
        b_ref = refs.pop(0)
        r_ref = refs.pop(0) if res is not None else None
        o_ref = refs.pop(0)
        acc = refs.pop(0) if nk > 1 else None
        at = refs.pop(0) if ta else None
        k = pl.program_id(2)
        if ta:
            @pl.when(pl.program_id(1) == 0)
            def _():
                at[...] = a_ref[...].T
            lhs = at[...]
        else:
            lhs = a_ref[...]
        p = lax.dot_general(lhs, b_ref[...], dims, preferred_element_type=F32)

        def finish(r):
            if r_ref is not None:
                r = r_ref[...] + r
            if len(o_ref.shape) == 3:
                w = o_ref.shape[2]
                for s in range(o_ref.shape[0]):
                    o_ref[s] = r[:, s * w:(s + 1) * w].astype(out_dtype)
            else:
                o_ref[...] = r.astype(out_dtype)

        if nk == 1:
            finish(p)
        else:
            @pl.when(k == 0)
            def _():
                acc[...] = p

            @pl.when(k > 0)
            def _():
                acc[...] += p

            @pl.when(k == nk - 1)
            def _():
                finish(acc[...])

    in_specs = [a_spec, b_spec]
    args = [a, b]
    if res is not None:
        in_specs.append(pl.BlockSpec((tm, tn), lambda i, j, k: (i, j)))
        args.append(res)
    scratch = []
    if nk > 1:
        scratch.append(pltpu.VMEM((tm, tn), F32))
    if ta:
        scratch.append(pltpu.VMEM((tm, tk), BF16))
    return pl.pallas_call(
        body, name=name, grid=(nm, nn, nk),
        in_specs=in_specs,
        out_specs=pl.BlockSpec((tm, tn), lambda i, j, k: (i, j)) if out_spec is None else out_spec,
        out_shape=jax.ShapeDtypeStruct((M, N) if out_shape is None else out_shape, out_dtype),
        scratch_shapes=scratch,
        compiler_params=_params(("arbitrary", "arbitrary", "arbitrary")),
    )(*args)


def _rms_fwd(name, x, g3, l):
    S, D = x.shape
    tr = _tile(S, 512, 16)

    def body(x_ref, g_ref, h_ref):
        xv = x_ref[...]
        r = lax.rsqrt(jnp.mean(xv * xv, axis=-1, keepdims=True) + EPS)
        h_ref[...] = ((xv * r) * g_ref[...]).astype(BF16)

    return pl.pallas_call(
        body, name=name, grid=(S // tr,),
        in_specs=[pl.BlockSpec((tr, D), lambda i: (i, 0)), _vec_spec(l, D)],
        out_specs=pl.BlockSpec((tr, D), lambda i: (i, 0)),
        out_shape=jax.ShapeDtypeStruct((S, D), BF16),
        compiler_params=_params(("arbitrary",)),
    )(x, g3)


def _rope(t, cos2, sin2):
    return t * cos2 + pltpu.roll(t, HEAD // 2, axis=1) * sin2


def _qkv_prep(name, z, qg3, kg3, cos2, sin2, l, cfg):
    S, AW, KVW, NQ, NKV = cfg["S"], cfg["AW"], cfg["KVW"], cfg["NQ"], cfg["NKV"]
    tr = _tile(S, 512, 16)
    kv_blk = (3 * AW) // (2 * KVW)

    def body(zq_ref, zkv_ref, qg_ref, kg_ref, c_ref, s_ref, q_ref, k_ref, v_ref):
        cosv, sinv = c_ref[...], s_ref[...]

        def norm_rope(t, g):
            r = lax.rsqrt(jnp.mean(t * t, axis=-1, keepdims=True) + EPS)
            return _rope((t * r) * g, cosv, sinv)

        for h in range(NQ):
            sl = slice(h * HEAD, (h + 1) * HEAD)
            q_ref[:, sl] = norm_rope(zq_ref[:, sl], qg_ref[...]).astype(BF16)
        for h in range(NKV):
            sl = slice(h * HEAD, (h + 1) * HEAD)
            k_ref[:, sl] = norm_rope(zkv_ref[:, sl], kg_ref[...]).astype(BF16)
        v_ref[...] = zkv_ref[:, KVW:].astype(BF16)

    return pl.pallas_call(
        body, name=name, grid=(S // tr,),
        in_specs=[pl.BlockSpec((tr, AW), lambda i: (i, 0)),
                  pl.BlockSpec((tr, 2 * KVW), lambda i: (i, kv_blk)),
                  _vec_spec(l, HEAD), _vec_spec(l, HEAD),
                  pl.BlockSpec((tr, HEAD), lambda i: (i, 0)),
                  pl.BlockSpec((tr, HEAD), lambda i: (i, 0))],
        out_specs=[pl.BlockSpec((tr, AW), lambda i: (i, 0)),
                   pl.BlockSpec((tr, KVW), lambda i: (i, 0)),
                   pl.BlockSpec((tr, KVW), lambda i: (i, 0))],
        out_shape=[jax.ShapeDtypeStruct((S, AW), BF16),
                   jax.ShapeDtypeStruct((S, KVW), BF16),
                   jax.ShapeDtypeStruct((S, KVW), BF16)],
        compiler_params=_params(("arbitrary",)),
    )(z, z, qg3, kg3, cos2, sin2)


def _band_specs(width, nb):
    return [pl.BlockSpec((HEAD, width), lambda n: (jnp.maximum(n - 1, 0), 0)),
            pl.BlockSpec((HEAD, width), lambda n: (n, 0)),
            pl.BlockSpec((HEAD, width), lambda n: (jnp.minimum(n + 1, nb - 1), 0))]


def _attn_probs(qs, kj, n, sink_of_row, S, G):
    s = lax.dot_general(qs, kj, (((1,), (1,)), ((), ())), preferred_element_type=F32) * (HEAD ** -0.5)
    rows = lax.broadcasted_iota(jnp.int32, (G * HEAD, 3 * HEAD), 0)
    cols = lax.broadcasted_iota(jnp.int32, (G * HEAD, 3 * HEAD), 1)
    qi = rows & (HEAD - 1)
    kpos = n * HEAD - HEAD + cols
    valid = (cols >= qi) & (cols <= qi + 2 * HEAD) & (kpos >= 0) & (kpos < S)
    s = jnp.where(valid, s, MASK_VALUE)
    m = jnp.maximum(jnp.max(s, axis=-1, keepdims=True), sink_of_row)
    p = jnp.exp(s - m)
    e_sink = jnp.exp(sink_of_row - m)
    inv = 1.0 / (jnp.sum(p, axis=-1, keepdims=True) + e_sink)
    return p * inv, e_sink * inv


def _sink_rows(sink_ref, l, j, G):
    hidx = lax.broadcasted_iota(jnp.int32, (G * HEAD, 1), 0) // HEAD
    col = jnp.full((G * HEAD, 1), sink_ref[l, j * G], F32)
    for g in range(1, G):
        col = jnp.where(hidx == g, sink_ref[l, j * G + g], col)
    return col


def _attn_fwd(name, q, k, v, sink, ag3, l, cfg):
    S, AW, KVW, NKV, G = cfg["S"], cfg["AW"], cfg["KVW"], cfg["NKV"], cfg["G"]
    nb = S // HEAD

    def body(q_ref, kp, kc, kn, vp, vc, vn, sink_ref, ag_ref, a_ref, mix_ref):
        n = pl.program_id(0)
        kb = jnp.concatenate([kp[...], kc[...], kn[...]], axis=0)
        vb = jnp.concatenate([vp[...], vc[...], vn[...]], axis=0)
        for j in range(NKV):
            sl = slice(j * HEAD, (j + 1) * HEAD)
            qs = jnp.concatenate([q_ref[:, (j * G + g) * HEAD:(j * G + g + 1) * HEAD] for g in range(G)], axis=0)
            probs, _ = _attn_probs(qs, kb[:, sl], n, _sink_rows(sink_ref, l, j, G), S, G)
            o = jnp.dot(probs.astype(BF16), vb[:, sl], preferred_element_type=F32)
            for g in range(G):
                a_ref[:, (j * G + g) * HEAD:(j * G + g + 1) * HEAD] = o[g * HEAD:(g + 1) * HEAD]
        a = a_ref[...]
        r = lax.rsqrt(jnp.mean(a * a, axis=-1, keepdims=True) + EPS)
        mix_ref[...] = ((a * r) * ag_ref[...]).astype(BF16)

    return pl.pallas_call(
        body, name=name, grid=(nb,),
        in_specs=[pl.BlockSpec((HEAD, AW), lambda n: (n, 0))] + _band_specs(KVW, nb) + _band_specs(KVW, nb)
                 + [pl.BlockSpec(memory_space=pltpu.SMEM), _vec_spec(l, AW)],
        out_specs=[pl.BlockSpec((HEAD, AW), lambda n: (n, 0)), pl.BlockSpec((HEAD, AW), lambda n: (n, 0))],
        out_shape=[jax.ShapeDtypeStruct((S, AW), F32), jax.ShapeDtypeStruct((S, AW), BF16)],
        compiler_params=_params(("arbitrary",)),
    )(q, k, k, k, v, v, v, sink, ag3)


def _sgu_forward_math(gu, gv, lng, lnb):
    u = _gelu(gu)
    vv = _gelu(gv)
    mu = jnp.mean(vv, axis=-1, keepdims=True)
    xc = vv - mu
    rstd = lax.rsqrt(jnp.mean(xc * xc, axis=-1, keepdims=True) + EPS)
    xhat = xc * rstd
    vn = xhat * lng + lnb
    return u, xhat, rstd, vn


def _sgu_fwd(name, z, lng3, lnb3, ws_b, bs_b, og3, l, cfg):
    S, GW, NG = cfg["S"], cfg["GW"], cfg["NG"]

    def body(gu_ref, gv_ref, lng_ref, lnb_ref, ws_ref, bs_ref, og_ref, mix_ref, sg_ref):
        u, _, _, vn = _sgu_forward_math(gu_ref[...], gv_ref[...], lng_ref[...], lnb_ref[...])
        vnb = vn.astype(BF16)
        for h in range(NG):
            sl = slice(h * HEAD, (h + 1) * HEAD)
            f = jnp.dot(ws_ref[h], vnb[:, sl], preferred_element_type=F32) + bs_ref[h]
            sg_ref[:, sl] = u[:, sl] * f
        sg = sg_ref[...]
        r = lax.rsqrt(jnp.mean(sg * sg, axis=-1, keepdims=True) + EPS)
        mix_ref[...] = ((sg * r) * og_ref[...]).astype(BF16)

    return pl.pallas_call(
        body, name=name, grid=(S // HEAD,),
        in_specs=[pl.BlockSpec((HEAD, GW), lambda c: (c, 1)),
                  pl.BlockSpec((HEAD, GW), lambda c: (c, 2)),
                  _vec_spec(l, GW), _vec_spec(l, GW),
                  pl.BlockSpec((None, NG, HEAD, HEAD), lambda c: (l, 0, 0, 0)),
                  pl.BlockSpec((None, NG, HEAD, HEAD), lambda c: (l, 0, 0, 0)),
                  _vec_spec(l, GW)],
        out_specs=pl.BlockSpec((HEAD, GW), lambda c: (c, 0)),
        out_shape=jax.ShapeDtypeStruct((S, GW), BF16),
        scratch_shapes=[pltpu.VMEM((HEAD, GW), F32)],
        compiler_params=_params(("arbitrary",)),
    )(z, z, lng3, lnb3, ws_b, bs_b, og3)


CONV_HALO = 8
CONV_ROWS = 128
CONV_COLS = 512


def _for_row_windows(S, fn):
    R, W = CONV_ROWS, CONV_ROWS + 2 * CONV_HALO
    n = S // R
    assert n * R == S and n >= 2
    fn(0, 0, 0, "top")
    if n > 2:
        def mid(k, carry):
            fn(pl.multiple_of(k * R - CONV_HALO, CONV_HALO), CONV_HALO, pl.multiple_of(k * R, R), "mid")
            return carry

        lax.fori_loop(1, n - 1, mid, 0, unroll=2 if (n - 2) % 2 == 0 else 1)
    fn(S - W, 2 * CONV_HALO, S - R, "bottom")


def _shifts(t, edge):
    W = t.shape[0]
    dn, up = pltpu.roll(t, 1, axis=0), pltpu.roll(t, W - 1, axis=0)
    if edge == "top":
        dn = jnp.where(lax.broadcasted_iota(jnp.int32, t.shape, 0) == 0, 0.0, dn)
    if edge == "bottom":
        up = jnp.where(lax.broadcasted_iota(jnp.int32, t.shape, 0) == W - 1, 0.0, up)
    return dn, up


def _conv3(t, w, b, edge):
    dn, up = _shifts(t, edge)
    return ((b + dn * w[0:1]) + t * w[1:2]) + up * w[2:3]


def _conv_glu_fwd(name, ap, cw, cb3, l, cfg):
    S, F = cfg["S"], cfg["F"]
    tc = _tile(F, CONV_COLS, 128)
    nf = F // tc
    R, W = CONV_ROWS, CONV_ROWS + 2 * CONV_HALO

    def body(g_ref, u_ref, wg_ref, wu_ref, bg_ref, bu_ref, y_ref):
        def window(start, lo, out0, edge):
            rows = pl.ds(start, W)
            for c0 in range(0, tc, 128):
                cols = slice(c0, c0 + 128)
                ag = _conv3(g_ref[rows, cols], wg_ref[:, cols], bg_ref[:, cols], edge)
                au = _conv3(u_ref[rows, cols], wu_ref[:, cols], bu_ref[:, cols], edge)
                y = (ag * _sigmoid(ag)) * au
                y_ref[pl.ds(out0, R), cols] = y[lo:lo + R].astype(BF16)

        _for_row_windows(S, window)

    return pl.pallas_call(
        body, name=name, grid=(nf,),
        in_specs=[pl.BlockSpec((S, tc), lambda j: (0, j)),
                  pl.BlockSpec((S, tc), lambda j: (0, j + nf)),
                  pl.BlockSpec((None, 3, tc), lambda j: (l, 0, j)),
                  pl.BlockSpec((None, 3, tc), lambda j: (l, 0, j + nf)),
                  pl.BlockSpec((None, 1, tc), lambda j: (l, 0, j)),
                  pl.BlockSpec((None, 1, tc), lambda j: (l, 0, j + nf))],
        out_specs=pl.BlockSpec((S, tc), lambda j: (0, j)),
        out_shape=jax.ShapeDtypeStruct((S, F), BF16),
        compiler_params=_params(("arbitrary",)),
    )(ap, ap, cw, cw, cb3, cb3)


def _mm_loss(name, a, b, res, target, *, M, N, K, tm, tn, b_spec):
    nm, nn = M // tm, N // tn

    def body(a_ref, b_ref, r_ref, t_ref, loss_ref, d_ref, db_ref):
        @pl.when((pl.program_id(0) == 0) & (pl.program_id(1) == 0))
        def _():
            loss_ref[...] = jnp.zeros_like(loss_ref)

        y = r_ref[...] + jnp.dot(a_ref[...], b_ref[...], preferred_element_type=F32)
        err = y - t_ref[...]
        part = 0.5 * jnp.sum(jnp.sum(err * err, axis=-1, keepdims=True) * (1.0 / N), axis=0, keepdims=True)
        loss_ref[...] += jnp.broadcast_to(part, loss_ref.shape)
        d = err * (1.0 / N)
        d_ref[...] = d
        db_ref[...] = d.astype(BF16)

    tile = pl.BlockSpec((tm, tn), lambda i, j: (i, j))
    return pl.pallas_call(
        body, name=name, grid=(nm, nn),
        in_specs=[pl.BlockSpec((tm, K), lambda i, j: (i, 0)), b_spec, tile, tile],
        out_specs=[pl.BlockSpec((8, 128), lambda i, j: (0, 0)), tile, tile],
        out_shape=[jax.ShapeDtypeStruct((8, 128), F32), jax.ShapeDtypeStruct((M, N), F32),
                   jax.ShapeDtypeStruct((M, N), BF16)],
        compiler_params=_params(("arbitrary", "arbitrary")),
    )(a, b, res, target)


def _rms_bwd(name, x, dh, dres, g3, l):
    S, D = x.shape
    tr = _tile(S, 512, 16)

    def body(x_ref, dh_ref, dr_ref, g_ref, dx_ref, dxb_ref, dg_ref):
        @pl.when(pl.program_id(0) == 0)
        def _():
            dg_ref[...] = jnp.zeros_like(dg_ref)

        xv = x_ref[...]
        dhv = dh_ref[...]
        r = lax.rsqrt(jnp.mean(xv * xv, axis=-1, keepdims=True) + EPS)
        xhat = xv * r
        dhg = dhv * g_ref[...]
        dx = dr_ref[...] + r * (dhg - xhat * jnp.mean(dhg * xhat, axis=-1, keepdims=True))
        dx_ref[...] = dx
        dxb_ref[...] = dx.astype(BF16)
        dg_ref[...] += jnp.sum(dhv * xhat, axis=0, keepdims=True)

    return pl.pallas_call(
        body, name=name, grid=(S // tr,),
        in_specs=[pl.BlockSpec((tr, D), lambda i: (i, 0)), pl.BlockSpec((tr, D), lambda i: (i, 0)),
                  pl.BlockSpec((tr, D), lambda i: (i, 0)), _vec_spec(l, D)],
        out_specs=[pl.BlockSpec((tr, D), lambda i: (i, 0)), pl.BlockSpec((tr, D), lambda i: (i, 0)),
                   pl.BlockSpec((1, D), lambda i: (0, 0))],
        out_shape=[jax.ShapeDtypeStruct((S, D), F32), jax.ShapeDtypeStruct((S, D), BF16),
                   jax.ShapeDtypeStruct((1, D), F32)],
        compiler_params=_params(("arbitrary",)),
    )(x, dh, dres, g3)


def _glu_conv_bwd(name, dy, ap, cw, cb3, l, cfg):
    S, F = cfg["S"], cfg["F"]
    tc = _tile(F, CONV_COLS, 128)
    nf = F // tc
    R, W = CONV_ROWS, CONV_ROWS + 2 * CONV_HALO

    def body(dy_ref, g_ref, u_ref, wg_ref, wu_ref, bg_ref, bu_ref, dap_ref, dw_ref, db_ref, acc):
        acc[...] = jnp.zeros_like(acc)

        def window(start, lo, out0, edge):
            rows = pl.ds(start, W)
            for c0 in range(0, tc, 128):
                cols = slice(c0, c0 + 128)
                apg, apu, dyv = g_ref[rows, cols], u_ref[rows, cols], dy_ref[rows, cols]
                wg, wu = wg_ref[:, cols], wu_ref[:, cols]
                ag = _conv3(apg, wg, bg_ref[:, cols], edge)
                au = _conv3(apu, wu, bu_ref[:, cols], edge)
                sig = _sigmoid(ag)
                da_u = dyv * (ag * sig)
                da_g = (dyv * au) * (sig * (1.0 + ag * (1.0 - sig)))

                def add(i, prod):
                    acc[i, :, cols] += jnp.sum(prod[lo:lo + R].reshape(R // 8, 8, 128), axis=0)

                for half, (da, t, w) in enumerate([(da_g, apg, wg), (da_u, apu, wu)]):
                    prv, nxt = _shifts(da, edge)
                    dap = (nxt * w[0:1] + da * w[1:2]) + prv * w[2:3]
                    dap_ref[half, pl.ds(out0, R), cols] = dap[lo:lo + R].astype(BF16)
                    add(3 * half, nxt * t)
                    add(3 * half + 1, da * t)
                    add(3 * half + 2, prv * t)
                    add(6 + half, da)

        _for_row_windows(S, window)
        col = [jnp.sum(acc[i], axis=0, keepdims=True) for i in range(8)]
        dw_ref[0] = jnp.concatenate(col[0:3], axis=0)
        dw_ref[1] = jnp.concatenate(col[3:6], axis=0)
        db_ref[0] = col[6]
        db_ref[1] = col[7]

    return pl.pallas_call(
        body, name=name, grid=(nf,),
        in_specs=[pl.BlockSpec((S, tc), lambda j: (0, j)),
                  pl.BlockSpec((S, tc), lambda j: (0, j)),
                  pl.BlockSpec((S, tc), lambda j: (0, j + nf)),
                  pl.BlockSpec((None, 3, tc), lambda j: (l, 0, j)),
                  pl.BlockSpec((None, 3, tc), lambda j: (l, 0, j + nf)),
                  pl.BlockSpec((None, 1, tc), lambda j: (l, 0, j)),
                  pl.BlockSpec((None, 1, tc), lambda j: (l, 0, j + nf))],
        out_specs=[pl.BlockSpec((2, S, tc), lambda j: (0, 0, j)),
                   pl.BlockSpec((2, 3, tc), lambda j: (0, 0, j)),
                   pl.BlockSpec((2, 1, tc), lambda j: (0, 0, j))],
        out_shape=[jax.ShapeDtypeStruct((2, S, F), BF16),
                   jax.ShapeDtypeStruct((2, 3, F), F32),
                   jax.ShapeDtypeStruct((2, 1, F), F32)],
        scratch_shapes=[pltpu.VMEM((8, 8, tc), F32)],
        compiler_params=_params(("arbitrary",)),
    )(dy, ap, ap, cw, cw, cb3, cb3)


def _attn_bwd(name, q, k, v, attn, dmix, sink, ag3, l, cfg):
    S, AW, KVW, NQ, NKV, G = cfg["S"], cfg["AW"], cfg["KVW"], cfg["NQ"], cfg["NKV"], cfg["G"]
    nb = S // HEAD
    scale = HEAD ** -0.5

    def body(q_ref, kp, kc, kn, vp, vc, vn, a_ref, dm_ref, sink_ref, ag_ref,
             dq_ref, dk_ref, dv_ref, dsink_ref, dag_ref, da_scr):
        n = pl.program_id(0)

        @pl.when(n == 0)
        def _():
            dk_ref[...] = jnp.zeros_like(dk_ref)
            dv_ref[...] = jnp.zeros_like(dv_ref)
            dsink_ref[...] = jnp.zeros_like(dsink_ref)
            dag_ref[...] = jnp.zeros_like(dag_ref)

        a = a_ref[...]
        dm = dm_ref[...]
        r = lax.rsqrt(jnp.mean(a * a, axis=-1, keepdims=True) + EPS)
        xhat = a * r
        dmg = dm * ag_ref[...]
        da_scr[...] = r * (dmg - xhat * jnp.mean(dmg * xhat, axis=-1, keepdims=True))
        dag_ref[...] += jnp.sum(dm * xhat, axis=0, keepdims=True)

        kb = jnp.concatenate([kp[...], kc[...], kn[...]], axis=0)
        vb = jnp.concatenate([vp[...], vc[...], vn[...]], axis=0)
        band = pl.ds(pl.multiple_of(n * HEAD, HEAD), 3 * HEAD)
        for j in range(NKV):
            sl = slice(j * HEAD, (j + 1) * HEAD)
            heads = [slice((j * G + g) * HEAD, (j * G + g + 1) * HEAD) for g in range(G)]
            qs = jnp.concatenate([q_ref[:, hs] for hs in heads], axis=0)
            do = jnp.concatenate([da_scr[:, hs] for hs in heads], axis=0)
            kj, vj = kb[:, sl], vb[:, sl]
            probs, p_sink = _attn_probs(qs, kj, n, _sink_rows(sink_ref, l, j, G), S, G)
            dob = do.astype(BF16)
            dprobs = lax.dot_general(dob, vj, (((1,), (1,)), ((), ())), preferred_element_type=F32)
            delta = jnp.sum(dprobs * probs, axis=-1, keepdims=True)
            ds = (probs * (dprobs - delta)) * scale
            dsb = ds.astype(BF16)
            dsk = -(p_sink * delta)
            dq = jnp.dot(dsb, kj, preferred_element_type=F32)
            for g in range(G):
                dq_ref[:, heads[g]] = dq[g * HEAD:(g + 1) * HEAD]
                part = jnp.sum(dsk[g * HEAD:(g + 1) * HEAD], axis=0, keepdims=True)
                dsink_ref[j * G + g:j * G + g + 1, :] += jnp.broadcast_to(part, (1, HEAD))
            dk_ref[band, sl] += lax.dot_general(dsb, qs, (((0,), (0,)), ((), ())), preferred_element_type=F32)
            dv_ref[band, sl] += lax.dot_general(probs.astype(BF16), dob, (((0,), (0,)), ((), ())),
                                                preferred_element_type=F32)

    return pl.pallas_call(
        body, name=name, grid=(nb,),
        in_specs=[pl.BlockSpec((HEAD, AW), lambda n: (n, 0))] + _band_specs(KVW, nb) + _band_specs(KVW, nb)
                 + [pl.BlockSpec((HEAD, AW), lambda n: (n, 0)),
                    pl.BlockSpec((HEAD, AW), lambda n: (n, 0)),
                    pl.BlockSpec(memory_space=pltpu.SMEM), _vec_spec(l, AW)],
        out_specs=[pl.BlockSpec((HEAD, AW), lambda n: (n, 0)),
                   pl.BlockSpec((S + 2 * HEAD, KVW), lambda n: (0, 0)),
                   pl.BlockSpec((S + 2 * HEAD, KVW), lambda n: (0, 0)),
                   pl.BlockSpec((NQ, HEAD), lambda n: (0, 0)),
                   pl.BlockSpec((1, AW), lambda n: (0, 0))],
        out_shape=[jax.ShapeDtypeStruct((S, AW), F32),
                   jax.ShapeDtypeStruct((S + 2 * HEAD, KVW), F32),
                   jax.ShapeDtypeStruct((S + 2 * HEAD, KVW), F32),
                   jax.ShapeDtypeStruct((NQ, HEAD), F32),
                   jax.ShapeDtypeStruct((1, AW), F32)],
        scratch_shapes=[pltpu.VMEM((HEAD, AW), F32)],
        compiler_params=_params(("arbitrary",)),
    )(q, k, k, k, v, v, v, attn, dmix, sink, ag3)


def _qkv_prep_bwd(name, z, dq, dk_pad, dv_pad, qg3, kg3, cos2, sin2, l, cfg):
    S, AW, KVW, NQ, NKV = cfg["S"], cfg["AW"], cfg["KVW"], cfg["NQ"], cfg["NKV"]
    kv_blk = (3 * AW) // (2 * KVW)

    def body(zq_ref, zkv_ref, dq_ref, dk_ref, dv_ref, qg_ref, kg_ref, c_ref, s_ref,
             dzq_ref, dzkv_ref, dqg_ref, dkg_ref):
        @pl.when(pl.program_id(0) == 0)
        def _():
            dqg_ref[...] = jnp.zeros_like(dqg_ref)
            dkg_ref[...] = jnp.zeros_like(dkg_ref)

        cosv, sinv = c_ref[...], s_ref[...]

        def back(t, dr, g):
            r = lax.rsqrt(jnp.mean(t * t, axis=-1, keepdims=True) + EPS)
            xhat = t * r
            dn = dr * cosv + pltpu.roll(dr * sinv, HEAD // 2, axis=1)
            dxh = dn * g
            dt = r * (dxh - xhat * jnp.mean(dxh * xhat, axis=-1, keepdims=True))
            return dt, jnp.sum(dn * xhat, axis=0, keepdims=True)

        gq = jnp.zeros((1, HEAD), F32)
        for h in range(NQ):
            sl = slice(h * HEAD, (h + 1) * HEAD)
            dt, gpart = back(zq_ref[:, sl], dq_ref[:, sl], qg_ref[...])
            dzq_ref[:, sl] = dt.astype(BF16)
            gq = gq + gpart
        dqg_ref[...] += gq
        gk = jnp.zeros((1, HEAD), F32)
        for h in range(NKV):
            sl = slice(h * HEAD, (h + 1) * HEAD)
            dt, gpart = back(zkv_ref[:, sl], dk_ref[:, sl], kg_ref[...])
            dzkv_ref[:, sl] = dt.astype(BF16)
            gk = gk + gpart
        dkg_ref[...] += gk
        dzkv_ref[:, KVW:] = dv_ref[...].astype(BF16)

    return pl.pallas_call(
        body, name=name, grid=(S // HEAD,),
        in_specs=[pl.BlockSpec((HEAD, AW), lambda i: (i, 0)),
                  pl.BlockSpec((HEAD, 2 * KVW), lambda i: (i, kv_blk)),
                  pl.BlockSpec((HEAD, AW), lambda i: (i, 0)),
                  pl.BlockSpec((HEAD, KVW), lambda i: (i + 1, 0)),
                  pl.BlockSpec((HEAD, KVW), lambda i: (i + 1, 0)),
                  _vec_spec(l, HEAD), _vec_spec(l, HEAD),
                  pl.BlockSpec((HEAD, HEAD), lambda i: (i, 0)),
                  pl.BlockSpec((HEAD, HEAD), lambda i: (i, 0))],
        out_specs=[pl.BlockSpec((HEAD, AW), lambda i: (i, 0)),
                   pl.BlockSpec((HEAD, 2 * KVW), lambda i: (i, 0)),
                   pl.BlockSpec((1, HEAD), lambda i: (0, 0)),
                   pl.BlockSpec((1, HEAD), lambda i: (0, 0))],
        out_shape=[jax.ShapeDtypeStruct((S, AW), BF16),
                   jax.ShapeDtypeStruct((S, 2 * KVW), BF16),
                   jax.ShapeDtypeStruct((1, HEAD), F32),
                   jax.ShapeDtypeStruct((1, HEAD), F32)],
        compiler_params=_params(("arbitrary",)),
    )(z, z, dq, dk_pad, dv_pad, qg3, kg3, cos2, sin2)


def _sgu_bwd(name, z, dmix, lng3, lnb3, ws_b, wst_b, bs_b, og3, l, cfg):
    S, GW, NG = cfg["S"], cfg["GW"], cfg["NG"]

    def body(gu_ref, gv_ref, dm_ref, lng_ref, lnb_ref, ws_ref, wst_ref, bs_ref, og_ref,
             dgu_ref, dgv_ref, dws_ref, dbs_ref, dlng_ref, dlnb_ref, dog_ref, sg_scr, f_scr, dvn_scr):
        @pl.when(pl.program_id(0) == 0)
        def _():
            dws_ref[...] = jnp.zeros_like(dws_ref)
            dbs_ref[...] = jnp.zeros_like(dbs_ref)
            dlng_ref[...] = jnp.zeros_like(dlng_ref)
            dlnb_ref[...] = jnp.zeros_like(dlnb_ref)
            dog_ref[...] = jnp.zeros_like(dog_ref)

        gu, gv = gu_ref[...], gv_ref[...]
        lng = lng_ref[...]
        u, xhat, rstd, vn = _sgu_forward_math(gu, gv, lng, lnb_ref[...])
        vnb = vn.astype(BF16)
        for h in range(NG):
            sl = slice(h * HEAD, (h + 1) * HEAD)
            f = jnp.dot(ws_ref[h], vnb[:, sl], preferred_element_type=F32) + bs_ref[h]
            f_scr[:, sl] = f
            sg_scr[:, sl] = u[:, sl] * f
        sg = sg_scr[...]
        dm = dm_ref[...]
        r = lax.rsqrt(jnp.mean(sg * sg, axis=-1, keepdims=True) + EPS)
        sghat = sg * r
        dmg = dm * og_ref[...]
        dsg = r * (dmg - sghat * jnp.mean(dmg * sghat, axis=-1, keepdims=True))
        dog_ref[...] += jnp.sum(dm * sghat, axis=0, keepdims=True)
        du = dsg * f_scr[...]
        df = dsg * u
        dfb = df.astype(BF16)
        for h in range(NG):
            sl = slice(h * HEAD, (h + 1) * HEAD)
            dvn_scr[:, sl] = jnp.dot(wst_ref[h], dfb[:, sl], preferred_element_type=F32)
            dws_ref[h] += lax.dot_general(dfb[:, sl], vnb[:, sl], (((1,), (1,)), ((), ())),
                                          preferred_element_type=F32)
            dbs_ref[h] += jnp.broadcast_to(jnp.sum(df[:, sl], axis=-1, keepdims=True), (HEAD, HEAD))
        dvn = dvn_scr[...]
        dlng_ref[...] += jnp.sum(dvn * xhat, axis=0, keepdims=True)
        dlnb_ref[...] += jnp.sum(dvn, axis=0, keepdims=True)
        dxh = dvn * lng
        dvv = rstd * ((dxh - jnp.mean(dxh, axis=-1, keepdims=True))
                      - xhat * jnp.mean(dxh * xhat, axis=-1, keepdims=True))
        dgu_ref[...] = (du * _gelu_grad(gu)).astype(BF16)
        dgv_ref[...] = (dvv * _gelu_grad(gv)).astype(BF16)

    vec = pl.BlockSpec((1, GW), lambda c: (0, 0))
    mat = pl.BlockSpec((NG, HEAD, HEAD), lambda c: (0, 0, 0))
    wsp = pl.BlockSpec((None, NG, HEAD, HEAD), lambda c: (l, 0, 0, 0))
    return pl.pallas_call(
        body, name=name, grid=(S // HEAD,),
        in_specs=[pl.BlockSpec((HEAD, GW), lambda c: (c, 1)),
                  pl.BlockSpec((HEAD, GW), lambda c: (c, 2)),
                  pl.BlockSpec((HEAD, GW), lambda c: (c, 1)),
                  _vec_spec(l, GW), _vec_spec(l, GW), wsp, wsp, wsp, _vec_spec(l, GW)],
        out_specs=[pl.BlockSpec((HEAD, GW), lambda c: (c, 0)), pl.BlockSpec((HEAD, GW), lambda c: (c, 0)),
                   mat, mat, vec, vec, vec],
        out_shape=[jax.ShapeDtypeStruct((S, GW), BF16), jax.ShapeDtypeStruct((S, GW), BF16),
                   jax.ShapeDtypeStruct((NG, HEAD, HEAD), F32), jax.ShapeDtypeStruct((NG, HEAD, HEAD), F32),
                   jax.ShapeDtypeStruct((1, GW), F32), jax.ShapeDtypeStruct((1, GW), F32),
                   jax.ShapeDtypeStruct((1, GW), F32)],
        scratch_shapes=[pltpu.VMEM((HEAD, GW), F32), pltpu.VMEM((HEAD, GW), F32), pltpu.VMEM((HEAD, GW), F32)],
        compiler_params=_params(("arbitrary",)),
    )(z, z, dmix, lng3, lnb3, ws_b, wst_b, bs_b, og3)


def _mesh_pos():
    x, y, c = lax.axis_index("x"), lax.axis_index("y"), lax.axis_index("c")
    return x, y, c


def _dev_index(p):
    return 4 * p[0] + 2 * p[1] + p[2]


def _handshake(peers):
    barrier = pltpu.get_barrier_semaphore()
    for p in peers:
        pl.semaphore_signal(barrier, inc=1, device_id=p, device_id_type=MESH)
    pl.semaphore_wait(barrier, len(peers))


GATHER_COPIES = 10


def _gather_body(full, half, in_half):
    def body(in_ref, out_ref, send_sems, recv_sems, local_sem):
        x, y, c = _mesh_pos()
        me, sib = (x, y, c), (x, y, 1 - c)
        xn, yn, dg = (1 - x, y, c), (x, 1 - y, c), (1 - x, 1 - y, c)
        _handshake([sib, xn, yn])

        def copy(k, dst, to, src=None):
            return pltpu.make_async_remote_copy(
                src_ref=dst if src is None else src, dst_ref=dst,
                send_sem=send_sems.at[k], recv_sem=recv_sems.at[k], device_id=to, device_id_type=MESH)

        def win(dev, h=None):
            idx = _dev_index(dev)
            return full(out_ref, idx) if h is None else half(out_ref, idx, h)

        local = pltpu.make_async_copy(in_ref, win(me), local_sem)
        local.start()
        sends = [copy(0, win(me), sib, src=in_ref),
                 copy(1, win(me, 0), xn, src=in_half(in_ref, 0)),
                 copy(2, win(me, 1), yn, src=in_half(in_ref, 1)),
                 copy(4, win(me, 0), yn, src=in_half(in_ref, 0)),
                 copy(6, win(me, 1), xn, src=in_half(in_ref, 1))]
        for cp in sends:
            cp.start()
        copy(1, win(xn, 0), me).wait_recv()
        sends.append(copy(3, win(xn, 0), yn))
        sends[-1].start()
        copy(2, win(yn, 1), me).wait_recv()
        sends.append(copy(5, win(yn, 1), xn))
        sends[-1].start()
        copy(6, win(xn, 1), me).wait_recv()
        sends.append(copy(7, win(xn), sib))
        sends[-1].start()
        copy(4, win(yn, 0), me).wait_recv()
        sends.append(copy(8, win(yn), sib))
        sends[-1].start()
        copy(3, win(dg, 0), me).wait_recv()
        copy(5, win(dg, 1), me).wait_recv()
        sends.append(copy(9, win(dg), sib))
        sends[-1].start()
        sib_xn, sib_yn, sib_dg = (1 - x, y, 1 - c), (x, 1 - y, 1 - c), (1 - x, 1 - y, 1 - c)
        for k, dev in [(0, sib), (7, sib_xn), (8, sib_yn), (9, sib_dg)]:
            copy(k, win(dev), me).wait_recv()
        for cp in sends:
            cp.wait_send()
        local.wait()

    return body


def _all_gather(name, cid, shard, out_shape, windows):
    return pl.kernel(
        _gather_body(*windows), out_type=out_shape,
        mesh=plsc.ScalarSubcoreMesh(axis_name="seq", num_cores=1), name=name,
        scratch_types=[pltpu.SemaphoreType.DMA((GATHER_COPIES,)), pltpu.SemaphoreType.DMA((GATHER_COPIES,)),
                       pltpu.SemaphoreType.DMA],
        compiler_params=pltpu.CompilerParams(collective_id=cid),
    )(shard)


def _d2d_body(nb, ns):
    n = nb + ns

    def body(*refs):
        ins, outs = refs[:n], refs[n:2 * n]
        send_sems, recv_sems = refs[2 * n:]
        x, y, c = _mesh_pos()
        sib = (x, y, 1 - c)
        _handshake([sib])
        copies = []
        for t in range(n):
            cp = pltpu.make_async_remote_copy(
                src_ref=ins[t].at[:, 1 - c] if t < nb else ins[t], dst_ref=outs[t],
                send_sem=send_sems.at[t], recv_sem=recv_sems.at[t],
                device_id=sib, device_id_type=MESH)
            cp.start()
            copies.append(cp)
        for cp in copies:
            cp.wait()

    return body


def _rs_d2d(name, cid, bigs, smalls):
    shapes = [jax.ShapeDtypeStruct((g.shape[0],) + g.shape[2:], g.dtype) for g in bigs]
    shapes += [jax.ShapeDtypeStruct(s.shape, s.dtype) for s in smalls]
    n = len(shapes)
    return pl.kernel(
        _d2d_body(len(bigs), len(smalls)), out_type=shapes,
        mesh=plsc.ScalarSubcoreMesh(axis_name="seq", num_cores=1), name=name,
        scratch_types=[pltpu.SemaphoreType.DMA((n,)), pltpu.SemaphoreType.DMA((n,))],
        compiler_params=pltpu.CompilerParams(collective_id=cid),
    )(*bigs, *smalls)


def _pair_sum(name, g4, recv, cvec, row_mult):
    Q, _, R, C = g4.shape
    tr = _tile(R, 1024, row_mult)

    def body(c_ref, a_ref, b_ref, o_ref):
        o_ref[...] = (a_ref[...].astype(F32) + b_ref[...].astype(F32)).astype(BF16)

    grid_spec = pltpu.PrefetchScalarGridSpec(
        num_scalar_prefetch=1, grid=(Q, R // tr),
        in_specs=[pl.BlockSpec((None, None, tr, C), lambda q, i, c_ref: (q, c_ref[0], i, 0)),
                  pl.BlockSpec((None, tr, C), lambda q, i, c_ref: (q, i, 0))],
        out_specs=pl.BlockSpec((None, tr, C), lambda q, i, c_ref: (q, i, 0)))
    return pl.pallas_call(
        body, name=name, grid_spec=grid_spec, out_shape=jax.ShapeDtypeStruct((Q, R, C), BF16),
        compiler_params=_params(("arbitrary", "arbitrary")),
    )(cvec, g4, recv)


def _add2(name, a, b):
    R, C = a.shape
    tr = _tile(R, 512, 8)

    def body(a_ref, b_ref, o_ref):
        o_ref[...] = a_ref[...] + b_ref[...]

    blk = pl.BlockSpec((tr, C), lambda i: (i, 0))
    return pl.pallas_call(body, name=name, grid=(R // tr,), in_specs=[blk, blk], out_specs=blk,
                          out_shape=jax.ShapeDtypeStruct((R, C), a.dtype),
                          compiler_params=_params(("arbitrary",)))(a, b)


def _ici_copies(nb, n, srcs, lands, send_sems, recv_sems):
    x, y, c = _mesh_pos()
    q_me = 2 * x + y
    copies = []
    for t in range(n):
        for k in range(1, 4):
            px, py = x ^ (k >> 1), y ^ (k & 1)
            copies.append(pltpu.make_async_remote_copy(
                src_ref=srcs[t].at[2 * px + py] if t < nb else srcs[t], dst_ref=lands[t].at[q_me],
                send_sem=send_sems.at[3 * t + k - 1], recv_sem=recv_sems.at[3 * t + k - 1],
                device_id=(px, py, c), device_id_type=MESH))
    return copies


_HBM = pl.BlockSpec(memory_space=pltpu.HBM)
_SEM = pl.BlockSpec(memory_space=pltpu.SEMAPHORE)
_DATAFLOW = pltpu.SideEffectType.DATAFLOW_SIDE_EFFECTING


def _ici_start(name, bigs, smalls):
    nb, n = len(bigs), len(bigs) + len(smalls)
    srcs = list(bigs) + list(smalls)
    lands = [lax.empty(g.shape, g.dtype) for g in bigs] + [lax.empty((N_DEV // 2,) + s.shape, s.dtype) for s in smalls]

    def body(*refs):
        src_refs, land_refs = refs[:n], refs[n:2 * n]
        send_sems, recv_sems = refs[2 * n], refs[2 * n + 1]
        token = refs[-1]
        for cp in _ici_copies(nb, n, src_refs, land_refs, send_sems, recv_sems):
            cp.start()
        token[...] = jnp.zeros_like(token)

    hbm = [pltpu.HBM(a.shape, a.dtype) for a in srcs + lands]
    args = [pltpu.with_memory_space_constraint(a, pltpu.HBM) for a in srcs + lands]
    out = pl.pallas_call(
        body, name=name,
        out_shape=[pltpu.SemaphoreType.DMA((3 * n,)), pltpu.SemaphoreType.DMA((3 * n,))] + hbm
                  + [jax.ShapeDtypeStruct((8, 128), F32)],
        in_specs=[_HBM] * (2 * n),
        out_specs=[_SEM, _SEM] + [_HBM] * (2 * n) + [pl.BlockSpec(memory_space=pltpu.VMEM)],
        input_output_aliases={i: 2 + i for i in range(2 * n)},
        compiler_params=pltpu.CompilerParams(has_side_effects=_DATAFLOW),
    )(*args)
    return out[0], out[1], list(out[2:2 + n]), list(out[2 + n:2 + 2 * n]), out[-1]


def _ici_wait(name, started, nb, not_before):
    send_sems, recv_sems, srcs, lands, token = started
    n = len(srcs)

    def body(*refs):
        src_refs, land_refs = refs[:n], refs[n:2 * n]
        send_refs, recv_refs = refs[2 * n], refs[2 * n + 1]
        for cp in _ici_copies(nb, n, src_refs, land_refs, send_refs, recv_refs):
            cp.wait_send()
            cp.wait_recv()

    hbm = [pltpu.HBM(a.shape, a.dtype) for a in srcs + lands]
    out = pl.pallas_call(
        body, name=name, out_shape=hbm,
        in_specs=[_HBM] * (2 * n) + [_SEM, _SEM] + [pl.BlockSpec(memory_space=pl.ANY)] * 2,
        out_specs=[_HBM] * (2 * n),
        input_output_aliases={i: i for i in range(2 * n)},
        compiler_params=pltpu.CompilerParams(has_side_effects=_DATAFLOW),
    )(*srcs, *lands, send_sems, recv_sems, not_before, token)
    return list(out[:n]), list(out[n:])


def _adamw_math(w, g, m, v):
    m2 = ADAM_B1 * m + (1.0 - ADAM_B1) * g
    v2 = ADAM_B2 * v + (1.0 - ADAM_B2) * (g * g)
    m_hat = m2 / (1.0 - ADAM_B1 ** ADAM_STEP)
    v_hat = v2 / (1.0 - ADAM_B2 ** ADAM_STEP)
    delta = -ADAM_LR * (m_hat / (jnp.sqrt(v_hat) + ADAM_EPS) + ADAM_WD * w)
    return delta, m2, v2


def _chip_terms(q, own, land_ref, n):
    return [jnp.where(q == s, own, land_ref[s]).astype(F32) for s in range(n)]


def _adamw_sum(name, land, own, qvec, w, m, v, l, prev, row_mult):
    NS, R, C = land.shape
    L = w.shape[0]
    tr = _tile(R, 256, row_mult)

    def body(*refs):
        q_ref, r_ref, o_ref, w_ref, m_ref, v_ref = refs[:6]
        g_ref, d_ref, nm_ref, nv_ref = refs[-4:]
        terms = _chip_terms(q_ref[0], o_ref[...], r_ref, NS)
        g = terms[0]
        for t in terms[1:]:
            g = g + t
        d, m2, v2 = _adamw_math(w_ref[...], g, m_ref[...], v_ref[...])
        g_ref[...] = g
        d_ref[...] = d
        nm_ref[...] = m2
        nv_ref[...] = v2

    blk = pl.BlockSpec((None, tr, C), lambda i, q: (l, i, 0))
    shp = jax.ShapeDtypeStruct((L, R, C), F32)
    in_specs = [pl.BlockSpec((NS, tr, C), lambda i, q: (0, i, 0)),
                pl.BlockSpec((None, tr, C), lambda i, q: (q[0], i, 0)), blk, blk, blk]
    args = [land, own, w, m, v]
    aliases = {}
    if prev is not None:
        in_specs += [pl.BlockSpec(memory_space=pl.ANY)] * 4
        args += list(prev)
        aliases = {6 + i: i for i in range(4)}
    grid_spec = pltpu.PrefetchScalarGridSpec(num_scalar_prefetch=1, grid=(R // tr,), in_specs=in_specs,
                                             out_specs=[blk, blk, blk, blk])
    return pl.pallas_call(
        body, name=name, grid_spec=grid_spec, out_shape=[shp, shp, shp, shp],
        input_output_aliases=aliases,
        compiler_params=_params(("arbitrary",)),
    )(qvec, *args)


def _adamw_plain(name, g, w, m, v):
    def body(g_ref, w_ref, m_ref, v_ref, d_ref, nm_ref, nv_ref):
        d, m2, v2 = _adamw_math(w_ref[...], g_ref[...], m_ref[...], v_ref[...])
        d_ref[...] = d
        nm_ref[...] = m2
        nv_ref[...] = v2

    shp = jax.ShapeDtypeStruct(g.shape, F32)
    return pl.pallas_call(body, name=name, out_shape=[shp, shp, shp], compiler_params=_params())(g, w, m, v)


SMALL = ["norm1_g", "q_norm_g", "k_norm_g", "sink", "sgu_ln_g", "sgu_ln_b", "w_s", "b_s",
         "attn_out_g", "sgu_out_g", "norm2_g", "conv_b"]
PACK_ALIGN = 1024


def _pack(pieces):
    flat = []
    for p in pieces:
        f = p.reshape(-1).astype(F32)
        pad = (-f.shape[0]) % PACK_ALIGN
        flat.append(jnp.pad(f, (0, pad)) if pad else f)
    return jnp.concatenate(flat).reshape(-1, 128)


def kernel(x, norm1_g, w_in, q_norm_g, k_norm_g, sink, sgu_ln_g, sgu_ln_b, w_s, b_s, attn_out_g, sgu_out_g, w_o, norm2_g, w_up, conv_w, conv_b, w_down, loss_target, m_norm1_g, m_w_in, m_q_norm_g, m_k_norm_g, m_sink, m_sgu_ln_g, m_sgu_ln_b, m_w_s, m_b_s, m_attn_out_g, m_sgu_out_g, m_w_o, m_norm2_g, m_w_up, m_conv_w, m_conv_b, m_w_down, v_norm1_g, v_w_in, v_q_norm_g, v_k_norm_g, v_sink, v_sgu_ln_g, v_sgu_ln_b, v_w_s, v_b_s, v_attn_out_g, v_sgu_out_g, v_w_o, v_norm2_g, v_w_up, v_conv_w, v_conv_b, v_w_down):
    weights = dict(norm1_g=norm1_g, w_in=w_in, q_norm_g=q_norm_g, k_norm_g=k_norm_g, sink=sink, sgu_ln_g=sgu_ln_g,
                   sgu_ln_b=sgu_ln_b, w_s=w_s, b_s=b_s, attn_out_g=attn_out_g, sgu_out_g=sgu_out_g, w_o=w_o,
                   norm2_g=norm2_g, w_up=w_up, conv_w=conv_w, conv_b=conv_b, w_down=w_down)
    mom_m = dict(norm1_g=m_norm1_g, w_in=m_w_in, q_norm_g=m_q_norm_g, k_norm_g=m_k_norm_g, sink=m_sink,
                 sgu_ln_g=m_sgu_ln_g, sgu_ln_b=m_sgu_ln_b, w_s=m_w_s, b_s=m_b_s, attn_out_g=m_attn_out_g,
                 sgu_out_g=m_sgu_out_g, w_o=m_w_o, norm2_g=m_norm2_g, w_up=m_w_up, conv_w=m_conv_w,
                 conv_b=m_conv_b, w_down=m_w_down)
    mom_v = dict(norm1_g=v_norm1_g, w_in=v_w_in, q_norm_g=v_q_norm_g, k_norm_g=v_k_norm_g, sink=v_sink,
                 sgu_ln_g=v_sgu_ln_g, sgu_ln_b=v_sgu_ln_b, w_s=v_w_s, b_s=v_b_s, attn_out_g=v_attn_out_g,
                 sgu_out_g=v_sgu_out_g, w_o=v_w_o, norm2_g=v_norm2_g, w_up=v_w_up, conv_w=v_conv_w,
                 conv_b=v_conv_b, w_down=v_w_down)
    order = ["norm1_g", "w_in", "q_norm_g", "k_norm_g", "sink", "sgu_ln_g", "sgu_ln_b", "w_s", "b_s",
             "attn_out_g", "sgu_out_g", "w_o", "norm2_g", "w_up", "conv_w", "conv_b", "w_down"]

    _, S, D = x.shape
    L = w_in.shape[0]
    AW = D // 2
    NQ = AW // HEAD
    NKV = max(1, NQ // 4)
    G = NQ // NKV
    KVW = NKV * HEAD
    GW = D - AW
    NG = GW // HEAD
    IN = AW + 2 * KVW + 2 * GW
    INS = w_in.shape[2]
    OS = w_o.shape[1]
    US = w_up.shape[2]
    DS = w_down.shape[1]
    F2 = US * N_DEV
    F = F2 // 2
    assert INS * N_DEV == IN and OS * N_DEV == D and DS * N_DEV == F and AW == GW and (3 * AW) % (2 * KVW) == 0
    cfg = dict(S=S, D=D, AW=AW, NQ=NQ, NKV=NKV, G=G, KVW=KVW, GW=GW, NG=NG, F=F, F2=F2)

    def row_windows(n):
        m = n // 2
        return (lambda ref, idx: ref.at[:, pl.ds(pl.multiple_of(idx * n, n), n), :],
                lambda ref, idx, h: ref.at[:, pl.ds(pl.multiple_of(idx * n + h * m, m), m), :],
                lambda ref, h: ref.at[:, pl.ds(h * m, m), :])

    def col_windows(n, rows):
        m = rows // 2
        return (lambda ref, idx: ref.at[:, :, pl.ds(pl.multiple_of(idx * n, n), n)],
                lambda ref, idx, h: ref.at[:, pl.ds(h * m, m), pl.ds(pl.multiple_of(idx * n, n), n)],
                lambda ref, h: ref.at[:, pl.ds(h * m, m), :])

    def lead_col_windows(n, lead):
        m = lead // 2
        return (lambda ref, idx: ref.at[:, :, pl.ds(pl.multiple_of(idx * n, n), n)],
                lambda ref, idx, h: ref.at[pl.ds(h * m, m), :, pl.ds(pl.multiple_of(idx * n, n), n)],
                lambda ref, h: ref.at[pl.ds(h * m, m)])

    def after(first, then):
        return lax.optimization_barrier((first, then))

    w_in_t, m_w_in_t, v_w_in_t = (jnp.swapaxes(a, 1, 2) for a in (w_in, m_w_in, v_w_in))

    ids = iter(range(64))
    w_in_f, w_o_f, w_up_f, w_down_f = [], [], [], []

    def gather_layer(l, not_before=None):
        in_shard = w_in_t[l:l + 1].astype(BF16)
        if not_before is not None:
            _, in_shard = after(not_before, in_shard)
        w_in_f.append(_all_gather("ag_in", next(ids), in_shard,
                                  jax.ShapeDtypeStruct((1, IN, D), BF16), row_windows(INS)))
        w_o_f.append(_all_gather("ag_o", next(ids), w_o[l:l + 1].astype(BF16),
                                 jax.ShapeDtypeStruct((1, D, D), BF16), row_windows(OS)))
        w_up_f.append(_all_gather("ag_up", next(ids), w_up[l:l + 1].astype(BF16),
                                  jax.ShapeDtypeStruct((1, D, F2), BF16), col_windows(US, D)))
        w_down_f.append(_all_gather("ag_down", next(ids), w_down[l:l + 1].astype(BF16),
                                    jax.ShapeDtypeStruct((1, F, D), BF16), row_windows(DS)))

    assert L % 2 == 0 and INS % 32 == 0 and OS % 32 == 0 and DS % 32 == 0 and D % 32 == 0
    gather_layer(0)
    conv_w_f = _all_gather("ag_conv_w", next(ids), conv_w, jax.ShapeDtypeStruct((L, 3, F2), F32),
                           lead_col_windows(US, L))

    n1g3, n2g3 = norm1_g.reshape(L, 1, D), norm2_g.reshape(L, 1, D)
    qg3, kg3 = q_norm_g.reshape(L, 1, HEAD), k_norm_g.reshape(L, 1, HEAD)
    lng3, lnb3 = sgu_ln_g.reshape(L, 1, GW), sgu_ln_b.reshape(L, 1, GW)
    ag3, og3 = attn_out_g.reshape(L, 1, AW), sgu_out_g.reshape(L, 1, GW)
    cb3 = conv_b.reshape(L, 1, F2)
    ws_b = w_s.astype(BF16)
    wst_b = jnp.swapaxes(w_s, 2, 3).astype(BF16)
    bs_b = jnp.broadcast_to(b_s[..., None], (L, NG, HEAD, HEAD))
    inv_freq = ROPE_THETA ** (-jnp.arange(0, HEAD, 2, dtype=F32) / HEAD)
    ang = jnp.arange(S, dtype=F32)[:, None] * inv_freq[None, :]
    cos2 = jnp.concatenate([jnp.cos(ang), jnp.cos(ang)], axis=1)
    sin2 = jnp.concatenate([-jnp.sin(ang), jnp.sin(ang)], axis=1)

    tn = 512
    t_in, t_d, t_f, t_f2 = _tile(IN, tn, 128), _tile(D, tn, 128), _tile(F, tn, 128), _tile(F2, tn, 128)
    tm_f = _tile(F, 512, 128)
    tm_s = _tile(S, 1024, 128)
    tm_g = _tile(D, 512, 128)
    assert AW % t_in == 0 and (2 * KVW) % t_in == 0
    n_q, n_kv, n_g = AW // t_in, (2 * KVW) // t_in, (2 * GW) // t_in

    def in_tile(j):
        return jnp.where(j < n_q, j, jnp.where(j < n_q + n_g, j + n_kv, j - n_g))

    def w_spec(tk, tn_):
        return pl.BlockSpec((None, tk, tn_), lambda i, j, k: (0, k, j))

    def wt_spec(tn_, tk):
        return pl.BlockSpec((None, tn_, tk), lambda i, j, k: (0, j, k))

    def a_spec(tm, tk):
        return pl.BlockSpec((tm, tk), lambda i, j, k: (i, k))

    def at_spec(tk, tm):
        return pl.BlockSpec((tk, tm), lambda i, j, k: (k, i))

    def b_spec(tk, tn_):
        return pl.BlockSpec((tk, tn_), lambda i, j, k: (k, j))

    xs = x.reshape(S, D)
    saved = []
    cur = xs
    for l in range(L):
        h = _rms_fwd("rms1_fwd", cur, n1g3, l)
        z = _mm("mm_in", h, w_in_f[l], M=S, N=IN, K=D, tm=S, tn=t_in, tk=D, tb=True,
                a_spec=a_spec(S, D),
                b_spec=pl.BlockSpec((None, t_in, D), lambda i, j, k: (0, in_tile(j), 0)))
        q_r, k_r, v_b = _qkv_prep("qkv_prep", z, qg3, kg3, cos2, sin2, l, cfg)
        attn, mix_l = _attn_fwd("attn_fwd", q_r, k_r, v_b, sink, ag3, l, cfg)
        mix_r = _sgu_fwd("sgu_fwd", z, lng3, lnb3, ws_b, bs_b, og3, l, cfg)
        mixed = jnp.concatenate([mix_l, mix_r], axis=1)
        x1 = _mm("mm_o", mixed, w_o_f[l], M=S, N=D, K=D, tm=S, tn=t_d, tk=D,
                 a_spec=a_spec(S, D), b_spec=w_spec(D, t_d), res=cur)
        if l + 1 < L:
            gather_layer(l + 1, not_before=x1)
        h2 = _rms_fwd("rms2_fwd", x1, n2g3, l)
        ap = _mm("mm_up", h2, w_up_f[l], M=S, N=F2, K=D, tm=S, tn=t_f2, tk=D,
                 a_spec=a_spec(S, D), b_spec=w_spec(D, t_f2))
        y_b = _conv_glu_fwd("conv_glu_fwd", ap, conv_w_f, cb3, l, cfg)
        saved.append(dict(x=cur, h=h, z=z, q=q_r, k=k_r, v=v_b, attn=attn, mixed=mixed, x1=x1, h2=h2, ap=ap, y=y_b))
        if l + 1 < L:
            cur = _mm("mm_down", y_b, w_down_f[l], M=S, N=D, K=F, tm=tm_s, tn=t_d, tk=F,
                      a_spec=a_spec(tm_s, F), b_spec=w_spec(F, t_d), res=x1)
        else:
            loss_tile, dx, dxb = _mm_loss("mm_down_loss", y_b, w_down_f[l], x1, loss_target.reshape(S, D),
                                          M=S, N=D, K=F, tm=tm_s, tn=t_d,
                                          b_spec=pl.BlockSpec((None, F, t_d), lambda i, j: (0, 0, j)))

    gS = [dict() for _ in range(L)]
    recv = [dict() for _ in range(L)]
    cvec = jnp.reshape(lax.axis_index("c"), (1,)).astype(jnp.int32)

    def rs_swap(name, g, R, C, small=None):
        g4 = g.reshape(N_DEV // 2, 2, R, C)
        got = _rs_d2d("rs_d2d_" + name, next(ids), [g4], [] if small is None else [small])
        return name, g4, got, small

    started = []

    def rs_finish(l, pending, then=None):
        name, g4, got, small = pending
        chip = _pair_sum("pair_sum_" + name, g4, got[0], cvec, 16)
        chip_small = [] if small is None else [_add2("pair_sum_small", small, got[1])]
        begun = _ici_start("rs_ici_" + name, [chip], chip_small)
        if then is not None:
            token, then = after(begun[4], then)
            begun = begun[:4] + (token,)
        started.append((l, name, begun))
        return then

    carried = None
    for l in reversed(range(L)):
        sv = saved[l]
        dy = _mm("mm_dy", dxb, w_down_f[l], M=S, N=F, K=D, tm=S, tn=t_f, tk=D, tb=True,
                 a_spec=a_spec(S, D), b_spec=wt_spec(t_f, D))
        if carried is not None:
            dy = rs_finish(l + 1, carried, dy)
        g_down = _mm("mm_gdown", sv["y"], dxb, M=F, N=D, K=S, tm=tm_f, tn=D, tk=S, ta=True,
                     a_spec=at_spec(S, tm_f), b_spec=b_spec(S, D), out_dtype=BF16)
        g_down, dy = after(g_down, dy)
        swap_down = rs_swap("w_down", g_down, DS, D)
        dap3, dcw, dcb = _glu_conv_bwd("glu_conv_bwd", dy, sv["ap"], conv_w_f, cb3, l, cfg)
        gS[l]["conv_w"] = jnp.concatenate([dcw[0], dcw[1]], axis=1)
        gS[l]["conv_b"] = jnp.concatenate([dcb[0], dcb[1]], axis=1).reshape(F2)
        dap3 = rs_finish(l, swap_down, dap3)
        dh2 = _mm("mm_dh2", dap3, w_up_f[l], M=S, N=D, K=F2, tm=tm_s, tn=t_d, tk=F, tb=True,
                  a_spec=pl.BlockSpec((None, tm_s, F), lambda i, j, k: (k, i, 0)),
                  b_spec=wt_spec(t_d, F))
        g_up = _mm("mm_gup", sv["h2"], dap3, M=D, N=F2, K=S, tm=tm_g, tn=2 * US, tk=S, ta=True,
                   a_spec=at_spec(S, tm_g),
                   b_spec=pl.BlockSpec((None, S, 2 * US), lambda i, j, k: (j // 2, 0, j % 2)),
                   out_dtype=BF16, out_shape=(N_DEV, D, US),
                   out_spec=pl.BlockSpec((2, tm_g, US), lambda i, j, k: (j, i, 0)))
        g_up, dh2 = after(g_up, dh2)
        swap_up = rs_swap("w_up", g_up, D, US)
        dx1, dx1b, dg2 = _rms_bwd("rms2_bwd", sv["x1"], dh2, dx, n2g3, l)
        gS[l]["norm2_g"] = dg2.reshape(D)
        dmix = _mm("mm_dmix", dx1b, w_o_f[l], M=S, N=D, K=D, tm=S, tn=t_d, tk=D, tb=True,
                   a_spec=a_spec(S, D), b_spec=wt_spec(t_d, D))
        g_o = _mm("mm_go", sv["mixed"], dx1b, M=D, N=D, K=S, tm=D, tn=t_d, tk=S, ta=True,
                  a_spec=at_spec(S, D), b_spec=b_spec(S, t_d), out_dtype=BF16)
        g_o, dmix = after(g_o, dmix)
        swap_o = rs_swap("w_o", g_o, OS, D)
        dmix = rs_finish(l, swap_up, dmix)
        dq_r, dk_pad, dv_pad, dsink, dag = _attn_bwd("attn_bwd", sv["q"], sv["k"], sv["v"], sv["attn"], dmix,
                                                     sink, ag3, l, cfg)
        dq_r = rs_finish(l, swap_o, dq_r)
        gS[l]["sink"] = dsink[:, 0]
        gS[l]["attn_out_g"] = dag.reshape(AW)
        dzgu, dzgv, dws, dbs, dlng, dlnb, dog = _sgu_bwd("sgu_bwd", sv["z"], dmix, lng3, lnb3, ws_b, wst_b, bs_b,
                                                        og3, l, cfg)
        gS[l]["w_s"] = dws
        gS[l]["b_s"] = dbs[:, :, 0]
        gS[l]["sgu_ln_g"] = dlng.reshape(GW)
        gS[l]["sgu_ln_b"] = dlnb.reshape(GW)
        gS[l]["sgu_out_g"] = dog.reshape(GW)
        dzq, dzkv, dqg, dkg = _qkv_prep_bwd("qkv_prep_bwd", sv["z"], dq_r, dk_pad, dv_pad, qg3, kg3, cos2, sin2,
                                            l, cfg)
        gS[l]["q_norm_g"] = dqg.reshape(HEAD)
        gS[l]["k_norm_g"] = dkg.reshape(HEAD)
        dz = jnp.concatenate([dzq, dzkv, dzgu, dzgv], axis=1)
        dh = _mm("mm_dh", dz, w_in_f[l], M=S, N=D, K=IN, tm=tm_s, tn=t_d, tk=IN,
                 a_spec=a_spec(tm_s, IN), b_spec=w_spec(IN, t_d))
        g_in = _mm("mm_gin", dz, sv["h"], M=IN, N=D, K=S, tm=t_in, tn=D, tk=S, ta=True,
                   a_spec=at_spec(S, t_in), b_spec=b_spec(S, D), out_dtype=BF16)
        g_in, dh = after(g_in, dh)
        dx, dxb, dg1 = _rms_bwd("rms1_bwd", sv["x"], dh, dx1, n1g3, l)
        gS[l]["norm1_g"] = dg1.reshape(D)
        carried = rs_swap("w_in", g_in, INS, D, small=_pack([gS[l][n] for n in SMALL] + [gS[l]["conv_w"]]))
    dx = rs_finish(0, carried, dx)
    grad_x = dx.reshape(1, S, D)

    grads, deltas, new_m, new_v = {}, {}, {}, {}
    results, placed = {}, dx
    qvec = jnp.reshape(2 * lax.axis_index("x") + lax.axis_index("y"), (1,)).astype(jnp.int32)
    for l, name, begun in started:
        sent, landed = _ici_wait("rs_wait_" + name, begun, 1, placed)
        recv[l][name] = (sent, landed)
        wmv = (w_in_t, m_w_in_t, v_w_in_t) if name == "w_in" else (weights[name], mom_m[name], mom_v[name])
        results[name] = _adamw_sum("adamw_" + name, landed[0], sent[0], qvec, *wmv, l, results.get(name), 16)
        placed = results[name][0]
    _, loss_local = after(placed, loss_tile[0, 0])
    loss = lax.psum(loss_local, ("x", "y", "c"))
    results["w_in"] = [jnp.swapaxes(r, 1, 2) for r in results["w_in"]]
    for name, res in results.items():
        grads[name], deltas[name], new_m[name], new_v[name] = res

    def as_rows(a):
        flat = a.reshape(L, -1)
        short = (-flat.shape[1]) % 128
        if short:
            flat = jnp.pad(flat, ((0, 0), (0, short)))
        return flat.reshape(L, -1, 128)

    sizes = [weights[n][0].size for n in SMALL] + [3 * F2]
    offsets, off = [], 0
    for sz in sizes:
        offsets.append(off // 128)
        off += sz + (-sz) % PACK_ALIGN
    n_small = len(SMALL)
    rows_of = [-(-sz // 128) for sz in sizes]

    def small_body(*refs):
        q_ref = refs[0]
        lands, owns = refs[1:1 + L], refs[1 + L:1 + 2 * L]
        wmv_refs = refs[1 + 2 * L:1 + 2 * L + 3 * n_small]
        outs = refs[1 + 2 * L + 3 * n_small:]
        for l in range(L):
            for p in range(n_small + 1):
                rows = slice(offsets[p], offsets[p] + rows_of[p])
                terms = [jnp.where(q_ref[0] == s, owns[l][rows, :], lands[l][s, rows, :]) for s in range(N_DEV // 2)]
                g = terms[0]
                for t in terms[1:]:
                    g = g + t
                if p == n_small:
                    outs[4 * n_small][l] = g
                    continue
                d, m2, v2 = _adamw_math(wmv_refs[3 * p][l], g, wmv_refs[3 * p + 1][l], wmv_refs[3 * p + 2][l])
                for t, val in enumerate([g, d, m2, v2]):
                    outs[4 * p + t][l] = val

    vmem = pl.BlockSpec(memory_space=pltpu.VMEM)
    wmv_args = []
    for n in SMALL:
        wmv_args += [as_rows(weights[n]), as_rows(mom_m[n]), as_rows(mom_v[n])]
    small_out = pl.pallas_call(
        small_body, name="adamw_small",
        in_specs=[pl.BlockSpec(memory_space=pltpu.SMEM)] + [vmem] * (2 * L + 3 * n_small),
        out_specs=[vmem] * (4 * n_small + 1),
        out_shape=[jax.ShapeDtypeStruct((L, rows_of[p], 128), F32) for p in range(n_small) for _ in range(4)]
                  + [jax.ShapeDtypeStruct((L, rows_of[n_small], 128), F32)],
        compiler_params=_params(),
    )(qvec, *[recv[l]["w_in"][1][1] for l in range(L)], *[recv[l]["w_in"][0][1] for l in range(L)], *wmv_args)
    for p, n in enumerate(SMALL):
        for t, store in enumerate([grads, deltas, new_m, new_v]):
            store[n] = small_out[4 * p + t].reshape(L, -1)[:, :sizes[p]].reshape(weights[n].shape)
    conv_full = small_out[4 * n_small].reshape(L, 3, F2)
    me_i = _dev_index(_mesh_pos())
    g_cw = lax.dynamic_slice_in_dim(conv_full, me_i * US, US, axis=2)
    grads["conv_w"] = g_cw
    d_cw, m_cw, v_cw = _adamw_plain("adamw_conv_w", g_cw.reshape(L * 3, US), conv_w.reshape(L * 3, US),
                                    m_conv_w.reshape(L * 3, US), v_conv_w.reshape(L * 3, US))
    deltas["conv_w"] = d_cw.reshape(L, 3, US)
    new_m["conv_w"] = m_cw.reshape(L, 3, US)
    new_v["conv_w"] = v_cw.reshape(L, 3, US)

    return (loss, grad_x, *[grads[n] for n in order], *[deltas[n] for n in order],
            *[new_m[n] for n in order], *[new_v[n] for n in order])
```
